```python
import math
import jax, jax.numpy as jnp
from jax import lax
import numpy as np

D_MODEL = 1024
BATCH = 2
SEQ = 16384
DEPTH = 1

HEAD_DIM = 64
DIFF_HEADS = 4
DIFF_V_DIM = 2 * HEAD_DIM
SWA_Q_HEADS = 8
SWA_KV_HEADS = 2
SWA_GROUP = SWA_Q_HEADS // SWA_KV_HEADS
WINDOW = 128
Q_BLOCK = 128
ROPE_THETA = 10000.0
N_GROUPS = 4
EXPERTS_PER_GROUP = 8
N_EXPERTS = N_GROUPS * EXPERTS_PER_GROUP
TOP_K_IN_GROUP = 2
EXPERT_FF = 512
EXPERT_BLOCK = 256
EPS = 1e-6
NEG = -1e30

DIFF_QK_COLS = DIFF_HEADS * 2 * HEAD_DIM
DIFF_V_COLS = DIFF_HEADS * DIFF_V_DIM
SWA_Q_COLS = SWA_Q_HEADS * HEAD_DIM
SWA_KV_COLS = SWA_KV_HEADS * HEAD_DIM
IN_COLS = 2 * DIFF_QK_COLS + DIFF_V_COLS + SWA_Q_COLS + 2 * SWA_KV_COLS
MIX_WIDTH = DIFF_V_COLS + SWA_Q_COLS
SPLITS = [DIFF_QK_COLS, 2 * DIFF_QK_COLS, 2 * DIFF_QK_COLS + DIFF_V_COLS,
          2 * DIFF_QK_COLS + DIFF_V_COLS + SWA_Q_COLS,
          2 * DIFF_QK_COLS + DIFF_V_COLS + SWA_Q_COLS + SWA_KV_COLS]

kernel_name = "hymba_diffattn_swasink_hiermoe"


def rms_norm(x, g):
    xf = x.astype(jnp.float32)
    y = xf * lax.rsqrt(jnp.mean(xf * xf, axis=-1, keepdims=True) + EPS)
    return (y * g.astype(jnp.float32)).astype(x.dtype)


def rope_tables(seq):
    inv = 1.0 / (ROPE_THETA ** (jnp.arange(0, HEAD_DIM, 2, dtype=jnp.float32) / HEAD_DIM))
    ang = jnp.arange(seq, dtype=jnp.float32)[:, None] * inv[None, :]
    return jnp.cos(ang), jnp.sin(ang)


def apply_rope(x, cos, sin):
    half = x.shape[-1] // 2
    shape = (1, x.shape[1]) + (1,) * (x.ndim - 3) + (half,)
    c = cos.reshape(shape).astype(x.dtype)
    s = sin.reshape(shape).astype(x.dtype)
    x1, x2 = x[..., :half], x[..., half:]
    return jnp.concatenate([x1 * c - x2 * s, x1 * s + x2 * c], axis=-1)


def diff_attention(q, k, v, lam, subln_g, lambda_init):
    B, S = q.shape[0], q.shape[1]
    nq = S // Q_BLOCK
    scale = HEAD_DIM ** -0.5
    qb = q.reshape(B, nq, Q_BLOCK, DIFF_HEADS, 2, HEAD_DIM).transpose(1, 0, 2, 3, 4, 5)
    kpos = jnp.arange(S)

    def block(args):
        qblk, i = args
        s = jnp.einsum('bqhcd,bkhcd->bhcqk', qblk, k).astype(jnp.float32) * scale
        qpos = i * Q_BLOCK + jnp.arange(Q_BLOCK)
        causal = kpos[None, :] <= qpos[:, None]
        p = jax.nn.softmax(jnp.where(causal, s, NEG), axis=-1)
        a = p[:, :, 0] - lam * p[:, :, 1]
        return jnp.einsum('bhqk,bkhd->bqhd', a.astype(v.dtype), v)

    o = lax.map(block, (qb, jnp.arange(nq)))
    o = o.transpose(1, 0, 2, 3, 4).reshape(B, S, DIFF_HEADS, DIFF_V_DIM)
    o = rms_norm(o, subln_g) * (1.0 - lambda_init)
    return o.reshape(B, S, DIFF_V_COLS)


def swa_sink_attention(q, k, v, sinks):
    B, S = q.shape[0], q.shape[1]
    nb = S // WINDOW
    scale = HEAD_DIM ** -0.5
    qb = q.reshape(B, nb, WINDOW, SWA_KV_HEADS, SWA_GROUP, HEAD_DIM)

    def with_prev(t):
        t = t.reshape(B, nb, WINDOW, SWA_KV_HEADS, HEAD_DIM)
        prev = jnp.concatenate([jnp.zeros_like(t[:, :1]), t[:, :-1]], axis=1)
        return jnp.concatenate([prev, t], axis=2)

    kw, vw = with_prev(k), with_prev(v)
    s = jnp.einsum('bnqkgd,bnjkd->bnkgqj', qb, kw).astype(jnp.float32) * scale
    qi = jnp.arange(WINDOW)[:, None]
    kj = jnp.arange(2 * WINDOW)[None, :]
    rel = qi + WINDOW - kj
    band = (rel >= 0) & (rel < WINDOW)
    blk = jnp.arange(nb)[:, None, None]
    valid = band[None] & ((blk * WINDOW + kj[None] - WINDOW) >= 0)
    s = jnp.where(valid[None, :, None, None], s, NEG)
    sink = jnp.broadcast_to(
        sinks.astype(jnp.float32).reshape(SWA_KV_HEADS, SWA_GROUP)[None, None, :, :, None, None],
        s.shape[:-1] + (1,))
    p = jax.nn.softmax(jnp.concatenate([s, sink], axis=-1), axis=-1)[..., :-1]
    o = jnp.einsum('bnkgqj,bnjkd->bnqkgd', p.astype(v.dtype), vw)
    return o.reshape(B, S, SWA_Q_COLS)


def hier_moe(xn, w_rg, b_rg, w_re, b_re, w_gate, w_up, w_down):
    B, S, D = xn.shape
    T = B * S
    xt = xn.reshape(T, D)
    pg = jax.nn.softmax((xt @ w_rg).astype(jnp.float32) + b_rg.astype(jnp.float32), axis=-1)
    p_top, g_idx = lax.top_k(pg, 1)
    el = ((xt @ w_re).astype(jnp.float32) + b_re.astype(jnp.float32)).reshape(T, N_GROUPS, EXPERTS_PER_GROUP)
    el_sel = jnp.take_along_axis(el, g_idx[:, :, None], axis=1)[:, 0]
    v2, i2 = lax.top_k(el_sel, TOP_K_IN_GROUP)
    gates = jax.nn.softmax(v2, axis=-1) * p_top
    expert_id = (g_idx * EXPERTS_PER_GROUP + i2).astype(jnp.int32)

    A = T * TOP_K_IN_GROUP
    NB = -(-A // EXPERT_BLOCK) + N_EXPERTS
    P = NB * EXPERT_BLOCK
    e_flat = expert_id.reshape(A)
    g_flat = gates.reshape(A)
    tok_flat = jnp.arange(A, dtype=jnp.int32) // TOP_K_IN_GROUP
    order = jnp.argsort(e_flat)
    e_s, g_s, tok_s = e_flat[order], g_flat[order], tok_flat[order]
    counts = jnp.bincount(e_flat, length=N_EXPERTS).astype(jnp.int32)
    start = jnp.cumsum(counts) - counts
    padded = ((counts + EXPERT_BLOCK - 1) // EXPERT_BLOCK) * EXPERT_BLOCK
    pad_end = jnp.cumsum(padded)
    pad_start = pad_end - padded
    dest = pad_start[e_s] + (jnp.arange(A, dtype=jnp.int32) - start[e_s])
    slot_tok = jnp.full((P,), T, jnp.int32).at[dest].set(tok_s)
    slot_gate = jnp.zeros((P,), jnp.float32).at[dest].set(g_s)
    block_start = jnp.arange(NB, dtype=jnp.int32) * EXPERT_BLOCK
    block_expert = jnp.minimum(jnp.searchsorted(pad_end, block_start, side='right'),
                               N_EXPERTS - 1).astype(jnp.int32)
    x_pad = jnp.concatenate([xt, jnp.zeros((1, D), xt.dtype)], axis=0)
    xs = x_pad[slot_tok].reshape(NB, EXPERT_BLOCK, D)

    def run_block(args):
        xb, e = args
        h = jax.nn.silu(xb @ w_gate[e]) * (xb @ w_up[e])
        return h @ w_down[e]

    ys = lax.map(run_block, (xs, block_expert)).reshape(P, D)
    out = jnp.zeros((T + 1, D), jnp.float32).at[slot_tok].add(ys.astype(jnp.float32) * slot_gate[:, None])[:T]
    return out.reshape(B, S, D).astype(xn.dtype)


def setup_inputs(seed: int = 0) -> dict:
    key = jax.random.key(seed)
    ks = jax.random.split(key, 20)
    f32 = jnp.float32
    L, D = DEPTH, D_MODEL
    nrm = lambda k, shape, s: jax.random.normal(k, shape, f32) * s
    return {
        "x": nrm(ks[0], (BATCH, SEQ, D), 1.0),
        "norm1_g": 1.0 + nrm(ks[1], (L, D), 0.05),
        "w_in": nrm(ks[2], (L, D, IN_COLS), D ** -0.5),
        "lambda_q1": nrm(ks[3], (L, HEAD_DIM), 0.1),
        "lambda_k1": nrm(ks[4], (L, HEAD_DIM), 0.1),
        "lambda_q2": nrm(ks[5], (L, HEAD_DIM), 0.1),
        "lambda_k2": nrm(ks[6], (L, HEAD_DIM), 0.1),
        "subln_g": 1.0 + nrm(ks[7], (L, DIFF_V_DIM), 0.05),
        "sinks": nrm(ks[8], (L, SWA_Q_HEADS), 0.5),
        "w_out": nrm(ks[9], (L, MIX_WIDTH, D), MIX_WIDTH ** -0.5),
        "norm2_g": 1.0 + nrm(ks[10], (L, D), 0.05),
        "w_router_group": nrm(ks[11], (L, D, N_GROUPS), D ** -0.5),
        "b_router_group": nrm(ks[12], (L, N_GROUPS), 0.01),
        "w_router_expert": nrm(ks[13], (L, D, N_EXPERTS), D ** -0.5),
        "b_router_expert": nrm(ks[14], (L, N_EXPERTS), 0.01),
        "w_gate": nrm(ks[15], (L, N_EXPERTS, D, EXPERT_FF), D ** -0.5),
        "w_up": nrm(ks[16], (L, N_EXPERTS, D, EXPERT_FF), D ** -0.5),
        "w_down": nrm(ks[17], (L, N_EXPERTS, EXPERT_FF, D), EXPERT_FF ** -0.5),
        "final_g": 1.0 + nrm(ks[18], (D,), 0.05),
    }


def reference(x, norm1_g, w_in, lambda_q1, lambda_k1, lambda_q2, lambda_k2, subln_g, sinks,
              w_out, norm2_g, w_router_group, b_router_group, w_router_expert, b_router_expert,
              w_gate, w_up, w_down, final_g):
    B, S, D = x.shape
    cos, sin = rope_tables(S)
    for l in range(DEPTH):
        lambda_init = 0.8 - 0.6 * math.exp(-0.3 * l)
        n1 = rms_norm(x, norm1_g[l])
        qkv = n1 @ w_in[l]
        dq, dk, dv, sq, sk, sv = jnp.split(qkv, SPLITS, axis=-1)
        dq = apply_rope(dq.reshape(B, S, DIFF_HEADS, 2, HEAD_DIM), cos, sin)
        dk = apply_rope(dk.reshape(B, S, DIFF_HEADS, 2, HEAD_DIM), cos, sin)
        dv = dv.reshape(B, S, DIFF_HEADS, DIFF_V_DIM)
        lam = (jnp.exp(jnp.sum(lambda_q1[l].astype(jnp.float32) * lambda_k1[l].astype(jnp.float32)))
               - jnp.exp(jnp.sum(lambda_q2[l].astype(jnp.float32) * lambda_k2[l].astype(jnp.float32)))
               + lambda_init)
        o_diff = diff_attention(dq, dk, dv, lam, subln_g[l], lambda_init)
        sq = apply_rope(sq.reshape(B, S, SWA_Q_HEADS, HEAD_DIM), cos, sin)
        sk = apply_rope(sk.reshape(B, S, SWA_KV_HEADS, HEAD_DIM), cos, sin)
        sv = sv.reshape(B, S, SWA_KV_HEADS, HEAD_DIM)
        o_swa = swa_sink_attention(sq, sk, sv, sinks[l])
        x = x + jnp.concatenate([o_diff, o_swa], axis=-1) @ w_out[l]
        n2 = rms_norm(x, norm2_g[l])
        x = x + hier_moe(n2, w_router_group[l], b_router_group[l], w_router_expert[l],
                         b_router_expert[l], w_gate[l], w_up[l], w_down[l])
    return rms_norm(x, final_g)
```

```python
import functools
import math

import jax
import jax.numpy as jnp
from jax import lax
from jax.experimental import pallas as pl
from jax.experimental.pallas import tpu as pltpu

HEAD_DIM = 64
DIFF_HEADS = 4
DIFF_V_DIM = 2 * HEAD_DIM
SWA_Q_HEADS = 8
SWA_KV_HEADS = 2
SWA_GROUP = SWA_Q_HEADS // SWA_KV_HEADS
WINDOW = 128
ROPE_THETA = 10000.0
N_GROUPS = 4
EXPERTS_PER_GROUP = 8
N_EXPERTS = N_GROUPS * EXPERTS_PER_GROUP
TOP_K = 2
EXPERT_BLOCK = 256
EPS = 1e-6
NEG = -1e30

DIFF_QK_COLS = DIFF_HEADS * 2 * HEAD_DIM
DIFF_V_COLS = DIFF_HEADS * DIFF_V_DIM
SWA_Q_COLS = SWA_Q_HEADS * HEAD_DIM
SWA_KV_COLS = SWA_KV_HEADS * HEAD_DIM
LANES = 128
ROUTER_COLS = LANES

BF16 = jnp.bfloat16
F32 = jnp.float32


def _rope_lanes(x, cos_l, sin_l, first_half):
    rot = jnp.where(first_half, pltpu.roll(x, 96, 1), pltpu.roll(x, 32, 1))
    return x * cos_l + rot * sin_l


def _proj_kernel(x_ref, g_ref, wnat_ref, wtr_ref, cosl_ref, sinl_ref, cost_ref, sint_ref,
                 dqt_ref, dk_ref, dvt_ref, sq_ref, sk_ref, sv_ref, *, tk):
    x = x_ref[0]
    tm = x.shape[0]
    n1 = x * lax.rsqrt(jnp.mean(x * x, axis=-1, keepdims=True) + EPS) * g_ref[...]
    n1b = n1.astype(BF16)
    nat = jnp.dot(n1b, wnat_ref[...], preferred_element_type=F32)
    tr = lax.dot_general(wtr_ref[...], n1b, (((1,), (1,)), ((), ())),
                         preferred_element_type=F32)

    cos_l, sin_l = cosl_ref[...], sinl_ref[...]
    first_half = (lax.broadcasted_iota(jnp.int32, (tm, LANES), 1) & (HEAD_DIM - 1)) < HEAD_DIM // 2
    for h in range(DIFF_HEADS):
        slab = nat[:, h * LANES:(h + 1) * LANES]
        dk_ref[0, h] = _rope_lanes(slab, cos_l, sin_l, first_half).astype(BF16)
    for c in range(SWA_Q_COLS // LANES):
        lo = DIFF_QK_COLS + c * LANES
        sq_ref[0, :, c * LANES:(c + 1) * LANES] = _rope_lanes(
            nat[:, lo:lo + LANES], cos_l, sin_l, first_half).astype(BF16)
    lo = DIFF_QK_COLS + SWA_Q_COLS
    sk_ref[0] = _rope_lanes(nat[:, lo:lo + LANES], cos_l, sin_l, first_half).astype(BF16)
    sv_ref[0] = nat[:, lo + LANES:lo + 2 * LANES].astype(BF16)

    cos_t, sin_t = cost_ref[...], sint_ref[...]
    half = HEAD_DIM // 2
    for h in range(DIFF_HEADS):
        for c in range(2):
            r0 = h * 2 * HEAD_DIM + c * HEAD_DIM
            x1 = tr[r0:r0 + half]
            x2 = tr[r0 + half:r0 + HEAD_DIM]
            dqt_ref[0, h, c * HEAD_DIM:c * HEAD_DIM + half] = (x1 * cos_t - x2 * sin_t).astype(BF16)
            dqt_ref[0, h, c * HEAD_DIM + half:(c + 1) * HEAD_DIM] = (x1 * sin_t + x2 * cos_t).astype(BF16)
    for h in range(DIFF_HEADS):
        r0 = DIFF_QK_COLS + h * DIFF_V_DIM
        for c in range(tm // tk):
            dvt_ref[0, h, c] = tr[r0:r0 + DIFF_V_DIM, c * tk:(c + 1) * tk].astype(BF16)


def _proj_call(x, g1, w_nat, w_tr, cos_l, sin_l, cos_t, sin_t, *, tm, tk):
    B, S, D = x.shape
    nkv = S // tk
    grid = (B, S // tm)
    const = lambda b, i: (0, 0)
    out_shape = (
        jax.ShapeDtypeStruct((B, DIFF_HEADS, 2 * HEAD_DIM, S), BF16),
        jax.ShapeDtypeStruct((B, DIFF_HEADS, S, 2 * HEAD_DIM), BF16),
        jax.ShapeDtypeStruct((B, DIFF_HEADS, nkv, DIFF_V_DIM, tk), BF16),
        jax.ShapeDtypeStruct((B, S, SWA_Q_COLS), BF16),
        jax.ShapeDtypeStruct((B, S, SWA_KV_COLS), BF16),
        jax.ShapeDtypeStruct((B, S, SWA_KV_COLS), BF16),
    )
    return pl.pallas_call(
        functools.partial(_proj_kernel, tk=tk),
        grid=grid,
        in_specs=[
            pl.BlockSpec((1, tm, D), lambda b, i: (b, i, 0)),
            pl.BlockSpec((1, D), const),
            pl.BlockSpec(w_nat.shape, const),
            pl.BlockSpec(w_tr.shape, const),
            pl.BlockSpec((tm, LANES), lambda b, i: (i, 0)),
            pl.BlockSpec((tm, LANES), lambda b, i: (i, 0)),
            pl.BlockSpec((HEAD_DIM // 2, tm), lambda b, i: (0, i)),
            pl.BlockSpec((HEAD_DIM // 2, tm), lambda b, i: (0, i)),
        ],
        out_specs=(
            pl.BlockSpec((1, DIFF_HEADS, 2 * HEAD_DIM, tm), lambda b, i: (b, 0, 0, i)),
            pl.BlockSpec((1, DIFF_HEADS, tm, 2 * HEAD_DIM), lambda b, i: (b, 0, i, 0)),
            pl.BlockSpec((1, DIFF_HEADS, tm // tk, DIFF_V_DIM, tk), lambda b, i: (b, 0, i, 0, 0)),
            pl.BlockSpec((1, tm, SWA_Q_COLS), lambda b, i: (b, i, 0)),
            pl.BlockSpec((1, tm, SWA_KV_COLS), lambda b, i: (b, i, 0)),
            pl.BlockSpec((1, tm, SWA_KV_COLS), lambda b, i: (b, i, 0)),
        ),
        out_shape=out_shape,
        compiler_params=pltpu.CompilerParams(
            dimension_semantics=("parallel", "parallel"), vmem_limit_bytes=48 * 1024 * 1024),
        name="proj_rope",
    )(x, g1, w_nat, w_tr, cos_l, sin_l, cos_t, sin_t)


def _diff_kernel(lam_ref, qt_ref, k_ref, vt_ref, g_ref, o_ref, *, tq, tk, lambda_init):
    i = pl.program_id(2)
    qt = qt_ref[0, 0]
    z = jnp.zeros((HEAD_DIM, tq), BF16)
    qw = jnp.concatenate([jnp.concatenate([qt[:HEAD_DIM], z], axis=1),
                          jnp.concatenate([z, qt[HEAD_DIM:]], axis=1)], axis=0)

    def step(j, carry, masked):
        m, l, acc = carry
        kt = k_ref[0, 0, pl.ds(pl.multiple_of(j * tk, tk), tk), :]
        s = jnp.dot(kt, qw, preferred_element_type=F32)
        if masked:
            kpos = j * tk + lax.broadcasted_iota(jnp.int32, (tk, 2 * tq), 0)
            qpos = i * tq + (lax.broadcasted_iota(jnp.int32, (tk, 2 * tq), 1) & (tq - 1))
            s = jnp.where(kpos <= qpos, s, NEG)
        m_new = jnp.maximum(m, jnp.max(s, axis=0, keepdims=True))
        alpha = jnp.exp(m - m_new)
        p = jnp.exp(s - m_new)
        l = alpha * l + jnp.sum(p, axis=0, keepdims=True)
        pv = jnp.dot(vt_ref[0, 0, j], p.astype(BF16), preferred_element_type=F32)
        return m_new, l, alpha * acc + pv

    nfull = (i * tq) // tk
    init = (jnp.full((1, 2 * tq), NEG, F32), jnp.zeros((1, 2 * tq), F32),
            jnp.zeros((DIFF_V_DIM, 2 * tq), F32))
    carry = lax.fori_loop(0, nfull, functools.partial(step, masked=False), init)
    m, l, acc = step(nfull, carry, masked=True)

    lam_p = lam_ref[...]
    lam = (jnp.exp(jnp.sum(lam_p[0:1] * lam_p[1:2], axis=-1, keepdims=True))
           - jnp.exp(jnp.sum(lam_p[2:3] * lam_p[3:4], axis=-1, keepdims=True)) + lambda_init)
    o = acc[:, :tq] / l[:, :tq] - lam * (acc[:, tq:] / l[:, tq:])
    o = o * lax.rsqrt(jnp.mean(o * o, axis=0, keepdims=True) + EPS)
    o_ref[0] = (o.T * g_ref[...] * (1.0 - lambda_init)).astype(BF16)


def _diff_call(lam_p, dqt, dk, dvt, subln_g, *, tq, tk, lambda_init):
    B, H, _, S = dqt.shape
    nkv = S // tk
    grid = (B, H, S // tq)
    return pl.pallas_call(
        functools.partial(_diff_kernel, tq=tq, tk=tk, lambda_init=lambda_init),
        grid=grid,
        in_specs=[
            pl.BlockSpec(lam_p.shape, lambda b, h, i: (0, 0)),
            pl.BlockSpec((1, 1, 2 * HEAD_DIM, tq), lambda b, h, i: (b, h, 0, i)),
            pl.BlockSpec((1, 1, S, 2 * HEAD_DIM), lambda b, h, i: (b, h, 0, 0)),
            pl.BlockSpec((1, 1, nkv, DIFF_V_DIM, tk), lambda b, h, i: (b, h, 0, 0, 0)),
            pl.BlockSpec((1, DIFF_V_DIM), lambda b, h, i: (0, 0)),
        ],
        out_specs=pl.BlockSpec((1, tq, DIFF_V_DIM), lambda b, h, i: (b, i, h)),
        out_shape=jax.ShapeDtypeStruct((B, S, DIFF_V_COLS), BF16),
        compiler_params=pltpu.CompilerParams(
            dimension_semantics=("parallel", "parallel", "arbitrary"),
            vmem_limit_bytes=48 * 1024 * 1024),
        name="diff_attn",
    )(lam_p, dqt, dk, dvt, subln_g)


def _swa_kernel(sinks_ref, q_ref, k_ref, v_ref, o_ref, *, tq):
    i = pl.program_id(1)
    q0 = i * tq
    kw_len = tq + WINDOW
    ks = pl.multiple_of(jnp.maximum(q0 - WINDOW, 0), WINDOW)
    kw = k_ref[0, pl.ds(ks, kw_len), :]
    vw = v_ref[0, pl.ds(ks, kw_len), :]
    qpos = q0 + lax.broadcasted_iota(jnp.int32, (tq, kw_len), 0)
    kpos = ks + lax.broadcasted_iota(jnp.int32, (tq, kw_len), 1)
    valid = (kpos <= qpos) & (kpos > qpos - WINDOW)
    lo = lax.broadcasted_iota(jnp.int32, (tq, LANES), 1) < HEAD_DIM
    for g in range(SWA_GROUP):
        qg = q_ref[0, :, g * LANES:(g + 1) * LANES]
        outs = []
        for kv in range(SWA_KV_HEADS):
            qm = jnp.where(lo if kv == 0 else jnp.logical_not(lo), qg, jnp.zeros_like(qg))
            s = lax.dot_general(qm, kw, (((1,), (1,)), ((), ())), preferred_element_type=F32)
            s = jnp.where(valid, s, NEG)
            sink = sinks_ref[kv * SWA_GROUP + g]
            m = jnp.maximum(jnp.max(s, axis=-1, keepdims=True), sink)
            p = jnp.exp(s - m)
            den = jnp.sum(p, axis=-1, keepdims=True) + jnp.exp(sink - m)
            outs.append(jnp.dot(p.astype(BF16), vw, preferred_element_type=F32) / den)
        o_ref[0, :, g * LANES:(g + 1) * LANES] = jnp.where(lo, outs[0], outs[1]).astype(BF16)


def _swa_call(sinks, sq, sk, sv, *, tq):
    B, S, _ = sq.shape
    grid_spec = pltpu.PrefetchScalarGridSpec(
        num_scalar_prefetch=1,
        grid=(B, S // tq),
        in_specs=[
            pl.BlockSpec((1, tq, SWA_Q_COLS), lambda b, i, s: (b, i, 0)),
            pl.BlockSpec((1, S, SWA_KV_COLS), lambda b, i, s: (b, 0, 0)),
            pl.BlockSpec((1, S, SWA_KV_COLS), lambda b, i, s: (b, 0, 0)),
        ],
        out_specs=pl.BlockSpec((1, tq, SWA_Q_COLS), lambda b, i, s: (b, i, 0)),
    )
    return pl.pallas_call(
        functools.partial(_swa_kernel, tq=tq),
        grid_spec=grid_spec,
        out_shape=jax.ShapeDtypeStruct((B, S, SWA_Q_COLS), BF16),
        compiler_params=pltpu.CompilerParams(
            dimension_semantics=("parallel", "arbitrary"), vmem_limit_bytes=40 * 1024 * 1024),
        name="swa_attn",
    )(sinks, sq, sk, sv)


def _mix_kernel(x_ref, od_ref, os_ref, wo_ref, g2_ref, wr_ref, br_ref, x1_ref, n2_ref, rt_ref):
    h = (x_ref[0]
         + jnp.dot(od_ref[0], wo_ref[:DIFF_V_COLS], preferred_element_type=F32)
         + jnp.dot(os_ref[0], wo_ref[DIFF_V_COLS:], preferred_element_type=F32))
    x1_ref[0] = h
    n2 = h * lax.rsqrt(jnp.mean(h * h, axis=-1, keepdims=True) + EPS) * g2_ref[...]
    n2_ref[0] = n2
    logits = jnp.dot(n2, wr_ref[...], preferred_element_type=F32,
                     precision=lax.Precision.HIGHEST) + br_ref[...]
    tm = logits.shape[0]
    lane = lax.broadcasted_iota(jnp.int32, (tm, ROUTER_COLS), 1)
    big = jnp.int32(ROUTER_COLS)
    gl = jnp.where(lane < N_GROUPS, logits, -jnp.inf)
    gm = jnp.max(gl, axis=-1, keepdims=True)
    p_top = 1.0 / jnp.sum(jnp.exp(gl - gm), axis=-1, keepdims=True)
    g_idx = jnp.min(jnp.where(gl == gm, lane, big), axis=-1, keepdims=True)
    e_lo = N_GROUPS + EXPERTS_PER_GROUP * g_idx
    el = jnp.where((lane >= e_lo) & (lane < e_lo + EXPERTS_PER_GROUP), logits, -jnp.inf)
    v1 = jnp.max(el, axis=-1, keepdims=True)
    i1 = jnp.min(jnp.where(el == v1, lane, big), axis=-1, keepdims=True)
    el2 = jnp.where(lane == i1, -jnp.inf, el)
    v2 = jnp.max(el2, axis=-1, keepdims=True)
    i2 = jnp.min(jnp.where(el2 == v2, lane, big), axis=-1, keepdims=True)
    e21 = jnp.exp(v2 - v1)
    gate1 = p_top / (1.0 + e21)
    gate2 = p_top * e21 / (1.0 + e21)
    rt = jnp.where(lane == 0, (i1 - N_GROUPS).astype(F32),
         jnp.where(lane == 1, (i2 - N_GROUPS).astype(F32),
         jnp.where(lane == 2, gate1, jnp.where(lane == 3, gate2, 0.0))))
    rt_ref[0] = rt


def _mix_call(x, o_diff, o_swa, w_out, g2, w_router, b_router, *, tm):
    B, S, D = x.shape
    const = lambda b, i: (0, 0)
    row = lambda b, i: (b, i, 0)
    return pl.pallas_call(
        _mix_kernel,
        grid=(B, S // tm),
        in_specs=[
            pl.BlockSpec((1, tm, D), row),
            pl.BlockSpec((1, tm, DIFF_V_COLS), row),
            pl.BlockSpec((1, tm, SWA_Q_COLS), row),
            pl.BlockSpec(w_out.shape, const),
            pl.BlockSpec((1, D), const),
            pl.BlockSpec(w_router.shape, const),
            pl.BlockSpec((1, ROUTER_COLS), const),
        ],
        out_specs=(pl.BlockSpec((1, tm, D), row), pl.BlockSpec((1, tm, D), row),
                   pl.BlockSpec((1, tm, ROUTER_COLS), row)),
        out_shape=(jax.ShapeDtypeStruct((B, S, D), F32), jax.ShapeDtypeStruct((B, S, D), F32),
                   jax.ShapeDtypeStruct((B, S, ROUTER_COLS), F32)),
        compiler_params=pltpu.CompilerParams(
            dimension_semantics=("parallel", "parallel"), vmem_limit_bytes=48 * 1024 * 1024),
        name="outproj_router",
    )(x, o_diff, o_swa, w_out, g2, w_router, b_router)


def _row_gather(idx_hbm_row, idx_smem, src_hbm, dst_vmem, idx_sem, row_sem, n_rows):
    idx_copy = pltpu.make_async_copy(idx_hbm_row, idx_smem, idx_sem)
    idx_copy.start()
    idx_copy.wait()

    def issue(r, c):
        pltpu.make_async_copy(src_hbm.at[pl.ds(idx_smem[r], 1)], dst_vmem.at[pl.ds(r, 1)], row_sem).start()
        return c

    lax.fori_loop(0, n_rows, issue, 0)
    pltpu.make_async_copy(src_hbm.at[pl.ds(0, n_rows)], dst_vmem, row_sem).wait()


def _expert_kernel(be_ref, nused_ref, tok_hbm, n2_hbm, wg_ref, wu_ref, wd_ref, y_ref,
                   tok_smem, xbuf, idx_sem, row_sem):
    b = pl.program_id(0)

    @pl.when(b < nused_ref[0])
    def _():
        _row_gather(tok_hbm.at[b], tok_smem, n2_hbm, xbuf, idx_sem, row_sem, EXPERT_BLOCK)
        xb = xbuf[...].astype(BF16)
        gate = jnp.dot(xb, wg_ref[0], preferred_element_type=F32)
        up = jnp.dot(xb, wu_ref[0], preferred_element_type=F32)
        hid = (gate * jax.nn.sigmoid(gate) * up).astype(BF16)
        y_ref[...] = jnp.dot(hid, wd_ref[0], preferred_element_type=F32)

    @pl.when(b >= nused_ref[0])
    def _():
        y_ref[...] = jnp.zeros_like(y_ref)


def _expert_call(block_expert, n_used, slot_tok, n2, w_gate, w_up, w_down):
    T, D = n2.shape
    NB = slot_tok.shape[0]
    E, _, F = w_gate.shape
    grid_spec = pltpu.PrefetchScalarGridSpec(
        num_scalar_prefetch=2,
        grid=(NB,),
        in_specs=[
            pl.BlockSpec(memory_space=pl.ANY),
            pl.BlockSpec(memory_space=pl.ANY),
            pl.BlockSpec((1, D, F), lambda b, be, nu: (be[b], 0, 0)),
            pl.BlockSpec((1, D, F), lambda b, be, nu: (be[b], 0, 0)),
            pl.BlockSpec((1, F, D), lambda b, be, nu: (be[b], 0, 0)),
        ],
        out_specs=pl.BlockSpec((EXPERT_BLOCK, D), lambda b, be, nu: (b, 0)),
        scratch_shapes=[
            pltpu.SMEM((EXPERT_BLOCK,), jnp.int32),
            pltpu.VMEM((EXPERT_BLOCK, D), F32),
            pltpu.SemaphoreType.DMA(()),
            pltpu.SemaphoreType.DMA(()),
        ],
    )
    return pl.pallas_call(
        _expert_kernel,
        grid_spec=grid_spec,
        out_shape=jax.ShapeDtypeStruct((NB * EXPERT_BLOCK, D), F32),
        compiler_params=pltpu.CompilerParams(
            dimension_semantics=("arbitrary",), vmem_limit_bytes=40 * 1024 * 1024),
        name="moe_experts",
    )(block_expert, n_used, slot_tok, n2, w_gate, w_up, w_down)


def _combine_kernel(d0_hbm, d1_hbm, x1_ref, rt_ref, ys_hbm, fg_ref, o_ref,
                    d_smem, y0, y1, idx_sem, row_sem, *, tm, final_norm):
    t = pl.program_id(0)
    _row_gather(d0_hbm.at[t], d_smem, ys_hbm, y0, idx_sem, row_sem, tm)
    _row_gather(d1_hbm.at[t], d_smem, ys_hbm, y1, idx_sem, row_sem, tm)
    rt = rt_ref[...]
    h = x1_ref[...] + rt[:, 2:3] * y0[...] + rt[:, 3:4] * y1[...]
    if final_norm:
        h = h * lax.rsqrt(jnp.mean(h * h, axis=-1, keepdims=True) + EPS) * fg_ref[...]
    o_ref[...] = h


def _combine_call(dest0, dest1, x1, rt, ys, final_g, *, tm, final_norm):
    T, D = x1.shape
    return pl.pallas_call(
        functools.partial(_combine_kernel, tm=tm, final_norm=final_norm),
        grid=(T // tm,),
        in_specs=[
            pl.BlockSpec(memory_space=pl.ANY),
            pl.BlockSpec(memory_space=pl.ANY),
            pl.BlockSpec((tm, D), lambda t: (t, 0)),
            pl.BlockSpec((tm, ROUTER_COLS), lambda t: (t, 0)),
            pl.BlockSpec(memory_space=pl.ANY),
            pl.BlockSpec((1, D), lambda t: (0, 0)),
        ],
        out_specs=pl.BlockSpec((tm, D), lambda t: (t, 0)),
        out_shape=jax.ShapeDtypeStruct((T, D), F32),
        scratch_shapes=[
            pltpu.SMEM((tm,), jnp.int32),
            pltpu.VMEM((tm, D), F32),
            pltpu.VMEM((tm, D), F32),
            pltpu.SemaphoreType.DMA(()),
            pltpu.SemaphoreType.DMA(()),
        ],
        compiler_params=pltpu.CompilerParams(
            dimension_semantics=("arbitrary",), vmem_limit_bytes=40 * 1024 * 1024),
        name="moe_combine",
    )(dest0, dest1, x1, rt, ys, final_g)


def _dispatch_plan(expert_id, T):
    A = T * TOP_K
    NB = -(-A // EXPERT_BLOCK) + N_EXPERTS
    P = NB * EXPERT_BLOCK
    e_flat = expert_id.reshape(A)
    order = jnp.argsort(e_flat).astype(jnp.int32)
    counts = jnp.sum(e_flat[:, None] == jnp.arange(N_EXPERTS, dtype=jnp.int32)[None, :],
                     axis=0, dtype=jnp.int32)
    start = jnp.cumsum(counts) - counts
    padded = ((counts + EXPERT_BLOCK - 1) // EXPERT_BLOCK) * EXPERT_BLOCK
    pad_end = jnp.cumsum(padded)
    pad_start = pad_end - padded
    block_start = jnp.arange(NB, dtype=jnp.int32) * EXPERT_BLOCK
    block_expert = jnp.minimum(jnp.searchsorted(pad_end, block_start, side='right'),
                               N_EXPERTS - 1).astype(jnp.int32)
    n_used = (pad_end[-1] // EXPERT_BLOCK).astype(jnp.int32).reshape(1)
    slot = jnp.arange(P, dtype=jnp.int32)
    se = jnp.repeat(block_expert, EXPERT_BLOCK)
    r = slot - pad_start[se]
    is_real = r < counts[se]
    src = jnp.clip(start[se] + r, 0, A - 1)
    slot_tok = jnp.where(is_real, order[src] // TOP_K, 0).astype(jnp.int32)
    e_s = e_flat[order]
    dest_sorted = pad_start[e_s] + (jnp.arange(A, dtype=jnp.int32) - start[e_s])
    dest = jnp.zeros((A,), jnp.int32).at[order].set(dest_sorted, unique_indices=True)
    return block_expert, n_used, slot_tok.reshape(NB, EXPERT_BLOCK), dest.reshape(T, TOP_K)


def _rope_tables(S):
    inv = 1.0 / (ROPE_THETA ** (jnp.arange(0, HEAD_DIM, 2, dtype=F32) / HEAD_DIM))
    ang = jnp.arange(S, dtype=F32)[:, None] * inv[None, :]
    cos, sin = jnp.cos(ang), jnp.sin(ang)
    cos_l = jnp.tile(cos, (1, LANES // (HEAD_DIM // 2)))
    sin_l = jnp.tile(jnp.concatenate([-sin, sin], axis=1), (1, LANES // HEAD_DIM))
    return cos_l, sin_l, cos.T, sin.T


def _swa_pair_perm():
    idx = []
    for g in range(SWA_GROUP):
        for kv in range(SWA_KV_HEADS):
            h = kv * SWA_GROUP + g
            idx.extend(range(h * HEAD_DIM, (h + 1) * HEAD_DIM))
    return jnp.asarray(idx, dtype=jnp.int32)


def kernel(x, norm1_g, w_in, lambda_q1, lambda_k1, lambda_q2, lambda_k2, subln_g, sinks, w_out,
           norm2_g, w_router_group, b_router_group, w_router_expert, b_router_expert,
           w_gate, w_up, w_down, final_g):
    B, S, D = x.shape
    T = B * S
    depth = w_in.shape[0]
    tq = tk = 256
    tm_proj = 512
    tm_mix = 512
    tq_swa = 256
    tm_comb = 256
    scale = HEAD_DIM ** -0.5
    cos_l, sin_l, cos_t, sin_t = _rope_tables(S)
    perm = _swa_pair_perm()

    c0 = DIFF_QK_COLS
    c1 = 2 * DIFF_QK_COLS
    c2 = c1 + DIFF_V_COLS
    c3 = c2 + SWA_Q_COLS
    c4 = c3 + SWA_KV_COLS
    for l in range(depth):
        lambda_init = 0.8 - 0.6 * math.exp(-0.3 * l)
        w = w_in[l]
        w_sq = (w[:, c2:c3] * scale)[:, perm]
        w_nat = jnp.concatenate([w[:, c0:c1], w_sq, w[:, c3:c4], w[:, c4:]], axis=1).astype(BF16)
        w_tr = jnp.concatenate([w[:, :c0] * scale, w[:, c1:c2]], axis=1).T.astype(BF16)
        dqt, dk, dvt, sq, sk, sv = _proj_call(
            x, norm1_g[l][None, :], w_nat, w_tr, cos_l, sin_l, cos_t, sin_t, tm=tm_proj, tk=tk)

        lam_p = jnp.stack([lambda_q1[l], lambda_k1[l], lambda_q2[l], lambda_k2[l]]).astype(F32)
        o_diff = _diff_call(lam_p, dqt, dk, dvt, subln_g[l][None, :].astype(F32),
                            tq=tq, tk=tk, lambda_init=lambda_init)
        sinks_paired = sinks[l].astype(F32)
        o_swa = _swa_call(sinks_paired, sq, sk, sv, tq=tq_swa)

        wo = w_out[l]
        wo_b = jnp.concatenate([wo[:DIFF_V_COLS], wo[DIFF_V_COLS:][perm]], axis=0).astype(BF16)
        w_router = jnp.zeros((D, ROUTER_COLS), F32)
        w_router = w_router.at[:, :N_GROUPS].set(w_router_group[l])
        w_router = w_router.at[:, N_GROUPS:N_GROUPS + N_EXPERTS].set(w_router_expert[l])
        b_router = jnp.zeros((1, ROUTER_COLS), F32)
        b_router = b_router.at[0, :N_GROUPS].set(b_router_group[l])
        b_router = b_router.at[0, N_GROUPS:N_GROUPS + N_EXPERTS].set(b_router_expert[l])
        x1, n2, rt = _mix_call(x, o_diff, o_swa, wo_b, norm2_g[l][None, :], w_router, b_router, tm=tm_mix)

        rt2 = rt.reshape(T, ROUTER_COLS)
        expert_id = rt2[:, :TOP_K].astype(jnp.int32)
        block_expert, n_used, slot_tok, dest = _dispatch_plan(expert_id, T)
        ys = _expert_call(block_expert, n_used, slot_tok, n2.reshape(T, D),
                          w_gate[l].astype(BF16), w_up[l].astype(BF16), w_down[l].astype(BF16))
        x = _combine_call(dest[:, 0].reshape(T // tm_comb, tm_comb), dest[:, 1].reshape(T // tm_comb, tm_comb),
                          x1.reshape(T, D), rt2, ys, final_g[None, :],
                          tm=tm_comb, final_norm=(l == depth - 1)).reshape(B, S, D)
    return x
```

```python
import functools
import math

import jax
import jax.numpy as jnp
from jax import lax
from jax.experimental import pallas as pl
from jax.experimental.pallas import tpu as pltpu

HEAD_DIM = 64
DIFF_HEADS = 4
DIFF_V_DIM = 2 * HEAD_DIM
SWA_Q_HEADS = 8
SWA_KV_HEADS = 2
SWA_GROUP = SWA_Q_HEADS // SWA_KV_HEADS
WINDOW = 128
ROPE_THETA = 10000.0
N_GROUPS = 4
EXPERTS_PER_GROUP = 8
N_EXPERTS = N_GROUPS * EXPERTS_PER_GROUP
TOP_K = 2
EXPERT_BLOCK = 256
EPS = 1e-6
NEG = -1e30

DIFF_QK_COLS = DIFF_HEADS * 2 * HEAD_DIM
DIFF_V_COLS = DIFF_HEADS * DIFF_V_DIM
SWA_Q_COLS = SWA_Q_HEADS * HEAD_DIM
SWA_KV_COLS = SWA_KV_HEADS * HEAD_DIM
LANES = 128
ROUTER_COLS = LANES

BF16 = jnp.bfloat16
F32 = jnp.float32


def _rope_lanes(x, cos_l, sin_l, first_half):
    rot = jnp.where(first_half, pltpu.roll(x, 96, 1), pltpu.roll(x, 32, 1))
    return x * cos_l + rot * sin_l


def _proj_kernel(x_ref, g_ref, wnat_ref, wtr_ref, cosl_ref, sinl_ref, cost_ref, sint_ref,
                 dqt_ref, dk_ref, dvt_ref, sq_ref, sk_ref, sv_ref, *, tk):
    x = x_ref[0]
    tm = x.shape[0]
    n1 = x * lax.rsqrt(jnp.mean(x * x, axis=-1, keepdims=True) + EPS) * g_ref[...]
    n1b = n1.astype(BF16)
    nat = jnp.dot(n1b, wnat_ref[...], preferred_element_type=F32)
    tr = lax.dot_general(wtr_ref[...], n1b, (((1,), (1,)), ((), ())),
                         preferred_element_type=F32)

    cos_l, sin_l = cosl_ref[...], sinl_ref[...]
    first_half = (lax.broadcasted_iota(jnp.int32, (tm, LANES), 1) & (HEAD_DIM - 1)) < HEAD_DIM // 2
    for h in range(DIFF_HEADS):
        slab = nat[:, h * LANES:(h + 1) * LANES]
        dk_ref[0, h] = _rope_lanes(slab, cos_l, sin_l, first_half).astype(BF16)
    for c in range(SWA_Q_COLS // LANES):
        lo = DIFF_QK_COLS + c * LANES
        sq_ref[0, :, c * LANES:(c + 1) * LANES] = _rope_lanes(
            nat[:, lo:lo + LANES], cos_l, sin_l, first_half).astype(BF16)
    lo = DIFF_QK_COLS + SWA_Q_COLS
    sk_ref[0] = _rope_lanes(nat[:, lo:lo + LANES], cos_l, sin_l, first_half).astype(BF16)
    sv_ref[0] = nat[:, lo + LANES:lo + 2 * LANES].astype(BF16)

    cos_t, sin_t = cost_ref[...], sint_ref[...]
    half = HEAD_DIM // 2
    for h in range(DIFF_HEADS):
        for c in range(2):
            r0 = h * 2 * HEAD_DIM + c * HEAD_DIM
            x1 = tr[r0:r0 + half]
            x2 = tr[r0 + half:r0 + HEAD_DIM]
            dqt_ref[0, h, c * HEAD_DIM:c * HEAD_DIM + half] = (x1 * cos_t - x2 * sin_t).astype(BF16)
            dqt_ref[0, h, c * HEAD_DIM + half:(c + 1) * HEAD_DIM] = (x1 * sin_t + x2 * cos_t).astype(BF16)
    for h in range(DIFF_HEADS):
        r0 = DIFF_QK_COLS + h * DIFF_V_DIM
        for c in range(tm // tk):
            dvt_ref[0, h, c] = tr[r0:r0 + DIFF_V_DIM, c * tk:(c + 1) * tk].astype(BF16)


def _proj_call(x, g1, w_nat, w_tr, cos_l, sin_l, cos_t, sin_t, *, tm, tk):
    B, S, D = x.shape
    nkv = S // tk
    grid = (B, S // tm)
    const = lambda b, i: (0, 0)
    out_shape = (
        jax.ShapeDtypeStruct((B, DIFF_HEADS, 2 * HEAD_DIM, S), BF16),
        jax.ShapeDtypeStruct((B, DIFF_HEADS, S, 2 * HEAD_DIM), BF16),
        jax.ShapeDtypeStruct((B, DIFF_HEADS, nkv, DIFF_V_DIM, tk), BF16),
        jax.ShapeDtypeStruct((B, S, SWA_Q_COLS), BF16),
        jax.ShapeDtypeStruct((B, S, SWA_KV_COLS), BF16),
        jax.ShapeDtypeStruct((B, S, SWA_KV_COLS), BF16),
    )
    return pl.pallas_call(
        functools.partial(_proj_kernel, tk=tk),
        grid=grid,
        in_specs=[
            pl.BlockSpec((1, tm, D), lambda b, i: (b, i, 0)),
            pl.BlockSpec((1, D), const),
            pl.BlockSpec(w_nat.shape, const),
            pl.BlockSpec(w_tr.shape, const),
            pl.BlockSpec((tm, LANES), lambda b, i: (i, 0)),
            pl.BlockSpec((tm, LANES), lambda b, i: (i, 0)),
            pl.BlockSpec((HEAD_DIM // 2, tm), lambda b, i: (0, i)),
            pl.BlockSpec((HEAD_DIM // 2, tm), lambda b, i: (0, i)),
        ],
        out_specs=(
            pl.BlockSpec((1, DIFF_HEADS, 2 * HEAD_DIM, tm), lambda b, i: (b, 0, 0, i)),
            pl.BlockSpec((1, DIFF_HEADS, tm, 2 * HEAD_DIM), lambda b, i: (b, 0, i, 0)),
            pl.BlockSpec((1, DIFF_HEADS, tm // tk, DIFF_V_DIM, tk), lambda b, i: (b, 0, i, 0, 0)),
            pl.BlockSpec((1, tm, SWA_Q_COLS), lambda b, i: (b, i, 0)),
            pl.BlockSpec((1, tm, SWA_KV_COLS), lambda b, i: (b, i, 0)),
            pl.BlockSpec((1, tm, SWA_KV_COLS), lambda b, i: (b, i, 0)),
        ),
        out_shape=out_shape,
        compiler_params=pltpu.CompilerParams(
            dimension_semantics=("parallel", "parallel"), vmem_limit_bytes=48 * 1024 * 1024),
        name="proj_rope",
    )(x, g1, w_nat, w_tr, cos_l, sin_l, cos_t, sin_t)


def _diff_kernel(lam_ref, qt_ref, k_ref, vt_ref, g_ref, o_ref, s0_ref, s1_ref, m_ref, l_ref, acc_ref,
                 *, tq, tk, lambda_init):
    i = pl.program_id(2)
    qt = qt_ref[0, 0]
    z = jnp.zeros((HEAD_DIM, tq), BF16)
    qw = jnp.concatenate([jnp.concatenate([qt[:HEAD_DIM], z], axis=1),
                          jnp.concatenate([z, qt[HEAD_DIM:]], axis=1)], axis=0)

    def scores(j, s_ref):
        kt = k_ref[0, 0, pl.ds(pl.multiple_of(j * tk, tk), tk), :]
        s_ref[...] = jnp.dot(kt, qw, preferred_element_type=F32)

    def absorb(j, s_ref, masked):
        s = s_ref[...]
        if masked:
            kpos = j * tk + lax.broadcasted_iota(jnp.int32, (tk, 2 * tq), 0)
            qpos = i * tq + (lax.broadcasted_iota(jnp.int32, (tk, 2 * tq), 1) & (tq - 1))
            s = jnp.where(kpos <= qpos, s, NEG)
        m = m_ref[...]
        m_new = jnp.maximum(m, jnp.max(s, axis=0, keepdims=True))
        alpha = jnp.exp2(m - m_new)
        p = jnp.exp2(s - m_new)
        m_ref[...] = m_new
        l_ref[...] = alpha * l_ref[...] + jnp.sum(p, axis=0, keepdims=True)
        pv = jnp.dot(vt_ref[0, 0, j], p.astype(BF16), preferred_element_type=F32)
        acc_ref[...] = alpha * acc_ref[...] + pv

    m_ref[...] = jnp.full(m_ref.shape, NEG, F32)
    l_ref[...] = jnp.zeros(l_ref.shape, F32)
    acc_ref[...] = jnp.zeros(acc_ref.shape, F32)

    nfull = (i * tq) // tk
    scores(0, s0_ref)

    def pair(t, c):
        j0 = 2 * t
        scores(j0 + 1, s1_ref)
        absorb(j0, s0_ref, False)
        scores(j0 + 2, s0_ref)
        absorb(j0 + 1, s1_ref, False)
        return c

    lax.fori_loop(0, nfull // 2, pair, 0)

    @pl.when(nfull % 2 == 0)
    def _():
        absorb(nfull, s0_ref, True)

    @pl.when(nfull % 2 == 1)
    def _():
        scores(nfull, s1_ref)
        absorb(nfull - 1, s0_ref, False)
        absorb(nfull, s1_ref, True)

    lam_p = lam_ref[...]
    lam = (jnp.exp(jnp.sum(lam_p[0:1] * lam_p[1:2], axis=-1, keepdims=True))
           - jnp.exp(jnp.sum(lam_p[2:3] * lam_p[3:4], axis=-1, keepdims=True)) + lambda_init)
    l = l_ref[...]
    o = acc_ref[:, :tq] / l[:, :tq] - lam * (acc_ref[:, tq:] / l[:, tq:])
    o = o * lax.rsqrt(jnp.mean(o * o, axis=0, keepdims=True) + EPS)
    o_ref[0] = (o.T * g_ref[...] * (1.0 - lambda_init)).astype(BF16)


def _diff_call(lam_p, dqt, dk, dvt, subln_g, *, tq, tk, lambda_init):
    B, H, _, S = dqt.shape
    nkv = S // tk
    grid = (B, H, S // tq)
    return pl.pallas_call(
        functools.partial(_diff_kernel, tq=tq, tk=tk, lambda_init=lambda_init),
        grid=grid,
        in_specs=[
            pl.BlockSpec(lam_p.shape, lambda b, h, i: (0, 0)),
            pl.BlockSpec((1, 1, 2 * HEAD_DIM, tq), lambda b, h, i: (b, h, 0, i)),
            pl.BlockSpec((1, 1, S, 2 * HEAD_DIM), lambda b, h, i: (b, h, 0, 0)),
            pl.BlockSpec((1, 1, nkv, DIFF_V_DIM, tk), lambda b, h, i: (b, h, 0, 0, 0)),
            pl.BlockSpec((1, DIFF_V_DIM), lambda b, h, i: (0, 0)),
        ],
        out_specs=pl.BlockSpec((1, tq, DIFF_V_DIM), lambda b, h, i: (b, i, h)),
        out_shape=jax.ShapeDtypeStruct((B, S, DIFF_V_COLS), BF16),
        scratch_shapes=[
            pltpu.VMEM((tk, 2 * tq), F32),
            pltpu.VMEM((tk, 2 * tq), F32),
            pltpu.VMEM((1, 2 * tq), F32),
            pltpu.VMEM((1, 2 * tq), F32),
            pltpu.VMEM((DIFF_V_DIM, 2 * tq), F32),
        ],
        compiler_params=pltpu.CompilerParams(
            dimension_semantics=("parallel", "parallel", "arbitrary"),
            vmem_limit_bytes=48 * 1024 * 1024),
        name="diff_attn",
    )(lam_p, dqt, dk, dvt, subln_g)


def _swa_kernel(sinks_ref, q_ref, k_ref, v_ref, o_ref, *, tq):
    i = pl.program_id(1)
    q0 = i * tq
    kw_len = tq + WINDOW
    ks = pl.multiple_of(jnp.maximum(q0 - WINDOW, 0), WINDOW)
    kw = k_ref[0, pl.ds(ks, kw_len), :]
    vw = v_ref[0, pl.ds(ks, kw_len), :]
    qpos = q0 + lax.broadcasted_iota(jnp.int32, (tq, kw_len), 0)
    kpos = ks + lax.broadcasted_iota(jnp.int32, (tq, kw_len), 1)
    valid = (kpos <= qpos) & (kpos > qpos - WINDOW)
    lo = lax.broadcasted_iota(jnp.int32, (tq, LANES), 1) < HEAD_DIM
    for g in range(SWA_GROUP):
        qg = q_ref[0, :, g * LANES:(g + 1) * LANES]
        outs = []
        for kv in range(SWA_KV_HEADS):
            qm = jnp.where(lo if kv == 0 else jnp.logical_not(lo), qg, jnp.zeros_like(qg))
            s = lax.dot_general(qm, kw, (((1,), (1,)), ((), ())), preferred_element_type=F32)
            s = jnp.where(valid, s, NEG)
            sink = sinks_ref[kv * SWA_GROUP + g]
            m = jnp.maximum(jnp.max(s, axis=-1, keepdims=True), sink)
            p = jnp.exp(s - m)
            den = jnp.sum(p, axis=-1, keepdims=True) + jnp.exp(sink - m)
            outs.append(jnp.dot(p.astype(BF16), vw, preferred_element_type=F32) / den)
        o_ref[0, :, g * LANES:(g + 1) * LANES] = jnp.where(lo, outs[0], outs[1]).astype(BF16)


def _swa_call(sinks, sq, sk, sv, *, tq):
    B, S, _ = sq.shape
    grid_spec = pltpu.PrefetchScalarGridSpec(
        num_scalar_prefetch=1,
        grid=(B, S // tq),
        in_specs=[
            pl.BlockSpec((1, tq, SWA_Q_COLS), lambda b, i, s: (b, i, 0)),
            pl.BlockSpec((1, S, SWA_KV_COLS), lambda b, i, s: (b, 0, 0)),
            pl.BlockSpec((1, S, SWA_KV_COLS), lambda b, i, s: (b, 0, 0)),
        ],
        out_specs=pl.BlockSpec((1, tq, SWA_Q_COLS), lambda b, i, s: (b, i, 0)),
    )
    return pl.pallas_call(
        functools.partial(_swa_kernel, tq=tq),
        grid_spec=grid_spec,
        out_shape=jax.ShapeDtypeStruct((B, S, SWA_Q_COLS), BF16),
        compiler_params=pltpu.CompilerParams(
            dimension_semantics=("parallel", "arbitrary"), vmem_limit_bytes=40 * 1024 * 1024),
        name="swa_attn",
    )(sinks, sq, sk, sv)


def _mix_kernel(x_ref, od_ref, os_ref, wo_ref, g2_ref, wr_ref, br_ref, x1_ref, n2_ref, rt_ref):
    h = (x_ref[0]
         + jnp.dot(od_ref[0], wo_ref[:DIFF_V_COLS], preferred_element_type=F32)
         + jnp.dot(os_ref[0], wo_ref[DIFF_V_COLS:], preferred_element_type=F32))
    x1_ref[0] = h
    n2 = h * lax.rsqrt(jnp.mean(h * h, axis=-1, keepdims=True) + EPS) * g2_ref[...]
    n2_ref[0] = n2
    logits = jnp.dot(n2, wr_ref[...], preferred_element_type=F32,
                     precision=lax.Precision.HIGHEST) + br_ref[...]
    tm = logits.shape[0]
    lane = lax.broadcasted_iota(jnp.int32, (tm, ROUTER_COLS), 1)
    big = jnp.int32(ROUTER_COLS)
    gl = jnp.where(lane < N_GROUPS, logits, -jnp.inf)
    gm = jnp.max(gl, axis=-1, keepdims=True)
    p_top = 1.0 / jnp.sum(jnp.exp(gl - gm), axis=-1, keepdims=True)
    g_idx = jnp.min(jnp.where(gl == gm, lane, big), axis=-1, keepdims=True)
    e_lo = N_GROUPS + EXPERTS_PER_GROUP * g_idx
    el = jnp.where((lane >= e_lo) & (lane < e_lo + EXPERTS_PER_GROUP), logits, -jnp.inf)
    v1 = jnp.max(el, axis=-1, keepdims=True)
    i1 = jnp.min(jnp.where(el == v1, lane, big), axis=-1, keepdims=True)
    el2 = jnp.where(lane == i1, -jnp.inf, el)
    v2 = jnp.max(el2, axis=-1, keepdims=True)
    i2 = jnp.min(jnp.where(el2 == v2, lane, big), axis=-1, keepdims=True)
    e21 = jnp.exp(v2 - v1)
    gate1 = p_top / (1.0 + e21)
    gate2 = p_top * e21 / (1.0 + e21)
    rt = jnp.where(lane == 0, (i1 - N_GROUPS).astype(F32),
         jnp.where(lane == 1, (i2 - N_GROUPS).astype(F32),
         jnp.where(lane == 2, gate1, jnp.where(lane == 3, gate2, 0.0))))
    rt_ref[0] = rt


def _mix_call(x, o_diff, o_swa, w_out, g2, w_router, b_router, *, tm):
    B, S, D = x.shape
    const = lambda b, i: (0, 0)
    row = lambda b, i: (b, i, 0)
    return pl.pallas_call(
        _mix_kernel,
        grid=(B, S // tm),
        in_specs=[
            pl.BlockSpec((1, tm, D), row),
            pl.BlockSpec((1, tm, DIFF_V_COLS), row),
            pl.BlockSpec((1, tm, SWA_Q_COLS), row),
            pl.BlockSpec(w_out.shape, const),
            pl.BlockSpec((1, D), const),
            pl.BlockSpec(w_router.shape, const),
            pl.BlockSpec((1, ROUTER_COLS), const),
        ],
        out_specs=(pl.BlockSpec((1, tm, D), row), pl.BlockSpec((1, tm, D), row),
                   pl.BlockSpec((1, tm, ROUTER_COLS), row)),
        out_shape=(jax.ShapeDtypeStruct((B, S, D), F32), jax.ShapeDtypeStruct((B, S, D), F32),
                   jax.ShapeDtypeStruct((B, S, ROUTER_COLS), F32)),
        compiler_params=pltpu.CompilerParams(
            dimension_semantics=("parallel", "parallel"), vmem_limit_bytes=48 * 1024 * 1024),
        name="outproj_router",
    )(x, o_diff, o_swa, w_out, g2, w_router, b_router)


def _row_gather(idx_hbm_row, idx_smem, src_hbm, dst_vmem, idx_sem, row_sem, n_rows):
    idx_copy = pltpu.make_async_copy(idx_hbm_row, idx_smem, idx_sem)
    idx_copy.start()
    idx_copy.wait()

    def issue(r, c):
        pltpu.make_async_copy(src_hbm.at[pl.ds(idx_smem[r], 1)], dst_vmem.at[pl.ds(r, 1)], row_sem).start()
        return c

    lax.fori_loop(0, n_rows, issue, 0)
    pltpu.make_async_copy(src_hbm.at[pl.ds(0, n_rows)], dst_vmem, row_sem).wait()


def _expert_kernel(be_ref, nused_ref, tok_hbm, n2_hbm, wg_ref, wu_ref, wd_ref, y_ref,
                   tok_smem, xbuf, idx_sem, row_sem):
    b = pl.program_id(0)

    @pl.when(b < nused_ref[0])
    def _():
        _row_gather(tok_hbm.at[b], tok_smem, n2_hbm, xbuf, idx_sem, row_sem, EXPERT_BLOCK)
        xb = xbuf[...].astype(BF16)
        gate = jnp.dot(xb, wg_ref[0], preferred_element_type=F32)
        up = jnp.dot(xb, wu_ref[0], preferred_element_type=F32)
        hid = (gate * jax.nn.sigmoid(gate) * up).astype(BF16)
        y_ref[...] = jnp.dot(hid, wd_ref[0], preferred_element_type=F32)

    @pl.when(b >= nused_ref[0])
    def _():
        y_ref[...] = jnp.zeros_like(y_ref)


def _expert_call(block_expert, n_used, slot_tok, n2, w_gate, w_up, w_down):
    T, D = n2.shape
    NB = slot_tok.shape[0]
    E, _, F = w_gate.shape
    grid_spec = pltpu.PrefetchScalarGridSpec(
        num_scalar_prefetch=2,
        grid=(NB,),
        in_specs=[
            pl.BlockSpec(memory_space=pl.ANY),
            pl.BlockSpec(memory_space=pl.ANY),
            pl.BlockSpec((1, D, F), lambda b, be, nu: (be[b], 0, 0)),
            pl.BlockSpec((1, D, F), lambda b, be, nu: (be[b], 0, 0)),
            pl.BlockSpec((1, F, D), lambda b, be, nu: (be[b], 0, 0)),
        ],
        out_specs=pl.BlockSpec((EXPERT_BLOCK, D), lambda b, be, nu: (b, 0)),
        scratch_shapes=[
            pltpu.SMEM((EXPERT_BLOCK,), jnp.int32),
            pltpu.VMEM((EXPERT_BLOCK, D), F32),
            pltpu.SemaphoreType.DMA(()),
            pltpu.SemaphoreType.DMA(()),
        ],
    )
    return pl.pallas_call(
        _expert_kernel,
        grid_spec=grid_spec,
        out_shape=jax.ShapeDtypeStruct((NB * EXPERT_BLOCK, D), F32),
        compiler_params=pltpu.CompilerParams(
            dimension_semantics=("arbitrary",), vmem_limit_bytes=40 * 1024 * 1024),
        name="moe_experts",
    )(block_expert, n_used, slot_tok, n2, w_gate, w_up, w_down)


def _combine_kernel(d0_hbm, d1_hbm, x1_ref, rt_ref, ys_hbm, fg_ref, o_ref,
                    d_smem, y0, y1, idx_sem, row_sem, *, tm, final_norm):
    t = pl.program_id(0)
    _row_gather(d0_hbm.at[t], d_smem, ys_hbm, y0, idx_sem, row_sem, tm)
    _row_gather(d1_hbm.at[t], d_smem, ys_hbm, y1, idx_sem, row_sem, tm)
    rt = rt_ref[...]
    h = x1_ref[...] + rt[:, 2:3] * y0[...] + rt[:, 3:4] * y1[...]
    if final_norm:
        h = h * lax.rsqrt(jnp.mean(h * h, axis=-1, keepdims=True) + EPS) * fg_ref[...]
    o_ref[...] = h


def _combine_call(dest0, dest1, x1, rt, ys, final_g, *, tm, final_norm):
    T, D = x1.shape
    return pl.pallas_call(
        functools.partial(_combine_kernel, tm=tm, final_norm=final_norm),
        grid=(T // tm,),
        in_specs=[
            pl.BlockSpec(memory_space=pl.ANY),
            pl.BlockSpec(memory_space=pl.ANY),
            pl.BlockSpec((tm, D), lambda t: (t, 0)),
            pl.BlockSpec((tm, ROUTER_COLS), lambda t: (t, 0)),
            pl.BlockSpec(memory_space=pl.ANY),
            pl.BlockSpec((1, D), lambda t: (0, 0)),
        ],
        out_specs=pl.BlockSpec((tm, D), lambda t: (t, 0)),
        out_shape=jax.ShapeDtypeStruct((T, D), F32),
        scratch_shapes=[
            pltpu.SMEM((tm,), jnp.int32),
            pltpu.VMEM((tm, D), F32),
            pltpu.VMEM((tm, D), F32),
            pltpu.SemaphoreType.DMA(()),
            pltpu.SemaphoreType.DMA(()),
        ],
        compiler_params=pltpu.CompilerParams(
            dimension_semantics=("arbitrary",), vmem_limit_bytes=40 * 1024 * 1024),
        name="moe_combine",
    )(dest0, dest1, x1, rt, ys, final_g)


def _dispatch_plan(expert_id, T):
    A = T * TOP_K
    NB = -(-A // EXPERT_BLOCK) + N_EXPERTS
    P = NB * EXPERT_BLOCK
    e_flat = expert_id.reshape(A)
    order = jnp.argsort(e_flat).astype(jnp.int32)
    counts = jnp.sum(e_flat[:, None] == jnp.arange(N_EXPERTS, dtype=jnp.int32)[None, :],
                     axis=0, dtype=jnp.int32)
    start = jnp.cumsum(counts) - counts
    padded = ((counts + EXPERT_BLOCK - 1) // EXPERT_BLOCK) * EXPERT_BLOCK
    pad_end = jnp.cumsum(padded)
    pad_start = pad_end - padded
    block_start = jnp.arange(NB, dtype=jnp.int32) * EXPERT_BLOCK
    block_expert = jnp.minimum(jnp.searchsorted(pad_end, block_start, side='right'),
                               N_EXPERTS - 1).astype(jnp.int32)
    n_used = (pad_end[-1] // EXPERT_BLOCK).astype(jnp.int32).reshape(1)
    slot = jnp.arange(P, dtype=jnp.int32)
    se = jnp.repeat(block_expert, EXPERT_BLOCK)
    r = slot - pad_start[se]
    is_real = r < counts[se]
    src = jnp.clip(start[se] + r, 0, A - 1)
    slot_tok = jnp.where(is_real, order[src] // TOP_K, 0).astype(jnp.int32)
    e_s = e_flat[order]
    dest_sorted = pad_start[e_s] + (jnp.arange(A, dtype=jnp.int32) - start[e_s])
    dest = jnp.zeros((A,), jnp.int32).at[order].set(dest_sorted, unique_indices=True)
    return block_expert, n_used, slot_tok.reshape(NB, EXPERT_BLOCK), dest.reshape(T, TOP_K)


def _rope_tables(S):
    inv = 1.0 / (ROPE_THETA ** (jnp.arange(0, HEAD_DIM, 2, dtype=F32) / HEAD_DIM))
    ang = jnp.arange(S, dtype=F32)[:, None] * inv[None, :]
    cos, sin = jnp.cos(ang), jnp.sin(ang)
    cos_l = jnp.tile(cos, (1, LANES // (HEAD_DIM // 2)))
    sin_l = jnp.tile(jnp.concatenate([-sin, sin], axis=1), (1, LANES // HEAD_DIM))
    return cos_l, sin_l, cos.T, sin.T


def _swa_pair_perm():
    idx = []
    for g in range(SWA_GROUP):
        for kv in range(SWA_KV_HEADS):
            h = kv * SWA_GROUP + g
            idx.extend(range(h * HEAD_DIM, (h + 1) * HEAD_DIM))
    return jnp.asarray(idx, dtype=jnp.int32)


def kernel(x, norm1_g, w_in, lambda_q1, lambda_k1, lambda_q2, lambda_k2, subln_g, sinks, w_out,
           norm2_g, w_router_group, b_router_group, w_router_expert, b_router_expert,
           w_gate, w_up, w_down, final_g):
    B, S, D = x.shape
    T = B * S
    depth = w_in.shape[0]
    tq, tk = 256, 512
    tm_proj = 512
    tm_mix = 512
    tq_swa = 256
    tm_comb = 256
    scale = HEAD_DIM ** -0.5
    cos_l, sin_l, cos_t, sin_t = _rope_tables(S)
    perm = _swa_pair_perm()

    c0 = DIFF_QK_COLS
    c1 = 2 * DIFF_QK_COLS
    c2 = c1 + DIFF_V_COLS
    c3 = c2 + SWA_Q_COLS
    c4 = c3 + SWA_KV_COLS
    for l in range(depth):
        lambda_init = 0.8 - 0.6 * math.exp(-0.3 * l)
        w = w_in[l]
        w_sq = (w[:, c2:c3] * scale)[:, perm]
        w_nat = jnp.concatenate([w[:, c0:c1], w_sq, w[:, c3:c4], w[:, c4:]], axis=1).astype(BF16)
        w_tr = jnp.concatenate([w[:, :c0] * (scale * math.log2(math.e)), w[:, c1:c2]], axis=1).T.astype(BF16)
        dqt, dk, dvt, sq, sk, sv = _proj_call(
            x, norm1_g[l][None, :], w_nat, w_tr, cos_l, sin_l, cos_t, sin_t, tm=tm_proj, tk=tk)

        lam_p = jnp.stack([lambda_q1[l], lambda_k1[l], lambda_q2[l], lambda_k2[l]]).astype(F32)
        o_diff = _diff_call(lam_p, dqt, dk, dvt, subln_g[l][None, :].astype(F32),
                            tq=tq, tk=tk, lambda_init=lambda_init)
        sinks_paired = sinks[l].astype(F32)
        o_swa = _swa_call(sinks_paired, sq, sk, sv, tq=tq_swa)

        wo = w_out[l]
        wo_b = jnp.concatenate([wo[:DIFF_V_COLS], wo[DIFF_V_COLS:][perm]], axis=0).astype(BF16)
        w_router = jnp.zeros((D, ROUTER_COLS), F32)
        w_router = w_router.at[:, :N_GROUPS].set(w_router_group[l])
        w_router = w_router.at[:, N_GROUPS:N_GROUPS + N_EXPERTS].set(w_router_expert[l])
        b_router = jnp.zeros((1, ROUTER_COLS), F32)
        b_router = b_router.at[0, :N_GROUPS].set(b_router_group[l])
        b_router = b_router.at[0, N_GROUPS:N_GROUPS + N_EXPERTS].set(b_router_expert[l])
        x1, n2, rt = _mix_call(x, o_diff, o_swa, wo_b, norm2_g[l][None, :], w_router, b_router, tm=tm_mix)

        rt2 = rt.reshape(T, ROUTER_COLS)
        expert_id = rt2[:, :TOP_K].astype(jnp.int32)
        block_expert, n_used, slot_tok, dest = _dispatch_plan(expert_id, T)
        ys = _expert_call(block_expert, n_used, slot_tok, n2.reshape(T, D),
                          w_gate[l].astype(BF16), w_up[l].astype(BF16), w_down[l].astype(BF16))
        x = _combine_call(dest[:, 0].reshape(T // tm_comb, tm_comb), dest[:, 1].reshape(T // tm_comb, tm_comb),
                          x1.reshape(T, D), rt2, ys, final_g[None, :],
                          tm=tm_comb, final_norm=(l == depth - 1)).reshape(B, S, D)
    return x
```

```python
import functools
import math

import jax
import jax.numpy as jnp
from jax import lax
from jax.experimental import pallas as pl
from jax.experimental.pallas import tpu as pltpu

HEAD_DIM = 64
DIFF_HEADS = 4
DIFF_V_DIM = 2 * HEAD_DIM
SWA_Q_HEADS = 8
SWA_KV_HEADS = 2
SWA_GROUP = SWA_Q_HEADS // SWA_KV_HEADS
WINDOW = 128
ROPE_THETA = 10000.0
N_GROUPS = 4
EXPERTS_PER_GROUP = 8
N_EXPERTS = N_GROUPS * EXPERTS_PER_GROUP
TOP_K = 2
EXPERT_BLOCK = 256
EPS = 1e-6
NEG = -1e30

DIFF_QK_COLS = DIFF_HEADS * 2 * HEAD_DIM
DIFF_V_COLS = DIFF_HEADS * DIFF_V_DIM
SWA_Q_COLS = SWA_Q_HEADS * HEAD_DIM
SWA_KV_COLS = SWA_KV_HEADS * HEAD_DIM
LANES = 128
SUBLANES = 8
ROUTER_COLS = LANES

BF16 = jnp.bfloat16
F32 = jnp.float32


def _rope_lanes(x, cos_l, sin_l, first_half):
    rot = jnp.where(first_half, pltpu.roll(x, 96, 1), pltpu.roll(x, 32, 1))
    return x * cos_l + rot * sin_l


def _proj_kernel(x_ref, g_ref, wnat_ref, wtr_ref, cosl_ref, sinl_ref, cost_ref, sint_ref,
                 dqt_ref, dk_ref, dvt_ref, sq_ref, sk_ref, sv_ref, *, tk):
    x = x_ref[0]
    tm = x.shape[0]
    n1 = x * lax.rsqrt(jnp.mean(x * x, axis=-1, keepdims=True) + EPS) * g_ref[...]
    n1b = n1.astype(BF16)
    nat = jnp.dot(n1b, wnat_ref[...], preferred_element_type=F32)
    tr = lax.dot_general(wtr_ref[...], n1b, (((1,), (1,)), ((), ())),
                         preferred_element_type=F32)

    cos_l, sin_l = cosl_ref[...], sinl_ref[...]
    first_half = (lax.broadcasted_iota(jnp.int32, (tm, LANES), 1) & (HEAD_DIM - 1)) < HEAD_DIM // 2
    for h in range(DIFF_HEADS):
        slab = nat[:, h * LANES:(h + 1) * LANES]
        dk_ref[0, h] = _rope_lanes(slab, cos_l, sin_l, first_half).astype(BF16)
    for c in range(SWA_Q_COLS // LANES):
        lo = DIFF_QK_COLS + c * LANES
        sq_ref[0, :, c * LANES:(c + 1) * LANES] = _rope_lanes(
            nat[:, lo:lo + LANES], cos_l, sin_l, first_half).astype(BF16)
    lo = DIFF_QK_COLS + SWA_Q_COLS
    sk_ref[0] = _rope_lanes(nat[:, lo:lo + LANES], cos_l, sin_l, first_half).astype(BF16)
    sv_ref[0] = nat[:, lo + LANES:lo + 2 * LANES].astype(BF16)

    cos_t, sin_t = cost_ref[...], sint_ref[...]
    half = HEAD_DIM // 2
    for h in range(DIFF_HEADS):
        for c in range(2):
            r0 = h * 2 * HEAD_DIM + c * HEAD_DIM
            x1 = tr[r0:r0 + half]
            x2 = tr[r0 + half:r0 + HEAD_DIM]
            dqt_ref[0, h, c * HEAD_DIM:c * HEAD_DIM + half] = (x1 * cos_t - x2 * sin_t).astype(BF16)
            dqt_ref[0, h, c * HEAD_DIM + half:(c + 1) * HEAD_DIM] = (x1 * sin_t + x2 * cos_t).astype(BF16)
    for h in range(DIFF_HEADS):
        r0 = DIFF_QK_COLS + h * DIFF_V_DIM
        for c in range(tm // tk):
            dvt_ref[0, h, c] = tr[r0:r0 + DIFF_V_DIM, c * tk:(c + 1) * tk].astype(BF16)


def _proj_call(x, g1, w_nat, w_tr, cos_l, sin_l, cos_t, sin_t, *, tm, tk):
    B, S, D = x.shape
    nkv = S // tk
    grid = (B, S // tm)
    const = lambda b, i: (0, 0)
    out_shape = (
        jax.ShapeDtypeStruct((B, DIFF_HEADS, 2 * HEAD_DIM, S), BF16),
        jax.ShapeDtypeStruct((B, DIFF_HEADS, S, 2 * HEAD_DIM), BF16),
        jax.ShapeDtypeStruct((B, DIFF_HEADS, nkv, DIFF_V_DIM, tk), BF16),
        jax.ShapeDtypeStruct((B, S, SWA_Q_COLS), BF16),
        jax.ShapeDtypeStruct((B, S, SWA_KV_COLS), BF16),
        jax.ShapeDtypeStruct((B, S, SWA_KV_COLS), BF16),
    )
    return pl.pallas_call(
        functools.partial(_proj_kernel, tk=tk),
        grid=grid,
        in_specs=[
            pl.BlockSpec((1, tm, D), lambda b, i: (b, i, 0)),
            pl.BlockSpec((1, D), const),
            pl.BlockSpec(w_nat.shape, const),
            pl.BlockSpec(w_tr.shape, const),
            pl.BlockSpec((tm, LANES), lambda b, i: (i, 0)),
            pl.BlockSpec((tm, LANES), lambda b, i: (i, 0)),
            pl.BlockSpec((HEAD_DIM // 2, tm), lambda b, i: (0, i)),
            pl.BlockSpec((HEAD_DIM // 2, tm), lambda b, i: (0, i)),
        ],
        out_specs=(
            pl.BlockSpec((1, DIFF_HEADS, 2 * HEAD_DIM, tm), lambda b, i: (b, 0, 0, i)),
            pl.BlockSpec((1, DIFF_HEADS, tm, 2 * HEAD_DIM), lambda b, i: (b, 0, i, 0)),
            pl.BlockSpec((1, DIFF_HEADS, tm // tk, DIFF_V_DIM, tk), lambda b, i: (b, 0, i, 0, 0)),
            pl.BlockSpec((1, tm, SWA_Q_COLS), lambda b, i: (b, i, 0)),
            pl.BlockSpec((1, tm, SWA_KV_COLS), lambda b, i: (b, i, 0)),
            pl.BlockSpec((1, tm, SWA_KV_COLS), lambda b, i: (b, i, 0)),
        ),
        out_shape=out_shape,
        compiler_params=pltpu.CompilerParams(
            dimension_semantics=("parallel", "parallel"), vmem_limit_bytes=48 * 1024 * 1024),
        name="proj_rope",
    )(x, g1, w_nat, w_tr, cos_l, sin_l, cos_t, sin_t)


def _diff_kernel(lam_ref, qt_ref, k_ref, vt_ref, g_ref, o_ref, s0_ref, s1_ref, m_ref, l_ref, acc_ref,
                 *, tq, tk, lambda_init):
    i = pl.program_id(2)
    qt = qt_ref[0, 0]
    z = jnp.zeros((HEAD_DIM, tq), BF16)
    qw = jnp.concatenate([jnp.concatenate([qt[:HEAD_DIM], z], axis=1),
                          jnp.concatenate([z, qt[HEAD_DIM:]], axis=1)], axis=0)

    def scores(j, s_ref):
        kt = k_ref[0, 0, pl.ds(pl.multiple_of(j * tk, tk), tk), :]
        s_ref[...] = jnp.dot(kt, qw, preferred_element_type=F32)

    def absorb(j, s_ref, masked):
        s = s_ref[...]
        if masked:
            kpos = j * tk + lax.broadcasted_iota(jnp.int32, (tk, 2 * tq), 0)
            qpos = i * tq + (lax.broadcasted_iota(jnp.int32, (tk, 2 * tq), 1) & (tq - 1))
            s = jnp.where(kpos <= qpos, s, NEG)
        m = m_ref[...]
        m_new = jnp.maximum(m, jnp.max(s, axis=0, keepdims=True))
        alpha = jnp.exp2(m - m_new)
        p = jnp.exp2(s - m_new)
        m_ref[...] = m_new
        l_ref[...] = alpha * l_ref[...] + jnp.sum(p, axis=0, keepdims=True)
        pv = jnp.dot(vt_ref[0, 0, j], p.astype(BF16), preferred_element_type=F32)
        acc_ref[...] = alpha * acc_ref[...] + pv

    m_ref[...] = jnp.full(m_ref.shape, NEG, F32)
    l_ref[...] = jnp.zeros(l_ref.shape, F32)
    acc_ref[...] = jnp.zeros(acc_ref.shape, F32)

    nfull = (i * tq) // tk
    scores(0, s0_ref)

    def pair(t, c):
        j0 = 2 * t
        scores(j0 + 1, s1_ref)
        absorb(j0, s0_ref, False)
        scores(j0 + 2, s0_ref)
        absorb(j0 + 1, s1_ref, False)
        return c

    lax.fori_loop(0, nfull // 2, pair, 0)

    @pl.when(nfull % 2 == 0)
    def _():
        absorb(nfull, s0_ref, True)

    @pl.when(nfull % 2 == 1)
    def _():
        scores(nfull, s1_ref)
        absorb(nfull - 1, s0_ref, False)
        absorb(nfull, s1_ref, True)

    lam_p = lam_ref[...]
    lam = (jnp.exp(jnp.sum(lam_p[0:1] * lam_p[1:2], axis=-1, keepdims=True))
           - jnp.exp(jnp.sum(lam_p[2:3] * lam_p[3:4], axis=-1, keepdims=True)) + lambda_init)
    l = l_ref[...]
    o = acc_ref[:, :tq] / l[:, :tq] - lam * (acc_ref[:, tq:] / l[:, tq:])
    o = o * lax.rsqrt(jnp.mean(o * o, axis=0, keepdims=True) + EPS)
    o_ref[0] = (o.T * g_ref[...] * (1.0 - lambda_init)).astype(BF16)


def _diff_call(lam_p, dqt, dk, dvt, subln_g, *, tq, tk, lambda_init):
    B, H, _, S = dqt.shape
    nkv = S // tk
    grid = (B, H, S // tq)
    return pl.pallas_call(
        functools.partial(_diff_kernel, tq=tq, tk=tk, lambda_init=lambda_init),
        grid=grid,
        in_specs=[
            pl.BlockSpec(lam_p.shape, lambda b, h, i: (0, 0)),
            pl.BlockSpec((1, 1, 2 * HEAD_DIM, tq), lambda b, h, i: (b, h, 0, i)),
            pl.BlockSpec((1, 1, S, 2 * HEAD_DIM), lambda b, h, i: (b, h, 0, 0)),
            pl.BlockSpec((1, 1, nkv, DIFF_V_DIM, tk), lambda b, h, i: (b, h, 0, 0, 0)),
            pl.BlockSpec((1, DIFF_V_DIM), lambda b, h, i: (0, 0)),
        ],
        out_specs=pl.BlockSpec((1, tq, DIFF_V_DIM), lambda b, h, i: (b, i, h)),
        out_shape=jax.ShapeDtypeStruct((B, S, DIFF_V_COLS), BF16),
        scratch_shapes=[
            pltpu.VMEM((tk, 2 * tq), F32),
            pltpu.VMEM((tk, 2 * tq), F32),
            pltpu.VMEM((1, 2 * tq), F32),
            pltpu.VMEM((1, 2 * tq), F32),
            pltpu.VMEM((DIFF_V_DIM, 2 * tq), F32),
        ],
        compiler_params=pltpu.CompilerParams(
            dimension_semantics=("parallel", "parallel", "arbitrary"),
            vmem_limit_bytes=48 * 1024 * 1024),
        name="diff_attn",
    )(lam_p, dqt, dk, dvt, subln_g)


def _swa_kernel(sinks_ref, q_ref, k_ref, v_ref, o_ref, *, tq):
    i = pl.program_id(1)
    q0 = i * tq
    kw_len = tq + WINDOW
    ks = pl.multiple_of(jnp.maximum(q0 - WINDOW, 0), WINDOW)
    kw = k_ref[0, pl.ds(ks, kw_len), :]
    vw = v_ref[0, pl.ds(ks, kw_len), :]
    qpos = q0 + lax.broadcasted_iota(jnp.int32, (tq, kw_len), 0)
    kpos = ks + lax.broadcasted_iota(jnp.int32, (tq, kw_len), 1)
    valid = (kpos <= qpos) & (kpos > qpos - WINDOW)
    lo = lax.broadcasted_iota(jnp.int32, (tq, LANES), 1) < HEAD_DIM
    for g in range(SWA_GROUP):
        qg = q_ref[0, :, g * LANES:(g + 1) * LANES]
        outs = []
        for kv in range(SWA_KV_HEADS):
            qm = jnp.where(lo if kv == 0 else jnp.logical_not(lo), qg, jnp.zeros_like(qg))
            s = lax.dot_general(qm, kw, (((1,), (1,)), ((), ())), preferred_element_type=F32)
            s = jnp.where(valid, s, NEG)
            sink = sinks_ref[kv * SWA_GROUP + g]
            m = jnp.maximum(jnp.max(s, axis=-1, keepdims=True), sink)
            p = jnp.exp(s - m)
            den = jnp.sum(p, axis=-1, keepdims=True) + jnp.exp(sink - m)
            outs.append(jnp.dot(p.astype(BF16), vw, preferred_element_type=F32) / den)
        o_ref[0, :, g * LANES:(g + 1) * LANES] = jnp.where(lo, outs[0], outs[1]).astype(BF16)


def _swa_call(sinks, sq, sk, sv, *, tq):
    B, S, _ = sq.shape
    grid_spec = pltpu.PrefetchScalarGridSpec(
        num_scalar_prefetch=1,
        grid=(B, S // tq),
        in_specs=[
            pl.BlockSpec((1, tq, SWA_Q_COLS), lambda b, i, s: (b, i, 0)),
            pl.BlockSpec((1, S, SWA_KV_COLS), lambda b, i, s: (b, 0, 0)),
            pl.BlockSpec((1, S, SWA_KV_COLS), lambda b, i, s: (b, 0, 0)),
        ],
        out_specs=pl.BlockSpec((1, tq, SWA_Q_COLS), lambda b, i, s: (b, i, 0)),
    )
    return pl.pallas_call(
        functools.partial(_swa_kernel, tq=tq),
        grid_spec=grid_spec,
        out_shape=jax.ShapeDtypeStruct((B, S, SWA_Q_COLS), BF16),
        compiler_params=pltpu.CompilerParams(
            dimension_semantics=("parallel", "arbitrary"), vmem_limit_bytes=40 * 1024 * 1024),
        name="swa_attn",
    )(sinks, sq, sk, sv)


def _pack_bf16_pairs(x):
    n = x.shape[1] // 2
    lo = lax.bitcast_convert_type(x[:, :n].astype(BF16).astype(F32), jnp.uint32)
    hi = lax.bitcast_convert_type(x[:, n:].astype(BF16).astype(F32), jnp.uint32)
    return (lo >> 16) | (hi & jnp.uint32(0xFFFF0000))


def _unpack_bf16_pairs(w):
    lo = lax.bitcast_convert_type(w << 16, F32)
    hi = lax.bitcast_convert_type(w & jnp.uint32(0xFFFF0000), F32)
    return jnp.concatenate([lo, hi], axis=1).astype(BF16)


def _mix_kernel(x_ref, od_ref, os_ref, wo_ref, g2_ref, wr_ref, br_ref, x1_ref, n2_ref, rt_ref, cnt_ref):
    h = (x_ref[0]
         + jnp.dot(od_ref[0], wo_ref[:DIFF_V_COLS], preferred_element_type=F32)
         + jnp.dot(os_ref[0], wo_ref[DIFF_V_COLS:], preferred_element_type=F32))
    x1_ref[0] = h
    n2 = h * lax.rsqrt(jnp.mean(h * h, axis=-1, keepdims=True) + EPS) * g2_ref[...]
    n2_ref[0] = _pack_bf16_pairs(n2)
    logits = jnp.dot(n2, wr_ref[...], preferred_element_type=F32,
                     precision=lax.Precision.HIGHEST) + br_ref[...]
    tm = logits.shape[0]
    lane = lax.broadcasted_iota(jnp.int32, (tm, ROUTER_COLS), 1)
    big = jnp.int32(ROUTER_COLS)
    gl = jnp.where(lane < N_GROUPS, logits, -jnp.inf)
    gm = jnp.max(gl, axis=-1, keepdims=True)
    p_top = 1.0 / jnp.sum(jnp.exp(gl - gm), axis=-1, keepdims=True)
    g_idx = jnp.min(jnp.where(gl == gm, lane, big), axis=-1, keepdims=True)
    e_lo = N_GROUPS + EXPERTS_PER_GROUP * g_idx
    el = jnp.where((lane >= e_lo) & (lane < e_lo + EXPERTS_PER_GROUP), logits, -jnp.inf)
    v1 = jnp.max(el, axis=-1, keepdims=True)
    i1 = jnp.min(jnp.where(el == v1, lane, big), axis=-1, keepdims=True)
    el2 = jnp.where(lane == i1, -jnp.inf, el)
    v2 = jnp.max(el2, axis=-1, keepdims=True)
    i2 = jnp.min(jnp.where(el2 == v2, lane, big), axis=-1, keepdims=True)
    e21 = jnp.exp(v2 - v1)
    gate1 = p_top / (1.0 + e21)
    gate2 = p_top * e21 / (1.0 + e21)
    rt = jnp.where(lane == 0, (i1 - N_GROUPS).astype(F32),
         jnp.where(lane == 1, (i2 - N_GROUPS).astype(F32),
         jnp.where(lane == 2, gate1, jnp.where(lane == 3, gate2, 0.0))))
    rt_ref[0] = rt
    chosen = ((lane == i1 - N_GROUPS) | (lane == i2 - N_GROUPS)).astype(F32)
    cnt_ref[0, 0] = jnp.broadcast_to(jnp.sum(chosen, axis=0, keepdims=True), cnt_ref.shape[2:])


def _mix_call(x, o_diff, o_swa, w_out, g2, w_router, b_router, *, tm):
    B, S, D = x.shape
    const = lambda b, i: (0, 0)
    row = lambda b, i: (b, i, 0)
    nt = S // tm
    return pl.pallas_call(
        _mix_kernel,
        grid=(B, nt),
        in_specs=[
            pl.BlockSpec((1, tm, D), row),
            pl.BlockSpec((1, tm, DIFF_V_COLS), row),
            pl.BlockSpec((1, tm, SWA_Q_COLS), row),
            pl.BlockSpec(w_out.shape, const),
            pl.BlockSpec((1, D), const),
            pl.BlockSpec(w_router.shape, const),
            pl.BlockSpec((1, ROUTER_COLS), const),
        ],
        out_specs=(pl.BlockSpec((1, tm, D), row), pl.BlockSpec((1, tm, D // 2), row),
                   pl.BlockSpec((1, tm, ROUTER_COLS), row),
                   pl.BlockSpec((1, 1, SUBLANES, ROUTER_COLS), lambda b, i: (b, i, 0, 0))),
        out_shape=(jax.ShapeDtypeStruct((B, S, D), F32), jax.ShapeDtypeStruct((B, S, D // 2), jnp.uint32),
                   jax.ShapeDtypeStruct((B, S, ROUTER_COLS), F32),
                   jax.ShapeDtypeStruct((B, nt, SUBLANES, ROUTER_COLS), F32)),
        compiler_params=pltpu.CompilerParams(
            dimension_semantics=("parallel", "parallel"), vmem_limit_bytes=48 * 1024 * 1024),
        name="outproj_router",
    )(x, o_diff, o_swa, w_out, g2, w_router, b_router)


ROW_UNROLL = 8


def _dispatch_kernel(rt_ref, base_ref, n2_ref, xs_in_hbm, dest_ref, xs_hbm, d_vmem, d_smem, idx_sem, row_sem,
                     *, tm):
    del xs_in_hbm
    rt_t = rt_ref[...].T
    e1 = rt_t[0:1].astype(jnp.int32)
    e2 = rt_t[1:2].astype(jnp.int32)
    eid = lax.broadcasted_iota(jnp.int32, (N_EXPERTS, tm), 0)
    oh1 = eid == e1
    oh2 = eid == e2
    earlier = (lax.broadcasted_iota(jnp.int32, (tm, tm), 0)
               < lax.broadcasted_iota(jnp.int32, (tm, tm), 1)).astype(BF16)
    before = jnp.dot((oh1 | oh2).astype(BF16), earlier, preferred_element_type=F32)
    slot = before + base_ref[0][:, 0:1]
    d1 = jnp.sum(jnp.where(oh1, slot, 0.0), axis=0, keepdims=True).astype(jnp.int32)
    d2 = jnp.sum(jnp.where(oh2, slot, 0.0), axis=0, keepdims=True).astype(jnp.int32)
    d = jnp.concatenate([d1, d2, jnp.zeros((SUBLANES - TOP_K, tm), jnp.int32)], axis=0)
    dest_ref[0] = d
    d_vmem[...] = d
    idx_copy = pltpu.make_async_copy(d_vmem, d_smem, idx_sem)
    idx_copy.start()
    idx_copy.wait()

    def issue(c, carry):
        for u in range(ROW_UNROLL):
            r = c * ROW_UNROLL + u
            for k in range(TOP_K):
                pltpu.make_async_copy(n2_ref.at[pl.ds(r, 1)], xs_hbm.at[pl.ds(d_smem[k, r], 1)], row_sem).start()
        return carry

    lax.fori_loop(0, tm // ROW_UNROLL, issue, 0)
    for k in range(TOP_K):
        pltpu.make_async_copy(n2_ref, xs_hbm.at[pl.ds(0, tm)], row_sem).wait()


def _dispatch_call(rt, tile_base, n2p, xs_zero, *, tm):
    T, DP = n2p.shape
    nt = T // tm
    return pl.pallas_call(
        functools.partial(_dispatch_kernel, tm=tm),
        grid=(nt,),
        in_specs=[
            pl.BlockSpec((tm, ROUTER_COLS), lambda t: (t, 0)),
            pl.BlockSpec((1, N_EXPERTS, LANES), lambda t: (t, 0, 0)),
            pl.BlockSpec((tm, DP), lambda t: (t, 0)),
            pl.BlockSpec(memory_space=pl.ANY),
        ],
        out_specs=(pl.BlockSpec((1, SUBLANES, tm), lambda t: (t, 0, 0)),
                   pl.BlockSpec(memory_space=pl.ANY)),
        out_shape=(jax.ShapeDtypeStruct((nt, SUBLANES, tm), jnp.int32),
                   jax.ShapeDtypeStruct(xs_zero.shape, xs_zero.dtype)),
        input_output_aliases={3: 1},
        scratch_shapes=[
            pltpu.VMEM((SUBLANES, tm), jnp.int32),
            pltpu.SMEM((SUBLANES, tm), jnp.int32),
            pltpu.SemaphoreType.DMA(()),
            pltpu.SemaphoreType.DMA(()),
        ],
        compiler_params=pltpu.CompilerParams(
            dimension_semantics=("arbitrary",), vmem_limit_bytes=40 * 1024 * 1024),
        name="moe_dispatch",
    )(rt, tile_base, n2p, xs_zero)


def _expert_kernel(be_ref, nused_ref, xs_ref, wg_ref, wu_ref, wd_ref, y_ref, wg_b, wu_b, wd_b):
    b = pl.program_id(0)

    @pl.when(b < nused_ref[0])
    def _():
        @pl.when((b == 0) | (be_ref[b] != be_ref[jnp.maximum(b - 1, 0)]))
        def _():
            wg_b[...] = wg_ref[0].astype(BF16)
            wu_b[...] = wu_ref[0].astype(BF16)
            wd_b[...] = wd_ref[0].astype(BF16)

        xb = _unpack_bf16_pairs(xs_ref[...])
        gate = jnp.dot(xb, wg_b[...], preferred_element_type=F32)
        up = jnp.dot(xb, wu_b[...], preferred_element_type=F32)
        hid = (gate * jax.nn.sigmoid(gate) * up).astype(BF16)
        y_ref[...] = jnp.dot(hid, wd_b[...], preferred_element_type=F32)

    @pl.when(b >= nused_ref[0])
    def _():
        y_ref[...] = jnp.zeros_like(y_ref)


def _expert_call(block_expert, n_used, xs, w_gate, w_up, w_down):
    P, DP = xs.shape
    NB = P // EXPERT_BLOCK
    E, D, F = w_gate.shape
    grid_spec = pltpu.PrefetchScalarGridSpec(
        num_scalar_prefetch=2,
        grid=(NB,),
        in_specs=[
            pl.BlockSpec((EXPERT_BLOCK, DP), lambda b, be, nu: (b, 0)),
            pl.BlockSpec((1, D, F), lambda b, be, nu: (be[b], 0, 0)),
            pl.BlockSpec((1, D, F), lambda b, be, nu: (be[b], 0, 0)),
            pl.BlockSpec((1, F, D), lambda b, be, nu: (be[b], 0, 0)),
        ],
        out_specs=pl.BlockSpec((EXPERT_BLOCK, D), lambda b, be, nu: (b, 0)),
        scratch_shapes=[
            pltpu.VMEM((D, F), BF16),
            pltpu.VMEM((D, F), BF16),
            pltpu.VMEM((F, D), BF16),
        ],
    )
    return pl.pallas_call(
        _expert_kernel,
        grid_spec=grid_spec,
        out_shape=jax.ShapeDtypeStruct((P, D), F32),
        compiler_params=pltpu.CompilerParams(
            dimension_semantics=("arbitrary",), vmem_limit_bytes=48 * 1024 * 1024),
        name="moe_experts",
    )(block_expert, n_used, xs, w_gate, w_up, w_down)


def _combine_kernel(dest_hbm, x1_ref, rt_ref, ys_hbm, fg_ref, o_ref, d_smem, ybuf, idx_sem, row_sem,
                    *, tm, final_norm):
    t = pl.program_id(0)
    nt = pl.num_programs(0)
    cur = t % 2
    nxt = 1 - cur

    def idx_copy(tile, s):
        return pltpu.make_async_copy(dest_hbm.at[tile], d_smem.at[s], idx_sem.at[s])

    def issue_rows(s):
        def issue(c, carry):
            for u in range(ROW_UNROLL):
                r = c * ROW_UNROLL + u
                for k in range(TOP_K):
                    pltpu.make_async_copy(ys_hbm.at[pl.ds(d_smem[s, k, r], 1)],
                                          ybuf.at[s, k, pl.ds(r, 1)], row_sem.at[s]).start()
            return carry

        lax.fori_loop(0, tm // ROW_UNROLL, issue, 0)

    @pl.when(t == 0)
    def _():
        first = idx_copy(0, 0)
        first.start()
        first.wait()
        issue_rows(0)

        @pl.when(nt > 1)
        def _():
            idx_copy(1, 1).start()

    @pl.when(t + 1 < nt)
    def _():
        idx_copy(t + 1, nxt).wait()
        issue_rows(nxt)

    @pl.when(t + 2 < nt)
    def _():
        idx_copy(t + 2, cur).start()

    for k in range(TOP_K):
        pltpu.make_async_copy(ys_hbm.at[pl.ds(0, tm)], ybuf.at[cur, k], row_sem.at[cur]).wait()
    rt = rt_ref[...]
    h = x1_ref[...] + rt[:, 2:3] * ybuf[cur, 0] + rt[:, 3:4] * ybuf[cur, 1]
    if final_norm:
        h = h * lax.rsqrt(jnp.mean(h * h, axis=-1, keepdims=True) + EPS) * fg_ref[...]
    o_ref[...] = h


def _combine_call(dest, x1, rt, ys, final_g, *, tm, final_norm):
    T, D = x1.shape
    return pl.pallas_call(
        functools.partial(_combine_kernel, tm=tm, final_norm=final_norm),
        grid=(T // tm,),
        in_specs=[
            pl.BlockSpec(memory_space=pl.ANY),
            pl.BlockSpec((tm, D), lambda t: (t, 0)),
            pl.BlockSpec((tm, ROUTER_COLS), lambda t: (t, 0)),
            pl.BlockSpec(memory_space=pl.ANY),
            pl.BlockSpec((1, D), lambda t: (0, 0)),
        ],
        out_specs=pl.BlockSpec((tm, D), lambda t: (t, 0)),
        out_shape=jax.ShapeDtypeStruct((T, D), F32),
        scratch_shapes=[
            pltpu.SMEM((2, SUBLANES, tm), jnp.int32),
            pltpu.VMEM((2, TOP_K, tm, D), F32),
            pltpu.SemaphoreType.DMA((2,)),
            pltpu.SemaphoreType.DMA((2,)),
        ],
        compiler_params=pltpu.CompilerParams(
            dimension_semantics=("arbitrary",), vmem_limit_bytes=40 * 1024 * 1024),
        name="moe_combine",
    )(dest, x1, rt, ys, final_g)


def _slot_layout(tile_counts, n_assign):
    NB = -(-n_assign // EXPERT_BLOCK) + N_EXPERTS
    counts = jnp.sum(tile_counts, axis=0)
    padded = ((counts + EXPERT_BLOCK - 1) // EXPERT_BLOCK) * EXPERT_BLOCK
    pad_end = jnp.cumsum(padded)
    pad_start = pad_end - padded
    tile_base = pad_start[None, :] + jnp.cumsum(tile_counts, axis=0) - tile_counts
    block_start = jnp.arange(NB, dtype=jnp.int32) * EXPERT_BLOCK
    block_expert = jnp.minimum(jnp.searchsorted(pad_end, block_start, side='right'),
                               N_EXPERTS - 1).astype(jnp.int32)
    n_used = (pad_end[-1] // EXPERT_BLOCK).astype(jnp.int32).reshape(1)
    return NB, block_expert, n_used, tile_base


def _rope_tables(S):
    inv = 1.0 / (ROPE_THETA ** (jnp.arange(0, HEAD_DIM, 2, dtype=F32) / HEAD_DIM))
    ang = jnp.arange(S, dtype=F32)[:, None] * inv[None, :]
    cos, sin = jnp.cos(ang), jnp.sin(ang)
    cos_l = jnp.tile(cos, (1, LANES // (HEAD_DIM // 2)))
    sin_l = jnp.tile(jnp.concatenate([-sin, sin], axis=1), (1, LANES // HEAD_DIM))
    return cos_l, sin_l, cos.T, sin.T


def _swa_pair_perm():
    idx = []
    for g in range(SWA_GROUP):
        for kv in range(SWA_KV_HEADS):
            h = kv * SWA_GROUP + g
            idx.extend(range(h * HEAD_DIM, (h + 1) * HEAD_DIM))
    return jnp.asarray(idx, dtype=jnp.int32)


def kernel(x, norm1_g, w_in, lambda_q1, lambda_k1, lambda_q2, lambda_k2, subln_g, sinks, w_out,
           norm2_g, w_router_group, b_router_group, w_router_expert, b_router_expert,
           w_gate, w_up, w_down, final_g):
    B, S, D = x.shape
    T = B * S
    depth = w_in.shape[0]
    tq, tk = 256, 512
    tm_proj = 512
    tm_tok = 512
    tq_swa = 256
    scale = HEAD_DIM ** -0.5
    cos_l, sin_l, cos_t, sin_t = _rope_tables(S)
    perm = _swa_pair_perm()

    c0 = DIFF_QK_COLS
    c1 = 2 * DIFF_QK_COLS
    c2 = c1 + DIFF_V_COLS
    c3 = c2 + SWA_Q_COLS
    c4 = c3 + SWA_KV_COLS
    for l in range(depth):
        lambda_init = 0.8 - 0.6 * math.exp(-0.3 * l)
        w = w_in[l]
        w_sq = (w[:, c2:c3] * scale)[:, perm]
        w_nat = jnp.concatenate([w[:, c0:c1], w_sq, w[:, c3:c4], w[:, c4:]], axis=1).astype(BF16)
        w_tr = jnp.concatenate([w[:, :c0] * (scale * math.log2(math.e)), w[:, c1:c2]], axis=1).T.astype(BF16)
        dqt, dk, dvt, sq, sk, sv = _proj_call(
            x, norm1_g[l][None, :], w_nat, w_tr, cos_l, sin_l, cos_t, sin_t, tm=tm_proj, tk=tk)

        lam_p = jnp.stack([lambda_q1[l], lambda_k1[l], lambda_q2[l], lambda_k2[l]]).astype(F32)
        o_diff = _diff_call(lam_p, dqt, dk, dvt, subln_g[l][None, :].astype(F32),
                            tq=tq, tk=tk, lambda_init=lambda_init)
        sinks_paired = sinks[l].astype(F32)
        o_swa = _swa_call(sinks_paired, sq, sk, sv, tq=tq_swa)

        wo = w_out[l]
        wo_b = jnp.concatenate([wo[:DIFF_V_COLS], wo[DIFF_V_COLS:][perm]], axis=0).astype(BF16)
        w_router = jnp.zeros((D, ROUTER_COLS), F32)
        w_router = w_router.at[:, :N_GROUPS].set(w_router_group[l])
        w_router = w_router.at[:, N_GROUPS:N_GROUPS + N_EXPERTS].set(w_router_expert[l])
        b_router = jnp.zeros((1, ROUTER_COLS), F32)
        b_router = b_router.at[0, :N_GROUPS].set(b_router_group[l])
        b_router = b_router.at[0, N_GROUPS:N_GROUPS + N_EXPERTS].set(b_router_expert[l])
        x1, n2p, rt, cnt = _mix_call(x, o_diff, o_swa, wo_b, norm2_g[l][None, :], w_router, b_router, tm=tm_tok)

        rt2 = rt.reshape(T, ROUTER_COLS)
        tile_counts = cnt[:, :, 0, :N_EXPERTS].reshape(T // tm_tok, N_EXPERTS).astype(jnp.int32)
        NB, block_expert, n_used, tile_base = _slot_layout(tile_counts, T * TOP_K)
        tile_base = jnp.broadcast_to(tile_base.astype(F32)[:, :, None], (T // tm_tok, N_EXPERTS, LANES))
        xs_zero = jnp.zeros((NB * EXPERT_BLOCK, D // 2), jnp.uint32)
        dest, xs = _dispatch_call(rt2, tile_base, n2p.reshape(T, D // 2), xs_zero, tm=tm_tok)
        ys = _expert_call(block_expert, n_used, xs, w_gate[l], w_up[l], w_down[l])
        x = _combine_call(dest, x1.reshape(T, D), rt2, ys, final_g[None, :],
                          tm=tm_tok, final_norm=(l == depth - 1)).reshape(B, S, D)
    return x
```

```python
import functools
import math

import jax
import jax.numpy as jnp
from jax import lax
from jax.experimental import pallas as pl
from jax.experimental.pallas import tpu as pltpu

HEAD_DIM = 64
DIFF_HEADS = 4
DIFF_V_DIM = 2 * HEAD_DIM
SWA_Q_HEADS = 8
SWA_KV_HEADS = 2
SWA_GROUP = SWA_Q_HEADS // SWA_KV_HEADS
WINDOW = 128
ROPE_THETA = 10000.0
N_GROUPS = 4
EXPERTS_PER_GROUP = 8
N_EXPERTS = N_GROUPS * EXPERTS_PER_GROUP
TOP_K = 2
EXPERT_BLOCK = 256
EPS = 1e-6
NEG = -1e30

DIFF_QK_COLS = DIFF_HEADS * 2 * HEAD_DIM
DIFF_V_COLS = DIFF_HEADS * DIFF_V_DIM
SWA_Q_COLS = SWA_Q_HEADS * HEAD_DIM
SWA_KV_COLS = SWA_KV_HEADS * HEAD_DIM
LANES = 128
SUBLANES = 8
BF16_SUBLANES = 16
VT_ROWS = DIFF_V_DIM + BF16_SUBLANES
ROUTER_COLS = LANES

BF16 = jnp.bfloat16
F32 = jnp.float32


def _rope_lanes(x, cos_l, sin_l, first_half):
    rot = jnp.where(first_half, pltpu.roll(x, 96, 1), pltpu.roll(x, 32, 1))
    return x * cos_l + rot * sin_l


def _proj_kernel(x_ref, g_ref, wnat_ref, wtr_ref, cosl_ref, sinl_ref, cost_ref, sint_ref,
                 dqt_ref, dk_ref, dvt_ref, sq_ref, sk_ref, sv_ref, *, tk):
    x = x_ref[0]
    tm = x.shape[0]
    n1 = x * lax.rsqrt(jnp.mean(x * x, axis=-1, keepdims=True) + EPS) * g_ref[...]
    n1b = n1.astype(BF16)
    nat = jnp.dot(n1b, wnat_ref[...], preferred_element_type=F32)
    tr = lax.dot_general(wtr_ref[...], n1b, (((1,), (1,)), ((), ())),
                         preferred_element_type=F32)

    cos_l, sin_l = cosl_ref[...], sinl_ref[...]
    first_half = (lax.broadcasted_iota(jnp.int32, (tm, LANES), 1) & (HEAD_DIM - 1)) < HEAD_DIM // 2
    for h in range(DIFF_HEADS):
        slab = nat[:, h * LANES:(h + 1) * LANES]
        dk_ref[0, h] = _rope_lanes(slab, cos_l, sin_l, first_half).astype(BF16)
    for c in range(SWA_Q_COLS // LANES):
        lo = DIFF_QK_COLS + c * LANES
        sq_ref[0, :, c * LANES:(c + 1) * LANES] = _rope_lanes(
            nat[:, lo:lo + LANES], cos_l, sin_l, first_half).astype(BF16)
    lo = DIFF_QK_COLS + SWA_Q_COLS
    sk_ref[0] = _rope_lanes(nat[:, lo:lo + LANES], cos_l, sin_l, first_half).astype(BF16)
    sv_ref[0] = nat[:, lo + LANES:lo + 2 * LANES].astype(BF16)

    cos_t, sin_t = cost_ref[...], sint_ref[...]
    half = HEAD_DIM // 2
    for h in range(DIFF_HEADS):
        for c in range(2):
            r0 = h * 2 * HEAD_DIM + c * HEAD_DIM
            x1 = tr[r0:r0 + half]
            x2 = tr[r0 + half:r0 + HEAD_DIM]
            dqt_ref[0, h, c * HEAD_DIM:c * HEAD_DIM + half] = (x1 * cos_t - x2 * sin_t).astype(BF16)
            dqt_ref[0, h, c * HEAD_DIM + half:(c + 1) * HEAD_DIM] = (x1 * sin_t + x2 * cos_t).astype(BF16)
    ones_rows = (lax.broadcasted_iota(jnp.int32, (VT_ROWS - DIFF_V_DIM, tk), 0) == 0).astype(BF16)
    for h in range(DIFF_HEADS):
        r0 = DIFF_QK_COLS + h * DIFF_V_DIM
        for c in range(tm // tk):
            dvt_ref[0, h, c, :DIFF_V_DIM] = tr[r0:r0 + DIFF_V_DIM, c * tk:(c + 1) * tk].astype(BF16)
            dvt_ref[0, h, c, DIFF_V_DIM:] = ones_rows


def _proj_call(x, g1, w_nat, w_tr, cos_l, sin_l, cos_t, sin_t, *, tm, tk):
    B, S, D = x.shape
    nkv = S // tk
    grid = (B, S // tm)
    const = lambda b, i: (0, 0)
    out_shape = (
        jax.ShapeDtypeStruct((B, DIFF_HEADS, 2 * HEAD_DIM, S), BF16),
        jax.ShapeDtypeStruct((B, DIFF_HEADS, S, 2 * HEAD_DIM), BF16),
        jax.ShapeDtypeStruct((B, DIFF_HEADS, nkv, VT_ROWS, tk), BF16),
        jax.ShapeDtypeStruct((B, S, SWA_Q_COLS), BF16),
        jax.ShapeDtypeStruct((B, S, SWA_KV_COLS), BF16),
        jax.ShapeDtypeStruct((B, S, SWA_KV_COLS), BF16),
    )
    return pl.pallas_call(
        functools.partial(_proj_kernel, tk=tk),
        grid=grid,
        in_specs=[
            pl.BlockSpec((1, tm, D), lambda b, i: (b, i, 0)),
            pl.BlockSpec((1, D), const),
            pl.BlockSpec(w_nat.shape, const),
            pl.BlockSpec(w_tr.shape, const),
            pl.BlockSpec((tm, LANES), lambda b, i: (i, 0)),
            pl.BlockSpec((tm, LANES), lambda b, i: (i, 0)),
            pl.BlockSpec((HEAD_DIM // 2, tm), lambda b, i: (0, i)),
            pl.BlockSpec((HEAD_DIM // 2, tm), lambda b, i: (0, i)),
        ],
        out_specs=(
            pl.BlockSpec((1, DIFF_HEADS, 2 * HEAD_DIM, tm), lambda b, i: (b, 0, 0, i)),
            pl.BlockSpec((1, DIFF_HEADS, tm, 2 * HEAD_DIM), lambda b, i: (b, 0, i, 0)),
            pl.BlockSpec((1, DIFF_HEADS, tm // tk, VT_ROWS, tk), lambda b, i: (b, 0, i, 0, 0)),
            pl.BlockSpec((1, tm, SWA_Q_COLS), lambda b, i: (b, i, 0)),
            pl.BlockSpec((1, tm, SWA_KV_COLS), lambda b, i: (b, i, 0)),
            pl.BlockSpec((1, tm, SWA_KV_COLS), lambda b, i: (b, i, 0)),
        ),
        out_shape=out_shape,
        compiler_params=pltpu.CompilerParams(
            dimension_semantics=("parallel", "parallel"), vmem_limit_bytes=48 * 1024 * 1024),
        name="proj_rope",
    )(x, g1, w_nat, w_tr, cos_l, sin_l, cos_t, sin_t)


def _diff_kernel(lam_ref, qt_ref, k_ref, vt_ref, g_ref, o_ref, s0_ref, s1_ref, m_ref, acc_ref,
                 *, tq, tk, lambda_init):
    i = pl.program_id(2)
    s_bufs = (s0_ref, s1_ref)
    qt = qt_ref[0, 0]
    z = jnp.zeros((HEAD_DIM, tq), BF16)
    qw = jnp.concatenate([jnp.concatenate([qt[:HEAD_DIM], z], axis=1),
                          jnp.concatenate([z, qt[HEAD_DIM:]], axis=1)], axis=0)

    def scores(j, par):
        kt = k_ref[0, 0, pl.ds(pl.multiple_of(j * tk, tk), tk), :]
        s_bufs[par][...] = jnp.dot(kt, qw, preferred_element_type=F32)

    def absorb(j, par, masked):
        s = s_bufs[par][...]
        if masked:
            kpos = j * tk + lax.broadcasted_iota(jnp.int32, (tk, 2 * tq), 0)
            qpos = i * tq + (lax.broadcasted_iota(jnp.int32, (tk, 2 * tq), 1) & (tq - 1))
            s = jnp.where(kpos <= qpos, s, NEG)
        m = m_ref[...]
        m_new = jnp.maximum(m, jnp.max(s, axis=0, keepdims=True))
        alpha = jnp.exp2(m - m_new)
        p = jnp.exp2(s - m_new).astype(BF16)
        m_ref[...] = m_new
        pv = jnp.dot(vt_ref[0, 0, j], p, preferred_element_type=F32)
        acc_ref[...] = alpha * acc_ref[...] + pv

    m_ref[...] = jnp.full(m_ref.shape, NEG, F32)
    acc_ref[...] = jnp.zeros(acc_ref.shape, F32)

    nfull = (i * tq) // tk
    n_diag = max(tq // tk, 1)
    scores(0, 0)

    def pair(t, c):
        j = 2 * t
        scores(j + 1, 1)
        absorb(j, 0, False)
        scores(j + 2, 0)
        absorb(j + 1, 1, False)
        return c

    lax.fori_loop(0, nfull // 2, pair, 0)

    def tail(first, masks):
        for idx, masked in enumerate(masks):
            if idx + 1 < len(masks):
                scores(first + idx + 1, (idx + 1) % 2)
            absorb(first + idx, idx % 2, masked)

    @pl.when(nfull % 2 == 0)
    def _():
        tail(nfull, [True] * n_diag)

    @pl.when(nfull % 2 == 1)
    def _():
        tail(nfull - 1, [False] + [True] * n_diag)

    lam_p = lam_ref[...]
    lam = (jnp.exp(jnp.sum(lam_p[0:1] * lam_p[1:2], axis=-1, keepdims=True))
           - jnp.exp(jnp.sum(lam_p[2:3] * lam_p[3:4], axis=-1, keepdims=True)) + lambda_init)
    l = acc_ref[DIFF_V_DIM:DIFF_V_DIM + 1, :]
    o = (acc_ref[:DIFF_V_DIM, :tq] / l[:, :tq]
         - lam * (acc_ref[:DIFF_V_DIM, tq:] / l[:, tq:]))
    o = o * lax.rsqrt(jnp.mean(o * o, axis=0, keepdims=True) + EPS)
    o_ref[0] = (o.T * g_ref[...] * (1.0 - lambda_init)).astype(BF16)


def _diff_call(lam_p, dqt, dk, dvt, subln_g, *, tq, tk, lambda_init):
    B, H, _, S = dqt.shape
    nkv = S // tk
    grid = (B, H, S // tq)
    return pl.pallas_call(
        functools.partial(_diff_kernel, tq=tq, tk=tk, lambda_init=lambda_init),
        grid=grid,
        in_specs=[
            pl.BlockSpec(lam_p.shape, lambda b, h, i: (0, 0)),
            pl.BlockSpec((1, 1, 2 * HEAD_DIM, tq), lambda b, h, i: (b, h, 0, i)),
            pl.BlockSpec((1, 1, S, 2 * HEAD_DIM), lambda b, h, i: (b, h, 0, 0)),
            pl.BlockSpec((1, 1, nkv, VT_ROWS, tk), lambda b, h, i: (b, h, 0, 0, 0)),
            pl.BlockSpec((1, DIFF_V_DIM), lambda b, h, i: (0, 0)),
        ],
        out_specs=pl.BlockSpec((1, tq, DIFF_V_DIM), lambda b, h, i: (b, i, h)),
        out_shape=jax.ShapeDtypeStruct((B, S, DIFF_V_COLS), BF16),
        scratch_shapes=[
            pltpu.VMEM((tk, 2 * tq), F32),
            pltpu.VMEM((tk, 2 * tq), F32),
            pltpu.VMEM((1, 2 * tq), F32),
            pltpu.VMEM((VT_ROWS, 2 * tq), F32),
        ],
        compiler_params=pltpu.CompilerParams(
            dimension_semantics=("parallel", "parallel", "arbitrary"),
            vmem_limit_bytes=48 * 1024 * 1024),
        name="diff_attn",
    )(lam_p, dqt, dk, dvt, subln_g)


def _swa_kernel(sinks_ref, q_ref, k_ref, v_ref, o_ref, *, tq):
    i = pl.program_id(1)
    q0 = i * tq
    kw_len = tq + WINDOW
    ks = pl.multiple_of(jnp.maximum(q0 - WINDOW, 0), WINDOW)
    kw = k_ref[0, pl.ds(ks, kw_len), :]
    vw = v_ref[0, pl.ds(ks, kw_len), :]
    qpos = q0 + lax.broadcasted_iota(jnp.int32, (tq, kw_len), 0)
    kpos = ks + lax.broadcasted_iota(jnp.int32, (tq, kw_len), 1)
    valid = (kpos <= qpos) & (kpos > qpos - WINDOW)
    lo = lax.broadcasted_iota(jnp.int32, (tq, LANES), 1) < HEAD_DIM
    for g in range(SWA_GROUP):
        qg = q_ref[0, :, g * LANES:(g + 1) * LANES]
        outs = []
        for kv in range(SWA_KV_HEADS):
            qm = jnp.where(lo if kv == 0 else jnp.logical_not(lo), qg, jnp.zeros_like(qg))
            s = lax.dot_general(qm, kw, (((1,), (1,)), ((), ())), preferred_element_type=F32)
            s = jnp.where(valid, s, NEG)
            sink = sinks_ref[kv * SWA_GROUP + g]
            m = jnp.maximum(jnp.max(s, axis=-1, keepdims=True), sink)
            p = jnp.exp(s - m)
            den = jnp.sum(p, axis=-1, keepdims=True) + jnp.exp(sink - m)
            outs.append(jnp.dot(p.astype(BF16), vw, preferred_element_type=F32) / den)
        o_ref[0, :, g * LANES:(g + 1) * LANES] = jnp.where(lo, outs[0], outs[1]).astype(BF16)


def _swa_call(sinks, sq, sk, sv, *, tq):
    B, S, _ = sq.shape
    grid_spec = pltpu.PrefetchScalarGridSpec(
        num_scalar_prefetch=1,
        grid=(B, S // tq),
        in_specs=[
            pl.BlockSpec((1, tq, SWA_Q_COLS), lambda b, i, s: (b, i, 0)),
            pl.BlockSpec((1, S, SWA_KV_COLS), lambda b, i, s: (b, 0, 0)),
            pl.BlockSpec((1, S, SWA_KV_COLS), lambda b, i, s: (b, 0, 0)),
        ],
        out_specs=pl.BlockSpec((1, tq, SWA_Q_COLS), lambda b, i, s: (b, i, 0)),
    )
    return pl.pallas_call(
        functools.partial(_swa_kernel, tq=tq),
        grid_spec=grid_spec,
        out_shape=jax.ShapeDtypeStruct((B, S, SWA_Q_COLS), BF16),
        compiler_params=pltpu.CompilerParams(
            dimension_semantics=("parallel", "arbitrary"), vmem_limit_bytes=40 * 1024 * 1024),
        name="swa_attn",
    )(sinks, sq, sk, sv)


def _pack_bf16_pairs(x):
    n = x.shape[1] // 2
    lo = lax.bitcast_convert_type(x[:, :n].astype(BF16).astype(F32), jnp.uint32)
    hi = lax.bitcast_convert_type(x[:, n:].astype(BF16).astype(F32), jnp.uint32)
    return (lo >> 16) | (hi & jnp.uint32(0xFFFF0000))


def _unpack_bf16_pairs(w):
    lo = lax.bitcast_convert_type(w << 16, F32)
    hi = lax.bitcast_convert_type(w & jnp.uint32(0xFFFF0000), F32)
    return jnp.concatenate([lo, hi], axis=1).astype(BF16)


def _mix_kernel(x_ref, od_ref, os_ref, wo_ref, g2_ref, wr_ref, br_ref, x1_ref, n2_ref, rt_ref, cnt_ref):
    h = (x_ref[0]
         + jnp.dot(od_ref[0], wo_ref[:DIFF_V_COLS], preferred_element_type=F32)
         + jnp.dot(os_ref[0], wo_ref[DIFF_V_COLS:], preferred_element_type=F32))
    x1_ref[0] = h
    n2 = h * lax.rsqrt(jnp.mean(h * h, axis=-1, keepdims=True) + EPS) * g2_ref[...]
    n2_ref[0] = _pack_bf16_pairs(n2)
    tm = n2.shape[0]
    n2_hi = n2.astype(BF16)
    n2_lo = (n2 - n2_hi.astype(F32)).astype(BF16)
    parts = jnp.dot(jnp.concatenate([n2_hi, n2_lo], axis=0), wr_ref[...],
                    preferred_element_type=F32)
    logits = ((parts[:tm, :ROUTER_COLS] + parts[tm:, ROUTER_COLS:])
              + (parts[:tm, ROUTER_COLS:] + parts[tm:, :ROUTER_COLS])) + br_ref[...]
    lane = lax.broadcasted_iota(jnp.int32, (tm, ROUTER_COLS), 1)
    big = jnp.int32(ROUTER_COLS)
    gl = jnp.where(lane < N_GROUPS, logits, -jnp.inf)
    gm = jnp.max(gl, axis=-1, keepdims=True)
    p_top = 1.0 / jnp.sum(jnp.exp(gl - gm), axis=-1, keepdims=True)
    g_idx = jnp.min(jnp.where(gl == gm, lane, big), axis=-1, keepdims=True)
    e_lo = N_GROUPS + EXPERTS_PER_GROUP * g_idx
    el = jnp.where((lane >= e_lo) & (lane < e_lo + EXPERTS_PER_GROUP), logits, -jnp.inf)
    v1 = jnp.max(el, axis=-1, keepdims=True)
    i1 = jnp.min(jnp.where(el == v1, lane, big), axis=-1, keepdims=True)
    el2 = jnp.where(lane == i1, -jnp.inf, el)
    v2 = jnp.max(el2, axis=-1, keepdims=True)
    i2 = jnp.min(jnp.where(el2 == v2, lane, big), axis=-1, keepdims=True)
    e21 = jnp.exp(v2 - v1)
    gate1 = p_top / (1.0 + e21)
    gate2 = p_top * e21 / (1.0 + e21)
    rt = jnp.where(lane == 0, (i1 - N_GROUPS).astype(F32),
         jnp.where(lane == 1, (i2 - N_GROUPS).astype(F32),
         jnp.where(lane == 2, gate1, jnp.where(lane == 3, gate2, 0.0))))
    rt_ref[0] = rt
    chosen = ((lane == i1 - N_GROUPS) | (lane == i2 - N_GROUPS)).astype(F32)
    cnt_ref[0, 0] = jnp.broadcast_to(jnp.sum(chosen, axis=0, keepdims=True), cnt_ref.shape[2:])


def _mix_call(x, o_diff, o_swa, w_out, g2, w_router, b_router, *, tm):
    B, S, D = x.shape
    const = lambda b, i: (0, 0)
    row = lambda b, i: (b, i, 0)
    nt = S // tm
    return pl.pallas_call(
        _mix_kernel,
        grid=(B, nt),
        in_specs=[
            pl.BlockSpec((1, tm, D), row),
            pl.BlockSpec((1, tm, DIFF_V_COLS), row),
            pl.BlockSpec((1, tm, SWA_Q_COLS), row),
            pl.BlockSpec(w_out.shape, const),
            pl.BlockSpec((1, D), const),
            pl.BlockSpec(w_router.shape, const),
            pl.BlockSpec((1, ROUTER_COLS), const),
        ],
        out_specs=(pl.BlockSpec((1, tm, D), row), pl.BlockSpec((1, tm, D // 2), row),
                   pl.BlockSpec((1, tm, ROUTER_COLS), row),
                   pl.BlockSpec((1, 1, SUBLANES, ROUTER_COLS), lambda b, i: (b, i, 0, 0))),
        out_shape=(jax.ShapeDtypeStruct((B, S, D), F32), jax.ShapeDtypeStruct((B, S, D // 2), jnp.uint32),
                   jax.ShapeDtypeStruct((B, S, ROUTER_COLS), F32),
                   jax.ShapeDtypeStruct((B, nt, SUBLANES, ROUTER_COLS), F32)),
        compiler_params=pltpu.CompilerParams(
            dimension_semantics=("parallel", "parallel"), vmem_limit_bytes=48 * 1024 * 1024),
        name="outproj_router",
    )(x, o_diff, o_swa, w_out, g2, w_router, b_router)


ROW_UNROLL = 8


def _dispatch_kernel(rt_ref, base_ref, n2_ref, xs_in_hbm, dest_ref, xs_hbm, d_vmem, d_smem, idx_sem, row_sem,
                     *, tm):
    del xs_in_hbm
    rt_t = rt_ref[...].T
    e1 = rt_t[0:1].astype(jnp.int32)
    e2 = rt_t[1:2].astype(jnp.int32)
    eid = lax.broadcasted_iota(jnp.int32, (N_EXPERTS, tm), 0)
    oh1 = eid == e1
    oh2 = eid == e2
    earlier = (lax.broadcasted_iota(jnp.int32, (tm, tm), 0)
               < lax.broadcasted_iota(jnp.int32, (tm, tm), 1)).astype(BF16)
    before = jnp.dot((oh1 | oh2).astype(BF16), earlier, preferred_element_type=F32)
    slot = before + base_ref[0][:, 0:1]
    d1 = jnp.sum(jnp.where(oh1, slot, 0.0), axis=0, keepdims=True).astype(jnp.int32)
    d2 = jnp.sum(jnp.where(oh2, slot, 0.0), axis=0, keepdims=True).astype(jnp.int32)
    d = jnp.concatenate([d1, d2, jnp.zeros((SUBLANES - TOP_K, tm), jnp.int32)], axis=0)
    dest_ref[0] = d
    d_vmem[...] = d
    idx_copy = pltpu.make_async_copy(d_vmem, d_smem, idx_sem)
    idx_copy.start()
    idx_copy.wait()

    def issue(c, carry):
        for u in range(ROW_UNROLL):
            r = c * ROW_UNROLL + u
            for k in range(TOP_K):
                pltpu.make_async_copy(n2_ref.at[pl.ds(r, 1)], xs_hbm.at[pl.ds(d_smem[k, r], 1)], row_sem).start()
        return carry

    lax.fori_loop(0, tm // ROW_UNROLL, issue, 0)
    for k in range(TOP_K):
        pltpu.make_async_copy(n2_ref, xs_hbm.at[pl.ds(0, tm)], row_sem).wait()


def _dispatch_call(rt, tile_base, n2p, xs_zero, *, tm):
    T, DP = n2p.shape
    nt = T // tm
    return pl.pallas_call(
        functools.partial(_dispatch_kernel, tm=tm),
        grid=(nt,),
        in_specs=[
            pl.BlockSpec((tm, ROUTER_COLS), lambda t: (t, 0)),
            pl.BlockSpec((1, N_EXPERTS, LANES), lambda t: (t, 0, 0)),
            pl.BlockSpec((tm, DP), lambda t: (t, 0)),
            pl.BlockSpec(memory_space=pl.ANY),
        ],
        out_specs=(pl.BlockSpec((1, SUBLANES, tm), lambda t: (t, 0, 0)),
                   pl.BlockSpec(memory_space=pl.ANY)),
        out_shape=(jax.ShapeDtypeStruct((nt, SUBLANES, tm), jnp.int32),
                   jax.ShapeDtypeStruct(xs_zero.shape, xs_zero.dtype)),
        input_output_aliases={3: 1},
        scratch_shapes=[
            pltpu.VMEM((SUBLANES, tm), jnp.int32),
            pltpu.SMEM((SUBLANES, tm), jnp.int32),
            pltpu.SemaphoreType.DMA(()),
            pltpu.SemaphoreType.DMA(()),
        ],
        compiler_params=pltpu.CompilerParams(
            dimension_semantics=("arbitrary",), vmem_limit_bytes=40 * 1024 * 1024),
        name="moe_dispatch",
    )(rt, tile_base, n2p, xs_zero)


def _expert_kernel(be_ref, nused_ref, xs_ref, wg_ref, wu_ref, wd_ref, y_ref, wg_b, wu_b, wd_b):
    b = pl.program_id(0)

    @pl.when(b < nused_ref[0])
    def _():
        @pl.when((b == 0) | (be_ref[b] != be_ref[jnp.maximum(b - 1, 0)]))
        def _():
            wg_b[...] = wg_ref[0].astype(BF16)
            wu_b[...] = wu_ref[0].astype(BF16)
            wd_b[...] = wd_ref[0].astype(BF16)

        xb = _unpack_bf16_pairs(xs_ref[...])
        gate = jnp.dot(xb, wg_b[...], preferred_element_type=F32)
        up = jnp.dot(xb, wu_b[...], preferred_element_type=F32)
        hid = (gate * jax.nn.sigmoid(gate) * up).astype(BF16)
        y_ref[...] = jnp.dot(hid, wd_b[...], preferred_element_type=F32)

    @pl.when(b >= nused_ref[0])
    def _():
        y_ref[...] = jnp.zeros_like(y_ref)


def _expert_call(block_expert, n_used, xs, w_gate, w_up, w_down):
    P, DP = xs.shape
    NB = P // EXPERT_BLOCK
    E, D, F = w_gate.shape
    grid_spec = pltpu.PrefetchScalarGridSpec(
        num_scalar_prefetch=2,
        grid=(NB,),
        in_specs=[
            pl.BlockSpec((EXPERT_BLOCK, DP), lambda b, be, nu: (b, 0)),
            pl.BlockSpec((1, D, F), lambda b, be, nu: (be[b], 0, 0)),
            pl.BlockSpec((1, D, F), lambda b, be, nu: (be[b], 0, 0)),
            pl.BlockSpec((1, F, D), lambda b, be, nu: (be[b], 0, 0)),
        ],
        out_specs=pl.BlockSpec((EXPERT_BLOCK, D), lambda b, be, nu: (b, 0)),
        scratch_shapes=[
            pltpu.VMEM((D, F), BF16),
            pltpu.VMEM((D, F), BF16),
            pltpu.VMEM((F, D), BF16),
        ],
    )
    return pl.pallas_call(
        _expert_kernel,
        grid_spec=grid_spec,
        out_shape=jax.ShapeDtypeStruct((P, D), F32),
        compiler_params=pltpu.CompilerParams(
            dimension_semantics=("arbitrary",), vmem_limit_bytes=48 * 1024 * 1024),
        name="moe_experts",
    )(block_expert, n_used, xs, w_gate, w_up, w_down)


def _combine_kernel(dest_hbm, x1_ref, rt_ref, ys_hbm, fg_ref, o_ref, d_smem, ybuf, idx_sem, row_sem,
                    *, tm, final_norm):
    t = pl.program_id(0)
    nt = pl.num_programs(0)
    cur = t % 2
    nxt = 1 - cur

    def idx_copy(tile, s):
        return pltpu.make_async_copy(dest_hbm.at[tile], d_smem.at[s], idx_sem.at[s])

    def issue_rows(s):
        def issue(c, carry):
            for u in range(ROW_UNROLL):
                r = c * ROW_UNROLL + u
                for k in range(TOP_K):
                    pltpu.make_async_copy(ys_hbm.at[pl.ds(d_smem[s, k, r], 1)],
                                          ybuf.at[s, k, pl.ds(r, 1)], row_sem.at[s]).start()
            return carry

        lax.fori_loop(0, tm // ROW_UNROLL, issue, 0)

    @pl.when(t == 0)
    def _():
        first = idx_copy(0, 0)
        first.start()
        first.wait()
        issue_rows(0)

        @pl.when(nt > 1)
        def _():
            idx_copy(1, 1).start()

    @pl.when(t + 1 < nt)
    def _():
        idx_copy(t + 1, nxt).wait()
        issue_rows(nxt)

    @pl.when(t + 2 < nt)
    def _():
        idx_copy(t + 2, cur).start()

    for k in range(TOP_K):
        pltpu.make_async_copy(ys_hbm.at[pl.ds(0, tm)], ybuf.at[cur, k], row_sem.at[cur]).wait()
    rt = rt_ref[...]
    h = x1_ref[...] + rt[:, 2:3] * ybuf[cur, 0] + rt[:, 3:4] * ybuf[cur, 1]
    if final_norm:
        h = h * lax.rsqrt(jnp.mean(h * h, axis=-1, keepdims=True) + EPS) * fg_ref[...]
    o_ref[...] = h


def _combine_call(dest, x1, rt, ys, final_g, *, tm, final_norm):
    T, D = x1.shape
    return pl.pallas_call(
        functools.partial(_combine_kernel, tm=tm, final_norm=final_norm),
        grid=(T // tm,),
        in_specs=[
            pl.BlockSpec(memory_space=pl.ANY),
            pl.BlockSpec((tm, D), lambda t: (t, 0)),
            pl.BlockSpec((tm, ROUTER_COLS), lambda t: (t, 0)),
            pl.BlockSpec(memory_space=pl.ANY),
            pl.BlockSpec((1, D), lambda t: (0, 0)),
        ],
        out_specs=pl.BlockSpec((tm, D), lambda t: (t, 0)),
        out_shape=jax.ShapeDtypeStruct((T, D), F32),
        scratch_shapes=[
            pltpu.SMEM((2, SUBLANES, tm), jnp.int32),
            pltpu.VMEM((2, TOP_K, tm, D), F32),
            pltpu.SemaphoreType.DMA((2,)),
            pltpu.SemaphoreType.DMA((2,)),
        ],
        compiler_params=pltpu.CompilerParams(
            dimension_semantics=("arbitrary",), vmem_limit_bytes=40 * 1024 * 1024),
        name="moe_combine",
    )(dest, x1, rt, ys, final_g)


def _slot_layout(tile_counts, n_assign):
    NB = -(-n_assign // EXPERT_BLOCK) + N_EXPERTS
    counts = jnp.sum(tile_counts, axis=0)
    padded = ((counts + EXPERT_BLOCK - 1) // EXPERT_BLOCK) * EXPERT_BLOCK
    pad_end = jnp.cumsum(padded)
    pad_start = pad_end - padded
    tile_base = pad_start[None, :] + jnp.cumsum(tile_counts, axis=0) - tile_counts
    block_start = jnp.arange(NB, dtype=jnp.int32) * EXPERT_BLOCK
    block_expert = jnp.minimum(jnp.sum(pad_end[None, :] <= block_start[:, None], axis=1),
                               N_EXPERTS - 1).astype(jnp.int32)
    n_used = (pad_end[-1] // EXPERT_BLOCK).astype(jnp.int32).reshape(1)
    return NB, block_expert, n_used, tile_base


def _rope_tables(S):
    inv = 1.0 / (ROPE_THETA ** (jnp.arange(0, HEAD_DIM, 2, dtype=F32) / HEAD_DIM))
    ang = jnp.arange(S, dtype=F32)[:, None] * inv[None, :]
    cos, sin = jnp.cos(ang), jnp.sin(ang)
    cos_l = jnp.tile(cos, (1, LANES // (HEAD_DIM // 2)))
    sin_l = jnp.tile(jnp.concatenate([-sin, sin], axis=1), (1, LANES // HEAD_DIM))
    return cos_l, sin_l, cos.T, sin.T


def _swa_pair_perm():
    idx = []
    for g in range(SWA_GROUP):
        for kv in range(SWA_KV_HEADS):
            h = kv * SWA_GROUP + g
            idx.extend(range(h * HEAD_DIM, (h + 1) * HEAD_DIM))
    return jnp.asarray(idx, dtype=jnp.int32)


def kernel(x, norm1_g, w_in, lambda_q1, lambda_k1, lambda_q2, lambda_k2, subln_g, sinks, w_out,
           norm2_g, w_router_group, b_router_group, w_router_expert, b_router_expert,
           w_gate, w_up, w_down, final_g):
    B, S, D = x.shape
    T = B * S
    depth = w_in.shape[0]
    tq, tk = 512, 512
    tm_proj = 512
    tm_tok = 512
    tq_swa = 256
    scale = HEAD_DIM ** -0.5
    cos_l, sin_l, cos_t, sin_t = _rope_tables(S)
    perm = _swa_pair_perm()

    c0 = DIFF_QK_COLS
    c1 = 2 * DIFF_QK_COLS
    c2 = c1 + DIFF_V_COLS
    c3 = c2 + SWA_Q_COLS
    c4 = c3 + SWA_KV_COLS
    for l in range(depth):
        lambda_init = 0.8 - 0.6 * math.exp(-0.3 * l)
        w = w_in[l]
        w_sq = (w[:, c2:c3] * scale)[:, perm]
        w_nat = jnp.concatenate([w[:, c0:c1], w_sq, w[:, c3:c4], w[:, c4:]], axis=1).astype(BF16)
        w_tr = jnp.concatenate([w[:, :c0] * (scale * math.log2(math.e)), w[:, c1:c2]], axis=1).T.astype(BF16)
        dqt, dk, dvt, sq, sk, sv = _proj_call(
            x, norm1_g[l][None, :], w_nat, w_tr, cos_l, sin_l, cos_t, sin_t, tm=tm_proj, tk=tk)

        lam_p = jnp.stack([lambda_q1[l], lambda_k1[l], lambda_q2[l], lambda_k2[l]]).astype(F32)
        o_diff = _diff_call(lam_p, dqt, dk, dvt, subln_g[l][None, :].astype(F32),
                            tq=tq, tk=tk, lambda_init=lambda_init)
        sinks_paired = sinks[l].astype(F32)
        o_swa = _swa_call(sinks_paired, sq, sk, sv, tq=tq_swa)

        wo = w_out[l]
        wo_b = jnp.concatenate([wo[:DIFF_V_COLS], wo[DIFF_V_COLS:][perm]], axis=0).astype(BF16)
        w_router = jnp.zeros((D, ROUTER_COLS), F32)
        w_router = w_router.at[:, :N_GROUPS].set(w_router_group[l])
        w_router = w_router.at[:, N_GROUPS:N_GROUPS + N_EXPERTS].set(w_router_expert[l])
        w_router_hi = w_router.astype(BF16)
        w_router_lo = (w_router - w_router_hi.astype(F32)).astype(BF16)
        w_router = jnp.concatenate([w_router_hi, w_router_lo], axis=1)
        b_router = jnp.zeros((1, ROUTER_COLS), F32)
        b_router = b_router.at[0, :N_GROUPS].set(b_router_group[l])
        b_router = b_router.at[0, N_GROUPS:N_GROUPS + N_EXPERTS].set(b_router_expert[l])
        x1, n2p, rt, cnt = _mix_call(x, o_diff, o_swa, wo_b, norm2_g[l][None, :], w_router, b_router, tm=tm_tok)

        rt2 = rt.reshape(T, ROUTER_COLS)
        tile_counts = cnt[:, :, 0, :N_EXPERTS].reshape(T // tm_tok, N_EXPERTS).astype(jnp.int32)
        NB, block_expert, n_used, tile_base = _slot_layout(tile_counts, T * TOP_K)
        tile_base = jnp.broadcast_to(tile_base.astype(F32)[:, :, None], (T // tm_tok, N_EXPERTS, LANES))
        xs_zero = jnp.zeros((NB * EXPERT_BLOCK, D // 2), jnp.uint32)
        dest, xs = _dispatch_call(rt2, tile_base, n2p.reshape(T, D // 2), xs_zero, tm=tm_tok)
        ys = _expert_call(block_expert, n_used, xs, w_gate[l], w_up[l], w_down[l])
        x = _combine_call(dest, x1.reshape(T, D), rt2, ys, final_g[None, :],
                          tm=tm_tok, final_norm=(l == depth - 1)).reshape(B, S, D)
    return x
```

```python
import functools
import math

import jax
import jax.numpy as jnp
from jax import lax
from jax.experimental import pallas as pl
from jax.experimental.pallas import tpu as pltpu

HEAD_DIM = 64
DIFF_HEADS = 4
DIFF_V_DIM = 2 * HEAD_DIM
SWA_Q_HEADS = 8
SWA_KV_HEADS = 2
SWA_GROUP = SWA_Q_HEADS // SWA_KV_HEADS
WINDOW = 128
ROPE_THETA = 10000.0
N_GROUPS = 4
EXPERTS_PER_GROUP = 8
N_EXPERTS = N_GROUPS * EXPERTS_PER_GROUP
TOP_K = 2
EXPERT_BLOCK = 512
EXPERT_CHUNK = 256
EPS = 1e-6
NEG = -1e30

DIFF_QK_COLS = DIFF_HEADS * 2 * HEAD_DIM
DIFF_V_COLS = DIFF_HEADS * DIFF_V_DIM
SWA_Q_COLS = SWA_Q_HEADS * HEAD_DIM
SWA_KV_COLS = SWA_KV_HEADS * HEAD_DIM
LANES = 128
SUBLANES = 8
BF16_SUBLANES = 16
VT_ROWS = DIFF_V_DIM + BF16_SUBLANES
ROUTER_COLS = LANES
DIFF_UNROLL = 4

BF16 = jnp.bfloat16
F32 = jnp.float32


def _rope_lanes(x, cos_l, sin_l, first_half):
    rot = jnp.where(first_half, pltpu.roll(x, 96, 1), pltpu.roll(x, 32, 1))
    return x * cos_l + rot * sin_l


def _proj_kernel(x_ref, g_ref, wnat_ref, wtr_ref, cosl_ref, sinl_ref, cost_ref, sint_ref,
                 dqt_ref, dk_ref, dvt_ref, sq_ref, sk_ref, sv_ref, *, tk):
    x = x_ref[0]
    tm = x.shape[0]
    n1 = x * lax.rsqrt(jnp.mean(x * x, axis=-1, keepdims=True) + EPS) * g_ref[...]
    n1b = n1.astype(BF16)
    nat = jnp.dot(n1b, wnat_ref[...], preferred_element_type=F32)
    tr = lax.dot_general(wtr_ref[...], n1b, (((1,), (1,)), ((), ())),
                         preferred_element_type=F32)

    cos_l, sin_l = cosl_ref[...], sinl_ref[...]
    first_half = (lax.broadcasted_iota(jnp.int32, (tm, LANES), 1) & (HEAD_DIM - 1)) < HEAD_DIM // 2
    for h in range(DIFF_HEADS):
        slab = nat[:, h * LANES:(h + 1) * LANES]
        dk_ref[0, h] = _rope_lanes(slab, cos_l, sin_l, first_half).astype(BF16)
    for c in range(SWA_Q_COLS // LANES):
        lo = DIFF_QK_COLS + c * LANES
        sq_ref[0, :, c * LANES:(c + 1) * LANES] = _rope_lanes(
            nat[:, lo:lo + LANES], cos_l, sin_l, first_half).astype(BF16)
    lo = DIFF_QK_COLS + SWA_Q_COLS
    sk_ref[0] = _rope_lanes(nat[:, lo:lo + LANES], cos_l, sin_l, first_half).astype(BF16)
    sv_ref[0] = nat[:, lo + LANES:lo + 2 * LANES].astype(BF16)

    cos_t, sin_t = cost_ref[...], sint_ref[...]
    half = HEAD_DIM // 2
    for h in range(DIFF_HEADS):
        for c in range(2):
            r0 = h * 2 * HEAD_DIM + c * HEAD_DIM
            x1 = tr[r0:r0 + half]
            x2 = tr[r0 + half:r0 + HEAD_DIM]
            dqt_ref[0, h, c * HEAD_DIM:c * HEAD_DIM + half] = (x1 * cos_t - x2 * sin_t).astype(BF16)
            dqt_ref[0, h, c * HEAD_DIM + half:(c + 1) * HEAD_DIM] = (x1 * sin_t + x2 * cos_t).astype(BF16)
    ones_rows = (lax.broadcasted_iota(jnp.int32, (VT_ROWS - DIFF_V_DIM, tk), 0) == 0).astype(BF16)
    for h in range(DIFF_HEADS):
        r0 = DIFF_QK_COLS + h * DIFF_V_DIM
        for c in range(tm // tk):
            dvt_ref[0, h, c, :DIFF_V_DIM] = tr[r0:r0 + DIFF_V_DIM, c * tk:(c + 1) * tk].astype(BF16)
            dvt_ref[0, h, c, DIFF_V_DIM:] = ones_rows


def _proj_call(x, g1, w_nat, w_tr, cos_l, sin_l, cos_t, sin_t, *, tm, tk):
    B, S, D = x.shape
    nkv = S // tk
    grid = (B, S // tm)
    const = lambda b, i: (0, 0)
    out_shape = (
        jax.ShapeDtypeStruct((B, DIFF_HEADS, 2 * HEAD_DIM, S), BF16),
        jax.ShapeDtypeStruct((B, DIFF_HEADS, S, 2 * HEAD_DIM), BF16),
        jax.ShapeDtypeStruct((B, DIFF_HEADS, nkv, VT_ROWS, tk), BF16),
        jax.ShapeDtypeStruct((B, S, SWA_Q_COLS), BF16),
        jax.ShapeDtypeStruct((B, S, SWA_KV_COLS), BF16),
        jax.ShapeDtypeStruct((B, S, SWA_KV_COLS), BF16),
    )
    return pl.pallas_call(
        functools.partial(_proj_kernel, tk=tk),
        grid=grid,
        in_specs=[
            pl.BlockSpec((1, tm, D), lambda b, i: (b, i, 0)),
            pl.BlockSpec((1, D), const),
            pl.BlockSpec(w_nat.shape, const),
            pl.BlockSpec(w_tr.shape, const),
            pl.BlockSpec((tm, LANES), lambda b, i: (i, 0)),
            pl.BlockSpec((tm, LANES), lambda b, i: (i, 0)),
            pl.BlockSpec((HEAD_DIM // 2, tm), lambda b, i: (0, i)),
            pl.BlockSpec((HEAD_DIM // 2, tm), lambda b, i: (0, i)),
        ],
        out_specs=(
            pl.BlockSpec((1, DIFF_HEADS, 2 * HEAD_DIM, tm), lambda b, i: (b, 0, 0, i)),
            pl.BlockSpec((1, DIFF_HEADS, tm, 2 * HEAD_DIM), lambda b, i: (b, 0, i, 0)),
            pl.BlockSpec((1, DIFF_HEADS, tm // tk, VT_ROWS, tk), lambda b, i: (b, 0, i, 0, 0)),
            pl.BlockSpec((1, tm, SWA_Q_COLS), lambda b, i: (b, i, 0)),
            pl.BlockSpec((1, tm, SWA_KV_COLS), lambda b, i: (b, i, 0)),
            pl.BlockSpec((1, tm, SWA_KV_COLS), lambda b, i: (b, i, 0)),
        ),
        out_shape=out_shape,
        compiler_params=pltpu.CompilerParams(
            dimension_semantics=("parallel", "parallel"), vmem_limit_bytes=48 * 1024 * 1024),
        name="proj_rope",
    )(x, g1, w_nat, w_tr, cos_l, sin_l, cos_t, sin_t)


def _diff_kernel(lam_ref, qt_ref, k_ref, vt_ref, g_ref, o_ref, s0_ref, s1_ref, m_ref, acc_ref,
                 *, tq, tk, lambda_init):
    i = pl.program_id(2)
    s_bufs = (s0_ref, s1_ref)
    qt = qt_ref[0, 0]
    z = jnp.zeros((HEAD_DIM, tq), BF16)
    qw = jnp.concatenate([jnp.concatenate([qt[:HEAD_DIM], z], axis=1),
                          jnp.concatenate([z, qt[HEAD_DIM:]], axis=1)], axis=0)

    def scores(j, par):
        kt = k_ref[0, 0, pl.ds(pl.multiple_of(j * tk, tk), tk), :]
        s_bufs[par][...] = jnp.dot(kt, qw, preferred_element_type=F32)

    def absorb(j, par, masked):
        s = s_bufs[par][...]
        if masked:
            kpos = j * tk + lax.broadcasted_iota(jnp.int32, (tk, 2 * tq), 0)
            qpos = i * tq + (lax.broadcasted_iota(jnp.int32, (tk, 2 * tq), 1) & (tq - 1))
            s = jnp.where(kpos <= qpos, s, NEG)
        m = m_ref[...]
        m_new = jnp.maximum(m, jnp.max(s, axis=0, keepdims=True))
        alpha = jnp.exp2(m - m_new)
        p = jnp.exp2(s - m_new).astype(BF16)
        m_ref[...] = m_new
        pv = jnp.dot(vt_ref[0, 0, j], p, preferred_element_type=F32)
        acc_ref[...] = alpha * acc_ref[...] + pv

    m_ref[...] = jnp.full(m_ref.shape, NEG, F32)
    acc_ref[...] = jnp.zeros(acc_ref.shape, F32)

    nfull = (i * tq) // tk
    n_diag = max(tq // tk, 1)
    scores(0, 0)

    def group(t, c):
        j = DIFF_UNROLL * t
        for idx in range(DIFF_UNROLL):
            scores(j + idx + 1, (idx + 1) % 2)
            absorb(j + idx, idx % 2, False)
        return c

    lax.fori_loop(0, nfull // DIFF_UNROLL, group, 0)

    def tail(first, masks):
        for idx, masked in enumerate(masks):
            if idx + 1 < len(masks):
                scores(first + idx + 1, (idx + 1) % 2)
            absorb(first + idx, idx % 2, masked)

    for rem in range(DIFF_UNROLL):
        @pl.when(nfull % DIFF_UNROLL == rem)
        def _():
            tail(nfull - rem, [False] * rem + [True] * n_diag)

    lam_p = lam_ref[...]
    lam = (jnp.exp(jnp.sum(lam_p[0:1] * lam_p[1:2], axis=-1, keepdims=True))
           - jnp.exp(jnp.sum(lam_p[2:3] * lam_p[3:4], axis=-1, keepdims=True)) + lambda_init)
    l = acc_ref[DIFF_V_DIM:DIFF_V_DIM + 1, :]
    o = (acc_ref[:DIFF_V_DIM, :tq] / l[:, :tq]
         - lam * (acc_ref[:DIFF_V_DIM, tq:] / l[:, tq:]))
    o = o * lax.rsqrt(jnp.mean(o * o, axis=0, keepdims=True) + EPS)
    o_ref[0] = (o.T * g_ref[...] * (1.0 - lambda_init)).astype(BF16)


def _diff_call(lam_p, dqt, dk, dvt, subln_g, *, tq, tk, lambda_init):
    B, H, _, S = dqt.shape
    nkv = S // tk
    grid = (B, H, S // tq)
    return pl.pallas_call(
        functools.partial(_diff_kernel, tq=tq, tk=tk, lambda_init=lambda_init),
        grid=grid,
        in_specs=[
            pl.BlockSpec(lam_p.shape, lambda b, h, i: (0, 0)),
            pl.BlockSpec((1, 1, 2 * HEAD_DIM, tq), lambda b, h, i: (b, h, 0, i)),
            pl.BlockSpec((1, 1, S, 2 * HEAD_DIM), lambda b, h, i: (b, h, 0, 0)),
            pl.BlockSpec((1, 1, nkv, VT_ROWS, tk), lambda b, h, i: (b, h, 0, 0, 0)),
            pl.BlockSpec((1, DIFF_V_DIM), lambda b, h, i: (0, 0)),
        ],
        out_specs=pl.BlockSpec((1, tq, DIFF_V_DIM), lambda b, h, i: (b, i, h)),
        out_shape=jax.ShapeDtypeStruct((B, S, DIFF_V_COLS), BF16),
        scratch_shapes=[
            pltpu.VMEM((tk, 2 * tq), F32),
            pltpu.VMEM((tk, 2 * tq), F32),
            pltpu.VMEM((1, 2 * tq), F32),
            pltpu.VMEM((VT_ROWS, 2 * tq), F32),
        ],
        compiler_params=pltpu.CompilerParams(
            dimension_semantics=("parallel", "parallel", "arbitrary"),
            vmem_limit_bytes=48 * 1024 * 1024),
        name="diff_attn",
    )(lam_p, dqt, dk, dvt, subln_g)


def _swa_kernel(sinks_ref, q_ref, k_ref, v_ref, o_ref, *, tq):
    i = pl.program_id(1)
    q0 = i * tq
    kw_len = tq + WINDOW
    ks = pl.multiple_of(jnp.maximum(q0 - WINDOW, 0), WINDOW)
    kw = k_ref[0, pl.ds(ks, kw_len), :]
    vw = v_ref[0, pl.ds(ks, kw_len), :]
    qpos = q0 + lax.broadcasted_iota(jnp.int32, (tq, kw_len), 0)
    kpos = ks + lax.broadcasted_iota(jnp.int32, (tq, kw_len), 1)
    valid = (kpos <= qpos) & (kpos > qpos - WINDOW)
    lo = lax.broadcasted_iota(jnp.int32, (tq, LANES), 1) < HEAD_DIM
    for g in range(SWA_GROUP):
        qg = q_ref[0, :, g * LANES:(g + 1) * LANES]
        outs = []
        for kv in range(SWA_KV_HEADS):
            qm = jnp.where(lo if kv == 0 else jnp.logical_not(lo), qg, jnp.zeros_like(qg))
            s = lax.dot_general(qm, kw, (((1,), (1,)), ((), ())), preferred_element_type=F32)
            s = jnp.where(valid, s, NEG)
            sink = sinks_ref[kv * SWA_GROUP + g]
            m = jnp.maximum(jnp.max(s, axis=-1, keepdims=True), sink)
            p = jnp.exp(s - m)
            den = jnp.sum(p, axis=-1, keepdims=True) + jnp.exp(sink - m)
            outs.append(jnp.dot(p.astype(BF16), vw, preferred_element_type=F32) / den)
        o_ref[0, :, g * LANES:(g + 1) * LANES] = jnp.where(lo, outs[0], outs[1]).astype(BF16)


def _swa_call(sinks, sq, sk, sv, *, tq):
    B, S, _ = sq.shape
    grid_spec = pltpu.PrefetchScalarGridSpec(
        num_scalar_prefetch=1,
        grid=(B, S // tq),
        in_specs=[
            pl.BlockSpec((1, tq, SWA_Q_COLS), lambda b, i, s: (b, i, 0)),
            pl.BlockSpec((1, S, SWA_KV_COLS), lambda b, i, s: (b, 0, 0)),
            pl.BlockSpec((1, S, SWA_KV_COLS), lambda b, i, s: (b, 0, 0)),
        ],
        out_specs=pl.BlockSpec((1, tq, SWA_Q_COLS), lambda b, i, s: (b, i, 0)),
    )
    return pl.pallas_call(
        functools.partial(_swa_kernel, tq=tq),
        grid_spec=grid_spec,
        out_shape=jax.ShapeDtypeStruct((B, S, SWA_Q_COLS), BF16),
        compiler_params=pltpu.CompilerParams(
            dimension_semantics=("parallel", "arbitrary"), vmem_limit_bytes=40 * 1024 * 1024),
        name="swa_attn",
    )(sinks, sq, sk, sv)


def _pack_bf16_pairs(x):
    n = x.shape[1] // 2
    lo = lax.bitcast_convert_type(x[:, :n].astype(BF16).astype(F32), jnp.uint32)
    hi = lax.bitcast_convert_type(x[:, n:].astype(BF16).astype(F32), jnp.uint32)
    return (lo >> 16) | (hi & jnp.uint32(0xFFFF0000))


def _unpack_bf16_pairs(w):
    lo = lax.bitcast_convert_type(w << 16, F32)
    hi = lax.bitcast_convert_type(w & jnp.uint32(0xFFFF0000), F32)
    return jnp.concatenate([lo, hi], axis=1).astype(BF16)


def _mix_kernel(x_ref, od_ref, os_ref, wo_ref, g2_ref, wr_ref, br_ref, x1_ref, n2_ref, rt_ref, cnt_ref):
    h = (x_ref[0]
         + jnp.dot(od_ref[0], wo_ref[:DIFF_V_COLS], preferred_element_type=F32)
         + jnp.dot(os_ref[0], wo_ref[DIFF_V_COLS:], preferred_element_type=F32))
    x1_ref[0] = h
    n2 = h * lax.rsqrt(jnp.mean(h * h, axis=-1, keepdims=True) + EPS) * g2_ref[...]
    n2_ref[0] = _pack_bf16_pairs(n2)
    tm = n2.shape[0]
    n2_hi = n2.astype(BF16)
    n2_lo = (n2 - n2_hi.astype(F32)).astype(BF16)
    parts = jnp.dot(jnp.concatenate([n2_hi, n2_lo], axis=0), wr_ref[...],
                    preferred_element_type=F32)
    logits = ((parts[:tm, :ROUTER_COLS] + parts[tm:, ROUTER_COLS:])
              + (parts[:tm, ROUTER_COLS:] + parts[tm:, :ROUTER_COLS])) + br_ref[...]
    lane = lax.broadcasted_iota(jnp.int32, (tm, ROUTER_COLS), 1)
    big = jnp.int32(ROUTER_COLS)
    gl = jnp.where(lane < N_GROUPS, logits, -jnp.inf)
    gm = jnp.max(gl, axis=-1, keepdims=True)
    p_top = 1.0 / jnp.sum(jnp.exp(gl - gm), axis=-1, keepdims=True)
    g_idx = jnp.min(jnp.where(gl == gm, lane, big), axis=-1, keepdims=True)
    e_lo = N_GROUPS + EXPERTS_PER_GROUP * g_idx
    el = jnp.where((lane >= e_lo) & (lane < e_lo + EXPERTS_PER_GROUP), logits, -jnp.inf)
    v1 = jnp.max(el, axis=-1, keepdims=True)
    i1 = jnp.min(jnp.where(el == v1, lane, big), axis=-1, keepdims=True)
    el2 = jnp.where(lane == i1, -jnp.inf, el)
    v2 = jnp.max(el2, axis=-1, keepdims=True)
    i2 = jnp.min(jnp.where(el2 == v2, lane, big), axis=-1, keepdims=True)
    e21 = jnp.exp(v2 - v1)
    gate1 = p_top / (1.0 + e21)
    gate2 = p_top * e21 / (1.0 + e21)
    rt = jnp.where(lane == 0, (i1 - N_GROUPS).astype(F32),
         jnp.where(lane == 1, (i2 - N_GROUPS).astype(F32),
         jnp.where(lane == 2, gate1, jnp.where(lane == 3, gate2, 0.0))))
    rt_ref[0] = rt
    chosen = ((lane == i1 - N_GROUPS) | (lane == i2 - N_GROUPS)).astype(F32)
    cnt_ref[0, 0] = jnp.broadcast_to(jnp.sum(chosen, axis=0, keepdims=True), cnt_ref.shape[2:])


def _mix_call(x, o_diff, o_swa, w_out, g2, w_router, b_router, *, tm):
    B, S, D = x.shape
    const = lambda b, i: (0, 0)
    row = lambda b, i: (b, i, 0)
    nt = S // tm
    return pl.pallas_call(
        _mix_kernel,
        grid=(B, nt),
        in_specs=[
            pl.BlockSpec((1, tm, D), row),
            pl.BlockSpec((1, tm, DIFF_V_COLS), row),
            pl.BlockSpec((1, tm, SWA_Q_COLS), row),
            pl.BlockSpec(w_out.shape, const),
            pl.BlockSpec((1, D), const),
            pl.BlockSpec(w_router.shape, const),
            pl.BlockSpec((1, ROUTER_COLS), const),
        ],
        out_specs=(pl.BlockSpec((1, tm, D), row), pl.BlockSpec((1, tm, D // 2), row),
                   pl.BlockSpec((1, tm, ROUTER_COLS), row),
                   pl.BlockSpec((1, 1, SUBLANES, ROUTER_COLS), lambda b, i: (b, i, 0, 0))),
        out_shape=(jax.ShapeDtypeStruct((B, S, D), F32), jax.ShapeDtypeStruct((B, S, D // 2), jnp.uint32),
                   jax.ShapeDtypeStruct((B, S, ROUTER_COLS), F32),
                   jax.ShapeDtypeStruct((B, nt, SUBLANES, ROUTER_COLS), F32)),
        compiler_params=pltpu.CompilerParams(
            dimension_semantics=("parallel", "parallel"), vmem_limit_bytes=48 * 1024 * 1024),
        name="outproj_router",
    )(x, o_diff, o_swa, w_out, g2, w_router, b_router)


ROW_UNROLL = 8


def _dispatch_kernel(rt_ref, base_ref, n2_ref, xs_in_hbm, dest_ref, xs_hbm, d_vmem, d_smem, idx_sem, row_sem,
                     *, tm):
    del xs_in_hbm
    rt_t = rt_ref[...].T
    e1 = rt_t[0:1].astype(jnp.int32)
    e2 = rt_t[1:2].astype(jnp.int32)
    eid = lax.broadcasted_iota(jnp.int32, (N_EXPERTS, tm), 0)
    oh1 = eid == e1
    oh2 = eid == e2
    earlier = (lax.broadcasted_iota(jnp.int32, (tm, tm), 0)
               < lax.broadcasted_iota(jnp.int32, (tm, tm), 1)).astype(BF16)
    before = jnp.dot((oh1 | oh2).astype(BF16), earlier, preferred_element_type=F32)
    slot = before + base_ref[0][:, 0:1]
    d1 = jnp.sum(jnp.where(oh1, slot, 0.0), axis=0, keepdims=True).astype(jnp.int32)
    d2 = jnp.sum(jnp.where(oh2, slot, 0.0), axis=0, keepdims=True).astype(jnp.int32)
    d = jnp.concatenate([d1, d2, jnp.zeros((SUBLANES - TOP_K, tm), jnp.int32)], axis=0)
    dest_ref[0] = d
    d_vmem[...] = d
    idx_copy = pltpu.make_async_copy(d_vmem, d_smem, idx_sem)
    idx_copy.start()
    idx_copy.wait()

    def issue(c, carry):
        for u in range(ROW_UNROLL):
            r = c * ROW_UNROLL + u
            for k in range(TOP_K):
                pltpu.make_async_copy(n2_ref.at[pl.ds(r, 1)], xs_hbm.at[pl.ds(d_smem[k, r], 1)], row_sem).start()
        return carry

    lax.fori_loop(0, tm // ROW_UNROLL, issue, 0)
    for k in range(TOP_K):
        pltpu.make_async_copy(n2_ref, xs_hbm.at[pl.ds(0, tm)], row_sem).wait()


def _dispatch_call(rt, tile_base, n2p, xs_zero, *, tm):
    T = n2p.shape[0]
    nt = T // tm
    return pl.pallas_call(
        functools.partial(_dispatch_kernel, tm=tm),
        grid=(nt,),
        in_specs=[
            pl.BlockSpec((tm, ROUTER_COLS), lambda t: (t, 0)),
            pl.BlockSpec((1, N_EXPERTS, LANES), lambda t: (t, 0, 0)),
            pl.BlockSpec((tm,) + n2p.shape[1:], lambda t: (t, 0)),
            pl.BlockSpec(memory_space=pl.ANY),
        ],
        out_specs=(pl.BlockSpec((1, SUBLANES, tm), lambda t: (t, 0, 0)),
                   pl.BlockSpec(memory_space=pl.ANY)),
        out_shape=(jax.ShapeDtypeStruct((nt, SUBLANES, tm), jnp.int32),
                   jax.ShapeDtypeStruct(xs_zero.shape, xs_zero.dtype)),
        input_output_aliases={3: 1},
        scratch_shapes=[
            pltpu.VMEM((SUBLANES, tm), jnp.int32),
            pltpu.SMEM((SUBLANES, tm), jnp.int32),
            pltpu.SemaphoreType.DMA(()),
            pltpu.SemaphoreType.DMA(()),
        ],
        compiler_params=pltpu.CompilerParams(
            dimension_semantics=("arbitrary",), vmem_limit_bytes=40 * 1024 * 1024),
        name="moe_dispatch",
    )(rt, tile_base, n2p, xs_zero)


def _expert_kernel(be_ref, nused_ref, xs_ref, wg_ref, wu_ref, wd_ref, y_ref, wg_b, wu_b, wd_b):
    b = pl.program_id(0)

    @pl.when(b < nused_ref[0])
    def _():
        @pl.when((b == 0) | (be_ref[b] != be_ref[jnp.maximum(b - 1, 0)]))
        def _():
            wg_b[...] = wg_ref[0].astype(BF16)
            wu_b[...] = wu_ref[0].astype(BF16)
            wd_b[...] = wd_ref[0].astype(BF16)

        for c in range(EXPERT_BLOCK // EXPERT_CHUNK):
            rows = pl.ds(c * EXPERT_CHUNK, EXPERT_CHUNK)
            xb = _unpack_bf16_pairs(xs_ref[rows, :])
            gate = jnp.dot(xb, wg_b[...], preferred_element_type=F32)
            up = jnp.dot(xb, wu_b[...], preferred_element_type=F32)
            hid = (gate * jax.nn.sigmoid(gate) * up).astype(BF16)
            y_ref[rows, :] = jnp.dot(hid, wd_b[...], preferred_element_type=F32)

    @pl.when(b >= nused_ref[0])
    def _():
        y_ref[...] = jnp.zeros_like(y_ref)


def _expert_call(block_expert, n_used, xs, w_gate, w_up, w_down):
    P = xs.shape[0]
    NB = P // EXPERT_BLOCK
    E, D, F = w_gate.shape
    grid_spec = pltpu.PrefetchScalarGridSpec(
        num_scalar_prefetch=2,
        grid=(NB,),
        in_specs=[
            pl.BlockSpec((EXPERT_BLOCK,) + xs.shape[1:], lambda b, be, nu: (b, 0)),
            pl.BlockSpec((1, D, F), lambda b, be, nu: (be[b], 0, 0)),
            pl.BlockSpec((1, D, F), lambda b, be, nu: (be[b], 0, 0)),
            pl.BlockSpec((1, F, D), lambda b, be, nu: (be[b], 0, 0)),
        ],
        out_specs=pl.BlockSpec((EXPERT_BLOCK, D), lambda b, be, nu: (b, 0)),
        scratch_shapes=[
            pltpu.VMEM((D, F), BF16),
            pltpu.VMEM((D, F), BF16),
            pltpu.VMEM((F, D), BF16),
        ],
    )
    return pl.pallas_call(
        _expert_kernel,
        grid_spec=grid_spec,
        out_shape=jax.ShapeDtypeStruct((P, D), F32),
        compiler_params=pltpu.CompilerParams(
            dimension_semantics=("arbitrary",), vmem_limit_bytes=48 * 1024 * 1024),
        name="moe_experts",
    )(block_expert, n_used, xs, w_gate, w_up, w_down)


def _combine_kernel(dest_hbm, x1_ref, rt_ref, ys_hbm, fg_ref, o_ref, d_smem, ybuf, idx_sem, row_sem,
                    *, tm, final_norm):
    t = pl.program_id(0)
    nt = pl.num_programs(0)
    cur = t % 2
    nxt = 1 - cur

    def idx_copy(tile, s):
        return pltpu.make_async_copy(dest_hbm.at[tile], d_smem.at[s], idx_sem.at[s])

    def issue_rows(s):
        def issue(c, carry):
            for u in range(ROW_UNROLL):
                r = c * ROW_UNROLL + u
                for k in range(TOP_K):
                    pltpu.make_async_copy(ys_hbm.at[pl.ds(d_smem[s, k, r], 1)],
                                          ybuf.at[s, k, pl.ds(r, 1)], row_sem.at[s]).start()
            return carry

        lax.fori_loop(0, tm // ROW_UNROLL, issue, 0)

    @pl.when(t == 0)
    def _():
        first = idx_copy(0, 0)
        first.start()
        first.wait()
        issue_rows(0)

        @pl.when(nt > 1)
        def _():
            idx_copy(1, 1).start()

    @pl.when(t + 1 < nt)
    def _():
        idx_copy(t + 1, nxt).wait()
        issue_rows(nxt)

    @pl.when(t + 2 < nt)
    def _():
        idx_copy(t + 2, cur).start()

    for k in range(TOP_K):
        pltpu.make_async_copy(ys_hbm.at[pl.ds(0, tm)], ybuf.at[cur, k], row_sem.at[cur]).wait()
    rt = rt_ref[...]
    h = x1_ref[...] + rt[:, 2:3] * ybuf[cur, 0] + rt[:, 3:4] * ybuf[cur, 1]
    if final_norm:
        h = h * lax.rsqrt(jnp.mean(h * h, axis=-1, keepdims=True) + EPS) * fg_ref[...]
    o_ref[...] = h


def _combine_call(dest, x1, rt, ys, final_g, *, tm, final_norm):
    T, D = x1.shape
    return pl.pallas_call(
        functools.partial(_combine_kernel, tm=tm, final_norm=final_norm),
        grid=(T // tm,),
        in_specs=[
            pl.BlockSpec(memory_space=pl.ANY),
            pl.BlockSpec((tm, D), lambda t: (t, 0)),
            pl.BlockSpec((tm, ROUTER_COLS), lambda t: (t, 0)),
            pl.BlockSpec(memory_space=pl.ANY),
            pl.BlockSpec((1, D), lambda t: (0, 0)),
        ],
        out_specs=pl.BlockSpec((tm, D), lambda t: (t, 0)),
        out_shape=jax.ShapeDtypeStruct((T, D), F32),
        scratch_shapes=[
            pltpu.SMEM((2, SUBLANES, tm), jnp.int32),
            pltpu.VMEM((2, TOP_K, tm, D), F32),
            pltpu.SemaphoreType.DMA((2,)),
            pltpu.SemaphoreType.DMA((2,)),
        ],
        compiler_params=pltpu.CompilerParams(
            dimension_semantics=("arbitrary",), vmem_limit_bytes=40 * 1024 * 1024),
        name="moe_combine",
    )(dest, x1, rt, ys, final_g)


def _slot_layout(tile_counts, n_assign):
    NB = -(-n_assign // EXPERT_BLOCK) + N_EXPERTS
    counts = jnp.sum(tile_counts, axis=0)
    padded = ((counts + EXPERT_BLOCK - 1) // EXPERT_BLOCK) * EXPERT_BLOCK
    pad_end = jnp.cumsum(padded)
    pad_start = pad_end - padded
    tile_base = pad_start[None, :] + jnp.cumsum(tile_counts, axis=0) - tile_counts
    block_start = jnp.arange(NB, dtype=jnp.int32) * EXPERT_BLOCK
    block_expert = jnp.minimum(jnp.sum(pad_end[None, :] <= block_start[:, None], axis=1),
                               N_EXPERTS - 1).astype(jnp.int32)
    n_used = (pad_end[-1] // EXPERT_BLOCK).astype(jnp.int32).reshape(1)
    return NB, block_expert, n_used, tile_base


def _rope_tables(S):
    inv = 1.0 / (ROPE_THETA ** (jnp.arange(0, HEAD_DIM, 2, dtype=F32) / HEAD_DIM))
    ang = jnp.arange(S, dtype=F32)[:, None] * inv[None, :]
    cos, sin = jnp.cos(ang), jnp.sin(ang)
    cos_l = jnp.tile(cos, (1, LANES // (HEAD_DIM // 2)))
    sin_l = jnp.tile(jnp.concatenate([-sin, sin], axis=1), (1, LANES // HEAD_DIM))
    return cos_l, sin_l, cos.T, sin.T


def _swa_pair_perm():
    idx = []
    for g in range(SWA_GROUP):
        for kv in range(SWA_KV_HEADS):
            h = kv * SWA_GROUP + g
            idx.extend(range(h * HEAD_DIM, (h + 1) * HEAD_DIM))
    return jnp.asarray(idx, dtype=jnp.int32)


def kernel(x, norm1_g, w_in, lambda_q1, lambda_k1, lambda_q2, lambda_k2, subln_g, sinks, w_out,
           norm2_g, w_router_group, b_router_group, w_router_expert, b_router_expert,
           w_gate, w_up, w_down, final_g):
    B, S, D = x.shape
    T = B * S
    depth = w_in.shape[0]
    tq, tk = 512, 512
    tm_proj = 512
    tm_tok = 512
    tq_swa = 256
    scale = HEAD_DIM ** -0.5
    cos_l, sin_l, cos_t, sin_t = _rope_tables(S)
    perm = _swa_pair_perm()

    c0 = DIFF_QK_COLS
    c1 = 2 * DIFF_QK_COLS
    c2 = c1 + DIFF_V_COLS
    c3 = c2 + SWA_Q_COLS
    c4 = c3 + SWA_KV_COLS
    for l in range(depth):
        lambda_init = 0.8 - 0.6 * math.exp(-0.3 * l)
        w = w_in[l]
        w_sq = (w[:, c2:c3] * scale)[:, perm]
        w_nat = jnp.concatenate([w[:, c0:c1], w_sq, w[:, c3:c4], w[:, c4:]], axis=1).astype(BF16)
        w_tr = jnp.concatenate([w[:, :c0] * (scale * math.log2(math.e)), w[:, c1:c2]], axis=1).T.astype(BF16)
        dqt, dk, dvt, sq, sk, sv = _proj_call(
            x, norm1_g[l][None, :], w_nat, w_tr, cos_l, sin_l, cos_t, sin_t, tm=tm_proj, tk=tk)

        lam_p = jnp.stack([lambda_q1[l], lambda_k1[l], lambda_q2[l], lambda_k2[l]]).astype(F32)
        o_diff = _diff_call(lam_p, dqt, dk, dvt, subln_g[l][None, :].astype(F32),
                            tq=tq, tk=tk, lambda_init=lambda_init)
        sinks_paired = sinks[l].astype(F32)
        o_swa = _swa_call(sinks_paired, sq, sk, sv, tq=tq_swa)

        wo = w_out[l]
        wo_b = jnp.concatenate([wo[:DIFF_V_COLS], wo[DIFF_V_COLS:][perm]], axis=0).astype(BF16)
        w_router = jnp.zeros((D, ROUTER_COLS), F32)
        w_router = w_router.at[:, :N_GROUPS].set(w_router_group[l])
        w_router = w_router.at[:, N_GROUPS:N_GROUPS + N_EXPERTS].set(w_router_expert[l])
        w_router_hi = w_router.astype(BF16)
        w_router_lo = (w_router - w_router_hi.astype(F32)).astype(BF16)
        w_router = jnp.concatenate([w_router_hi, w_router_lo], axis=1)
        b_router = jnp.zeros((1, ROUTER_COLS), F32)
        b_router = b_router.at[0, :N_GROUPS].set(b_router_group[l])
        b_router = b_router.at[0, N_GROUPS:N_GROUPS + N_EXPERTS].set(b_router_expert[l])
        x1, n2p, rt, cnt = _mix_call(x, o_diff, o_swa, wo_b, norm2_g[l][None, :], w_router, b_router, tm=tm_tok)

        rt2 = rt.reshape(T, ROUTER_COLS)
        tile_counts = cnt[:, :, 0, :N_EXPERTS].reshape(T // tm_tok, N_EXPERTS).astype(jnp.int32)
        NB, block_expert, n_used, tile_base = _slot_layout(tile_counts, T * TOP_K)
        tile_base = jnp.broadcast_to(tile_base.astype(F32)[:, :, None], (T // tm_tok, N_EXPERTS, LANES))
        xs_zero = jnp.zeros((NB * EXPERT_BLOCK,) + n2p.shape[2:], jnp.uint32)
        dest, xs = _dispatch_call(rt2, tile_base, n2p.reshape((T,) + n2p.shape[2:]), xs_zero, tm=tm_tok)
        ys = _expert_call(block_expert, n_used, xs, w_gate[l], w_up[l], w_down[l])
        x = _combine_call(dest, x1.reshape(T, D), rt2, ys, final_g[None, :],
                          tm=tm_tok, final_norm=(l == depth - 1)).reshape(B, S, D)
    return x
```

```python
import functools
import math

import jax
import jax.numpy as jnp
from jax import lax
from jax.experimental import pallas as pl
from jax.experimental.pallas import tpu as pltpu

HEAD_DIM = 64
DIFF_HEADS = 4
DIFF_V_DIM = 2 * HEAD_DIM
SWA_Q_HEADS = 8
SWA_KV_HEADS = 2
SWA_GROUP = SWA_Q_HEADS // SWA_KV_HEADS
WINDOW = 128
ROPE_THETA = 10000.0
N_GROUPS = 4
EXPERTS_PER_GROUP = 8
N_EXPERTS = N_GROUPS * EXPERTS_PER_GROUP
TOP_K = 2
EXPERT_BLOCK = 512
EXPERT_CHUNK = 256
EPS = 1e-6
NEG = -1e30

DIFF_QK_COLS = DIFF_HEADS * 2 * HEAD_DIM
DIFF_V_COLS = DIFF_HEADS * DIFF_V_DIM
SWA_Q_COLS = SWA_Q_HEADS * HEAD_DIM
SWA_KV_COLS = SWA_KV_HEADS * HEAD_DIM
LANES = 128
SUBLANES = 8
BF16_SUBLANES = 16
VT_ROWS = DIFF_V_DIM + BF16_SUBLANES
SWA_VT_ROWS = SWA_KV_COLS + BF16_SUBLANES
ROUTER_COLS = LANES
DIFF_UNROLL = 4

BF16 = jnp.bfloat16
F32 = jnp.float32


def _rope_lanes(x, cos_l, sin_l, first_half):
    rot = jnp.where(first_half, pltpu.roll(x, 96, 1), pltpu.roll(x, 32, 1))
    return x * cos_l + rot * sin_l


def _proj_kernel(x_ref, g_ref, wnat_ref, wtr_ref, cosl_ref, sinl_ref, cost_ref, sint_ref,
                 dqt_ref, dk_ref, dvt_ref, sqt_ref, sk_ref, svt_ref, *, tk):
    x = x_ref[0]
    tm = x.shape[0]
    n1 = x * lax.rsqrt(jnp.mean(x * x, axis=-1, keepdims=True) + EPS) * g_ref[...]
    n1b = n1.astype(BF16)
    nat = jnp.dot(n1b, wnat_ref[...], preferred_element_type=F32)
    tr = lax.dot_general(wtr_ref[...], n1b, (((1,), (1,)), ((), ())),
                         preferred_element_type=F32)

    cos_l, sin_l = cosl_ref[...], sinl_ref[...]
    first_half = (lax.broadcasted_iota(jnp.int32, (tm, LANES), 1) & (HEAD_DIM - 1)) < HEAD_DIM // 2
    for h in range(DIFF_HEADS):
        slab = nat[:, h * LANES:(h + 1) * LANES]
        dk_ref[0, h] = _rope_lanes(slab, cos_l, sin_l, first_half).astype(BF16)
    sk = _rope_lanes(nat[:, DIFF_QK_COLS:DIFF_QK_COLS + LANES], cos_l, sin_l, first_half).astype(BF16)
    for c in range(tm // WINDOW):
        sk_ref[0, c] = sk[c * WINDOW:(c + 1) * WINDOW]

    cos_t, sin_t = cost_ref[...], sint_ref[...]
    half = HEAD_DIM // 2

    def rope_rows(r0):
        x1 = tr[r0:r0 + half]
        x2 = tr[r0 + half:r0 + HEAD_DIM]
        return (x1 * cos_t - x2 * sin_t).astype(BF16), (x1 * sin_t + x2 * cos_t).astype(BF16)

    for h in range(DIFF_HEADS):
        for c in range(2):
            lo, hi = rope_rows(h * 2 * HEAD_DIM + c * HEAD_DIM)
            dqt_ref[0, h, c * HEAD_DIM:c * HEAD_DIM + half] = lo
            dqt_ref[0, h, c * HEAD_DIM + half:(c + 1) * HEAD_DIM] = hi
    ones_rows = (lax.broadcasted_iota(jnp.int32, (BF16_SUBLANES, tk), 0) == 0).astype(BF16)
    for h in range(DIFF_HEADS):
        r0 = DIFF_QK_COLS + h * DIFF_V_DIM
        for c in range(tm // tk):
            dvt_ref[0, h, c, :DIFF_V_DIM] = tr[r0:r0 + DIFF_V_DIM, c * tk:(c + 1) * tk].astype(BF16)
            dvt_ref[0, h, c, DIFF_V_DIM:] = ones_rows

    r0 = DIFF_QK_COLS + DIFF_V_COLS
    for h in range(SWA_Q_HEADS):
        lo, hi = rope_rows(r0 + h * HEAD_DIM)
        sqt_ref[0, h * HEAD_DIM:h * HEAD_DIM + half] = lo
        sqt_ref[0, h * HEAD_DIM + half:(h + 1) * HEAD_DIM] = hi
    r0 += SWA_Q_COLS
    for c in range(tm // WINDOW):
        svt_ref[0, c, :SWA_KV_COLS] = tr[r0:r0 + SWA_KV_COLS, c * WINDOW:(c + 1) * WINDOW].astype(BF16)
        svt_ref[0, c, SWA_KV_COLS:] = ones_rows[:, :WINDOW]


def _proj_call(x, g1, w_nat, w_tr, cos_l, sin_l, cos_t, sin_t, *, tm, tk):
    B, S, D = x.shape
    nkv = S // tk
    grid = (B, S // tm)
    const = lambda b, i: (0, 0)
    out_shape = (
        jax.ShapeDtypeStruct((B, DIFF_HEADS, 2 * HEAD_DIM, S), BF16),
        jax.ShapeDtypeStruct((B, DIFF_HEADS, S, 2 * HEAD_DIM), BF16),
        jax.ShapeDtypeStruct((B, DIFF_HEADS, nkv, VT_ROWS, tk), BF16),
        jax.ShapeDtypeStruct((B, SWA_Q_COLS, S), BF16),
        jax.ShapeDtypeStruct((B, S // WINDOW, WINDOW, SWA_KV_COLS), BF16),
        jax.ShapeDtypeStruct((B, S // WINDOW, SWA_VT_ROWS, WINDOW), BF16),
    )
    return pl.pallas_call(
        functools.partial(_proj_kernel, tk=tk),
        grid=grid,
        in_specs=[
            pl.BlockSpec((1, tm, D), lambda b, i: (b, i, 0)),
            pl.BlockSpec((1, D), const),
            pl.BlockSpec(w_nat.shape, const),
            pl.BlockSpec(w_tr.shape, const),
            pl.BlockSpec((tm, LANES), lambda b, i: (i, 0)),
            pl.BlockSpec((tm, LANES), lambda b, i: (i, 0)),
            pl.BlockSpec((HEAD_DIM // 2, tm), lambda b, i: (0, i)),
            pl.BlockSpec((HEAD_DIM // 2, tm), lambda b, i: (0, i)),
        ],
        out_specs=(
            pl.BlockSpec((1, DIFF_HEADS, 2 * HEAD_DIM, tm), lambda b, i: (b, 0, 0, i)),
            pl.BlockSpec((1, DIFF_HEADS, tm, 2 * HEAD_DIM), lambda b, i: (b, 0, i, 0)),
            pl.BlockSpec((1, DIFF_HEADS, tm // tk, VT_ROWS, tk), lambda b, i: (b, 0, i, 0, 0)),
            pl.BlockSpec((1, SWA_Q_COLS, tm), lambda b, i: (b, 0, i)),
            pl.BlockSpec((1, tm // WINDOW, WINDOW, SWA_KV_COLS), lambda b, i: (b, i, 0, 0)),
            pl.BlockSpec((1, tm // WINDOW, SWA_VT_ROWS, WINDOW), lambda b, i: (b, i, 0, 0)),
        ),
        out_shape=out_shape,
        compiler_params=pltpu.CompilerParams(
            dimension_semantics=("parallel", "parallel"), vmem_limit_bytes=48 * 1024 * 1024),
        name="proj_rope",
    )(x, g1, w_nat, w_tr, cos_l, sin_l, cos_t, sin_t)


def _diff_kernel(lam_ref, qt_ref, k_ref, vt_ref, g_ref, o_ref, s0_ref, s1_ref, m_ref, acc_ref,
                 *, tq, tk, lambda_init):
    i = pl.program_id(2)
    s_bufs = (s0_ref, s1_ref)
    qt = qt_ref[0, 0]
    z = jnp.zeros((HEAD_DIM, tq), BF16)
    qw = jnp.concatenate([jnp.concatenate([qt[:HEAD_DIM], z], axis=1),
                          jnp.concatenate([z, qt[HEAD_DIM:]], axis=1)], axis=0)

    def scores(j, par):
        kt = k_ref[0, 0, pl.ds(pl.multiple_of(j * tk, tk), tk), :]
        s_bufs[par][...] = jnp.dot(kt, qw, preferred_element_type=F32)

    def absorb(j, par, masked):
        s = s_bufs[par][...]
        if masked:
            kpos = j * tk + lax.broadcasted_iota(jnp.int32, (tk, 2 * tq), 0)
            qpos = i * tq + (lax.broadcasted_iota(jnp.int32, (tk, 2 * tq), 1) & (tq - 1))
            s = jnp.where(kpos <= qpos, s, NEG)
        m = m_ref[...]
        m_new = jnp.maximum(m, jnp.max(s, axis=0, keepdims=True))
        alpha = jnp.exp2(m - m_new)
        p = jnp.exp2(s - m_new).astype(BF16)
        m_ref[...] = m_new
        pv = jnp.dot(vt_ref[0, 0, j], p, preferred_element_type=F32)
        acc_ref[...] = alpha * acc_ref[...] + pv

    m_ref[...] = jnp.full(m_ref.shape, NEG, F32)
    acc_ref[...] = jnp.zeros(acc_ref.shape, F32)

    nfull = (i * tq) // tk
    n_diag = max(tq // tk, 1)
    scores(0, 0)

    def group(t, c):
        j = DIFF_UNROLL * t
        for idx in range(DIFF_UNROLL):
            scores(j + idx + 1, (idx + 1) % 2)
            absorb(j + idx, idx % 2, False)
        return c

    lax.fori_loop(0, nfull // DIFF_UNROLL, group, 0)

    def tail(first, masks):
        for idx, masked in enumerate(masks):
            if idx + 1 < len(masks):
                scores(first + idx + 1, (idx + 1) % 2)
            absorb(first + idx, idx % 2, masked)

    for rem in range(DIFF_UNROLL):
        @pl.when(nfull % DIFF_UNROLL == rem)
        def _():
            tail(nfull - rem, [False] * rem + [True] * n_diag)

    lam_p = lam_ref[...]
    lam = (jnp.exp(jnp.sum(lam_p[0:1] * lam_p[1:2], axis=-1, keepdims=True))
           - jnp.exp(jnp.sum(lam_p[2:3] * lam_p[3:4], axis=-1, keepdims=True)) + lambda_init)
    l = acc_ref[DIFF_V_DIM:DIFF_V_DIM + 1, :]
    o = (acc_ref[:DIFF_V_DIM, :tq] / l[:, :tq]
         - lam * (acc_ref[:DIFF_V_DIM, tq:] / l[:, tq:]))
    o = o * lax.rsqrt(jnp.mean(o * o, axis=0, keepdims=True) + EPS)
    o_ref[0] = (o.T * g_ref[...] * (1.0 - lambda_init)).astype(BF16)


def _diff_call(lam_p, dqt, dk, dvt, subln_g, *, tq, tk, lambda_init):
    B, H, _, S = dqt.shape
    nkv = S // tk
    grid = (B, H, S // tq)
    return pl.pallas_call(
        functools.partial(_diff_kernel, tq=tq, tk=tk, lambda_init=lambda_init),
        grid=grid,
        in_specs=[
            pl.BlockSpec(lam_p.shape, lambda b, h, i: (0, 0)),
            pl.BlockSpec((1, 1, 2 * HEAD_DIM, tq), lambda b, h, i: (b, h, 0, i)),
            pl.BlockSpec((1, 1, S, 2 * HEAD_DIM), lambda b, h, i: (b, h, 0, 0)),
            pl.BlockSpec((1, 1, nkv, VT_ROWS, tk), lambda b, h, i: (b, h, 0, 0, 0)),
            pl.BlockSpec((1, DIFF_V_DIM), lambda b, h, i: (0, 0)),
        ],
        out_specs=pl.BlockSpec((1, tq, DIFF_V_DIM), lambda b, h, i: (b, i, h)),
        out_shape=jax.ShapeDtypeStruct((B, S, DIFF_V_COLS), BF16),
        scratch_shapes=[
            pltpu.VMEM((tk, 2 * tq), F32),
            pltpu.VMEM((tk, 2 * tq), F32),
            pltpu.VMEM((1, 2 * tq), F32),
            pltpu.VMEM((VT_ROWS, 2 * tq), F32),
        ],
        compiler_params=pltpu.CompilerParams(
            dimension_semantics=("parallel", "parallel", "arbitrary"),
            vmem_limit_bytes=48 * 1024 * 1024),
        name="diff_attn",
    )(lam_p, dqt, dk, dvt, subln_g)


def _swa_kernel(sink_ref, qt_ref, k_ref, vt_ref, o_ref, *, tq):
    i = pl.program_id(1)
    n_cols = SWA_Q_HEADS * WINDOW
    half_cols = n_cols // SWA_KV_HEADS
    sink = sink_ref[...]
    row = lax.broadcasted_iota(jnp.int32, (2 * WINDOW, WINDOW), 0)
    qrel = lax.broadcasted_iota(jnp.int32, (2 * WINDOW, WINDOW), 1)
    band = (row - WINDOW <= qrel) & (row > qrel)
    in_current = row >= WINDOW
    z = jnp.zeros((HEAD_DIM, half_cols), BF16)
    for sub in range(tq // WINDOW):
        n = i * (tq // WINDOW) + sub
        prev = jnp.maximum(n - 1, 0)
        kwin = jnp.concatenate([k_ref[0, prev], k_ref[0, n]], axis=0)
        vtwin = jnp.concatenate([vt_ref[0, prev], vt_ref[0, n]], axis=1)
        qt = qt_ref[0, :, sub * WINDOW:(sub + 1) * WINDOW]
        heads = [qt[h * HEAD_DIM:(h + 1) * HEAD_DIM] for h in range(SWA_Q_HEADS)]
        qw = jnp.concatenate(
            [jnp.concatenate(heads[:SWA_GROUP] + [z], axis=1),
             jnp.concatenate([z] + heads[SWA_GROUP:], axis=1)], axis=0)
        s = jnp.dot(kwin, qw, preferred_element_type=F32)
        valid = band & (in_current | (n >= 1))
        s = jnp.concatenate(
            [jnp.where(valid, s[:, h * WINDOW:(h + 1) * WINDOW], NEG) for h in range(SWA_Q_HEADS)], axis=1)
        m = jnp.maximum(jnp.max(s, axis=0, keepdims=True), sink)
        p = jnp.exp2(s - m).astype(BF16)
        acc = jnp.dot(vtwin, p, preferred_element_type=F32)
        den = acc[SWA_KV_COLS:SWA_KV_COLS + 1] + jnp.exp2(sink - m)
        on = acc[:SWA_KV_COLS] / den
        u = jnp.concatenate([on[:HEAD_DIM, :half_cols], on[HEAD_DIM:, half_cols:]], axis=1)
        for hp in range(SWA_Q_HEADS // 2):
            two = jnp.concatenate([u[:, (2 * hp) * WINDOW:(2 * hp + 1) * WINDOW],
                                   u[:, (2 * hp + 1) * WINDOW:(2 * hp + 2) * WINDOW]], axis=0)
            o_ref[0, sub * WINDOW:(sub + 1) * WINDOW, hp * LANES:(hp + 1) * LANES] = two.T.astype(BF16)


def _swa_call(sink_row, sqt, sk, svt, *, tq):
    B, _, S = sqt.shape
    nb = S // WINDOW
    return pl.pallas_call(
        functools.partial(_swa_kernel, tq=tq),
        grid=(B, S // tq),
        in_specs=[
            pl.BlockSpec(sink_row.shape, lambda b, i: (0, 0)),
            pl.BlockSpec((1, SWA_Q_COLS, tq), lambda b, i: (b, 0, i)),
            pl.BlockSpec((1, nb, WINDOW, SWA_KV_COLS), lambda b, i: (b, 0, 0, 0)),
            pl.BlockSpec((1, nb, SWA_VT_ROWS, WINDOW), lambda b, i: (b, 0, 0, 0)),
        ],
        out_specs=pl.BlockSpec((1, tq, SWA_Q_COLS), lambda b, i: (b, i, 0)),
        out_shape=jax.ShapeDtypeStruct((B, S, SWA_Q_COLS), BF16),
        compiler_params=pltpu.CompilerParams(
            dimension_semantics=("parallel", "arbitrary"), vmem_limit_bytes=40 * 1024 * 1024),
        name="swa_attn",
    )(sink_row, sqt, sk, svt)


def _pack_bf16_pairs(x):
    n = x.shape[1] // 2
    lo = lax.bitcast_convert_type(x[:, :n].astype(BF16).astype(F32), jnp.uint32)
    hi = lax.bitcast_convert_type(x[:, n:].astype(BF16).astype(F32), jnp.uint32)
    return (lo >> 16) | (hi & jnp.uint32(0xFFFF0000))


def _unpack_bf16_pairs(w):
    lo = lax.bitcast_convert_type(w << 16, F32)
    hi = lax.bitcast_convert_type(w & jnp.uint32(0xFFFF0000), F32)
    return jnp.concatenate([lo, hi], axis=1).astype(BF16)


def _mix_kernel(x_ref, od_ref, os_ref, wo_ref, g2_ref, wr_ref, br_ref, x1_ref, n2_ref, rt_ref, cnt_ref):
    h = (x_ref[0]
         + jnp.dot(od_ref[0], wo_ref[:DIFF_V_COLS], preferred_element_type=F32)
         + jnp.dot(os_ref[0], wo_ref[DIFF_V_COLS:], preferred_element_type=F32))
    x1_ref[0] = h
    n2 = h * lax.rsqrt(jnp.mean(h * h, axis=-1, keepdims=True) + EPS) * g2_ref[...]
    n2_ref[0] = _pack_bf16_pairs(n2)
    tm = n2.shape[0]
    n2_hi = n2.astype(BF16)
    n2_lo = (n2 - n2_hi.astype(F32)).astype(BF16)
    parts = jnp.dot(jnp.concatenate([n2_hi, n2_lo], axis=0), wr_ref[...],
                    preferred_element_type=F32)
    logits = ((parts[:tm, :ROUTER_COLS] + parts[tm:, ROUTER_COLS:])
              + (parts[:tm, ROUTER_COLS:] + parts[tm:, :ROUTER_COLS])) + br_ref[...]
    lane = lax.broadcasted_iota(jnp.int32, (tm, ROUTER_COLS), 1)
    big = jnp.int32(ROUTER_COLS)
    gl = jnp.where(lane < N_GROUPS, logits, -jnp.inf)
    gm = jnp.max(gl, axis=-1, keepdims=True)
    p_top = 1.0 / jnp.sum(jnp.exp(gl - gm), axis=-1, keepdims=True)
    g_idx = jnp.min(jnp.where(gl == gm, lane, big), axis=-1, keepdims=True)
    e_lo = N_GROUPS + EXPERTS_PER_GROUP * g_idx
    el = jnp.where((lane >= e_lo) & (lane < e_lo + EXPERTS_PER_GROUP), logits, -jnp.inf)
    v1 = jnp.max(el, axis=-1, keepdims=True)
    i1 = jnp.min(jnp.where(el == v1, lane, big), axis=-1, keepdims=True)
    el2 = jnp.where(lane == i1, -jnp.inf, el)
    v2 = jnp.max(el2, axis=-1, keepdims=True)
    i2 = jnp.min(jnp.where(el2 == v2, lane, big), axis=-1, keepdims=True)
    e21 = jnp.exp(v2 - v1)
    gate1 = p_top / (1.0 + e21)
    gate2 = p_top * e21 / (1.0 + e21)
    rt = jnp.where(lane == 0, (i1 - N_GROUPS).astype(F32),
         jnp.where(lane == 1, (i2 - N_GROUPS).astype(F32),
         jnp.where(lane == 2, gate1, jnp.where(lane == 3, gate2, 0.0))))
    rt_ref[0] = rt
    chosen = ((lane == i1 - N_GROUPS) | (lane == i2 - N_GROUPS)).astype(F32)
    cnt_ref[0, 0] = jnp.broadcast_to(jnp.sum(chosen, axis=0, keepdims=True), cnt_ref.shape[2:])


def _mix_call(x, o_diff, o_swa, w_out, g2, w_router, b_router, *, tm):
    B, S, D = x.shape
    const = lambda b, i: (0, 0)
    row = lambda b, i: (b, i, 0)
    nt = S // tm
    return pl.pallas_call(
        _mix_kernel,
        grid=(B, nt),
        in_specs=[
            pl.BlockSpec((1, tm, D), row),
            pl.BlockSpec((1, tm, DIFF_V_COLS), row),
            pl.BlockSpec((1, tm, SWA_Q_COLS), row),
            pl.BlockSpec(w_out.shape, const),
            pl.BlockSpec((1, D), const),
            pl.BlockSpec(w_router.shape, const),
            pl.BlockSpec((1, ROUTER_COLS), const),
        ],
        out_specs=(pl.BlockSpec((1, tm, D), row), pl.BlockSpec((1, tm, D // 2), row),
                   pl.BlockSpec((1, tm, ROUTER_COLS), row),
                   pl.BlockSpec((1, 1, SUBLANES, ROUTER_COLS), lambda b, i: (b, i, 0, 0))),
        out_shape=(jax.ShapeDtypeStruct((B, S, D), F32), jax.ShapeDtypeStruct((B, S, D // 2), jnp.uint32),
                   jax.ShapeDtypeStruct((B, S, ROUTER_COLS), F32),
                   jax.ShapeDtypeStruct((B, nt, SUBLANES, ROUTER_COLS), F32)),
        compiler_params=pltpu.CompilerParams(
            dimension_semantics=("parallel", "parallel"), vmem_limit_bytes=48 * 1024 * 1024),
        name="outproj_router",
    )(x, o_diff, o_swa, w_out, g2, w_router, b_router)


ROW_UNROLL = 8


def _dispatch_kernel(rt_ref, base_ref, n2_ref, xs_in_hbm, dest_ref, xs_hbm, d_vmem, d_smem, idx_sem, row_sem,
                     *, tm):
    del xs_in_hbm
    rt_t = rt_ref[...].T
    e1 = rt_t[0:1].astype(jnp.int32)
    e2 = rt_t[1:2].astype(jnp.int32)
    eid = lax.broadcasted_iota(jnp.int32, (N_EXPERTS, tm), 0)
    oh1 = eid == e1
    oh2 = eid == e2
    earlier = (lax.broadcasted_iota(jnp.int32, (tm, tm), 0)
               < lax.broadcasted_iota(jnp.int32, (tm, tm), 1)).astype(BF16)
    before = jnp.dot((oh1 | oh2).astype(BF16), earlier, preferred_element_type=F32)
    slot = before + base_ref[0][:, 0:1]
    d1 = jnp.sum(jnp.where(oh1, slot, 0.0), axis=0, keepdims=True).astype(jnp.int32)
    d2 = jnp.sum(jnp.where(oh2, slot, 0.0), axis=0, keepdims=True).astype(jnp.int32)
    d = jnp.concatenate([d1, d2, jnp.zeros((SUBLANES - TOP_K, tm), jnp.int32)], axis=0)
    dest_ref[0] = d
    d_vmem[...] = d
    idx_copy = pltpu.make_async_copy(d_vmem, d_smem, idx_sem)
    idx_copy.start()
    idx_copy.wait()

    def issue(c, carry):
        for u in range(ROW_UNROLL):
            r = c * ROW_UNROLL + u
            for k in range(TOP_K):
                pltpu.make_async_copy(n2_ref.at[pl.ds(r, 1)], xs_hbm.at[pl.ds(d_smem[k, r], 1)], row_sem).start()
        return carry

    lax.fori_loop(0, tm // ROW_UNROLL, issue, 0)
    for k in range(TOP_K):
        pltpu.make_async_copy(n2_ref, xs_hbm.at[pl.ds(0, tm)], row_sem).wait()


def _dispatch_call(rt, tile_base, n2p, xs_zero, *, tm):
    T = n2p.shape[0]
    nt = T // tm
    return pl.pallas_call(
        functools.partial(_dispatch_kernel, tm=tm),
        grid=(nt,),
        in_specs=[
            pl.BlockSpec((tm, ROUTER_COLS), lambda t: (t, 0)),
            pl.BlockSpec((1, N_EXPERTS, LANES), lambda t: (t, 0, 0)),
            pl.BlockSpec((tm,) + n2p.shape[1:], lambda t: (t, 0)),
            pl.BlockSpec(memory_space=pl.ANY),
        ],
        out_specs=(pl.BlockSpec((1, SUBLANES, tm), lambda t: (t, 0, 0)),
                   pl.BlockSpec(memory_space=pl.ANY)),
        out_shape=(jax.ShapeDtypeStruct((nt, SUBLANES, tm), jnp.int32),
                   jax.ShapeDtypeStruct(xs_zero.shape, xs_zero.dtype)),
        input_output_aliases={3: 1},
        scratch_shapes=[
            pltpu.VMEM((SUBLANES, tm), jnp.int32),
            pltpu.SMEM((SUBLANES, tm), jnp.int32),
            pltpu.SemaphoreType.DMA(()),
            pltpu.SemaphoreType.DMA(()),
        ],
        compiler_params=pltpu.CompilerParams(
            dimension_semantics=("arbitrary",), vmem_limit_bytes=40 * 1024 * 1024),
        name="moe_dispatch",
    )(rt, tile_base, n2p, xs_zero)


def _expert_kernel(be_ref, nused_ref, xs_ref, wg_ref, wu_ref, wd_ref, y_ref, wg_b, wu_b, wd_b):
    b = pl.program_id(0)

    @pl.when(b < nused_ref[0])
    def _():
        @pl.when((b == 0) | (be_ref[b] != be_ref[jnp.maximum(b - 1, 0)]))
        def _():
            wg_b[...] = wg_ref[0].astype(BF16)
            wu_b[...] = wu_ref[0].astype(BF16)
            wd_b[...] = wd_ref[0].astype(BF16)

        for c in range(EXPERT_BLOCK // EXPERT_CHUNK):
            rows = pl.ds(c * EXPERT_CHUNK, EXPERT_CHUNK)
            xb = _unpack_bf16_pairs(xs_ref[rows, :])
            gate = jnp.dot(xb, wg_b[...], preferred_element_type=F32)
            up = jnp.dot(xb, wu_b[...], preferred_element_type=F32)
            hid = (gate * jax.nn.sigmoid(gate) * up).astype(BF16)
            y_ref[rows, :] = jnp.dot(hid, wd_b[...], preferred_element_type=F32)

    @pl.when(b >= nused_ref[0])
    def _():
        y_ref[...] = jnp.zeros_like(y_ref)


def _expert_call(block_expert, n_used, xs, w_gate, w_up, w_down):
    P = xs.shape[0]
    NB = P // EXPERT_BLOCK
    E, D, F = w_gate.shape
    grid_spec = pltpu.PrefetchScalarGridSpec(
        num_scalar_prefetch=2,
        grid=(NB,),
        in_specs=[
            pl.BlockSpec((EXPERT_BLOCK,) + xs.shape[1:], lambda b, be, nu: (b, 0)),
            pl.BlockSpec((1, D, F), lambda b, be, nu: (be[b], 0, 0)),
            pl.BlockSpec((1, D, F), lambda b, be, nu: (be[b], 0, 0)),
            pl.BlockSpec((1, F, D), lambda b, be, nu: (be[b], 0, 0)),
        ],
        out_specs=pl.BlockSpec((EXPERT_BLOCK, D), lambda b, be, nu: (b, 0)),
        scratch_shapes=[
            pltpu.VMEM((D, F), BF16),
            pltpu.VMEM((D, F), BF16),
            pltpu.VMEM((F, D), BF16),
        ],
    )
    return pl.pallas_call(
        _expert_kernel,
        grid_spec=grid_spec,
        out_shape=jax.ShapeDtypeStruct((P, D), F32),
        compiler_params=pltpu.CompilerParams(
            dimension_semantics=("arbitrary",), vmem_limit_bytes=48 * 1024 * 1024),
        name="moe_experts",
    )(block_expert, n_used, xs, w_gate, w_up, w_down)


def _combine_kernel(dest_hbm, x1_ref, rt_ref, ys_hbm, fg_ref, o_ref, d_smem, ybuf, idx_sem, row_sem,
                    *, tm, final_norm):
    t = pl.program_id(0)
    nt = pl.num_programs(0)
    cur = t % 2
    nxt = 1 - cur

    def idx_copy(tile, s):
        return pltpu.make_async_copy(dest_hbm.at[tile], d_smem.at[s], idx_sem.at[s])

    def issue_rows(s):
        def issue(c, carry):
            for u in range(ROW_UNROLL):
                r = c * ROW_UNROLL + u
                for k in range(TOP_K):
                    pltpu.make_async_copy(ys_hbm.at[pl.ds(d_smem[s, k, r], 1)],
                                          ybuf.at[s, k, pl.ds(r, 1)], row_sem.at[s]).start()
            return carry

        lax.fori_loop(0, tm // ROW_UNROLL, issue, 0)

    @pl.when(t == 0)
    def _():
        first = idx_copy(0, 0)
        first.start()
        first.wait()
        issue_rows(0)

        @pl.when(nt > 1)
        def _():
            idx_copy(1, 1).start()

    @pl.when(t + 1 < nt)
    def _():
        idx_copy(t + 1, nxt).wait()
        issue_rows(nxt)

    @pl.when(t + 2 < nt)
    def _():
        idx_copy(t + 2, cur).start()

    for k in range(TOP_K):
        pltpu.make_async_copy(ys_hbm.at[pl.ds(0, tm)], ybuf.at[cur, k], row_sem.at[cur]).wait()
    rt = rt_ref[...]
    h = x1_ref[...] + rt[:, 2:3] * ybuf[cur, 0] + rt[:, 3:4] * ybuf[cur, 1]
    if final_norm:
        h = h * lax.rsqrt(jnp.mean(h * h, axis=-1, keepdims=True) + EPS) * fg_ref[...]
    o_ref[...] = h


def _combine_call(dest, x1, rt, ys, final_g, *, tm, final_norm):
    T, D = x1.shape
    return pl.pallas_call(
        functools.partial(_combine_kernel, tm=tm, final_norm=final_norm),
        grid=(T // tm,),
        in_specs=[
            pl.BlockSpec(memory_space=pl.ANY),
            pl.BlockSpec((tm, D), lambda t: (t, 0)),
            pl.BlockSpec((tm, ROUTER_COLS), lambda t: (t, 0)),
            pl.BlockSpec(memory_space=pl.ANY),
            pl.BlockSpec((1, D), lambda t: (0, 0)),
        ],
        out_specs=pl.BlockSpec((tm, D), lambda t: (t, 0)),
        out_shape=jax.ShapeDtypeStruct((T, D), F32),
        scratch_shapes=[
            pltpu.SMEM((2, SUBLANES, tm), jnp.int32),
            pltpu.VMEM((2, TOP_K, tm, D), F32),
            pltpu.SemaphoreType.DMA((2,)),
            pltpu.SemaphoreType.DMA((2,)),
        ],
        compiler_params=pltpu.CompilerParams(
            dimension_semantics=("arbitrary",), vmem_limit_bytes=40 * 1024 * 1024),
        name="moe_combine",
    )(dest, x1, rt, ys, final_g)


def _slot_layout(tile_counts, n_assign):
    NB = -(-n_assign // EXPERT_BLOCK) + N_EXPERTS
    counts = jnp.sum(tile_counts, axis=0)
    padded = ((counts + EXPERT_BLOCK - 1) // EXPERT_BLOCK) * EXPERT_BLOCK
    pad_end = jnp.cumsum(padded)
    pad_start = pad_end - padded
    tile_base = pad_start[None, :] + jnp.cumsum(tile_counts, axis=0) - tile_counts
    block_start = jnp.arange(NB, dtype=jnp.int32) * EXPERT_BLOCK
    block_expert = jnp.minimum(jnp.sum(pad_end[None, :] <= block_start[:, None], axis=1),
                               N_EXPERTS - 1).astype(jnp.int32)
    n_used = (pad_end[-1] // EXPERT_BLOCK).astype(jnp.int32).reshape(1)
    return NB, block_expert, n_used, tile_base


def _rope_tables(S):
    inv = 1.0 / (ROPE_THETA ** (jnp.arange(0, HEAD_DIM, 2, dtype=F32) / HEAD_DIM))
    ang = jnp.arange(S, dtype=F32)[:, None] * inv[None, :]
    cos, sin = jnp.cos(ang), jnp.sin(ang)
    cos_l = jnp.tile(cos, (1, LANES // (HEAD_DIM // 2)))
    sin_l = jnp.tile(jnp.concatenate([-sin, sin], axis=1), (1, LANES // HEAD_DIM))
    return cos_l, sin_l, cos.T, sin.T


def kernel(x, norm1_g, w_in, lambda_q1, lambda_k1, lambda_q2, lambda_k2, subln_g, sinks, w_out,
           norm2_g, w_router_group, b_router_group, w_router_expert, b_router_expert,
           w_gate, w_up, w_down, final_g):
    B, S, D = x.shape
    T = B * S
    depth = w_in.shape[0]
    tq, tk = 512, 512
    tm_proj = 512
    tm_tok = 512
    tq_swa = 512
    qscale = HEAD_DIM ** -0.5 * math.log2(math.e)
    cos_l, sin_l, cos_t, sin_t = _rope_tables(S)

    c0 = DIFF_QK_COLS
    c1 = 2 * DIFF_QK_COLS
    c2 = c1 + DIFF_V_COLS
    c3 = c2 + SWA_Q_COLS
    c4 = c3 + SWA_KV_COLS
    for l in range(depth):
        lambda_init = 0.8 - 0.6 * math.exp(-0.3 * l)
        w = w_in[l]
        w_nat = jnp.concatenate([w[:, c0:c1], w[:, c3:c4]], axis=1).astype(BF16)
        w_tr = jnp.concatenate([w[:, :c0] * qscale, w[:, c1:c2], w[:, c2:c3] * qscale, w[:, c4:]],
                               axis=1).T.astype(BF16)
        dqt, dk, dvt, sqt, sk, svt = _proj_call(
            x, norm1_g[l][None, :], w_nat, w_tr, cos_l, sin_l, cos_t, sin_t, tm=tm_proj, tk=tk)

        lam_p = jnp.stack([lambda_q1[l], lambda_k1[l], lambda_q2[l], lambda_k2[l]]).astype(F32)
        o_diff = _diff_call(lam_p, dqt, dk, dvt, subln_g[l][None, :].astype(F32),
                            tq=tq, tk=tk, lambda_init=lambda_init)
        sink_row = jnp.repeat(sinks[l].astype(F32) * math.log2(math.e), WINDOW)[None, :]
        o_swa = _swa_call(sink_row, sqt, sk, svt, tq=tq_swa)

        wo_b = w_out[l].astype(BF16)
        w_router = jnp.zeros((D, ROUTER_COLS), F32)
        w_router = w_router.at[:, :N_GROUPS].set(w_router_group[l])
        w_router = w_router.at[:, N_GROUPS:N_GROUPS + N_EXPERTS].set(w_router_expert[l])
        w_router_hi = w_router.astype(BF16)
        w_router_lo = (w_router - w_router_hi.astype(F32)).astype(BF16)
        w_router = jnp.concatenate([w_router_hi, w_router_lo], axis=1)
        b_router = jnp.zeros((1, ROUTER_COLS), F32)
        b_router = b_router.at[0, :N_GROUPS].set(b_router_group[l])
        b_router = b_router.at[0, N_GROUPS:N_GROUPS + N_EXPERTS].set(b_router_expert[l])
        x1, n2p, rt, cnt = _mix_call(x, o_diff, o_swa, wo_b, norm2_g[l][None, :], w_router, b_router, tm=tm_tok)

        rt2 = rt.reshape(T, ROUTER_COLS)
        tile_counts = cnt[:, :, 0, :N_EXPERTS].reshape(T // tm_tok, N_EXPERTS).astype(jnp.int32)
        NB, block_expert, n_used, tile_base = _slot_layout(tile_counts, T * TOP_K)
        tile_base = jnp.broadcast_to(tile_base.astype(F32)[:, :, None], (T // tm_tok, N_EXPERTS, LANES))
        xs_zero = jnp.zeros((NB * EXPERT_BLOCK,) + n2p.shape[2:], jnp.uint32)
        dest, xs = _dispatch_call(rt2, tile_base, n2p.reshape((T,) + n2p.shape[2:]), xs_zero, tm=tm_tok)
        ys = _expert_call(block_expert, n_used, xs, w_gate[l], w_up[l], w_down[l])
        x = _combine_call(dest, x1.reshape(T, D), rt2, ys, final_g[None, :],
                          tm=tm_tok, final_norm=(l == depth - 1)).reshape(B, S, D)
    return x
```

```python
import functools
import math

import jax
import jax.numpy as jnp
from jax import lax
from jax.experimental import pallas as pl
from jax.experimental.pallas import tpu as pltpu

HEAD_DIM = 64
DIFF_HEADS = 4
DIFF_V_DIM = 2 * HEAD_DIM
SWA_Q_HEADS = 8
SWA_KV_HEADS = 2
SWA_GROUP = SWA_Q_HEADS // SWA_KV_HEADS
WINDOW = 128
ROPE_THETA = 10000.0
N_GROUPS = 4
EXPERTS_PER_GROUP = 8
N_EXPERTS = N_GROUPS * EXPERTS_PER_GROUP
TOP_K = 2
EXPERT_BLOCK = 512
EXPERT_CHUNK = 256
EPS = 1e-6
NEG = -1e30

DIFF_QK_COLS = DIFF_HEADS * 2 * HEAD_DIM
DIFF_V_COLS = DIFF_HEADS * DIFF_V_DIM
SWA_Q_COLS = SWA_Q_HEADS * HEAD_DIM
SWA_KV_COLS = SWA_KV_HEADS * HEAD_DIM
LANES = 128
SUBLANES = 8
BF16_SUBLANES = 16
VT_ROWS = DIFF_V_DIM + BF16_SUBLANES
SWA_VT_ROWS = SWA_KV_COLS + BF16_SUBLANES
ROUTER_COLS = LANES
DIFF_UNROLL = 4
DIFF_S_BUFS = 4

BF16 = jnp.bfloat16
F32 = jnp.float32


def _rope_lanes(x, cos_l, sin_l, first_half):
    rot = jnp.where(first_half, pltpu.roll(x, 96, 1), pltpu.roll(x, 32, 1))
    return x * cos_l + rot * sin_l


def _proj_kernel(x_ref, g_ref, wnat_ref, wtr_ref, cosl_ref, sinl_ref, cost_ref, sint_ref,
                 dqt_ref, dk_ref, dvt_ref, sqt_ref, sk_ref, svt_ref, *, tk):
    x = x_ref[0]
    tm = x.shape[0]
    n1 = x * lax.rsqrt(jnp.mean(x * x, axis=-1, keepdims=True) + EPS) * g_ref[...]
    n1b = n1.astype(BF16)
    nat = jnp.dot(n1b, wnat_ref[...], preferred_element_type=F32)
    tr = lax.dot_general(wtr_ref[...], n1b, (((1,), (1,)), ((), ())),
                         preferred_element_type=F32)

    cos_l, sin_l = cosl_ref[...], sinl_ref[...]
    first_half = (lax.broadcasted_iota(jnp.int32, (tm, LANES), 1) & (HEAD_DIM - 1)) < HEAD_DIM // 2
    for h in range(DIFF_HEADS):
        slab = nat[:, h * LANES:(h + 1) * LANES]
        dk_ref[0, h] = _rope_lanes(slab, cos_l, sin_l, first_half).astype(BF16)
    sk = _rope_lanes(nat[:, DIFF_QK_COLS:DIFF_QK_COLS + LANES], cos_l, sin_l, first_half).astype(BF16)
    for c in range(tm // WINDOW):
        sk_ref[0, c] = sk[c * WINDOW:(c + 1) * WINDOW]

    cos_t, sin_t = cost_ref[...], sint_ref[...]
    half = HEAD_DIM // 2

    def rope_rows(r0):
        x1 = tr[r0:r0 + half]
        x2 = tr[r0 + half:r0 + HEAD_DIM]
        return (x1 * cos_t - x2 * sin_t).astype(BF16), (x1 * sin_t + x2 * cos_t).astype(BF16)

    for h in range(DIFF_HEADS):
        for c in range(2):
            lo, hi = rope_rows(h * 2 * HEAD_DIM + c * HEAD_DIM)
            dqt_ref[0, h, c * HEAD_DIM:c * HEAD_DIM + half] = lo
            dqt_ref[0, h, c * HEAD_DIM + half:(c + 1) * HEAD_DIM] = hi
    ones_rows = (lax.broadcasted_iota(jnp.int32, (BF16_SUBLANES, tk), 0) == 0).astype(BF16)
    for h in range(DIFF_HEADS):
        r0 = DIFF_QK_COLS + h * DIFF_V_DIM
        for c in range(tm // tk):
            dvt_ref[0, h, c, :DIFF_V_DIM] = tr[r0:r0 + DIFF_V_DIM, c * tk:(c + 1) * tk].astype(BF16)
            dvt_ref[0, h, c, DIFF_V_DIM:] = ones_rows

    r0 = DIFF_QK_COLS + DIFF_V_COLS
    for h in range(SWA_Q_HEADS):
        lo, hi = rope_rows(r0 + h * HEAD_DIM)
        sqt_ref[0, h * HEAD_DIM:h * HEAD_DIM + half] = lo
        sqt_ref[0, h * HEAD_DIM + half:(h + 1) * HEAD_DIM] = hi
    r0 += SWA_Q_COLS
    for c in range(tm // WINDOW):
        svt_ref[0, c, :SWA_KV_COLS] = tr[r0:r0 + SWA_KV_COLS, c * WINDOW:(c + 1) * WINDOW].astype(BF16)
        svt_ref[0, c, SWA_KV_COLS:] = ones_rows[:, :WINDOW]


def _proj_call(x, g1, w_nat, w_tr, cos_l, sin_l, cos_t, sin_t, *, tm, tk):
    B, S, D = x.shape
    nkv = S // tk
    grid = (B, S // tm)
    const = lambda b, i: (0, 0)
    out_shape = (
        jax.ShapeDtypeStruct((B, DIFF_HEADS, 2 * HEAD_DIM, S), BF16),
        jax.ShapeDtypeStruct((B, DIFF_HEADS, S, 2 * HEAD_DIM), BF16),
        jax.ShapeDtypeStruct((B, DIFF_HEADS, nkv, VT_ROWS, tk), BF16),
        jax.ShapeDtypeStruct((B, SWA_Q_COLS, S), BF16),
        jax.ShapeDtypeStruct((B, S // WINDOW, WINDOW, SWA_KV_COLS), BF16),
        jax.ShapeDtypeStruct((B, S // WINDOW, SWA_VT_ROWS, WINDOW), BF16),
    )
    return pl.pallas_call(
        functools.partial(_proj_kernel, tk=tk),
        grid=grid,
        in_specs=[
            pl.BlockSpec((1, tm, D), lambda b, i: (b, i, 0)),
            pl.BlockSpec((1, D), const),
            pl.BlockSpec(w_nat.shape, const),
            pl.BlockSpec(w_tr.shape, const),
            pl.BlockSpec((tm, LANES), lambda b, i: (i, 0)),
            pl.BlockSpec((tm, LANES), lambda b, i: (i, 0)),
            pl.BlockSpec((HEAD_DIM // 2, tm), lambda b, i: (0, i)),
            pl.BlockSpec((HEAD_DIM // 2, tm), lambda b, i: (0, i)),
        ],
        out_specs=(
            pl.BlockSpec((1, DIFF_HEADS, 2 * HEAD_DIM, tm), lambda b, i: (b, 0, 0, i)),
            pl.BlockSpec((1, DIFF_HEADS, tm, 2 * HEAD_DIM), lambda b, i: (b, 0, i, 0)),
            pl.BlockSpec((1, DIFF_HEADS, tm // tk, VT_ROWS, tk), lambda b, i: (b, 0, i, 0, 0)),
            pl.BlockSpec((1, SWA_Q_COLS, tm), lambda b, i: (b, 0, i)),
            pl.BlockSpec((1, tm // WINDOW, WINDOW, SWA_KV_COLS), lambda b, i: (b, i, 0, 0)),
            pl.BlockSpec((1, tm // WINDOW, SWA_VT_ROWS, WINDOW), lambda b, i: (b, i, 0, 0)),
        ),
        out_shape=out_shape,
        compiler_params=pltpu.CompilerParams(
            dimension_semantics=("parallel", "parallel"), vmem_limit_bytes=48 * 1024 * 1024),
        name="proj_rope",
    )(x, g1, w_nat, w_tr, cos_l, sin_l, cos_t, sin_t)


def _diff_kernel(lam_ref, qt_ref, k_ref, vt_ref, g_ref, o_ref, *scratch, tq, tk, lambda_init):
    i = pl.program_id(2)
    s_bufs, (m_ref, acc_ref) = scratch[:DIFF_S_BUFS], scratch[DIFF_S_BUFS:]
    qt = qt_ref[0, 0]
    z = jnp.zeros((HEAD_DIM, tq), BF16)
    qw = jnp.concatenate([jnp.concatenate([qt[:HEAD_DIM], z], axis=1),
                          jnp.concatenate([z, qt[HEAD_DIM:]], axis=1)], axis=0)

    def scores(j, par):
        kt = k_ref[0, 0, pl.ds(pl.multiple_of(j * tk, tk), tk), :]
        s_bufs[par][...] = jnp.dot(kt, qw, preferred_element_type=F32)

    def absorb(j, par, masked):
        s = s_bufs[par][...]
        if masked:
            kpos = j * tk + lax.broadcasted_iota(jnp.int32, (tk, 2 * tq), 0)
            qpos = i * tq + (lax.broadcasted_iota(jnp.int32, (tk, 2 * tq), 1) & (tq - 1))
            s = jnp.where(kpos <= qpos, s, NEG)
        m = m_ref[...]
        m_new = jnp.maximum(m, jnp.max(s, axis=0, keepdims=True))
        alpha = jnp.exp2(m - m_new)
        p = jnp.exp2(s - m_new).astype(BF16)
        m_ref[...] = m_new
        pv = jnp.dot(vt_ref[0, 0, j], p, preferred_element_type=F32)
        acc_ref[...] = alpha * acc_ref[...] + pv

    m_ref[...] = jnp.full(m_ref.shape, NEG, F32)
    acc_ref[...] = jnp.zeros(acc_ref.shape, F32)

    nfull = (i * tq) // tk
    n_diag = max(tq // tk, 1)
    scores(0, 0)

    def group(t, c):
        j = DIFF_UNROLL * t
        for idx in range(DIFF_UNROLL):
            scores(j + idx + 1, (idx + 1) % DIFF_S_BUFS)
            absorb(j + idx, idx % DIFF_S_BUFS, False)
        return c

    lax.fori_loop(0, nfull // DIFF_UNROLL, group, 0)

    def tail(first, masks):
        for idx, masked in enumerate(masks):
            if idx + 1 < len(masks):
                scores(first + idx + 1, (idx + 1) % DIFF_S_BUFS)
            absorb(first + idx, idx % DIFF_S_BUFS, masked)

    for rem in range(DIFF_UNROLL):
        @pl.when(nfull % DIFF_UNROLL == rem)
        def _():
            tail(nfull - rem, [False] * rem + [True] * n_diag)

    lam_p = lam_ref[...]
    lam = (jnp.exp(jnp.sum(lam_p[0:1] * lam_p[1:2], axis=-1, keepdims=True))
           - jnp.exp(jnp.sum(lam_p[2:3] * lam_p[3:4], axis=-1, keepdims=True)) + lambda_init)
    l = acc_ref[DIFF_V_DIM:DIFF_V_DIM + 1, :]
    o = (acc_ref[:DIFF_V_DIM, :tq] / l[:, :tq]
         - lam * (acc_ref[:DIFF_V_DIM, tq:] / l[:, tq:]))
    o = o * lax.rsqrt(jnp.mean(o * o, axis=0, keepdims=True) + EPS)
    o_ref[0] = (o.T * g_ref[...] * (1.0 - lambda_init)).astype(BF16)


def _diff_call(lam_p, dqt, dk, dvt, subln_g, *, tq, tk, lambda_init):
    B, H, _, S = dqt.shape
    nkv = S // tk
    grid = (B, H, S // tq)
    return pl.pallas_call(
        functools.partial(_diff_kernel, tq=tq, tk=tk, lambda_init=lambda_init),
        grid=grid,
        in_specs=[
            pl.BlockSpec(lam_p.shape, lambda b, h, i: (0, 0)),
            pl.BlockSpec((1, 1, 2 * HEAD_DIM, tq), lambda b, h, i: (b, h, 0, i)),
            pl.BlockSpec((1, 1, S, 2 * HEAD_DIM), lambda b, h, i: (b, h, 0, 0)),
            pl.BlockSpec((1, 1, nkv, VT_ROWS, tk), lambda b, h, i: (b, h, 0, 0, 0)),
            pl.BlockSpec((1, DIFF_V_DIM), lambda b, h, i: (0, 0)),
        ],
        out_specs=pl.BlockSpec((1, tq, DIFF_V_DIM), lambda b, h, i: (b, i, h)),
        out_shape=jax.ShapeDtypeStruct((B, S, DIFF_V_COLS), BF16),
        scratch_shapes=[pltpu.VMEM((tk, 2 * tq), F32)] * DIFF_S_BUFS + [
            pltpu.VMEM((1, 2 * tq), F32),
            pltpu.VMEM((VT_ROWS, 2 * tq), F32),
        ],
        compiler_params=pltpu.CompilerParams(
            dimension_semantics=("parallel", "parallel", "arbitrary"),
            vmem_limit_bytes=48 * 1024 * 1024),
        name="diff_attn",
    )(lam_p, dqt, dk, dvt, subln_g)


def _swa_kernel(sink_ref, qt_ref, k_ref, vt_ref, o_ref, *, tq):
    i = pl.program_id(1)
    n_cols = SWA_Q_HEADS * WINDOW
    half_cols = n_cols // SWA_KV_HEADS
    sink = sink_ref[...]
    row = lax.broadcasted_iota(jnp.int32, (2 * WINDOW, WINDOW), 0)
    qrel = lax.broadcasted_iota(jnp.int32, (2 * WINDOW, WINDOW), 1)
    band = (row - WINDOW <= qrel) & (row > qrel)
    in_current = row >= WINDOW
    z = jnp.zeros((HEAD_DIM, half_cols), BF16)
    for sub in range(tq // WINDOW):
        n = i * (tq // WINDOW) + sub
        prev = jnp.maximum(n - 1, 0)
        kwin = jnp.concatenate([k_ref[0, prev], k_ref[0, n]], axis=0)
        vtwin = jnp.concatenate([vt_ref[0, prev], vt_ref[0, n]], axis=1)
        qt = qt_ref[0, :, sub * WINDOW:(sub + 1) * WINDOW]
        heads = [qt[h * HEAD_DIM:(h + 1) * HEAD_DIM] for h in range(SWA_Q_HEADS)]
        qw = jnp.concatenate(
            [jnp.concatenate(heads[:SWA_GROUP] + [z], axis=1),
             jnp.concatenate([z] + heads[SWA_GROUP:], axis=1)], axis=0)
        s = jnp.dot(kwin, qw, preferred_element_type=F32)
        valid = band & (in_current | (n >= 1))
        s = jnp.concatenate(
            [jnp.where(valid, s[:, h * WINDOW:(h + 1) * WINDOW], NEG) for h in range(SWA_Q_HEADS)], axis=1)
        m = jnp.maximum(jnp.max(s, axis=0, keepdims=True), sink)
        p = jnp.exp2(s - m).astype(BF16)
        acc = jnp.dot(vtwin, p, preferred_element_type=F32)
        den = acc[SWA_KV_COLS:SWA_KV_COLS + 1] + jnp.exp2(sink - m)
        on = acc[:SWA_KV_COLS] / den
        u = jnp.concatenate([on[:HEAD_DIM, :half_cols], on[HEAD_DIM:, half_cols:]], axis=1)
        for hp in range(SWA_Q_HEADS // 2):
            two = jnp.concatenate([u[:, (2 * hp) * WINDOW:(2 * hp + 1) * WINDOW],
                                   u[:, (2 * hp + 1) * WINDOW:(2 * hp + 2) * WINDOW]], axis=0)
            o_ref[0, sub * WINDOW:(sub + 1) * WINDOW, hp * LANES:(hp + 1) * LANES] = two.T.astype(BF16)


def _swa_call(sink_row, sqt, sk, svt, *, tq):
    B, _, S = sqt.shape
    nb = S // WINDOW
    return pl.pallas_call(
        functools.partial(_swa_kernel, tq=tq),
        grid=(B, S // tq),
        in_specs=[
            pl.BlockSpec(sink_row.shape, lambda b, i: (0, 0)),
            pl.BlockSpec((1, SWA_Q_COLS, tq), lambda b, i: (b, 0, i)),
            pl.BlockSpec((1, nb, WINDOW, SWA_KV_COLS), lambda b, i: (b, 0, 0, 0)),
            pl.BlockSpec((1, nb, SWA_VT_ROWS, WINDOW), lambda b, i: (b, 0, 0, 0)),
        ],
        out_specs=pl.BlockSpec((1, tq, SWA_Q_COLS), lambda b, i: (b, i, 0)),
        out_shape=jax.ShapeDtypeStruct((B, S, SWA_Q_COLS), BF16),
        compiler_params=pltpu.CompilerParams(
            dimension_semantics=("parallel", "arbitrary"), vmem_limit_bytes=40 * 1024 * 1024),
        name="swa_attn",
    )(sink_row, sqt, sk, svt)


def _pack_bf16_pairs(x):
    n = x.shape[1] // 2
    lo = lax.bitcast_convert_type(x[:, :n].astype(BF16).astype(F32), jnp.uint32)
    hi = lax.bitcast_convert_type(x[:, n:].astype(BF16).astype(F32), jnp.uint32)
    return (lo >> 16) | (hi & jnp.uint32(0xFFFF0000))


def _unpack_bf16_pairs(w):
    lo = lax.bitcast_convert_type(w << 16, F32)
    hi = lax.bitcast_convert_type(w & jnp.uint32(0xFFFF0000), F32)
    return jnp.concatenate([lo, hi], axis=1).astype(BF16)


def _mix_kernel(x_ref, od_ref, os_ref, wo_ref, g2_ref, wr_ref, br_ref, x1_ref, n2_ref, rt_ref, cnt_ref):
    h = (x_ref[0]
         + jnp.dot(od_ref[0], wo_ref[:DIFF_V_COLS], preferred_element_type=F32)
         + jnp.dot(os_ref[0], wo_ref[DIFF_V_COLS:], preferred_element_type=F32))
    x1_ref[0] = h
    n2 = h * lax.rsqrt(jnp.mean(h * h, axis=-1, keepdims=True) + EPS) * g2_ref[...]
    n2_ref[0] = _pack_bf16_pairs(n2)
    tm = n2.shape[0]
    n2_hi = n2.astype(BF16)
    n2_lo = (n2 - n2_hi.astype(F32)).astype(BF16)
    parts = jnp.dot(jnp.concatenate([n2_hi, n2_lo], axis=0), wr_ref[...],
                    preferred_element_type=F32)
    logits = ((parts[:tm, :ROUTER_COLS] + parts[tm:, ROUTER_COLS:])
              + (parts[:tm, ROUTER_COLS:] + parts[tm:, :ROUTER_COLS])) + br_ref[...]
    lane = lax.broadcasted_iota(jnp.int32, (tm, ROUTER_COLS), 1)
    big = jnp.int32(ROUTER_COLS)
    gl = jnp.where(lane < N_GROUPS, logits, -jnp.inf)
    gm = jnp.max(gl, axis=-1, keepdims=True)
    p_top = 1.0 / jnp.sum(jnp.exp(gl - gm), axis=-1, keepdims=True)
    g_idx = jnp.min(jnp.where(gl == gm, lane, big), axis=-1, keepdims=True)
    e_lo = N_GROUPS + EXPERTS_PER_GROUP * g_idx
    el = jnp.where((lane >= e_lo) & (lane < e_lo + EXPERTS_PER_GROUP), logits, -jnp.inf)
    v1 = jnp.max(el, axis=-1, keepdims=True)
    i1 = jnp.min(jnp.where(el == v1, lane, big), axis=-1, keepdims=True)
    el2 = jnp.where(lane == i1, -jnp.inf, el)
    v2 = jnp.max(el2, axis=-1, keepdims=True)
    i2 = jnp.min(jnp.where(el2 == v2, lane, big), axis=-1, keepdims=True)
    e21 = jnp.exp(v2 - v1)
    gate1 = p_top / (1.0 + e21)
    gate2 = p_top * e21 / (1.0 + e21)
    rt = jnp.where(lane == 0, (i1 - N_GROUPS).astype(F32),
         jnp.where(lane == 1, (i2 - N_GROUPS).astype(F32),
         jnp.where(lane == 2, gate1, jnp.where(lane == 3, gate2, 0.0))))
    rt_ref[0] = rt
    chosen = ((lane == i1 - N_GROUPS) | (lane == i2 - N_GROUPS)).astype(F32)
    cnt_ref[0, 0] = jnp.broadcast_to(jnp.sum(chosen, axis=0, keepdims=True), cnt_ref.shape[2:])


def _mix_call(x, o_diff, o_swa, w_out, g2, w_router, b_router, *, tm):
    B, S, D = x.shape
    const = lambda b, i: (0, 0)
    row = lambda b, i: (b, i, 0)
    nt = S // tm
    return pl.pallas_call(
        _mix_kernel,
        grid=(B, nt),
        in_specs=[
            pl.BlockSpec((1, tm, D), row),
            pl.BlockSpec((1, tm, DIFF_V_COLS), row),
            pl.BlockSpec((1, tm, SWA_Q_COLS), row),
            pl.BlockSpec(w_out.shape, const),
            pl.BlockSpec((1, D), const),
            pl.BlockSpec(w_router.shape, const),
            pl.BlockSpec((1, ROUTER_COLS), const),
        ],
        out_specs=(pl.BlockSpec((1, tm, D), row), pl.BlockSpec((1, tm, D // 2), row),
                   pl.BlockSpec((1, tm, ROUTER_COLS), row),
                   pl.BlockSpec((1, 1, SUBLANES, ROUTER_COLS), lambda b, i: (b, i, 0, 0))),
        out_shape=(jax.ShapeDtypeStruct((B, S, D), F32), jax.ShapeDtypeStruct((B, S, D // 2), jnp.uint32),
                   jax.ShapeDtypeStruct((B, S, ROUTER_COLS), F32),
                   jax.ShapeDtypeStruct((B, nt, SUBLANES, ROUTER_COLS), F32)),
        compiler_params=pltpu.CompilerParams(
            dimension_semantics=("parallel", "parallel"), vmem_limit_bytes=48 * 1024 * 1024),
        name="outproj_router",
    )(x, o_diff, o_swa, w_out, g2, w_router, b_router)


ROW_UNROLL = 8


def _dispatch_kernel(rt_ref, base_ref, n2_ref, xs_in_hbm, dest_ref, xs_hbm, d_vmem, d_smem, idx_sem, row_sem,
                     *, tm):
    del xs_in_hbm
    rt_t = rt_ref[...].T
    e1 = rt_t[0:1].astype(jnp.int32)
    e2 = rt_t[1:2].astype(jnp.int32)
    eid = lax.broadcasted_iota(jnp.int32, (N_EXPERTS, tm), 0)
    oh1 = eid == e1
    oh2 = eid == e2
    earlier = (lax.broadcasted_iota(jnp.int32, (tm, tm), 0)
               < lax.broadcasted_iota(jnp.int32, (tm, tm), 1)).astype(BF16)
    before = jnp.dot((oh1 | oh2).astype(BF16), earlier, preferred_element_type=F32)
    slot = before + base_ref[0][:, 0:1]
    d1 = jnp.sum(jnp.where(oh1, slot, 0.0), axis=0, keepdims=True).astype(jnp.int32)
    d2 = jnp.sum(jnp.where(oh2, slot, 0.0), axis=0, keepdims=True).astype(jnp.int32)
    d = jnp.concatenate([d1, d2, jnp.zeros((SUBLANES - TOP_K, tm), jnp.int32)], axis=0)
    dest_ref[0] = d
    d_vmem[...] = d
    idx_copy = pltpu.make_async_copy(d_vmem, d_smem, idx_sem)
    idx_copy.start()
    idx_copy.wait()

    def issue(c, carry):
        r0 = pl.multiple_of(c * ROW_UNROLL, ROW_UNROLL)
        rows = n2_ref.at[pl.ds(r0, ROW_UNROLL)]
        for u in range(ROW_UNROLL):
            for k in range(TOP_K):
                pltpu.make_async_copy(rows.at[pl.ds(u, 1)], xs_hbm.at[pl.ds(d_smem[k, r0 + u], 1)], row_sem).start()
        return carry

    lax.fori_loop(0, tm // ROW_UNROLL, issue, 0)
    for k in range(TOP_K):
        pltpu.make_async_copy(n2_ref, xs_hbm.at[pl.ds(0, tm)], row_sem).wait()


def _dispatch_call(rt, tile_base, n2p, xs_zero, *, tm):
    T = n2p.shape[0]
    nt = T // tm
    return pl.pallas_call(
        functools.partial(_dispatch_kernel, tm=tm),
        grid=(nt,),
        in_specs=[
            pl.BlockSpec((tm, ROUTER_COLS), lambda t: (t, 0)),
            pl.BlockSpec((1, N_EXPERTS, LANES), lambda t: (t, 0, 0)),
            pl.BlockSpec((tm,) + n2p.shape[1:], lambda t: (t, 0)),
            pl.BlockSpec(memory_space=pl.ANY),
        ],
        out_specs=(pl.BlockSpec((1, SUBLANES, tm), lambda t: (t, 0, 0)),
                   pl.BlockSpec(memory_space=pl.ANY)),
        out_shape=(jax.ShapeDtypeStruct((nt, SUBLANES, tm), jnp.int32),
                   jax.ShapeDtypeStruct(xs_zero.shape, xs_zero.dtype)),
        input_output_aliases={3: 1},
        scratch_shapes=[
            pltpu.VMEM((SUBLANES, tm), jnp.int32),
            pltpu.SMEM((SUBLANES, tm), jnp.int32),
            pltpu.SemaphoreType.DMA(()),
            pltpu.SemaphoreType.DMA(()),
        ],
        compiler_params=pltpu.CompilerParams(
            dimension_semantics=("arbitrary",), vmem_limit_bytes=40 * 1024 * 1024),
        name="moe_dispatch",
    )(rt, tile_base, n2p, xs_zero)


def _expert_kernel(be_ref, nused_ref, xs_ref, wg_ref, wu_ref, wd_ref, y_ref, wg_b, wu_b, wd_b):
    b = pl.program_id(0)

    @pl.when(b < nused_ref[0])
    def _():
        @pl.when((b == 0) | (be_ref[b] != be_ref[jnp.maximum(b - 1, 0)]))
        def _():
            wg_b[...] = wg_ref[0].astype(BF16)
            wu_b[...] = wu_ref[0].astype(BF16)
            wd_b[...] = wd_ref[0].astype(BF16)

        for c in range(EXPERT_BLOCK // EXPERT_CHUNK):
            rows = pl.ds(c * EXPERT_CHUNK, EXPERT_CHUNK)
            xb = _unpack_bf16_pairs(xs_ref[rows, :])
            gate = jnp.dot(xb, wg_b[...], preferred_element_type=F32)
            up = jnp.dot(xb, wu_b[...], preferred_element_type=F32)
            hid = (gate * jax.nn.sigmoid(gate) * up).astype(BF16)
            y_ref[rows, :] = jnp.dot(hid, wd_b[...], preferred_element_type=F32)

    @pl.when(b >= nused_ref[0])
    def _():
        y_ref[...] = jnp.zeros_like(y_ref)


def _expert_call(block_expert, n_used, xs, w_gate, w_up, w_down):
    P = xs.shape[0]
    NB = P // EXPERT_BLOCK
    E, D, F = w_gate.shape
    grid_spec = pltpu.PrefetchScalarGridSpec(
        num_scalar_prefetch=2,
        grid=(NB,),
        in_specs=[
            pl.BlockSpec((EXPERT_BLOCK,) + xs.shape[1:], lambda b, be, nu: (b, 0)),
            pl.BlockSpec((1, D, F), lambda b, be, nu: (be[b], 0, 0)),
            pl.BlockSpec((1, D, F), lambda b, be, nu: (be[b], 0, 0)),
            pl.BlockSpec((1, F, D), lambda b, be, nu: (be[b], 0, 0)),
        ],
        out_specs=pl.BlockSpec((EXPERT_BLOCK, D), lambda b, be, nu: (b, 0)),
        scratch_shapes=[
            pltpu.VMEM((D, F), BF16),
            pltpu.VMEM((D, F), BF16),
            pltpu.VMEM((F, D), BF16),
        ],
    )
    return pl.pallas_call(
        _expert_kernel,
        grid_spec=grid_spec,
        out_shape=jax.ShapeDtypeStruct((P, D), F32),
        compiler_params=pltpu.CompilerParams(
            dimension_semantics=("arbitrary",), vmem_limit_bytes=48 * 1024 * 1024),
        name="moe_experts",
    )(block_expert, n_used, xs, w_gate, w_up, w_down)


def _combine_kernel(dest_hbm, x1_ref, rt_ref, ys_hbm, fg_ref, o_ref, d_smem, ybuf, idx_sem, row_sem,
                    *, tm, final_norm):
    t = pl.program_id(0)
    nt = pl.num_programs(0)
    cur = t % 2
    nxt = 1 - cur

    def idx_copy(tile, s):
        return pltpu.make_async_copy(dest_hbm.at[tile], d_smem.at[s], idx_sem.at[s])

    def issue_rows(s):
        def issue(c, carry):
            r0 = pl.multiple_of(c * ROW_UNROLL, ROW_UNROLL)
            for k in range(TOP_K):
                rows = ybuf.at[s, k, pl.ds(r0, ROW_UNROLL)]
                for u in range(ROW_UNROLL):
                    pltpu.make_async_copy(ys_hbm.at[pl.ds(d_smem[s, k, r0 + u], 1)],
                                          rows.at[pl.ds(u, 1)], row_sem.at[s]).start()
            return carry

        lax.fori_loop(0, tm // ROW_UNROLL, issue, 0)

    @pl.when(t == 0)
    def _():
        first = idx_copy(0, 0)
        first.start()
        first.wait()
        issue_rows(0)

        @pl.when(nt > 1)
        def _():
            idx_copy(1, 1).start()

    @pl.when(t + 1 < nt)
    def _():
        idx_copy(t + 1, nxt).wait()
        issue_rows(nxt)

    @pl.when(t + 2 < nt)
    def _():
        idx_copy(t + 2, cur).start()

    for k in range(TOP_K):
        pltpu.make_async_copy(ys_hbm.at[pl.ds(0, tm)], ybuf.at[cur, k], row_sem.at[cur]).wait()
    rt = rt_ref[...]
    h = x1_ref[...] + rt[:, 2:3] * ybuf[cur, 0] + rt[:, 3:4] * ybuf[cur, 1]
    if final_norm:
        h = h * lax.rsqrt(jnp.mean(h * h, axis=-1, keepdims=True) + EPS) * fg_ref[...]
    o_ref[...] = h


def _combine_call(dest, x1, rt, ys, final_g, *, tm, final_norm):
    T, D = x1.shape
    return pl.pallas_call(
        functools.partial(_combine_kernel, tm=tm, final_norm=final_norm),
        grid=(T // tm,),
        in_specs=[
            pl.BlockSpec(memory_space=pl.ANY),
            pl.BlockSpec((tm, D), lambda t: (t, 0)),
            pl.BlockSpec((tm, ROUTER_COLS), lambda t: (t, 0)),
            pl.BlockSpec(memory_space=pl.ANY),
            pl.BlockSpec((1, D), lambda t: (0, 0)),
        ],
        out_specs=pl.BlockSpec((tm, D), lambda t: (t, 0)),
        out_shape=jax.ShapeDtypeStruct((T, D), F32),
        scratch_shapes=[
            pltpu.SMEM((2, SUBLANES, tm), jnp.int32),
            pltpu.VMEM((2, TOP_K, tm, D), F32),
            pltpu.SemaphoreType.DMA((2,)),
            pltpu.SemaphoreType.DMA((2,)),
        ],
        compiler_params=pltpu.CompilerParams(
            dimension_semantics=("arbitrary",), vmem_limit_bytes=40 * 1024 * 1024),
        name="moe_combine",
    )(dest, x1, rt, ys, final_g)


def _slot_layout(tile_counts, n_assign):
    NB = -(-n_assign // EXPERT_BLOCK) + N_EXPERTS
    counts = jnp.sum(tile_counts, axis=0)
    padded = ((counts + EXPERT_BLOCK - 1) // EXPERT_BLOCK) * EXPERT_BLOCK
    pad_end = jnp.cumsum(padded)
    pad_start = pad_end - padded
    tile_base = pad_start[None, :] + jnp.cumsum(tile_counts, axis=0) - tile_counts
    block_start = jnp.arange(NB, dtype=jnp.int32) * EXPERT_BLOCK
    block_expert = jnp.minimum(jnp.sum(pad_end[None, :] <= block_start[:, None], axis=1),
                               N_EXPERTS - 1).astype(jnp.int32)
    n_used = (pad_end[-1] // EXPERT_BLOCK).astype(jnp.int32).reshape(1)
    return NB, block_expert, n_used, tile_base


def _rope_tables(S):
    inv = 1.0 / (ROPE_THETA ** (jnp.arange(0, HEAD_DIM, 2, dtype=F32) / HEAD_DIM))
    ang = jnp.arange(S, dtype=F32)[:, None] * inv[None, :]
    cos, sin = jnp.cos(ang), jnp.sin(ang)
    cos_l = jnp.tile(cos, (1, LANES // (HEAD_DIM // 2)))
    sin_l = jnp.tile(jnp.concatenate([-sin, sin], axis=1), (1, LANES // HEAD_DIM))
    return cos_l, sin_l, cos.T, sin.T


def kernel(x, norm1_g, w_in, lambda_q1, lambda_k1, lambda_q2, lambda_k2, subln_g, sinks, w_out,
           norm2_g, w_router_group, b_router_group, w_router_expert, b_router_expert,
           w_gate, w_up, w_down, final_g):
    B, S, D = x.shape
    T = B * S
    depth = w_in.shape[0]
    tq, tk = 512, 512
    tm_proj = 512
    tm_tok = 512
    tq_swa = 512
    qscale = HEAD_DIM ** -0.5 * math.log2(math.e)
    cos_l, sin_l, cos_t, sin_t = _rope_tables(S)

    c0 = DIFF_QK_COLS
    c1 = 2 * DIFF_QK_COLS
    c2 = c1 + DIFF_V_COLS
    c3 = c2 + SWA_Q_COLS
    c4 = c3 + SWA_KV_COLS
    for l in range(depth):
        lambda_init = 0.8 - 0.6 * math.exp(-0.3 * l)
        w = w_in[l]
        w_nat = jnp.concatenate([w[:, c0:c1], w[:, c3:c4]], axis=1).astype(BF16)
        w_tr = jnp.concatenate([w[:, :c0] * qscale, w[:, c1:c2], w[:, c2:c3] * qscale, w[:, c4:]],
                               axis=1).T.astype(BF16)
        dqt, dk, dvt, sqt, sk, svt = _proj_call(
            x, norm1_g[l][None, :], w_nat, w_tr, cos_l, sin_l, cos_t, sin_t, tm=tm_proj, tk=tk)

        lam_p = jnp.stack([lambda_q1[l], lambda_k1[l], lambda_q2[l], lambda_k2[l]]).astype(F32)
        o_diff = _diff_call(lam_p, dqt, dk, dvt, subln_g[l][None, :].astype(F32),
                            tq=tq, tk=tk, lambda_init=lambda_init)
        sink_row = jnp.repeat(sinks[l].astype(F32) * math.log2(math.e), WINDOW)[None, :]
        o_swa = _swa_call(sink_row, sqt, sk, svt, tq=tq_swa)

        wo_b = w_out[l].astype(BF16)
        w_router = jnp.zeros((D, ROUTER_COLS), F32)
        w_router = w_router.at[:, :N_GROUPS].set(w_router_group[l])
        w_router = w_router.at[:, N_GROUPS:N_GROUPS + N_EXPERTS].set(w_router_expert[l])
        w_router_hi = w_router.astype(BF16)
        w_router_lo = (w_router - w_router_hi.astype(F32)).astype(BF16)
        w_router = jnp.concatenate([w_router_hi, w_router_lo], axis=1)
        b_router = jnp.zeros((1, ROUTER_COLS), F32)
        b_router = b_router.at[0, :N_GROUPS].set(b_router_group[l])
        b_router = b_router.at[0, N_GROUPS:N_GROUPS + N_EXPERTS].set(b_router_expert[l])
        x1, n2p, rt, cnt = _mix_call(x, o_diff, o_swa, wo_b, norm2_g[l][None, :], w_router, b_router, tm=tm_tok)

        rt2 = rt.reshape(T, ROUTER_COLS)
        tile_counts = cnt[:, :, 0, :N_EXPERTS].reshape(T // tm_tok, N_EXPERTS).astype(jnp.int32)
        NB, block_expert, n_used, tile_base = _slot_layout(tile_counts, T * TOP_K)
        tile_base = jnp.broadcast_to(tile_base.astype(F32)[:, :, None], (T // tm_tok, N_EXPERTS, LANES))
        xs_zero = jnp.zeros((NB * EXPERT_BLOCK,) + n2p.shape[2:], jnp.uint32)
        dest, xs = _dispatch_call(rt2, tile_base, n2p.reshape((T,) + n2p.shape[2:]), xs_zero, tm=tm_tok)
        ys = _expert_call(block_expert, n_used, xs, w_gate[l], w_up[l], w_down[l])
        x = _combine_call(dest, x1.reshape(T, D), rt2, ys, final_g[None, :],
                          tm=tm_tok, final_norm=(l == depth - 1)).reshape(B, S, D)
    return x
```

```python
import functools
import math

import jax
import jax.numpy as jnp
from jax import lax
from jax.experimental import pallas as pl
from jax.experimental.pallas import tpu as pltpu

HEAD_DIM = 64
DIFF_HEADS = 4
DIFF_V_DIM = 2 * HEAD_DIM
SWA_Q_HEADS = 8
SWA_KV_HEADS = 2
SWA_GROUP = SWA_Q_HEADS // SWA_KV_HEADS
WINDOW = 128
ROPE_THETA = 10000.0
N_GROUPS = 4
EXPERTS_PER_GROUP = 8
N_EXPERTS = N_GROUPS * EXPERTS_PER_GROUP
TOP_K = 2
EXPERT_BLOCK = 512
EXPERT_CHUNK = 256
EPS = 1e-6
NEG = -1e30

DIFF_QK_COLS = DIFF_HEADS * 2 * HEAD_DIM
DIFF_V_COLS = DIFF_HEADS * DIFF_V_DIM
SWA_Q_COLS = SWA_Q_HEADS * HEAD_DIM
SWA_KV_COLS = SWA_KV_HEADS * HEAD_DIM
LANES = 128
SUBLANES = 8
BF16_SUBLANES = 16
VT_ROWS = DIFF_V_DIM + BF16_SUBLANES
SWA_VT_ROWS = SWA_KV_COLS + BF16_SUBLANES
ROUTER_COLS = LANES
DIFF_UNROLL = 4
DIFF_S_BUFS = 4

BF16 = jnp.bfloat16
F32 = jnp.float32


def _rope_lanes(x, cos_l, sin_l, first_half):
    rot = jnp.where(first_half, pltpu.roll(x, 96, 1), pltpu.roll(x, 32, 1))
    return x * cos_l + rot * sin_l


def _proj_kernel(x_ref, g_ref, wnat_ref, wtr_ref, cosl_ref, sinl_ref, cost_ref, sint_ref,
                 dqt_ref, dk_ref, dvt_ref, sqt_ref, sk_ref, svt_ref, *, tk):
    x = x_ref[0]
    tm = x.shape[0]
    n1 = x * lax.rsqrt(jnp.mean(x * x, axis=-1, keepdims=True) + EPS) * g_ref[...]
    n1b = n1.astype(BF16)
    nat = jnp.dot(n1b, wnat_ref[...], preferred_element_type=F32)
    tr = lax.dot_general(wtr_ref[...], n1b, (((1,), (1,)), ((), ())),
                         preferred_element_type=F32)

    cos_l, sin_l = cosl_ref[...], sinl_ref[...]
    first_half = (lax.broadcasted_iota(jnp.int32, (tm, LANES), 1) & (HEAD_DIM - 1)) < HEAD_DIM // 2
    for h in range(DIFF_HEADS):
        slab = nat[:, h * LANES:(h + 1) * LANES]
        dk_ref[0, h] = _rope_lanes(slab, cos_l, sin_l, first_half).astype(BF16)
    sk = _rope_lanes(nat[:, DIFF_QK_COLS:DIFF_QK_COLS + LANES], cos_l, sin_l, first_half).astype(BF16)
    for c in range(tm // WINDOW):
        sk_ref[0, c] = sk[c * WINDOW:(c + 1) * WINDOW]

    cos_t, sin_t = cost_ref[...], sint_ref[...]
    half = HEAD_DIM // 2

    def rope_rows(r0):
        x1 = tr[r0:r0 + half]
        x2 = tr[r0 + half:r0 + HEAD_DIM]
        return (x1 * cos_t - x2 * sin_t).astype(BF16), (x1 * sin_t + x2 * cos_t).astype(BF16)

    for h in range(DIFF_HEADS):
        for c in range(2):
            lo, hi = rope_rows(h * 2 * HEAD_DIM + c * HEAD_DIM)
            dqt_ref[0, h, c * HEAD_DIM:c * HEAD_DIM + half] = lo
            dqt_ref[0, h, c * HEAD_DIM + half:(c + 1) * HEAD_DIM] = hi
    ones_rows = (lax.broadcasted_iota(jnp.int32, (BF16_SUBLANES, tk), 0) == 0).astype(BF16)
    for h in range(DIFF_HEADS):
        r0 = DIFF_QK_COLS + h * DIFF_V_DIM
        for c in range(tm // tk):
            dvt_ref[0, h, c, :DIFF_V_DIM] = tr[r0:r0 + DIFF_V_DIM, c * tk:(c + 1) * tk].astype(BF16)
            dvt_ref[0, h, c, DIFF_V_DIM:] = ones_rows

    r0 = DIFF_QK_COLS + DIFF_V_COLS
    for h in range(SWA_Q_HEADS):
        lo, hi = rope_rows(r0 + h * HEAD_DIM)
        sqt_ref[0, h * HEAD_DIM:h * HEAD_DIM + half] = lo
        sqt_ref[0, h * HEAD_DIM + half:(h + 1) * HEAD_DIM] = hi
    r0 += SWA_Q_COLS
    for c in range(tm // WINDOW):
        svt_ref[0, c, :SWA_KV_COLS] = tr[r0:r0 + SWA_KV_COLS, c * WINDOW:(c + 1) * WINDOW].astype(BF16)
        svt_ref[0, c, SWA_KV_COLS:] = ones_rows[:, :WINDOW]


def _proj_call(x, g1, w_nat, w_tr, cos_l, sin_l, cos_t, sin_t, *, tm, tk):
    B, S, D = x.shape
    nkv = S // tk
    grid = (B, S // tm)
    const = lambda b, i: (0, 0)
    out_shape = (
        jax.ShapeDtypeStruct((B, DIFF_HEADS, 2 * HEAD_DIM, S), BF16),
        jax.ShapeDtypeStruct((B, DIFF_HEADS, S, 2 * HEAD_DIM), BF16),
        jax.ShapeDtypeStruct((B, DIFF_HEADS, nkv, VT_ROWS, tk), BF16),
        jax.ShapeDtypeStruct((B, SWA_Q_COLS, S), BF16),
        jax.ShapeDtypeStruct((B, S // WINDOW, WINDOW, SWA_KV_COLS), BF16),
        jax.ShapeDtypeStruct((B, S // WINDOW, SWA_VT_ROWS, WINDOW), BF16),
    )
    return pl.pallas_call(
        functools.partial(_proj_kernel, tk=tk),
        grid=grid,
        in_specs=[
            pl.BlockSpec((1, tm, D), lambda b, i: (b, i, 0)),
            pl.BlockSpec((1, D), const),
            pl.BlockSpec(w_nat.shape, const),
            pl.BlockSpec(w_tr.shape, const),
            pl.BlockSpec((tm, LANES), lambda b, i: (i, 0)),
            pl.BlockSpec((tm, LANES), lambda b, i: (i, 0)),
            pl.BlockSpec((HEAD_DIM // 2, tm), lambda b, i: (0, i)),
            pl.BlockSpec((HEAD_DIM // 2, tm), lambda b, i: (0, i)),
        ],
        out_specs=(
            pl.BlockSpec((1, DIFF_HEADS, 2 * HEAD_DIM, tm), lambda b, i: (b, 0, 0, i)),
            pl.BlockSpec((1, DIFF_HEADS, tm, 2 * HEAD_DIM), lambda b, i: (b, 0, i, 0)),
            pl.BlockSpec((1, DIFF_HEADS, tm // tk, VT_ROWS, tk), lambda b, i: (b, 0, i, 0, 0)),
            pl.BlockSpec((1, SWA_Q_COLS, tm), lambda b, i: (b, 0, i)),
            pl.BlockSpec((1, tm // WINDOW, WINDOW, SWA_KV_COLS), lambda b, i: (b, i, 0, 0)),
            pl.BlockSpec((1, tm // WINDOW, SWA_VT_ROWS, WINDOW), lambda b, i: (b, i, 0, 0)),
        ),
        out_shape=out_shape,
        compiler_params=pltpu.CompilerParams(
            dimension_semantics=("parallel", "parallel"), vmem_limit_bytes=48 * 1024 * 1024),
        name="proj_rope",
    )(x, g1, w_nat, w_tr, cos_l, sin_l, cos_t, sin_t)


def _diff_kernel(lam_ref, qt_ref, k_ref, vt_ref, g_ref, o_ref, *scratch, tq, tk, lambda_init):
    i = pl.program_id(2)
    s_bufs = scratch[:DIFF_S_BUFS]
    top_bufs = scratch[DIFF_S_BUFS:2 * DIFF_S_BUFS]
    m_ref, acc_ref = scratch[2 * DIFF_S_BUFS:]
    qt = qt_ref[0, 0]
    z = jnp.zeros((HEAD_DIM, tq), BF16)
    qw = jnp.concatenate([jnp.concatenate([qt[:HEAD_DIM], z], axis=1),
                          jnp.concatenate([z, qt[HEAD_DIM:]], axis=1)], axis=0)

    def scores(j, par):
        kt = k_ref[0, 0, pl.ds(pl.multiple_of(j * tk, tk), tk), :]
        s = jnp.dot(kt, qw, preferred_element_type=F32)
        s_bufs[par][...] = s
        top_bufs[par][...] = jnp.max(s, axis=0, keepdims=True)

    def absorb(j, par, masked):
        s = s_bufs[par][...]
        if masked:
            kpos = j * tk + lax.broadcasted_iota(jnp.int32, (tk, 2 * tq), 0)
            qpos = i * tq + (lax.broadcasted_iota(jnp.int32, (tk, 2 * tq), 1) & (tq - 1))
            s = jnp.where(kpos <= qpos, s, NEG)
            top = jnp.max(s, axis=0, keepdims=True)
        else:
            top = top_bufs[par][...]
        m = m_ref[...]
        m_new = jnp.maximum(m, top)
        alpha = jnp.exp2(m - m_new)
        p = jnp.exp2(s - m_new).astype(BF16)
        m_ref[...] = m_new
        pv = jnp.dot(vt_ref[0, 0, j], p, preferred_element_type=F32)
        acc_ref[...] = alpha * acc_ref[...] + pv

    m_ref[...] = jnp.full(m_ref.shape, NEG, F32)
    acc_ref[...] = jnp.zeros(acc_ref.shape, F32)

    nfull = (i * tq) // tk
    scores(nfull, 0)
    scores(0, 1)
    absorb(nfull, 0, True)

    def group(t, c):
        j = DIFF_UNROLL * t
        for idx in range(DIFF_UNROLL):
            scores(j + idx + 1, (idx + 2) % DIFF_S_BUFS)
            absorb(j + idx, (idx + 1) % DIFF_S_BUFS, False)
        return c

    lax.fori_loop(0, nfull // DIFF_UNROLL, group, 0)

    for rem in range(1, DIFF_UNROLL):
        @pl.when(nfull % DIFF_UNROLL == rem)
        def _():
            first = nfull - rem
            for idx in range(rem):
                if idx + 1 < rem:
                    scores(first + idx + 1, (idx + 2) % DIFF_S_BUFS)
                absorb(first + idx, (idx + 1) % DIFF_S_BUFS, False)

    lam_p = lam_ref[...]
    lam = (jnp.exp(jnp.sum(lam_p[0:1] * lam_p[1:2], axis=-1, keepdims=True))
           - jnp.exp(jnp.sum(lam_p[2:3] * lam_p[3:4], axis=-1, keepdims=True)) + lambda_init)
    l = acc_ref[DIFF_V_DIM:DIFF_V_DIM + 1, :]
    o = (acc_ref[:DIFF_V_DIM, :tq] / l[:, :tq]
         - lam * (acc_ref[:DIFF_V_DIM, tq:] / l[:, tq:]))
    o = o * lax.rsqrt(jnp.mean(o * o, axis=0, keepdims=True) + EPS)
    o_ref[0] = (o.T * g_ref[...] * (1.0 - lambda_init)).astype(BF16)


def _diff_call(lam_p, dqt, dk, dvt, subln_g, *, tq, tk, lambda_init):
    B, H, _, S = dqt.shape
    assert tk % tq == 0 and S % tk == 0, "one key tile must cover a query tile's diagonal"
    nkv = S // tk
    grid = (B, H, S // tq)
    return pl.pallas_call(
        functools.partial(_diff_kernel, tq=tq, tk=tk, lambda_init=lambda_init),
        grid=grid,
        in_specs=[
            pl.BlockSpec(lam_p.shape, lambda b, h, i: (0, 0)),
            pl.BlockSpec((1, 1, 2 * HEAD_DIM, tq), lambda b, h, i: (b, h, 0, i)),
            pl.BlockSpec((1, 1, S, 2 * HEAD_DIM), lambda b, h, i: (b, h, 0, 0)),
            pl.BlockSpec((1, 1, nkv, VT_ROWS, tk), lambda b, h, i: (b, h, 0, 0, 0)),
            pl.BlockSpec((1, DIFF_V_DIM), lambda b, h, i: (0, 0)),
        ],
        out_specs=pl.BlockSpec((1, tq, DIFF_V_DIM), lambda b, h, i: (b, i, h)),
        out_shape=jax.ShapeDtypeStruct((B, S, DIFF_V_COLS), BF16),
        scratch_shapes=[pltpu.VMEM((tk, 2 * tq), F32)] * DIFF_S_BUFS + [
            pltpu.VMEM((1, 2 * tq), F32)] * DIFF_S_BUFS + [
            pltpu.VMEM((1, 2 * tq), F32),
            pltpu.VMEM((VT_ROWS, 2 * tq), F32),
        ],
        compiler_params=pltpu.CompilerParams(
            dimension_semantics=("parallel", "parallel", "arbitrary"),
            vmem_limit_bytes=48 * 1024 * 1024),
        name="diff_attn",
    )(lam_p, dqt, dk, dvt, subln_g)


def _swa_kernel(sink_ref, qt_ref, k_ref, vt_ref, o_ref, *, tq):
    i = pl.program_id(1)
    n_cols = SWA_Q_HEADS * WINDOW
    half_cols = n_cols // SWA_KV_HEADS
    sink = sink_ref[...]
    row = lax.broadcasted_iota(jnp.int32, (2 * WINDOW, WINDOW), 0)
    qrel = lax.broadcasted_iota(jnp.int32, (2 * WINDOW, WINDOW), 1)
    band = (row - WINDOW <= qrel) & (row > qrel)
    in_current = row >= WINDOW
    z = jnp.zeros((HEAD_DIM, half_cols), BF16)
    for sub in range(tq // WINDOW):
        n = i * (tq // WINDOW) + sub
        prev = jnp.maximum(n - 1, 0)
        kwin = jnp.concatenate([k_ref[0, prev], k_ref[0, n]], axis=0)
        vtwin = jnp.concatenate([vt_ref[0, prev], vt_ref[0, n]], axis=1)
        qt = qt_ref[0, :, sub * WINDOW:(sub + 1) * WINDOW]
        heads = [qt[h * HEAD_DIM:(h + 1) * HEAD_DIM] for h in range(SWA_Q_HEADS)]
        qw = jnp.concatenate(
            [jnp.concatenate(heads[:SWA_GROUP] + [z], axis=1),
             jnp.concatenate([z] + heads[SWA_GROUP:], axis=1)], axis=0)
        s = jnp.dot(kwin, qw, preferred_element_type=F32)
        valid = band & (in_current | (n >= 1))
        s = jnp.concatenate(
            [jnp.where(valid, s[:, h * WINDOW:(h + 1) * WINDOW], NEG) for h in range(SWA_Q_HEADS)], axis=1)
        m = jnp.maximum(jnp.max(s, axis=0, keepdims=True), sink)
        p = jnp.exp2(s - m).astype(BF16)
        acc = jnp.dot(vtwin, p, preferred_element_type=F32)
        den = acc[SWA_KV_COLS:SWA_KV_COLS + 1] + jnp.exp2(sink - m)
        on = acc[:SWA_KV_COLS] / den
        u = jnp.concatenate([on[:HEAD_DIM, :half_cols], on[HEAD_DIM:, half_cols:]], axis=1)
        for hp in range(SWA_Q_HEADS // 2):
            two = jnp.concatenate([u[:, (2 * hp) * WINDOW:(2 * hp + 1) * WINDOW],
                                   u[:, (2 * hp + 1) * WINDOW:(2 * hp + 2) * WINDOW]], axis=0)
            o_ref[0, sub * WINDOW:(sub + 1) * WINDOW, hp * LANES:(hp + 1) * LANES] = two.T.astype(BF16)


def _swa_call(sink_row, sqt, sk, svt, *, tq):
    B, _, S = sqt.shape
    nb = S // WINDOW
    return pl.pallas_call(
        functools.partial(_swa_kernel, tq=tq),
        grid=(B, S // tq),
        in_specs=[
            pl.BlockSpec(sink_row.shape, lambda b, i: (0, 0)),
            pl.BlockSpec((1, SWA_Q_COLS, tq), lambda b, i: (b, 0, i)),
            pl.BlockSpec((1, nb, WINDOW, SWA_KV_COLS), lambda b, i: (b, 0, 0, 0)),
            pl.BlockSpec((1, nb, SWA_VT_ROWS, WINDOW), lambda b, i: (b, 0, 0, 0)),
        ],
        out_specs=pl.BlockSpec((1, tq, SWA_Q_COLS), lambda b, i: (b, i, 0)),
        out_shape=jax.ShapeDtypeStruct((B, S, SWA_Q_COLS), BF16),
        compiler_params=pltpu.CompilerParams(
            dimension_semantics=("parallel", "arbitrary"), vmem_limit_bytes=40 * 1024 * 1024),
        name="swa_attn",
    )(sink_row, sqt, sk, svt)


def _pack_bf16_pairs(x):
    n = x.shape[1] // 2
    lo = lax.bitcast_convert_type(x[:, :n].astype(BF16).astype(F32), jnp.uint32)
    hi = lax.bitcast_convert_type(x[:, n:].astype(BF16).astype(F32), jnp.uint32)
    return (lo >> 16) | (hi & jnp.uint32(0xFFFF0000))


def _unpack_bf16_pairs(w):
    lo = lax.bitcast_convert_type(w << 16, F32)
    hi = lax.bitcast_convert_type(w & jnp.uint32(0xFFFF0000), F32)
    return jnp.concatenate([lo, hi], axis=1).astype(BF16)


def _mix_kernel(x_ref, od_ref, os_ref, wo_ref, g2_ref, wr_ref, br_ref, x1_ref, n2_ref, rt_ref, cnt_ref):
    h = (x_ref[0]
         + jnp.dot(od_ref[0], wo_ref[:DIFF_V_COLS], preferred_element_type=F32)
         + jnp.dot(os_ref[0], wo_ref[DIFF_V_COLS:], preferred_element_type=F32))
    x1_ref[0] = h
    n2 = h * lax.rsqrt(jnp.mean(h * h, axis=-1, keepdims=True) + EPS) * g2_ref[...]
    n2_ref[0] = _pack_bf16_pairs(n2)
    tm = n2.shape[0]
    n2_hi = n2.astype(BF16)
    n2_lo = (n2 - n2_hi.astype(F32)).astype(BF16)
    parts = jnp.dot(jnp.concatenate([n2_hi, n2_lo], axis=0), wr_ref[...],
                    preferred_element_type=F32)
    logits = ((parts[:tm, :ROUTER_COLS] + parts[tm:, ROUTER_COLS:])
              + (parts[:tm, ROUTER_COLS:] + parts[tm:, :ROUTER_COLS])) + br_ref[...]
    lane = lax.broadcasted_iota(jnp.int32, (tm, ROUTER_COLS), 1)
    big = jnp.int32(ROUTER_COLS)
    gl = jnp.where(lane < N_GROUPS, logits, -jnp.inf)
    gm = jnp.max(gl, axis=-1, keepdims=True)
    p_top = 1.0 / jnp.sum(jnp.exp(gl - gm), axis=-1, keepdims=True)
    g_idx = jnp.min(jnp.where(gl == gm, lane, big), axis=-1, keepdims=True)
    e_lo = N_GROUPS + EXPERTS_PER_GROUP * g_idx
    el = jnp.where((lane >= e_lo) & (lane < e_lo + EXPERTS_PER_GROUP), logits, -jnp.inf)
    v1 = jnp.max(el, axis=-1, keepdims=True)
    i1 = jnp.min(jnp.where(el == v1, lane, big), axis=-1, keepdims=True)
    el2 = jnp.where(lane == i1, -jnp.inf, el)
    v2 = jnp.max(el2, axis=-1, keepdims=True)
    i2 = jnp.min(jnp.where(el2 == v2, lane, big), axis=-1, keepdims=True)
    e21 = jnp.exp(v2 - v1)
    gate1 = p_top / (1.0 + e21)
    gate2 = p_top * e21 / (1.0 + e21)
    rt = jnp.where(lane == 0, (i1 - N_GROUPS).astype(F32),
         jnp.where(lane == 1, (i2 - N_GROUPS).astype(F32),
         jnp.where(lane == 2, gate1, jnp.where(lane == 3, gate2, 0.0))))
    rt_ref[0] = rt
    chosen = ((lane == i1 - N_GROUPS) | (lane == i2 - N_GROUPS)).astype(F32)
    cnt_ref[0, 0] = jnp.broadcast_to(jnp.sum(chosen, axis=0, keepdims=True), cnt_ref.shape[2:])


def _mix_call(x, o_diff, o_swa, w_out, g2, w_router, b_router, *, tm):
    B, S, D = x.shape
    const = lambda b, i: (0, 0)
    row = lambda b, i: (b, i, 0)
    nt = S // tm
    return pl.pallas_call(
        _mix_kernel,
        grid=(B, nt),
        in_specs=[
            pl.BlockSpec((1, tm, D), row),
            pl.BlockSpec((1, tm, DIFF_V_COLS), row),
            pl.BlockSpec((1, tm, SWA_Q_COLS), row),
            pl.BlockSpec(w_out.shape, const),
            pl.BlockSpec((1, D), const),
            pl.BlockSpec(w_router.shape, const),
            pl.BlockSpec((1, ROUTER_COLS), const),
        ],
        out_specs=(pl.BlockSpec((1, tm, D), row), pl.BlockSpec((1, tm, D // 2), row),
                   pl.BlockSpec((1, tm, ROUTER_COLS), row),
                   pl.BlockSpec((1, 1, SUBLANES, ROUTER_COLS), lambda b, i: (b, i, 0, 0))),
        out_shape=(jax.ShapeDtypeStruct((B, S, D), F32), jax.ShapeDtypeStruct((B, S, D // 2), jnp.uint32),
                   jax.ShapeDtypeStruct((B, S, ROUTER_COLS), F32),
                   jax.ShapeDtypeStruct((B, nt, SUBLANES, ROUTER_COLS), F32)),
        compiler_params=pltpu.CompilerParams(
            dimension_semantics=("parallel", "parallel"), vmem_limit_bytes=48 * 1024 * 1024),
        name="outproj_router",
    )(x, o_diff, o_swa, w_out, g2, w_router, b_router)


ROW_UNROLL = 8


def _dispatch_kernel(rt_ref, base_ref, n2_ref, xs_in_hbm, dest_ref, xs_hbm, d_vmem, d_smem, idx_sem, row_sem,
                     *, tm):
    del xs_in_hbm
    rt_t = rt_ref[...].T
    e1 = rt_t[0:1].astype(jnp.int32)
    e2 = rt_t[1:2].astype(jnp.int32)
    eid = lax.broadcasted_iota(jnp.int32, (N_EXPERTS, tm), 0)
    oh1 = eid == e1
    oh2 = eid == e2
    earlier = (lax.broadcasted_iota(jnp.int32, (tm, tm), 0)
               < lax.broadcasted_iota(jnp.int32, (tm, tm), 1)).astype(BF16)
    before = jnp.dot((oh1 | oh2).astype(BF16), earlier, preferred_element_type=F32)
    slot = before + base_ref[0][:, 0:1]
    d1 = jnp.sum(jnp.where(oh1, slot, 0.0), axis=0, keepdims=True).astype(jnp.int32)
    d2 = jnp.sum(jnp.where(oh2, slot, 0.0), axis=0, keepdims=True).astype(jnp.int32)
    d = jnp.concatenate([d1, d2, jnp.zeros((SUBLANES - TOP_K, tm), jnp.int32)], axis=0)
    dest_ref[0] = d
    d_vmem[...] = d
    idx_copy = pltpu.make_async_copy(d_vmem, d_smem, idx_sem)
    idx_copy.start()
    idx_copy.wait()

    def issue(c, carry):
        r0 = pl.multiple_of(c * ROW_UNROLL, ROW_UNROLL)
        rows = n2_ref.at[pl.ds(r0, ROW_UNROLL)]
        for u in range(ROW_UNROLL):
            for k in range(TOP_K):
                pltpu.make_async_copy(rows.at[pl.ds(u, 1)], xs_hbm.at[pl.ds(d_smem[k, r0 + u], 1)], row_sem).start()
        return carry

    lax.fori_loop(0, tm // ROW_UNROLL, issue, 0)
    for k in range(TOP_K):
        pltpu.make_async_copy(n2_ref, xs_hbm.at[pl.ds(0, tm)], row_sem).wait()


def _dispatch_call(rt, tile_base, n2p, xs_zero, *, tm):
    T = n2p.shape[0]
    nt = T // tm
    return pl.pallas_call(
        functools.partial(_dispatch_kernel, tm=tm),
        grid=(nt,),
        in_specs=[
            pl.BlockSpec((tm, ROUTER_COLS), lambda t: (t, 0)),
            pl.BlockSpec((1, N_EXPERTS, LANES), lambda t: (t, 0, 0)),
            pl.BlockSpec((tm,) + n2p.shape[1:], lambda t: (t, 0)),
            pl.BlockSpec(memory_space=pl.ANY),
        ],
        out_specs=(pl.BlockSpec((1, SUBLANES, tm), lambda t: (t, 0, 0)),
                   pl.BlockSpec(memory_space=pl.ANY)),
        out_shape=(jax.ShapeDtypeStruct((nt, SUBLANES, tm), jnp.int32),
                   jax.ShapeDtypeStruct(xs_zero.shape, xs_zero.dtype)),
        input_output_aliases={3: 1},
        scratch_shapes=[
            pltpu.VMEM((SUBLANES, tm), jnp.int32),
            pltpu.SMEM((SUBLANES, tm), jnp.int32),
            pltpu.SemaphoreType.DMA(()),
            pltpu.SemaphoreType.DMA(()),
        ],
        compiler_params=pltpu.CompilerParams(
            dimension_semantics=("arbitrary",), vmem_limit_bytes=40 * 1024 * 1024),
        name="moe_dispatch",
    )(rt, tile_base, n2p, xs_zero)


def _expert_kernel(be_ref, nused_ref, xs_ref, wg_ref, wu_ref, wd_ref, y_ref, wg_b, wu_b, wd_b):
    b = pl.program_id(0)

    @pl.when(b < nused_ref[0])
    def _():
        @pl.when((b == 0) | (be_ref[b] != be_ref[jnp.maximum(b - 1, 0)]))
        def _():
            wg_b[...] = wg_ref[0].astype(BF16)
            wu_b[...] = wu_ref[0].astype(BF16)
            wd_b[...] = wd_ref[0].astype(BF16)

        for c in range(EXPERT_BLOCK // EXPERT_CHUNK):
            rows = pl.ds(c * EXPERT_CHUNK, EXPERT_CHUNK)
            xb = _unpack_bf16_pairs(xs_ref[rows, :])
            gate = jnp.dot(xb, wg_b[...], preferred_element_type=F32)
            up = jnp.dot(xb, wu_b[...], preferred_element_type=F32)
            hid = (gate * jax.nn.sigmoid(gate) * up).astype(BF16)
            y_ref[rows, :] = jnp.dot(hid, wd_b[...], preferred_element_type=F32)

    @pl.when(b >= nused_ref[0])
    def _():
        y_ref[...] = jnp.zeros_like(y_ref)


def _expert_call(block_expert, n_used, xs, w_gate, w_up, w_down):
    P = xs.shape[0]
    NB = P // EXPERT_BLOCK
    E, D, F = w_gate.shape
    grid_spec = pltpu.PrefetchScalarGridSpec(
        num_scalar_prefetch=2,
        grid=(NB,),
        in_specs=[
            pl.BlockSpec((EXPERT_BLOCK,) + xs.shape[1:], lambda b, be, nu: (b, 0)),
            pl.BlockSpec((1, D, F), lambda b, be, nu: (be[b], 0, 0)),
            pl.BlockSpec((1, D, F), lambda b, be, nu: (be[b], 0, 0)),
            pl.BlockSpec((1, F, D), lambda b, be, nu: (be[b], 0, 0)),
        ],
        out_specs=pl.BlockSpec((EXPERT_BLOCK, D), lambda b, be, nu: (b, 0)),
        scratch_shapes=[
            pltpu.VMEM((D, F), BF16),
            pltpu.VMEM((D, F), BF16),
            pltpu.VMEM((F, D), BF16),
        ],
    )
    return pl.pallas_call(
        _expert_kernel,
        grid_spec=grid_spec,
        out_shape=jax.ShapeDtypeStruct((P, D), F32),
        compiler_params=pltpu.CompilerParams(
            dimension_semantics=("arbitrary",), vmem_limit_bytes=48 * 1024 * 1024),
        name="moe_experts",
    )(block_expert, n_used, xs, w_gate, w_up, w_down)


def _combine_kernel(dest_hbm, x1_ref, rt_ref, ys_hbm, fg_ref, o_ref, d_smem, ybuf, idx_sem, row_sem,
                    *, tm, final_norm):
    t = pl.program_id(0)
    nt = pl.num_programs(0)
    cur = t % 2
    nxt = 1 - cur

    def idx_copy(tile, s):
        return pltpu.make_async_copy(dest_hbm.at[tile], d_smem.at[s], idx_sem.at[s])

    def issue_rows(s):
        def issue(c, carry):
            r0 = pl.multiple_of(c * ROW_UNROLL, ROW_UNROLL)
            for k in range(TOP_K):
                rows = ybuf.at[s, k, pl.ds(r0, ROW_UNROLL)]
                for u in range(ROW_UNROLL):
                    pltpu.make_async_copy(ys_hbm.at[pl.ds(d_smem[s, k, r0 + u], 1)],
                                          rows.at[pl.ds(u, 1)], row_sem.at[s]).start()
            return carry

        lax.fori_loop(0, tm // ROW_UNROLL, issue, 0)

    @pl.when(t == 0)
    def _():
        first = idx_copy(0, 0)
        first.start()
        first.wait()
        issue_rows(0)

        @pl.when(nt > 1)
        def _():
            idx_copy(1, 1).start()

    @pl.when(t + 1 < nt)
    def _():
        idx_copy(t + 1, nxt).wait()
        issue_rows(nxt)

    @pl.when(t + 2 < nt)
    def _():
        idx_copy(t + 2, cur).start()

    for k in range(TOP_K):
        pltpu.make_async_copy(ys_hbm.at[pl.ds(0, tm)], ybuf.at[cur, k], row_sem.at[cur]).wait()
    rt = rt_ref[...]
    h = x1_ref[...] + rt[:, 2:3] * ybuf[cur, 0] + rt[:, 3:4] * ybuf[cur, 1]
    if final_norm:
        h = h * lax.rsqrt(jnp.mean(h * h, axis=-1, keepdims=True) + EPS) * fg_ref[...]
    o_ref[...] = h


def _combine_call(dest, x1, rt, ys, final_g, *, tm, final_norm):
    T, D = x1.shape
    return pl.pallas_call(
        functools.partial(_combine_kernel, tm=tm, final_norm=final_norm),
        grid=(T // tm,),
        in_specs=[
            pl.BlockSpec(memory_space=pl.ANY),
            pl.BlockSpec((tm, D), lambda t: (t, 0)),
            pl.BlockSpec((tm, ROUTER_COLS), lambda t: (t, 0)),
            pl.BlockSpec(memory_space=pl.ANY),
            pl.BlockSpec((1, D), lambda t: (0, 0)),
        ],
        out_specs=pl.BlockSpec((tm, D), lambda t: (t, 0)),
        out_shape=jax.ShapeDtypeStruct((T, D), F32),
        scratch_shapes=[
            pltpu.SMEM((2, SUBLANES, tm), jnp.int32),
            pltpu.VMEM((2, TOP_K, tm, D), F32),
            pltpu.SemaphoreType.DMA((2,)),
            pltpu.SemaphoreType.DMA((2,)),
        ],
        compiler_params=pltpu.CompilerParams(
            dimension_semantics=("arbitrary",), vmem_limit_bytes=40 * 1024 * 1024),
        name="moe_combine",
    )(dest, x1, rt, ys, final_g)


def _slot_layout(tile_counts, n_assign):
    NB = -(-n_assign // EXPERT_BLOCK) + N_EXPERTS
    counts = jnp.sum(tile_counts, axis=0)
    padded = ((counts + EXPERT_BLOCK - 1) // EXPERT_BLOCK) * EXPERT_BLOCK
    pad_end = jnp.cumsum(padded)
    pad_start = pad_end - padded
    tile_base = pad_start[None, :] + jnp.cumsum(tile_counts, axis=0) - tile_counts
    block_start = jnp.arange(NB, dtype=jnp.int32) * EXPERT_BLOCK
    block_expert = jnp.minimum(jnp.sum(pad_end[None, :] <= block_start[:, None], axis=1),
                               N_EXPERTS - 1).astype(jnp.int32)
    n_used = (pad_end[-1] // EXPERT_BLOCK).astype(jnp.int32).reshape(1)
    return NB, block_expert, n_used, tile_base


def _rope_tables(S):
    inv = 1.0 / (ROPE_THETA ** (jnp.arange(0, HEAD_DIM, 2, dtype=F32) / HEAD_DIM))
    ang = jnp.arange(S, dtype=F32)[:, None] * inv[None, :]
    cos, sin = jnp.cos(ang), jnp.sin(ang)
    cos_l = jnp.tile(cos, (1, LANES // (HEAD_DIM // 2)))
    sin_l = jnp.tile(jnp.concatenate([-sin, sin], axis=1), (1, LANES // HEAD_DIM))
    return cos_l, sin_l, cos.T, sin.T


def kernel(x, norm1_g, w_in, lambda_q1, lambda_k1, lambda_q2, lambda_k2, subln_g, sinks, w_out,
           norm2_g, w_router_group, b_router_group, w_router_expert, b_router_expert,
           w_gate, w_up, w_down, final_g):
    B, S, D = x.shape
    T = B * S
    depth = w_in.shape[0]
    tq, tk = 512, 512
    tm_proj = 512
    tm_tok = 512
    tq_swa = 512
    qscale = HEAD_DIM ** -0.5 * math.log2(math.e)
    cos_l, sin_l, cos_t, sin_t = _rope_tables(S)

    c0 = DIFF_QK_COLS
    c1 = 2 * DIFF_QK_COLS
    c2 = c1 + DIFF_V_COLS
    c3 = c2 + SWA_Q_COLS
    c4 = c3 + SWA_KV_COLS
    for l in range(depth):
        lambda_init = 0.8 - 0.6 * math.exp(-0.3 * l)
        w = w_in[l]
        w_nat = jnp.concatenate([w[:, c0:c1], w[:, c3:c4]], axis=1).astype(BF16)
        w_tr = jnp.concatenate([w[:, :c0] * qscale, w[:, c1:c2], w[:, c2:c3] * qscale, w[:, c4:]],
                               axis=1).T.astype(BF16)
        dqt, dk, dvt, sqt, sk, svt = _proj_call(
            x, norm1_g[l][None, :], w_nat, w_tr, cos_l, sin_l, cos_t, sin_t, tm=tm_proj, tk=tk)

        lam_p = jnp.stack([lambda_q1[l], lambda_k1[l], lambda_q2[l], lambda_k2[l]]).astype(F32)
        o_diff = _diff_call(lam_p, dqt, dk, dvt, subln_g[l][None, :].astype(F32),
                            tq=tq, tk=tk, lambda_init=lambda_init)
        sink_row = jnp.repeat(sinks[l].astype(F32) * math.log2(math.e), WINDOW)[None, :]
        o_swa = _swa_call(sink_row, sqt, sk, svt, tq=tq_swa)

        wo_b = w_out[l].astype(BF16)
        w_router = jnp.zeros((D, ROUTER_COLS), F32)
        w_router = w_router.at[:, :N_GROUPS].set(w_router_group[l])
        w_router = w_router.at[:, N_GROUPS:N_GROUPS + N_EXPERTS].set(w_router_expert[l])
        w_router_hi = w_router.astype(BF16)
        w_router_lo = (w_router - w_router_hi.astype(F32)).astype(BF16)
        w_router = jnp.concatenate([w_router_hi, w_router_lo], axis=1)
        b_router = jnp.zeros((1, ROUTER_COLS), F32)
        b_router = b_router.at[0, :N_GROUPS].set(b_router_group[l])
        b_router = b_router.at[0, N_GROUPS:N_GROUPS + N_EXPERTS].set(b_router_expert[l])
        x1, n2p, rt, cnt = _mix_call(x, o_diff, o_swa, wo_b, norm2_g[l][None, :], w_router, b_router, tm=tm_tok)

        rt2 = rt.reshape(T, ROUTER_COLS)
        tile_counts = cnt[:, :, 0, :N_EXPERTS].reshape(T // tm_tok, N_EXPERTS).astype(jnp.int32)
        NB, block_expert, n_used, tile_base = _slot_layout(tile_counts, T * TOP_K)
        tile_base = jnp.broadcast_to(tile_base.astype(F32)[:, :, None], (T // tm_tok, N_EXPERTS, LANES))
        xs_zero = jnp.zeros((NB * EXPERT_BLOCK,) + n2p.shape[2:], jnp.uint32)
        dest, xs = _dispatch_call(rt2, tile_base, n2p.reshape((T,) + n2p.shape[2:]), xs_zero, tm=tm_tok)
        ys = _expert_call(block_expert, n_used, xs, w_gate[l], w_up[l], w_down[l])
        x = _combine_call(dest, x1.reshape(T, D), rt2, ys, final_g[None, :],
                          tm=tm_tok, final_norm=(l == depth - 1)).reshape(B, S, D)
    return x
```

```python
import functools
import math

import jax
import jax.numpy as jnp
from jax import lax
from jax.experimental import pallas as pl
from jax.experimental.pallas import tpu as pltpu
from jax.experimental.pallas import tpu_sc as plsc

HEAD_DIM = 64
DIFF_HEADS = 4
DIFF_V_DIM = 2 * HEAD_DIM
SWA_Q_HEADS = 8
SWA_KV_HEADS = 2
SWA_GROUP = SWA_Q_HEADS // SWA_KV_HEADS
WINDOW = 128
ROPE_THETA = 10000.0
N_GROUPS = 4
EXPERTS_PER_GROUP = 8
N_EXPERTS = N_GROUPS * EXPERTS_PER_GROUP
TOP_K = 2
EXPERT_BLOCK = 512
EXPERT_CHUNK = 256
EPS = 1e-6
NEG = -1e30

DIFF_QK_COLS = DIFF_HEADS * 2 * HEAD_DIM
DIFF_V_COLS = DIFF_HEADS * DIFF_V_DIM
SWA_Q_COLS = SWA_Q_HEADS * HEAD_DIM
SWA_KV_COLS = SWA_KV_HEADS * HEAD_DIM
LANES = 128
SUBLANES = 8
BF16_SUBLANES = 16
VT_ROWS = DIFF_V_DIM + BF16_SUBLANES
SWA_VT_ROWS = SWA_KV_COLS + BF16_SUBLANES
ROUTER_COLS = LANES
DIFF_UNROLL = 4
DIFF_S_BUFS = 4

BF16 = jnp.bfloat16
F32 = jnp.float32


def _rope_lanes(x, cos_l, sin_l, first_half):
    rot = jnp.where(first_half, pltpu.roll(x, 96, 1), pltpu.roll(x, 32, 1))
    return x * cos_l + rot * sin_l


def _proj_kernel(x_ref, g_ref, wnat_ref, wtr_ref, cosl_ref, sinl_ref, cost_ref, sint_ref,
                 dqt_ref, dk_ref, dvt_ref, sqt_ref, sk_ref, svt_ref, *, tk):
    x = x_ref[0]
    tm = x.shape[0]
    n1 = x * lax.rsqrt(jnp.mean(x * x, axis=-1, keepdims=True) + EPS) * g_ref[...]
    n1b = n1.astype(BF16)
    nat = jnp.dot(n1b, wnat_ref[...], preferred_element_type=F32)
    tr = lax.dot_general(wtr_ref[...], n1b, (((1,), (1,)), ((), ())),
                         preferred_element_type=F32)

    cos_l, sin_l = cosl_ref[...], sinl_ref[...]
    first_half = (lax.broadcasted_iota(jnp.int32, (tm, LANES), 1) & (HEAD_DIM - 1)) < HEAD_DIM // 2
    for h in range(DIFF_HEADS):
        slab = nat[:, h * LANES:(h + 1) * LANES]
        dk_ref[0, h] = _rope_lanes(slab, cos_l, sin_l, first_half).astype(BF16)
    sk = _rope_lanes(nat[:, DIFF_QK_COLS:DIFF_QK_COLS + LANES], cos_l, sin_l, first_half).astype(BF16)
    for c in range(tm // WINDOW):
        sk_ref[0, c] = sk[c * WINDOW:(c + 1) * WINDOW]

    cos_t, sin_t = cost_ref[...], sint_ref[...]
    half = HEAD_DIM // 2

    def rope_rows(r0):
        x1 = tr[r0:r0 + half]
        x2 = tr[r0 + half:r0 + HEAD_DIM]
        return (x1 * cos_t - x2 * sin_t).astype(BF16), (x1 * sin_t + x2 * cos_t).astype(BF16)

    for h in range(DIFF_HEADS):
        for c in range(2):
            lo, hi = rope_rows(h * 2 * HEAD_DIM + c * HEAD_DIM)
            dqt_ref[0, h, c * HEAD_DIM:c * HEAD_DIM + half] = lo
            dqt_ref[0, h, c * HEAD_DIM + half:(c + 1) * HEAD_DIM] = hi
    ones_rows = (lax.broadcasted_iota(jnp.int32, (BF16_SUBLANES, tk), 0) == 0).astype(BF16)
    for h in range(DIFF_HEADS):
        r0 = DIFF_QK_COLS + h * DIFF_V_DIM
        for c in range(tm // tk):
            dvt_ref[0, h, c, :DIFF_V_DIM] = tr[r0:r0 + DIFF_V_DIM, c * tk:(c + 1) * tk].astype(BF16)
            dvt_ref[0, h, c, DIFF_V_DIM:] = ones_rows

    r0 = DIFF_QK_COLS + DIFF_V_COLS
    for h in range(SWA_Q_HEADS):
        lo, hi = rope_rows(r0 + h * HEAD_DIM)
        sqt_ref[0, h * HEAD_DIM:h * HEAD_DIM + half] = lo
        sqt_ref[0, h * HEAD_DIM + half:(h + 1) * HEAD_DIM] = hi
    r0 += SWA_Q_COLS
    for c in range(tm // WINDOW):
        svt_ref[0, c, :SWA_KV_COLS] = tr[r0:r0 + SWA_KV_COLS, c * WINDOW:(c + 1) * WINDOW].astype(BF16)
        svt_ref[0, c, SWA_KV_COLS:] = ones_rows[:, :WINDOW]


def _proj_call(x, g1, w_nat, w_tr, cos_l, sin_l, cos_t, sin_t, *, tm, tk):
    B, S, D = x.shape
    nkv = S // tk
    grid = (B, S // tm)
    const = lambda b, i: (0, 0)
    out_shape = (
        jax.ShapeDtypeStruct((B, DIFF_HEADS, 2 * HEAD_DIM, S), BF16),
        jax.ShapeDtypeStruct((B, DIFF_HEADS, S, 2 * HEAD_DIM), BF16),
        jax.ShapeDtypeStruct((B, DIFF_HEADS, nkv, VT_ROWS, tk), BF16),
        jax.ShapeDtypeStruct((B, SWA_Q_COLS, S), BF16),
        jax.ShapeDtypeStruct((B, S // WINDOW, WINDOW, SWA_KV_COLS), BF16),
        jax.ShapeDtypeStruct((B, S // WINDOW, SWA_VT_ROWS, WINDOW), BF16),
    )
    return pl.pallas_call(
        functools.partial(_proj_kernel, tk=tk),
        grid=grid,
        in_specs=[
            pl.BlockSpec((1, tm, D), lambda b, i: (b, i, 0)),
            pl.BlockSpec((1, D), const),
            pl.BlockSpec(w_nat.shape, const),
            pl.BlockSpec(w_tr.shape, const),
            pl.BlockSpec((tm, LANES), lambda b, i: (i, 0)),
            pl.BlockSpec((tm, LANES), lambda b, i: (i, 0)),
            pl.BlockSpec((HEAD_DIM // 2, tm), lambda b, i: (0, i)),
            pl.BlockSpec((HEAD_DIM // 2, tm), lambda b, i: (0, i)),
        ],
        out_specs=(
            pl.BlockSpec((1, DIFF_HEADS, 2 * HEAD_DIM, tm), lambda b, i: (b, 0, 0, i)),
            pl.BlockSpec((1, DIFF_HEADS, tm, 2 * HEAD_DIM), lambda b, i: (b, 0, i, 0)),
            pl.BlockSpec((1, DIFF_HEADS, tm // tk, VT_ROWS, tk), lambda b, i: (b, 0, i, 0, 0)),
            pl.BlockSpec((1, SWA_Q_COLS, tm), lambda b, i: (b, 0, i)),
            pl.BlockSpec((1, tm // WINDOW, WINDOW, SWA_KV_COLS), lambda b, i: (b, i, 0, 0)),
            pl.BlockSpec((1, tm // WINDOW, SWA_VT_ROWS, WINDOW), lambda b, i: (b, i, 0, 0)),
        ),
        out_shape=out_shape,
        compiler_params=pltpu.CompilerParams(
            dimension_semantics=("parallel", "parallel"), vmem_limit_bytes=48 * 1024 * 1024),
        name="proj_rope",
    )(x, g1, w_nat, w_tr, cos_l, sin_l, cos_t, sin_t)


def _diff_kernel(lam_ref, qt_ref, k_ref, vt_ref, g_ref, o_ref, *scratch, tq, tk, lambda_init):
    i = pl.program_id(2)
    s_bufs = scratch[:DIFF_S_BUFS]
    top_bufs = scratch[DIFF_S_BUFS:2 * DIFF_S_BUFS]
    m_ref, acc_ref = scratch[2 * DIFF_S_BUFS:]
    qt = qt_ref[0, 0]
    z = jnp.zeros((HEAD_DIM, tq), BF16)
    qw = jnp.concatenate([jnp.concatenate([qt[:HEAD_DIM], z], axis=1),
                          jnp.concatenate([z, qt[HEAD_DIM:]], axis=1)], axis=0)

    def scores(j, par):
        kt = k_ref[0, 0, pl.ds(pl.multiple_of(j * tk, tk), tk), :]
        s = jnp.dot(kt, qw, preferred_element_type=F32)
        s_bufs[par][...] = s
        top_bufs[par][...] = jnp.max(s, axis=0, keepdims=True)

    def absorb(j, par, masked):
        s = s_bufs[par][...]
        if masked:
            kpos = j * tk + lax.broadcasted_iota(jnp.int32, (tk, 2 * tq), 0)
            qpos = i * tq + (lax.broadcasted_iota(jnp.int32, (tk, 2 * tq), 1) & (tq - 1))
            s = jnp.where(kpos <= qpos, s, NEG)
            top = jnp.max(s, axis=0, keepdims=True)
        else:
            top = top_bufs[par][...]
        m = m_ref[...]
        m_new = jnp.maximum(m, top)
        alpha = jnp.exp2(m - m_new)
        p = jnp.exp2(s - m_new).astype(BF16)
        m_ref[...] = m_new
        pv = jnp.dot(vt_ref[0, 0, j], p, preferred_element_type=F32)
        acc_ref[...] = alpha * acc_ref[...] + pv

    m_ref[...] = jnp.full(m_ref.shape, NEG, F32)
    acc_ref[...] = jnp.zeros(acc_ref.shape, F32)

    nfull = (i * tq) // tk
    scores(nfull, 0)
    scores(0, 1)
    absorb(nfull, 0, True)

    def group(t, c):
        j = DIFF_UNROLL * t
        for idx in range(DIFF_UNROLL):
            scores(j + idx + 1, (idx + 2) % DIFF_S_BUFS)
            absorb(j + idx, (idx + 1) % DIFF_S_BUFS, False)
        return c

    lax.fori_loop(0, nfull // DIFF_UNROLL, group, 0)

    for rem in range(1, DIFF_UNROLL):
        @pl.when(nfull % DIFF_UNROLL == rem)
        def _():
            first = nfull - rem
            for idx in range(rem):
                if idx + 1 < rem:
                    scores(first + idx + 1, (idx + 2) % DIFF_S_BUFS)
                absorb(first + idx, (idx + 1) % DIFF_S_BUFS, False)

    lam_p = lam_ref[...]
    lam = (jnp.exp(jnp.sum(lam_p[0:1] * lam_p[1:2], axis=-1, keepdims=True))
           - jnp.exp(jnp.sum(lam_p[2:3] * lam_p[3:4], axis=-1, keepdims=True)) + lambda_init)
    l = acc_ref[DIFF_V_DIM:DIFF_V_DIM + 1, :]
    o = (acc_ref[:DIFF_V_DIM, :tq] / l[:, :tq]
         - lam * (acc_ref[:DIFF_V_DIM, tq:] / l[:, tq:]))
    o = o * lax.rsqrt(jnp.mean(o * o, axis=0, keepdims=True) + EPS)
    o_ref[0] = (o.T * g_ref[...] * (1.0 - lambda_init)).astype(BF16)


def _diff_call(lam_p, dqt, dk, dvt, subln_g, *, tq, tk, lambda_init):
    B, H, _, S = dqt.shape
    assert tk % tq == 0 and S % tk == 0, "one key tile must cover a query tile's diagonal"
    nkv = S // tk
    grid = (B, H, S // tq)
    return pl.pallas_call(
        functools.partial(_diff_kernel, tq=tq, tk=tk, lambda_init=lambda_init),
        grid=grid,
        in_specs=[
            pl.BlockSpec(lam_p.shape, lambda b, h, i: (0, 0)),
            pl.BlockSpec((1, 1, 2 * HEAD_DIM, tq), lambda b, h, i: (b, h, 0, i)),
            pl.BlockSpec((1, 1, S, 2 * HEAD_DIM), lambda b, h, i: (b, h, 0, 0)),
            pl.BlockSpec((1, 1, nkv, VT_ROWS, tk), lambda b, h, i: (b, h, 0, 0, 0)),
            pl.BlockSpec((1, DIFF_V_DIM), lambda b, h, i: (0, 0)),
        ],
        out_specs=pl.BlockSpec((1, tq, DIFF_V_DIM), lambda b, h, i: (b, i, h)),
        out_shape=jax.ShapeDtypeStruct((B, S, DIFF_V_COLS), BF16),
        scratch_shapes=[pltpu.VMEM((tk, 2 * tq), F32)] * DIFF_S_BUFS + [
            pltpu.VMEM((1, 2 * tq), F32)] * DIFF_S_BUFS + [
            pltpu.VMEM((1, 2 * tq), F32),
            pltpu.VMEM((VT_ROWS, 2 * tq), F32),
        ],
        compiler_params=pltpu.CompilerParams(
            dimension_semantics=("parallel", "parallel", "arbitrary"),
            vmem_limit_bytes=48 * 1024 * 1024),
        name="diff_attn",
    )(lam_p, dqt, dk, dvt, subln_g)


def _swa_kernel(sink_ref, qt_ref, k_ref, vt_ref, o_ref, *, tq):
    i = pl.program_id(1)
    n_cols = SWA_Q_HEADS * WINDOW
    half_cols = n_cols // SWA_KV_HEADS
    sink = sink_ref[...]
    row = lax.broadcasted_iota(jnp.int32, (2 * WINDOW, WINDOW), 0)
    qrel = lax.broadcasted_iota(jnp.int32, (2 * WINDOW, WINDOW), 1)
    band = (row - WINDOW <= qrel) & (row > qrel)
    in_current = row >= WINDOW
    z = jnp.zeros((HEAD_DIM, half_cols), BF16)
    for sub in range(tq // WINDOW):
        n = i * (tq // WINDOW) + sub
        prev = jnp.maximum(n - 1, 0)
        kwin = jnp.concatenate([k_ref[0, prev], k_ref[0, n]], axis=0)
        vtwin = jnp.concatenate([vt_ref[0, prev], vt_ref[0, n]], axis=1)
        qt = qt_ref[0, :, sub * WINDOW:(sub + 1) * WINDOW]
        heads = [qt[h * HEAD_DIM:(h + 1) * HEAD_DIM] for h in range(SWA_Q_HEADS)]
        qw = jnp.concatenate(
            [jnp.concatenate(heads[:SWA_GROUP] + [z], axis=1),
             jnp.concatenate([z] + heads[SWA_GROUP:], axis=1)], axis=0)
        s = jnp.dot(kwin, qw, preferred_element_type=F32)
        valid = band & (in_current | (n >= 1))
        s = jnp.concatenate(
            [jnp.where(valid, s[:, h * WINDOW:(h + 1) * WINDOW], NEG) for h in range(SWA_Q_HEADS)], axis=1)
        m = jnp.maximum(jnp.max(s, axis=0, keepdims=True), sink)
        p = jnp.exp2(s - m).astype(BF16)
        acc = jnp.dot(vtwin, p, preferred_element_type=F32)
        den = acc[SWA_KV_COLS:SWA_KV_COLS + 1] + jnp.exp2(sink - m)
        on = acc[:SWA_KV_COLS] / den
        u = jnp.concatenate([on[:HEAD_DIM, :half_cols], on[HEAD_DIM:, half_cols:]], axis=1)
        for hp in range(SWA_Q_HEADS // 2):
            two = jnp.concatenate([u[:, (2 * hp) * WINDOW:(2 * hp + 1) * WINDOW],
                                   u[:, (2 * hp + 1) * WINDOW:(2 * hp + 2) * WINDOW]], axis=0)
            o_ref[0, sub * WINDOW:(sub + 1) * WINDOW, hp * LANES:(hp + 1) * LANES] = two.T.astype(BF16)


def _swa_call(sink_row, sqt, sk, svt, *, tq):
    B, _, S = sqt.shape
    nb = S // WINDOW
    return pl.pallas_call(
        functools.partial(_swa_kernel, tq=tq),
        grid=(B, S // tq),
        in_specs=[
            pl.BlockSpec(sink_row.shape, lambda b, i: (0, 0)),
            pl.BlockSpec((1, SWA_Q_COLS, tq), lambda b, i: (b, 0, i)),
            pl.BlockSpec((1, nb, WINDOW, SWA_KV_COLS), lambda b, i: (b, 0, 0, 0)),
            pl.BlockSpec((1, nb, SWA_VT_ROWS, WINDOW), lambda b, i: (b, 0, 0, 0)),
        ],
        out_specs=pl.BlockSpec((1, tq, SWA_Q_COLS), lambda b, i: (b, i, 0)),
        out_shape=jax.ShapeDtypeStruct((B, S, SWA_Q_COLS), BF16),
        compiler_params=pltpu.CompilerParams(
            dimension_semantics=("parallel", "arbitrary"), vmem_limit_bytes=40 * 1024 * 1024),
        name="swa_attn",
    )(sink_row, sqt, sk, svt)


def _pack_bf16_pairs(x):
    n = x.shape[1] // 2
    lo = lax.bitcast_convert_type(x[:, :n].astype(BF16).astype(F32), jnp.uint32)
    hi = lax.bitcast_convert_type(x[:, n:].astype(BF16).astype(F32), jnp.uint32)
    return (lo >> 16) | (hi & jnp.uint32(0xFFFF0000))


def _unpack_bf16_pairs(w):
    lo = lax.bitcast_convert_type(w << 16, F32)
    hi = lax.bitcast_convert_type(w & jnp.uint32(0xFFFF0000), F32)
    return jnp.concatenate([lo, hi], axis=1).astype(BF16)


def _mix_kernel(x_ref, od_ref, os_ref, wo_ref, g2_ref, wr_ref, br_ref, x1_ref, n2_ref, rt_ref, cnt_ref):
    h = (x_ref[0]
         + jnp.dot(od_ref[0], wo_ref[:DIFF_V_COLS], preferred_element_type=F32)
         + jnp.dot(os_ref[0], wo_ref[DIFF_V_COLS:], preferred_element_type=F32))
    x1_ref[0] = h
    n2 = h * lax.rsqrt(jnp.mean(h * h, axis=-1, keepdims=True) + EPS) * g2_ref[...]
    n2_ref[0] = _pack_bf16_pairs(n2)
    tm = n2.shape[0]
    n2_hi = n2.astype(BF16)
    n2_lo = (n2 - n2_hi.astype(F32)).astype(BF16)
    parts = jnp.dot(jnp.concatenate([n2_hi, n2_lo], axis=0), wr_ref[...],
                    preferred_element_type=F32)
    logits = ((parts[:tm, :ROUTER_COLS] + parts[tm:, ROUTER_COLS:])
              + (parts[:tm, ROUTER_COLS:] + parts[tm:, :ROUTER_COLS])) + br_ref[...]
    lane = lax.broadcasted_iota(jnp.int32, (tm, ROUTER_COLS), 1)
    big = jnp.int32(ROUTER_COLS)
    gl = jnp.where(lane < N_GROUPS, logits, -jnp.inf)
    gm = jnp.max(gl, axis=-1, keepdims=True)
    p_top = 1.0 / jnp.sum(jnp.exp(gl - gm), axis=-1, keepdims=True)
    g_idx = jnp.min(jnp.where(gl == gm, lane, big), axis=-1, keepdims=True)
    e_lo = N_GROUPS + EXPERTS_PER_GROUP * g_idx
    el = jnp.where((lane >= e_lo) & (lane < e_lo + EXPERTS_PER_GROUP), logits, -jnp.inf)
    v1 = jnp.max(el, axis=-1, keepdims=True)
    i1 = jnp.min(jnp.where(el == v1, lane, big), axis=-1, keepdims=True)
    el2 = jnp.where(lane == i1, -jnp.inf, el)
    v2 = jnp.max(el2, axis=-1, keepdims=True)
    i2 = jnp.min(jnp.where(el2 == v2, lane, big), axis=-1, keepdims=True)
    e21 = jnp.exp(v2 - v1)
    gate1 = p_top / (1.0 + e21)
    gate2 = p_top * e21 / (1.0 + e21)
    rt = jnp.where(lane == 0, (i1 - N_GROUPS).astype(F32),
         jnp.where(lane == 1, (i2 - N_GROUPS).astype(F32),
         jnp.where(lane == 2, gate1, jnp.where(lane == 3, gate2, 0.0))))
    rt_ref[0] = rt
    chosen = ((lane == i1 - N_GROUPS) | (lane == i2 - N_GROUPS)).astype(F32)
    cnt_ref[0, 0] = jnp.broadcast_to(jnp.sum(chosen, axis=0, keepdims=True), cnt_ref.shape[2:])


def _mix_call(x, o_diff, o_swa, w_out, g2, w_router, b_router, *, tm):
    B, S, D = x.shape
    const = lambda b, i: (0, 0)
    row = lambda b, i: (b, i, 0)
    nt = S // tm
    return pl.pallas_call(
        _mix_kernel,
        grid=(B, nt),
        in_specs=[
            pl.BlockSpec((1, tm, D), row),
            pl.BlockSpec((1, tm, DIFF_V_COLS), row),
            pl.BlockSpec((1, tm, SWA_Q_COLS), row),
            pl.BlockSpec(w_out.shape, const),
            pl.BlockSpec((1, D), const),
            pl.BlockSpec(w_router.shape, const),
            pl.BlockSpec((1, ROUTER_COLS), const),
        ],
        out_specs=(pl.BlockSpec((1, tm, D), row), pl.BlockSpec((1, tm, D // 2), row),
                   pl.BlockSpec((1, tm, ROUTER_COLS), row),
                   pl.BlockSpec((1, 1, SUBLANES, ROUTER_COLS), lambda b, i: (b, i, 0, 0))),
        out_shape=(jax.ShapeDtypeStruct((B, S, D), F32), jax.ShapeDtypeStruct((B, S, D // 2), jnp.uint32),
                   jax.ShapeDtypeStruct((B, S, ROUTER_COLS), F32),
                   jax.ShapeDtypeStruct((B, nt, SUBLANES, ROUTER_COLS), F32)),
        compiler_params=pltpu.CompilerParams(
            dimension_semantics=("parallel", "parallel"), vmem_limit_bytes=48 * 1024 * 1024),
        name="outproj_router",
    )(x, o_diff, o_swa, w_out, g2, w_router, b_router)


ROW_UNROLL = 8


def _dispatch_kernel(rt_ref, base_ref, n2_ref, xs_in_hbm, dest_ref, xs_hbm, d_vmem, d_smem, idx_sem, row_sem,
                     *, tm):
    del xs_in_hbm
    rt_t = rt_ref[...].T
    e1 = rt_t[0:1].astype(jnp.int32)
    e2 = rt_t[1:2].astype(jnp.int32)
    eid = lax.broadcasted_iota(jnp.int32, (N_EXPERTS, tm), 0)
    oh1 = eid == e1
    oh2 = eid == e2
    earlier = (lax.broadcasted_iota(jnp.int32, (tm, tm), 0)
               < lax.broadcasted_iota(jnp.int32, (tm, tm), 1)).astype(BF16)
    before = jnp.dot((oh1 | oh2).astype(BF16), earlier, preferred_element_type=F32)
    slot = before + base_ref[0][:, 0:1]
    d1 = jnp.sum(jnp.where(oh1, slot, 0.0), axis=0, keepdims=True).astype(jnp.int32)
    d2 = jnp.sum(jnp.where(oh2, slot, 0.0), axis=0, keepdims=True).astype(jnp.int32)
    d = jnp.concatenate([d1, d2, jnp.zeros((SUBLANES - TOP_K, tm), jnp.int32)], axis=0)
    dest_ref[0] = d
    d_vmem[...] = d
    idx_copy = pltpu.make_async_copy(d_vmem, d_smem, idx_sem)
    idx_copy.start()
    idx_copy.wait()

    def issue(c, carry):
        r0 = pl.multiple_of(c * ROW_UNROLL, ROW_UNROLL)
        rows = n2_ref.at[pl.ds(r0, ROW_UNROLL)]
        for u in range(ROW_UNROLL):
            for k in range(TOP_K):
                pltpu.make_async_copy(rows.at[pl.ds(u, 1)], xs_hbm.at[pl.ds(d_smem[k, r0 + u], 1)], row_sem).start()
        return carry

    lax.fori_loop(0, tm // ROW_UNROLL, issue, 0)
    for k in range(TOP_K):
        pltpu.make_async_copy(n2_ref, xs_hbm.at[pl.ds(0, tm)], row_sem).wait()


def _dispatch_call(rt, tile_base, n2p, xs_zero, *, tm):
    T = n2p.shape[0]
    nt = T // tm
    return pl.pallas_call(
        functools.partial(_dispatch_kernel, tm=tm),
        grid=(nt,),
        in_specs=[
            pl.BlockSpec((tm, ROUTER_COLS), lambda t: (t, 0)),
            pl.BlockSpec((1, N_EXPERTS, LANES), lambda t: (t, 0, 0)),
            pl.BlockSpec((tm,) + n2p.shape[1:], lambda t: (t, 0)),
            pl.BlockSpec(memory_space=pl.ANY),
        ],
        out_specs=(pl.BlockSpec((1, SUBLANES, tm), lambda t: (t, 0, 0)),
                   pl.BlockSpec(memory_space=pl.ANY)),
        out_shape=(jax.ShapeDtypeStruct((nt, SUBLANES, tm), jnp.int32),
                   jax.ShapeDtypeStruct(xs_zero.shape, xs_zero.dtype)),
        input_output_aliases={3: 1},
        scratch_shapes=[
            pltpu.VMEM((SUBLANES, tm), jnp.int32),
            pltpu.SMEM((SUBLANES, tm), jnp.int32),
            pltpu.SemaphoreType.DMA(()),
            pltpu.SemaphoreType.DMA(()),
        ],
        compiler_params=pltpu.CompilerParams(
            dimension_semantics=("arbitrary",), vmem_limit_bytes=40 * 1024 * 1024),
        name="moe_dispatch",
    )(rt, tile_base, n2p, xs_zero)


def _expert_kernel(be_ref, nused_ref, xs_ref, wg_ref, wu_ref, wd_ref, y_ref, wg_b, wu_b, wd_b):
    b = pl.program_id(0)

    @pl.when(b < nused_ref[0])
    def _():
        @pl.when((b == 0) | (be_ref[b] != be_ref[jnp.maximum(b - 1, 0)]))
        def _():
            wg_b[...] = wg_ref[0].astype(BF16)
            wu_b[...] = wu_ref[0].astype(BF16)
            wd_b[...] = wd_ref[0].astype(BF16)

        for c in range(EXPERT_BLOCK // EXPERT_CHUNK):
            rows = pl.ds(c * EXPERT_CHUNK, EXPERT_CHUNK)
            xb = _unpack_bf16_pairs(xs_ref[rows, :])
            gate = jnp.dot(xb, wg_b[...], preferred_element_type=F32)
            up = jnp.dot(xb, wu_b[...], preferred_element_type=F32)
            hid = (gate * jax.nn.sigmoid(gate) * up).astype(BF16)
            y_ref[rows, :] = jnp.dot(hid, wd_b[...], preferred_element_type=F32)

    @pl.when(b >= nused_ref[0])
    def _():
        y_ref[...] = jnp.zeros_like(y_ref)


def _expert_call(block_expert, n_used, xs, w_gate, w_up, w_down):
    P = xs.shape[0]
    NB = P // EXPERT_BLOCK
    E, D, F = w_gate.shape
    grid_spec = pltpu.PrefetchScalarGridSpec(
        num_scalar_prefetch=2,
        grid=(NB,),
        in_specs=[
            pl.BlockSpec((EXPERT_BLOCK,) + xs.shape[1:], lambda b, be, nu: (b, 0)),
            pl.BlockSpec((1, D, F), lambda b, be, nu: (be[b], 0, 0)),
            pl.BlockSpec((1, D, F), lambda b, be, nu: (be[b], 0, 0)),
            pl.BlockSpec((1, F, D), lambda b, be, nu: (be[b], 0, 0)),
        ],
        out_specs=pl.BlockSpec((EXPERT_BLOCK, D), lambda b, be, nu: (b, 0)),
        scratch_shapes=[
            pltpu.VMEM((D, F), BF16),
            pltpu.VMEM((D, F), BF16),
            pltpu.VMEM((F, D), BF16),
        ],
    )
    return pl.pallas_call(
        _expert_kernel,
        grid_spec=grid_spec,
        out_shape=jax.ShapeDtypeStruct((P, D), F32),
        compiler_params=pltpu.CompilerParams(
            dimension_semantics=("arbitrary",), vmem_limit_bytes=48 * 1024 * 1024),
        name="moe_experts",
    )(block_expert, n_used, xs, w_gate, w_up, w_down)


SC_GATHER_CHUNK = 64


def _sc_gather_rows(table, idx):
    n_rows, width = idx.shape[0], table.shape[1]
    info = plsc.get_sparse_core_info()
    n_workers = info.num_cores * info.num_subcores
    per_worker = n_rows // n_workers
    assert n_rows % (n_workers * SC_GATHER_CHUNK) == 0
    mesh = plsc.VectorSubcoreMesh(core_axis_name="c", subcore_axis_name="s")

    @functools.partial(
        pl.kernel, mesh=mesh,
        out_type=jax.ShapeDtypeStruct((n_rows, width), table.dtype),
        scratch_types=[
            pltpu.VMEM((SC_GATHER_CHUNK,), jnp.int32),
            pltpu.VMEM((SC_GATHER_CHUNK, width), table.dtype),
            pltpu.SemaphoreType.DMA,
        ],
    )
    def gather(table_hbm, idx_hbm, out_hbm, idx_v, rows_v, sem):
        worker = lax.axis_index("s") * info.num_cores + lax.axis_index("c")
        base = worker * per_worker

        @pl.loop(0, per_worker // SC_GATHER_CHUNK)
        def _(c):
            off = pl.multiple_of(base + c * SC_GATHER_CHUNK, SC_GATHER_CHUNK)
            pltpu.sync_copy(idx_hbm.at[pl.ds(off, SC_GATHER_CHUNK)], idx_v)
            pltpu.async_copy(table_hbm.at[idx_v], rows_v, sem).wait()
            pltpu.sync_copy(rows_v, out_hbm.at[pl.ds(off, SC_GATHER_CHUNK)])

    return gather(table, idx)


def _combine_kernel(x1_ref, rt_ref, y_ref, fg_ref, o_ref, *, final_norm):
    rt = rt_ref[...]
    h = x1_ref[...] + rt[:, 2:3] * y_ref[0, 0] + rt[:, 3:4] * y_ref[0, 1]
    if final_norm:
        h = h * lax.rsqrt(jnp.mean(h * h, axis=-1, keepdims=True) + EPS) * fg_ref[...]
    o_ref[...] = h


def _combine_call(x1, rt, ysg, final_g, *, tm, final_norm):
    T, D = x1.shape
    return pl.pallas_call(
        functools.partial(_combine_kernel, final_norm=final_norm),
        grid=(T // tm,),
        in_specs=[
            pl.BlockSpec((tm, D), lambda t: (t, 0)),
            pl.BlockSpec((tm, ROUTER_COLS), lambda t: (t, 0)),
            pl.BlockSpec((1, TOP_K, tm, D), lambda t: (t, 0, 0, 0)),
            pl.BlockSpec((1, D), lambda t: (0, 0)),
        ],
        out_specs=pl.BlockSpec((tm, D), lambda t: (t, 0)),
        out_shape=jax.ShapeDtypeStruct((T, D), F32),
        compiler_params=pltpu.CompilerParams(
            dimension_semantics=("parallel",), vmem_limit_bytes=40 * 1024 * 1024),
        name="moe_combine",
    )(x1, rt, ysg, final_g)


def _slot_layout(tile_counts, n_assign):
    NB = -(-n_assign // EXPERT_BLOCK) + N_EXPERTS
    counts = jnp.sum(tile_counts, axis=0)
    padded = ((counts + EXPERT_BLOCK - 1) // EXPERT_BLOCK) * EXPERT_BLOCK
    pad_end = jnp.cumsum(padded)
    pad_start = pad_end - padded
    tile_base = pad_start[None, :] + jnp.cumsum(tile_counts, axis=0) - tile_counts
    block_start = jnp.arange(NB, dtype=jnp.int32) * EXPERT_BLOCK
    block_expert = jnp.minimum(jnp.sum(pad_end[None, :] <= block_start[:, None], axis=1),
                               N_EXPERTS - 1).astype(jnp.int32)
    n_used = (pad_end[-1] // EXPERT_BLOCK).astype(jnp.int32).reshape(1)
    return NB, block_expert, n_used, tile_base


def _rope_tables(S):
    inv = 1.0 / (ROPE_THETA ** (jnp.arange(0, HEAD_DIM, 2, dtype=F32) / HEAD_DIM))
    ang = jnp.arange(S, dtype=F32)[:, None] * inv[None, :]
    cos, sin = jnp.cos(ang), jnp.sin(ang)
    cos_l = jnp.tile(cos, (1, LANES // (HEAD_DIM // 2)))
    sin_l = jnp.tile(jnp.concatenate([-sin, sin], axis=1), (1, LANES // HEAD_DIM))
    return cos_l, sin_l, cos.T, sin.T


def kernel(x, norm1_g, w_in, lambda_q1, lambda_k1, lambda_q2, lambda_k2, subln_g, sinks, w_out,
           norm2_g, w_router_group, b_router_group, w_router_expert, b_router_expert,
           w_gate, w_up, w_down, final_g):
    B, S, D = x.shape
    T = B * S
    depth = w_in.shape[0]
    tq, tk = 512, 512
    tm_proj = 512
    tm_tok = 512
    tq_swa = 512
    qscale = HEAD_DIM ** -0.5 * math.log2(math.e)
    cos_l, sin_l, cos_t, sin_t = _rope_tables(S)

    c0 = DIFF_QK_COLS
    c1 = 2 * DIFF_QK_COLS
    c2 = c1 + DIFF_V_COLS
    c3 = c2 + SWA_Q_COLS
    c4 = c3 + SWA_KV_COLS
    for l in range(depth):
        lambda_init = 0.8 - 0.6 * math.exp(-0.3 * l)
        w = w_in[l]
        w_nat = jnp.concatenate([w[:, c0:c1], w[:, c3:c4]], axis=1).astype(BF16)
        w_tr = jnp.concatenate([w[:, :c0] * qscale, w[:, c1:c2], w[:, c2:c3] * qscale, w[:, c4:]],
                               axis=1).T.astype(BF16)
        dqt, dk, dvt, sqt, sk, svt = _proj_call(
            x, norm1_g[l][None, :], w_nat, w_tr, cos_l, sin_l, cos_t, sin_t, tm=tm_proj, tk=tk)

        lam_p = jnp.stack([lambda_q1[l], lambda_k1[l], lambda_q2[l], lambda_k2[l]]).astype(F32)
        o_diff = _diff_call(lam_p, dqt, dk, dvt, subln_g[l][None, :].astype(F32),
                            tq=tq, tk=tk, lambda_init=lambda_init)
        sink_row = jnp.repeat(sinks[l].astype(F32) * math.log2(math.e), WINDOW)[None, :]
        o_swa = _swa_call(sink_row, sqt, sk, svt, tq=tq_swa)

        wo_b = w_out[l].astype(BF16)
        w_router = jnp.zeros((D, ROUTER_COLS), F32)
        w_router = w_router.at[:, :N_GROUPS].set(w_router_group[l])
        w_router = w_router.at[:, N_GROUPS:N_GROUPS + N_EXPERTS].set(w_router_expert[l])
        w_router_hi = w_router.astype(BF16)
        w_router_lo = (w_router - w_router_hi.astype(F32)).astype(BF16)
        w_router = jnp.concatenate([w_router_hi, w_router_lo], axis=1)
        b_router = jnp.zeros((1, ROUTER_COLS), F32)
        b_router = b_router.at[0, :N_GROUPS].set(b_router_group[l])
        b_router = b_router.at[0, N_GROUPS:N_GROUPS + N_EXPERTS].set(b_router_expert[l])
        x1, n2p, rt, cnt = _mix_call(x, o_diff, o_swa, wo_b, norm2_g[l][None, :], w_router, b_router, tm=tm_tok)

        rt2 = rt.reshape(T, ROUTER_COLS)
        tile_counts = cnt[:, :, 0, :N_EXPERTS].reshape(T // tm_tok, N_EXPERTS).astype(jnp.int32)
        NB, block_expert, n_used, tile_base = _slot_layout(tile_counts, T * TOP_K)
        tile_base = jnp.broadcast_to(tile_base.astype(F32)[:, :, None], (T // tm_tok, N_EXPERTS, LANES))
        xs_zero = jnp.zeros((NB * EXPERT_BLOCK,) + n2p.shape[2:], jnp.uint32)
        dest, xs = _dispatch_call(rt2, tile_base, n2p.reshape((T,) + n2p.shape[2:]), xs_zero, tm=tm_tok)
        ys = _expert_call(block_expert, n_used, xs, w_gate[l], w_up[l], w_down[l])
        ysg = _sc_gather_rows(ys, dest[:, :TOP_K, :].reshape(T * TOP_K))
        x = _combine_call(x1.reshape(T, D), rt2, ysg.reshape(T // tm_tok, TOP_K, tm_tok, D), final_g[None, :],
                          tm=tm_tok, final_norm=(l == depth - 1)).reshape(B, S, D)
    return x
```

```python
import functools
import math

import jax
import jax.numpy as jnp
from jax import lax
from jax.experimental import pallas as pl
from jax.experimental.pallas import tpu as pltpu
from jax.experimental.pallas import tpu_sc as plsc

HEAD_DIM = 64
DIFF_HEADS = 4
DIFF_V_DIM = 2 * HEAD_DIM
SWA_Q_HEADS = 8
SWA_KV_HEADS = 2
SWA_GROUP = SWA_Q_HEADS // SWA_KV_HEADS
WINDOW = 128
ROPE_THETA = 10000.0
N_GROUPS = 4
EXPERTS_PER_GROUP = 8
N_EXPERTS = N_GROUPS * EXPERTS_PER_GROUP
TOP_K = 2
EXPERT_BLOCK = 512
EXPERT_CHUNK = 256
EPS = 1e-6
NEG = -1e30

DIFF_QK_COLS = DIFF_HEADS * 2 * HEAD_DIM
DIFF_V_COLS = DIFF_HEADS * DIFF_V_DIM
SWA_Q_COLS = SWA_Q_HEADS * HEAD_DIM
SWA_KV_COLS = SWA_KV_HEADS * HEAD_DIM
LANES = 128
SUBLANES = 8
BF16_SUBLANES = 16
VT_ROWS = DIFF_V_DIM + BF16_SUBLANES
SWA_VT_ROWS = SWA_KV_COLS + BF16_SUBLANES
ROUTER_COLS = LANES
DIFF_UNROLL = 4
DIFF_S_BUFS = 4

BF16 = jnp.bfloat16
F32 = jnp.float32


def _rope_lanes(x, cos_l, sin_l, first_half):
    rot = jnp.where(first_half, pltpu.roll(x, 96, 1), pltpu.roll(x, 32, 1))
    return x * cos_l + rot * sin_l


def _proj_kernel(x_ref, g_ref, wnat_ref, wtr_ref, cosl_ref, sinl_ref, cost_ref, sint_ref,
                 dqt_ref, dk_ref, dvt_ref, sqt_ref, sk_ref, svt_ref, *, tk):
    x = x_ref[0]
    tm = x.shape[0]
    n1 = x * lax.rsqrt(jnp.mean(x * x, axis=-1, keepdims=True) + EPS) * g_ref[...]
    n1b = n1.astype(BF16)
    nat = jnp.dot(n1b, wnat_ref[...], preferred_element_type=F32)
    tr = lax.dot_general(wtr_ref[...], n1b, (((1,), (1,)), ((), ())),
                         preferred_element_type=F32)

    cos_l, sin_l = cosl_ref[...], sinl_ref[...]
    first_half = (lax.broadcasted_iota(jnp.int32, (tm, LANES), 1) & (HEAD_DIM - 1)) < HEAD_DIM // 2
    for h in range(DIFF_HEADS):
        slab = nat[:, h * LANES:(h + 1) * LANES]
        dk_ref[0, h] = _rope_lanes(slab, cos_l, sin_l, first_half).astype(BF16)
    sk = _rope_lanes(nat[:, DIFF_QK_COLS:DIFF_QK_COLS + LANES], cos_l, sin_l, first_half).astype(BF16)
    for c in range(tm // WINDOW):
        sk_ref[0, c] = sk[c * WINDOW:(c + 1) * WINDOW]

    cos_t, sin_t = cost_ref[...], sint_ref[...]
    half = HEAD_DIM // 2

    def rope_rows(r0):
        x1 = tr[r0:r0 + half]
        x2 = tr[r0 + half:r0 + HEAD_DIM]
        return (x1 * cos_t - x2 * sin_t).astype(BF16), (x1 * sin_t + x2 * cos_t).astype(BF16)

    for h in range(DIFF_HEADS):
        for c in range(2):
            lo, hi = rope_rows(h * 2 * HEAD_DIM + c * HEAD_DIM)
            dqt_ref[0, h, c * HEAD_DIM:c * HEAD_DIM + half] = lo
            dqt_ref[0, h, c * HEAD_DIM + half:(c + 1) * HEAD_DIM] = hi
    ones_rows = (lax.broadcasted_iota(jnp.int32, (BF16_SUBLANES, tk), 0) == 0).astype(BF16)
    for h in range(DIFF_HEADS):
        r0 = DIFF_QK_COLS + h * DIFF_V_DIM
        for c in range(tm // tk):
            dvt_ref[0, h, c, :DIFF_V_DIM] = tr[r0:r0 + DIFF_V_DIM, c * tk:(c + 1) * tk].astype(BF16)
            dvt_ref[0, h, c, DIFF_V_DIM:] = ones_rows

    r0 = DIFF_QK_COLS + DIFF_V_COLS
    for h in range(SWA_Q_HEADS):
        lo, hi = rope_rows(r0 + h * HEAD_DIM)
        sqt_ref[0, h * HEAD_DIM:h * HEAD_DIM + half] = lo
        sqt_ref[0, h * HEAD_DIM + half:(h + 1) * HEAD_DIM] = hi
    r0 += SWA_Q_COLS
    for c in range(tm // WINDOW):
        svt_ref[0, c, :SWA_KV_COLS] = tr[r0:r0 + SWA_KV_COLS, c * WINDOW:(c + 1) * WINDOW].astype(BF16)
        svt_ref[0, c, SWA_KV_COLS:] = ones_rows[:, :WINDOW]


def _proj_call(x, g1, w_nat, w_tr, cos_l, sin_l, cos_t, sin_t, *, tm, tk):
    B, S, D = x.shape
    nkv = S // tk
    grid = (B, S // tm)
    const = lambda b, i: (0, 0)
    out_shape = (
        jax.ShapeDtypeStruct((B, DIFF_HEADS, 2 * HEAD_DIM, S), BF16),
        jax.ShapeDtypeStruct((B, DIFF_HEADS, S, 2 * HEAD_DIM), BF16),
        jax.ShapeDtypeStruct((B, DIFF_HEADS, nkv, VT_ROWS, tk), BF16),
        jax.ShapeDtypeStruct((B, SWA_Q_COLS, S), BF16),
        jax.ShapeDtypeStruct((B, S // WINDOW, WINDOW, SWA_KV_COLS), BF16),
        jax.ShapeDtypeStruct((B, S // WINDOW, SWA_VT_ROWS, WINDOW), BF16),
    )
    return pl.pallas_call(
        functools.partial(_proj_kernel, tk=tk),
        grid=grid,
        in_specs=[
            pl.BlockSpec((1, tm, D), lambda b, i: (b, i, 0)),
            pl.BlockSpec((1, D), const),
            pl.BlockSpec(w_nat.shape, const),
            pl.BlockSpec(w_tr.shape, const),
            pl.BlockSpec((tm, LANES), lambda b, i: (i, 0)),
            pl.BlockSpec((tm, LANES), lambda b, i: (i, 0)),
            pl.BlockSpec((HEAD_DIM // 2, tm), lambda b, i: (0, i)),
            pl.BlockSpec((HEAD_DIM // 2, tm), lambda b, i: (0, i)),
        ],
        out_specs=(
            pl.BlockSpec((1, DIFF_HEADS, 2 * HEAD_DIM, tm), lambda b, i: (b, 0, 0, i)),
            pl.BlockSpec((1, DIFF_HEADS, tm, 2 * HEAD_DIM), lambda b, i: (b, 0, i, 0)),
            pl.BlockSpec((1, DIFF_HEADS, tm // tk, VT_ROWS, tk), lambda b, i: (b, 0, i, 0, 0)),
            pl.BlockSpec((1, SWA_Q_COLS, tm), lambda b, i: (b, 0, i)),
            pl.BlockSpec((1, tm // WINDOW, WINDOW, SWA_KV_COLS), lambda b, i: (b, i, 0, 0)),
            pl.BlockSpec((1, tm // WINDOW, SWA_VT_ROWS, WINDOW), lambda b, i: (b, i, 0, 0)),
        ),
        out_shape=out_shape,
        compiler_params=pltpu.CompilerParams(
            dimension_semantics=("parallel", "parallel"), vmem_limit_bytes=48 * 1024 * 1024),
        name="proj_rope",
    )(x, g1, w_nat, w_tr, cos_l, sin_l, cos_t, sin_t)


def _diff_kernel(lam_ref, qt_ref, k_ref, vt_ref, g_ref, o_ref, *scratch, tq, tk, lambda_init):
    i = pl.program_id(2)
    s_bufs = scratch[:DIFF_S_BUFS]
    top_bufs = scratch[DIFF_S_BUFS:2 * DIFF_S_BUFS]
    m_ref, acc_ref = scratch[2 * DIFF_S_BUFS:]
    qt = qt_ref[0, 0]
    z = jnp.zeros((HEAD_DIM, tq), BF16)
    qw = jnp.concatenate([jnp.concatenate([qt[:HEAD_DIM], z], axis=1),
                          jnp.concatenate([z, qt[HEAD_DIM:]], axis=1)], axis=0)

    def scores(j, par):
        kt = k_ref[0, 0, pl.ds(pl.multiple_of(j * tk, tk), tk), :]
        s = jnp.dot(kt, qw, preferred_element_type=F32)
        s_bufs[par][...] = s
        top_bufs[par][...] = jnp.max(s, axis=0, keepdims=True)

    def absorb(j, par, masked):
        s = s_bufs[par][...]
        if masked:
            kpos = j * tk + lax.broadcasted_iota(jnp.int32, (tk, 2 * tq), 0)
            qpos = i * tq + (lax.broadcasted_iota(jnp.int32, (tk, 2 * tq), 1) & (tq - 1))
            s = jnp.where(kpos <= qpos, s, NEG)
            top = jnp.max(s, axis=0, keepdims=True)
        else:
            top = top_bufs[par][...]
        m = m_ref[...]
        m_new = jnp.maximum(m, top)
        alpha = jnp.exp2(m - m_new)
        p = jnp.exp2(s - m_new).astype(BF16)
        m_ref[...] = m_new
        pv = jnp.dot(vt_ref[0, 0, j], p, preferred_element_type=F32)
        acc_ref[...] = alpha * acc_ref[...] + pv

    m_ref[...] = jnp.full(m_ref.shape, NEG, F32)
    acc_ref[...] = jnp.zeros(acc_ref.shape, F32)

    nfull = (i * tq) // tk
    scores(nfull, 0)
    scores(0, 1)
    absorb(nfull, 0, True)

    def group(t, c):
        j = DIFF_UNROLL * t
        for idx in range(DIFF_UNROLL):
            scores(j + idx + 1, (idx + 2) % DIFF_S_BUFS)
            absorb(j + idx, (idx + 1) % DIFF_S_BUFS, False)
        return c

    lax.fori_loop(0, nfull // DIFF_UNROLL, group, 0)

    for rem in range(1, DIFF_UNROLL):
        @pl.when(nfull % DIFF_UNROLL == rem)
        def _():
            first = nfull - rem
            for idx in range(rem):
                if idx + 1 < rem:
                    scores(first + idx + 1, (idx + 2) % DIFF_S_BUFS)
                absorb(first + idx, (idx + 1) % DIFF_S_BUFS, False)

    lam_p = lam_ref[...]
    lam = (jnp.exp(jnp.sum(lam_p[0:1] * lam_p[1:2], axis=-1, keepdims=True))
           - jnp.exp(jnp.sum(lam_p[2:3] * lam_p[3:4], axis=-1, keepdims=True)) + lambda_init)
    l = acc_ref[DIFF_V_DIM:DIFF_V_DIM + 1, :]
    o = (acc_ref[:DIFF_V_DIM, :tq] / l[:, :tq]
         - lam * (acc_ref[:DIFF_V_DIM, tq:] / l[:, tq:]))
    o = o * lax.rsqrt(jnp.mean(o * o, axis=0, keepdims=True) + EPS)
    o_ref[0] = (o.T * g_ref[...] * (1.0 - lambda_init)).astype(BF16)


def _diff_call(lam_p, dqt, dk, dvt, subln_g, *, tq, tk, lambda_init):
    B, H, _, S = dqt.shape
    assert tk % tq == 0 and S % tk == 0, "one key tile must cover a query tile's diagonal"
    nkv = S // tk
    grid = (B, H, S // tq)
    return pl.pallas_call(
        functools.partial(_diff_kernel, tq=tq, tk=tk, lambda_init=lambda_init),
        grid=grid,
        in_specs=[
            pl.BlockSpec(lam_p.shape, lambda b, h, i: (0, 0)),
            pl.BlockSpec((1, 1, 2 * HEAD_DIM, tq), lambda b, h, i: (b, h, 0, i)),
            pl.BlockSpec((1, 1, S, 2 * HEAD_DIM), lambda b, h, i: (b, h, 0, 0)),
            pl.BlockSpec((1, 1, nkv, VT_ROWS, tk), lambda b, h, i: (b, h, 0, 0, 0)),
            pl.BlockSpec((1, DIFF_V_DIM), lambda b, h, i: (0, 0)),
        ],
        out_specs=pl.BlockSpec((1, tq, DIFF_V_DIM), lambda b, h, i: (b, i, h)),
        out_shape=jax.ShapeDtypeStruct((B, S, DIFF_V_COLS), BF16),
        scratch_shapes=[pltpu.VMEM((tk, 2 * tq), F32)] * DIFF_S_BUFS + [
            pltpu.VMEM((1, 2 * tq), F32)] * DIFF_S_BUFS + [
            pltpu.VMEM((1, 2 * tq), F32),
            pltpu.VMEM((VT_ROWS, 2 * tq), F32),
        ],
        compiler_params=pltpu.CompilerParams(
            dimension_semantics=("parallel", "parallel", "arbitrary"),
            vmem_limit_bytes=48 * 1024 * 1024),
        name="diff_attn",
    )(lam_p, dqt, dk, dvt, subln_g)


def _swa_kernel(sink_ref, qt_ref, k_ref, vt_ref, o_ref, *, tq):
    i = pl.program_id(1)
    n_cols = SWA_Q_HEADS * WINDOW
    half_cols = n_cols // SWA_KV_HEADS
    sink = sink_ref[...]
    row = lax.broadcasted_iota(jnp.int32, (2 * WINDOW, WINDOW), 0)
    qrel = lax.broadcasted_iota(jnp.int32, (2 * WINDOW, WINDOW), 1)
    band = (row - WINDOW <= qrel) & (row > qrel)
    in_current = row >= WINDOW
    z = jnp.zeros((HEAD_DIM, half_cols), BF16)
    for sub in range(tq // WINDOW):
        n = i * (tq // WINDOW) + sub
        prev = jnp.maximum(n - 1, 0)
        kwin = jnp.concatenate([k_ref[0, prev], k_ref[0, n]], axis=0)
        vtwin = jnp.concatenate([vt_ref[0, prev], vt_ref[0, n]], axis=1)
        qt = qt_ref[0, :, sub * WINDOW:(sub + 1) * WINDOW]
        heads = [qt[h * HEAD_DIM:(h + 1) * HEAD_DIM] for h in range(SWA_Q_HEADS)]
        qw = jnp.concatenate(
            [jnp.concatenate(heads[:SWA_GROUP] + [z], axis=1),
             jnp.concatenate([z] + heads[SWA_GROUP:], axis=1)], axis=0)
        s = jnp.dot(kwin, qw, preferred_element_type=F32)
        valid = band & (in_current | (n >= 1))
        s = jnp.concatenate(
            [jnp.where(valid, s[:, h * WINDOW:(h + 1) * WINDOW], NEG) for h in range(SWA_Q_HEADS)], axis=1)
        m = jnp.maximum(jnp.max(s, axis=0, keepdims=True), sink)
        p = jnp.exp2(s - m).astype(BF16)
        acc = jnp.dot(vtwin, p, preferred_element_type=F32)
        den = acc[SWA_KV_COLS:SWA_KV_COLS + 1] + jnp.exp2(sink - m)
        on = acc[:SWA_KV_COLS] / den
        u = jnp.concatenate([on[:HEAD_DIM, :half_cols], on[HEAD_DIM:, half_cols:]], axis=1)
        for hp in range(SWA_Q_HEADS // 2):
            two = jnp.concatenate([u[:, (2 * hp) * WINDOW:(2 * hp + 1) * WINDOW],
                                   u[:, (2 * hp + 1) * WINDOW:(2 * hp + 2) * WINDOW]], axis=0)
            o_ref[0, sub * WINDOW:(sub + 1) * WINDOW, hp * LANES:(hp + 1) * LANES] = two.T.astype(BF16)


def _swa_call(sink_row, sqt, sk, svt, *, tq):
    B, _, S = sqt.shape
    nb = S // WINDOW
    return pl.pallas_call(
        functools.partial(_swa_kernel, tq=tq),
        grid=(B, S // tq),
        in_specs=[
            pl.BlockSpec(sink_row.shape, lambda b, i: (0, 0)),
            pl.BlockSpec((1, SWA_Q_COLS, tq), lambda b, i: (b, 0, i)),
            pl.BlockSpec((1, nb, WINDOW, SWA_KV_COLS), lambda b, i: (b, 0, 0, 0)),
            pl.BlockSpec((1, nb, SWA_VT_ROWS, WINDOW), lambda b, i: (b, 0, 0, 0)),
        ],
        out_specs=pl.BlockSpec((1, tq, SWA_Q_COLS), lambda b, i: (b, i, 0)),
        out_shape=jax.ShapeDtypeStruct((B, S, SWA_Q_COLS), BF16),
        compiler_params=pltpu.CompilerParams(
            dimension_semantics=("parallel", "arbitrary"), vmem_limit_bytes=40 * 1024 * 1024),
        name="swa_attn",
    )(sink_row, sqt, sk, svt)


def _pack_bf16_pairs(x):
    n = x.shape[1] // 2
    lo = lax.bitcast_convert_type(x[:, :n].astype(BF16).astype(F32), jnp.uint32)
    hi = lax.bitcast_convert_type(x[:, n:].astype(BF16).astype(F32), jnp.uint32)
    return (lo >> 16) | (hi & jnp.uint32(0xFFFF0000))


def _unpack_bf16_pairs(w):
    lo = lax.bitcast_convert_type(w << 16, F32)
    hi = lax.bitcast_convert_type(w & jnp.uint32(0xFFFF0000), F32)
    return jnp.concatenate([lo, hi], axis=1).astype(BF16)


def _mix_kernel(x_ref, od_ref, os_ref, wo_ref, g2_ref, wr_ref, br_ref, x1_ref, n2_ref, rt_ref, cnt_ref):
    h = (x_ref[0]
         + jnp.dot(od_ref[0], wo_ref[:DIFF_V_COLS], preferred_element_type=F32)
         + jnp.dot(os_ref[0], wo_ref[DIFF_V_COLS:], preferred_element_type=F32))
    x1_ref[0] = h
    n2 = h * lax.rsqrt(jnp.mean(h * h, axis=-1, keepdims=True) + EPS) * g2_ref[...]
    n2_ref[0] = _pack_bf16_pairs(n2)
    tm = n2.shape[0]
    n2_hi = n2.astype(BF16)
    n2_lo = (n2 - n2_hi.astype(F32)).astype(BF16)
    parts = jnp.dot(jnp.concatenate([n2_hi, n2_lo], axis=0), wr_ref[...],
                    preferred_element_type=F32)
    logits = ((parts[:tm, :ROUTER_COLS] + parts[tm:, ROUTER_COLS:])
              + (parts[:tm, ROUTER_COLS:] + parts[tm:, :ROUTER_COLS])) + br_ref[...]
    lane = lax.broadcasted_iota(jnp.int32, (tm, ROUTER_COLS), 1)
    big = jnp.int32(ROUTER_COLS)
    gl = jnp.where(lane < N_GROUPS, logits, -jnp.inf)
    gm = jnp.max(gl, axis=-1, keepdims=True)
    p_top = 1.0 / jnp.sum(jnp.exp(gl - gm), axis=-1, keepdims=True)
    g_idx = jnp.min(jnp.where(gl == gm, lane, big), axis=-1, keepdims=True)
    e_lo = N_GROUPS + EXPERTS_PER_GROUP * g_idx
    el = jnp.where((lane >= e_lo) & (lane < e_lo + EXPERTS_PER_GROUP), logits, -jnp.inf)
    v1 = jnp.max(el, axis=-1, keepdims=True)
    i1 = jnp.min(jnp.where(el == v1, lane, big), axis=-1, keepdims=True)
    el2 = jnp.where(lane == i1, -jnp.inf, el)
    v2 = jnp.max(el2, axis=-1, keepdims=True)
    i2 = jnp.min(jnp.where(el2 == v2, lane, big), axis=-1, keepdims=True)
    e21 = jnp.exp(v2 - v1)
    gate1 = p_top / (1.0 + e21)
    gate2 = p_top * e21 / (1.0 + e21)
    rt = jnp.where(lane == 0, (i1 - N_GROUPS).astype(F32),
         jnp.where(lane == 1, (i2 - N_GROUPS).astype(F32),
         jnp.where(lane == 2, gate1, jnp.where(lane == 3, gate2, 0.0))))
    rt_ref[0] = rt
    chosen = ((lane == i1 - N_GROUPS) | (lane == i2 - N_GROUPS)).astype(F32)
    cnt_ref[0, 0] = jnp.broadcast_to(jnp.sum(chosen, axis=0, keepdims=True), cnt_ref.shape[2:])


def _mix_call(x, o_diff, o_swa, w_out, g2, w_router, b_router, *, tm):
    B, S, D = x.shape
    const = lambda b, i: (0, 0)
    row = lambda b, i: (b, i, 0)
    nt = S // tm
    return pl.pallas_call(
        _mix_kernel,
        grid=(B, nt),
        in_specs=[
            pl.BlockSpec((1, tm, D), row),
            pl.BlockSpec((1, tm, DIFF_V_COLS), row),
            pl.BlockSpec((1, tm, SWA_Q_COLS), row),
            pl.BlockSpec(w_out.shape, const),
            pl.BlockSpec((1, D), const),
            pl.BlockSpec(w_router.shape, const),
            pl.BlockSpec((1, ROUTER_COLS), const),
        ],
        out_specs=(pl.BlockSpec((1, tm, D), row), pl.BlockSpec((1, tm, D // 2), row),
                   pl.BlockSpec((1, tm, ROUTER_COLS), row),
                   pl.BlockSpec((1, 1, SUBLANES, ROUTER_COLS), lambda b, i: (b, i, 0, 0))),
        out_shape=(jax.ShapeDtypeStruct((B, S, D), F32), jax.ShapeDtypeStruct((B, S, D // 2), jnp.uint32),
                   jax.ShapeDtypeStruct((B, S, ROUTER_COLS), F32),
                   jax.ShapeDtypeStruct((B, nt, SUBLANES, ROUTER_COLS), F32)),
        compiler_params=pltpu.CompilerParams(
            dimension_semantics=("parallel", "parallel"), vmem_limit_bytes=48 * 1024 * 1024),
        name="outproj_router",
    )(x, o_diff, o_swa, w_out, g2, w_router, b_router)


def _slot_kernel(rt_ref, base_ref, dest_ref, *, tm):
    rt_t = rt_ref[...].T
    e1 = rt_t[0:1].astype(jnp.int32)
    e2 = rt_t[1:2].astype(jnp.int32)
    eid = lax.broadcasted_iota(jnp.int32, (N_EXPERTS, tm), 0)
    oh1 = eid == e1
    oh2 = eid == e2
    earlier = (lax.broadcasted_iota(jnp.int32, (tm, tm), 0)
               < lax.broadcasted_iota(jnp.int32, (tm, tm), 1)).astype(BF16)
    before = jnp.dot((oh1 | oh2).astype(BF16), earlier, preferred_element_type=F32)
    slot = before + base_ref[0][:, 0:1]
    d1 = jnp.sum(jnp.where(oh1, slot, 0.0), axis=0, keepdims=True).astype(jnp.int32)
    d2 = jnp.sum(jnp.where(oh2, slot, 0.0), axis=0, keepdims=True).astype(jnp.int32)
    dest_ref[0] = jnp.concatenate([d1, d2, jnp.zeros((SUBLANES - TOP_K, tm), jnp.int32)], axis=0)


def _slot_call(rt, tile_base, *, tm):
    nt = rt.shape[0] // tm
    return pl.pallas_call(
        functools.partial(_slot_kernel, tm=tm),
        grid=(nt,),
        in_specs=[
            pl.BlockSpec((tm, ROUTER_COLS), lambda t: (t, 0)),
            pl.BlockSpec((1, N_EXPERTS, LANES), lambda t: (t, 0, 0)),
        ],
        out_specs=pl.BlockSpec((1, SUBLANES, tm), lambda t: (t, 0, 0)),
        out_shape=jax.ShapeDtypeStruct((nt, SUBLANES, tm), jnp.int32),
        compiler_params=pltpu.CompilerParams(dimension_semantics=("parallel",)),
        name="moe_slots",
    )(rt, tile_base)


SC_ROW_CHUNK = 64


def _sc_workers():
    info = plsc.get_sparse_core_info()
    return info.num_cores, info.num_cores * info.num_subcores


def _sc_scatter_rows(rows, idx, n_out):
    n, width = rows.shape
    n_cores, n_workers = _sc_workers()
    n_chunks = n // SC_ROW_CHUNK
    per_worker = n_chunks // n_workers
    assert n_chunks % n_workers == 0
    mesh = plsc.VectorSubcoreMesh(core_axis_name="c", subcore_axis_name="s")

    @functools.partial(
        pl.kernel, mesh=mesh,
        out_type=jax.ShapeDtypeStruct((n_out, width), rows.dtype),
        scratch_types=[
            pltpu.VMEM((SC_ROW_CHUNK,), jnp.int32),
            pltpu.VMEM((SC_ROW_CHUNK, width), rows.dtype),
        ],
    )
    def scatter(rows_hbm, idx_hbm, out_hbm, idx_v, rows_v):
        worker = lax.axis_index("s") * n_cores + lax.axis_index("c")

        @pl.loop(0, per_worker)
        def _(i):
            c = worker * per_worker + i
            pltpu.sync_copy(rows_hbm.at[pl.ds(pl.multiple_of(c * SC_ROW_CHUNK, SC_ROW_CHUNK), SC_ROW_CHUNK)], rows_v)
            for k in range(TOP_K):
                pltpu.sync_copy(idx_hbm.at[k, c], idx_v)
                pltpu.sync_copy(rows_v, out_hbm.at[idx_v])

    return scatter(rows, idx)


def _expert_kernel(be_ref, nvalid_ref, xs_ref, wg_ref, wu_ref, wd_ref, y_ref, wg_b, wu_b, wd_b):
    b = pl.program_id(0)
    n_valid = nvalid_ref[b]

    @pl.when(n_valid > 0)
    def _():
        @pl.when((b == 0) | (be_ref[b] != be_ref[jnp.maximum(b - 1, 0)]))
        def _():
            wg_b[...] = wg_ref[0].astype(BF16)
            wu_b[...] = wu_ref[0].astype(BF16)
            wd_b[...] = wd_ref[0].astype(BF16)

        for c in range(EXPERT_BLOCK // EXPERT_CHUNK):
            rows = pl.ds(c * EXPERT_CHUNK, EXPERT_CHUNK)
            row_id = c * EXPERT_CHUNK + lax.broadcasted_iota(jnp.int32, (EXPERT_CHUNK, xs_ref.shape[1]), 0)
            packed = jnp.where(row_id < n_valid, xs_ref[rows, :], jnp.uint32(0))
            xb = _unpack_bf16_pairs(packed)
            gate = jnp.dot(xb, wg_b[...], preferred_element_type=F32)
            up = jnp.dot(xb, wu_b[...], preferred_element_type=F32)
            hid = (gate * jax.nn.sigmoid(gate) * up).astype(BF16)
            y_ref[rows, :] = jnp.dot(hid, wd_b[...], preferred_element_type=F32)

    @pl.when(n_valid == 0)
    def _():
        y_ref[...] = jnp.zeros_like(y_ref)


def _expert_call(block_expert, n_valid, xs, w_gate, w_up, w_down):
    P = xs.shape[0]
    NB = P // EXPERT_BLOCK
    E, D, F = w_gate.shape
    grid_spec = pltpu.PrefetchScalarGridSpec(
        num_scalar_prefetch=2,
        grid=(NB,),
        in_specs=[
            pl.BlockSpec((EXPERT_BLOCK,) + xs.shape[1:], lambda b, be, nu: (b, 0)),
            pl.BlockSpec((1, D, F), lambda b, be, nu: (be[b], 0, 0)),
            pl.BlockSpec((1, D, F), lambda b, be, nu: (be[b], 0, 0)),
            pl.BlockSpec((1, F, D), lambda b, be, nu: (be[b], 0, 0)),
        ],
        out_specs=pl.BlockSpec((EXPERT_BLOCK, D), lambda b, be, nu: (b, 0)),
        scratch_shapes=[
            pltpu.VMEM((D, F), BF16),
            pltpu.VMEM((D, F), BF16),
            pltpu.VMEM((F, D), BF16),
        ],
    )
    return pl.pallas_call(
        _expert_kernel,
        grid_spec=grid_spec,
        out_shape=jax.ShapeDtypeStruct((P, D), F32),
        compiler_params=pltpu.CompilerParams(
            dimension_semantics=("arbitrary",), vmem_limit_bytes=48 * 1024 * 1024),
        name="moe_experts",
    )(block_expert, n_valid, xs, w_gate, w_up, w_down)


def _sc_gather_rows(table, idx):
    n_rows, width = idx.shape[0], table.shape[1]
    n_cores, n_workers = _sc_workers()
    per_worker = n_rows // n_workers
    assert n_rows % (n_workers * SC_ROW_CHUNK) == 0
    mesh = plsc.VectorSubcoreMesh(core_axis_name="c", subcore_axis_name="s")

    @functools.partial(
        pl.kernel, mesh=mesh,
        out_type=jax.ShapeDtypeStruct((n_rows, width), table.dtype),
        scratch_types=[
            pltpu.VMEM((SC_ROW_CHUNK,), jnp.int32),
            pltpu.VMEM((SC_ROW_CHUNK, width), table.dtype),
        ],
    )
    def gather(table_hbm, idx_hbm, out_hbm, idx_v, rows_v):
        worker = lax.axis_index("s") * n_cores + lax.axis_index("c")
        base = worker * per_worker

        @pl.loop(0, per_worker // SC_ROW_CHUNK)
        def _(c):
            off = pl.multiple_of(base + c * SC_ROW_CHUNK, SC_ROW_CHUNK)
            pltpu.sync_copy(idx_hbm.at[pl.ds(off, SC_ROW_CHUNK)], idx_v)
            pltpu.sync_copy(table_hbm.at[idx_v], rows_v)
            pltpu.sync_copy(rows_v, out_hbm.at[pl.ds(off, SC_ROW_CHUNK)])

    return gather(table, idx)


def _combine_kernel(x1_ref, rt_ref, y_ref, fg_ref, o_ref, *, final_norm):
    rt = rt_ref[...]
    h = x1_ref[...] + rt[:, 2:3] * y_ref[0, 0] + rt[:, 3:4] * y_ref[0, 1]
    if final_norm:
        h = h * lax.rsqrt(jnp.mean(h * h, axis=-1, keepdims=True) + EPS) * fg_ref[...]
    o_ref[...] = h


def _combine_call(x1, rt, ysg, final_g, *, tm, final_norm):
    T, D = x1.shape
    return pl.pallas_call(
        functools.partial(_combine_kernel, final_norm=final_norm),
        grid=(T // tm,),
        in_specs=[
            pl.BlockSpec((tm, D), lambda t: (t, 0)),
            pl.BlockSpec((tm, ROUTER_COLS), lambda t: (t, 0)),
            pl.BlockSpec((1, TOP_K, tm, D), lambda t: (t, 0, 0, 0)),
            pl.BlockSpec((1, D), lambda t: (0, 0)),
        ],
        out_specs=pl.BlockSpec((tm, D), lambda t: (t, 0)),
        out_shape=jax.ShapeDtypeStruct((T, D), F32),
        compiler_params=pltpu.CompilerParams(
            dimension_semantics=("parallel",), vmem_limit_bytes=40 * 1024 * 1024),
        name="moe_combine",
    )(x1, rt, ysg, final_g)


def _slot_layout(tile_counts, n_assign):
    NB = -(-n_assign // EXPERT_BLOCK) + N_EXPERTS
    counts = jnp.sum(tile_counts, axis=0)
    padded = ((counts + EXPERT_BLOCK - 1) // EXPERT_BLOCK) * EXPERT_BLOCK
    pad_end = jnp.cumsum(padded)
    pad_start = pad_end - padded
    tile_base = pad_start[None, :] + jnp.cumsum(tile_counts, axis=0) - tile_counts
    block_start = jnp.arange(NB, dtype=jnp.int32) * EXPERT_BLOCK
    block_expert = jnp.minimum(jnp.sum(pad_end[None, :] <= block_start[:, None], axis=1),
                               N_EXPERTS - 1).astype(jnp.int32)
    run_end = (pad_start + counts)[block_expert]
    n_valid = jnp.clip(run_end - block_start, 0, EXPERT_BLOCK).astype(jnp.int32)
    return NB, block_expert, n_valid, tile_base


def _rope_tables(S):
    inv = 1.0 / (ROPE_THETA ** (jnp.arange(0, HEAD_DIM, 2, dtype=F32) / HEAD_DIM))
    ang = jnp.arange(S, dtype=F32)[:, None] * inv[None, :]
    cos, sin = jnp.cos(ang), jnp.sin(ang)
    cos_l = jnp.tile(cos, (1, LANES // (HEAD_DIM // 2)))
    sin_l = jnp.tile(jnp.concatenate([-sin, sin], axis=1), (1, LANES // HEAD_DIM))
    return cos_l, sin_l, cos.T, sin.T


def kernel(x, norm1_g, w_in, lambda_q1, lambda_k1, lambda_q2, lambda_k2, subln_g, sinks, w_out,
           norm2_g, w_router_group, b_router_group, w_router_expert, b_router_expert,
           w_gate, w_up, w_down, final_g):
    B, S, D = x.shape
    T = B * S
    depth = w_in.shape[0]
    tq, tk = 512, 512
    tm_proj = 512
    tm_tok = 512
    tq_swa = 512
    qscale = HEAD_DIM ** -0.5 * math.log2(math.e)
    cos_l, sin_l, cos_t, sin_t = _rope_tables(S)

    c0 = DIFF_QK_COLS
    c1 = 2 * DIFF_QK_COLS
    c2 = c1 + DIFF_V_COLS
    c3 = c2 + SWA_Q_COLS
    c4 = c3 + SWA_KV_COLS
    for l in range(depth):
        lambda_init = 0.8 - 0.6 * math.exp(-0.3 * l)
        w = w_in[l]
        w_nat = jnp.concatenate([w[:, c0:c1], w[:, c3:c4]], axis=1).astype(BF16)
        w_tr = jnp.concatenate([w[:, :c0] * qscale, w[:, c1:c2], w[:, c2:c3] * qscale, w[:, c4:]],
                               axis=1).T.astype(BF16)
        dqt, dk, dvt, sqt, sk, svt = _proj_call(
            x, norm1_g[l][None, :], w_nat, w_tr, cos_l, sin_l, cos_t, sin_t, tm=tm_proj, tk=tk)

        lam_p = jnp.stack([lambda_q1[l], lambda_k1[l], lambda_q2[l], lambda_k2[l]]).astype(F32)
        o_diff = _diff_call(lam_p, dqt, dk, dvt, subln_g[l][None, :].astype(F32),
                            tq=tq, tk=tk, lambda_init=lambda_init)
        sink_row = jnp.repeat(sinks[l].astype(F32) * math.log2(math.e), WINDOW)[None, :]
        o_swa = _swa_call(sink_row, sqt, sk, svt, tq=tq_swa)

        wo_b = w_out[l].astype(BF16)
        w_router = jnp.zeros((D, ROUTER_COLS), F32)
        w_router = w_router.at[:, :N_GROUPS].set(w_router_group[l])
        w_router = w_router.at[:, N_GROUPS:N_GROUPS + N_EXPERTS].set(w_router_expert[l])
        w_router_hi = w_router.astype(BF16)
        w_router_lo = (w_router - w_router_hi.astype(F32)).astype(BF16)
        w_router = jnp.concatenate([w_router_hi, w_router_lo], axis=1)
        b_router = jnp.zeros((1, ROUTER_COLS), F32)
        b_router = b_router.at[0, :N_GROUPS].set(b_router_group[l])
        b_router = b_router.at[0, N_GROUPS:N_GROUPS + N_EXPERTS].set(b_router_expert[l])
        x1, n2p, rt, cnt = _mix_call(x, o_diff, o_swa, wo_b, norm2_g[l][None, :], w_router, b_router, tm=tm_tok)

        rt2 = rt.reshape(T, ROUTER_COLS)
        tile_counts = cnt[:, :, 0, :N_EXPERTS].reshape(T // tm_tok, N_EXPERTS).astype(jnp.int32)
        NB, block_expert, n_valid, tile_base = _slot_layout(tile_counts, T * TOP_K)
        tile_base = jnp.broadcast_to(tile_base.astype(F32)[:, :, None], (T // tm_tok, N_EXPERTS, LANES))
        dest = _slot_call(rt2, tile_base, tm=tm_tok)
        scatter_idx = jnp.swapaxes(dest[:, :TOP_K, :], 0, 1).reshape(TOP_K, T // SC_ROW_CHUNK, SC_ROW_CHUNK)
        xs = _sc_scatter_rows(n2p.reshape(T, D // 2), scatter_idx, NB * EXPERT_BLOCK)
        ys = _expert_call(block_expert, n_valid, xs, w_gate[l], w_up[l], w_down[l])
        ysg = _sc_gather_rows(ys, dest[:, :TOP_K, :].reshape(T * TOP_K))
        x = _combine_call(x1.reshape(T, D), rt2, ysg.reshape(T // tm_tok, TOP_K, tm_tok, D), final_g[None, :],
                          tm=tm_tok, final_norm=(l == depth - 1)).reshape(B, S, D)
    return x
```

```python
import functools
import math

import jax
import jax.numpy as jnp
from jax import lax
from jax.experimental import pallas as pl
from jax.experimental.pallas import tpu as pltpu
from jax.experimental.pallas import tpu_sc as plsc

HEAD_DIM = 64
DIFF_HEADS = 4
DIFF_V_DIM = 2 * HEAD_DIM
SWA_Q_HEADS = 8
SWA_KV_HEADS = 2
SWA_GROUP = SWA_Q_HEADS // SWA_KV_HEADS
WINDOW = 128
ROPE_THETA = 10000.0
N_GROUPS = 4
EXPERTS_PER_GROUP = 8
N_EXPERTS = N_GROUPS * EXPERTS_PER_GROUP
TOP_K = 2
EXPERT_BLOCK = 512
EXPERT_CHUNK = 256
EPS = 1e-6
NEG = -1e30

DIFF_QK_COLS = DIFF_HEADS * 2 * HEAD_DIM
DIFF_V_COLS = DIFF_HEADS * DIFF_V_DIM
SWA_Q_COLS = SWA_Q_HEADS * HEAD_DIM
SWA_KV_COLS = SWA_KV_HEADS * HEAD_DIM
LANES = 128
SUBLANES = 8
BF16_SUBLANES = 16
VT_ROWS = DIFF_V_DIM + BF16_SUBLANES
SWA_VT_ROWS = SWA_KV_COLS + BF16_SUBLANES
ROUTER_COLS = LANES
DIFF_UNROLL = 4
DIFF_S_BUFS = 4

BF16 = jnp.bfloat16
F32 = jnp.float32


def _rope_lanes(x, cos_l, sin_l, first_half):
    rot = jnp.where(first_half, pltpu.roll(x, 96, 1), pltpu.roll(x, 32, 1))
    return x * cos_l + rot * sin_l


def _proj_kernel(x_ref, g_ref, wnat_ref, wtr_ref, cosl_ref, sinl_ref, cost_ref, sint_ref,
                 dqt_ref, dk_ref, dvt_ref, sqt_ref, sk_ref, svt_ref, *, tk):
    x = x_ref[0]
    tm = x.shape[0]
    n1 = x * lax.rsqrt(jnp.mean(x * x, axis=-1, keepdims=True) + EPS) * g_ref[...]
    n1b = n1.astype(BF16)
    nat = jnp.dot(n1b, wnat_ref[...], preferred_element_type=F32)
    tr = lax.dot_general(wtr_ref[...], n1b, (((1,), (1,)), ((), ())),
                         preferred_element_type=F32)

    cos_l, sin_l = cosl_ref[...], sinl_ref[...]
    first_half = (lax.broadcasted_iota(jnp.int32, (tm, LANES), 1) & (HEAD_DIM - 1)) < HEAD_DIM // 2
    for h in range(DIFF_HEADS):
        slab = nat[:, h * LANES:(h + 1) * LANES]
        dk_ref[0, h] = _rope_lanes(slab, cos_l, sin_l, first_half).astype(BF16)
    sk = _rope_lanes(nat[:, DIFF_QK_COLS:DIFF_QK_COLS + LANES], cos_l, sin_l, first_half).astype(BF16)
    for c in range(tm // WINDOW):
        sk_ref[0, c] = sk[c * WINDOW:(c + 1) * WINDOW]

    cos_t, sin_t = cost_ref[...], sint_ref[...]
    half = HEAD_DIM // 2

    def rope_rows(r0):
        x1 = tr[r0:r0 + half]
        x2 = tr[r0 + half:r0 + HEAD_DIM]
        return (x1 * cos_t - x2 * sin_t).astype(BF16), (x1 * sin_t + x2 * cos_t).astype(BF16)

    for h in range(DIFF_HEADS):
        for c in range(2):
            lo, hi = rope_rows(h * 2 * HEAD_DIM + c * HEAD_DIM)
            dqt_ref[0, h, c * HEAD_DIM:c * HEAD_DIM + half] = lo
            dqt_ref[0, h, c * HEAD_DIM + half:(c + 1) * HEAD_DIM] = hi
    ones_rows = (lax.broadcasted_iota(jnp.int32, (BF16_SUBLANES, tk), 0) == 0).astype(BF16)
    for h in range(DIFF_HEADS):
        r0 = DIFF_QK_COLS + h * DIFF_V_DIM
        for c in range(tm // tk):
            dvt_ref[0, h, c, :DIFF_V_DIM] = tr[r0:r0 + DIFF_V_DIM, c * tk:(c + 1) * tk].astype(BF16)
            dvt_ref[0, h, c, DIFF_V_DIM:] = ones_rows

    r0 = DIFF_QK_COLS + DIFF_V_COLS
    for h in range(SWA_Q_HEADS):
        lo, hi = rope_rows(r0 + h * HEAD_DIM)
        sqt_ref[0, h * HEAD_DIM:h * HEAD_DIM + half] = lo
        sqt_ref[0, h * HEAD_DIM + half:(h + 1) * HEAD_DIM] = hi
    r0 += SWA_Q_COLS
    for c in range(tm // WINDOW):
        svt_ref[0, c, :SWA_KV_COLS] = tr[r0:r0 + SWA_KV_COLS, c * WINDOW:(c + 1) * WINDOW].astype(BF16)
        svt_ref[0, c, SWA_KV_COLS:] = ones_rows[:, :WINDOW]


def _proj_call(x, g1, w_nat, w_tr, cos_l, sin_l, cos_t, sin_t, *, tm, tk):
    B, S, D = x.shape
    nkv = S // tk
    grid = (B, S // tm)
    const = lambda b, i: (0, 0)
    out_shape = (
        jax.ShapeDtypeStruct((B, DIFF_HEADS, 2 * HEAD_DIM, S), BF16),
        jax.ShapeDtypeStruct((B, DIFF_HEADS, S, 2 * HEAD_DIM), BF16),
        jax.ShapeDtypeStruct((B, DIFF_HEADS, nkv, VT_ROWS, tk), BF16),
        jax.ShapeDtypeStruct((B, SWA_Q_COLS, S), BF16),
        jax.ShapeDtypeStruct((B, S // WINDOW, WINDOW, SWA_KV_COLS), BF16),
        jax.ShapeDtypeStruct((B, S // WINDOW, SWA_VT_ROWS, WINDOW), BF16),
    )
    return pl.pallas_call(
        functools.partial(_proj_kernel, tk=tk),
        grid=grid,
        in_specs=[
            pl.BlockSpec((1, tm, D), lambda b, i: (b, i, 0)),
            pl.BlockSpec((1, D), const),
            pl.BlockSpec(w_nat.shape, const),
            pl.BlockSpec(w_tr.shape, const),
            pl.BlockSpec((tm, LANES), lambda b, i: (i, 0)),
            pl.BlockSpec((tm, LANES), lambda b, i: (i, 0)),
            pl.BlockSpec((HEAD_DIM // 2, tm), lambda b, i: (0, i)),
            pl.BlockSpec((HEAD_DIM // 2, tm), lambda b, i: (0, i)),
        ],
        out_specs=(
            pl.BlockSpec((1, DIFF_HEADS, 2 * HEAD_DIM, tm), lambda b, i: (b, 0, 0, i)),
            pl.BlockSpec((1, DIFF_HEADS, tm, 2 * HEAD_DIM), lambda b, i: (b, 0, i, 0)),
            pl.BlockSpec((1, DIFF_HEADS, tm // tk, VT_ROWS, tk), lambda b, i: (b, 0, i, 0, 0)),
            pl.BlockSpec((1, SWA_Q_COLS, tm), lambda b, i: (b, 0, i)),
            pl.BlockSpec((1, tm // WINDOW, WINDOW, SWA_KV_COLS), lambda b, i: (b, i, 0, 0)),
            pl.BlockSpec((1, tm // WINDOW, SWA_VT_ROWS, WINDOW), lambda b, i: (b, i, 0, 0)),
        ),
        out_shape=out_shape,
        compiler_params=pltpu.CompilerParams(
            dimension_semantics=("parallel", "parallel"), vmem_limit_bytes=48 * 1024 * 1024),
        name="proj_rope",
    )(x, g1, w_nat, w_tr, cos_l, sin_l, cos_t, sin_t)


def _diff_kernel(lam_ref, qt_ref, k_ref, vt_ref, g_ref, o_ref, *scratch, tq, tk, lambda_init):
    i = pl.program_id(2)
    s_bufs = scratch[:DIFF_S_BUFS]
    top_bufs = scratch[DIFF_S_BUFS:2 * DIFF_S_BUFS]
    m_ref, acc_ref = scratch[2 * DIFF_S_BUFS:]
    qt = qt_ref[0, 0]
    z = jnp.zeros((HEAD_DIM, tq), BF16)
    qw = jnp.concatenate([jnp.concatenate([qt[:HEAD_DIM], z], axis=1),
                          jnp.concatenate([z, qt[HEAD_DIM:]], axis=1)], axis=0)

    def scores(j, par):
        kt = k_ref[0, 0, pl.ds(pl.multiple_of(j * tk, tk), tk), :]
        s = jnp.dot(kt, qw, preferred_element_type=F32)
        s_bufs[par][...] = s
        top_bufs[par][...] = jnp.max(s, axis=0, keepdims=True)

    def absorb(j, par, masked):
        s = s_bufs[par][...]
        if masked:
            kpos = j * tk + lax.broadcasted_iota(jnp.int32, (tk, 2 * tq), 0)
            qpos = i * tq + (lax.broadcasted_iota(jnp.int32, (tk, 2 * tq), 1) & (tq - 1))
            s = jnp.where(kpos <= qpos, s, NEG)
            top = jnp.max(s, axis=0, keepdims=True)
        else:
            top = top_bufs[par][...]
        m = m_ref[...]
        m_new = jnp.maximum(m, top)
        alpha = jnp.exp2(m - m_new)
        p = jnp.exp2(s - m_new).astype(BF16)
        m_ref[...] = m_new
        pv = jnp.dot(vt_ref[0, 0, j], p, preferred_element_type=F32)
        acc_ref[...] = alpha * acc_ref[...] + pv

    m_ref[...] = jnp.full(m_ref.shape, NEG, F32)
    acc_ref[...] = jnp.zeros(acc_ref.shape, F32)

    nfull = (i * tq) // tk
    scores(nfull, 0)
    scores(0, 1)
    absorb(nfull, 0, True)

    def group(t, c):
        j = DIFF_UNROLL * t
        for idx in range(DIFF_UNROLL):
            scores(j + idx + 1, (idx + 2) % DIFF_S_BUFS)
            absorb(j + idx, (idx + 1) % DIFF_S_BUFS, False)
        return c

    lax.fori_loop(0, nfull // DIFF_UNROLL, group, 0)

    for rem in range(1, DIFF_UNROLL):
        @pl.when(nfull % DIFF_UNROLL == rem)
        def _():
            first = nfull - rem
            for idx in range(rem):
                if idx + 1 < rem:
                    scores(first + idx + 1, (idx + 2) % DIFF_S_BUFS)
                absorb(first + idx, (idx + 1) % DIFF_S_BUFS, False)

    lam_p = lam_ref[...]
    lam = (jnp.exp(jnp.sum(lam_p[0:1] * lam_p[1:2], axis=-1, keepdims=True))
           - jnp.exp(jnp.sum(lam_p[2:3] * lam_p[3:4], axis=-1, keepdims=True)) + lambda_init)
    l = acc_ref[DIFF_V_DIM:DIFF_V_DIM + 1, :]
    o = (acc_ref[:DIFF_V_DIM, :tq] / l[:, :tq]
         - lam * (acc_ref[:DIFF_V_DIM, tq:] / l[:, tq:]))
    o = o * lax.rsqrt(jnp.mean(o * o, axis=0, keepdims=True) + EPS)
    o_ref[0] = (o.T * g_ref[...] * (1.0 - lambda_init)).astype(BF16)


def _diff_call(lam_p, dqt, dk, dvt, subln_g, *, tq, tk, lambda_init):
    B, H, _, S = dqt.shape
    assert tk % tq == 0 and S % tk == 0, "one key tile must cover a query tile's diagonal"
    nkv = S // tk
    grid = (B, H, S // tq)
    return pl.pallas_call(
        functools.partial(_diff_kernel, tq=tq, tk=tk, lambda_init=lambda_init),
        grid=grid,
        in_specs=[
            pl.BlockSpec(lam_p.shape, lambda b, h, i: (0, 0)),
            pl.BlockSpec((1, 1, 2 * HEAD_DIM, tq), lambda b, h, i: (b, h, 0, i)),
            pl.BlockSpec((1, 1, S, 2 * HEAD_DIM), lambda b, h, i: (b, h, 0, 0)),
            pl.BlockSpec((1, 1, nkv, VT_ROWS, tk), lambda b, h, i: (b, h, 0, 0, 0)),
            pl.BlockSpec((1, DIFF_V_DIM), lambda b, h, i: (0, 0)),
        ],
        out_specs=pl.BlockSpec((1, tq, DIFF_V_DIM), lambda b, h, i: (b, i, h)),
        out_shape=jax.ShapeDtypeStruct((B, S, DIFF_V_COLS), BF16),
        scratch_shapes=[pltpu.VMEM((tk, 2 * tq), F32)] * DIFF_S_BUFS + [
            pltpu.VMEM((1, 2 * tq), F32)] * DIFF_S_BUFS + [
            pltpu.VMEM((1, 2 * tq), F32),
            pltpu.VMEM((VT_ROWS, 2 * tq), F32),
        ],
        compiler_params=pltpu.CompilerParams(
            dimension_semantics=("parallel", "parallel", "arbitrary"),
            vmem_limit_bytes=48 * 1024 * 1024),
        name="diff_attn",
    )(lam_p, dqt, dk, dvt, subln_g)


def _swa_kernel(sink_ref, qt_ref, k_ref, vt_ref, o_ref, *, tq):
    i = pl.program_id(1)
    n_cols = SWA_Q_HEADS * WINDOW
    half_cols = n_cols // SWA_KV_HEADS
    sink = sink_ref[...]
    row = lax.broadcasted_iota(jnp.int32, (2 * WINDOW, WINDOW), 0)
    qrel = lax.broadcasted_iota(jnp.int32, (2 * WINDOW, WINDOW), 1)
    band = (row - WINDOW <= qrel) & (row > qrel)
    in_current = row >= WINDOW
    z = jnp.zeros((HEAD_DIM, half_cols), BF16)
    for sub in range(tq // WINDOW):
        n = i * (tq // WINDOW) + sub
        prev = jnp.maximum(n - 1, 0)
        kwin = jnp.concatenate([k_ref[0, prev], k_ref[0, n]], axis=0)
        vtwin = jnp.concatenate([vt_ref[0, prev], vt_ref[0, n]], axis=1)
        qt = qt_ref[0, :, sub * WINDOW:(sub + 1) * WINDOW]
        heads = [qt[h * HEAD_DIM:(h + 1) * HEAD_DIM] for h in range(SWA_Q_HEADS)]
        qw = jnp.concatenate(
            [jnp.concatenate(heads[:SWA_GROUP] + [z], axis=1),
             jnp.concatenate([z] + heads[SWA_GROUP:], axis=1)], axis=0)
        s = jnp.dot(kwin, qw, preferred_element_type=F32)
        valid = band & (in_current | (n >= 1))
        s = jnp.concatenate(
            [jnp.where(valid, s[:, h * WINDOW:(h + 1) * WINDOW], NEG) for h in range(SWA_Q_HEADS)], axis=1)
        m = jnp.maximum(jnp.max(s, axis=0, keepdims=True), sink)
        p = jnp.exp2(s - m).astype(BF16)
        acc = jnp.dot(vtwin, p, preferred_element_type=F32)
        den = acc[SWA_KV_COLS:SWA_KV_COLS + 1] + jnp.exp2(sink - m)
        on = acc[:SWA_KV_COLS] / den
        u = jnp.concatenate([on[:HEAD_DIM, :half_cols], on[HEAD_DIM:, half_cols:]], axis=1)
        for hp in range(SWA_Q_HEADS // 2):
            two = jnp.concatenate([u[:, (2 * hp) * WINDOW:(2 * hp + 1) * WINDOW],
                                   u[:, (2 * hp + 1) * WINDOW:(2 * hp + 2) * WINDOW]], axis=0)
            o_ref[0, sub * WINDOW:(sub + 1) * WINDOW, hp * LANES:(hp + 1) * LANES] = two.T.astype(BF16)


def _swa_call(sink_row, sqt, sk, svt, *, tq):
    B, _, S = sqt.shape
    nb = S // WINDOW
    return pl.pallas_call(
        functools.partial(_swa_kernel, tq=tq),
        grid=(B, S // tq),
        in_specs=[
            pl.BlockSpec(sink_row.shape, lambda b, i: (0, 0)),
            pl.BlockSpec((1, SWA_Q_COLS, tq), lambda b, i: (b, 0, i)),
            pl.BlockSpec((1, nb, WINDOW, SWA_KV_COLS), lambda b, i: (b, 0, 0, 0)),
            pl.BlockSpec((1, nb, SWA_VT_ROWS, WINDOW), lambda b, i: (b, 0, 0, 0)),
        ],
        out_specs=pl.BlockSpec((1, tq, SWA_Q_COLS), lambda b, i: (b, i, 0)),
        out_shape=jax.ShapeDtypeStruct((B, S, SWA_Q_COLS), BF16),
        compiler_params=pltpu.CompilerParams(
            dimension_semantics=("parallel", "arbitrary"), vmem_limit_bytes=40 * 1024 * 1024),
        name="swa_attn",
    )(sink_row, sqt, sk, svt)


def _pack_bf16_pairs(x):
    n = x.shape[1] // 2
    lo = lax.bitcast_convert_type(x[:, :n].astype(BF16).astype(F32), jnp.uint32)
    hi = lax.bitcast_convert_type(x[:, n:].astype(BF16).astype(F32), jnp.uint32)
    return (lo >> 16) | (hi & jnp.uint32(0xFFFF0000))


def _unpack_bf16_pairs(w):
    lo = lax.bitcast_convert_type(w << 16, F32)
    hi = lax.bitcast_convert_type(w & jnp.uint32(0xFFFF0000), F32)
    return jnp.concatenate([lo, hi], axis=1).astype(BF16)


def _mix_kernel(x_ref, od_ref, os_ref, wo_ref, g2_ref, wr_ref, br_ref, x1_ref, n2_ref, rt_ref, cnt_ref):
    h = (x_ref[0]
         + jnp.dot(od_ref[0], wo_ref[:DIFF_V_COLS], preferred_element_type=F32)
         + jnp.dot(os_ref[0], wo_ref[DIFF_V_COLS:], preferred_element_type=F32))
    x1_ref[0] = h
    n2 = h * lax.rsqrt(jnp.mean(h * h, axis=-1, keepdims=True) + EPS) * g2_ref[...]
    n2_ref[0] = _pack_bf16_pairs(n2)
    tm = n2.shape[0]
    n2_hi = n2.astype(BF16)
    n2_lo = (n2 - n2_hi.astype(F32)).astype(BF16)
    parts = jnp.dot(jnp.concatenate([n2_hi, n2_lo], axis=0), wr_ref[...],
                    preferred_element_type=F32)
    logits = ((parts[:tm, :ROUTER_COLS] + parts[tm:, ROUTER_COLS:])
              + (parts[:tm, ROUTER_COLS:] + parts[tm:, :ROUTER_COLS])) + br_ref[...]
    lane = lax.broadcasted_iota(jnp.int32, (tm, ROUTER_COLS), 1)
    big = jnp.int32(ROUTER_COLS)
    gl = jnp.where(lane < N_GROUPS, logits, -jnp.inf)
    gm = jnp.max(gl, axis=-1, keepdims=True)
    p_top = 1.0 / jnp.sum(jnp.exp(gl - gm), axis=-1, keepdims=True)
    g_idx = jnp.min(jnp.where(gl == gm, lane, big), axis=-1, keepdims=True)
    e_lo = N_GROUPS + EXPERTS_PER_GROUP * g_idx
    el = jnp.where((lane >= e_lo) & (lane < e_lo + EXPERTS_PER_GROUP), logits, -jnp.inf)
    v1 = jnp.max(el, axis=-1, keepdims=True)
    i1 = jnp.min(jnp.where(el == v1, lane, big), axis=-1, keepdims=True)
    el2 = jnp.where(lane == i1, -jnp.inf, el)
    v2 = jnp.max(el2, axis=-1, keepdims=True)
    i2 = jnp.min(jnp.where(el2 == v2, lane, big), axis=-1, keepdims=True)
    e21 = jnp.exp(v2 - v1)
    gate1 = p_top / (1.0 + e21)
    gate2 = p_top * e21 / (1.0 + e21)
    rt = jnp.where(lane == 0, (i1 - N_GROUPS).astype(F32),
         jnp.where(lane == 1, (i2 - N_GROUPS).astype(F32),
         jnp.where(lane == 2, gate1, jnp.where(lane == 3, gate2, 0.0))))
    rt_ref[0] = rt
    chosen = ((lane == i1 - N_GROUPS) | (lane == i2 - N_GROUPS)).astype(F32)
    cnt_ref[0, 0] = jnp.broadcast_to(jnp.sum(chosen, axis=0, keepdims=True), cnt_ref.shape[2:])


def _mix_call(x, o_diff, o_swa, w_out, g2, w_router, b_router, *, tm):
    B, S, D = x.shape
    const = lambda b, i: (0, 0)
    row = lambda b, i: (b, i, 0)
    nt = S // tm
    return pl.pallas_call(
        _mix_kernel,
        grid=(B, nt),
        in_specs=[
            pl.BlockSpec((1, tm, D), row),
            pl.BlockSpec((1, tm, DIFF_V_COLS), row),
            pl.BlockSpec((1, tm, SWA_Q_COLS), row),
            pl.BlockSpec(w_out.shape, const),
            pl.BlockSpec((1, D), const),
            pl.BlockSpec(w_router.shape, const),
            pl.BlockSpec((1, ROUTER_COLS), const),
        ],
        out_specs=(pl.BlockSpec((1, tm, D), row), pl.BlockSpec((1, tm, D // 2), row),
                   pl.BlockSpec((1, tm, ROUTER_COLS), row),
                   pl.BlockSpec((1, 1, SUBLANES, ROUTER_COLS), lambda b, i: (b, i, 0, 0))),
        out_shape=(jax.ShapeDtypeStruct((B, S, D), F32), jax.ShapeDtypeStruct((B, S, D // 2), jnp.uint32),
                   jax.ShapeDtypeStruct((B, S, ROUTER_COLS), F32),
                   jax.ShapeDtypeStruct((B, nt, SUBLANES, ROUTER_COLS), F32)),
        compiler_params=pltpu.CompilerParams(
            dimension_semantics=("parallel", "parallel"), vmem_limit_bytes=48 * 1024 * 1024),
        name="outproj_router",
    )(x, o_diff, o_swa, w_out, g2, w_router, b_router)


def _slot_kernel(rt_ref, base_ref, dest_ref, *, tm):
    rt_t = rt_ref[...].T
    e1 = rt_t[0:1].astype(jnp.int32)
    e2 = rt_t[1:2].astype(jnp.int32)
    eid = lax.broadcasted_iota(jnp.int32, (N_EXPERTS, tm), 0)
    oh1 = eid == e1
    oh2 = eid == e2
    earlier = (lax.broadcasted_iota(jnp.int32, (tm, tm), 0)
               < lax.broadcasted_iota(jnp.int32, (tm, tm), 1)).astype(BF16)
    before = jnp.dot((oh1 | oh2).astype(BF16), earlier, preferred_element_type=F32)
    slot = before + base_ref[0][:, 0:1]
    d1 = jnp.sum(jnp.where(oh1, slot, 0.0), axis=0, keepdims=True).astype(jnp.int32)
    d2 = jnp.sum(jnp.where(oh2, slot, 0.0), axis=0, keepdims=True).astype(jnp.int32)
    dest_ref[0] = jnp.concatenate([d1, d2, jnp.zeros((SUBLANES - TOP_K, tm), jnp.int32)], axis=0)


def _slot_call(rt, tile_base, *, tm):
    nt = rt.shape[0] // tm
    return pl.pallas_call(
        functools.partial(_slot_kernel, tm=tm),
        grid=(nt,),
        in_specs=[
            pl.BlockSpec((tm, ROUTER_COLS), lambda t: (t, 0)),
            pl.BlockSpec((1, N_EXPERTS, LANES), lambda t: (t, 0, 0)),
        ],
        out_specs=pl.BlockSpec((1, SUBLANES, tm), lambda t: (t, 0, 0)),
        out_shape=jax.ShapeDtypeStruct((nt, SUBLANES, tm), jnp.int32),
        compiler_params=pltpu.CompilerParams(dimension_semantics=("parallel",)),
        name="moe_slots",
    )(rt, tile_base)


SC_ROW_CHUNK = 64
COMBINE_PARTS = 4


def _sc_workers():
    info = plsc.get_sparse_core_info()
    return info.num_cores, info.num_cores * info.num_subcores


def _sc_scatter_rows(rows, idx, n_out):
    n, width = rows.shape
    n_cores, n_workers = _sc_workers()
    n_chunks = n // SC_ROW_CHUNK
    per_worker = n_chunks // n_workers
    assert n_chunks % n_workers == 0
    mesh = plsc.VectorSubcoreMesh(core_axis_name="c", subcore_axis_name="s")

    @functools.partial(
        pl.kernel, mesh=mesh,
        out_type=jax.ShapeDtypeStruct((n_out, width), rows.dtype),
        scratch_types=[
            pltpu.VMEM((SC_ROW_CHUNK,), jnp.int32),
            pltpu.VMEM((SC_ROW_CHUNK, width), rows.dtype),
        ],
    )
    def scatter(rows_hbm, idx_hbm, out_hbm, idx_v, rows_v):
        worker = lax.axis_index("s") * n_cores + lax.axis_index("c")

        @pl.loop(0, per_worker)
        def _(i):
            c = worker * per_worker + i
            pltpu.sync_copy(rows_hbm.at[pl.ds(pl.multiple_of(c * SC_ROW_CHUNK, SC_ROW_CHUNK), SC_ROW_CHUNK)], rows_v)
            for k in range(TOP_K):
                pltpu.sync_copy(idx_hbm.at[k, c], idx_v)
                pltpu.sync_copy(rows_v, out_hbm.at[idx_v])

    return scatter(rows, idx)


def _expert_kernel(be_ref, nvalid_ref, xs_ref, wg_ref, wu_ref, wd_ref, y_ref, wg_b, wu_b, wd_b):
    b = pl.program_id(0)
    n_valid = nvalid_ref[b]

    @pl.when(n_valid > 0)
    def _():
        @pl.when((b == 0) | (be_ref[b] != be_ref[jnp.maximum(b - 1, 0)]))
        def _():
            wg_b[...] = wg_ref[0].astype(BF16)
            wu_b[...] = wu_ref[0].astype(BF16)
            wd_b[...] = wd_ref[0].astype(BF16)

        for c in range(EXPERT_BLOCK // EXPERT_CHUNK):
            rows = pl.ds(c * EXPERT_CHUNK, EXPERT_CHUNK)
            row_id = c * EXPERT_CHUNK + lax.broadcasted_iota(jnp.int32, (EXPERT_CHUNK, xs_ref.shape[1]), 0)
            packed = jnp.where(row_id < n_valid, xs_ref[rows, :], jnp.uint32(0))
            xb = _unpack_bf16_pairs(packed)
            gate = jnp.dot(xb, wg_b[...], preferred_element_type=F32)
            up = jnp.dot(xb, wu_b[...], preferred_element_type=F32)
            hid = (gate * jax.nn.sigmoid(gate) * up).astype(BF16)
            y_ref[rows, :] = jnp.dot(hid, wd_b[...], preferred_element_type=F32)

    @pl.when(n_valid == 0)
    def _():
        y_ref[...] = jnp.zeros_like(y_ref)


def _expert_call(block_expert, n_valid, xs, w_gate, w_up, w_down):
    P = xs.shape[0]
    NB = P // EXPERT_BLOCK
    E, D, F = w_gate.shape
    grid_spec = pltpu.PrefetchScalarGridSpec(
        num_scalar_prefetch=2,
        grid=(NB,),
        in_specs=[
            pl.BlockSpec((EXPERT_BLOCK,) + xs.shape[1:], lambda b, be, nu: (b, 0)),
            pl.BlockSpec((1, D, F), lambda b, be, nu: (be[b], 0, 0)),
            pl.BlockSpec((1, D, F), lambda b, be, nu: (be[b], 0, 0)),
            pl.BlockSpec((1, F, D), lambda b, be, nu: (be[b], 0, 0)),
        ],
        out_specs=pl.BlockSpec((EXPERT_BLOCK, D), lambda b, be, nu: (b, 0)),
        scratch_shapes=[
            pltpu.VMEM((D, F), BF16),
            pltpu.VMEM((D, F), BF16),
            pltpu.VMEM((F, D), BF16),
        ],
    )
    return pl.pallas_call(
        _expert_kernel,
        grid_spec=grid_spec,
        out_shape=jax.ShapeDtypeStruct((P, D), F32),
        compiler_params=pltpu.CompilerParams(
            dimension_semantics=("arbitrary",), vmem_limit_bytes=48 * 1024 * 1024),
        name="moe_experts",
    )(block_expert, n_valid, xs, w_gate, w_up, w_down)


def _sc_gather_rows(table, idx):
    n_rows, width = idx.shape[0], table.shape[1]
    n_cores, n_workers = _sc_workers()
    per_worker = n_rows // n_workers
    assert n_rows % (n_workers * SC_ROW_CHUNK) == 0
    mesh = plsc.VectorSubcoreMesh(core_axis_name="c", subcore_axis_name="s")

    @functools.partial(
        pl.kernel, mesh=mesh,
        out_type=jax.ShapeDtypeStruct((n_rows, width), table.dtype),
        scratch_types=[
            pltpu.VMEM((SC_ROW_CHUNK,), jnp.int32),
            pltpu.VMEM((SC_ROW_CHUNK, width), table.dtype),
        ],
    )
    def gather(table_hbm, idx_hbm, out_hbm, idx_v, rows_v):
        worker = lax.axis_index("s") * n_cores + lax.axis_index("c")
        base = worker * per_worker

        @pl.loop(0, per_worker // SC_ROW_CHUNK)
        def _(c):
            off = pl.multiple_of(base + c * SC_ROW_CHUNK, SC_ROW_CHUNK)
            pltpu.sync_copy(idx_hbm.at[pl.ds(off, SC_ROW_CHUNK)], idx_v)
            pltpu.sync_copy(table_hbm.at[idx_v], rows_v)
            pltpu.sync_copy(rows_v, out_hbm.at[pl.ds(off, SC_ROW_CHUNK)])

    return gather(table, idx)


def _combine_kernel(x1_ref, rt_ref, y_ref, fg_ref, *rest, final_norm):
    o_ref = rest[-1]
    rt = rt_ref[...]
    h = x1_ref[...] + rt[:, 2:3] * y_ref[0, 0] + rt[:, 3:4] * y_ref[0, 1]
    if final_norm:
        h = h * lax.rsqrt(jnp.mean(h * h, axis=-1, keepdims=True) + EPS) * fg_ref[...]
    o_ref[...] = h


def _combine_call(x1, rt, ysg, final_g, out_prev, first_tile, *, tm, final_norm):
    T, D = x1.shape
    tile = lambda t: (t + first_tile, 0)
    in_specs = [
        pl.BlockSpec((tm, D), tile),
        pl.BlockSpec((tm, ROUTER_COLS), tile),
        pl.BlockSpec((1, TOP_K, tm, D), lambda t: (t, 0, 0, 0)),
        pl.BlockSpec((1, D), lambda t: (0, 0)),
    ]
    args = [x1, rt, ysg, final_g]
    aliases = {}
    if out_prev is not None:
        in_specs.append(pl.BlockSpec(memory_space=pl.ANY))
        args.append(out_prev)
        aliases = {4: 0}
    return pl.pallas_call(
        functools.partial(_combine_kernel, final_norm=final_norm),
        grid=(ysg.shape[0],),
        in_specs=in_specs,
        out_specs=pl.BlockSpec((tm, D), tile),
        out_shape=jax.ShapeDtypeStruct((T, D), F32),
        input_output_aliases=aliases,
        compiler_params=pltpu.CompilerParams(
            dimension_semantics=("parallel",), vmem_limit_bytes=40 * 1024 * 1024),
        name="moe_combine",
    )(*args)


def _slot_layout(tile_counts, n_assign):
    NB = -(-n_assign // EXPERT_BLOCK) + N_EXPERTS
    counts = jnp.sum(tile_counts, axis=0)
    padded = ((counts + EXPERT_BLOCK - 1) // EXPERT_BLOCK) * EXPERT_BLOCK
    pad_end = jnp.cumsum(padded)
    pad_start = pad_end - padded
    tile_base = pad_start[None, :] + jnp.cumsum(tile_counts, axis=0) - tile_counts
    block_start = jnp.arange(NB, dtype=jnp.int32) * EXPERT_BLOCK
    block_expert = jnp.minimum(jnp.sum(pad_end[None, :] <= block_start[:, None], axis=1),
                               N_EXPERTS - 1).astype(jnp.int32)
    run_end = (pad_start + counts)[block_expert]
    n_valid = jnp.clip(run_end - block_start, 0, EXPERT_BLOCK).astype(jnp.int32)
    return NB, block_expert, n_valid, tile_base


def _rope_tables(S):
    inv = 1.0 / (ROPE_THETA ** (jnp.arange(0, HEAD_DIM, 2, dtype=F32) / HEAD_DIM))
    ang = jnp.arange(S, dtype=F32)[:, None] * inv[None, :]
    cos, sin = jnp.cos(ang), jnp.sin(ang)
    cos_l = jnp.tile(cos, (1, LANES // (HEAD_DIM // 2)))
    sin_l = jnp.tile(jnp.concatenate([-sin, sin], axis=1), (1, LANES // HEAD_DIM))
    return cos_l, sin_l, cos.T, sin.T


def kernel(x, norm1_g, w_in, lambda_q1, lambda_k1, lambda_q2, lambda_k2, subln_g, sinks, w_out,
           norm2_g, w_router_group, b_router_group, w_router_expert, b_router_expert,
           w_gate, w_up, w_down, final_g):
    B, S, D = x.shape
    T = B * S
    depth = w_in.shape[0]
    tq, tk = 512, 512
    tm_proj = 512
    tm_tok = 512
    tq_swa = 512
    qscale = HEAD_DIM ** -0.5 * math.log2(math.e)
    cos_l, sin_l, cos_t, sin_t = _rope_tables(S)

    c0 = DIFF_QK_COLS
    c1 = 2 * DIFF_QK_COLS
    c2 = c1 + DIFF_V_COLS
    c3 = c2 + SWA_Q_COLS
    c4 = c3 + SWA_KV_COLS
    for l in range(depth):
        lambda_init = 0.8 - 0.6 * math.exp(-0.3 * l)
        w = w_in[l]
        w_nat = jnp.concatenate([w[:, c0:c1], w[:, c3:c4]], axis=1).astype(BF16)
        w_tr = jnp.concatenate([w[:, :c0] * qscale, w[:, c1:c2], w[:, c2:c3] * qscale, w[:, c4:]],
                               axis=1).T.astype(BF16)
        dqt, dk, dvt, sqt, sk, svt = _proj_call(
            x, norm1_g[l][None, :], w_nat, w_tr, cos_l, sin_l, cos_t, sin_t, tm=tm_proj, tk=tk)

        lam_p = jnp.stack([lambda_q1[l], lambda_k1[l], lambda_q2[l], lambda_k2[l]]).astype(F32)
        o_diff = _diff_call(lam_p, dqt, dk, dvt, subln_g[l][None, :].astype(F32),
                            tq=tq, tk=tk, lambda_init=lambda_init)
        sink_row = jnp.repeat(sinks[l].astype(F32) * math.log2(math.e), WINDOW)[None, :]
        o_swa = _swa_call(sink_row, sqt, sk, svt, tq=tq_swa)

        wo_b = w_out[l].astype(BF16)
        w_router = jnp.zeros((D, ROUTER_COLS), F32)
        w_router = w_router.at[:, :N_GROUPS].set(w_router_group[l])
        w_router = w_router.at[:, N_GROUPS:N_GROUPS + N_EXPERTS].set(w_router_expert[l])
        w_router_hi = w_router.astype(BF16)
        w_router_lo = (w_router - w_router_hi.astype(F32)).astype(BF16)
        w_router = jnp.concatenate([w_router_hi, w_router_lo], axis=1)
        b_router = jnp.zeros((1, ROUTER_COLS), F32)
        b_router = b_router.at[0, :N_GROUPS].set(b_router_group[l])
        b_router = b_router.at[0, N_GROUPS:N_GROUPS + N_EXPERTS].set(b_router_expert[l])
        x1, n2p, rt, cnt = _mix_call(x, o_diff, o_swa, wo_b, norm2_g[l][None, :], w_router, b_router, tm=tm_tok)

        rt2 = rt.reshape(T, ROUTER_COLS)
        tile_counts = cnt[:, :, 0, :N_EXPERTS].reshape(T // tm_tok, N_EXPERTS).astype(jnp.int32)
        NB, block_expert, n_valid, tile_base = _slot_layout(tile_counts, T * TOP_K)
        tile_base = jnp.broadcast_to(tile_base.astype(F32)[:, :, None], (T // tm_tok, N_EXPERTS, LANES))
        dest = _slot_call(rt2, tile_base, tm=tm_tok)
        scatter_idx = jnp.swapaxes(dest[:, :TOP_K, :], 0, 1).reshape(TOP_K, T // SC_ROW_CHUNK, SC_ROW_CHUNK)
        xs = _sc_scatter_rows(n2p.reshape(T, D // 2), scatter_idx, NB * EXPERT_BLOCK)
        ys = _expert_call(block_expert, n_valid, xs, w_gate[l], w_up[l], w_down[l])
        tiles_per_part = (T // tm_tok) // COMBINE_PARTS
        out = None
        for part in range(COMBINE_PARTS):
            t0 = part * tiles_per_part
            idx = dest[t0:t0 + tiles_per_part, :TOP_K, :].reshape(-1)
            ysg = _sc_gather_rows(ys, idx).reshape(tiles_per_part, TOP_K, tm_tok, D)
            out = _combine_call(x1.reshape(T, D), rt2, ysg, final_g[None, :], out, t0,
                                tm=tm_tok, final_norm=(l == depth - 1))
        x = out.reshape(B, S, D)
    return x
```

```python
import functools
import math

import jax
import jax.numpy as jnp
from jax import lax
from jax.experimental import pallas as pl
from jax.experimental.pallas import tpu as pltpu
from jax.experimental.pallas import tpu_sc as plsc

HEAD_DIM = 64
DIFF_HEADS = 4
DIFF_V_DIM = 2 * HEAD_DIM
SWA_Q_HEADS = 8
SWA_KV_HEADS = 2
SWA_GROUP = SWA_Q_HEADS // SWA_KV_HEADS
WINDOW = 128
ROPE_THETA = 10000.0
N_GROUPS = 4
EXPERTS_PER_GROUP = 8
N_EXPERTS = N_GROUPS * EXPERTS_PER_GROUP
TOP_K = 2
EXPERT_BLOCK = 512
EXPERT_CHUNK = 256
EPS = 1e-6
NEG = -1e30

DIFF_QK_COLS = DIFF_HEADS * 2 * HEAD_DIM
DIFF_V_COLS = DIFF_HEADS * DIFF_V_DIM
SWA_Q_COLS = SWA_Q_HEADS * HEAD_DIM
SWA_KV_COLS = SWA_KV_HEADS * HEAD_DIM
LANES = 128
SUBLANES = 8
BF16_SUBLANES = 16
VT_ROWS = DIFF_V_DIM + BF16_SUBLANES
SWA_VT_ROWS = SWA_KV_COLS + BF16_SUBLANES
ROUTER_COLS = LANES
DIFF_UNROLL = 4
DIFF_S_BUFS = 4

BF16 = jnp.bfloat16
F32 = jnp.float32


def _rope_lanes(x, cos_l, sin_l, first_half):
    rot = jnp.where(first_half, pltpu.roll(x, 96, 1), pltpu.roll(x, 32, 1))
    return x * cos_l + rot * sin_l


def _proj_kernel(x_ref, g_ref, wnat_ref, wtr_ref, cosl_ref, sinl_ref, cost_ref, sint_ref,
                 dqt_ref, dk_ref, dvt_ref, sqt_ref, sk_ref, svt_ref, *, tk):
    x = x_ref[0]
    tm = x.shape[0]
    n1 = x * lax.rsqrt(jnp.mean(x * x, axis=-1, keepdims=True) + EPS) * g_ref[...]
    n1b = n1.astype(BF16)
    nat = jnp.dot(n1b, wnat_ref[...], preferred_element_type=F32)
    tr = lax.dot_general(wtr_ref[...], n1b, (((1,), (1,)), ((), ())),
                         preferred_element_type=F32)

    cos_l, sin_l = cosl_ref[...], sinl_ref[...]
    first_half = (lax.broadcasted_iota(jnp.int32, (tm, LANES), 1) & (HEAD_DIM - 1)) < HEAD_DIM // 2
    for h in range(DIFF_HEADS):
        slab = nat[:, h * LANES:(h + 1) * LANES]
        dk_ref[0, h] = _rope_lanes(slab, cos_l, sin_l, first_half).astype(BF16)
    sk = _rope_lanes(nat[:, DIFF_QK_COLS:DIFF_QK_COLS + LANES], cos_l, sin_l, first_half).astype(BF16)
    for c in range(tm // WINDOW):
        sk_ref[0, c] = sk[c * WINDOW:(c + 1) * WINDOW]

    cos_t, sin_t = cost_ref[...], sint_ref[...]
    half = HEAD_DIM // 2

    def rope_rows(r0):
        x1 = tr[r0:r0 + half]
        x2 = tr[r0 + half:r0 + HEAD_DIM]
        return (x1 * cos_t - x2 * sin_t).astype(BF16), (x1 * sin_t + x2 * cos_t).astype(BF16)

    for h in range(DIFF_HEADS):
        for c in range(2):
            lo, hi = rope_rows(h * 2 * HEAD_DIM + c * HEAD_DIM)
            dqt_ref[0, h, c * HEAD_DIM:c * HEAD_DIM + half] = lo
            dqt_ref[0, h, c * HEAD_DIM + half:(c + 1) * HEAD_DIM] = hi
    ones_rows = (lax.broadcasted_iota(jnp.int32, (BF16_SUBLANES, tk), 0) == 0).astype(BF16)
    for h in range(DIFF_HEADS):
        r0 = DIFF_QK_COLS + h * DIFF_V_DIM
        for c in range(tm // tk):
            dvt_ref[0, h, c, :DIFF_V_DIM] = tr[r0:r0 + DIFF_V_DIM, c * tk:(c + 1) * tk].astype(BF16)
            dvt_ref[0, h, c, DIFF_V_DIM:] = ones_rows

    r0 = DIFF_QK_COLS + DIFF_V_COLS
    for h in range(SWA_Q_HEADS):
        lo, hi = rope_rows(r0 + h * HEAD_DIM)
        sqt_ref[0, h * HEAD_DIM:h * HEAD_DIM + half] = lo
        sqt_ref[0, h * HEAD_DIM + half:(h + 1) * HEAD_DIM] = hi
    r0 += SWA_Q_COLS
    for c in range(tm // WINDOW):
        svt_ref[0, c, :SWA_KV_COLS] = tr[r0:r0 + SWA_KV_COLS, c * WINDOW:(c + 1) * WINDOW].astype(BF16)
        svt_ref[0, c, SWA_KV_COLS:] = ones_rows[:, :WINDOW]


def _proj_call(x, g1, w_nat, w_tr, cos_l, sin_l, cos_t, sin_t, *, tm, tk):
    B, S, D = x.shape
    nkv = S // tk
    grid = (B, S // tm)
    const = lambda b, i: (0, 0)
    out_shape = (
        jax.ShapeDtypeStruct((B, DIFF_HEADS, 2 * HEAD_DIM, S), BF16),
        jax.ShapeDtypeStruct((B, DIFF_HEADS, S, 2 * HEAD_DIM), BF16),
        jax.ShapeDtypeStruct((B, DIFF_HEADS, nkv, VT_ROWS, tk), BF16),
        jax.ShapeDtypeStruct((B, SWA_Q_COLS, S), BF16),
        jax.ShapeDtypeStruct((B, S // WINDOW, WINDOW, SWA_KV_COLS), BF16),
        jax.ShapeDtypeStruct((B, S // WINDOW, SWA_VT_ROWS, WINDOW), BF16),
    )
    return pl.pallas_call(
        functools.partial(_proj_kernel, tk=tk),
        grid=grid,
        in_specs=[
            pl.BlockSpec((1, tm, D), lambda b, i: (b, i, 0)),
            pl.BlockSpec((1, D), const),
            pl.BlockSpec(w_nat.shape, const),
            pl.BlockSpec(w_tr.shape, const),
            pl.BlockSpec((tm, LANES), lambda b, i: (i, 0)),
            pl.BlockSpec((tm, LANES), lambda b, i: (i, 0)),
            pl.BlockSpec((HEAD_DIM // 2, tm), lambda b, i: (0, i)),
            pl.BlockSpec((HEAD_DIM // 2, tm), lambda b, i: (0, i)),
        ],
        out_specs=(
            pl.BlockSpec((1, DIFF_HEADS, 2 * HEAD_DIM, tm), lambda b, i: (b, 0, 0, i)),
            pl.BlockSpec((1, DIFF_HEADS, tm, 2 * HEAD_DIM), lambda b, i: (b, 0, i, 0)),
            pl.BlockSpec((1, DIFF_HEADS, tm // tk, VT_ROWS, tk), lambda b, i: (b, 0, i, 0, 0)),
            pl.BlockSpec((1, SWA_Q_COLS, tm), lambda b, i: (b, 0, i)),
            pl.BlockSpec((1, tm // WINDOW, WINDOW, SWA_KV_COLS), lambda b, i: (b, i, 0, 0)),
            pl.BlockSpec((1, tm // WINDOW, SWA_VT_ROWS, WINDOW), lambda b, i: (b, i, 0, 0)),
        ),
        out_shape=out_shape,
        compiler_params=pltpu.CompilerParams(
            dimension_semantics=("parallel", "parallel"), vmem_limit_bytes=48 * 1024 * 1024),
        name="proj_rope",
    )(x, g1, w_nat, w_tr, cos_l, sin_l, cos_t, sin_t)


def _diff_kernel(lam_ref, qt_ref, k_ref, vt_ref, g_ref, o_ref, *scratch, tq, tk, lambda_init):
    i = pl.program_id(2)
    s_bufs = scratch[:DIFF_S_BUFS]
    top_bufs = scratch[DIFF_S_BUFS:2 * DIFF_S_BUFS]
    m_ref, acc_ref = scratch[2 * DIFF_S_BUFS:]
    qt = qt_ref[0, 0]
    z = jnp.zeros((HEAD_DIM, tq), BF16)
    qw = jnp.concatenate([jnp.concatenate([qt[:HEAD_DIM], z], axis=1),
                          jnp.concatenate([z, qt[HEAD_DIM:]], axis=1)], axis=0)

    def scores(j, par):
        kt = k_ref[0, 0, pl.ds(pl.multiple_of(j * tk, tk), tk), :]
        s = jnp.dot(kt, qw, preferred_element_type=F32)
        s_bufs[par][...] = s
        top_bufs[par][...] = jnp.max(s, axis=0, keepdims=True)

    def absorb(j, par, masked):
        s = s_bufs[par][...]
        if masked:
            kpos = j * tk + lax.broadcasted_iota(jnp.int32, (tk, 2 * tq), 0)
            qpos = i * tq + (lax.broadcasted_iota(jnp.int32, (tk, 2 * tq), 1) & (tq - 1))
            s = jnp.where(kpos <= qpos, s, NEG)
            top = jnp.max(s, axis=0, keepdims=True)
        else:
            top = top_bufs[par][...]
        m = m_ref[...]
        m_new = jnp.maximum(m, top)
        alpha = jnp.exp2(m - m_new)
        p = jnp.exp2(s - m_new).astype(BF16)
        m_ref[...] = m_new
        pv = jnp.dot(vt_ref[0, 0, j], p, preferred_element_type=F32)
        acc_ref[...] = alpha * acc_ref[...] + pv

    m_ref[...] = jnp.full(m_ref.shape, NEG, F32)
    acc_ref[...] = jnp.zeros(acc_ref.shape, F32)

    nfull = (i * tq) // tk
    scores(nfull, 0)
    scores(0, 1)
    absorb(nfull, 0, True)

    def group(t, c):
        j = DIFF_UNROLL * t
        for idx in range(DIFF_UNROLL):
            scores(j + idx + 1, (idx + 2) % DIFF_S_BUFS)
            absorb(j + idx, (idx + 1) % DIFF_S_BUFS, False)
        return c

    lax.fori_loop(0, nfull // DIFF_UNROLL, group, 0)

    for rem in range(1, DIFF_UNROLL):
        @pl.when(nfull % DIFF_UNROLL == rem)
        def _():
            first = nfull - rem
            for idx in range(rem):
                if idx + 1 < rem:
                    scores(first + idx + 1, (idx + 2) % DIFF_S_BUFS)
                absorb(first + idx, (idx + 1) % DIFF_S_BUFS, False)

    lam_p = lam_ref[...]
    lam = (jnp.exp(jnp.sum(lam_p[0:1] * lam_p[1:2], axis=-1, keepdims=True))
           - jnp.exp(jnp.sum(lam_p[2:3] * lam_p[3:4], axis=-1, keepdims=True)) + lambda_init)
    l = acc_ref[DIFF_V_DIM:DIFF_V_DIM + 1, :]
    o = (acc_ref[:DIFF_V_DIM, :tq] / l[:, :tq]
         - lam * (acc_ref[:DIFF_V_DIM, tq:] / l[:, tq:]))
    o = o * lax.rsqrt(jnp.mean(o * o, axis=0, keepdims=True) + EPS)
    o_ref[0] = (o.T * g_ref[...] * (1.0 - lambda_init)).astype(BF16)


def _diff_call(lam_p, dqt, dk, dvt, subln_g, *, tq, tk, lambda_init):
    B, H, _, S = dqt.shape
    assert tk % tq == 0 and S % tk == 0, "one key tile must cover a query tile's diagonal"
    nkv = S // tk
    grid = (B, H, S // tq)
    return pl.pallas_call(
        functools.partial(_diff_kernel, tq=tq, tk=tk, lambda_init=lambda_init),
        grid=grid,
        in_specs=[
            pl.BlockSpec(lam_p.shape, lambda b, h, i: (0, 0)),
            pl.BlockSpec((1, 1, 2 * HEAD_DIM, tq), lambda b, h, i: (b, h, 0, i)),
            pl.BlockSpec((1, 1, S, 2 * HEAD_DIM), lambda b, h, i: (b, h, 0, 0)),
            pl.BlockSpec((1, 1, nkv, VT_ROWS, tk), lambda b, h, i: (b, h, 0, 0, 0)),
            pl.BlockSpec((1, DIFF_V_DIM), lambda b, h, i: (0, 0)),
        ],
        out_specs=pl.BlockSpec((1, tq, DIFF_V_DIM), lambda b, h, i: (b, i, h)),
        out_shape=jax.ShapeDtypeStruct((B, S, DIFF_V_COLS), BF16),
        scratch_shapes=[pltpu.VMEM((tk, 2 * tq), F32)] * DIFF_S_BUFS + [
            pltpu.VMEM((1, 2 * tq), F32)] * DIFF_S_BUFS + [
            pltpu.VMEM((1, 2 * tq), F32),
            pltpu.VMEM((VT_ROWS, 2 * tq), F32),
        ],
        compiler_params=pltpu.CompilerParams(
            dimension_semantics=("parallel", "parallel", "arbitrary"),
            vmem_limit_bytes=48 * 1024 * 1024),
        name="diff_attn",
    )(lam_p, dqt, dk, dvt, subln_g)


def _swa_kernel(sink_ref, qt_ref, k_ref, vt_ref, o_ref, *, tq):
    i = pl.program_id(1)
    n_cols = SWA_Q_HEADS * WINDOW
    half_cols = n_cols // SWA_KV_HEADS
    sink = sink_ref[...]
    row = lax.broadcasted_iota(jnp.int32, (2 * WINDOW, WINDOW), 0)
    qrel = lax.broadcasted_iota(jnp.int32, (2 * WINDOW, WINDOW), 1)
    band = (row - WINDOW <= qrel) & (row > qrel)
    in_current = row >= WINDOW
    z = jnp.zeros((HEAD_DIM, half_cols), BF16)
    for sub in range(tq // WINDOW):
        n = i * (tq // WINDOW) + sub
        prev = jnp.maximum(n - 1, 0)
        kwin = jnp.concatenate([k_ref[0, prev], k_ref[0, n]], axis=0)
        vtwin = jnp.concatenate([vt_ref[0, prev], vt_ref[0, n]], axis=1)
        qt = qt_ref[0, :, sub * WINDOW:(sub + 1) * WINDOW]
        heads = [qt[h * HEAD_DIM:(h + 1) * HEAD_DIM] for h in range(SWA_Q_HEADS)]
        qw = jnp.concatenate(
            [jnp.concatenate(heads[:SWA_GROUP] + [z], axis=1),
             jnp.concatenate([z] + heads[SWA_GROUP:], axis=1)], axis=0)
        s = jnp.dot(kwin, qw, preferred_element_type=F32)
        valid = band & (in_current | (n >= 1))
        s = jnp.concatenate(
            [jnp.where(valid, s[:, h * WINDOW:(h + 1) * WINDOW], NEG) for h in range(SWA_Q_HEADS)], axis=1)
        m = jnp.maximum(jnp.max(s, axis=0, keepdims=True), sink)
        p = jnp.exp2(s - m).astype(BF16)
        acc = jnp.dot(vtwin, p, preferred_element_type=F32)
        den = acc[SWA_KV_COLS:SWA_KV_COLS + 1] + jnp.exp2(sink - m)
        on = acc[:SWA_KV_COLS] / den
        u = jnp.concatenate([on[:HEAD_DIM, :half_cols], on[HEAD_DIM:, half_cols:]], axis=1)
        for hp in range(SWA_Q_HEADS // 2):
            two = jnp.concatenate([u[:, (2 * hp) * WINDOW:(2 * hp + 1) * WINDOW],
                                   u[:, (2 * hp + 1) * WINDOW:(2 * hp + 2) * WINDOW]], axis=0)
            o_ref[0, sub * WINDOW:(sub + 1) * WINDOW, hp * LANES:(hp + 1) * LANES] = two.T.astype(BF16)


def _swa_call(sink_row, sqt, sk, svt, *, tq):
    B, _, S = sqt.shape
    nb = S // WINDOW
    return pl.pallas_call(
        functools.partial(_swa_kernel, tq=tq),
        grid=(B, S // tq),
        in_specs=[
            pl.BlockSpec(sink_row.shape, lambda b, i: (0, 0)),
            pl.BlockSpec((1, SWA_Q_COLS, tq), lambda b, i: (b, 0, i)),
            pl.BlockSpec((1, nb, WINDOW, SWA_KV_COLS), lambda b, i: (b, 0, 0, 0)),
            pl.BlockSpec((1, nb, SWA_VT_ROWS, WINDOW), lambda b, i: (b, 0, 0, 0)),
        ],
        out_specs=pl.BlockSpec((1, tq, SWA_Q_COLS), lambda b, i: (b, i, 0)),
        out_shape=jax.ShapeDtypeStruct((B, S, SWA_Q_COLS), BF16),
        compiler_params=pltpu.CompilerParams(
            dimension_semantics=("parallel", "arbitrary"), vmem_limit_bytes=40 * 1024 * 1024),
        name="swa_attn",
    )(sink_row, sqt, sk, svt)


def _pack_bf16_pairs(x):
    n = x.shape[1] // 2
    lo = lax.bitcast_convert_type(x[:, :n].astype(BF16).astype(F32), jnp.uint32)
    hi = lax.bitcast_convert_type(x[:, n:].astype(BF16).astype(F32), jnp.uint32)
    return (lo >> 16) | (hi & jnp.uint32(0xFFFF0000))


def _unpack_bf16_pairs(w):
    lo = lax.bitcast_convert_type(w << 16, F32)
    hi = lax.bitcast_convert_type(w & jnp.uint32(0xFFFF0000), F32)
    return jnp.concatenate([lo, hi], axis=1).astype(BF16)


def _mix_kernel(x_ref, od_ref, os_ref, wo_ref, g2_ref, wr_ref, br_ref, x1_ref, n2_ref, rt_ref, cnt_ref):
    h = (x_ref[0]
         + jnp.dot(od_ref[0], wo_ref[:DIFF_V_COLS], preferred_element_type=F32)
         + jnp.dot(os_ref[0], wo_ref[DIFF_V_COLS:], preferred_element_type=F32))
    x1_ref[0] = h
    n2 = h * lax.rsqrt(jnp.mean(h * h, axis=-1, keepdims=True) + EPS) * g2_ref[...]
    n2_ref[0] = _pack_bf16_pairs(n2)
    tm = n2.shape[0]
    n2_hi = n2.astype(BF16)
    n2_lo = (n2 - n2_hi.astype(F32)).astype(BF16)
    parts = jnp.dot(jnp.concatenate([n2_hi, n2_lo], axis=0), wr_ref[...],
                    preferred_element_type=F32)
    logits = ((parts[:tm, :ROUTER_COLS] + parts[tm:, ROUTER_COLS:])
              + (parts[:tm, ROUTER_COLS:] + parts[tm:, :ROUTER_COLS])) + br_ref[...]
    lane = lax.broadcasted_iota(jnp.int32, (tm, ROUTER_COLS), 1)
    big = jnp.int32(ROUTER_COLS)
    gl = jnp.where(lane < N_GROUPS, logits, -jnp.inf)
    gm = jnp.max(gl, axis=-1, keepdims=True)
    p_top = 1.0 / jnp.sum(jnp.exp(gl - gm), axis=-1, keepdims=True)
    g_idx = jnp.min(jnp.where(gl == gm, lane, big), axis=-1, keepdims=True)
    e_lo = N_GROUPS + EXPERTS_PER_GROUP * g_idx
    el = jnp.where((lane >= e_lo) & (lane < e_lo + EXPERTS_PER_GROUP), logits, -jnp.inf)
    v1 = jnp.max(el, axis=-1, keepdims=True)
    i1 = jnp.min(jnp.where(el == v1, lane, big), axis=-1, keepdims=True)
    el2 = jnp.where(lane == i1, -jnp.inf, el)
    v2 = jnp.max(el2, axis=-1, keepdims=True)
    i2 = jnp.min(jnp.where(el2 == v2, lane, big), axis=-1, keepdims=True)
    e21 = jnp.exp(v2 - v1)
    gate1 = p_top / (1.0 + e21)
    gate2 = p_top * e21 / (1.0 + e21)
    rt = jnp.where(lane == 0, (i1 - N_GROUPS).astype(F32),
         jnp.where(lane == 1, (i2 - N_GROUPS).astype(F32),
         jnp.where(lane == 2, gate1, jnp.where(lane == 3, gate2, 0.0))))
    rt_ref[0] = rt
    chosen = ((lane == i1 - N_GROUPS) | (lane == i2 - N_GROUPS)).astype(F32)
    cnt_ref[0, 0] = jnp.broadcast_to(jnp.sum(chosen, axis=0, keepdims=True), cnt_ref.shape[2:])


def _mix_call(x, o_diff, o_swa, w_out, g2, w_router, b_router, *, tm):
    B, S, D = x.shape
    const = lambda b, i: (0, 0)
    row = lambda b, i: (b, i, 0)
    nt = S // tm
    return pl.pallas_call(
        _mix_kernel,
        grid=(B, nt),
        in_specs=[
            pl.BlockSpec((1, tm, D), row),
            pl.BlockSpec((1, tm, DIFF_V_COLS), row),
            pl.BlockSpec((1, tm, SWA_Q_COLS), row),
            pl.BlockSpec(w_out.shape, const),
            pl.BlockSpec((1, D), const),
            pl.BlockSpec(w_router.shape, const),
            pl.BlockSpec((1, ROUTER_COLS), const),
        ],
        out_specs=(pl.BlockSpec((1, tm, D), row), pl.BlockSpec((1, tm, D // 2), row),
                   pl.BlockSpec((1, tm, ROUTER_COLS), row),
                   pl.BlockSpec((1, 1, SUBLANES, ROUTER_COLS), lambda b, i: (b, i, 0, 0))),
        out_shape=(jax.ShapeDtypeStruct((B, S, D), F32), jax.ShapeDtypeStruct((B, S, D // 2), jnp.uint32),
                   jax.ShapeDtypeStruct((B, S, ROUTER_COLS), F32),
                   jax.ShapeDtypeStruct((B, nt, SUBLANES, ROUTER_COLS), F32)),
        compiler_params=pltpu.CompilerParams(
            dimension_semantics=("parallel", "parallel"), vmem_limit_bytes=48 * 1024 * 1024),
        name="outproj_router",
    )(x, o_diff, o_swa, w_out, g2, w_router, b_router)


def _slot_kernel(rt_ref, base_ref, dest_ref, *, tm):
    rt_t = rt_ref[...].T
    e1 = rt_t[0:1].astype(jnp.int32)
    e2 = rt_t[1:2].astype(jnp.int32)
    eid = lax.broadcasted_iota(jnp.int32, (N_EXPERTS, tm), 0)
    oh1 = eid == e1
    oh2 = eid == e2
    earlier = (lax.broadcasted_iota(jnp.int32, (tm, tm), 0)
               < lax.broadcasted_iota(jnp.int32, (tm, tm), 1)).astype(BF16)
    before = jnp.dot((oh1 | oh2).astype(BF16), earlier, preferred_element_type=F32)
    slot = before + base_ref[0][:, 0:1]
    d1 = jnp.sum(jnp.where(oh1, slot, 0.0), axis=0, keepdims=True).astype(jnp.int32)
    d2 = jnp.sum(jnp.where(oh2, slot, 0.0), axis=0, keepdims=True).astype(jnp.int32)
    dest_ref[0] = jnp.concatenate([d1, d2, jnp.zeros((SUBLANES - TOP_K, tm), jnp.int32)], axis=0)


def _slot_call(rt, tile_base, *, tm):
    nt = rt.shape[0] // tm
    return pl.pallas_call(
        functools.partial(_slot_kernel, tm=tm),
        grid=(nt,),
        in_specs=[
            pl.BlockSpec((tm, ROUTER_COLS), lambda t: (t, 0)),
            pl.BlockSpec((1, N_EXPERTS, LANES), lambda t: (t, 0, 0)),
        ],
        out_specs=pl.BlockSpec((1, SUBLANES, tm), lambda t: (t, 0, 0)),
        out_shape=jax.ShapeDtypeStruct((nt, SUBLANES, tm), jnp.int32),
        compiler_params=pltpu.CompilerParams(dimension_semantics=("parallel",)),
        name="moe_slots",
    )(rt, tile_base)


SC_ROW_CHUNK = 64


def _sc_workers():
    info = plsc.get_sparse_core_info()
    return info.num_cores, info.num_cores * info.num_subcores


def _sc_scatter_rows(rows, idx, n_out):
    n, width = rows.shape
    n_cores, n_workers = _sc_workers()
    n_chunks = n // SC_ROW_CHUNK
    per_worker = n_chunks // n_workers
    assert n_chunks % n_workers == 0
    mesh = plsc.VectorSubcoreMesh(core_axis_name="c", subcore_axis_name="s")

    @functools.partial(
        pl.kernel, mesh=mesh,
        out_type=jax.ShapeDtypeStruct((n_out, width), rows.dtype),
        scratch_types=[
            pltpu.VMEM((SC_ROW_CHUNK,), jnp.int32),
            pltpu.VMEM((SC_ROW_CHUNK, width), rows.dtype),
        ],
    )
    def scatter(rows_hbm, idx_hbm, out_hbm, idx_v, rows_v):
        worker = lax.axis_index("s") * n_cores + lax.axis_index("c")

        @pl.loop(0, per_worker)
        def _(i):
            c = worker * per_worker + i
            pltpu.sync_copy(rows_hbm.at[pl.ds(pl.multiple_of(c * SC_ROW_CHUNK, SC_ROW_CHUNK), SC_ROW_CHUNK)], rows_v)
            for k in range(TOP_K):
                pltpu.sync_copy(idx_hbm.at[k, c], idx_v)
                pltpu.sync_copy(rows_v, out_hbm.at[idx_v])

    return scatter(rows, idx)


def _expert_kernel(be_ref, nvalid_ref, xs_ref, wg_ref, wu_ref, wd_ref, y_ref, wg_b, wu_b, wd_b):
    b = pl.program_id(0)
    n_valid = nvalid_ref[b]

    @pl.when(n_valid > 0)
    def _():
        @pl.when((b == 0) | (be_ref[b] != be_ref[jnp.maximum(b - 1, 0)]))
        def _():
            wg_b[...] = wg_ref[0].astype(BF16)
            wu_b[...] = wu_ref[0].astype(BF16)
            wd_b[...] = wd_ref[0].astype(BF16)

        for c in range(EXPERT_BLOCK // EXPERT_CHUNK):
            rows = pl.ds(c * EXPERT_CHUNK, EXPERT_CHUNK)
            row_id = c * EXPERT_CHUNK + lax.broadcasted_iota(jnp.int32, (EXPERT_CHUNK, xs_ref.shape[1]), 0)
            packed = jnp.where(row_id < n_valid, xs_ref[rows, :], jnp.uint32(0))
            xb = _unpack_bf16_pairs(packed)
            gate = jnp.dot(xb, wg_b[...], preferred_element_type=F32)
            up = jnp.dot(xb, wu_b[...], preferred_element_type=F32)
            hid = (gate * jax.nn.sigmoid(gate) * up).astype(BF16)
            y_ref[rows, :] = _pack_bf16_pairs(jnp.dot(hid, wd_b[...], preferred_element_type=F32))

    @pl.when(n_valid == 0)
    def _():
        y_ref[...] = jnp.zeros_like(y_ref)


def _expert_call(block_expert, n_valid, xs, w_gate, w_up, w_down):
    P = xs.shape[0]
    NB = P // EXPERT_BLOCK
    E, D, F = w_gate.shape
    grid_spec = pltpu.PrefetchScalarGridSpec(
        num_scalar_prefetch=2,
        grid=(NB,),
        in_specs=[
            pl.BlockSpec((EXPERT_BLOCK,) + xs.shape[1:], lambda b, be, nu: (b, 0)),
            pl.BlockSpec((1, D, F), lambda b, be, nu: (be[b], 0, 0)),
            pl.BlockSpec((1, D, F), lambda b, be, nu: (be[b], 0, 0)),
            pl.BlockSpec((1, F, D), lambda b, be, nu: (be[b], 0, 0)),
        ],
        out_specs=pl.BlockSpec((EXPERT_BLOCK, D // 2), lambda b, be, nu: (b, 0)),
        scratch_shapes=[
            pltpu.VMEM((D, F), BF16),
            pltpu.VMEM((D, F), BF16),
            pltpu.VMEM((F, D), BF16),
        ],
    )
    return pl.pallas_call(
        _expert_kernel,
        grid_spec=grid_spec,
        out_shape=jax.ShapeDtypeStruct((P, D // 2), jnp.uint32),
        compiler_params=pltpu.CompilerParams(
            dimension_semantics=("arbitrary",), vmem_limit_bytes=48 * 1024 * 1024),
        name="moe_experts",
    )(block_expert, n_valid, xs, w_gate, w_up, w_down)


def _sc_gather_rows(table, idx):
    n_rows, width = idx.shape[0], table.shape[1]
    n_cores, n_workers = _sc_workers()
    per_worker = n_rows // n_workers
    assert n_rows % (n_workers * SC_ROW_CHUNK) == 0
    mesh = plsc.VectorSubcoreMesh(core_axis_name="c", subcore_axis_name="s")

    @functools.partial(
        pl.kernel, mesh=mesh,
        out_type=jax.ShapeDtypeStruct((n_rows, width), table.dtype),
        scratch_types=[
            pltpu.VMEM((SC_ROW_CHUNK,), jnp.int32),
            pltpu.VMEM((SC_ROW_CHUNK, width), table.dtype),
        ],
    )
    def gather(table_hbm, idx_hbm, out_hbm, idx_v, rows_v):
        worker = lax.axis_index("s") * n_cores + lax.axis_index("c")
        base = worker * per_worker

        @pl.loop(0, per_worker // SC_ROW_CHUNK)
        def _(c):
            off = pl.multiple_of(base + c * SC_ROW_CHUNK, SC_ROW_CHUNK)
            pltpu.sync_copy(idx_hbm.at[pl.ds(off, SC_ROW_CHUNK)], idx_v)
            pltpu.sync_copy(table_hbm.at[idx_v], rows_v)
            pltpu.sync_copy(rows_v, out_hbm.at[pl.ds(off, SC_ROW_CHUNK)])

    return gather(table, idx)


def _combine_kernel(x1_ref, rt_ref, y_ref, fg_ref, o_ref, *, final_norm):
    rt = rt_ref[...]
    y1 = _unpack_bf16_pairs(y_ref[0, 0]).astype(F32)
    y2 = _unpack_bf16_pairs(y_ref[0, 1]).astype(F32)
    h = x1_ref[...] + rt[:, 2:3] * y1 + rt[:, 3:4] * y2
    if final_norm:
        h = h * lax.rsqrt(jnp.mean(h * h, axis=-1, keepdims=True) + EPS) * fg_ref[...]
    o_ref[...] = h


def _combine_call(x1, rt, ysg, final_g, *, tm, final_norm):
    T, D = x1.shape
    return pl.pallas_call(
        functools.partial(_combine_kernel, final_norm=final_norm),
        grid=(T // tm,),
        in_specs=[
            pl.BlockSpec((tm, D), lambda t: (t, 0)),
            pl.BlockSpec((tm, ROUTER_COLS), lambda t: (t, 0)),
            pl.BlockSpec((1, TOP_K, tm, D // 2), lambda t: (t, 0, 0, 0)),
            pl.BlockSpec((1, D), lambda t: (0, 0)),
        ],
        out_specs=pl.BlockSpec((tm, D), lambda t: (t, 0)),
        out_shape=jax.ShapeDtypeStruct((T, D), F32),
        compiler_params=pltpu.CompilerParams(
            dimension_semantics=("parallel",), vmem_limit_bytes=40 * 1024 * 1024),
        name="moe_combine",
    )(x1, rt, ysg, final_g)


def _slot_layout(tile_counts, n_assign):
    NB = -(-n_assign // EXPERT_BLOCK) + N_EXPERTS
    counts = jnp.sum(tile_counts, axis=0)
    padded = ((counts + EXPERT_BLOCK - 1) // EXPERT_BLOCK) * EXPERT_BLOCK
    pad_end = jnp.cumsum(padded)
    pad_start = pad_end - padded
    tile_base = pad_start[None, :] + jnp.cumsum(tile_counts, axis=0) - tile_counts
    block_start = jnp.arange(NB, dtype=jnp.int32) * EXPERT_BLOCK
    block_expert = jnp.minimum(jnp.sum(pad_end[None, :] <= block_start[:, None], axis=1),
                               N_EXPERTS - 1).astype(jnp.int32)
    run_end = (pad_start + counts)[block_expert]
    n_valid = jnp.clip(run_end - block_start, 0, EXPERT_BLOCK).astype(jnp.int32)
    return NB, block_expert, n_valid, tile_base


def _rope_tables(S):
    inv = 1.0 / (ROPE_THETA ** (jnp.arange(0, HEAD_DIM, 2, dtype=F32) / HEAD_DIM))
    ang = jnp.arange(S, dtype=F32)[:, None] * inv[None, :]
    cos, sin = jnp.cos(ang), jnp.sin(ang)
    cos_l = jnp.tile(cos, (1, LANES // (HEAD_DIM // 2)))
    sin_l = jnp.tile(jnp.concatenate([-sin, sin], axis=1), (1, LANES // HEAD_DIM))
    return cos_l, sin_l, cos.T, sin.T


def kernel(x, norm1_g, w_in, lambda_q1, lambda_k1, lambda_q2, lambda_k2, subln_g, sinks, w_out,
           norm2_g, w_router_group, b_router_group, w_router_expert, b_router_expert,
           w_gate, w_up, w_down, final_g):
    B, S, D = x.shape
    T = B * S
    depth = w_in.shape[0]
    tq, tk = 512, 512
    tm_proj = 512
    tm_tok = 512
    tq_swa = 512
    qscale = HEAD_DIM ** -0.5 * math.log2(math.e)
    cos_l, sin_l, cos_t, sin_t = _rope_tables(S)

    c0 = DIFF_QK_COLS
    c1 = 2 * DIFF_QK_COLS
    c2 = c1 + DIFF_V_COLS
    c3 = c2 + SWA_Q_COLS
    c4 = c3 + SWA_KV_COLS
    for l in range(depth):
        lambda_init = 0.8 - 0.6 * math.exp(-0.3 * l)
        w = w_in[l]
        w_nat = jnp.concatenate([w[:, c0:c1], w[:, c3:c4]], axis=1).astype(BF16)
        w_tr = jnp.concatenate([w[:, :c0] * qscale, w[:, c1:c2], w[:, c2:c3] * qscale, w[:, c4:]],
                               axis=1).T.astype(BF16)
        dqt, dk, dvt, sqt, sk, svt = _proj_call(
            x, norm1_g[l][None, :], w_nat, w_tr, cos_l, sin_l, cos_t, sin_t, tm=tm_proj, tk=tk)

        lam_p = jnp.stack([lambda_q1[l], lambda_k1[l], lambda_q2[l], lambda_k2[l]]).astype(F32)
        o_diff = _diff_call(lam_p, dqt, dk, dvt, subln_g[l][None, :].astype(F32),
                            tq=tq, tk=tk, lambda_init=lambda_init)
        sink_row = jnp.repeat(sinks[l].astype(F32) * math.log2(math.e), WINDOW)[None, :]
        o_swa = _swa_call(sink_row, sqt, sk, svt, tq=tq_swa)

        wo_b = w_out[l].astype(BF16)
        w_router = jnp.zeros((D, ROUTER_COLS), F32)
        w_router = w_router.at[:, :N_GROUPS].set(w_router_group[l])
        w_router = w_router.at[:, N_GROUPS:N_GROUPS + N_EXPERTS].set(w_router_expert[l])
        w_router_hi = w_router.astype(BF16)
        w_router_lo = (w_router - w_router_hi.astype(F32)).astype(BF16)
        w_router = jnp.concatenate([w_router_hi, w_router_lo], axis=1)
        b_router = jnp.zeros((1, ROUTER_COLS), F32)
        b_router = b_router.at[0, :N_GROUPS].set(b_router_group[l])
        b_router = b_router.at[0, N_GROUPS:N_GROUPS + N_EXPERTS].set(b_router_expert[l])
        x1, n2p, rt, cnt = _mix_call(x, o_diff, o_swa, wo_b, norm2_g[l][None, :], w_router, b_router, tm=tm_tok)

        rt2 = rt.reshape(T, ROUTER_COLS)
        tile_counts = cnt[:, :, 0, :N_EXPERTS].reshape(T // tm_tok, N_EXPERTS).astype(jnp.int32)
        NB, block_expert, n_valid, tile_base = _slot_layout(tile_counts, T * TOP_K)
        tile_base = jnp.broadcast_to(tile_base.astype(F32)[:, :, None], (T // tm_tok, N_EXPERTS, LANES))
        dest = _slot_call(rt2, tile_base, tm=tm_tok)
        scatter_idx = jnp.swapaxes(dest[:, :TOP_K, :], 0, 1).reshape(TOP_K, T // SC_ROW_CHUNK, SC_ROW_CHUNK)
        xs = _sc_scatter_rows(n2p.reshape(T, D // 2), scatter_idx, NB * EXPERT_BLOCK)
        ys = _expert_call(block_expert, n_valid, xs, w_gate[l], w_up[l], w_down[l])
        ysg = _sc_gather_rows(ys, dest[:, :TOP_K, :].reshape(T * TOP_K))
        x = _combine_call(x1.reshape(T, D), rt2, ysg.reshape(T // tm_tok, TOP_K, tm_tok, D // 2),
                          final_g[None, :], tm=tm_tok, final_norm=(l == depth - 1)).reshape(B, S, D)
    return x
```

```python
import functools
import math

import jax
import jax.numpy as jnp
from jax import lax
from jax.experimental import pallas as pl
from jax.experimental.pallas import tpu as pltpu
from jax.experimental.pallas import tpu_sc as plsc

HEAD_DIM = 64
DIFF_HEADS = 4
DIFF_V_DIM = 2 * HEAD_DIM
SWA_Q_HEADS = 8
SWA_KV_HEADS = 2
SWA_GROUP = SWA_Q_HEADS // SWA_KV_HEADS
WINDOW = 128
ROPE_THETA = 10000.0
N_GROUPS = 4
EXPERTS_PER_GROUP = 8
N_EXPERTS = N_GROUPS * EXPERTS_PER_GROUP
TOP_K = 2
EXPERT_BLOCK = 512
EXPERT_CHUNK = 256
EPS = 1e-6
NEG = -1e30

DIFF_QK_COLS = DIFF_HEADS * 2 * HEAD_DIM
DIFF_V_COLS = DIFF_HEADS * DIFF_V_DIM
SWA_Q_COLS = SWA_Q_HEADS * HEAD_DIM
SWA_KV_COLS = SWA_KV_HEADS * HEAD_DIM
LANES = 128
SUBLANES = 8
BF16_SUBLANES = 16
VT_ROWS = DIFF_V_DIM + BF16_SUBLANES
SWA_VT_ROWS = SWA_KV_COLS + BF16_SUBLANES
ROUTER_COLS = LANES
DIFF_UNROLL = 4
DIFF_S_BUFS = 4

BF16 = jnp.bfloat16
F32 = jnp.float32


def _rope_lanes(x, cos_l, sin_l, first_half):
    rot = jnp.where(first_half, pltpu.roll(x, 96, 1), pltpu.roll(x, 32, 1))
    return x * cos_l + rot * sin_l


def _proj_kernel(x_ref, g_ref, wnat_ref, wtr_ref, cosl_ref, sinl_ref, cost_ref, sint_ref,
                 dqt_ref, dk_ref, dvt_ref, sqt_ref, sk_ref, svt_ref, *, tk):
    x = x_ref[0]
    tm = x.shape[0]
    n1 = x * lax.rsqrt(jnp.mean(x * x, axis=-1, keepdims=True) + EPS) * g_ref[...]
    n1b = n1.astype(BF16)
    nat = jnp.dot(n1b, wnat_ref[...], preferred_element_type=F32)
    tr = lax.dot_general(wtr_ref[...], n1b, (((1,), (1,)), ((), ())),
                         preferred_element_type=F32)

    cos_l, sin_l = cosl_ref[...], sinl_ref[...]
    first_half = (lax.broadcasted_iota(jnp.int32, (tm, LANES), 1) & (HEAD_DIM - 1)) < HEAD_DIM // 2
    for h in range(DIFF_HEADS):
        slab = nat[:, h * LANES:(h + 1) * LANES]
        dk_ref[0, h] = _rope_lanes(slab, cos_l, sin_l, first_half).astype(BF16)
    sk = _rope_lanes(nat[:, DIFF_QK_COLS:DIFF_QK_COLS + LANES], cos_l, sin_l, first_half).astype(BF16)
    for c in range(tm // WINDOW):
        sk_ref[0, c] = sk[c * WINDOW:(c + 1) * WINDOW]

    cos_t, sin_t = cost_ref[...], sint_ref[...]
    half = HEAD_DIM // 2

    def rope_rows(r0):
        x1 = tr[r0:r0 + half]
        x2 = tr[r0 + half:r0 + HEAD_DIM]
        return (x1 * cos_t - x2 * sin_t).astype(BF16), (x1 * sin_t + x2 * cos_t).astype(BF16)

    for h in range(DIFF_HEADS):
        for c in range(2):
            lo, hi = rope_rows(h * 2 * HEAD_DIM + c * HEAD_DIM)
            dqt_ref[0, h, c * HEAD_DIM:c * HEAD_DIM + half] = lo
            dqt_ref[0, h, c * HEAD_DIM + half:(c + 1) * HEAD_DIM] = hi
    ones_rows = (lax.broadcasted_iota(jnp.int32, (BF16_SUBLANES, tk), 0) == 0).astype(BF16)
    for h in range(DIFF_HEADS):
        r0 = DIFF_QK_COLS + h * DIFF_V_DIM
        for c in range(tm // tk):
            dvt_ref[0, h, c, :DIFF_V_DIM] = tr[r0:r0 + DIFF_V_DIM, c * tk:(c + 1) * tk].astype(BF16)
            dvt_ref[0, h, c, DIFF_V_DIM:] = ones_rows

    r0 = DIFF_QK_COLS + DIFF_V_COLS
    for h in range(SWA_Q_HEADS):
        lo, hi = rope_rows(r0 + h * HEAD_DIM)
        sqt_ref[0, h * HEAD_DIM:h * HEAD_DIM + half] = lo
        sqt_ref[0, h * HEAD_DIM + half:(h + 1) * HEAD_DIM] = hi
    r0 += SWA_Q_COLS
    for c in range(tm // WINDOW):
        svt_ref[0, c, :SWA_KV_COLS] = tr[r0:r0 + SWA_KV_COLS, c * WINDOW:(c + 1) * WINDOW].astype(BF16)
        svt_ref[0, c, SWA_KV_COLS:] = ones_rows[:, :WINDOW]


def _proj_call(x, g1, w_nat, w_tr, cos_l, sin_l, cos_t, sin_t, *, tm, tk):
    B, S, D = x.shape
    nkv = S // tk
    grid = (B, S // tm)
    const = lambda b, i: (0, 0)
    out_shape = (
        jax.ShapeDtypeStruct((B, DIFF_HEADS, 2 * HEAD_DIM, S), BF16),
        jax.ShapeDtypeStruct((B, DIFF_HEADS, S, 2 * HEAD_DIM), BF16),
        jax.ShapeDtypeStruct((B, DIFF_HEADS, nkv, VT_ROWS, tk), BF16),
        jax.ShapeDtypeStruct((B, SWA_Q_COLS, S), BF16),
        jax.ShapeDtypeStruct((B, S // WINDOW, WINDOW, SWA_KV_COLS), BF16),
        jax.ShapeDtypeStruct((B, S // WINDOW, SWA_VT_ROWS, WINDOW), BF16),
    )
    return pl.pallas_call(
        functools.partial(_proj_kernel, tk=tk),
        grid=grid,
        in_specs=[
            pl.BlockSpec((1, tm, D), lambda b, i: (b, i, 0)),
            pl.BlockSpec((1, D), const),
            pl.BlockSpec(w_nat.shape, const),
            pl.BlockSpec(w_tr.shape, const),
            pl.BlockSpec((tm, LANES), lambda b, i: (i, 0)),
            pl.BlockSpec((tm, LANES), lambda b, i: (i, 0)),
            pl.BlockSpec((HEAD_DIM // 2, tm), lambda b, i: (0, i)),
            pl.BlockSpec((HEAD_DIM // 2, tm), lambda b, i: (0, i)),
        ],
        out_specs=(
            pl.BlockSpec((1, DIFF_HEADS, 2 * HEAD_DIM, tm), lambda b, i: (b, 0, 0, i)),
            pl.BlockSpec((1, DIFF_HEADS, tm, 2 * HEAD_DIM), lambda b, i: (b, 0, i, 0)),
            pl.BlockSpec((1, DIFF_HEADS, tm // tk, VT_ROWS, tk), lambda b, i: (b, 0, i, 0, 0)),
            pl.BlockSpec((1, SWA_Q_COLS, tm), lambda b, i: (b, 0, i)),
            pl.BlockSpec((1, tm // WINDOW, WINDOW, SWA_KV_COLS), lambda b, i: (b, i, 0, 0)),
            pl.BlockSpec((1, tm // WINDOW, SWA_VT_ROWS, WINDOW), lambda b, i: (b, i, 0, 0)),
        ),
        out_shape=out_shape,
        compiler_params=pltpu.CompilerParams(
            dimension_semantics=("parallel", "parallel"), vmem_limit_bytes=48 * 1024 * 1024),
        name="proj_rope",
    )(x, g1, w_nat, w_tr, cos_l, sin_l, cos_t, sin_t)


def _diff_kernel(lam_ref, qt_ref, k_ref, vt_ref, g_ref, o_ref, *scratch, tq, tk, lambda_init):
    i = pl.program_id(2)
    s_bufs = scratch[:DIFF_S_BUFS]
    top_bufs = scratch[DIFF_S_BUFS:2 * DIFF_S_BUFS]
    m_ref, acc_ref = scratch[2 * DIFF_S_BUFS:]
    qt = qt_ref[0, 0]
    z = jnp.zeros((HEAD_DIM, tq), BF16)
    qw = jnp.concatenate([jnp.concatenate([qt[:HEAD_DIM], z], axis=1),
                          jnp.concatenate([z, qt[HEAD_DIM:]], axis=1)], axis=0)

    def scores(j, par):
        kt = k_ref[0, 0, pl.ds(pl.multiple_of(j * tk, tk), tk), :]
        s = jnp.dot(kt, qw, preferred_element_type=F32)
        s_bufs[par][...] = s
        top_bufs[par][...] = jnp.max(s, axis=0, keepdims=True)

    def absorb(j, par, masked):
        s = s_bufs[par][...]
        if masked:
            kpos = j * tk + lax.broadcasted_iota(jnp.int32, (tk, 2 * tq), 0)
            qpos = i * tq + (lax.broadcasted_iota(jnp.int32, (tk, 2 * tq), 1) & (tq - 1))
            s = jnp.where(kpos <= qpos, s, NEG)
            top = jnp.max(s, axis=0, keepdims=True)
        else:
            top = top_bufs[par][...]
        m = m_ref[...]
        m_new = jnp.maximum(m, top)
        alpha = jnp.exp2(m - m_new)
        p = jnp.exp2(s - m_new).astype(BF16)
        m_ref[...] = m_new
        pv = jnp.dot(vt_ref[0, 0, j], p, preferred_element_type=F32)
        acc_ref[...] = alpha * acc_ref[...] + pv

    m_ref[...] = jnp.full(m_ref.shape, NEG, F32)
    acc_ref[...] = jnp.zeros(acc_ref.shape, F32)

    nfull = (i * tq) // tk
    scores(nfull, 0)
    scores(0, 1)
    absorb(nfull, 0, True)

    def group(t, c):
        j = DIFF_UNROLL * t
        for idx in range(DIFF_UNROLL):
            scores(j + idx + 1, (idx + 2) % DIFF_S_BUFS)
            absorb(j + idx, (idx + 1) % DIFF_S_BUFS, False)
        return c

    lax.fori_loop(0, nfull // DIFF_UNROLL, group, 0)

    for rem in range(1, DIFF_UNROLL):
        @pl.when(nfull % DIFF_UNROLL == rem)
        def _():
            first = nfull - rem
            for idx in range(rem):
                if idx + 1 < rem:
                    scores(first + idx + 1, (idx + 2) % DIFF_S_BUFS)
                absorb(first + idx, (idx + 1) % DIFF_S_BUFS, False)

    lam_p = lam_ref[...]
    lam = (jnp.exp(jnp.sum(lam_p[0:1] * lam_p[1:2], axis=-1, keepdims=True))
           - jnp.exp(jnp.sum(lam_p[2:3] * lam_p[3:4], axis=-1, keepdims=True)) + lambda_init)
    inv_l = 1.0 / acc_ref[DIFF_V_DIM:DIFF_V_DIM + 1, :]
    o = (acc_ref[:DIFF_V_DIM, :tq] * inv_l[:, :tq]
         - lam * (acc_ref[:DIFF_V_DIM, tq:] * inv_l[:, tq:]))
    o = o * lax.rsqrt(jnp.mean(o * o, axis=0, keepdims=True) + EPS)
    o_ref[0] = (o.T * g_ref[...] * (1.0 - lambda_init)).astype(BF16)


def _diff_call(lam_p, dqt, dk, dvt, subln_g, *, tq, tk, lambda_init):
    B, H, _, S = dqt.shape
    assert tk % tq == 0 and S % tk == 0, "one key tile must cover a query tile's diagonal"
    nkv = S // tk
    grid = (B, H, S // tq)
    return pl.pallas_call(
        functools.partial(_diff_kernel, tq=tq, tk=tk, lambda_init=lambda_init),
        grid=grid,
        in_specs=[
            pl.BlockSpec(lam_p.shape, lambda b, h, i: (0, 0)),
            pl.BlockSpec((1, 1, 2 * HEAD_DIM, tq), lambda b, h, i: (b, h, 0, i)),
            pl.BlockSpec((1, 1, S, 2 * HEAD_DIM), lambda b, h, i: (b, h, 0, 0)),
            pl.BlockSpec((1, 1, nkv, VT_ROWS, tk), lambda b, h, i: (b, h, 0, 0, 0)),
            pl.BlockSpec((1, DIFF_V_DIM), lambda b, h, i: (0, 0)),
        ],
        out_specs=pl.BlockSpec((1, tq, DIFF_V_DIM), lambda b, h, i: (b, i, h)),
        out_shape=jax.ShapeDtypeStruct((B, S, DIFF_V_COLS), BF16),
        scratch_shapes=[pltpu.VMEM((tk, 2 * tq), F32)] * DIFF_S_BUFS + [
            pltpu.VMEM((1, 2 * tq), F32)] * DIFF_S_BUFS + [
            pltpu.VMEM((1, 2 * tq), F32),
            pltpu.VMEM((VT_ROWS, 2 * tq), F32),
        ],
        compiler_params=pltpu.CompilerParams(
            dimension_semantics=("parallel", "parallel", "arbitrary"),
            vmem_limit_bytes=48 * 1024 * 1024),
        name="diff_attn",
    )(lam_p, dqt, dk, dvt, subln_g)


def _swa_kernel(sink_ref, qt_ref, k_ref, vt_ref, o_ref, *, tq):
    i = pl.program_id(1)
    n_cols = SWA_Q_HEADS * WINDOW
    half_cols = n_cols // SWA_KV_HEADS
    sink = sink_ref[...]
    row = lax.broadcasted_iota(jnp.int32, (2 * WINDOW, WINDOW), 0)
    qrel = lax.broadcasted_iota(jnp.int32, (2 * WINDOW, WINDOW), 1)
    band = (row - WINDOW <= qrel) & (row > qrel)
    in_current = row >= WINDOW
    z = jnp.zeros((HEAD_DIM, half_cols), BF16)
    for sub in range(tq // WINDOW):
        n = i * (tq // WINDOW) + sub
        prev = jnp.maximum(n - 1, 0)
        kwin = jnp.concatenate([k_ref[0, prev], k_ref[0, n]], axis=0)
        vtwin = jnp.concatenate([vt_ref[0, prev], vt_ref[0, n]], axis=1)
        qt = qt_ref[0, :, sub * WINDOW:(sub + 1) * WINDOW]
        heads = [qt[h * HEAD_DIM:(h + 1) * HEAD_DIM] for h in range(SWA_Q_HEADS)]
        qw = jnp.concatenate(
            [jnp.concatenate(heads[:SWA_GROUP] + [z], axis=1),
             jnp.concatenate([z] + heads[SWA_GROUP:], axis=1)], axis=0)
        s = jnp.dot(kwin, qw, preferred_element_type=F32)
        valid = band & (in_current | (n >= 1))
        s = jnp.concatenate(
            [jnp.where(valid, s[:, h * WINDOW:(h + 1) * WINDOW], NEG) for h in range(SWA_Q_HEADS)], axis=1)
        m = jnp.maximum(jnp.max(s, axis=0, keepdims=True), sink)
        p = jnp.exp2(s - m).astype(BF16)
        acc = jnp.dot(vtwin, p, preferred_element_type=F32)
        den = acc[SWA_KV_COLS:SWA_KV_COLS + 1] + jnp.exp2(sink - m)
        on = acc[:SWA_KV_COLS] / den
        u = jnp.concatenate([on[:HEAD_DIM, :half_cols], on[HEAD_DIM:, half_cols:]], axis=1)
        for hp in range(SWA_Q_HEADS // 2):
            two = jnp.concatenate([u[:, (2 * hp) * WINDOW:(2 * hp + 1) * WINDOW],
                                   u[:, (2 * hp + 1) * WINDOW:(2 * hp + 2) * WINDOW]], axis=0)
            o_ref[0, sub * WINDOW:(sub + 1) * WINDOW, hp * LANES:(hp + 1) * LANES] = two.T.astype(BF16)


def _swa_call(sink_row, sqt, sk, svt, *, tq):
    B, _, S = sqt.shape
    nb = S // WINDOW
    return pl.pallas_call(
        functools.partial(_swa_kernel, tq=tq),
        grid=(B, S // tq),
        in_specs=[
            pl.BlockSpec(sink_row.shape, lambda b, i: (0, 0)),
            pl.BlockSpec((1, SWA_Q_COLS, tq), lambda b, i: (b, 0, i)),
            pl.BlockSpec((1, nb, WINDOW, SWA_KV_COLS), lambda b, i: (b, 0, 0, 0)),
            pl.BlockSpec((1, nb, SWA_VT_ROWS, WINDOW), lambda b, i: (b, 0, 0, 0)),
        ],
        out_specs=pl.BlockSpec((1, tq, SWA_Q_COLS), lambda b, i: (b, i, 0)),
        out_shape=jax.ShapeDtypeStruct((B, S, SWA_Q_COLS), BF16),
        compiler_params=pltpu.CompilerParams(
            dimension_semantics=("parallel", "arbitrary"), vmem_limit_bytes=40 * 1024 * 1024),
        name="swa_attn",
    )(sink_row, sqt, sk, svt)


def _pack_bf16_pairs(x):
    n = x.shape[1] // 2
    lo = lax.bitcast_convert_type(x[:, :n].astype(BF16).astype(F32), jnp.uint32)
    hi = lax.bitcast_convert_type(x[:, n:].astype(BF16).astype(F32), jnp.uint32)
    return (lo >> 16) | (hi & jnp.uint32(0xFFFF0000))


def _unpack_bf16_pairs(w):
    lo = lax.bitcast_convert_type(w << 16, F32)
    hi = lax.bitcast_convert_type(w & jnp.uint32(0xFFFF0000), F32)
    return jnp.concatenate([lo, hi], axis=1).astype(BF16)


MIX_CHUNKS = 1


def _mix_kernel(x_ref, od_ref, os_ref, wo_ref, g2_ref, wr_ref, br_ref, x1_ref, n2_ref, rt_ref, cnt_ref):
    tm = x_ref.shape[1] // MIX_CHUNKS
    lane = lax.broadcasted_iota(jnp.int32, (tm, ROUTER_COLS), 1)
    big = jnp.int32(ROUTER_COLS)
    counts = jnp.zeros((1, ROUTER_COLS), F32)
    x1_ref[0] = (x_ref[0]
                 + jnp.dot(od_ref[0], wo_ref[:DIFF_V_COLS], preferred_element_type=F32)
                 + jnp.dot(os_ref[0], wo_ref[DIFF_V_COLS:], preferred_element_type=F32))
    for c in range(MIX_CHUNKS):
        rows = pl.ds(c * tm, tm)
        h = x1_ref[0, rows, :]
        n2 = h * lax.rsqrt(jnp.mean(h * h, axis=-1, keepdims=True) + EPS) * g2_ref[...]
        n2_ref[0, rows, :] = _pack_bf16_pairs(n2)
        n2_hi = n2.astype(BF16)
        n2_lo = (n2 - n2_hi.astype(F32)).astype(BF16)
        parts = jnp.dot(jnp.concatenate([n2_hi, n2_lo], axis=0), wr_ref[...],
                        preferred_element_type=F32)
        logits = ((parts[:tm, :ROUTER_COLS] + parts[tm:, ROUTER_COLS:])
                  + (parts[:tm, ROUTER_COLS:] + parts[tm:, :ROUTER_COLS])) + br_ref[...]
        gl = jnp.where(lane < N_GROUPS, logits, -jnp.inf)
        gm = jnp.max(gl, axis=-1, keepdims=True)
        p_top = 1.0 / jnp.sum(jnp.exp(gl - gm), axis=-1, keepdims=True)
        g_idx = jnp.min(jnp.where(gl == gm, lane, big), axis=-1, keepdims=True)
        e_lo = N_GROUPS + EXPERTS_PER_GROUP * g_idx
        el = jnp.where((lane >= e_lo) & (lane < e_lo + EXPERTS_PER_GROUP), logits, -jnp.inf)
        v1 = jnp.max(el, axis=-1, keepdims=True)
        i1 = jnp.min(jnp.where(el == v1, lane, big), axis=-1, keepdims=True)
        el2 = jnp.where(lane == i1, -jnp.inf, el)
        v2 = jnp.max(el2, axis=-1, keepdims=True)
        i2 = jnp.min(jnp.where(el2 == v2, lane, big), axis=-1, keepdims=True)
        e21 = jnp.exp(v2 - v1)
        gate1 = p_top / (1.0 + e21)
        gate2 = p_top * e21 / (1.0 + e21)
        rt_ref[0, rows, :] = jnp.where(lane == 0, (i1 - N_GROUPS).astype(F32),
                             jnp.where(lane == 1, (i2 - N_GROUPS).astype(F32),
                             jnp.where(lane == 2, gate1, jnp.where(lane == 3, gate2, 0.0))))
        chosen = ((lane == i1 - N_GROUPS) | (lane == i2 - N_GROUPS)).astype(F32)
        counts = counts + jnp.sum(chosen, axis=0, keepdims=True)
    cnt_ref[0, 0] = jnp.broadcast_to(counts, cnt_ref.shape[2:])


def _mix_call(x, o_diff, o_swa, w_out, g2, w_router, b_router, *, tm):
    B, S, D = x.shape
    const = lambda b, i: (0, 0)
    row = lambda b, i: (b, i, 0)
    nt = S // tm
    return pl.pallas_call(
        _mix_kernel,
        grid=(B, nt),
        in_specs=[
            pl.BlockSpec((1, tm, D), row),
            pl.BlockSpec((1, tm, DIFF_V_COLS), row),
            pl.BlockSpec((1, tm, SWA_Q_COLS), row),
            pl.BlockSpec(w_out.shape, const),
            pl.BlockSpec((1, D), const),
            pl.BlockSpec(w_router.shape, const),
            pl.BlockSpec((1, ROUTER_COLS), const),
        ],
        out_specs=(pl.BlockSpec((1, tm, D), row), pl.BlockSpec((1, tm, D // 2), row),
                   pl.BlockSpec((1, tm, ROUTER_COLS), row),
                   pl.BlockSpec((1, 1, SUBLANES, ROUTER_COLS), lambda b, i: (b, i, 0, 0))),
        out_shape=(jax.ShapeDtypeStruct((B, S, D), F32), jax.ShapeDtypeStruct((B, S, D // 2), jnp.uint32),
                   jax.ShapeDtypeStruct((B, S, ROUTER_COLS), F32),
                   jax.ShapeDtypeStruct((B, nt, SUBLANES, ROUTER_COLS), F32)),
        compiler_params=pltpu.CompilerParams(
            dimension_semantics=("parallel", "parallel"), vmem_limit_bytes=48 * 1024 * 1024),
        name="outproj_router",
    )(x, o_diff, o_swa, w_out, g2, w_router, b_router)


def _slot_kernel(rt_ref, base_ref, dest_ref, *, tm):
    rt_t = rt_ref[...].T
    e1 = rt_t[0:1].astype(jnp.int32)
    e2 = rt_t[1:2].astype(jnp.int32)
    eid = lax.broadcasted_iota(jnp.int32, (N_EXPERTS, tm), 0)
    oh1 = eid == e1
    oh2 = eid == e2
    earlier = (lax.broadcasted_iota(jnp.int32, (tm, tm), 0)
               < lax.broadcasted_iota(jnp.int32, (tm, tm), 1)).astype(BF16)
    before = jnp.dot((oh1 | oh2).astype(BF16), earlier, preferred_element_type=F32)
    slot = before + base_ref[0][:, 0:1]
    d1 = jnp.sum(jnp.where(oh1, slot, 0.0), axis=0, keepdims=True).astype(jnp.int32)
    d2 = jnp.sum(jnp.where(oh2, slot, 0.0), axis=0, keepdims=True).astype(jnp.int32)
    dest_ref[0] = jnp.concatenate([d1, d2, jnp.zeros((SUBLANES - TOP_K, tm), jnp.int32)], axis=0)


def _slot_call(rt, tile_base, *, tm):
    nt = rt.shape[0] // tm
    return pl.pallas_call(
        functools.partial(_slot_kernel, tm=tm),
        grid=(nt,),
        in_specs=[
            pl.BlockSpec((tm, ROUTER_COLS), lambda t: (t, 0)),
            pl.BlockSpec((1, N_EXPERTS, LANES), lambda t: (t, 0, 0)),
        ],
        out_specs=pl.BlockSpec((1, SUBLANES, tm), lambda t: (t, 0, 0)),
        out_shape=jax.ShapeDtypeStruct((nt, SUBLANES, tm), jnp.int32),
        compiler_params=pltpu.CompilerParams(dimension_semantics=("parallel",)),
        name="moe_slots",
    )(rt, tile_base)


SC_ROW_CHUNK = 64


def _sc_workers():
    info = plsc.get_sparse_core_info()
    return info.num_cores, info.num_cores * info.num_subcores


def _sc_scatter_rows(rows, idx, n_out):
    n, width = rows.shape
    n_cores, n_workers = _sc_workers()
    n_chunks = n // SC_ROW_CHUNK
    per_worker = n_chunks // n_workers
    assert n_chunks % n_workers == 0
    mesh = plsc.VectorSubcoreMesh(core_axis_name="c", subcore_axis_name="s")

    @functools.partial(
        pl.kernel, mesh=mesh,
        out_type=jax.ShapeDtypeStruct((n_out, width), rows.dtype),
        scratch_types=[
            pltpu.VMEM((SC_ROW_CHUNK,), jnp.int32),
            pltpu.VMEM((SC_ROW_CHUNK, width), rows.dtype),
        ],
    )
    def scatter(rows_hbm, idx_hbm, out_hbm, idx_v, rows_v):
        worker = lax.axis_index("s") * n_cores + lax.axis_index("c")

        @pl.loop(0, per_worker)
        def _(i):
            c = worker * per_worker + i
            pltpu.sync_copy(rows_hbm.at[pl.ds(pl.multiple_of(c * SC_ROW_CHUNK, SC_ROW_CHUNK), SC_ROW_CHUNK)], rows_v)
            for k in range(TOP_K):
                pltpu.sync_copy(idx_hbm.at[k, c], idx_v)
                pltpu.sync_copy(rows_v, out_hbm.at[idx_v])

    return scatter(rows, idx)


def _expert_kernel(be_ref, nvalid_ref, xs_ref, wg_ref, wu_ref, wd_ref, y_ref, wg_b, wu_b, wd_b):
    b = pl.program_id(0)
    n_valid = nvalid_ref[b]

    @pl.when(n_valid > 0)
    def _():
        @pl.when((b == 0) | (be_ref[b] != be_ref[jnp.maximum(b - 1, 0)]))
        def _():
            wg_b[...] = wg_ref[0].astype(BF16)
            wu_b[...] = wu_ref[0].astype(BF16)
            wd_b[...] = wd_ref[0].astype(BF16)

        for c in range(EXPERT_BLOCK // EXPERT_CHUNK):
            rows = pl.ds(c * EXPERT_CHUNK, EXPERT_CHUNK)
            row_id = c * EXPERT_CHUNK + lax.broadcasted_iota(jnp.int32, (EXPERT_CHUNK, xs_ref.shape[1]), 0)
            packed = jnp.where(row_id < n_valid, xs_ref[rows, :], jnp.uint32(0))
            xb = _unpack_bf16_pairs(packed)
            gate = jnp.dot(xb, wg_b[...], preferred_element_type=F32)
            up = jnp.dot(xb, wu_b[...], preferred_element_type=F32)
            hid = (gate * jax.nn.sigmoid(gate) * up).astype(BF16)
            y_ref[rows, :] = _pack_bf16_pairs(jnp.dot(hid, wd_b[...], preferred_element_type=F32))

    @pl.when(n_valid == 0)
    def _():
        y_ref[...] = jnp.zeros_like(y_ref)


def _expert_call(block_expert, n_valid, xs, w_gate, w_up, w_down):
    P = xs.shape[0]
    NB = P // EXPERT_BLOCK
    E, D, F = w_gate.shape
    grid_spec = pltpu.PrefetchScalarGridSpec(
        num_scalar_prefetch=2,
        grid=(NB,),
        in_specs=[
            pl.BlockSpec((EXPERT_BLOCK,) + xs.shape[1:], lambda b, be, nu: (b, 0)),
            pl.BlockSpec((1, D, F), lambda b, be, nu: (be[b], 0, 0)),
            pl.BlockSpec((1, D, F), lambda b, be, nu: (be[b], 0, 0)),
            pl.BlockSpec((1, F, D), lambda b, be, nu: (be[b], 0, 0)),
        ],
        out_specs=pl.BlockSpec((EXPERT_BLOCK, D // 2), lambda b, be, nu: (b, 0)),
        scratch_shapes=[
            pltpu.VMEM((D, F), BF16),
            pltpu.VMEM((D, F), BF16),
            pltpu.VMEM((F, D), BF16),
        ],
    )
    return pl.pallas_call(
        _expert_kernel,
        grid_spec=grid_spec,
        out_shape=jax.ShapeDtypeStruct((P, D // 2), jnp.uint32),
        compiler_params=pltpu.CompilerParams(
            dimension_semantics=("arbitrary",), vmem_limit_bytes=48 * 1024 * 1024),
        name="moe_experts",
    )(block_expert, n_valid, xs, w_gate, w_up, w_down)


def _sc_gather_rows(table, idx):
    n_rows, width = idx.shape[0], table.shape[1]
    n_cores, n_workers = _sc_workers()
    per_worker = n_rows // n_workers
    assert n_rows % (n_workers * 2 * SC_ROW_CHUNK) == 0
    mesh = plsc.VectorSubcoreMesh(core_axis_name="c", subcore_axis_name="s")

    @functools.partial(
        pl.kernel, mesh=mesh,
        out_type=jax.ShapeDtypeStruct((n_rows, width), table.dtype),
        scratch_types=[pltpu.VMEM((SC_ROW_CHUNK,), jnp.int32)] * 2
        + [pltpu.VMEM((SC_ROW_CHUNK, width), table.dtype)] * 2
        + [pltpu.SemaphoreType.DMA] * 4,
    )
    def gather(table_hbm, idx_hbm, out_hbm, idx_a, idx_b, rows_a, rows_b, sem_ga, sem_gb, sem_wa, sem_wb):
        worker = lax.axis_index("s") * n_cores + lax.axis_index("c")
        base = worker * per_worker

        @pl.loop(0, per_worker // SC_ROW_CHUNK, step=2)
        def _(c):
            off_a = pl.multiple_of(base + c * SC_ROW_CHUNK, SC_ROW_CHUNK)
            off_b = pl.multiple_of(off_a + SC_ROW_CHUNK, SC_ROW_CHUNK)
            pltpu.sync_copy(idx_hbm.at[pl.ds(off_a, SC_ROW_CHUNK)], idx_a)
            pltpu.sync_copy(idx_hbm.at[pl.ds(off_b, SC_ROW_CHUNK)], idx_b)
            gather_a = pltpu.async_copy(table_hbm.at[idx_a], rows_a, sem_ga)
            gather_b = pltpu.async_copy(table_hbm.at[idx_b], rows_b, sem_gb)
            gather_a.wait()
            write_a = pltpu.async_copy(rows_a, out_hbm.at[pl.ds(off_a, SC_ROW_CHUNK)], sem_wa)
            gather_b.wait()
            write_b = pltpu.async_copy(rows_b, out_hbm.at[pl.ds(off_b, SC_ROW_CHUNK)], sem_wb)
            write_a.wait()
            write_b.wait()

    return gather(table, idx)


def _combine_kernel(x1_ref, rt_ref, y_ref, fg_ref, o_ref, *, final_norm):
    rt = rt_ref[...]
    y1 = _unpack_bf16_pairs(y_ref[0, 0]).astype(F32)
    y2 = _unpack_bf16_pairs(y_ref[0, 1]).astype(F32)
    h = x1_ref[...] + rt[:, 2:3] * y1 + rt[:, 3:4] * y2
    if final_norm:
        h = h * lax.rsqrt(jnp.mean(h * h, axis=-1, keepdims=True) + EPS) * fg_ref[...]
    o_ref[...] = h


def _combine_call(x1, rt, ysg, final_g, *, tm, final_norm):
    T, D = x1.shape
    return pl.pallas_call(
        functools.partial(_combine_kernel, final_norm=final_norm),
        grid=(T // tm,),
        in_specs=[
            pl.BlockSpec((tm, D), lambda t: (t, 0)),
            pl.BlockSpec((tm, ROUTER_COLS), lambda t: (t, 0)),
            pl.BlockSpec((1, TOP_K, tm, D // 2), lambda t: (t, 0, 0, 0)),
            pl.BlockSpec((1, D), lambda t: (0, 0)),
        ],
        out_specs=pl.BlockSpec((tm, D), lambda t: (t, 0)),
        out_shape=jax.ShapeDtypeStruct((T, D), F32),
        compiler_params=pltpu.CompilerParams(
            dimension_semantics=("parallel",), vmem_limit_bytes=40 * 1024 * 1024),
        name="moe_combine",
    )(x1, rt, ysg, final_g)


def _slot_layout(tile_counts, n_assign):
    NB = -(-n_assign // EXPERT_BLOCK) + N_EXPERTS
    counts = jnp.sum(tile_counts, axis=0)
    padded = ((counts + EXPERT_BLOCK - 1) // EXPERT_BLOCK) * EXPERT_BLOCK
    pad_end = jnp.cumsum(padded)
    pad_start = pad_end - padded
    tile_base = pad_start[None, :] + jnp.cumsum(tile_counts, axis=0) - tile_counts
    block_start = jnp.arange(NB, dtype=jnp.int32) * EXPERT_BLOCK
    block_expert = jnp.minimum(jnp.sum(pad_end[None, :] <= block_start[:, None], axis=1),
                               N_EXPERTS - 1).astype(jnp.int32)
    run_end = (pad_start + counts)[block_expert]
    n_valid = jnp.clip(run_end - block_start, 0, EXPERT_BLOCK).astype(jnp.int32)
    return NB, block_expert, n_valid, tile_base


def _rope_tables(S):
    inv = 1.0 / (ROPE_THETA ** (jnp.arange(0, HEAD_DIM, 2, dtype=F32) / HEAD_DIM))
    ang = jnp.arange(S, dtype=F32)[:, None] * inv[None, :]
    cos, sin = jnp.cos(ang), jnp.sin(ang)
    cos_l = jnp.tile(cos, (1, LANES // (HEAD_DIM // 2)))
    sin_l = jnp.tile(jnp.concatenate([-sin, sin], axis=1), (1, LANES // HEAD_DIM))
    return cos_l, sin_l, cos.T, sin.T


def kernel(x, norm1_g, w_in, lambda_q1, lambda_k1, lambda_q2, lambda_k2, subln_g, sinks, w_out,
           norm2_g, w_router_group, b_router_group, w_router_expert, b_router_expert,
           w_gate, w_up, w_down, final_g):
    B, S, D = x.shape
    T = B * S
    depth = w_in.shape[0]
    tq, tk = 512, 512
    tm_proj = 512
    tm_tok = 512
    tq_swa = 1024
    qscale = HEAD_DIM ** -0.5 * math.log2(math.e)
    cos_l, sin_l, cos_t, sin_t = _rope_tables(S)

    c0 = DIFF_QK_COLS
    c1 = 2 * DIFF_QK_COLS
    c2 = c1 + DIFF_V_COLS
    c3 = c2 + SWA_Q_COLS
    c4 = c3 + SWA_KV_COLS
    for l in range(depth):
        lambda_init = 0.8 - 0.6 * math.exp(-0.3 * l)
        w = w_in[l]
        w_nat = jnp.concatenate([w[:, c0:c1], w[:, c3:c4]], axis=1).astype(BF16)
        w_tr = jnp.concatenate([w[:, :c0] * qscale, w[:, c1:c2], w[:, c2:c3] * qscale, w[:, c4:]],
                               axis=1).T.astype(BF16)
        dqt, dk, dvt, sqt, sk, svt = _proj_call(
            x, norm1_g[l][None, :], w_nat, w_tr, cos_l, sin_l, cos_t, sin_t, tm=tm_proj, tk=tk)

        lam_p = jnp.stack([lambda_q1[l], lambda_k1[l], lambda_q2[l], lambda_k2[l]]).astype(F32)
        o_diff = _diff_call(lam_p, dqt, dk, dvt, subln_g[l][None, :].astype(F32),
                            tq=tq, tk=tk, lambda_init=lambda_init)
        sink_row = jnp.repeat(sinks[l].astype(F32) * math.log2(math.e), WINDOW)[None, :]
        o_swa = _swa_call(sink_row, sqt, sk, svt, tq=tq_swa)

        wo_b = w_out[l].astype(BF16)
        w_router = jnp.zeros((D, ROUTER_COLS), F32)
        w_router = w_router.at[:, :N_GROUPS].set(w_router_group[l])
        w_router = w_router.at[:, N_GROUPS:N_GROUPS + N_EXPERTS].set(w_router_expert[l])
        w_router_hi = w_router.astype(BF16)
        w_router_lo = (w_router - w_router_hi.astype(F32)).astype(BF16)
        w_router = jnp.concatenate([w_router_hi, w_router_lo], axis=1)
        b_router = jnp.zeros((1, ROUTER_COLS), F32)
        b_router = b_router.at[0, :N_GROUPS].set(b_router_group[l])
        b_router = b_router.at[0, N_GROUPS:N_GROUPS + N_EXPERTS].set(b_router_expert[l])
        x1, n2p, rt, cnt = _mix_call(x, o_diff, o_swa, wo_b, norm2_g[l][None, :], w_router, b_router, tm=tm_tok)

        rt2 = rt.reshape(T, ROUTER_COLS)
        tile_counts = cnt[:, :, 0, :N_EXPERTS].reshape(T // tm_tok, N_EXPERTS).astype(jnp.int32)
        NB, block_expert, n_valid, tile_base = _slot_layout(tile_counts, T * TOP_K)
        tile_base = jnp.broadcast_to(tile_base.astype(F32)[:, :, None], (T // tm_tok, N_EXPERTS, LANES))
        dest = _slot_call(rt2, tile_base, tm=tm_tok)
        scatter_idx = jnp.swapaxes(dest[:, :TOP_K, :], 0, 1).reshape(TOP_K, T // SC_ROW_CHUNK, SC_ROW_CHUNK)
        xs = _sc_scatter_rows(n2p.reshape(T, D // 2), scatter_idx, NB * EXPERT_BLOCK)
        ys = _expert_call(block_expert, n_valid, xs, w_gate[l], w_up[l], w_down[l])
        ysg = _sc_gather_rows(ys, dest[:, :TOP_K, :].reshape(T * TOP_K))
        x = _combine_call(x1.reshape(T, D), rt2, ysg.reshape(T // tm_tok, TOP_K, tm_tok, D // 2),
                          final_g[None, :], tm=tm_tok, final_norm=(l == depth - 1)).reshape(B, S, D)
    return x
```

```python
import functools
import math

import jax
import jax.numpy as jnp
from jax import lax
from jax.experimental import pallas as pl
from jax.experimental.pallas import tpu as pltpu
from jax.experimental.pallas import tpu_sc as plsc

HEAD_DIM = 64
DIFF_HEADS = 4
DIFF_V_DIM = 2 * HEAD_DIM
SWA_Q_HEADS = 8
SWA_KV_HEADS = 2
SWA_GROUP = SWA_Q_HEADS // SWA_KV_HEADS
WINDOW = 128
ROPE_THETA = 10000.0
N_GROUPS = 4
EXPERTS_PER_GROUP = 8
N_EXPERTS = N_GROUPS * EXPERTS_PER_GROUP
TOP_K = 2
EXPERT_BLOCK = 512
EXPERT_CHUNK = 256
EPS = 1e-6
NEG = -1e30

DIFF_QK_COLS = DIFF_HEADS * 2 * HEAD_DIM
DIFF_V_COLS = DIFF_HEADS * DIFF_V_DIM
SWA_Q_COLS = SWA_Q_HEADS * HEAD_DIM
SWA_KV_COLS = SWA_KV_HEADS * HEAD_DIM
LANES = 128
SUBLANES = 8
BF16_SUBLANES = 16
VT_ROWS = DIFF_V_DIM + BF16_SUBLANES
SWA_VT_ROWS = SWA_KV_COLS + BF16_SUBLANES
ROUTER_COLS = LANES
DIFF_UNROLL = 4
DIFF_S_BUFS = 4

BF16 = jnp.bfloat16
F32 = jnp.float32


def _rope_lanes(x, cos_l, sin_l, first_half):
    rot = jnp.where(first_half, pltpu.roll(x, 96, 1), pltpu.roll(x, 32, 1))
    return x * cos_l + rot * sin_l


def _proj_kernel(x_ref, g_ref, wnat_ref, wtr_ref, cosl_ref, sinl_ref, cost_ref, sint_ref,
                 dqt_ref, dk_ref, dvt_ref, sqt_ref, sk_ref, svt_ref, *, tk):
    x = x_ref[0]
    tm = x.shape[0]
    n1 = x * lax.rsqrt(jnp.mean(x * x, axis=-1, keepdims=True) + EPS) * g_ref[...]
    n1b = n1.astype(BF16)
    nat = jnp.dot(n1b, wnat_ref[...], preferred_element_type=F32)
    tr = lax.dot_general(wtr_ref[...], n1b, (((1,), (1,)), ((), ())),
                         preferred_element_type=F32)

    cos_l, sin_l = cosl_ref[...], sinl_ref[...]
    first_half = (lax.broadcasted_iota(jnp.int32, (tm, LANES), 1) & (HEAD_DIM - 1)) < HEAD_DIM // 2
    for h in range(DIFF_HEADS):
        slab = nat[:, h * LANES:(h + 1) * LANES]
        dk_ref[0, h] = _rope_lanes(slab, cos_l, sin_l, first_half).astype(BF16)
    sk = _rope_lanes(nat[:, DIFF_QK_COLS:DIFF_QK_COLS + LANES], cos_l, sin_l, first_half).astype(BF16)
    for c in range(tm // WINDOW):
        sk_ref[0, c] = sk[c * WINDOW:(c + 1) * WINDOW]

    cos_t, sin_t = cost_ref[...], sint_ref[...]
    half = HEAD_DIM // 2

    def rope_rows(r0):
        x1 = tr[r0:r0 + half]
        x2 = tr[r0 + half:r0 + HEAD_DIM]
        return (x1 * cos_t - x2 * sin_t).astype(BF16), (x1 * sin_t + x2 * cos_t).astype(BF16)

    for h in range(DIFF_HEADS):
        for c in range(2):
            lo, hi = rope_rows(h * 2 * HEAD_DIM + c * HEAD_DIM)
            dqt_ref[0, h, c * HEAD_DIM:c * HEAD_DIM + half] = lo
            dqt_ref[0, h, c * HEAD_DIM + half:(c + 1) * HEAD_DIM] = hi
    ones_rows = (lax.broadcasted_iota(jnp.int32, (BF16_SUBLANES, tk), 0) == 0).astype(BF16)
    for h in range(DIFF_HEADS):
        r0 = DIFF_QK_COLS + h * DIFF_V_DIM
        for c in range(tm // tk):
            dvt_ref[0, h, c, :DIFF_V_DIM] = tr[r0:r0 + DIFF_V_DIM, c * tk:(c + 1) * tk].astype(BF16)
            dvt_ref[0, h, c, DIFF_V_DIM:] = ones_rows

    r0 = DIFF_QK_COLS + DIFF_V_COLS
    for h in range(SWA_Q_HEADS):
        lo, hi = rope_rows(r0 + h * HEAD_DIM)
        sqt_ref[0, h * HEAD_DIM:h * HEAD_DIM + half] = lo
        sqt_ref[0, h * HEAD_DIM + half:(h + 1) * HEAD_DIM] = hi
    r0 += SWA_Q_COLS
    for c in range(tm // WINDOW):
        svt_ref[0, c, :SWA_KV_COLS] = tr[r0:r0 + SWA_KV_COLS, c * WINDOW:(c + 1) * WINDOW].astype(BF16)
        svt_ref[0, c, SWA_KV_COLS:] = ones_rows[:, :WINDOW]


def _proj_call(x, g1, w_nat, w_tr, cos_l, sin_l, cos_t, sin_t, *, tm, tk):
    B, S, D = x.shape
    nkv = S // tk
    grid = (B, S // tm)
    const = lambda b, i: (0, 0)
    out_shape = (
        jax.ShapeDtypeStruct((B, DIFF_HEADS, 2 * HEAD_DIM, S), BF16),
        jax.ShapeDtypeStruct((B, DIFF_HEADS, S, 2 * HEAD_DIM), BF16),
        jax.ShapeDtypeStruct((B, DIFF_HEADS, nkv, VT_ROWS, tk), BF16),
        jax.ShapeDtypeStruct((B, SWA_Q_COLS, S), BF16),
        jax.ShapeDtypeStruct((B, S // WINDOW, WINDOW, SWA_KV_COLS), BF16),
        jax.ShapeDtypeStruct((B, S // WINDOW, SWA_VT_ROWS, WINDOW), BF16),
    )
    return pl.pallas_call(
        functools.partial(_proj_kernel, tk=tk),
        grid=grid,
        in_specs=[
            pl.BlockSpec((1, tm, D), lambda b, i: (b, i, 0)),
            pl.BlockSpec((1, D), const),
            pl.BlockSpec(w_nat.shape, const),
            pl.BlockSpec(w_tr.shape, const),
            pl.BlockSpec((tm, LANES), lambda b, i: (i, 0)),
            pl.BlockSpec((tm, LANES), lambda b, i: (i, 0)),
            pl.BlockSpec((HEAD_DIM // 2, tm), lambda b, i: (0, i)),
            pl.BlockSpec((HEAD_DIM // 2, tm), lambda b, i: (0, i)),
        ],
        out_specs=(
            pl.BlockSpec((1, DIFF_HEADS, 2 * HEAD_DIM, tm), lambda b, i: (b, 0, 0, i)),
            pl.BlockSpec((1, DIFF_HEADS, tm, 2 * HEAD_DIM), lambda b, i: (b, 0, i, 0)),
            pl.BlockSpec((1, DIFF_HEADS, tm // tk, VT_ROWS, tk), lambda b, i: (b, 0, i, 0, 0)),
            pl.BlockSpec((1, SWA_Q_COLS, tm), lambda b, i: (b, 0, i)),
            pl.BlockSpec((1, tm // WINDOW, WINDOW, SWA_KV_COLS), lambda b, i: (b, i, 0, 0)),
            pl.BlockSpec((1, tm // WINDOW, SWA_VT_ROWS, WINDOW), lambda b, i: (b, i, 0, 0)),
        ),
        out_shape=out_shape,
        compiler_params=pltpu.CompilerParams(
            dimension_semantics=("parallel", "parallel"), vmem_limit_bytes=48 * 1024 * 1024),
        name="proj_rope",
    )(x, g1, w_nat, w_tr, cos_l, sin_l, cos_t, sin_t)


def _diff_kernel(lam_ref, qt_ref, k_ref, vt_ref, g_ref, o_ref, *scratch, tq, tk, lambda_init):
    i = pl.program_id(2)
    s_bufs = scratch[:DIFF_S_BUFS]
    top_bufs = scratch[DIFF_S_BUFS:2 * DIFF_S_BUFS]
    m_ref, acc_ref, bias_ref = scratch[2 * DIFF_S_BUFS:]

    @pl.when(i == 0)
    def _():
        r = lax.broadcasted_iota(jnp.int32, (tk, 2 * tq), 0)
        c = lax.broadcasted_iota(jnp.int32, (tk, 2 * tq), 1) & (tq - 1)
        bias_ref[...] = jnp.where(r <= c, 0.0, NEG).astype(F32)
    qt = qt_ref[0, 0]
    z = jnp.zeros((HEAD_DIM, tq), BF16)
    qw = jnp.concatenate([jnp.concatenate([qt[:HEAD_DIM], z], axis=1),
                          jnp.concatenate([z, qt[HEAD_DIM:]], axis=1)], axis=0)

    def scores(j, par):
        kt = k_ref[0, 0, pl.ds(pl.multiple_of(j * tk, tk), tk), :]
        s = jnp.dot(kt, qw, preferred_element_type=F32)
        s_bufs[par][...] = s
        top_bufs[par][...] = jnp.max(s, axis=0, keepdims=True)

    def absorb(j, par, masked):
        s = s_bufs[par][...]
        if masked:
            s = s + bias_ref[...]
            top = jnp.max(s, axis=0, keepdims=True)
        else:
            top = top_bufs[par][...]
        m = m_ref[...]
        m_new = jnp.maximum(m, top)
        alpha = jnp.exp2(m - m_new)
        p = jnp.exp2(s - m_new).astype(BF16)
        m_ref[...] = m_new
        pv = jnp.dot(vt_ref[0, 0, j], p, preferred_element_type=F32)
        acc_ref[...] = alpha * acc_ref[...] + pv

    m_ref[...] = jnp.full(m_ref.shape, NEG, F32)
    acc_ref[...] = jnp.zeros(acc_ref.shape, F32)

    nfull = (i * tq) // tk
    scores(nfull, 0)
    scores(0, 1)
    absorb(nfull, 0, True)

    def group(t, c):
        j = DIFF_UNROLL * t
        for idx in range(DIFF_UNROLL):
            scores(j + idx + 1, (idx + 2) % DIFF_S_BUFS)
            absorb(j + idx, (idx + 1) % DIFF_S_BUFS, False)
        return c

    lax.fori_loop(0, nfull // DIFF_UNROLL, group, 0)

    for rem in range(1, DIFF_UNROLL):
        @pl.when(nfull % DIFF_UNROLL == rem)
        def _():
            first = nfull - rem
            for idx in range(rem):
                if idx + 1 < rem:
                    scores(first + idx + 1, (idx + 2) % DIFF_S_BUFS)
                absorb(first + idx, (idx + 1) % DIFF_S_BUFS, False)

    lam_p = lam_ref[...]
    lam = (jnp.exp(jnp.sum(lam_p[0:1] * lam_p[1:2], axis=-1, keepdims=True))
           - jnp.exp(jnp.sum(lam_p[2:3] * lam_p[3:4], axis=-1, keepdims=True)) + lambda_init)
    inv_l = 1.0 / acc_ref[DIFF_V_DIM:DIFF_V_DIM + 1, :]
    o = (acc_ref[:DIFF_V_DIM, :tq] * inv_l[:, :tq]
         - lam * (acc_ref[:DIFF_V_DIM, tq:] * inv_l[:, tq:]))
    o = o * lax.rsqrt(jnp.mean(o * o, axis=0, keepdims=True) + EPS)
    o_ref[0] = (o.T * g_ref[...] * (1.0 - lambda_init)).astype(BF16)


def _diff_call(lam_p, dqt, dk, dvt, subln_g, *, tq, tk, lambda_init):
    B, H, _, S = dqt.shape
    assert tk == tq and S % tk == 0, "the diagonal tile's causal pattern is built for square tiles"
    nkv = S // tk
    grid = (B, H, S // tq)
    return pl.pallas_call(
        functools.partial(_diff_kernel, tq=tq, tk=tk, lambda_init=lambda_init),
        grid=grid,
        in_specs=[
            pl.BlockSpec(lam_p.shape, lambda b, h, i: (0, 0)),
            pl.BlockSpec((1, 1, 2 * HEAD_DIM, tq), lambda b, h, i: (b, h, 0, i)),
            pl.BlockSpec((1, 1, S, 2 * HEAD_DIM), lambda b, h, i: (b, h, 0, 0)),
            pl.BlockSpec((1, 1, nkv, VT_ROWS, tk), lambda b, h, i: (b, h, 0, 0, 0)),
            pl.BlockSpec((1, DIFF_V_DIM), lambda b, h, i: (0, 0)),
        ],
        out_specs=pl.BlockSpec((1, tq, DIFF_V_DIM), lambda b, h, i: (b, i, h)),
        out_shape=jax.ShapeDtypeStruct((B, S, DIFF_V_COLS), BF16),
        scratch_shapes=[pltpu.VMEM((tk, 2 * tq), F32)] * DIFF_S_BUFS + [
            pltpu.VMEM((1, 2 * tq), F32)] * DIFF_S_BUFS + [
            pltpu.VMEM((1, 2 * tq), F32),
            pltpu.VMEM((VT_ROWS, 2 * tq), F32),
            pltpu.VMEM((tk, 2 * tq), F32),
        ],
        compiler_params=pltpu.CompilerParams(
            dimension_semantics=("parallel", "parallel", "arbitrary"),
            vmem_limit_bytes=48 * 1024 * 1024),
        name="diff_attn",
    )(lam_p, dqt, dk, dvt, subln_g)


def _swa_kernel(sink_ref, qt_ref, k_ref, vt_ref, o_ref, *, tq):
    i = pl.program_id(1)
    n_cols = SWA_Q_HEADS * WINDOW
    half_cols = n_cols // SWA_KV_HEADS
    sink = sink_ref[...]
    row = lax.broadcasted_iota(jnp.int32, (2 * WINDOW, WINDOW), 0)
    qrel = lax.broadcasted_iota(jnp.int32, (2 * WINDOW, WINDOW), 1)
    band = (row - WINDOW <= qrel) & (row > qrel)
    in_current = row >= WINDOW
    z = jnp.zeros((HEAD_DIM, half_cols), BF16)
    for sub in range(tq // WINDOW):
        n = i * (tq // WINDOW) + sub
        prev = jnp.maximum(n - 1, 0)
        kwin = jnp.concatenate([k_ref[0, prev], k_ref[0, n]], axis=0)
        vtwin = jnp.concatenate([vt_ref[0, prev], vt_ref[0, n]], axis=1)
        qt = qt_ref[0, :, sub * WINDOW:(sub + 1) * WINDOW]
        heads = [qt[h * HEAD_DIM:(h + 1) * HEAD_DIM] for h in range(SWA_Q_HEADS)]
        qw = jnp.concatenate(
            [jnp.concatenate(heads[:SWA_GROUP] + [z], axis=1),
             jnp.concatenate([z] + heads[SWA_GROUP:], axis=1)], axis=0)
        s = jnp.dot(kwin, qw, preferred_element_type=F32)
        valid = band & (in_current | (n >= 1))
        s = jnp.concatenate(
            [jnp.where(valid, s[:, h * WINDOW:(h + 1) * WINDOW], NEG) for h in range(SWA_Q_HEADS)], axis=1)
        m = jnp.maximum(jnp.max(s, axis=0, keepdims=True), sink)
        p = jnp.exp2(s - m).astype(BF16)
        acc = jnp.dot(vtwin, p, preferred_element_type=F32)
        den = acc[SWA_KV_COLS:SWA_KV_COLS + 1] + jnp.exp2(sink - m)
        on = acc[:SWA_KV_COLS] / den
        u = jnp.concatenate([on[:HEAD_DIM, :half_cols], on[HEAD_DIM:, half_cols:]], axis=1)
        for hp in range(SWA_Q_HEADS // 2):
            two = jnp.concatenate([u[:, (2 * hp) * WINDOW:(2 * hp + 1) * WINDOW],
                                   u[:, (2 * hp + 1) * WINDOW:(2 * hp + 2) * WINDOW]], axis=0)
            o_ref[0, sub * WINDOW:(sub + 1) * WINDOW, hp * LANES:(hp + 1) * LANES] = two.T.astype(BF16)


def _swa_call(sink_row, sqt, sk, svt, *, tq):
    B, _, S = sqt.shape
    nb = S // WINDOW
    return pl.pallas_call(
        functools.partial(_swa_kernel, tq=tq),
        grid=(B, S // tq),
        in_specs=[
            pl.BlockSpec(sink_row.shape, lambda b, i: (0, 0)),
            pl.BlockSpec((1, SWA_Q_COLS, tq), lambda b, i: (b, 0, i)),
            pl.BlockSpec((1, nb, WINDOW, SWA_KV_COLS), lambda b, i: (b, 0, 0, 0)),
            pl.BlockSpec((1, nb, SWA_VT_ROWS, WINDOW), lambda b, i: (b, 0, 0, 0)),
        ],
        out_specs=pl.BlockSpec((1, tq, SWA_Q_COLS), lambda b, i: (b, i, 0)),
        out_shape=jax.ShapeDtypeStruct((B, S, SWA_Q_COLS), BF16),
        compiler_params=pltpu.CompilerParams(
            dimension_semantics=("parallel", "arbitrary"), vmem_limit_bytes=40 * 1024 * 1024),
        name="swa_attn",
    )(sink_row, sqt, sk, svt)


def _pack_bf16_pairs(x):
    n = x.shape[1] // 2
    lo = lax.bitcast_convert_type(x[:, :n].astype(BF16).astype(F32), jnp.uint32)
    hi = lax.bitcast_convert_type(x[:, n:].astype(BF16).astype(F32), jnp.uint32)
    return (lo >> 16) | (hi & jnp.uint32(0xFFFF0000))


def _unpack_bf16_pairs(w):
    lo = lax.bitcast_convert_type(w << 16, F32)
    hi = lax.bitcast_convert_type(w & jnp.uint32(0xFFFF0000), F32)
    return jnp.concatenate([lo, hi], axis=1).astype(BF16)


MIX_CHUNKS = 1


def _mix_kernel(x_ref, od_ref, os_ref, wo_ref, g2_ref, wr_ref, br_ref, x1_ref, n2_ref, rt_ref, cnt_ref):
    tm = x_ref.shape[1] // MIX_CHUNKS
    lane = lax.broadcasted_iota(jnp.int32, (tm, ROUTER_COLS), 1)
    big = jnp.int32(ROUTER_COLS)
    counts = jnp.zeros((1, ROUTER_COLS), F32)
    x1_ref[0] = (x_ref[0]
                 + jnp.dot(od_ref[0], wo_ref[:DIFF_V_COLS], preferred_element_type=F32)
                 + jnp.dot(os_ref[0], wo_ref[DIFF_V_COLS:], preferred_element_type=F32))
    for c in range(MIX_CHUNKS):
        rows = pl.ds(c * tm, tm)
        h = x1_ref[0, rows, :]
        n2 = h * lax.rsqrt(jnp.mean(h * h, axis=-1, keepdims=True) + EPS) * g2_ref[...]
        n2_ref[0, rows, :] = _pack_bf16_pairs(n2)
        n2_hi = n2.astype(BF16)
        n2_lo = (n2 - n2_hi.astype(F32)).astype(BF16)
        parts = jnp.dot(jnp.concatenate([n2_hi, n2_lo], axis=0), wr_ref[...],
                        preferred_element_type=F32)
        logits = ((parts[:tm, :ROUTER_COLS] + parts[tm:, ROUTER_COLS:])
                  + (parts[:tm, ROUTER_COLS:] + parts[tm:, :ROUTER_COLS])) + br_ref[...]
        gl = jnp.where(lane < N_GROUPS, logits, -jnp.inf)
        gm = jnp.max(gl, axis=-1, keepdims=True)
        p_top = 1.0 / jnp.sum(jnp.exp(gl - gm), axis=-1, keepdims=True)
        g_idx = jnp.min(jnp.where(gl == gm, lane, big), axis=-1, keepdims=True)
        e_lo = N_GROUPS + EXPERTS_PER_GROUP * g_idx
        el = jnp.where((lane >= e_lo) & (lane < e_lo + EXPERTS_PER_GROUP), logits, -jnp.inf)
        v1 = jnp.max(el, axis=-1, keepdims=True)
        i1 = jnp.min(jnp.where(el == v1, lane, big), axis=-1, keepdims=True)
        el2 = jnp.where(lane == i1, -jnp.inf, el)
        v2 = jnp.max(el2, axis=-1, keepdims=True)
        i2 = jnp.min(jnp.where(el2 == v2, lane, big), axis=-1, keepdims=True)
        e21 = jnp.exp(v2 - v1)
        gate1 = p_top / (1.0 + e21)
        gate2 = p_top * e21 / (1.0 + e21)
        rt_ref[0, rows, :] = jnp.where(lane == 0, (i1 - N_GROUPS).astype(F32),
                             jnp.where(lane == 1, (i2 - N_GROUPS).astype(F32),
                             jnp.where(lane == 2, gate1, jnp.where(lane == 3, gate2, 0.0))))
        chosen = ((lane == i1 - N_GROUPS) | (lane == i2 - N_GROUPS)).astype(F32)
        counts = counts + jnp.sum(chosen, axis=0, keepdims=True)
    cnt_ref[0, 0] = jnp.broadcast_to(counts, cnt_ref.shape[2:])


def _mix_call(x, o_diff, o_swa, w_out, g2, w_router, b_router, *, tm):
    B, S, D = x.shape
    const = lambda b, i: (0, 0)
    row = lambda b, i: (b, i, 0)
    nt = S // tm
    return pl.pallas_call(
        _mix_kernel,
        grid=(B, nt),
        in_specs=[
            pl.BlockSpec((1, tm, D), row),
            pl.BlockSpec((1, tm, DIFF_V_COLS), row),
            pl.BlockSpec((1, tm, SWA_Q_COLS), row),
            pl.BlockSpec(w_out.shape, const),
            pl.BlockSpec((1, D), const),
            pl.BlockSpec(w_router.shape, const),
            pl.BlockSpec((1, ROUTER_COLS), const),
        ],
        out_specs=(pl.BlockSpec((1, tm, D), row), pl.BlockSpec((1, tm, D // 2), row),
                   pl.BlockSpec((1, tm, ROUTER_COLS), row),
                   pl.BlockSpec((1, 1, SUBLANES, ROUTER_COLS), lambda b, i: (b, i, 0, 0))),
        out_shape=(jax.ShapeDtypeStruct((B, S, D), F32), jax.ShapeDtypeStruct((B, S, D // 2), jnp.uint32),
                   jax.ShapeDtypeStruct((B, S, ROUTER_COLS), F32),
                   jax.ShapeDtypeStruct((B, nt, SUBLANES, ROUTER_COLS), F32)),
        compiler_params=pltpu.CompilerParams(
            dimension_semantics=("parallel", "parallel"), vmem_limit_bytes=48 * 1024 * 1024),
        name="outproj_router",
    )(x, o_diff, o_swa, w_out, g2, w_router, b_router)


def _slot_kernel(rt_ref, base_ref, dest_ref, *, tm):
    rt_t = rt_ref[...].T
    e1 = rt_t[0:1].astype(jnp.int32)
    e2 = rt_t[1:2].astype(jnp.int32)
    eid = lax.broadcasted_iota(jnp.int32, (N_EXPERTS, tm), 0)
    oh1 = eid == e1
    oh2 = eid == e2
    earlier = (lax.broadcasted_iota(jnp.int32, (tm, tm), 0)
               < lax.broadcasted_iota(jnp.int32, (tm, tm), 1)).astype(BF16)
    before = jnp.dot((oh1 | oh2).astype(BF16), earlier, preferred_element_type=F32)
    slot = before + base_ref[0][:, 0:1]
    d1 = jnp.sum(jnp.where(oh1, slot, 0.0), axis=0, keepdims=True).astype(jnp.int32)
    d2 = jnp.sum(jnp.where(oh2, slot, 0.0), axis=0, keepdims=True).astype(jnp.int32)
    dest_ref[0] = jnp.concatenate([d1, d2, jnp.zeros((SUBLANES - TOP_K, tm), jnp.int32)], axis=0)


def _slot_call(rt, tile_base, *, tm):
    nt = rt.shape[0] // tm
    return pl.pallas_call(
        functools.partial(_slot_kernel, tm=tm),
        grid=(nt,),
        in_specs=[
            pl.BlockSpec((tm, ROUTER_COLS), lambda t: (t, 0)),
            pl.BlockSpec((1, N_EXPERTS, LANES), lambda t: (t, 0, 0)),
        ],
        out_specs=pl.BlockSpec((1, SUBLANES, tm), lambda t: (t, 0, 0)),
        out_shape=jax.ShapeDtypeStruct((nt, SUBLANES, tm), jnp.int32),
        compiler_params=pltpu.CompilerParams(dimension_semantics=("parallel",)),
        name="moe_slots",
    )(rt, tile_base)


SC_ROW_CHUNK = 64


def _sc_workers():
    info = plsc.get_sparse_core_info()
    return info.num_cores, info.num_cores * info.num_subcores


def _sc_scatter_rows(rows, idx, n_out):
    n, width = rows.shape
    n_cores, n_workers = _sc_workers()
    n_chunks = n // SC_ROW_CHUNK
    per_worker = n_chunks // n_workers
    assert n_chunks % n_workers == 0
    mesh = plsc.VectorSubcoreMesh(core_axis_name="c", subcore_axis_name="s")

    @functools.partial(
        pl.kernel, mesh=mesh,
        out_type=jax.ShapeDtypeStruct((n_out, width), rows.dtype),
        scratch_types=[
            pltpu.VMEM((SC_ROW_CHUNK,), jnp.int32),
            pltpu.VMEM((SC_ROW_CHUNK, width), rows.dtype),
        ],
    )
    def scatter(rows_hbm, idx_hbm, out_hbm, idx_v, rows_v):
        worker = lax.axis_index("s") * n_cores + lax.axis_index("c")

        @pl.loop(0, per_worker)
        def _(i):
            c = worker * per_worker + i
            pltpu.sync_copy(rows_hbm.at[pl.ds(pl.multiple_of(c * SC_ROW_CHUNK, SC_ROW_CHUNK), SC_ROW_CHUNK)], rows_v)
            for k in range(TOP_K):
                pltpu.sync_copy(idx_hbm.at[k, c], idx_v)
                pltpu.sync_copy(rows_v, out_hbm.at[idx_v])

    return scatter(rows, idx)


def _expert_kernel(be_ref, nvalid_ref, xs_ref, wg_ref, wu_ref, wd_ref, y_ref, wg_b, wu_b, wd_b):
    b = pl.program_id(0)
    n_valid = nvalid_ref[b]

    @pl.when(n_valid > 0)
    def _():
        @pl.when((b == 0) | (be_ref[b] != be_ref[jnp.maximum(b - 1, 0)]))
        def _():
            wg_b[...] = wg_ref[0].astype(BF16)
            wu_b[...] = wu_ref[0].astype(BF16)
            wd_b[...] = wd_ref[0].astype(BF16)

        for c in range(EXPERT_BLOCK // EXPERT_CHUNK):
            rows = pl.ds(c * EXPERT_CHUNK, EXPERT_CHUNK)
            row_id = c * EXPERT_CHUNK + lax.broadcasted_iota(jnp.int32, (EXPERT_CHUNK, xs_ref.shape[1]), 0)
            packed = jnp.where(row_id < n_valid, xs_ref[rows, :], jnp.uint32(0))
            xb = _unpack_bf16_pairs(packed)
            gate = jnp.dot(xb, wg_b[...], preferred_element_type=F32)
            up = jnp.dot(xb, wu_b[...], preferred_element_type=F32)
            hid = (gate * jax.nn.sigmoid(gate) * up).astype(BF16)
            y_ref[rows, :] = _pack_bf16_pairs(jnp.dot(hid, wd_b[...], preferred_element_type=F32))

    @pl.when(n_valid == 0)
    def _():
        y_ref[...] = jnp.zeros_like(y_ref)


def _expert_call(block_expert, n_valid, xs, w_gate, w_up, w_down):
    P = xs.shape[0]
    NB = P // EXPERT_BLOCK
    E, D, F = w_gate.shape
    grid_spec = pltpu.PrefetchScalarGridSpec(
        num_scalar_prefetch=2,
        grid=(NB,),
        in_specs=[
            pl.BlockSpec((EXPERT_BLOCK,) + xs.shape[1:], lambda b, be, nu: (b, 0)),
            pl.BlockSpec((1, D, F), lambda b, be, nu: (be[b], 0, 0)),
            pl.BlockSpec((1, D, F), lambda b, be, nu: (be[b], 0, 0)),
            pl.BlockSpec((1, F, D), lambda b, be, nu: (be[b], 0, 0)),
        ],
        out_specs=pl.BlockSpec((EXPERT_BLOCK, D // 2), lambda b, be, nu: (b, 0)),
        scratch_shapes=[
            pltpu.VMEM((D, F), BF16),
            pltpu.VMEM((D, F), BF16),
            pltpu.VMEM((F, D), BF16),
        ],
    )
    return pl.pallas_call(
        _expert_kernel,
        grid_spec=grid_spec,
        out_shape=jax.ShapeDtypeStruct((P, D // 2), jnp.uint32),
        compiler_params=pltpu.CompilerParams(
            dimension_semantics=("arbitrary",), vmem_limit_bytes=48 * 1024 * 1024),
        name="moe_experts",
    )(block_expert, n_valid, xs, w_gate, w_up, w_down)


def _sc_gather_rows(table, idx):
    n_rows, width = idx.shape[0], table.shape[1]
    n_cores, n_workers = _sc_workers()
    per_worker = n_rows // n_workers
    assert n_rows % (n_workers * 2 * SC_ROW_CHUNK) == 0
    mesh = plsc.VectorSubcoreMesh(core_axis_name="c", subcore_axis_name="s")

    @functools.partial(
        pl.kernel, mesh=mesh,
        out_type=jax.ShapeDtypeStruct((n_rows, width), table.dtype),
        scratch_types=[pltpu.VMEM((SC_ROW_CHUNK,), jnp.int32)] * 2
        + [pltpu.VMEM((SC_ROW_CHUNK, width), table.dtype)] * 2
        + [pltpu.SemaphoreType.DMA] * 4,
    )
    def gather(table_hbm, idx_hbm, out_hbm, idx_a, idx_b, rows_a, rows_b, sem_ga, sem_gb, sem_wa, sem_wb):
        worker = lax.axis_index("s") * n_cores + lax.axis_index("c")
        base = worker * per_worker

        @pl.loop(0, per_worker // SC_ROW_CHUNK, step=2)
        def _(c):
            off_a = pl.multiple_of(base + c * SC_ROW_CHUNK, SC_ROW_CHUNK)
            off_b = pl.multiple_of(off_a + SC_ROW_CHUNK, SC_ROW_CHUNK)
            pltpu.sync_copy(idx_hbm.at[pl.ds(off_a, SC_ROW_CHUNK)], idx_a)
            pltpu.sync_copy(idx_hbm.at[pl.ds(off_b, SC_ROW_CHUNK)], idx_b)
            gather_a = pltpu.async_copy(table_hbm.at[idx_a], rows_a, sem_ga)
            gather_b = pltpu.async_copy(table_hbm.at[idx_b], rows_b, sem_gb)
            gather_a.wait()
            write_a = pltpu.async_copy(rows_a, out_hbm.at[pl.ds(off_a, SC_ROW_CHUNK)], sem_wa)
            gather_b.wait()
            write_b = pltpu.async_copy(rows_b, out_hbm.at[pl.ds(off_b, SC_ROW_CHUNK)], sem_wb)
            write_a.wait()
            write_b.wait()

    return gather(table, idx)


def _combine_kernel(x1_ref, rt_ref, y_ref, fg_ref, o_ref, *, final_norm):
    rt = rt_ref[...]
    y1 = _unpack_bf16_pairs(y_ref[0, 0]).astype(F32)
    y2 = _unpack_bf16_pairs(y_ref[0, 1]).astype(F32)
    h = x1_ref[...] + rt[:, 2:3] * y1 + rt[:, 3:4] * y2
    if final_norm:
        h = h * lax.rsqrt(jnp.mean(h * h, axis=-1, keepdims=True) + EPS) * fg_ref[...]
    o_ref[...] = h


def _combine_call(x1, rt, ysg, final_g, *, tm, final_norm):
    T, D = x1.shape
    return pl.pallas_call(
        functools.partial(_combine_kernel, final_norm=final_norm),
        grid=(T // tm,),
        in_specs=[
            pl.BlockSpec((tm, D), lambda t: (t, 0)),
            pl.BlockSpec((tm, ROUTER_COLS), lambda t: (t, 0)),
            pl.BlockSpec((1, TOP_K, tm, D // 2), lambda t: (t, 0, 0, 0)),
            pl.BlockSpec((1, D), lambda t: (0, 0)),
        ],
        out_specs=pl.BlockSpec((tm, D), lambda t: (t, 0)),
        out_shape=jax.ShapeDtypeStruct((T, D), F32),
        compiler_params=pltpu.CompilerParams(
            dimension_semantics=("parallel",), vmem_limit_bytes=40 * 1024 * 1024),
        name="moe_combine",
    )(x1, rt, ysg, final_g)


def _slot_layout(tile_counts, n_assign):
    NB = -(-n_assign // EXPERT_BLOCK) + N_EXPERTS
    n_tiles = tile_counts.shape[0]
    tc = tile_counts.astype(F32)
    hp = lax.Precision.HIGHEST
    counts = jnp.sum(tc, axis=0)
    padded = jnp.ceil(counts / EXPERT_BLOCK) * EXPERT_BLOCK
    upper = (jnp.arange(N_EXPERTS)[:, None] < jnp.arange(N_EXPERTS)[None, :]).astype(F32)
    pad_start = jnp.dot(padded, upper, precision=hp)
    pad_end = pad_start + padded
    lower = (jnp.arange(n_tiles)[:, None] > jnp.arange(n_tiles)[None, :]).astype(F32)
    tile_base = pad_start[None, :] + jnp.dot(lower, tc, precision=hp)
    block_start = jnp.arange(NB, dtype=F32) * EXPERT_BLOCK
    block_expert = jnp.minimum(jnp.sum((pad_end[None, :] <= block_start[:, None]).astype(jnp.int32), axis=1),
                               N_EXPERTS - 1)
    mine = block_expert[:, None] == jnp.arange(N_EXPERTS)[None, :]
    run_end = jnp.sum(jnp.where(mine, (pad_start + counts)[None, :], 0.0), axis=1)
    n_valid = jnp.clip(run_end - block_start, 0, EXPERT_BLOCK).astype(jnp.int32)
    return NB, block_expert.astype(jnp.int32), n_valid, tile_base


def _rope_tables(S):
    half = HEAD_DIM // 2
    inv = 1.0 / (ROPE_THETA ** (jnp.arange(0, HEAD_DIM, 2, dtype=F32) / HEAD_DIM))
    pos = jnp.arange(S, dtype=F32)
    ang_l = pos[:, None] * jnp.tile(inv, LANES // half)[None, :]
    sign = jnp.tile(jnp.concatenate([-jnp.ones((half,), F32), jnp.ones((half,), F32)]), LANES // HEAD_DIM)
    ang_t = inv[:, None] * pos[None, :]
    return jnp.cos(ang_l), jnp.sin(ang_l) * sign[None, :], jnp.cos(ang_t), jnp.sin(ang_t)


def kernel(x, norm1_g, w_in, lambda_q1, lambda_k1, lambda_q2, lambda_k2, subln_g, sinks, w_out,
           norm2_g, w_router_group, b_router_group, w_router_expert, b_router_expert,
           w_gate, w_up, w_down, final_g):
    B, S, D = x.shape
    T = B * S
    depth = w_in.shape[0]
    tq, tk = 512, 512
    tm_proj = 512
    tm_tok = 512
    tq_swa = 1024
    qscale = HEAD_DIM ** -0.5 * math.log2(math.e)
    cos_l, sin_l, cos_t, sin_t = _rope_tables(S)

    c0 = DIFF_QK_COLS
    c1 = 2 * DIFF_QK_COLS
    c2 = c1 + DIFF_V_COLS
    c3 = c2 + SWA_Q_COLS
    c4 = c3 + SWA_KV_COLS
    for l in range(depth):
        lambda_init = 0.8 - 0.6 * math.exp(-0.3 * l)
        w = w_in[l]
        w_nat = jnp.concatenate([w[:, c0:c1], w[:, c3:c4]], axis=1).astype(BF16)
        w_tr = jnp.concatenate([w[:, :c0] * qscale, w[:, c1:c2], w[:, c2:c3] * qscale, w[:, c4:]],
                               axis=1).T.astype(BF16)
        dqt, dk, dvt, sqt, sk, svt = _proj_call(
            x, norm1_g[l][None, :], w_nat, w_tr, cos_l, sin_l, cos_t, sin_t, tm=tm_proj, tk=tk)

        lam_p = jnp.stack([lambda_q1[l], lambda_k1[l], lambda_q2[l], lambda_k2[l]]).astype(F32)
        o_diff = _diff_call(lam_p, dqt, dk, dvt, subln_g[l][None, :].astype(F32),
                            tq=tq, tk=tk, lambda_init=lambda_init)
        sink_row = jnp.repeat(sinks[l].astype(F32) * math.log2(math.e), WINDOW)[None, :]
        o_swa = _swa_call(sink_row, sqt, sk, svt, tq=tq_swa)

        wo_b = w_out[l].astype(BF16)
        w_router = jnp.zeros((D, ROUTER_COLS), F32)
        w_router = w_router.at[:, :N_GROUPS].set(w_router_group[l])
        w_router = w_router.at[:, N_GROUPS:N_GROUPS + N_EXPERTS].set(w_router_expert[l])
        w_router_hi = w_router.astype(BF16)
        w_router_lo = (w_router - w_router_hi.astype(F32)).astype(BF16)
        w_router = jnp.concatenate([w_router_hi, w_router_lo], axis=1)
        b_router = jnp.zeros((1, ROUTER_COLS), F32)
        b_router = b_router.at[0, :N_GROUPS].set(b_router_group[l])
        b_router = b_router.at[0, N_GROUPS:N_GROUPS + N_EXPERTS].set(b_router_expert[l])
        x1, n2p, rt, cnt = _mix_call(x, o_diff, o_swa, wo_b, norm2_g[l][None, :], w_router, b_router, tm=tm_tok)

        rt2 = rt.reshape(T, ROUTER_COLS)
        tile_counts = cnt[:, :, 0, :N_EXPERTS].reshape(T // tm_tok, N_EXPERTS).astype(jnp.int32)
        NB, block_expert, n_valid, tile_base = _slot_layout(tile_counts, T * TOP_K)
        tile_base = jnp.broadcast_to(tile_base.astype(F32)[:, :, None], (T // tm_tok, N_EXPERTS, LANES))
        dest = _slot_call(rt2, tile_base, tm=tm_tok)
        scatter_idx = jnp.swapaxes(dest[:, :TOP_K, :], 0, 1).reshape(TOP_K, T // SC_ROW_CHUNK, SC_ROW_CHUNK)
        xs = _sc_scatter_rows(n2p.reshape(T, D // 2), scatter_idx, NB * EXPERT_BLOCK)
        ys = _expert_call(block_expert, n_valid, xs, w_gate[l], w_up[l], w_down[l])
        ysg = _sc_gather_rows(ys, dest[:, :TOP_K, :].reshape(T * TOP_K))
        x = _combine_call(x1.reshape(T, D), rt2, ysg.reshape(T // tm_tok, TOP_K, tm_tok, D // 2),
                          final_g[None, :], tm=tm_tok, final_norm=(l == depth - 1)).reshape(B, S, D)
    return x
```

```python
import functools
import math

import jax
import jax.numpy as jnp
from jax import lax
from jax.experimental import pallas as pl
from jax.experimental.pallas import tpu as pltpu
from jax.experimental.pallas import tpu_sc as plsc

HEAD_DIM = 64
DIFF_HEADS = 4
DIFF_V_DIM = 2 * HEAD_DIM
SWA_Q_HEADS = 8
SWA_KV_HEADS = 2
SWA_GROUP = SWA_Q_HEADS // SWA_KV_HEADS
WINDOW = 128
ROPE_THETA = 10000.0
N_GROUPS = 4
EXPERTS_PER_GROUP = 8
N_EXPERTS = N_GROUPS * EXPERTS_PER_GROUP
TOP_K = 2
EXPERT_BLOCK = 512
EXPERT_CHUNK = 256
EPS = 1e-6
NEG = -1e30

DIFF_QK_COLS = DIFF_HEADS * 2 * HEAD_DIM
DIFF_V_COLS = DIFF_HEADS * DIFF_V_DIM
SWA_Q_COLS = SWA_Q_HEADS * HEAD_DIM
SWA_KV_COLS = SWA_KV_HEADS * HEAD_DIM
LANES = 128
SUBLANES = 8
BF16_SUBLANES = 16
VT_ROWS = DIFF_V_DIM + BF16_SUBLANES
SWA_VT_ROWS = SWA_KV_COLS + BF16_SUBLANES
ROUTER_COLS = LANES
DIFF_UNROLL = 4
DIFF_S_BUFS = 4
DIFF_Q_TILES = 2

BF16 = jnp.bfloat16
F32 = jnp.float32


def _rope_lanes(x, cos_l, sin_l, first_half):
    rot = jnp.where(first_half, pltpu.roll(x, 96, 1), pltpu.roll(x, 32, 1))
    return x * cos_l + rot * sin_l


def _proj_kernel(x_ref, g_ref, wnat_ref, wtr_ref, cosl_ref, sinl_ref, cost_ref, sint_ref,
                 dqt_ref, dk_ref, dvt_ref, sqt_ref, sk_ref, svt_ref, *, tk):
    x = x_ref[0]
    tm = x.shape[0]
    n1 = x * lax.rsqrt(jnp.mean(x * x, axis=-1, keepdims=True) + EPS) * g_ref[...]
    n1b = n1.astype(BF16)
    nat = jnp.dot(n1b, wnat_ref[...], preferred_element_type=F32)
    tr = lax.dot_general(wtr_ref[...], n1b, (((1,), (1,)), ((), ())),
                         preferred_element_type=F32)

    cos_l, sin_l = cosl_ref[...], sinl_ref[...]
    first_half = (lax.broadcasted_iota(jnp.int32, (tm, LANES), 1) & (HEAD_DIM - 1)) < HEAD_DIM // 2
    for h in range(DIFF_HEADS):
        slab = nat[:, h * LANES:(h + 1) * LANES]
        dk_ref[0, h] = _rope_lanes(slab, cos_l, sin_l, first_half).astype(BF16)
    sk = _rope_lanes(nat[:, DIFF_QK_COLS:DIFF_QK_COLS + LANES], cos_l, sin_l, first_half).astype(BF16)
    for c in range(tm // WINDOW):
        sk_ref[0, c] = sk[c * WINDOW:(c + 1) * WINDOW]

    cos_t, sin_t = cost_ref[...], sint_ref[...]
    half = HEAD_DIM // 2

    def rope_rows(r0):
        x1 = tr[r0:r0 + half]
        x2 = tr[r0 + half:r0 + HEAD_DIM]
        return (x1 * cos_t - x2 * sin_t).astype(BF16), (x1 * sin_t + x2 * cos_t).astype(BF16)

    for h in range(DIFF_HEADS):
        for c in range(2):
            lo, hi = rope_rows(h * 2 * HEAD_DIM + c * HEAD_DIM)
            dqt_ref[0, h, c * HEAD_DIM:c * HEAD_DIM + half] = lo
            dqt_ref[0, h, c * HEAD_DIM + half:(c + 1) * HEAD_DIM] = hi
    ones_rows = (lax.broadcasted_iota(jnp.int32, (BF16_SUBLANES, tk), 0) == 0).astype(BF16)
    for h in range(DIFF_HEADS):
        r0 = DIFF_QK_COLS + h * DIFF_V_DIM
        for c in range(tm // tk):
            dvt_ref[0, h, c, :DIFF_V_DIM] = tr[r0:r0 + DIFF_V_DIM, c * tk:(c + 1) * tk].astype(BF16)
            dvt_ref[0, h, c, DIFF_V_DIM:] = ones_rows

    r0 = DIFF_QK_COLS + DIFF_V_COLS
    for h in range(SWA_Q_HEADS):
        lo, hi = rope_rows(r0 + h * HEAD_DIM)
        sqt_ref[0, h * HEAD_DIM:h * HEAD_DIM + half] = lo
        sqt_ref[0, h * HEAD_DIM + half:(h + 1) * HEAD_DIM] = hi
    r0 += SWA_Q_COLS
    for c in range(tm // WINDOW):
        svt_ref[0, c, :SWA_KV_COLS] = tr[r0:r0 + SWA_KV_COLS, c * WINDOW:(c + 1) * WINDOW].astype(BF16)
        svt_ref[0, c, SWA_KV_COLS:] = ones_rows[:, :WINDOW]


def _proj_call(x, g1, w_nat, w_tr, cos_l, sin_l, cos_t, sin_t, *, tm, tk):
    B, S, D = x.shape
    nkv = S // tk
    grid = (B, S // tm)
    const = lambda b, i: (0, 0)
    out_shape = (
        jax.ShapeDtypeStruct((B, DIFF_HEADS, 2 * HEAD_DIM, S), BF16),
        jax.ShapeDtypeStruct((B, DIFF_HEADS, S, 2 * HEAD_DIM), BF16),
        jax.ShapeDtypeStruct((B, DIFF_HEADS, nkv, VT_ROWS, tk), BF16),
        jax.ShapeDtypeStruct((B, SWA_Q_COLS, S), BF16),
        jax.ShapeDtypeStruct((B, S // WINDOW, WINDOW, SWA_KV_COLS), BF16),
        jax.ShapeDtypeStruct((B, S // WINDOW, SWA_VT_ROWS, WINDOW), BF16),
    )
    return pl.pallas_call(
        functools.partial(_proj_kernel, tk=tk),
        grid=grid,
        in_specs=[
            pl.BlockSpec((1, tm, D), lambda b, i: (b, i, 0)),
            pl.BlockSpec((1, D), const),
            pl.BlockSpec(w_nat.shape, const),
            pl.BlockSpec(w_tr.shape, const),
            pl.BlockSpec((tm, LANES), lambda b, i: (i, 0)),
            pl.BlockSpec((tm, LANES), lambda b, i: (i, 0)),
            pl.BlockSpec((HEAD_DIM // 2, tm), lambda b, i: (0, i)),
            pl.BlockSpec((HEAD_DIM // 2, tm), lambda b, i: (0, i)),
        ],
        out_specs=(
            pl.BlockSpec((1, DIFF_HEADS, 2 * HEAD_DIM, tm), lambda b, i: (b, 0, 0, i)),
            pl.BlockSpec((1, DIFF_HEADS, tm, 2 * HEAD_DIM), lambda b, i: (b, 0, i, 0)),
            pl.BlockSpec((1, DIFF_HEADS, tm // tk, VT_ROWS, tk), lambda b, i: (b, 0, i, 0, 0)),
            pl.BlockSpec((1, SWA_Q_COLS, tm), lambda b, i: (b, 0, i)),
            pl.BlockSpec((1, tm // WINDOW, WINDOW, SWA_KV_COLS), lambda b, i: (b, i, 0, 0)),
            pl.BlockSpec((1, tm // WINDOW, SWA_VT_ROWS, WINDOW), lambda b, i: (b, i, 0, 0)),
        ),
        out_shape=out_shape,
        compiler_params=pltpu.CompilerParams(
            dimension_semantics=("parallel", "parallel"), vmem_limit_bytes=48 * 1024 * 1024),
        name="proj_rope",
    )(x, g1, w_nat, w_tr, cos_l, sin_l, cos_t, sin_t)


def _diff_kernel(lam_ref, qt_ref, k_ref, vt_ref, g_ref, o_ref, *scratch, tq, tk, lambda_init):
    step = pl.program_id(2)
    s_bufs = scratch[:DIFF_S_BUFS]
    top_bufs = scratch[DIFF_S_BUFS:2 * DIFF_S_BUFS]
    state = scratch[2 * DIFF_S_BUFS:2 * DIFF_S_BUFS + 2 * DIFF_Q_TILES]
    bias_ref = scratch[-1]

    @pl.when(step == 0)
    def _():
        r = lax.broadcasted_iota(jnp.int32, (tk, 2 * tq), 0)
        c = lax.broadcasted_iota(jnp.int32, (tk, 2 * tq), 1) & (tq - 1)
        bias_ref[...] = jnp.where(r <= c, 0.0, NEG).astype(F32)

    lam_p = lam_ref[...]
    lam = (jnp.exp(jnp.sum(lam_p[0:1] * lam_p[1:2], axis=-1, keepdims=True))
           - jnp.exp(jnp.sum(lam_p[2:3] * lam_p[3:4], axis=-1, keepdims=True)) + lambda_init)

    for sub in range(DIFF_Q_TILES):
        _diff_query_tile(step * DIFF_Q_TILES + sub, qt_ref[0, 0, :, sub * tq:(sub + 1) * tq], k_ref, vt_ref,
                         g_ref, o_ref.at[0, pl.ds(sub * tq, tq), :], s_bufs, top_bufs,
                         state[2 * sub], state[2 * sub + 1], bias_ref, lam,
                         tq=tq, tk=tk, lambda_init=lambda_init)


def _diff_query_tile(i, qt, k_ref, vt_ref, g_ref, o_ref, s_bufs, top_bufs, m_ref, acc_ref, bias_ref, lam,
                     *, tq, tk, lambda_init):
    z = jnp.zeros((HEAD_DIM, tq), BF16)
    qw = jnp.concatenate([jnp.concatenate([qt[:HEAD_DIM], z], axis=1),
                          jnp.concatenate([z, qt[HEAD_DIM:]], axis=1)], axis=0)

    def scores(j, par):
        kt = k_ref[0, 0, pl.ds(pl.multiple_of(j * tk, tk), tk), :]
        s = jnp.dot(kt, qw, preferred_element_type=F32)
        s_bufs[par][...] = s
        top_bufs[par][...] = jnp.max(s, axis=0, keepdims=True)

    def absorb(j, par, masked):
        s = s_bufs[par][...]
        if masked:
            s = s + bias_ref[...]
            top = jnp.max(s, axis=0, keepdims=True)
        else:
            top = top_bufs[par][...]
        m = m_ref[...]
        m_new = jnp.maximum(m, top)
        alpha = jnp.exp2(m - m_new)
        p = jnp.exp2(s - m_new).astype(BF16)
        m_ref[...] = m_new
        pv = jnp.dot(vt_ref[0, 0, j], p, preferred_element_type=F32)
        acc_ref[...] = alpha * acc_ref[...] + pv

    m_ref[...] = jnp.full(m_ref.shape, NEG, F32)
    acc_ref[...] = jnp.zeros(acc_ref.shape, F32)

    nfull = (i * tq) // tk
    scores(nfull, 0)
    scores(0, 1)
    absorb(nfull, 0, True)

    def group(t, c):
        j = DIFF_UNROLL * t
        for idx in range(DIFF_UNROLL):
            scores(j + idx + 1, (idx + 2) % DIFF_S_BUFS)
            absorb(j + idx, (idx + 1) % DIFF_S_BUFS, False)
        return c

    lax.fori_loop(0, nfull // DIFF_UNROLL, group, 0)

    for rem in range(1, DIFF_UNROLL):
        @pl.when(nfull % DIFF_UNROLL == rem)
        def _():
            first = nfull - rem
            for idx in range(rem):
                if idx + 1 < rem:
                    scores(first + idx + 1, (idx + 2) % DIFF_S_BUFS)
                absorb(first + idx, (idx + 1) % DIFF_S_BUFS, False)

    inv_l = 1.0 / acc_ref[DIFF_V_DIM:DIFF_V_DIM + 1, :]
    o = (acc_ref[:DIFF_V_DIM, :tq] * inv_l[:, :tq]
         - lam * (acc_ref[:DIFF_V_DIM, tq:] * inv_l[:, tq:]))
    o = o * lax.rsqrt(jnp.mean(o * o, axis=0, keepdims=True) + EPS)
    o_ref[...] = (o.T * g_ref[...] * (1.0 - lambda_init)).astype(BF16)


def _diff_call(lam_p, dqt, dk, dvt, subln_g, *, tq, tk, lambda_init):
    B, H, _, S = dqt.shape
    assert tk == tq and S % tk == 0, "the diagonal tile's causal pattern is built for square tiles"
    nkv = S // tk
    tq_step = DIFF_Q_TILES * tq
    assert S % tq_step == 0
    grid = (B, H, S // tq_step)
    return pl.pallas_call(
        functools.partial(_diff_kernel, tq=tq, tk=tk, lambda_init=lambda_init),
        grid=grid,
        in_specs=[
            pl.BlockSpec(lam_p.shape, lambda b, h, i: (0, 0)),
            pl.BlockSpec((1, 1, 2 * HEAD_DIM, tq_step), lambda b, h, i: (b, h, 0, i)),
            pl.BlockSpec((1, 1, S, 2 * HEAD_DIM), lambda b, h, i: (b, h, 0, 0)),
            pl.BlockSpec((1, 1, nkv, VT_ROWS, tk), lambda b, h, i: (b, h, 0, 0, 0)),
            pl.BlockSpec((1, DIFF_V_DIM), lambda b, h, i: (0, 0)),
        ],
        out_specs=pl.BlockSpec((1, tq_step, DIFF_V_DIM), lambda b, h, i: (b, i, h)),
        out_shape=jax.ShapeDtypeStruct((B, S, DIFF_V_COLS), BF16),
        scratch_shapes=[pltpu.VMEM((tk, 2 * tq), F32)] * DIFF_S_BUFS + [
            pltpu.VMEM((1, 2 * tq), F32)] * DIFF_S_BUFS + [
            pltpu.VMEM((1, 2 * tq), F32),
            pltpu.VMEM((VT_ROWS, 2 * tq), F32)] * DIFF_Q_TILES + [
            pltpu.VMEM((tk, 2 * tq), F32),
        ],
        compiler_params=pltpu.CompilerParams(
            dimension_semantics=("parallel", "parallel", "arbitrary"),
            vmem_limit_bytes=48 * 1024 * 1024),
        name="diff_attn",
    )(lam_p, dqt, dk, dvt, subln_g)


def _swa_kernel(sink_ref, qt_ref, k_ref, vt_ref, o_ref, *, tq):
    i = pl.program_id(1)
    n_cols = SWA_Q_HEADS * WINDOW
    half_cols = n_cols // SWA_KV_HEADS
    sink = sink_ref[...]
    row = lax.broadcasted_iota(jnp.int32, (2 * WINDOW, WINDOW), 0)
    qrel = lax.broadcasted_iota(jnp.int32, (2 * WINDOW, WINDOW), 1)
    band = (row - WINDOW <= qrel) & (row > qrel)
    in_current = row >= WINDOW
    z = jnp.zeros((HEAD_DIM, half_cols), BF16)
    for sub in range(tq // WINDOW):
        n = i * (tq // WINDOW) + sub
        prev = jnp.maximum(n - 1, 0)
        kwin = jnp.concatenate([k_ref[0, prev], k_ref[0, n]], axis=0)
        vtwin = jnp.concatenate([vt_ref[0, prev], vt_ref[0, n]], axis=1)
        qt = qt_ref[0, :, sub * WINDOW:(sub + 1) * WINDOW]
        heads = [qt[h * HEAD_DIM:(h + 1) * HEAD_DIM] for h in range(SWA_Q_HEADS)]
        qw = jnp.concatenate(
            [jnp.concatenate(heads[:SWA_GROUP] + [z], axis=1),
             jnp.concatenate([z] + heads[SWA_GROUP:], axis=1)], axis=0)
        s = jnp.dot(kwin, qw, preferred_element_type=F32)
        valid = band & (in_current | (n >= 1))
        s = jnp.concatenate(
            [jnp.where(valid, s[:, h * WINDOW:(h + 1) * WINDOW], NEG) for h in range(SWA_Q_HEADS)], axis=1)
        m = jnp.maximum(jnp.max(s, axis=0, keepdims=True), sink)
        p = jnp.exp2(s - m).astype(BF16)
        acc = jnp.dot(vtwin, p, preferred_element_type=F32)
        den = acc[SWA_KV_COLS:SWA_KV_COLS + 1] + jnp.exp2(sink - m)
        on = acc[:SWA_KV_COLS] / den
        u = jnp.concatenate([on[:HEAD_DIM, :half_cols], on[HEAD_DIM:, half_cols:]], axis=1)
        for hp in range(SWA_Q_HEADS // 2):
            two = jnp.concatenate([u[:, (2 * hp) * WINDOW:(2 * hp + 1) * WINDOW],
                                   u[:, (2 * hp + 1) * WINDOW:(2 * hp + 2) * WINDOW]], axis=0)
            o_ref[0, sub * WINDOW:(sub + 1) * WINDOW, hp * LANES:(hp + 1) * LANES] = two.T.astype(BF16)


def _swa_call(sink_row, sqt, sk, svt, *, tq):
    B, _, S = sqt.shape
    nb = S // WINDOW
    return pl.pallas_call(
        functools.partial(_swa_kernel, tq=tq),
        grid=(B, S // tq),
        in_specs=[
            pl.BlockSpec(sink_row.shape, lambda b, i: (0, 0)),
            pl.BlockSpec((1, SWA_Q_COLS, tq), lambda b, i: (b, 0, i)),
            pl.BlockSpec((1, nb, WINDOW, SWA_KV_COLS), lambda b, i: (b, 0, 0, 0)),
            pl.BlockSpec((1, nb, SWA_VT_ROWS, WINDOW), lambda b, i: (b, 0, 0, 0)),
        ],
        out_specs=pl.BlockSpec((1, tq, SWA_Q_COLS), lambda b, i: (b, i, 0)),
        out_shape=jax.ShapeDtypeStruct((B, S, SWA_Q_COLS), BF16),
        compiler_params=pltpu.CompilerParams(
            dimension_semantics=("parallel", "arbitrary"), vmem_limit_bytes=40 * 1024 * 1024),
        name="swa_attn",
    )(sink_row, sqt, sk, svt)


def _pack_bf16_pairs(x):
    n = x.shape[1] // 2
    lo = lax.bitcast_convert_type(x[:, :n].astype(BF16).astype(F32), jnp.uint32)
    hi = lax.bitcast_convert_type(x[:, n:].astype(BF16).astype(F32), jnp.uint32)
    return (lo >> 16) | (hi & jnp.uint32(0xFFFF0000))


def _unpack_bf16_pairs(w):
    lo = lax.bitcast_convert_type(w << 16, F32)
    hi = lax.bitcast_convert_type(w & jnp.uint32(0xFFFF0000), F32)
    return jnp.concatenate([lo, hi], axis=1).astype(BF16)


MIX_CHUNKS = 1


def _mix_kernel(x_ref, od_ref, os_ref, wo_ref, g2_ref, wr_ref, br_ref, x1_ref, n2_ref, rt_ref, cnt_ref):
    tm = x_ref.shape[1] // MIX_CHUNKS
    lane = lax.broadcasted_iota(jnp.int32, (tm, ROUTER_COLS), 1)
    big = jnp.int32(ROUTER_COLS)
    counts = jnp.zeros((1, ROUTER_COLS), F32)
    x1_ref[0] = (x_ref[0]
                 + jnp.dot(od_ref[0], wo_ref[:DIFF_V_COLS], preferred_element_type=F32)
                 + jnp.dot(os_ref[0], wo_ref[DIFF_V_COLS:], preferred_element_type=F32))
    for c in range(MIX_CHUNKS):
        rows = pl.ds(c * tm, tm)
        h = x1_ref[0, rows, :]
        n2 = h * lax.rsqrt(jnp.mean(h * h, axis=-1, keepdims=True) + EPS) * g2_ref[...]
        n2_ref[0, rows, :] = _pack_bf16_pairs(n2)
        n2_hi = n2.astype(BF16)
        n2_lo = (n2 - n2_hi.astype(F32)).astype(BF16)
        parts = jnp.dot(jnp.concatenate([n2_hi, n2_lo], axis=0), wr_ref[...],
                        preferred_element_type=F32)
        logits = ((parts[:tm, :ROUTER_COLS] + parts[tm:, ROUTER_COLS:])
                  + (parts[:tm, ROUTER_COLS:] + parts[tm:, :ROUTER_COLS])) + br_ref[...]
        gl = jnp.where(lane < N_GROUPS, logits, -jnp.inf)
        gm = jnp.max(gl, axis=-1, keepdims=True)
        p_top = 1.0 / jnp.sum(jnp.exp(gl - gm), axis=-1, keepdims=True)
        g_idx = jnp.min(jnp.where(gl == gm, lane, big), axis=-1, keepdims=True)
        e_lo = N_GROUPS + EXPERTS_PER_GROUP * g_idx
        el = jnp.where((lane >= e_lo) & (lane < e_lo + EXPERTS_PER_GROUP), logits, -jnp.inf)
        v1 = jnp.max(el, axis=-1, keepdims=True)
        i1 = jnp.min(jnp.where(el == v1, lane, big), axis=-1, keepdims=True)
        el2 = jnp.where(lane == i1, -jnp.inf, el)
        v2 = jnp.max(el2, axis=-1, keepdims=True)
        i2 = jnp.min(jnp.where(el2 == v2, lane, big), axis=-1, keepdims=True)
        e21 = jnp.exp(v2 - v1)
        gate1 = p_top / (1.0 + e21)
        gate2 = p_top * e21 / (1.0 + e21)
        rt_ref[0, rows, :] = jnp.where(lane == 0, (i1 - N_GROUPS).astype(F32),
                             jnp.where(lane == 1, (i2 - N_GROUPS).astype(F32),
                             jnp.where(lane == 2, gate1, jnp.where(lane == 3, gate2, 0.0))))
        chosen = ((lane == i1 - N_GROUPS) | (lane == i2 - N_GROUPS)).astype(F32)
        counts = counts + jnp.sum(chosen, axis=0, keepdims=True)
    cnt_ref[0, 0] = jnp.broadcast_to(counts, cnt_ref.shape[2:])


def _mix_call(x, o_diff, o_swa, w_out, g2, w_router, b_router, *, tm):
    B, S, D = x.shape
    const = lambda b, i: (0, 0)
    row = lambda b, i: (b, i, 0)
    nt = S // tm
    return pl.pallas_call(
        _mix_kernel,
        grid=(B, nt),
        in_specs=[
            pl.BlockSpec((1, tm, D), row),
            pl.BlockSpec((1, tm, DIFF_V_COLS), row),
            pl.BlockSpec((1, tm, SWA_Q_COLS), row),
            pl.BlockSpec(w_out.shape, const),
            pl.BlockSpec((1, D), const),
            pl.BlockSpec(w_router.shape, const),
            pl.BlockSpec((1, ROUTER_COLS), const),
        ],
        out_specs=(pl.BlockSpec((1, tm, D), row), pl.BlockSpec((1, tm, D // 2), row),
                   pl.BlockSpec((1, tm, ROUTER_COLS), row),
                   pl.BlockSpec((1, 1, SUBLANES, ROUTER_COLS), lambda b, i: (b, i, 0, 0))),
        out_shape=(jax.ShapeDtypeStruct((B, S, D), F32), jax.ShapeDtypeStruct((B, S, D // 2), jnp.uint32),
                   jax.ShapeDtypeStruct((B, S, ROUTER_COLS), F32),
                   jax.ShapeDtypeStruct((B, nt, SUBLANES, ROUTER_COLS), F32)),
        compiler_params=pltpu.CompilerParams(
            dimension_semantics=("parallel", "parallel"), vmem_limit_bytes=48 * 1024 * 1024),
        name="outproj_router",
    )(x, o_diff, o_swa, w_out, g2, w_router, b_router)


def _slot_kernel(rt_ref, base_ref, dest_ref, *, tm):
    rt_t = rt_ref[...].T
    e1 = rt_t[0:1].astype(jnp.int32)
    e2 = rt_t[1:2].astype(jnp.int32)
    eid = lax.broadcasted_iota(jnp.int32, (N_EXPERTS, tm), 0)
    oh1 = eid == e1
    oh2 = eid == e2
    earlier = (lax.broadcasted_iota(jnp.int32, (tm, tm), 0)
               < lax.broadcasted_iota(jnp.int32, (tm, tm), 1)).astype(BF16)
    before = jnp.dot((oh1 | oh2).astype(BF16), earlier, preferred_element_type=F32)
    slot = before + base_ref[0][:, 0:1]
    d1 = jnp.sum(jnp.where(oh1, slot, 0.0), axis=0, keepdims=True).astype(jnp.int32)
    d2 = jnp.sum(jnp.where(oh2, slot, 0.0), axis=0, keepdims=True).astype(jnp.int32)
    dest_ref[0] = jnp.concatenate([d1, d2, jnp.zeros((SUBLANES - TOP_K, tm), jnp.int32)], axis=0)


def _slot_call(rt, tile_base, *, tm):
    nt = rt.shape[0] // tm
    return pl.pallas_call(
        functools.partial(_slot_kernel, tm=tm),
        grid=(nt,),
        in_specs=[
            pl.BlockSpec((tm, ROUTER_COLS), lambda t: (t, 0)),
            pl.BlockSpec((1, N_EXPERTS, LANES), lambda t: (t, 0, 0)),
        ],
        out_specs=pl.BlockSpec((1, SUBLANES, tm), lambda t: (t, 0, 0)),
        out_shape=jax.ShapeDtypeStruct((nt, SUBLANES, tm), jnp.int32),
        compiler_params=pltpu.CompilerParams(dimension_semantics=("parallel",)),
        name="moe_slots",
    )(rt, tile_base)


SC_ROW_CHUNK = 64


def _sc_workers():
    info = plsc.get_sparse_core_info()
    return info.num_cores, info.num_cores * info.num_subcores


def _sc_scatter_rows(rows, idx, n_out):
    n, width = rows.shape
    n_cores, n_workers = _sc_workers()
    n_chunks = n // SC_ROW_CHUNK
    per_worker = n_chunks // n_workers
    assert n_chunks % n_workers == 0
    mesh = plsc.VectorSubcoreMesh(core_axis_name="c", subcore_axis_name="s")

    @functools.partial(
        pl.kernel, mesh=mesh,
        out_type=jax.ShapeDtypeStruct((n_out, width), rows.dtype),
        scratch_types=[
            pltpu.VMEM((SC_ROW_CHUNK,), jnp.int32),
            pltpu.VMEM((SC_ROW_CHUNK, width), rows.dtype),
        ],
    )
    def scatter(rows_hbm, idx_hbm, out_hbm, idx_v, rows_v):
        worker = lax.axis_index("s") * n_cores + lax.axis_index("c")

        @pl.loop(0, per_worker)
        def _(i):
            c = worker * per_worker + i
            pltpu.sync_copy(rows_hbm.at[pl.ds(pl.multiple_of(c * SC_ROW_CHUNK, SC_ROW_CHUNK), SC_ROW_CHUNK)], rows_v)
            for k in range(TOP_K):
                pltpu.sync_copy(idx_hbm.at[k, c], idx_v)
                pltpu.sync_copy(rows_v, out_hbm.at[idx_v])

    return scatter(rows, idx)


def _expert_kernel(be_ref, nvalid_ref, xs_ref, wg_ref, wu_ref, wd_ref, y_ref, wg_b, wu_b, wd_b):
    b = pl.program_id(0)
    n_valid = nvalid_ref[b]

    @pl.when(n_valid > 0)
    def _():
        @pl.when((b == 0) | (be_ref[b] != be_ref[jnp.maximum(b - 1, 0)]))
        def _():
            wg_b[...] = wg_ref[0].astype(BF16)
            wu_b[...] = wu_ref[0].astype(BF16)
            wd_b[...] = wd_ref[0].astype(BF16)

        for c in range(EXPERT_BLOCK // EXPERT_CHUNK):
            rows = pl.ds(c * EXPERT_CHUNK, EXPERT_CHUNK)
            row_id = c * EXPERT_CHUNK + lax.broadcasted_iota(jnp.int32, (EXPERT_CHUNK, xs_ref.shape[1]), 0)
            packed = jnp.where(row_id < n_valid, xs_ref[rows, :], jnp.uint32(0))
            xb = _unpack_bf16_pairs(packed)
            gate = jnp.dot(xb, wg_b[...], preferred_element_type=F32)
            up = jnp.dot(xb, wu_b[...], preferred_element_type=F32)
            hid = (gate * jax.nn.sigmoid(gate) * up).astype(BF16)
            y_ref[rows, :] = _pack_bf16_pairs(jnp.dot(hid, wd_b[...], preferred_element_type=F32))

    @pl.when(n_valid == 0)
    def _():
        y_ref[...] = jnp.zeros_like(y_ref)


def _expert_call(block_expert, n_valid, xs, w_gate, w_up, w_down):
    P = xs.shape[0]
    NB = P // EXPERT_BLOCK
    E, D, F = w_gate.shape
    grid_spec = pltpu.PrefetchScalarGridSpec(
        num_scalar_prefetch=2,
        grid=(NB,),
        in_specs=[
            pl.BlockSpec((EXPERT_BLOCK,) + xs.shape[1:], lambda b, be, nu: (b, 0)),
            pl.BlockSpec((1, D, F), lambda b, be, nu: (be[b], 0, 0)),
            pl.BlockSpec((1, D, F), lambda b, be, nu: (be[b], 0, 0)),
            pl.BlockSpec((1, F, D), lambda b, be, nu: (be[b], 0, 0)),
        ],
        out_specs=pl.BlockSpec((EXPERT_BLOCK, D // 2), lambda b, be, nu: (b, 0)),
        scratch_shapes=[
            pltpu.VMEM((D, F), BF16),
            pltpu.VMEM((D, F), BF16),
            pltpu.VMEM((F, D), BF16),
        ],
    )
    return pl.pallas_call(
        _expert_kernel,
        grid_spec=grid_spec,
        out_shape=jax.ShapeDtypeStruct((P, D // 2), jnp.uint32),
        compiler_params=pltpu.CompilerParams(
            dimension_semantics=("arbitrary",), vmem_limit_bytes=48 * 1024 * 1024),
        name="moe_experts",
    )(block_expert, n_valid, xs, w_gate, w_up, w_down)


def _sc_gather_rows(table, idx):
    n_rows, width = idx.shape[0], table.shape[1]
    n_cores, n_workers = _sc_workers()
    per_worker = n_rows // n_workers
    assert n_rows % (n_workers * 2 * SC_ROW_CHUNK) == 0
    mesh = plsc.VectorSubcoreMesh(core_axis_name="c", subcore_axis_name="s")

    @functools.partial(
        pl.kernel, mesh=mesh,
        out_type=jax.ShapeDtypeStruct((n_rows, width), table.dtype),
        scratch_types=[pltpu.VMEM((SC_ROW_CHUNK,), jnp.int32)] * 2
        + [pltpu.VMEM((SC_ROW_CHUNK, width), table.dtype)] * 2
        + [pltpu.SemaphoreType.DMA] * 4,
    )
    def gather(table_hbm, idx_hbm, out_hbm, idx_a, idx_b, rows_a, rows_b, sem_ga, sem_gb, sem_wa, sem_wb):
        worker = lax.axis_index("s") * n_cores + lax.axis_index("c")
        base = worker * per_worker

        @pl.loop(0, per_worker // SC_ROW_CHUNK, step=2)
        def _(c):
            off_a = pl.multiple_of(base + c * SC_ROW_CHUNK, SC_ROW_CHUNK)
            off_b = pl.multiple_of(off_a + SC_ROW_CHUNK, SC_ROW_CHUNK)
            pltpu.sync_copy(idx_hbm.at[pl.ds(off_a, SC_ROW_CHUNK)], idx_a)
            pltpu.sync_copy(idx_hbm.at[pl.ds(off_b, SC_ROW_CHUNK)], idx_b)
            gather_a = pltpu.async_copy(table_hbm.at[idx_a], rows_a, sem_ga)
            gather_b = pltpu.async_copy(table_hbm.at[idx_b], rows_b, sem_gb)
            gather_a.wait()
            write_a = pltpu.async_copy(rows_a, out_hbm.at[pl.ds(off_a, SC_ROW_CHUNK)], sem_wa)
            gather_b.wait()
            write_b = pltpu.async_copy(rows_b, out_hbm.at[pl.ds(off_b, SC_ROW_CHUNK)], sem_wb)
            write_a.wait()
            write_b.wait()

    return gather(table, idx)


def _combine_kernel(x1_ref, rt_ref, y_ref, fg_ref, o_ref, *, final_norm):
    rt = rt_ref[...]
    y1 = _unpack_bf16_pairs(y_ref[0, 0]).astype(F32)
    y2 = _unpack_bf16_pairs(y_ref[0, 1]).astype(F32)
    h = x1_ref[...] + rt[:, 2:3] * y1 + rt[:, 3:4] * y2
    if final_norm:
        h = h * lax.rsqrt(jnp.mean(h * h, axis=-1, keepdims=True) + EPS) * fg_ref[...]
    o_ref[...] = h


def _combine_call(x1, rt, ysg, final_g, *, tm, final_norm):
    T, D = x1.shape
    return pl.pallas_call(
        functools.partial(_combine_kernel, final_norm=final_norm),
        grid=(T // tm,),
        in_specs=[
            pl.BlockSpec((tm, D), lambda t: (t, 0)),
            pl.BlockSpec((tm, ROUTER_COLS), lambda t: (t, 0)),
            pl.BlockSpec((1, TOP_K, tm, D // 2), lambda t: (t, 0, 0, 0)),
            pl.BlockSpec((1, D), lambda t: (0, 0)),
        ],
        out_specs=pl.BlockSpec((tm, D), lambda t: (t, 0)),
        out_shape=jax.ShapeDtypeStruct((T, D), F32),
        compiler_params=pltpu.CompilerParams(
            dimension_semantics=("parallel",), vmem_limit_bytes=40 * 1024 * 1024),
        name="moe_combine",
    )(x1, rt, ysg, final_g)


def _slot_layout(tile_counts, n_assign):
    NB = -(-n_assign // EXPERT_BLOCK) + N_EXPERTS
    n_tiles = tile_counts.shape[0]
    tc = tile_counts.astype(F32)
    hp = lax.Precision.HIGHEST
    counts = jnp.sum(tc, axis=0)
    padded = jnp.ceil(counts / EXPERT_BLOCK) * EXPERT_BLOCK
    upper = (jnp.arange(N_EXPERTS)[:, None] < jnp.arange(N_EXPERTS)[None, :]).astype(F32)
    pad_start = jnp.dot(padded, upper, precision=hp)
    pad_end = pad_start + padded
    lower = (jnp.arange(n_tiles)[:, None] > jnp.arange(n_tiles)[None, :]).astype(F32)
    tile_base = pad_start[None, :] + jnp.dot(lower, tc, precision=hp)
    block_start = jnp.arange(NB, dtype=F32) * EXPERT_BLOCK
    block_expert = jnp.minimum(jnp.sum((pad_end[None, :] <= block_start[:, None]).astype(jnp.int32), axis=1),
                               N_EXPERTS - 1)
    mine = block_expert[:, None] == jnp.arange(N_EXPERTS)[None, :]
    run_end = jnp.sum(jnp.where(mine, (pad_start + counts)[None, :], 0.0), axis=1)
    n_valid = jnp.clip(run_end - block_start, 0, EXPERT_BLOCK).astype(jnp.int32)
    return NB, block_expert.astype(jnp.int32), n_valid, tile_base


def _rope_tables(S):
    half = HEAD_DIM // 2
    inv = 1.0 / (ROPE_THETA ** (jnp.arange(0, HEAD_DIM, 2, dtype=F32) / HEAD_DIM))
    pos = jnp.arange(S, dtype=F32)
    ang_l = pos[:, None] * jnp.tile(inv, LANES // half)[None, :]
    sign = jnp.tile(jnp.concatenate([-jnp.ones((half,), F32), jnp.ones((half,), F32)]), LANES // HEAD_DIM)
    ang_t = inv[:, None] * pos[None, :]
    return jnp.cos(ang_l), jnp.sin(ang_l) * sign[None, :], jnp.cos(ang_t), jnp.sin(ang_t)


def kernel(x, norm1_g, w_in, lambda_q1, lambda_k1, lambda_q2, lambda_k2, subln_g, sinks, w_out,
           norm2_g, w_router_group, b_router_group, w_router_expert, b_router_expert,
           w_gate, w_up, w_down, final_g):
    B, S, D = x.shape
    T = B * S
    depth = w_in.shape[0]
    tq, tk = 512, 512
    tm_proj = 512
    tm_tok = 512
    tq_swa = 1024
    qscale = HEAD_DIM ** -0.5 * math.log2(math.e)
    cos_l, sin_l, cos_t, sin_t = _rope_tables(S)

    c0 = DIFF_QK_COLS
    c1 = 2 * DIFF_QK_COLS
    c2 = c1 + DIFF_V_COLS
    c3 = c2 + SWA_Q_COLS
    c4 = c3 + SWA_KV_COLS
    for l in range(depth):
        lambda_init = 0.8 - 0.6 * math.exp(-0.3 * l)
        w = w_in[l]
        w_nat = jnp.concatenate([w[:, c0:c1], w[:, c3:c4]], axis=1).astype(BF16)
        w_tr = jnp.concatenate([w[:, :c0] * qscale, w[:, c1:c2], w[:, c2:c3] * qscale, w[:, c4:]],
                               axis=1).T.astype(BF16)
        dqt, dk, dvt, sqt, sk, svt = _proj_call(
            x, norm1_g[l][None, :], w_nat, w_tr, cos_l, sin_l, cos_t, sin_t, tm=tm_proj, tk=tk)

        lam_p = jnp.stack([lambda_q1[l], lambda_k1[l], lambda_q2[l], lambda_k2[l]]).astype(F32)
        o_diff = _diff_call(lam_p, dqt, dk, dvt, subln_g[l][None, :].astype(F32),
                            tq=tq, tk=tk, lambda_init=lambda_init)
        sink_row = jnp.repeat(sinks[l].astype(F32) * math.log2(math.e), WINDOW)[None, :]
        o_swa = _swa_call(sink_row, sqt, sk, svt, tq=tq_swa)

        wo_b = w_out[l].astype(BF16)
        w_router = jnp.zeros((D, ROUTER_COLS), F32)
        w_router = w_router.at[:, :N_GROUPS].set(w_router_group[l])
        w_router = w_router.at[:, N_GROUPS:N_GROUPS + N_EXPERTS].set(w_router_expert[l])
        w_router_hi = w_router.astype(BF16)
        w_router_lo = (w_router - w_router_hi.astype(F32)).astype(BF16)
        w_router = jnp.concatenate([w_router_hi, w_router_lo], axis=1)
        b_router = jnp.zeros((1, ROUTER_COLS), F32)
        b_router = b_router.at[0, :N_GROUPS].set(b_router_group[l])
        b_router = b_router.at[0, N_GROUPS:N_GROUPS + N_EXPERTS].set(b_router_expert[l])
        x1, n2p, rt, cnt = _mix_call(x, o_diff, o_swa, wo_b, norm2_g[l][None, :], w_router, b_router, tm=tm_tok)

        rt2 = rt.reshape(T, ROUTER_COLS)
        tile_counts = cnt[:, :, 0, :N_EXPERTS].reshape(T // tm_tok, N_EXPERTS).astype(jnp.int32)
        NB, block_expert, n_valid, tile_base = _slot_layout(tile_counts, T * TOP_K)
        tile_base = jnp.broadcast_to(tile_base.astype(F32)[:, :, None], (T // tm_tok, N_EXPERTS, LANES))
        dest = _slot_call(rt2, tile_base, tm=tm_tok)
        scatter_idx = jnp.swapaxes(dest[:, :TOP_K, :], 0, 1).reshape(TOP_K, T // SC_ROW_CHUNK, SC_ROW_CHUNK)
        xs = _sc_scatter_rows(n2p.reshape(T, D // 2), scatter_idx, NB * EXPERT_BLOCK)
        ys = _expert_call(block_expert, n_valid, xs, w_gate[l], w_up[l], w_down[l])
        ysg = _sc_gather_rows(ys, dest[:, :TOP_K, :].reshape(T * TOP_K))
        x = _combine_call(x1.reshape(T, D), rt2, ysg.reshape(T // tm_tok, TOP_K, tm_tok, D // 2),
                          final_g[None, :], tm=tm_tok, final_norm=(l == depth - 1)).reshape(B, S, D)
    return x
```

```python
import functools
import math

import jax
import jax.numpy as jnp
from jax import lax
from jax.experimental import pallas as pl
from jax.experimental.pallas import tpu as pltpu
from jax.experimental.pallas import tpu_sc as plsc

HEAD_DIM = 64
DIFF_HEADS = 4
DIFF_V_DIM = 2 * HEAD_DIM
SWA_Q_HEADS = 8
SWA_KV_HEADS = 2
SWA_GROUP = SWA_Q_HEADS // SWA_KV_HEADS
WINDOW = 128
ROPE_THETA = 10000.0
N_GROUPS = 4
EXPERTS_PER_GROUP = 8
N_EXPERTS = N_GROUPS * EXPERTS_PER_GROUP
TOP_K = 2
EXPERT_BLOCK = 512
EXPERT_CHUNK = 256
EPS = 1e-6
NEG = -1e30

DIFF_QK_COLS = DIFF_HEADS * 2 * HEAD_DIM
DIFF_V_COLS = DIFF_HEADS * DIFF_V_DIM
SWA_Q_COLS = SWA_Q_HEADS * HEAD_DIM
SWA_KV_COLS = SWA_KV_HEADS * HEAD_DIM
LANES = 128
SUBLANES = 8
BF16_SUBLANES = 16
VT_ROWS = DIFF_V_DIM + BF16_SUBLANES
SWA_VT_ROWS = SWA_KV_COLS + BF16_SUBLANES
ROUTER_COLS = LANES
DIFF_UNROLL = 4
DIFF_S_BUFS = 4
DIFF_Q_TILES = 2

BF16 = jnp.bfloat16
F32 = jnp.float32


def _rope_lanes(x, cos_l, sin_l, first_half):
    rot = jnp.where(first_half, pltpu.roll(x, 96, 1), pltpu.roll(x, 32, 1))
    return x * cos_l + rot * sin_l


def _proj_kernel(x_ref, g_ref, wnat_ref, wtr_ref, cosl_ref, sinl_ref, cost_ref, sint_ref,
                 dqt_ref, dk_ref, dvt_ref, sqt_ref, sk_ref, svt_ref, *, tk):
    x = x_ref[0]
    tm = x.shape[0]
    n1 = x * lax.rsqrt(jnp.mean(x * x, axis=-1, keepdims=True) + EPS) * g_ref[...]
    n1b = n1.astype(BF16)
    nat = jnp.dot(n1b, wnat_ref[...], preferred_element_type=F32)
    tr = lax.dot_general(wtr_ref[...], n1b, (((1,), (1,)), ((), ())),
                         preferred_element_type=F32)

    cos_l, sin_l = cosl_ref[...], sinl_ref[...]
    first_half = (lax.broadcasted_iota(jnp.int32, (tm, LANES), 1) & (HEAD_DIM - 1)) < HEAD_DIM // 2
    for h in range(DIFF_HEADS):
        slab = nat[:, h * LANES:(h + 1) * LANES]
        dk_ref[0, h] = _rope_lanes(slab, cos_l, sin_l, first_half).astype(BF16)
    sk = _rope_lanes(nat[:, DIFF_QK_COLS:DIFF_QK_COLS + LANES], cos_l, sin_l, first_half).astype(BF16)
    for c in range(tm // WINDOW):
        sk_ref[0, c] = sk[c * WINDOW:(c + 1) * WINDOW]

    cos_t, sin_t = cost_ref[...], sint_ref[...]
    half = HEAD_DIM // 2

    def rope_rows(r0):
        x1 = tr[r0:r0 + half]
        x2 = tr[r0 + half:r0 + HEAD_DIM]
        return (x1 * cos_t - x2 * sin_t).astype(BF16), (x1 * sin_t + x2 * cos_t).astype(BF16)

    for h in range(DIFF_HEADS):
        for c in range(2):
            lo, hi = rope_rows(h * 2 * HEAD_DIM + c * HEAD_DIM)
            dqt_ref[0, h, c * HEAD_DIM:c * HEAD_DIM + half] = lo
            dqt_ref[0, h, c * HEAD_DIM + half:(c + 1) * HEAD_DIM] = hi
    ones_rows = (lax.broadcasted_iota(jnp.int32, (BF16_SUBLANES, tk), 0) == 0).astype(BF16)
    for h in range(DIFF_HEADS):
        r0 = DIFF_QK_COLS + h * DIFF_V_DIM
        for c in range(tm // tk):
            dvt_ref[0, h, c, :DIFF_V_DIM] = tr[r0:r0 + DIFF_V_DIM, c * tk:(c + 1) * tk].astype(BF16)
            dvt_ref[0, h, c, DIFF_V_DIM:] = ones_rows

    r0 = DIFF_QK_COLS + DIFF_V_COLS
    for h in range(SWA_Q_HEADS):
        lo, hi = rope_rows(r0 + h * HEAD_DIM)
        sqt_ref[0, h * HEAD_DIM:h * HEAD_DIM + half] = lo
        sqt_ref[0, h * HEAD_DIM + half:(h + 1) * HEAD_DIM] = hi
    r0 += SWA_Q_COLS
    for c in range(tm // WINDOW):
        svt_ref[0, c, :SWA_KV_COLS] = tr[r0:r0 + SWA_KV_COLS, c * WINDOW:(c + 1) * WINDOW].astype(BF16)
        svt_ref[0, c, SWA_KV_COLS:] = ones_rows[:, :WINDOW]


def _proj_call(x, g1, w_nat, w_tr, cos_l, sin_l, cos_t, sin_t, *, tm, tk):
    B, S, D = x.shape
    nkv = S // tk
    grid = (B, S // tm)
    const = lambda b, i: (0, 0)
    out_shape = (
        jax.ShapeDtypeStruct((B, DIFF_HEADS, 2 * HEAD_DIM, S), BF16),
        jax.ShapeDtypeStruct((B, DIFF_HEADS, S, 2 * HEAD_DIM), BF16),
        jax.ShapeDtypeStruct((B, DIFF_HEADS, nkv, VT_ROWS, tk), BF16),
        jax.ShapeDtypeStruct((B, SWA_Q_COLS, S), BF16),
        jax.ShapeDtypeStruct((B, S // WINDOW, WINDOW, SWA_KV_COLS), BF16),
        jax.ShapeDtypeStruct((B, S // WINDOW, SWA_VT_ROWS, WINDOW), BF16),
    )
    return pl.pallas_call(
        functools.partial(_proj_kernel, tk=tk),
        grid=grid,
        in_specs=[
            pl.BlockSpec((1, tm, D), lambda b, i: (b, i, 0)),
            pl.BlockSpec((1, D), const),
            pl.BlockSpec(w_nat.shape, const),
            pl.BlockSpec(w_tr.shape, const),
            pl.BlockSpec((tm, LANES), lambda b, i: (i, 0)),
            pl.BlockSpec((tm, LANES), lambda b, i: (i, 0)),
            pl.BlockSpec((HEAD_DIM // 2, tm), lambda b, i: (0, i)),
            pl.BlockSpec((HEAD_DIM // 2, tm), lambda b, i: (0, i)),
        ],
        out_specs=(
            pl.BlockSpec((1, DIFF_HEADS, 2 * HEAD_DIM, tm), lambda b, i: (b, 0, 0, i)),
            pl.BlockSpec((1, DIFF_HEADS, tm, 2 * HEAD_DIM), lambda b, i: (b, 0, i, 0)),
            pl.BlockSpec((1, DIFF_HEADS, tm // tk, VT_ROWS, tk), lambda b, i: (b, 0, i, 0, 0)),
            pl.BlockSpec((1, SWA_Q_COLS, tm), lambda b, i: (b, 0, i)),
            pl.BlockSpec((1, tm // WINDOW, WINDOW, SWA_KV_COLS), lambda b, i: (b, i, 0, 0)),
            pl.BlockSpec((1, tm // WINDOW, SWA_VT_ROWS, WINDOW), lambda b, i: (b, i, 0, 0)),
        ),
        out_shape=out_shape,
        compiler_params=pltpu.CompilerParams(
            dimension_semantics=("parallel", "parallel"), vmem_limit_bytes=48 * 1024 * 1024),
        name="proj_rope",
    )(x, g1, w_nat, w_tr, cos_l, sin_l, cos_t, sin_t)


def _diff_kernel(lam_ref, qt_ref, k_ref, vt_ref, g_ref, o_ref, *scratch, tq, tk, lambda_init):
    step = pl.program_id(2)
    s_bufs = scratch[:DIFF_S_BUFS]
    top_bufs = scratch[DIFF_S_BUFS:2 * DIFF_S_BUFS]
    state = scratch[2 * DIFF_S_BUFS:2 * DIFF_S_BUFS + 2 * DIFF_Q_TILES]
    bias_ref = scratch[-1]

    @pl.when(step == 0)
    def _():
        r = lax.broadcasted_iota(jnp.int32, (tk, 2 * tq), 0)
        c = lax.broadcasted_iota(jnp.int32, (tk, 2 * tq), 1) & (tq - 1)
        bias_ref[...] = jnp.where(r <= c, 0.0, NEG).astype(F32)

    lam_p = lam_ref[...]
    lam = (jnp.exp(jnp.sum(lam_p[0:1] * lam_p[1:2], axis=-1, keepdims=True))
           - jnp.exp(jnp.sum(lam_p[2:3] * lam_p[3:4], axis=-1, keepdims=True)) + lambda_init)

    for sub in range(DIFF_Q_TILES):
        _diff_query_tile(step * DIFF_Q_TILES + sub, qt_ref[0, 0, :, sub * tq:(sub + 1) * tq], k_ref, vt_ref,
                         g_ref, o_ref.at[0, pl.ds(sub * tq, tq), :], s_bufs, top_bufs,
                         state[2 * sub], state[2 * sub + 1], bias_ref, lam,
                         tq=tq, tk=tk, lambda_init=lambda_init)


def _diff_query_tile(i, qt, k_ref, vt_ref, g_ref, o_ref, s_bufs, top_bufs, m_ref, acc_ref, bias_ref, lam,
                     *, tq, tk, lambda_init):
    z = jnp.zeros((HEAD_DIM, tq), BF16)
    qw = jnp.concatenate([jnp.concatenate([qt[:HEAD_DIM], z], axis=1),
                          jnp.concatenate([z, qt[HEAD_DIM:]], axis=1)], axis=0)

    def scores(j, par):
        kt = k_ref[0, 0, pl.ds(pl.multiple_of(j * tk, tk), tk), :]
        s = jnp.dot(kt, qw, preferred_element_type=F32)
        s_bufs[par][...] = s
        top_bufs[par][...] = jnp.max(s, axis=0, keepdims=True)

    def absorb(j, par, masked):
        s = s_bufs[par][...]
        if masked:
            s = s + bias_ref[...]
            top = jnp.max(s, axis=0, keepdims=True)
        else:
            top = top_bufs[par][...]
        m = m_ref[...]
        m_new = jnp.maximum(m, top)
        alpha = jnp.exp2(m - m_new)
        p = jnp.exp2(s - m_new).astype(BF16)
        m_ref[...] = m_new
        pv = jnp.dot(vt_ref[0, 0, j], p, preferred_element_type=F32)
        acc_ref[...] = alpha * acc_ref[...] + pv

    m_ref[...] = jnp.full(m_ref.shape, NEG, F32)
    acc_ref[...] = jnp.zeros(acc_ref.shape, F32)

    nfull = (i * tq) // tk
    scores(nfull, 0)
    scores(0, 1)
    absorb(nfull, 0, True)

    def group(t, c):
        j = DIFF_UNROLL * t
        for idx in range(DIFF_UNROLL):
            scores(j + idx + 1, (idx + 2) % DIFF_S_BUFS)
            absorb(j + idx, (idx + 1) % DIFF_S_BUFS, False)
        return c

    lax.fori_loop(0, nfull // DIFF_UNROLL, group, 0)

    for rem in range(1, DIFF_UNROLL):
        @pl.when(nfull % DIFF_UNROLL == rem)
        def _():
            first = nfull - rem
            for idx in range(rem):
                if idx + 1 < rem:
                    scores(first + idx + 1, (idx + 2) % DIFF_S_BUFS)
                absorb(first + idx, (idx + 1) % DIFF_S_BUFS, False)

    inv_l = 1.0 / acc_ref[DIFF_V_DIM:DIFF_V_DIM + 1, :]
    o = (acc_ref[:DIFF_V_DIM, :tq] * inv_l[:, :tq]
         - lam * (acc_ref[:DIFF_V_DIM, tq:] * inv_l[:, tq:]))
    o = o * lax.rsqrt(jnp.mean(o * o, axis=0, keepdims=True) + EPS)
    o_ref[...] = (o.T * g_ref[...] * (1.0 - lambda_init)).astype(BF16)


def _diff_call(lam_p, dqt, dk, dvt, subln_g, *, tq, tk, lambda_init):
    B, H, _, S = dqt.shape
    assert tk == tq and S % tk == 0, "the diagonal tile's causal pattern is built for square tiles"
    nkv = S // tk
    tq_step = DIFF_Q_TILES * tq
    assert S % tq_step == 0
    grid = (B, H, S // tq_step)
    return pl.pallas_call(
        functools.partial(_diff_kernel, tq=tq, tk=tk, lambda_init=lambda_init),
        grid=grid,
        in_specs=[
            pl.BlockSpec(lam_p.shape, lambda b, h, i: (0, 0)),
            pl.BlockSpec((1, 1, 2 * HEAD_DIM, tq_step), lambda b, h, i: (b, h, 0, i)),
            pl.BlockSpec((1, 1, S, 2 * HEAD_DIM), lambda b, h, i: (b, h, 0, 0)),
            pl.BlockSpec((1, 1, nkv, VT_ROWS, tk), lambda b, h, i: (b, h, 0, 0, 0)),
            pl.BlockSpec((1, DIFF_V_DIM), lambda b, h, i: (0, 0)),
        ],
        out_specs=pl.BlockSpec((1, tq_step, DIFF_V_DIM), lambda b, h, i: (b, i, h)),
        out_shape=jax.ShapeDtypeStruct((B, S, DIFF_V_COLS), BF16),
        scratch_shapes=[pltpu.VMEM((tk, 2 * tq), F32)] * DIFF_S_BUFS + [
            pltpu.VMEM((1, 2 * tq), F32)] * DIFF_S_BUFS + [
            pltpu.VMEM((1, 2 * tq), F32),
            pltpu.VMEM((VT_ROWS, 2 * tq), F32)] * DIFF_Q_TILES + [
            pltpu.VMEM((tk, 2 * tq), F32),
        ],
        compiler_params=pltpu.CompilerParams(
            dimension_semantics=("parallel", "parallel", "arbitrary"),
            vmem_limit_bytes=48 * 1024 * 1024),
        name="diff_attn",
    )(lam_p, dqt, dk, dvt, subln_g)


def _swa_kernel(sink_ref, qt_ref, k_ref, vt_ref, o_ref, *, tq):
    i = pl.program_id(1)
    n_cols = SWA_Q_HEADS * WINDOW
    half_cols = n_cols // SWA_KV_HEADS
    sink = sink_ref[...]
    row = lax.broadcasted_iota(jnp.int32, (2 * WINDOW, WINDOW), 0)
    qrel = lax.broadcasted_iota(jnp.int32, (2 * WINDOW, WINDOW), 1)
    band = (row - WINDOW <= qrel) & (row > qrel)
    in_current = row >= WINDOW
    z = jnp.zeros((HEAD_DIM, half_cols), BF16)
    for sub in range(tq // WINDOW):
        n = i * (tq // WINDOW) + sub
        prev = jnp.maximum(n - 1, 0)
        kwin = jnp.concatenate([k_ref[0, prev], k_ref[0, n]], axis=0)
        vtwin = jnp.concatenate([vt_ref[0, prev], vt_ref[0, n]], axis=1)
        qt = qt_ref[0, :, sub * WINDOW:(sub + 1) * WINDOW]
        heads = [qt[h * HEAD_DIM:(h + 1) * HEAD_DIM] for h in range(SWA_Q_HEADS)]
        qw = jnp.concatenate(
            [jnp.concatenate(heads[:SWA_GROUP] + [z], axis=1),
             jnp.concatenate([z] + heads[SWA_GROUP:], axis=1)], axis=0)
        s = jnp.dot(kwin, qw, preferred_element_type=F32)
        valid = band & (in_current | (n >= 1))
        s = jnp.concatenate(
            [jnp.where(valid, s[:, h * WINDOW:(h + 1) * WINDOW], NEG) for h in range(SWA_Q_HEADS)], axis=1)
        m = jnp.maximum(jnp.max(s, axis=0, keepdims=True), sink)
        p = jnp.exp2(s - m).astype(BF16)
        acc = jnp.dot(vtwin, p, preferred_element_type=F32)
        den = acc[SWA_KV_COLS:SWA_KV_COLS + 1] + jnp.exp2(sink - m)
        on = acc[:SWA_KV_COLS] / den
        u = jnp.concatenate([on[:HEAD_DIM, :half_cols], on[HEAD_DIM:, half_cols:]], axis=1)
        for hp in range(SWA_Q_HEADS // 2):
            two = jnp.concatenate([u[:, (2 * hp) * WINDOW:(2 * hp + 1) * WINDOW],
                                   u[:, (2 * hp + 1) * WINDOW:(2 * hp + 2) * WINDOW]], axis=0)
            o_ref[0, sub * WINDOW:(sub + 1) * WINDOW, hp * LANES:(hp + 1) * LANES] = two.T.astype(BF16)


def _swa_call(sink_row, sqt, sk, svt, *, tq):
    B, _, S = sqt.shape
    nb = S // WINDOW
    return pl.pallas_call(
        functools.partial(_swa_kernel, tq=tq),
        grid=(B, S // tq),
        in_specs=[
            pl.BlockSpec(sink_row.shape, lambda b, i: (0, 0)),
            pl.BlockSpec((1, SWA_Q_COLS, tq), lambda b, i: (b, 0, i)),
            pl.BlockSpec((1, nb, WINDOW, SWA_KV_COLS), lambda b, i: (b, 0, 0, 0)),
            pl.BlockSpec((1, nb, SWA_VT_ROWS, WINDOW), lambda b, i: (b, 0, 0, 0)),
        ],
        out_specs=pl.BlockSpec((1, tq, SWA_Q_COLS), lambda b, i: (b, i, 0)),
        out_shape=jax.ShapeDtypeStruct((B, S, SWA_Q_COLS), BF16),
        compiler_params=pltpu.CompilerParams(
            dimension_semantics=("parallel", "arbitrary"), vmem_limit_bytes=40 * 1024 * 1024),
        name="swa_attn",
    )(sink_row, sqt, sk, svt)


def _pack_bf16_pairs(x):
    n = x.shape[1] // 2
    lo = lax.bitcast_convert_type(x[:, :n].astype(BF16).astype(F32), jnp.uint32)
    hi = lax.bitcast_convert_type(x[:, n:].astype(BF16).astype(F32), jnp.uint32)
    return (lo >> 16) | (hi & jnp.uint32(0xFFFF0000))


def _unpack_bf16_pairs(w):
    lo = lax.bitcast_convert_type(w << 16, F32)
    hi = lax.bitcast_convert_type(w & jnp.uint32(0xFFFF0000), F32)
    return jnp.concatenate([lo, hi], axis=1).astype(BF16)


MIX_CHUNKS = 1


def _mix_kernel(x_ref, od_ref, os_ref, wo_ref, g2_ref, wr_ref, br_ref, x1_ref, n2_ref, rt_ref, cnt_ref):
    tm = x_ref.shape[1] // MIX_CHUNKS
    lane = lax.broadcasted_iota(jnp.int32, (tm, ROUTER_COLS), 1)
    lane_f = lane.astype(F32)
    big = float(ROUTER_COLS)
    counts = jnp.zeros((1, ROUTER_COLS), F32)
    mixed = jnp.concatenate([od_ref[0], os_ref[0]], axis=1)
    x1_ref[0] = x_ref[0] + jnp.dot(mixed, wo_ref[...], preferred_element_type=F32)
    for c in range(MIX_CHUNKS):
        rows = pl.ds(c * tm, tm)
        h = x1_ref[0, rows, :]
        n2 = h * lax.rsqrt(jnp.mean(h * h, axis=-1, keepdims=True) + EPS) * g2_ref[...]
        n2_ref[0, rows, :] = _pack_bf16_pairs(n2)
        n2_hi = n2.astype(BF16)
        n2_lo = (n2 - n2_hi.astype(F32)).astype(BF16)
        parts = jnp.dot(jnp.concatenate([n2_hi, n2_lo], axis=0), wr_ref[...],
                        preferred_element_type=F32)
        logits = ((parts[:tm, :ROUTER_COLS] + parts[tm:, ROUTER_COLS:])
                  + (parts[:tm, ROUTER_COLS:] + parts[tm:, :ROUTER_COLS])) + br_ref[...]
        gl = jnp.where(lane < N_GROUPS, logits, -jnp.inf)
        gm = jnp.max(gl, axis=-1, keepdims=True)
        p_top = 1.0 / jnp.sum(jnp.exp(gl - gm), axis=-1, keepdims=True)
        g_idx = jnp.min(jnp.where(gl == gm, lane_f, big), axis=-1, keepdims=True)
        e_lo = N_GROUPS + EXPERTS_PER_GROUP * g_idx
        el = jnp.where((lane_f >= e_lo) & (lane_f < e_lo + EXPERTS_PER_GROUP), logits, -jnp.inf)
        v1 = jnp.max(el, axis=-1, keepdims=True)
        i1 = jnp.min(jnp.where(el == v1, lane_f, big), axis=-1, keepdims=True)
        el2 = jnp.where(lane_f == i1, -jnp.inf, el)
        v2 = jnp.max(el2, axis=-1, keepdims=True)
        i2 = jnp.min(jnp.where(el2 == v2, lane_f, big), axis=-1, keepdims=True)
        e21 = jnp.exp(v2 - v1)
        gate1 = p_top / (1.0 + e21)
        gate2 = p_top * e21 / (1.0 + e21)
        rt_ref[0, rows, :] = jnp.where(lane == 0, i1 - N_GROUPS,
                             jnp.where(lane == 1, i2 - N_GROUPS,
                             jnp.where(lane == 2, gate1, jnp.where(lane == 3, gate2, 0.0))))
        chosen = ((lane_f == i1 - N_GROUPS) | (lane_f == i2 - N_GROUPS)).astype(F32)
        counts = counts + jnp.sum(chosen, axis=0, keepdims=True)
    cnt_ref[0, 0] = jnp.broadcast_to(counts, cnt_ref.shape[2:])


def _mix_call(x, o_diff, o_swa, w_out, g2, w_router, b_router, *, tm):
    B, S, D = x.shape
    const = lambda b, i: (0, 0)
    row = lambda b, i: (b, i, 0)
    nt = S // tm
    return pl.pallas_call(
        _mix_kernel,
        grid=(B, nt),
        in_specs=[
            pl.BlockSpec((1, tm, D), row),
            pl.BlockSpec((1, tm, DIFF_V_COLS), row),
            pl.BlockSpec((1, tm, SWA_Q_COLS), row),
            pl.BlockSpec(w_out.shape, const),
            pl.BlockSpec((1, D), const),
            pl.BlockSpec(w_router.shape, const),
            pl.BlockSpec((1, ROUTER_COLS), const),
        ],
        out_specs=(pl.BlockSpec((1, tm, D), row), pl.BlockSpec((1, tm, D // 2), row),
                   pl.BlockSpec((1, tm, ROUTER_COLS), row),
                   pl.BlockSpec((1, 1, SUBLANES, ROUTER_COLS), lambda b, i: (b, i, 0, 0))),
        out_shape=(jax.ShapeDtypeStruct((B, S, D), F32), jax.ShapeDtypeStruct((B, S, D // 2), jnp.uint32),
                   jax.ShapeDtypeStruct((B, S, ROUTER_COLS), F32),
                   jax.ShapeDtypeStruct((B, nt, SUBLANES, ROUTER_COLS), F32)),
        compiler_params=pltpu.CompilerParams(
            dimension_semantics=("parallel", "parallel"), vmem_limit_bytes=48 * 1024 * 1024),
        name="outproj_router",
    )(x, o_diff, o_swa, w_out, g2, w_router, b_router)


def _slot_kernel(rt_ref, base_ref, dest_ref, *, tm):
    rt_t = rt_ref[...].T
    e1 = rt_t[0:1].astype(jnp.int32)
    e2 = rt_t[1:2].astype(jnp.int32)
    eid = lax.broadcasted_iota(jnp.int32, (N_EXPERTS, tm), 0)
    oh1 = eid == e1
    oh2 = eid == e2
    earlier = (lax.broadcasted_iota(jnp.int32, (tm, tm), 0)
               < lax.broadcasted_iota(jnp.int32, (tm, tm), 1)).astype(BF16)
    before = jnp.dot((oh1 | oh2).astype(BF16), earlier, preferred_element_type=F32)
    slot = before + base_ref[0][:, 0:1]
    d1 = jnp.sum(jnp.where(oh1, slot, 0.0), axis=0, keepdims=True).astype(jnp.int32)
    d2 = jnp.sum(jnp.where(oh2, slot, 0.0), axis=0, keepdims=True).astype(jnp.int32)
    dest_ref[0] = jnp.concatenate([d1, d2, jnp.zeros((SUBLANES - TOP_K, tm), jnp.int32)], axis=0)


def _slot_call(rt, tile_base, *, tm):
    nt = rt.shape[0] // tm
    return pl.pallas_call(
        functools.partial(_slot_kernel, tm=tm),
        grid=(nt,),
        in_specs=[
            pl.BlockSpec((tm, ROUTER_COLS), lambda t: (t, 0)),
            pl.BlockSpec((1, N_EXPERTS, LANES), lambda t: (t, 0, 0)),
        ],
        out_specs=pl.BlockSpec((1, SUBLANES, tm), lambda t: (t, 0, 0)),
        out_shape=jax.ShapeDtypeStruct((nt, SUBLANES, tm), jnp.int32),
        compiler_params=pltpu.CompilerParams(dimension_semantics=("parallel",)),
        name="moe_slots",
    )(rt, tile_base)


SC_ROW_CHUNK = 64


def _sc_workers():
    info = plsc.get_sparse_core_info()
    return info.num_cores, info.num_cores * info.num_subcores


def _sc_scatter_rows(rows, idx, n_out):
    n, width = rows.shape
    n_cores, n_workers = _sc_workers()
    n_chunks = n // SC_ROW_CHUNK
    per_worker = n_chunks // n_workers
    assert n_chunks % n_workers == 0
    mesh = plsc.VectorSubcoreMesh(core_axis_name="c", subcore_axis_name="s")

    @functools.partial(
        pl.kernel, mesh=mesh,
        out_type=jax.ShapeDtypeStruct((n_out, width), rows.dtype),
        scratch_types=[
            pltpu.VMEM((SC_ROW_CHUNK,), jnp.int32),
            pltpu.VMEM((SC_ROW_CHUNK, width), rows.dtype),
        ],
    )
    def scatter(rows_hbm, idx_hbm, out_hbm, idx_v, rows_v):
        worker = lax.axis_index("s") * n_cores + lax.axis_index("c")

        @pl.loop(0, per_worker)
        def _(i):
            c = worker * per_worker + i
            pltpu.sync_copy(rows_hbm.at[pl.ds(pl.multiple_of(c * SC_ROW_CHUNK, SC_ROW_CHUNK), SC_ROW_CHUNK)], rows_v)
            for k in range(TOP_K):
                pltpu.sync_copy(idx_hbm.at[k, c], idx_v)
                pltpu.sync_copy(rows_v, out_hbm.at[idx_v])

    return scatter(rows, idx)


def _expert_kernel(be_ref, nvalid_ref, xs_ref, wg_ref, wu_ref, wd_ref, y_ref, wg_b, wu_b, wd_b):
    b = pl.program_id(0)
    n_valid = nvalid_ref[b]

    @pl.when(n_valid > 0)
    def _():
        @pl.when((b == 0) | (be_ref[b] != be_ref[jnp.maximum(b - 1, 0)]))
        def _():
            wg_b[...] = wg_ref[0].astype(BF16)
            wu_b[...] = wu_ref[0].astype(BF16)
            wd_b[...] = wd_ref[0].astype(BF16)

        for c in range(EXPERT_BLOCK // EXPERT_CHUNK):
            rows = pl.ds(c * EXPERT_CHUNK, EXPERT_CHUNK)
            row_id = c * EXPERT_CHUNK + lax.broadcasted_iota(jnp.int32, (EXPERT_CHUNK, xs_ref.shape[1]), 0)
            packed = jnp.where(row_id < n_valid, xs_ref[rows, :], jnp.uint32(0))
            xb = _unpack_bf16_pairs(packed)
            gate = jnp.dot(xb, wg_b[...], preferred_element_type=F32)
            up = jnp.dot(xb, wu_b[...], preferred_element_type=F32)
            hid = (gate * jax.nn.sigmoid(gate) * up).astype(BF16)
            y_ref[rows, :] = _pack_bf16_pairs(jnp.dot(hid, wd_b[...], preferred_element_type=F32))

    @pl.when(n_valid == 0)
    def _():
        y_ref[...] = jnp.zeros_like(y_ref)


def _expert_call(block_expert, n_valid, xs, w_gate, w_up, w_down):
    P = xs.shape[0]
    NB = P // EXPERT_BLOCK
    E, D, F = w_gate.shape
    grid_spec = pltpu.PrefetchScalarGridSpec(
        num_scalar_prefetch=2,
        grid=(NB,),
        in_specs=[
            pl.BlockSpec((EXPERT_BLOCK,) + xs.shape[1:], lambda b, be, nu: (b, 0)),
            pl.BlockSpec((1, D, F), lambda b, be, nu: (be[b], 0, 0)),
            pl.BlockSpec((1, D, F), lambda b, be, nu: (be[b], 0, 0)),
            pl.BlockSpec((1, F, D), lambda b, be, nu: (be[b], 0, 0)),
        ],
        out_specs=pl.BlockSpec((EXPERT_BLOCK, D // 2), lambda b, be, nu: (b, 0)),
        scratch_shapes=[
            pltpu.VMEM((D, F), BF16),
            pltpu.VMEM((D, F), BF16),
            pltpu.VMEM((F, D), BF16),
        ],
    )
    return pl.pallas_call(
        _expert_kernel,
        grid_spec=grid_spec,
        out_shape=jax.ShapeDtypeStruct((P, D // 2), jnp.uint32),
        compiler_params=pltpu.CompilerParams(
            dimension_semantics=("arbitrary",), vmem_limit_bytes=48 * 1024 * 1024),
        name="moe_experts",
    )(block_expert, n_valid, xs, w_gate, w_up, w_down)


def _sc_gather_rows(table, idx):
    n_rows, width = idx.shape[0], table.shape[1]
    n_cores, n_workers = _sc_workers()
    per_worker = n_rows // n_workers
    assert n_rows % (n_workers * 2 * SC_ROW_CHUNK) == 0
    mesh = plsc.VectorSubcoreMesh(core_axis_name="c", subcore_axis_name="s")

    @functools.partial(
        pl.kernel, mesh=mesh,
        out_type=jax.ShapeDtypeStruct((n_rows, width), table.dtype),
        scratch_types=[pltpu.VMEM((SC_ROW_CHUNK,), jnp.int32)] * 2
        + [pltpu.VMEM((SC_ROW_CHUNK, width), table.dtype)] * 2
        + [pltpu.SemaphoreType.DMA] * 4,
    )
    def gather(table_hbm, idx_hbm, out_hbm, idx_a, idx_b, rows_a, rows_b, sem_ga, sem_gb, sem_wa, sem_wb):
        worker = lax.axis_index("s") * n_cores + lax.axis_index("c")
        base = worker * per_worker

        @pl.loop(0, per_worker // SC_ROW_CHUNK, step=2)
        def _(c):
            off_a = pl.multiple_of(base + c * SC_ROW_CHUNK, SC_ROW_CHUNK)
            off_b = pl.multiple_of(off_a + SC_ROW_CHUNK, SC_ROW_CHUNK)
            pltpu.sync_copy(idx_hbm.at[pl.ds(off_a, SC_ROW_CHUNK)], idx_a)
            pltpu.sync_copy(idx_hbm.at[pl.ds(off_b, SC_ROW_CHUNK)], idx_b)
            gather_a = pltpu.async_copy(table_hbm.at[idx_a], rows_a, sem_ga)
            gather_b = pltpu.async_copy(table_hbm.at[idx_b], rows_b, sem_gb)
            gather_a.wait()
            write_a = pltpu.async_copy(rows_a, out_hbm.at[pl.ds(off_a, SC_ROW_CHUNK)], sem_wa)
            gather_b.wait()
            write_b = pltpu.async_copy(rows_b, out_hbm.at[pl.ds(off_b, SC_ROW_CHUNK)], sem_wb)
            write_a.wait()
            write_b.wait()

    return gather(table, idx)


def _combine_kernel(x1_ref, rt_ref, y_ref, fg_ref, o_ref, *, final_norm):
    rt = rt_ref[...]
    y1 = _unpack_bf16_pairs(y_ref[0, 0]).astype(F32)
    y2 = _unpack_bf16_pairs(y_ref[0, 1]).astype(F32)
    h = x1_ref[...] + rt[:, 2:3] * y1 + rt[:, 3:4] * y2
    if final_norm:
        h = h * lax.rsqrt(jnp.mean(h * h, axis=-1, keepdims=True) + EPS) * fg_ref[...]
    o_ref[...] = h


def _combine_call(x1, rt, ysg, final_g, *, tm, final_norm):
    T, D = x1.shape
    return pl.pallas_call(
        functools.partial(_combine_kernel, final_norm=final_norm),
        grid=(T // tm,),
        in_specs=[
            pl.BlockSpec((tm, D), lambda t: (t, 0)),
            pl.BlockSpec((tm, ROUTER_COLS), lambda t: (t, 0)),
            pl.BlockSpec((1, TOP_K, tm, D // 2), lambda t: (t, 0, 0, 0)),
            pl.BlockSpec((1, D), lambda t: (0, 0)),
        ],
        out_specs=pl.BlockSpec((tm, D), lambda t: (t, 0)),
        out_shape=jax.ShapeDtypeStruct((T, D), F32),
        compiler_params=pltpu.CompilerParams(
            dimension_semantics=("parallel",), vmem_limit_bytes=40 * 1024 * 1024),
        name="moe_combine",
    )(x1, rt, ysg, final_g)


def _slot_layout(tile_counts, n_assign):
    NB = -(-n_assign // EXPERT_BLOCK) + N_EXPERTS
    n_tiles = tile_counts.shape[0]
    tc = tile_counts.astype(F32)
    hp = lax.Precision.HIGHEST
    counts = jnp.sum(tc, axis=0)
    padded = jnp.ceil(counts / EXPERT_BLOCK) * EXPERT_BLOCK
    upper = (jnp.arange(N_EXPERTS)[:, None] < jnp.arange(N_EXPERTS)[None, :]).astype(F32)
    pad_start = jnp.dot(padded, upper, precision=hp)
    pad_end = pad_start + padded
    lower = (jnp.arange(n_tiles)[:, None] > jnp.arange(n_tiles)[None, :]).astype(F32)
    tile_base = pad_start[None, :] + jnp.dot(lower, tc, precision=hp)
    block_start = jnp.arange(NB, dtype=F32) * EXPERT_BLOCK
    block_expert = jnp.minimum(jnp.sum((pad_end[None, :] <= block_start[:, None]).astype(jnp.int32), axis=1),
                               N_EXPERTS - 1)
    mine = block_expert[:, None] == jnp.arange(N_EXPERTS)[None, :]
    run_end = jnp.sum(jnp.where(mine, (pad_start + counts)[None, :], 0.0), axis=1)
    n_valid = jnp.clip(run_end - block_start, 0, EXPERT_BLOCK).astype(jnp.int32)
    return NB, block_expert.astype(jnp.int32), n_valid, tile_base


def _rope_tables(S):
    half = HEAD_DIM // 2
    inv = 1.0 / (ROPE_THETA ** (jnp.arange(0, HEAD_DIM, 2, dtype=F32) / HEAD_DIM))
    pos = jnp.arange(S, dtype=F32)
    ang_l = pos[:, None] * jnp.tile(inv, LANES // half)[None, :]
    sign = jnp.tile(jnp.concatenate([-jnp.ones((half,), F32), jnp.ones((half,), F32)]), LANES // HEAD_DIM)
    ang_t = inv[:, None] * pos[None, :]
    return jnp.cos(ang_l), jnp.sin(ang_l) * sign[None, :], jnp.cos(ang_t), jnp.sin(ang_t)


def kernel(x, norm1_g, w_in, lambda_q1, lambda_k1, lambda_q2, lambda_k2, subln_g, sinks, w_out,
           norm2_g, w_router_group, b_router_group, w_router_expert, b_router_expert,
           w_gate, w_up, w_down, final_g):
    B, S, D = x.shape
    T = B * S
    depth = w_in.shape[0]
    tq, tk = 512, 512
    tm_proj = 512
    tm_tok = 512
    tq_swa = 1024
    qscale = HEAD_DIM ** -0.5 * math.log2(math.e)
    cos_l, sin_l, cos_t, sin_t = _rope_tables(S)

    c0 = DIFF_QK_COLS
    c1 = 2 * DIFF_QK_COLS
    c2 = c1 + DIFF_V_COLS
    c3 = c2 + SWA_Q_COLS
    c4 = c3 + SWA_KV_COLS
    for l in range(depth):
        lambda_init = 0.8 - 0.6 * math.exp(-0.3 * l)
        w = w_in[l]
        w_nat = jnp.concatenate([w[:, c0:c1], w[:, c3:c4]], axis=1).astype(BF16)
        w_tr = jnp.concatenate([w[:, :c0] * qscale, w[:, c1:c2], w[:, c2:c3] * qscale, w[:, c4:]],
                               axis=1).T.astype(BF16)
        dqt, dk, dvt, sqt, sk, svt = _proj_call(
            x, norm1_g[l][None, :], w_nat, w_tr, cos_l, sin_l, cos_t, sin_t, tm=tm_proj, tk=tk)

        lam_p = jnp.stack([lambda_q1[l], lambda_k1[l], lambda_q2[l], lambda_k2[l]]).astype(F32)
        o_diff = _diff_call(lam_p, dqt, dk, dvt, subln_g[l][None, :].astype(F32),
                            tq=tq, tk=tk, lambda_init=lambda_init)
        sink_row = jnp.repeat(sinks[l].astype(F32) * math.log2(math.e), WINDOW)[None, :]
        o_swa = _swa_call(sink_row, sqt, sk, svt, tq=tq_swa)

        wo_b = w_out[l].astype(BF16)
        w_router = jnp.zeros((D, ROUTER_COLS), F32)
        w_router = w_router.at[:, :N_GROUPS].set(w_router_group[l])
        w_router = w_router.at[:, N_GROUPS:N_GROUPS + N_EXPERTS].set(w_router_expert[l])
        w_router_hi = w_router.astype(BF16)
        w_router_lo = (w_router - w_router_hi.astype(F32)).astype(BF16)
        w_router = jnp.concatenate([w_router_hi, w_router_lo], axis=1)
        b_router = jnp.zeros((1, ROUTER_COLS), F32)
        b_router = b_router.at[0, :N_GROUPS].set(b_router_group[l])
        b_router = b_router.at[0, N_GROUPS:N_GROUPS + N_EXPERTS].set(b_router_expert[l])
        x1, n2p, rt, cnt = _mix_call(x, o_diff, o_swa, wo_b, norm2_g[l][None, :], w_router, b_router, tm=tm_tok)

        rt2 = rt.reshape(T, ROUTER_COLS)
        tile_counts = cnt[:, :, 0, :N_EXPERTS].reshape(T // tm_tok, N_EXPERTS).astype(jnp.int32)
        NB, block_expert, n_valid, tile_base = _slot_layout(tile_counts, T * TOP_K)
        tile_base = jnp.broadcast_to(tile_base.astype(F32)[:, :, None], (T // tm_tok, N_EXPERTS, LANES))
        dest = _slot_call(rt2, tile_base, tm=tm_tok)
        scatter_idx = jnp.swapaxes(dest[:, :TOP_K, :], 0, 1).reshape(TOP_K, T // SC_ROW_CHUNK, SC_ROW_CHUNK)
        xs = _sc_scatter_rows(n2p.reshape(T, D // 2), scatter_idx, NB * EXPERT_BLOCK)
        ys = _expert_call(block_expert, n_valid, xs, w_gate[l], w_up[l], w_down[l])
        ysg = _sc_gather_rows(ys, dest[:, :TOP_K, :].reshape(T * TOP_K))
        x = _combine_call(x1.reshape(T, D), rt2, ysg.reshape(T // tm_tok, TOP_K, tm_tok, D // 2),
                          final_g[None, :], tm=tm_tok, final_norm=(l == depth - 1)).reshape(B, S, D)
    return x
```

```python
import functools
import math

import jax
import jax.numpy as jnp
from jax import lax
from jax.experimental import pallas as pl
from jax.experimental.pallas import tpu as pltpu
from jax.experimental.pallas import tpu_sc as plsc

HEAD_DIM = 64
DIFF_HEADS = 4
DIFF_V_DIM = 2 * HEAD_DIM
SWA_Q_HEADS = 8
SWA_KV_HEADS = 2
SWA_GROUP = SWA_Q_HEADS // SWA_KV_HEADS
WINDOW = 128
ROPE_THETA = 10000.0
N_GROUPS = 4
EXPERTS_PER_GROUP = 8
N_EXPERTS = N_GROUPS * EXPERTS_PER_GROUP
TOP_K = 2
EXPERT_BLOCK = 1024
EXPERT_GROUP = 512
EXPERT_CHUNK = 256
EPS = 1e-6
NEG = -1e30

DIFF_QK_COLS = DIFF_HEADS * 2 * HEAD_DIM
DIFF_V_COLS = DIFF_HEADS * DIFF_V_DIM
SWA_Q_COLS = SWA_Q_HEADS * HEAD_DIM
SWA_KV_COLS = SWA_KV_HEADS * HEAD_DIM
LANES = 128
SUBLANES = 8
BF16_SUBLANES = 16
VT_ROWS = DIFF_V_DIM + BF16_SUBLANES
SWA_VT_ROWS = SWA_KV_COLS + BF16_SUBLANES
ROUTER_COLS = LANES
DIFF_UNROLL = 4
DIFF_S_BUFS = 4
DIFF_Q_TILES = 2

BF16 = jnp.bfloat16
F32 = jnp.float32


def _rope_lanes(x, cos_l, sin_l, first_half):
    rot = jnp.where(first_half, pltpu.roll(x, 96, 1), pltpu.roll(x, 32, 1))
    return x * cos_l + rot * sin_l


def _proj_kernel(x_ref, g_ref, wnat_ref, wtr_ref, cosl_ref, sinl_ref, cost_ref, sint_ref,
                 dqt_ref, dk_ref, dvt_ref, sqt_ref, sk_ref, svt_ref, *, tk):
    x = x_ref[0]
    tm = x.shape[0]
    n1 = x * lax.rsqrt(jnp.mean(x * x, axis=-1, keepdims=True) + EPS) * g_ref[...]
    n1b = n1.astype(BF16)
    nat = jnp.dot(n1b, wnat_ref[...], preferred_element_type=F32)
    tr = lax.dot_general(wtr_ref[...], n1b, (((1,), (1,)), ((), ())),
                         preferred_element_type=F32)

    cos_l, sin_l = cosl_ref[...], sinl_ref[...]
    first_half = (lax.broadcasted_iota(jnp.int32, (tm, LANES), 1) & (HEAD_DIM - 1)) < HEAD_DIM // 2
    for h in range(DIFF_HEADS):
        slab = nat[:, h * LANES:(h + 1) * LANES]
        dk_ref[0, h] = _rope_lanes(slab, cos_l, sin_l, first_half).astype(BF16)
    sk = _rope_lanes(nat[:, DIFF_QK_COLS:DIFF_QK_COLS + LANES], cos_l, sin_l, first_half).astype(BF16)
    for c in range(tm // WINDOW):
        sk_ref[0, c] = sk[c * WINDOW:(c + 1) * WINDOW]

    cos_t, sin_t = cost_ref[...], sint_ref[...]
    half = HEAD_DIM // 2

    def rope_rows(r0):
        x1 = tr[r0:r0 + half]
        x2 = tr[r0 + half:r0 + HEAD_DIM]
        return (x1 * cos_t - x2 * sin_t).astype(BF16), (x1 * sin_t + x2 * cos_t).astype(BF16)

    for h in range(DIFF_HEADS):
        for c in range(2):
            lo, hi = rope_rows(h * 2 * HEAD_DIM + c * HEAD_DIM)
            dqt_ref[0, h, c * HEAD_DIM:c * HEAD_DIM + half] = lo
            dqt_ref[0, h, c * HEAD_DIM + half:(c + 1) * HEAD_DIM] = hi
    ones_rows = (lax.broadcasted_iota(jnp.int32, (BF16_SUBLANES, tk), 0) == 0).astype(BF16)
    for h in range(DIFF_HEADS):
        r0 = DIFF_QK_COLS + h * DIFF_V_DIM
        for c in range(tm // tk):
            dvt_ref[0, h, c, :DIFF_V_DIM] = tr[r0:r0 + DIFF_V_DIM, c * tk:(c + 1) * tk].astype(BF16)
            dvt_ref[0, h, c, DIFF_V_DIM:] = ones_rows

    r0 = DIFF_QK_COLS + DIFF_V_COLS
    for h in range(SWA_Q_HEADS):
        lo, hi = rope_rows(r0 + h * HEAD_DIM)
        sqt_ref[0, h * HEAD_DIM:h * HEAD_DIM + half] = lo
        sqt_ref[0, h * HEAD_DIM + half:(h + 1) * HEAD_DIM] = hi
    r0 += SWA_Q_COLS
    for c in range(tm // WINDOW):
        svt_ref[0, c, :SWA_KV_COLS] = tr[r0:r0 + SWA_KV_COLS, c * WINDOW:(c + 1) * WINDOW].astype(BF16)
        svt_ref[0, c, SWA_KV_COLS:] = ones_rows[:, :WINDOW]


def _proj_call(x, g1, w_nat, w_tr, cos_l, sin_l, cos_t, sin_t, *, tm, tk):
    B, S, D = x.shape
    nkv = S // tk
    grid = (B, S // tm)
    const = lambda b, i: (0, 0)
    out_shape = (
        jax.ShapeDtypeStruct((B, DIFF_HEADS, 2 * HEAD_DIM, S), BF16),
        jax.ShapeDtypeStruct((B, DIFF_HEADS, S, 2 * HEAD_DIM), BF16),
        jax.ShapeDtypeStruct((B, DIFF_HEADS, nkv, VT_ROWS, tk), BF16),
        jax.ShapeDtypeStruct((B, SWA_Q_COLS, S), BF16),
        jax.ShapeDtypeStruct((B, S // WINDOW, WINDOW, SWA_KV_COLS), BF16),
        jax.ShapeDtypeStruct((B, S // WINDOW, SWA_VT_ROWS, WINDOW), BF16),
    )
    return pl.pallas_call(
        functools.partial(_proj_kernel, tk=tk),
        grid=grid,
        in_specs=[
            pl.BlockSpec((1, tm, D), lambda b, i: (b, i, 0)),
            pl.BlockSpec((1, D), const),
            pl.BlockSpec(w_nat.shape, const),
            pl.BlockSpec(w_tr.shape, const),
            pl.BlockSpec((tm, LANES), lambda b, i: (i, 0)),
            pl.BlockSpec((tm, LANES), lambda b, i: (i, 0)),
            pl.BlockSpec((HEAD_DIM // 2, tm), lambda b, i: (0, i)),
            pl.BlockSpec((HEAD_DIM // 2, tm), lambda b, i: (0, i)),
        ],
        out_specs=(
            pl.BlockSpec((1, DIFF_HEADS, 2 * HEAD_DIM, tm), lambda b, i: (b, 0, 0, i)),
            pl.BlockSpec((1, DIFF_HEADS, tm, 2 * HEAD_DIM), lambda b, i: (b, 0, i, 0)),
            pl.BlockSpec((1, DIFF_HEADS, tm // tk, VT_ROWS, tk), lambda b, i: (b, 0, i, 0, 0)),
            pl.BlockSpec((1, SWA_Q_COLS, tm), lambda b, i: (b, 0, i)),
            pl.BlockSpec((1, tm // WINDOW, WINDOW, SWA_KV_COLS), lambda b, i: (b, i, 0, 0)),
            pl.BlockSpec((1, tm // WINDOW, SWA_VT_ROWS, WINDOW), lambda b, i: (b, i, 0, 0)),
        ),
        out_shape=out_shape,
        compiler_params=pltpu.CompilerParams(
            dimension_semantics=("parallel", "parallel"), vmem_limit_bytes=48 * 1024 * 1024),
        name="proj_rope",
    )(x, g1, w_nat, w_tr, cos_l, sin_l, cos_t, sin_t)


def _diff_kernel(lam_ref, qt_ref, k_ref, vt_ref, g_ref, o_ref, *scratch, tq, tk, lambda_init):
    step = pl.program_id(2)
    s_bufs = scratch[:DIFF_S_BUFS]
    top_bufs = scratch[DIFF_S_BUFS:2 * DIFF_S_BUFS]
    state = scratch[2 * DIFF_S_BUFS:2 * DIFF_S_BUFS + 2 * DIFF_Q_TILES]
    bias_ref = scratch[-1]

    @pl.when(step == 0)
    def _():
        r = lax.broadcasted_iota(jnp.int32, (tk, 2 * tq), 0)
        c = lax.broadcasted_iota(jnp.int32, (tk, 2 * tq), 1) & (tq - 1)
        bias_ref[...] = jnp.where(r <= c, 0.0, NEG).astype(F32)

    lam_p = lam_ref[...]
    lam = (jnp.exp(jnp.sum(lam_p[0:1] * lam_p[1:2], axis=-1, keepdims=True))
           - jnp.exp(jnp.sum(lam_p[2:3] * lam_p[3:4], axis=-1, keepdims=True)) + lambda_init)

    for sub in range(DIFF_Q_TILES):
        _diff_query_tile(step * DIFF_Q_TILES + sub, qt_ref[0, 0, :, sub * tq:(sub + 1) * tq], k_ref, vt_ref,
                         g_ref, o_ref.at[0, pl.ds(sub * tq, tq), :], s_bufs, top_bufs,
                         state[2 * sub], state[2 * sub + 1], bias_ref, lam,
                         tq=tq, tk=tk, lambda_init=lambda_init)


def _diff_query_tile(i, qt, k_ref, vt_ref, g_ref, o_ref, s_bufs, top_bufs, m_ref, acc_ref, bias_ref, lam,
                     *, tq, tk, lambda_init):
    z = jnp.zeros((HEAD_DIM, tq), BF16)
    qw = jnp.concatenate([jnp.concatenate([qt[:HEAD_DIM], z], axis=1),
                          jnp.concatenate([z, qt[HEAD_DIM:]], axis=1)], axis=0)

    def scores(j, par):
        kt = k_ref[0, 0, pl.ds(pl.multiple_of(j * tk, tk), tk), :]
        s = jnp.dot(kt, qw, preferred_element_type=F32)
        s_bufs[par][...] = s
        top_bufs[par][...] = jnp.max(s, axis=0, keepdims=True)

    def absorb(j, par, masked):
        s = s_bufs[par][...]
        if masked:
            s = s + bias_ref[...]
            top = jnp.max(s, axis=0, keepdims=True)
        else:
            top = top_bufs[par][...]
        m = m_ref[...]
        m_new = jnp.maximum(m, top)
        alpha = jnp.exp2(m - m_new)
        p = jnp.exp2(s - m_new).astype(BF16)
        m_ref[...] = m_new
        pv = jnp.dot(vt_ref[0, 0, j], p, preferred_element_type=F32)
        acc_ref[...] = alpha * acc_ref[...] + pv

    m_ref[...] = jnp.full(m_ref.shape, NEG, F32)
    acc_ref[...] = jnp.zeros(acc_ref.shape, F32)

    nfull = (i * tq) // tk
    scores(nfull, 0)
    scores(0, 1)
    absorb(nfull, 0, True)

    def group(t, c):
        j = DIFF_UNROLL * t
        for idx in range(DIFF_UNROLL):
            scores(j + idx + 1, (idx + 2) % DIFF_S_BUFS)
            absorb(j + idx, (idx + 1) % DIFF_S_BUFS, False)
        return c

    lax.fori_loop(0, nfull // DIFF_UNROLL, group, 0)

    for rem in range(1, DIFF_UNROLL):
        @pl.when(nfull % DIFF_UNROLL == rem)
        def _():
            first = nfull - rem
            for idx in range(rem):
                if idx + 1 < rem:
                    scores(first + idx + 1, (idx + 2) % DIFF_S_BUFS)
                absorb(first + idx, (idx + 1) % DIFF_S_BUFS, False)

    inv_l = 1.0 / acc_ref[DIFF_V_DIM:DIFF_V_DIM + 1, :]
    o = (acc_ref[:DIFF_V_DIM, :tq] * inv_l[:, :tq]
         - lam * (acc_ref[:DIFF_V_DIM, tq:] * inv_l[:, tq:]))
    o = o * lax.rsqrt(jnp.mean(o * o, axis=0, keepdims=True) + EPS)
    o_ref[...] = (o.T * g_ref[...] * (1.0 - lambda_init)).astype(BF16)


def _diff_call(lam_p, dqt, dk, dvt, subln_g, *, tq, tk, lambda_init):
    B, H, _, S = dqt.shape
    assert tk == tq and S % tk == 0, "the diagonal tile's causal pattern is built for square tiles"
    nkv = S // tk
    tq_step = DIFF_Q_TILES * tq
    assert S % tq_step == 0
    grid = (B, H, S // tq_step)
    return pl.pallas_call(
        functools.partial(_diff_kernel, tq=tq, tk=tk, lambda_init=lambda_init),
        grid=grid,
        in_specs=[
            pl.BlockSpec(lam_p.shape, lambda b, h, i: (0, 0)),
            pl.BlockSpec((1, 1, 2 * HEAD_DIM, tq_step), lambda b, h, i: (b, h, 0, i)),
            pl.BlockSpec((1, 1, S, 2 * HEAD_DIM), lambda b, h, i: (b, h, 0, 0)),
            pl.BlockSpec((1, 1, nkv, VT_ROWS, tk), lambda b, h, i: (b, h, 0, 0, 0)),
            pl.BlockSpec((1, DIFF_V_DIM), lambda b, h, i: (0, 0)),
        ],
        out_specs=pl.BlockSpec((1, tq_step, DIFF_V_DIM), lambda b, h, i: (b, i, h)),
        out_shape=jax.ShapeDtypeStruct((B, S, DIFF_V_COLS), BF16),
        scratch_shapes=[pltpu.VMEM((tk, 2 * tq), F32)] * DIFF_S_BUFS + [
            pltpu.VMEM((1, 2 * tq), F32)] * DIFF_S_BUFS + [
            pltpu.VMEM((1, 2 * tq), F32),
            pltpu.VMEM((VT_ROWS, 2 * tq), F32)] * DIFF_Q_TILES + [
            pltpu.VMEM((tk, 2 * tq), F32),
        ],
        compiler_params=pltpu.CompilerParams(
            dimension_semantics=("parallel", "parallel", "arbitrary"),
            vmem_limit_bytes=48 * 1024 * 1024),
        name="diff_attn",
    )(lam_p, dqt, dk, dvt, subln_g)


def _swa_kernel(sink_ref, qt_ref, k_ref, vt_ref, o_ref, *, tq):
    i = pl.program_id(1)
    n_cols = SWA_Q_HEADS * WINDOW
    half_cols = n_cols // SWA_KV_HEADS
    sink = sink_ref[...]
    row = lax.broadcasted_iota(jnp.int32, (2 * WINDOW, WINDOW), 0)
    qrel = lax.broadcasted_iota(jnp.int32, (2 * WINDOW, WINDOW), 1)
    band = (row - WINDOW <= qrel) & (row > qrel)
    in_current = row >= WINDOW
    z = jnp.zeros((HEAD_DIM, half_cols), BF16)
    for sub in range(tq // WINDOW):
        n = i * (tq // WINDOW) + sub
        prev = jnp.maximum(n - 1, 0)
        kwin = jnp.concatenate([k_ref[0, prev], k_ref[0, n]], axis=0)
        vtwin = jnp.concatenate([vt_ref[0, prev], vt_ref[0, n]], axis=1)
        qt = qt_ref[0, :, sub * WINDOW:(sub + 1) * WINDOW]
        heads = [qt[h * HEAD_DIM:(h + 1) * HEAD_DIM] for h in range(SWA_Q_HEADS)]
        qw = jnp.concatenate(
            [jnp.concatenate(heads[:SWA_GROUP] + [z], axis=1),
             jnp.concatenate([z] + heads[SWA_GROUP:], axis=1)], axis=0)
        s = jnp.dot(kwin, qw, preferred_element_type=F32)
        valid = band & (in_current | (n >= 1))
        s = jnp.concatenate(
            [jnp.where(valid, s[:, h * WINDOW:(h + 1) * WINDOW], NEG) for h in range(SWA_Q_HEADS)], axis=1)
        m = jnp.maximum(jnp.max(s, axis=0, keepdims=True), sink)
        p = jnp.exp2(s - m).astype(BF16)
        acc = jnp.dot(vtwin, p, preferred_element_type=F32)
        den = acc[SWA_KV_COLS:SWA_KV_COLS + 1] + jnp.exp2(sink - m)
        on = acc[:SWA_KV_COLS] / den
        u = jnp.concatenate([on[:HEAD_DIM, :half_cols], on[HEAD_DIM:, half_cols:]], axis=1)
        for hp in range(SWA_Q_HEADS // 2):
            two = jnp.concatenate([u[:, (2 * hp) * WINDOW:(2 * hp + 1) * WINDOW],
                                   u[:, (2 * hp + 1) * WINDOW:(2 * hp + 2) * WINDOW]], axis=0)
            o_ref[0, sub * WINDOW:(sub + 1) * WINDOW, hp * LANES:(hp + 1) * LANES] = two.T.astype(BF16)


def _swa_call(sink_row, sqt, sk, svt, *, tq):
    B, _, S = sqt.shape
    nb = S // WINDOW
    return pl.pallas_call(
        functools.partial(_swa_kernel, tq=tq),
        grid=(B, S // tq),
        in_specs=[
            pl.BlockSpec(sink_row.shape, lambda b, i: (0, 0)),
            pl.BlockSpec((1, SWA_Q_COLS, tq), lambda b, i: (b, 0, i)),
            pl.BlockSpec((1, nb, WINDOW, SWA_KV_COLS), lambda b, i: (b, 0, 0, 0)),
            pl.BlockSpec((1, nb, SWA_VT_ROWS, WINDOW), lambda b, i: (b, 0, 0, 0)),
        ],
        out_specs=pl.BlockSpec((1, tq, SWA_Q_COLS), lambda b, i: (b, i, 0)),
        out_shape=jax.ShapeDtypeStruct((B, S, SWA_Q_COLS), BF16),
        compiler_params=pltpu.CompilerParams(
            dimension_semantics=("parallel", "arbitrary"), vmem_limit_bytes=40 * 1024 * 1024),
        name="swa_attn",
    )(sink_row, sqt, sk, svt)


def _pack_bf16_pairs(x):
    n = x.shape[1] // 2
    lo = lax.bitcast_convert_type(x[:, :n].astype(BF16).astype(F32), jnp.uint32)
    hi = lax.bitcast_convert_type(x[:, n:].astype(BF16).astype(F32), jnp.uint32)
    return (lo >> 16) | (hi & jnp.uint32(0xFFFF0000))


def _unpack_bf16_pairs(w):
    lo = lax.bitcast_convert_type(w << 16, F32)
    hi = lax.bitcast_convert_type(w & jnp.uint32(0xFFFF0000), F32)
    return jnp.concatenate([lo, hi], axis=1).astype(BF16)


MIX_CHUNKS = 1


def _mix_kernel(x_ref, od_ref, os_ref, wo_ref, g2_ref, wr_ref, br_ref, x1_ref, n2_ref, rt_ref, cnt_ref):
    tm = x_ref.shape[1] // MIX_CHUNKS
    lane = lax.broadcasted_iota(jnp.int32, (tm, ROUTER_COLS), 1)
    lane_f = lane.astype(F32)
    big = float(ROUTER_COLS)
    counts = jnp.zeros((1, ROUTER_COLS), F32)
    mixed = jnp.concatenate([od_ref[0], os_ref[0]], axis=1)
    x1_ref[0] = x_ref[0] + jnp.dot(mixed, wo_ref[...], preferred_element_type=F32)
    for c in range(MIX_CHUNKS):
        rows = pl.ds(c * tm, tm)
        h = x1_ref[0, rows, :]
        n2 = h * lax.rsqrt(jnp.mean(h * h, axis=-1, keepdims=True) + EPS) * g2_ref[...]
        n2_ref[0, rows, :] = _pack_bf16_pairs(n2)
        n2_hi = n2.astype(BF16)
        n2_lo = (n2 - n2_hi.astype(F32)).astype(BF16)
        parts = jnp.dot(jnp.concatenate([n2_hi, n2_lo], axis=0), wr_ref[...],
                        preferred_element_type=F32)
        logits = ((parts[:tm, :ROUTER_COLS] + parts[tm:, ROUTER_COLS:])
                  + (parts[:tm, ROUTER_COLS:] + parts[tm:, :ROUTER_COLS])) + br_ref[...]
        gl = jnp.where(lane < N_GROUPS, logits, -jnp.inf)
        gm = jnp.max(gl, axis=-1, keepdims=True)
        p_top = 1.0 / jnp.sum(jnp.exp(gl - gm), axis=-1, keepdims=True)
        g_idx = jnp.min(jnp.where(gl == gm, lane_f, big), axis=-1, keepdims=True)
        e_lo = N_GROUPS + EXPERTS_PER_GROUP * g_idx
        el = jnp.where((lane_f >= e_lo) & (lane_f < e_lo + EXPERTS_PER_GROUP), logits, -jnp.inf)
        v1 = jnp.max(el, axis=-1, keepdims=True)
        i1 = jnp.min(jnp.where(el == v1, lane_f, big), axis=-1, keepdims=True)
        el2 = jnp.where(lane_f == i1, -jnp.inf, el)
        v2 = jnp.max(el2, axis=-1, keepdims=True)
        i2 = jnp.min(jnp.where(el2 == v2, lane_f, big), axis=-1, keepdims=True)
        e21 = jnp.exp(v2 - v1)
        gate1 = p_top / (1.0 + e21)
        gate2 = p_top * e21 / (1.0 + e21)
        rt_ref[0, rows, :] = jnp.where(lane == 0, i1 - N_GROUPS,
                             jnp.where(lane == 1, i2 - N_GROUPS,
                             jnp.where(lane == 2, gate1, jnp.where(lane == 3, gate2, 0.0))))
        chosen = ((lane_f == i1 - N_GROUPS) | (lane_f == i2 - N_GROUPS)).astype(F32)
        counts = counts + jnp.sum(chosen, axis=0, keepdims=True)
    cnt_ref[0, 0] = jnp.broadcast_to(counts, cnt_ref.shape[2:])


def _mix_call(x, o_diff, o_swa, w_out, g2, w_router, b_router, *, tm):
    B, S, D = x.shape
    const = lambda b, i: (0, 0)
    row = lambda b, i: (b, i, 0)
    nt = S // tm
    return pl.pallas_call(
        _mix_kernel,
        grid=(B, nt),
        in_specs=[
            pl.BlockSpec((1, tm, D), row),
            pl.BlockSpec((1, tm, DIFF_V_COLS), row),
            pl.BlockSpec((1, tm, SWA_Q_COLS), row),
            pl.BlockSpec(w_out.shape, const),
            pl.BlockSpec((1, D), const),
            pl.BlockSpec(w_router.shape, const),
            pl.BlockSpec((1, ROUTER_COLS), const),
        ],
        out_specs=(pl.BlockSpec((1, tm, D), row), pl.BlockSpec((1, tm, D // 2), row),
                   pl.BlockSpec((1, tm, ROUTER_COLS), row),
                   pl.BlockSpec((1, 1, SUBLANES, ROUTER_COLS), lambda b, i: (b, i, 0, 0))),
        out_shape=(jax.ShapeDtypeStruct((B, S, D), F32), jax.ShapeDtypeStruct((B, S, D // 2), jnp.uint32),
                   jax.ShapeDtypeStruct((B, S, ROUTER_COLS), F32),
                   jax.ShapeDtypeStruct((B, nt, SUBLANES, ROUTER_COLS), F32)),
        compiler_params=pltpu.CompilerParams(
            dimension_semantics=("parallel", "parallel"), vmem_limit_bytes=48 * 1024 * 1024),
        name="outproj_router",
    )(x, o_diff, o_swa, w_out, g2, w_router, b_router)


def _slot_kernel(rt_ref, base_ref, dest_ref, *, tm):
    rt_t = rt_ref[...].T
    e1 = rt_t[0:1].astype(jnp.int32)
    e2 = rt_t[1:2].astype(jnp.int32)
    eid = lax.broadcasted_iota(jnp.int32, (N_EXPERTS, tm), 0)
    oh1 = eid == e1
    oh2 = eid == e2
    earlier = (lax.broadcasted_iota(jnp.int32, (tm, tm), 0)
               < lax.broadcasted_iota(jnp.int32, (tm, tm), 1)).astype(BF16)
    before = jnp.dot((oh1 | oh2).astype(BF16), earlier, preferred_element_type=F32)
    slot = before + base_ref[0][:, 0:1]
    d1 = jnp.sum(jnp.where(oh1, slot, 0.0), axis=0, keepdims=True).astype(jnp.int32)
    d2 = jnp.sum(jnp.where(oh2, slot, 0.0), axis=0, keepdims=True).astype(jnp.int32)
    dest_ref[0] = jnp.concatenate([d1, d2, jnp.zeros((SUBLANES - TOP_K, tm), jnp.int32)], axis=0)


def _slot_call(rt, tile_base, *, tm):
    nt = rt.shape[0] // tm
    return pl.pallas_call(
        functools.partial(_slot_kernel, tm=tm),
        grid=(nt,),
        in_specs=[
            pl.BlockSpec((tm, ROUTER_COLS), lambda t: (t, 0)),
            pl.BlockSpec((1, N_EXPERTS, LANES), lambda t: (t, 0, 0)),
        ],
        out_specs=pl.BlockSpec((1, SUBLANES, tm), lambda t: (t, 0, 0)),
        out_shape=jax.ShapeDtypeStruct((nt, SUBLANES, tm), jnp.int32),
        compiler_params=pltpu.CompilerParams(dimension_semantics=("parallel",)),
        name="moe_slots",
    )(rt, tile_base)


SC_ROW_CHUNK = 64


def _sc_workers():
    info = plsc.get_sparse_core_info()
    return info.num_cores, info.num_cores * info.num_subcores


def _sc_scatter_rows(rows, idx, n_out):
    n, width = rows.shape
    n_cores, n_workers = _sc_workers()
    n_chunks = n // SC_ROW_CHUNK
    per_worker = n_chunks // n_workers
    assert n_chunks % n_workers == 0
    mesh = plsc.VectorSubcoreMesh(core_axis_name="c", subcore_axis_name="s")

    @functools.partial(
        pl.kernel, mesh=mesh,
        out_type=jax.ShapeDtypeStruct((n_out, width), rows.dtype),
        scratch_types=[
            pltpu.VMEM((SC_ROW_CHUNK,), jnp.int32),
            pltpu.VMEM((SC_ROW_CHUNK, width), rows.dtype),
        ],
    )
    def scatter(rows_hbm, idx_hbm, out_hbm, idx_v, rows_v):
        worker = lax.axis_index("s") * n_cores + lax.axis_index("c")

        @pl.loop(0, per_worker)
        def _(i):
            c = worker * per_worker + i
            pltpu.sync_copy(rows_hbm.at[pl.ds(pl.multiple_of(c * SC_ROW_CHUNK, SC_ROW_CHUNK), SC_ROW_CHUNK)], rows_v)
            for k in range(TOP_K):
                pltpu.sync_copy(idx_hbm.at[k, c], idx_v)
                pltpu.sync_copy(rows_v, out_hbm.at[idx_v])

    return scatter(rows, idx)


def _expert_kernel(be_ref, nvalid_ref, xs_ref, wg_ref, wu_ref, wd_ref, y_ref, wg_b, wu_b, wd_b):
    b = pl.program_id(0)
    n_valid = nvalid_ref[b]

    @pl.when(n_valid > 0)
    def _():
        @pl.when((b == 0) | (be_ref[b] != be_ref[jnp.maximum(b - 1, 0)]))
        def _():
            wg_b[...] = wg_ref[0].astype(BF16)
            wu_b[...] = wu_ref[0].astype(BF16)
            wd_b[...] = wd_ref[0].astype(BF16)

    for g in range(EXPERT_BLOCK // EXPERT_GROUP):
        g0 = g * EXPERT_GROUP

        @pl.when(n_valid > g0)
        def _():
            for c in range(EXPERT_GROUP // EXPERT_CHUNK):
                r0 = g0 + c * EXPERT_CHUNK
                rows = pl.ds(r0, EXPERT_CHUNK)
                row_id = r0 + lax.broadcasted_iota(jnp.int32, (EXPERT_CHUNK, xs_ref.shape[1]), 0)
                packed = jnp.where(row_id < n_valid, xs_ref[rows, :], jnp.uint32(0))
                xb = _unpack_bf16_pairs(packed)
                gate = jnp.dot(xb, wg_b[...], preferred_element_type=F32)
                up = jnp.dot(xb, wu_b[...], preferred_element_type=F32)
                hid = (gate * jax.nn.sigmoid(gate) * up).astype(BF16)
                y_ref[rows, :] = _pack_bf16_pairs(jnp.dot(hid, wd_b[...], preferred_element_type=F32))

        @pl.when(n_valid <= g0)
        def _():
            y_ref[pl.ds(g0, EXPERT_GROUP), :] = jnp.zeros((EXPERT_GROUP, y_ref.shape[1]), y_ref.dtype)


def _expert_call(block_expert, n_valid, xs, w_gate, w_up, w_down):
    P = xs.shape[0]
    NB = P // EXPERT_BLOCK
    E, D, F = w_gate.shape
    grid_spec = pltpu.PrefetchScalarGridSpec(
        num_scalar_prefetch=2,
        grid=(NB,),
        in_specs=[
            pl.BlockSpec((EXPERT_BLOCK,) + xs.shape[1:], lambda b, be, nu: (b, 0)),
            pl.BlockSpec((1, D, F), lambda b, be, nu: (be[b], 0, 0)),
            pl.BlockSpec((1, D, F), lambda b, be, nu: (be[b], 0, 0)),
            pl.BlockSpec((1, F, D), lambda b, be, nu: (be[b], 0, 0)),
        ],
        out_specs=pl.BlockSpec((EXPERT_BLOCK, D // 2), lambda b, be, nu: (b, 0)),
        scratch_shapes=[
            pltpu.VMEM((D, F), BF16),
            pltpu.VMEM((D, F), BF16),
            pltpu.VMEM((F, D), BF16),
        ],
    )
    return pl.pallas_call(
        _expert_kernel,
        grid_spec=grid_spec,
        out_shape=jax.ShapeDtypeStruct((P, D // 2), jnp.uint32),
        compiler_params=pltpu.CompilerParams(
            dimension_semantics=("arbitrary",), vmem_limit_bytes=48 * 1024 * 1024),
        name="moe_experts",
    )(block_expert, n_valid, xs, w_gate, w_up, w_down)


def _sc_gather_rows(table, idx):
    n_rows, width = idx.shape[0], table.shape[1]
    n_cores, n_workers = _sc_workers()
    per_worker = n_rows // n_workers
    assert n_rows % (n_workers * 2 * SC_ROW_CHUNK) == 0
    mesh = plsc.VectorSubcoreMesh(core_axis_name="c", subcore_axis_name="s")

    @functools.partial(
        pl.kernel, mesh=mesh,
        out_type=jax.ShapeDtypeStruct((n_rows, width), table.dtype),
        scratch_types=[pltpu.VMEM((SC_ROW_CHUNK,), jnp.int32)] * 2
        + [pltpu.VMEM((SC_ROW_CHUNK, width), table.dtype)] * 2
        + [pltpu.SemaphoreType.DMA] * 4,
    )
    def gather(table_hbm, idx_hbm, out_hbm, idx_a, idx_b, rows_a, rows_b, sem_ga, sem_gb, sem_wa, sem_wb):
        worker = lax.axis_index("s") * n_cores + lax.axis_index("c")
        base = worker * per_worker

        @pl.loop(0, per_worker // SC_ROW_CHUNK, step=2)
        def _(c):
            off_a = pl.multiple_of(base + c * SC_ROW_CHUNK, SC_ROW_CHUNK)
            off_b = pl.multiple_of(off_a + SC_ROW_CHUNK, SC_ROW_CHUNK)
            pltpu.sync_copy(idx_hbm.at[pl.ds(off_a, SC_ROW_CHUNK)], idx_a)
            pltpu.sync_copy(idx_hbm.at[pl.ds(off_b, SC_ROW_CHUNK)], idx_b)
            gather_a = pltpu.async_copy(table_hbm.at[idx_a], rows_a, sem_ga)
            gather_b = pltpu.async_copy(table_hbm.at[idx_b], rows_b, sem_gb)
            gather_a.wait()
            write_a = pltpu.async_copy(rows_a, out_hbm.at[pl.ds(off_a, SC_ROW_CHUNK)], sem_wa)
            gather_b.wait()
            write_b = pltpu.async_copy(rows_b, out_hbm.at[pl.ds(off_b, SC_ROW_CHUNK)], sem_wb)
            write_a.wait()
            write_b.wait()

    return gather(table, idx)


def _combine_kernel(x1_ref, rt_ref, y_ref, fg_ref, o_ref, *, final_norm):
    rt = rt_ref[...]
    y1 = _unpack_bf16_pairs(y_ref[0, 0]).astype(F32)
    y2 = _unpack_bf16_pairs(y_ref[0, 1]).astype(F32)
    h = x1_ref[...] + rt[:, 2:3] * y1 + rt[:, 3:4] * y2
    if final_norm:
        h = h * lax.rsqrt(jnp.mean(h * h, axis=-1, keepdims=True) + EPS) * fg_ref[...]
    o_ref[...] = h


def _combine_call(x1, rt, ysg, final_g, *, tm, final_norm):
    T, D = x1.shape
    return pl.pallas_call(
        functools.partial(_combine_kernel, final_norm=final_norm),
        grid=(T // tm,),
        in_specs=[
            pl.BlockSpec((tm, D), lambda t: (t, 0)),
            pl.BlockSpec((tm, ROUTER_COLS), lambda t: (t, 0)),
            pl.BlockSpec((1, TOP_K, tm, D // 2), lambda t: (t, 0, 0, 0)),
            pl.BlockSpec((1, D), lambda t: (0, 0)),
        ],
        out_specs=pl.BlockSpec((tm, D), lambda t: (t, 0)),
        out_shape=jax.ShapeDtypeStruct((T, D), F32),
        compiler_params=pltpu.CompilerParams(
            dimension_semantics=("parallel",), vmem_limit_bytes=40 * 1024 * 1024),
        name="moe_combine",
    )(x1, rt, ysg, final_g)


def _slot_layout(tile_counts, n_assign):
    NB = -(-n_assign // EXPERT_BLOCK) + N_EXPERTS
    n_tiles = tile_counts.shape[0]
    tc = tile_counts.astype(F32)
    hp = lax.Precision.HIGHEST
    counts = jnp.sum(tc, axis=0)
    padded = jnp.ceil(counts / EXPERT_BLOCK) * EXPERT_BLOCK
    upper = (jnp.arange(N_EXPERTS)[:, None] < jnp.arange(N_EXPERTS)[None, :]).astype(F32)
    pad_start = jnp.dot(padded, upper, precision=hp)
    pad_end = pad_start + padded
    lower = (jnp.arange(n_tiles)[:, None] > jnp.arange(n_tiles)[None, :]).astype(F32)
    tile_base = pad_start[None, :] + jnp.dot(lower, tc, precision=hp)
    block_start = jnp.arange(NB, dtype=F32) * EXPERT_BLOCK
    block_expert = jnp.minimum(jnp.sum((pad_end[None, :] <= block_start[:, None]).astype(jnp.int32), axis=1),
                               N_EXPERTS - 1)
    mine = block_expert[:, None] == jnp.arange(N_EXPERTS)[None, :]
    run_end = jnp.sum(jnp.where(mine, (pad_start + counts)[None, :], 0.0), axis=1)
    n_valid = jnp.clip(run_end - block_start, 0, EXPERT_BLOCK).astype(jnp.int32)
    return NB, block_expert.astype(jnp.int32), n_valid, tile_base


def _rope_tables(S):
    half = HEAD_DIM // 2
    inv = 1.0 / (ROPE_THETA ** (jnp.arange(0, HEAD_DIM, 2, dtype=F32) / HEAD_DIM))
    pos = jnp.arange(S, dtype=F32)
    ang_l = pos[:, None] * jnp.tile(inv, LANES // half)[None, :]
    sign = jnp.tile(jnp.concatenate([-jnp.ones((half,), F32), jnp.ones((half,), F32)]), LANES // HEAD_DIM)
    ang_t = inv[:, None] * pos[None, :]
    return jnp.cos(ang_l), jnp.sin(ang_l) * sign[None, :], jnp.cos(ang_t), jnp.sin(ang_t)


def kernel(x, norm1_g, w_in, lambda_q1, lambda_k1, lambda_q2, lambda_k2, subln_g, sinks, w_out,
           norm2_g, w_router_group, b_router_group, w_router_expert, b_router_expert,
           w_gate, w_up, w_down, final_g):
    B, S, D = x.shape
    T = B * S
    depth = w_in.shape[0]
    tq, tk = 512, 512
    tm_proj = 512
    tm_tok = 512
    tq_swa = 1024
    qscale = HEAD_DIM ** -0.5 * math.log2(math.e)
    cos_l, sin_l, cos_t, sin_t = _rope_tables(S)

    c0 = DIFF_QK_COLS
    c1 = 2 * DIFF_QK_COLS
    c2 = c1 + DIFF_V_COLS
    c3 = c2 + SWA_Q_COLS
    c4 = c3 + SWA_KV_COLS
    for l in range(depth):
        lambda_init = 0.8 - 0.6 * math.exp(-0.3 * l)
        w = w_in[l]
        w_nat = jnp.concatenate([w[:, c0:c1], w[:, c3:c4]], axis=1).astype(BF16)
        w_tr = jnp.concatenate([w[:, :c0] * qscale, w[:, c1:c2], w[:, c2:c3] * qscale, w[:, c4:]],
                               axis=1).T.astype(BF16)
        dqt, dk, dvt, sqt, sk, svt = _proj_call(
            x, norm1_g[l][None, :], w_nat, w_tr, cos_l, sin_l, cos_t, sin_t, tm=tm_proj, tk=tk)

        lam_p = jnp.stack([lambda_q1[l], lambda_k1[l], lambda_q2[l], lambda_k2[l]]).astype(F32)
        o_diff = _diff_call(lam_p, dqt, dk, dvt, subln_g[l][None, :].astype(F32),
                            tq=tq, tk=tk, lambda_init=lambda_init)
        sink_row = jnp.repeat(sinks[l].astype(F32) * math.log2(math.e), WINDOW)[None, :]
        o_swa = _swa_call(sink_row, sqt, sk, svt, tq=tq_swa)

        wo_b = w_out[l].astype(BF16)
        w_router = jnp.zeros((D, ROUTER_COLS), F32)
        w_router = w_router.at[:, :N_GROUPS].set(w_router_group[l])
        w_router = w_router.at[:, N_GROUPS:N_GROUPS + N_EXPERTS].set(w_router_expert[l])
        w_router_hi = w_router.astype(BF16)
        w_router_lo = (w_router - w_router_hi.astype(F32)).astype(BF16)
        w_router = jnp.concatenate([w_router_hi, w_router_lo], axis=1)
        b_router = jnp.zeros((1, ROUTER_COLS), F32)
        b_router = b_router.at[0, :N_GROUPS].set(b_router_group[l])
        b_router = b_router.at[0, N_GROUPS:N_GROUPS + N_EXPERTS].set(b_router_expert[l])
        x1, n2p, rt, cnt = _mix_call(x, o_diff, o_swa, wo_b, norm2_g[l][None, :], w_router, b_router, tm=tm_tok)

        rt2 = rt.reshape(T, ROUTER_COLS)
        tile_counts = cnt[:, :, 0, :N_EXPERTS].reshape(T // tm_tok, N_EXPERTS).astype(jnp.int32)
        NB, block_expert, n_valid, tile_base = _slot_layout(tile_counts, T * TOP_K)
        tile_base = jnp.broadcast_to(tile_base.astype(F32)[:, :, None], (T // tm_tok, N_EXPERTS, LANES))
        dest = _slot_call(rt2, tile_base, tm=tm_tok)
        scatter_idx = jnp.swapaxes(dest[:, :TOP_K, :], 0, 1).reshape(TOP_K, T // SC_ROW_CHUNK, SC_ROW_CHUNK)
        xs = _sc_scatter_rows(n2p.reshape(T, D // 2), scatter_idx, NB * EXPERT_BLOCK)
        ys = _expert_call(block_expert, n_valid, xs, w_gate[l], w_up[l], w_down[l])
        ysg = _sc_gather_rows(ys, dest[:, :TOP_K, :].reshape(T * TOP_K))
        x = _combine_call(x1.reshape(T, D), rt2, ysg.reshape(T // tm_tok, TOP_K, tm_tok, D // 2),
                          final_g[None, :], tm=tm_tok, final_norm=(l == depth - 1)).reshape(B, S, D)
    return x
```

```python
import functools
import math

import jax
import jax.numpy as jnp
from jax import lax
from jax.experimental import pallas as pl
from jax.experimental.pallas import tpu as pltpu
from jax.experimental.pallas import tpu_sc as plsc

HEAD_DIM = 64
DIFF_HEADS = 4
DIFF_V_DIM = 2 * HEAD_DIM
SWA_Q_HEADS = 8
SWA_KV_HEADS = 2
SWA_GROUP = SWA_Q_HEADS // SWA_KV_HEADS
WINDOW = 128
ROPE_THETA = 10000.0
N_GROUPS = 4
EXPERTS_PER_GROUP = 8
N_EXPERTS = N_GROUPS * EXPERTS_PER_GROUP
TOP_K = 2
EXPERT_BLOCK = 512
EXPERT_CHUNK = 256
EPS = 1e-6
NEG = -1e30

DIFF_QK_COLS = DIFF_HEADS * 2 * HEAD_DIM
DIFF_V_COLS = DIFF_HEADS * DIFF_V_DIM
SWA_Q_COLS = SWA_Q_HEADS * HEAD_DIM
SWA_KV_COLS = SWA_KV_HEADS * HEAD_DIM
LANES = 128
SUBLANES = 8
BF16_SUBLANES = 16
VMEM_LIMIT_BYTES = 48 * 1024 * 1024
VT_ROWS = DIFF_V_DIM + BF16_SUBLANES
SWA_VT_ROWS = SWA_KV_COLS + BF16_SUBLANES
ROUTER_COLS = LANES
DIFF_UNROLL = 4
DIFF_S_BUFS = 4
DIFF_Q_TILES = 2

BF16 = jnp.bfloat16
F32 = jnp.float32


def _rope_lanes(x, cos_l, sin_l, first_half):
    rot = jnp.where(first_half, pltpu.roll(x, 96, 1), pltpu.roll(x, 32, 1))
    return x * cos_l + rot * sin_l


def _proj_kernel(x_ref, g_ref, wnat_ref, wtr_ref, cosl_ref, sinl_ref, cost_ref, sint_ref,
                 dqt_ref, dk_ref, dvt_ref, sqt_ref, sk_ref, svt_ref, *, tk):
    x = x_ref[0]
    tm = x.shape[0]
    n1 = x * lax.rsqrt(jnp.mean(x * x, axis=-1, keepdims=True) + EPS) * g_ref[...]
    n1b = n1.astype(BF16)
    nat = jnp.dot(n1b, wnat_ref[...], preferred_element_type=F32)
    tr = lax.dot_general(wtr_ref[...], n1b, (((1,), (1,)), ((), ())),
                         preferred_element_type=F32)

    cos_l, sin_l = cosl_ref[...], sinl_ref[...]
    first_half = (lax.broadcasted_iota(jnp.int32, (tm, LANES), 1) & (HEAD_DIM - 1)) < HEAD_DIM // 2
    for h in range(DIFF_HEADS):
        slab = nat[:, h * LANES:(h + 1) * LANES]
        dk_ref[0, h] = _rope_lanes(slab, cos_l, sin_l, first_half).astype(BF16)
    sk = _rope_lanes(nat[:, DIFF_QK_COLS:DIFF_QK_COLS + LANES], cos_l, sin_l, first_half).astype(BF16)
    for c in range(tm // WINDOW):
        sk_ref[0, c] = sk[c * WINDOW:(c + 1) * WINDOW]

    cos_t, sin_t = cost_ref[...], sint_ref[...]
    half = HEAD_DIM // 2

    def rope_rows(r0):
        x1 = tr[r0:r0 + half]
        x2 = tr[r0 + half:r0 + HEAD_DIM]
        return (x1 * cos_t - x2 * sin_t).astype(BF16), (x1 * sin_t + x2 * cos_t).astype(BF16)

    for h in range(DIFF_HEADS):
        for c in range(2):
            lo, hi = rope_rows(h * 2 * HEAD_DIM + c * HEAD_DIM)
            dqt_ref[0, h, c * HEAD_DIM:c * HEAD_DIM + half] = lo
            dqt_ref[0, h, c * HEAD_DIM + half:(c + 1) * HEAD_DIM] = hi
    ones_rows = (lax.broadcasted_iota(jnp.int32, (BF16_SUBLANES, tk), 0) == 0).astype(BF16)
    for h in range(DIFF_HEADS):
        r0 = DIFF_QK_COLS + h * DIFF_V_DIM
        for c in range(tm // tk):
            dvt_ref[0, h, c, :DIFF_V_DIM] = tr[r0:r0 + DIFF_V_DIM, c * tk:(c + 1) * tk].astype(BF16)
            dvt_ref[0, h, c, DIFF_V_DIM:] = ones_rows

    r0 = DIFF_QK_COLS + DIFF_V_COLS
    for h in range(SWA_Q_HEADS):
        lo, hi = rope_rows(r0 + h * HEAD_DIM)
        sqt_ref[0, h * HEAD_DIM:h * HEAD_DIM + half] = lo
        sqt_ref[0, h * HEAD_DIM + half:(h + 1) * HEAD_DIM] = hi
    r0 += SWA_Q_COLS
    for c in range(tm // WINDOW):
        svt_ref[0, c, :SWA_KV_COLS] = tr[r0:r0 + SWA_KV_COLS, c * WINDOW:(c + 1) * WINDOW].astype(BF16)
        svt_ref[0, c, SWA_KV_COLS:] = ones_rows[:, :WINDOW]


def _proj_call(x, g1, w_nat, w_tr, cos_l, sin_l, cos_t, sin_t, *, tm, tk):
    B, S, D = x.shape
    nkv = S // tk
    grid = (B, S // tm)
    const = lambda b, i: (0, 0)
    out_shape = (
        jax.ShapeDtypeStruct((B, DIFF_HEADS, 2 * HEAD_DIM, S), BF16),
        jax.ShapeDtypeStruct((B, DIFF_HEADS, S, 2 * HEAD_DIM), BF16),
        jax.ShapeDtypeStruct((B, DIFF_HEADS, nkv, VT_ROWS, tk), BF16),
        jax.ShapeDtypeStruct((B, SWA_Q_COLS, S), BF16),
        jax.ShapeDtypeStruct((B, S // WINDOW, WINDOW, SWA_KV_COLS), BF16),
        jax.ShapeDtypeStruct((B, S // WINDOW, SWA_VT_ROWS, WINDOW), BF16),
    )
    return pl.pallas_call(
        functools.partial(_proj_kernel, tk=tk),
        grid=grid,
        in_specs=[
            pl.BlockSpec((1, tm, D), lambda b, i: (b, i, 0)),
            pl.BlockSpec((1, D), const),
            pl.BlockSpec(w_nat.shape, const),
            pl.BlockSpec(w_tr.shape, const),
            pl.BlockSpec((tm, LANES), lambda b, i: (i, 0)),
            pl.BlockSpec((tm, LANES), lambda b, i: (i, 0)),
            pl.BlockSpec((HEAD_DIM // 2, tm), lambda b, i: (0, i)),
            pl.BlockSpec((HEAD_DIM // 2, tm), lambda b, i: (0, i)),
        ],
        out_specs=(
            pl.BlockSpec((1, DIFF_HEADS, 2 * HEAD_DIM, tm), lambda b, i: (b, 0, 0, i)),
            pl.BlockSpec((1, DIFF_HEADS, tm, 2 * HEAD_DIM), lambda b, i: (b, 0, i, 0)),
            pl.BlockSpec((1, DIFF_HEADS, tm // tk, VT_ROWS, tk), lambda b, i: (b, 0, i, 0, 0)),
            pl.BlockSpec((1, SWA_Q_COLS, tm), lambda b, i: (b, 0, i)),
            pl.BlockSpec((1, tm // WINDOW, WINDOW, SWA_KV_COLS), lambda b, i: (b, i, 0, 0)),
            pl.BlockSpec((1, tm // WINDOW, SWA_VT_ROWS, WINDOW), lambda b, i: (b, i, 0, 0)),
        ),
        out_shape=out_shape,
        compiler_params=pltpu.CompilerParams(
            dimension_semantics=("parallel", "parallel"), vmem_limit_bytes=VMEM_LIMIT_BYTES),
        name="proj_rope",
    )(x, g1, w_nat, w_tr, cos_l, sin_l, cos_t, sin_t)


def _diff_kernel(lam_ref, qt_ref, k_ref, vt_ref, g_ref, o_ref, *scratch, tq, tk, lambda_init):
    step = pl.program_id(2)
    s_bufs = scratch[:DIFF_S_BUFS]
    top_bufs = scratch[DIFF_S_BUFS:2 * DIFF_S_BUFS]
    state = scratch[2 * DIFF_S_BUFS:2 * DIFF_S_BUFS + 2 * DIFF_Q_TILES]
    bias_ref = scratch[-1]

    @pl.when(step == 0)
    def _():
        r = lax.broadcasted_iota(jnp.int32, (tk, 2 * tq), 0)
        c = lax.broadcasted_iota(jnp.int32, (tk, 2 * tq), 1) & (tq - 1)
        bias_ref[...] = jnp.where(r <= c, 0.0, NEG).astype(F32)

    lam_p = lam_ref[...]
    lam = (jnp.exp(jnp.sum(lam_p[0:1] * lam_p[1:2], axis=-1, keepdims=True))
           - jnp.exp(jnp.sum(lam_p[2:3] * lam_p[3:4], axis=-1, keepdims=True)) + lambda_init)

    for sub in range(DIFF_Q_TILES):
        _diff_query_tile(step * DIFF_Q_TILES + sub, qt_ref[0, 0, :, sub * tq:(sub + 1) * tq], k_ref, vt_ref,
                         g_ref, o_ref.at[0, pl.ds(sub * tq, tq), :], s_bufs, top_bufs,
                         state[2 * sub], state[2 * sub + 1], bias_ref, lam,
                         tq=tq, tk=tk, lambda_init=lambda_init)


def _diff_query_tile(i, qt, k_ref, vt_ref, g_ref, o_ref, s_bufs, top_bufs, m_ref, acc_ref, bias_ref, lam,
                     *, tq, tk, lambda_init):
    z = jnp.zeros((HEAD_DIM, tq), BF16)
    qw = jnp.concatenate([jnp.concatenate([qt[:HEAD_DIM], z], axis=1),
                          jnp.concatenate([z, qt[HEAD_DIM:]], axis=1)], axis=0)

    def scores(j, par):
        kt = k_ref[0, 0, pl.ds(pl.multiple_of(j * tk, tk), tk), :]
        s = jnp.dot(kt, qw, preferred_element_type=F32)
        s_bufs[par][...] = s
        top_bufs[par][...] = jnp.max(s, axis=0, keepdims=True)

    def absorb(j, par, masked):
        s = s_bufs[par][...]
        if masked:
            s = s + bias_ref[...]
            top = jnp.max(s, axis=0, keepdims=True)
        else:
            top = top_bufs[par][...]
        m = m_ref[...]
        m_new = jnp.maximum(m, top)
        alpha = jnp.exp2(m - m_new)
        p = jnp.exp2(s - m_new).astype(BF16)
        m_ref[...] = m_new
        pv = jnp.dot(vt_ref[0, 0, j], p, preferred_element_type=F32)
        acc_ref[...] = alpha * acc_ref[...] + pv

    m_ref[...] = jnp.full(m_ref.shape, NEG, F32)
    acc_ref[...] = jnp.zeros(acc_ref.shape, F32)

    nfull = (i * tq) // tk
    scores(nfull, 0)
    scores(0, 1)
    absorb(nfull, 0, True)

    def group(t, c):
        j = DIFF_UNROLL * t
        for idx in range(DIFF_UNROLL):
            scores(j + idx + 1, (idx + 2) % DIFF_S_BUFS)
            absorb(j + idx, (idx + 1) % DIFF_S_BUFS, False)
        return c

    lax.fori_loop(0, nfull // DIFF_UNROLL, group, 0)

    for rem in range(1, DIFF_UNROLL):
        @pl.when(nfull % DIFF_UNROLL == rem)
        def _():
            first = nfull - rem
            for idx in range(rem):
                if idx + 1 < rem:
                    scores(first + idx + 1, (idx + 2) % DIFF_S_BUFS)
                absorb(first + idx, (idx + 1) % DIFF_S_BUFS, False)

    inv_l = 1.0 / acc_ref[DIFF_V_DIM:DIFF_V_DIM + 1, :]
    o = (acc_ref[:DIFF_V_DIM, :tq] * inv_l[:, :tq]
         - lam * (acc_ref[:DIFF_V_DIM, tq:] * inv_l[:, tq:]))
    o = o * lax.rsqrt(jnp.mean(o * o, axis=0, keepdims=True) + EPS)
    o_ref[...] = (o.T * g_ref[...] * (1.0 - lambda_init)).astype(BF16)


def _diff_call(lam_p, dqt, dk, dvt, subln_g, *, tq, tk, lambda_init):
    B, H, _, S = dqt.shape
    assert tk == tq and S % tk == 0, "the diagonal tile's causal pattern is built for square tiles"
    nkv = S // tk
    tq_step = DIFF_Q_TILES * tq
    assert S % tq_step == 0
    grid = (B, H, S // tq_step)
    return pl.pallas_call(
        functools.partial(_diff_kernel, tq=tq, tk=tk, lambda_init=lambda_init),
        grid=grid,
        in_specs=[
            pl.BlockSpec(lam_p.shape, lambda b, h, i: (0, 0)),
            pl.BlockSpec((1, 1, 2 * HEAD_DIM, tq_step), lambda b, h, i: (b, h, 0, i)),
            pl.BlockSpec((1, 1, S, 2 * HEAD_DIM), lambda b, h, i: (b, h, 0, 0)),
            pl.BlockSpec((1, 1, nkv, VT_ROWS, tk), lambda b, h, i: (b, h, 0, 0, 0)),
            pl.BlockSpec((1, DIFF_V_DIM), lambda b, h, i: (0, 0)),
        ],
        out_specs=pl.BlockSpec((1, tq_step, DIFF_V_DIM), lambda b, h, i: (b, i, h)),
        out_shape=jax.ShapeDtypeStruct((B, S, DIFF_V_COLS), BF16),
        scratch_shapes=[pltpu.VMEM((tk, 2 * tq), F32)] * DIFF_S_BUFS + [
            pltpu.VMEM((1, 2 * tq), F32)] * DIFF_S_BUFS + [
            pltpu.VMEM((1, 2 * tq), F32),
            pltpu.VMEM((VT_ROWS, 2 * tq), F32)] * DIFF_Q_TILES + [
            pltpu.VMEM((tk, 2 * tq), F32),
        ],
        compiler_params=pltpu.CompilerParams(
            dimension_semantics=("parallel", "parallel", "arbitrary"),
            vmem_limit_bytes=VMEM_LIMIT_BYTES),
        name="diff_attn",
    )(lam_p, dqt, dk, dvt, subln_g)


def _swa_kernel(sink_ref, qt_ref, k_ref, vt_ref, o_ref, *, tq):
    i = pl.program_id(1)
    n_cols = SWA_Q_HEADS * WINDOW
    half_cols = n_cols // SWA_KV_HEADS
    sink = sink_ref[...]
    row = lax.broadcasted_iota(jnp.int32, (2 * WINDOW, WINDOW), 0)
    qrel = lax.broadcasted_iota(jnp.int32, (2 * WINDOW, WINDOW), 1)
    band = (row - WINDOW <= qrel) & (row > qrel)
    in_current = row >= WINDOW
    z = jnp.zeros((HEAD_DIM, half_cols), BF16)
    for sub in range(tq // WINDOW):
        n = i * (tq // WINDOW) + sub
        prev = jnp.maximum(n - 1, 0)
        kwin = jnp.concatenate([k_ref[0, prev], k_ref[0, n]], axis=0)
        vtwin = jnp.concatenate([vt_ref[0, prev], vt_ref[0, n]], axis=1)
        qt = qt_ref[0, :, sub * WINDOW:(sub + 1) * WINDOW]
        heads = [qt[h * HEAD_DIM:(h + 1) * HEAD_DIM] for h in range(SWA_Q_HEADS)]
        qw = jnp.concatenate(
            [jnp.concatenate(heads[:SWA_GROUP] + [z], axis=1),
             jnp.concatenate([z] + heads[SWA_GROUP:], axis=1)], axis=0)
        s = jnp.dot(kwin, qw, preferred_element_type=F32)
        valid = band & (in_current | (n >= 1))
        s = jnp.concatenate(
            [jnp.where(valid, s[:, h * WINDOW:(h + 1) * WINDOW], NEG) for h in range(SWA_Q_HEADS)], axis=1)
        m = jnp.maximum(jnp.max(s, axis=0, keepdims=True), sink)
        p = jnp.exp2(s - m).astype(BF16)
        acc = jnp.dot(vtwin, p, preferred_element_type=F32)
        den = acc[SWA_KV_COLS:SWA_KV_COLS + 1] + jnp.exp2(sink - m)
        on = acc[:SWA_KV_COLS] / den
        u = jnp.concatenate([on[:HEAD_DIM, :half_cols], on[HEAD_DIM:, half_cols:]], axis=1)
        for hp in range(SWA_Q_HEADS // 2):
            two = jnp.concatenate([u[:, (2 * hp) * WINDOW:(2 * hp + 1) * WINDOW],
                                   u[:, (2 * hp + 1) * WINDOW:(2 * hp + 2) * WINDOW]], axis=0)
            o_ref[0, sub * WINDOW:(sub + 1) * WINDOW, hp * LANES:(hp + 1) * LANES] = two.T.astype(BF16)


def _swa_call(sink_row, sqt, sk, svt, *, tq):
    B, _, S = sqt.shape
    nb = S // WINDOW
    return pl.pallas_call(
        functools.partial(_swa_kernel, tq=tq),
        grid=(B, S // tq),
        in_specs=[
            pl.BlockSpec(sink_row.shape, lambda b, i: (0, 0)),
            pl.BlockSpec((1, SWA_Q_COLS, tq), lambda b, i: (b, 0, i)),
            pl.BlockSpec((1, nb, WINDOW, SWA_KV_COLS), lambda b, i: (b, 0, 0, 0)),
            pl.BlockSpec((1, nb, SWA_VT_ROWS, WINDOW), lambda b, i: (b, 0, 0, 0)),
        ],
        out_specs=pl.BlockSpec((1, tq, SWA_Q_COLS), lambda b, i: (b, i, 0)),
        out_shape=jax.ShapeDtypeStruct((B, S, SWA_Q_COLS), BF16),
        compiler_params=pltpu.CompilerParams(
            dimension_semantics=("parallel", "arbitrary"), vmem_limit_bytes=VMEM_LIMIT_BYTES),
        name="swa_attn",
    )(sink_row, sqt, sk, svt)


def _pack_bf16_pairs(x):
    n = x.shape[1] // 2
    lo = lax.bitcast_convert_type(x[:, :n].astype(BF16).astype(F32), jnp.uint32)
    hi = lax.bitcast_convert_type(x[:, n:].astype(BF16).astype(F32), jnp.uint32)
    return (lo >> 16) | (hi & jnp.uint32(0xFFFF0000))


def _unpack_bf16_pairs(w):
    lo = lax.bitcast_convert_type(w << 16, F32)
    hi = lax.bitcast_convert_type(w & jnp.uint32(0xFFFF0000), F32)
    return jnp.concatenate([lo, hi], axis=1).astype(BF16)


def _mix_kernel(x_ref, od_ref, os_ref, wo_ref, g2_ref, wr_ref, br_ref, x1_ref, n2_ref, rt_ref, cnt_ref):
    tm = x_ref.shape[1]
    lane = lax.broadcasted_iota(jnp.int32, (tm, ROUTER_COLS), 1)
    lane_f = lane.astype(F32)
    big = float(ROUTER_COLS)
    mixed = jnp.concatenate([od_ref[0], os_ref[0]], axis=1)
    h = x_ref[0] + jnp.dot(mixed, wo_ref[...], preferred_element_type=F32)
    x1_ref[0] = h
    n2 = h * lax.rsqrt(jnp.mean(h * h, axis=-1, keepdims=True) + EPS) * g2_ref[...]
    n2_ref[0] = _pack_bf16_pairs(n2)
    n2_hi = n2.astype(BF16)
    n2_lo = (n2 - n2_hi.astype(F32)).astype(BF16)
    parts = jnp.dot(jnp.concatenate([n2_hi, n2_lo], axis=0), wr_ref[...],
                    preferred_element_type=F32)
    logits = ((parts[:tm, :ROUTER_COLS] + parts[tm:, ROUTER_COLS:])
              + (parts[:tm, ROUTER_COLS:] + parts[tm:, :ROUTER_COLS])) + br_ref[...]
    gl = jnp.where(lane < N_GROUPS, logits, -jnp.inf)
    gm = jnp.max(gl, axis=-1, keepdims=True)
    p_top = 1.0 / jnp.sum(jnp.exp(gl - gm), axis=-1, keepdims=True)
    g_idx = jnp.min(jnp.where(gl == gm, lane_f, big), axis=-1, keepdims=True)
    e_lo = N_GROUPS + EXPERTS_PER_GROUP * g_idx
    el = jnp.where((lane_f >= e_lo) & (lane_f < e_lo + EXPERTS_PER_GROUP), logits, -jnp.inf)
    v1 = jnp.max(el, axis=-1, keepdims=True)
    i1 = jnp.min(jnp.where(el == v1, lane_f, big), axis=-1, keepdims=True)
    el2 = jnp.where(lane_f == i1, -jnp.inf, el)
    v2 = jnp.max(el2, axis=-1, keepdims=True)
    i2 = jnp.min(jnp.where(el2 == v2, lane_f, big), axis=-1, keepdims=True)
    e21 = jnp.exp(v2 - v1)
    gate1 = p_top / (1.0 + e21)
    gate2 = p_top * e21 / (1.0 + e21)
    rt_ref[0] = jnp.where(lane == 0, i1 - N_GROUPS,
                jnp.where(lane == 1, i2 - N_GROUPS,
                jnp.where(lane == 2, gate1, jnp.where(lane == 3, gate2, 0.0))))
    chosen = ((lane_f == i1 - N_GROUPS) | (lane_f == i2 - N_GROUPS)).astype(F32)
    cnt_ref[0, 0] = jnp.broadcast_to(jnp.sum(chosen, axis=0, keepdims=True), cnt_ref.shape[2:])


def _mix_call(x, o_diff, o_swa, w_out, g2, w_router, b_router, *, tm):
    B, S, D = x.shape
    const = lambda b, i: (0, 0)
    row = lambda b, i: (b, i, 0)
    nt = S // tm
    return pl.pallas_call(
        _mix_kernel,
        grid=(B, nt),
        in_specs=[
            pl.BlockSpec((1, tm, D), row),
            pl.BlockSpec((1, tm, DIFF_V_COLS), row),
            pl.BlockSpec((1, tm, SWA_Q_COLS), row),
            pl.BlockSpec(w_out.shape, const),
            pl.BlockSpec((1, D), const),
            pl.BlockSpec(w_router.shape, const),
            pl.BlockSpec((1, ROUTER_COLS), const),
        ],
        out_specs=(pl.BlockSpec((1, tm, D), row), pl.BlockSpec((1, tm, D // 2), row),
                   pl.BlockSpec((1, tm, ROUTER_COLS), row),
                   pl.BlockSpec((1, 1, SUBLANES, ROUTER_COLS), lambda b, i: (b, i, 0, 0))),
        out_shape=(jax.ShapeDtypeStruct((B, S, D), F32), jax.ShapeDtypeStruct((B, S, D // 2), jnp.uint32),
                   jax.ShapeDtypeStruct((B, S, ROUTER_COLS), F32),
                   jax.ShapeDtypeStruct((B, nt, SUBLANES, ROUTER_COLS), F32)),
        compiler_params=pltpu.CompilerParams(
            dimension_semantics=("parallel", "parallel"), vmem_limit_bytes=VMEM_LIMIT_BYTES),
        name="outproj_router",
    )(x, o_diff, o_swa, w_out, g2, w_router, b_router)


def _slot_kernel(rt_ref, base_ref, dest_ref, *, tm):
    rt_t = rt_ref[...].T
    e1 = rt_t[0:1].astype(jnp.int32)
    e2 = rt_t[1:2].astype(jnp.int32)
    eid = lax.broadcasted_iota(jnp.int32, (N_EXPERTS, tm), 0)
    oh1 = eid == e1
    oh2 = eid == e2
    earlier = (lax.broadcasted_iota(jnp.int32, (tm, tm), 0)
               < lax.broadcasted_iota(jnp.int32, (tm, tm), 1)).astype(BF16)
    before = jnp.dot((oh1 | oh2).astype(BF16), earlier, preferred_element_type=F32)
    slot = before + base_ref[0][:, 0:1]
    d1 = jnp.sum(jnp.where(oh1, slot, 0.0), axis=0, keepdims=True).astype(jnp.int32)
    d2 = jnp.sum(jnp.where(oh2, slot, 0.0), axis=0, keepdims=True).astype(jnp.int32)
    dest_ref[0] = jnp.concatenate([d1, d2, jnp.zeros((SUBLANES - TOP_K, tm), jnp.int32)], axis=0)


def _slot_call(rt, tile_base, *, tm):
    nt = rt.shape[0] // tm
    return pl.pallas_call(
        functools.partial(_slot_kernel, tm=tm),
        grid=(nt,),
        in_specs=[
            pl.BlockSpec((tm, ROUTER_COLS), lambda t: (t, 0)),
            pl.BlockSpec((1, N_EXPERTS, LANES), lambda t: (t, 0, 0)),
        ],
        out_specs=pl.BlockSpec((1, SUBLANES, tm), lambda t: (t, 0, 0)),
        out_shape=jax.ShapeDtypeStruct((nt, SUBLANES, tm), jnp.int32),
        compiler_params=pltpu.CompilerParams(dimension_semantics=("parallel",)),
        name="moe_slots",
    )(rt, tile_base)


SC_ROW_CHUNK = 64


def _sc_workers():
    info = plsc.get_sparse_core_info()
    return info.num_cores, info.num_cores * info.num_subcores


def _sc_scatter_rows(rows, idx, n_out):
    n, width = rows.shape
    n_cores, n_workers = _sc_workers()
    n_chunks = n // SC_ROW_CHUNK
    per_worker = n_chunks // n_workers
    assert n_chunks % n_workers == 0
    mesh = plsc.VectorSubcoreMesh(core_axis_name="c", subcore_axis_name="s")

    @functools.partial(
        pl.kernel, mesh=mesh,
        out_type=jax.ShapeDtypeStruct((n_out, width), rows.dtype),
        scratch_types=[
            pltpu.VMEM((SC_ROW_CHUNK,), jnp.int32),
            pltpu.VMEM((SC_ROW_CHUNK, width), rows.dtype),
        ],
    )
    def scatter(rows_hbm, idx_hbm, out_hbm, idx_v, rows_v):
        worker = lax.axis_index("s") * n_cores + lax.axis_index("c")

        @pl.loop(0, per_worker)
        def _(i):
            c = worker * per_worker + i
            pltpu.sync_copy(rows_hbm.at[pl.ds(pl.multiple_of(c * SC_ROW_CHUNK, SC_ROW_CHUNK), SC_ROW_CHUNK)], rows_v)
            for k in range(TOP_K):
                pltpu.sync_copy(idx_hbm.at[k, c], idx_v)
                pltpu.sync_copy(rows_v, out_hbm.at[idx_v])

    return scatter(rows, idx)


def _expert_kernel(be_ref, nvalid_ref, xs_ref, wg_ref, wu_ref, wd_ref, y_ref, wg_b, wu_b, wd_b):
    b = pl.program_id(0)
    n_valid = nvalid_ref[b]

    @pl.when(n_valid > 0)
    def _():
        @pl.when((b == 0) | (be_ref[b] != be_ref[jnp.maximum(b - 1, 0)]))
        def _():
            wg_b[...] = wg_ref[0].astype(BF16)
            wu_b[...] = wu_ref[0].astype(BF16)
            wd_b[...] = wd_ref[0].astype(BF16)

        for c in range(EXPERT_BLOCK // EXPERT_CHUNK):
            rows = pl.ds(c * EXPERT_CHUNK, EXPERT_CHUNK)
            row_id = c * EXPERT_CHUNK + lax.broadcasted_iota(jnp.int32, (EXPERT_CHUNK, xs_ref.shape[1]), 0)
            packed = jnp.where(row_id < n_valid, xs_ref[rows, :], jnp.uint32(0))
            xb = _unpack_bf16_pairs(packed)
            gate = jnp.dot(xb, wg_b[...], preferred_element_type=F32)
            up = jnp.dot(xb, wu_b[...], preferred_element_type=F32)
            hid = (gate * jax.nn.sigmoid(gate) * up).astype(BF16)
            y_ref[rows, :] = _pack_bf16_pairs(jnp.dot(hid, wd_b[...], preferred_element_type=F32))

    @pl.when(n_valid == 0)
    def _():
        y_ref[...] = jnp.zeros_like(y_ref)


def _expert_call(block_expert, n_valid, xs, w_gate, w_up, w_down):
    P = xs.shape[0]
    NB = P // EXPERT_BLOCK
    E, D, F = w_gate.shape
    grid_spec = pltpu.PrefetchScalarGridSpec(
        num_scalar_prefetch=2,
        grid=(NB,),
        in_specs=[
            pl.BlockSpec((EXPERT_BLOCK,) + xs.shape[1:], lambda b, be, nu: (b, 0)),
            pl.BlockSpec((1, D, F), lambda b, be, nu: (be[b], 0, 0)),
            pl.BlockSpec((1, D, F), lambda b, be, nu: (be[b], 0, 0)),
            pl.BlockSpec((1, F, D), lambda b, be, nu: (be[b], 0, 0)),
        ],
        out_specs=pl.BlockSpec((EXPERT_BLOCK, D // 2), lambda b, be, nu: (b, 0)),
        scratch_shapes=[
            pltpu.VMEM((D, F), BF16),
            pltpu.VMEM((D, F), BF16),
            pltpu.VMEM((F, D), BF16),
        ],
    )
    return pl.pallas_call(
        _expert_kernel,
        grid_spec=grid_spec,
        out_shape=jax.ShapeDtypeStruct((P, D // 2), jnp.uint32),
        compiler_params=pltpu.CompilerParams(
            dimension_semantics=("arbitrary",), vmem_limit_bytes=VMEM_LIMIT_BYTES),
        name="moe_experts",
    )(block_expert, n_valid, xs, w_gate, w_up, w_down)


def _sc_gather_rows(table, idx):
    n_rows, width = idx.shape[0], table.shape[1]
    n_cores, n_workers = _sc_workers()
    per_worker = n_rows // n_workers
    assert n_rows % (n_workers * 2 * SC_ROW_CHUNK) == 0
    mesh = plsc.VectorSubcoreMesh(core_axis_name="c", subcore_axis_name="s")

    @functools.partial(
        pl.kernel, mesh=mesh,
        out_type=jax.ShapeDtypeStruct((n_rows, width), table.dtype),
        scratch_types=[pltpu.VMEM((SC_ROW_CHUNK,), jnp.int32)] * 2
        + [pltpu.VMEM((SC_ROW_CHUNK, width), table.dtype)] * 2
        + [pltpu.SemaphoreType.DMA] * 4,
    )
    def gather(table_hbm, idx_hbm, out_hbm, idx_a, idx_b, rows_a, rows_b, sem_ga, sem_gb, sem_wa, sem_wb):
        worker = lax.axis_index("s") * n_cores + lax.axis_index("c")
        base = worker * per_worker

        @pl.loop(0, per_worker // SC_ROW_CHUNK, step=2)
        def _(c):
            off_a = pl.multiple_of(base + c * SC_ROW_CHUNK, SC_ROW_CHUNK)
            off_b = pl.multiple_of(off_a + SC_ROW_CHUNK, SC_ROW_CHUNK)
            pltpu.sync_copy(idx_hbm.at[pl.ds(off_a, SC_ROW_CHUNK)], idx_a)
            pltpu.sync_copy(idx_hbm.at[pl.ds(off_b, SC_ROW_CHUNK)], idx_b)
            gather_a = pltpu.async_copy(table_hbm.at[idx_a], rows_a, sem_ga)
            gather_b = pltpu.async_copy(table_hbm.at[idx_b], rows_b, sem_gb)
            gather_a.wait()
            write_a = pltpu.async_copy(rows_a, out_hbm.at[pl.ds(off_a, SC_ROW_CHUNK)], sem_wa)
            gather_b.wait()
            write_b = pltpu.async_copy(rows_b, out_hbm.at[pl.ds(off_b, SC_ROW_CHUNK)], sem_wb)
            write_a.wait()
            write_b.wait()

    return gather(table, idx)


def _combine_kernel(x1_ref, rt_ref, y_ref, fg_ref, o_ref, *, final_norm):
    rt = rt_ref[...]
    y1 = _unpack_bf16_pairs(y_ref[0, 0]).astype(F32)
    y2 = _unpack_bf16_pairs(y_ref[0, 1]).astype(F32)
    h = x1_ref[...] + rt[:, 2:3] * y1 + rt[:, 3:4] * y2
    if final_norm:
        h = h * lax.rsqrt(jnp.mean(h * h, axis=-1, keepdims=True) + EPS) * fg_ref[...]
    o_ref[...] = h


def _combine_call(x1, rt, ysg, final_g, *, tm, final_norm):
    T, D = x1.shape
    return pl.pallas_call(
        functools.partial(_combine_kernel, final_norm=final_norm),
        grid=(T // tm,),
        in_specs=[
            pl.BlockSpec((tm, D), lambda t: (t, 0)),
            pl.BlockSpec((tm, ROUTER_COLS), lambda t: (t, 0)),
            pl.BlockSpec((1, TOP_K, tm, D // 2), lambda t: (t, 0, 0, 0)),
            pl.BlockSpec((1, D), lambda t: (0, 0)),
        ],
        out_specs=pl.BlockSpec((tm, D), lambda t: (t, 0)),
        out_shape=jax.ShapeDtypeStruct((T, D), F32),
        compiler_params=pltpu.CompilerParams(
            dimension_semantics=("parallel",), vmem_limit_bytes=VMEM_LIMIT_BYTES),
        name="moe_combine",
    )(x1, rt, ysg, final_g)


def _slot_layout(tile_counts, n_assign):
    NB = -(-n_assign // EXPERT_BLOCK) + N_EXPERTS
    n_tiles = tile_counts.shape[0]
    tc = tile_counts.astype(F32)
    hp = lax.Precision.HIGHEST
    counts = jnp.sum(tc, axis=0)
    padded = jnp.ceil(counts / EXPERT_BLOCK) * EXPERT_BLOCK
    upper = (jnp.arange(N_EXPERTS)[:, None] < jnp.arange(N_EXPERTS)[None, :]).astype(F32)
    pad_start = jnp.dot(padded, upper, precision=hp)
    pad_end = pad_start + padded
    lower = (jnp.arange(n_tiles)[:, None] > jnp.arange(n_tiles)[None, :]).astype(F32)
    tile_base = pad_start[None, :] + jnp.dot(lower, tc, precision=hp)
    block_start = jnp.arange(NB, dtype=F32) * EXPERT_BLOCK
    block_expert = jnp.minimum(jnp.sum((pad_end[None, :] <= block_start[:, None]).astype(jnp.int32), axis=1),
                               N_EXPERTS - 1)
    mine = block_expert[:, None] == jnp.arange(N_EXPERTS)[None, :]
    run_end = jnp.sum(jnp.where(mine, (pad_start + counts)[None, :], 0.0), axis=1)
    n_valid = jnp.clip(run_end - block_start, 0, EXPERT_BLOCK).astype(jnp.int32)
    return NB, block_expert.astype(jnp.int32), n_valid, tile_base


def _rope_tables(S):
    half = HEAD_DIM // 2
    inv = 1.0 / (ROPE_THETA ** (jnp.arange(0, HEAD_DIM, 2, dtype=F32) / HEAD_DIM))
    pos = jnp.arange(S, dtype=F32)
    ang_l = pos[:, None] * jnp.tile(inv, LANES // half)[None, :]
    sign = jnp.tile(jnp.concatenate([-jnp.ones((half,), F32), jnp.ones((half,), F32)]), LANES // HEAD_DIM)
    ang_t = inv[:, None] * pos[None, :]
    return jnp.cos(ang_l), jnp.sin(ang_l) * sign[None, :], jnp.cos(ang_t), jnp.sin(ang_t)


def _tiles(S):
    tile = min(512, S)
    return tile, tile, tile, tile, min(1024, S)


def kernel(x, norm1_g, w_in, lambda_q1, lambda_k1, lambda_q2, lambda_k2, subln_g, sinks, w_out,
           norm2_g, w_router_group, b_router_group, w_router_expert, b_router_expert,
           w_gate, w_up, w_down, final_g):
    B, S, D = x.shape
    T = B * S
    depth = w_in.shape[0]
    tq, tk, tm_proj, tm_tok, tq_swa = _tiles(S)
    qscale = HEAD_DIM ** -0.5 * math.log2(math.e)
    cos_l, sin_l, cos_t, sin_t = _rope_tables(S)

    c0 = DIFF_QK_COLS
    c1 = 2 * DIFF_QK_COLS
    c2 = c1 + DIFF_V_COLS
    c3 = c2 + SWA_Q_COLS
    c4 = c3 + SWA_KV_COLS
    for l in range(depth):
        lambda_init = 0.8 - 0.6 * math.exp(-0.3 * l)
        w = w_in[l]
        w_nat = jnp.concatenate([w[:, c0:c1], w[:, c3:c4]], axis=1).astype(BF16)
        w_tr = jnp.concatenate([w[:, :c0] * qscale, w[:, c1:c2], w[:, c2:c3] * qscale, w[:, c4:]],
                               axis=1).T.astype(BF16)
        dqt, dk, dvt, sqt, sk, svt = _proj_call(
            x, norm1_g[l][None, :], w_nat, w_tr, cos_l, sin_l, cos_t, sin_t, tm=tm_proj, tk=tk)

        lam_p = jnp.stack([lambda_q1[l], lambda_k1[l], lambda_q2[l], lambda_k2[l]]).astype(F32)
        o_diff = _diff_call(lam_p, dqt, dk, dvt, subln_g[l][None, :].astype(F32),
                            tq=tq, tk=tk, lambda_init=lambda_init)
        sink_row = jnp.repeat(sinks[l].astype(F32) * math.log2(math.e), WINDOW)[None, :]
        o_swa = _swa_call(sink_row, sqt, sk, svt, tq=tq_swa)

        wo_b = w_out[l].astype(BF16)
        w_router = jnp.zeros((D, ROUTER_COLS), F32)
        w_router = w_router.at[:, :N_GROUPS].set(w_router_group[l])
        w_router = w_router.at[:, N_GROUPS:N_GROUPS + N_EXPERTS].set(w_router_expert[l])
        w_router_hi = w_router.astype(BF16)
        w_router_lo = (w_router - w_router_hi.astype(F32)).astype(BF16)
        w_router = jnp.concatenate([w_router_hi, w_router_lo], axis=1)
        b_router = jnp.zeros((1, ROUTER_COLS), F32)
        b_router = b_router.at[0, :N_GROUPS].set(b_router_group[l])
        b_router = b_router.at[0, N_GROUPS:N_GROUPS + N_EXPERTS].set(b_router_expert[l])
        x1, n2p, rt, cnt = _mix_call(x, o_diff, o_swa, wo_b, norm2_g[l][None, :], w_router, b_router, tm=tm_tok)

        rt2 = rt.reshape(T, ROUTER_COLS)
        tile_counts = cnt[:, :, 0, :N_EXPERTS].reshape(T // tm_tok, N_EXPERTS).astype(jnp.int32)
        NB, block_expert, n_valid, tile_base = _slot_layout(tile_counts, T * TOP_K)
        tile_base = jnp.broadcast_to(tile_base.astype(F32)[:, :, None], (T // tm_tok, N_EXPERTS, LANES))
        dest = _slot_call(rt2, tile_base, tm=tm_tok)
        scatter_idx = jnp.swapaxes(dest[:, :TOP_K, :], 0, 1).reshape(TOP_K, T // SC_ROW_CHUNK, SC_ROW_CHUNK)
        xs = _sc_scatter_rows(n2p.reshape(T, D // 2), scatter_idx, NB * EXPERT_BLOCK)
        ys = _expert_call(block_expert, n_valid, xs, w_gate[l], w_up[l], w_down[l])
        ysg = _sc_gather_rows(ys, dest[:, :TOP_K, :].reshape(T * TOP_K))
        x = _combine_call(x1.reshape(T, D), rt2, ysg.reshape(T // tm_tok, TOP_K, tm_tok, D // 2),
                          final_g[None, :], tm=tm_tok, final_norm=(l == depth - 1)).reshape(B, S, D)
    return x
```

```python
import functools
import math

import jax
import jax.numpy as jnp
from jax import lax
from jax.experimental import pallas as pl
from jax.experimental.pallas import tpu as pltpu
from jax.experimental.pallas import tpu_sc as plsc

HEAD_DIM = 64
DIFF_HEADS = 4
DIFF_V_DIM = 2 * HEAD_DIM
SWA_Q_HEADS = 8
SWA_KV_HEADS = 2
SWA_GROUP = SWA_Q_HEADS // SWA_KV_HEADS
WINDOW = 128
ROPE_THETA = 10000.0
N_GROUPS = 4
EXPERTS_PER_GROUP = 8
N_EXPERTS = N_GROUPS * EXPERTS_PER_GROUP
TOP_K = 2
EXPERT_BLOCK = 512
EXPERT_CHUNK = 256
EPS = 1e-6
NEG = -1e30

DIFF_QK_COLS = DIFF_HEADS * 2 * HEAD_DIM
DIFF_V_COLS = DIFF_HEADS * DIFF_V_DIM
SWA_Q_COLS = SWA_Q_HEADS * HEAD_DIM
SWA_KV_COLS = SWA_KV_HEADS * HEAD_DIM
LANES = 128
SUBLANES = 8
BF16_SUBLANES = 16
VMEM_LIMIT_BYTES = 48 * 1024 * 1024
VT_ROWS = DIFF_V_DIM + BF16_SUBLANES
SWA_VT_ROWS = SWA_KV_COLS + BF16_SUBLANES
ROUTER_COLS = LANES
DIFF_UNROLL = 4
DIFF_S_BUFS = 4
DIFF_Q_TILES = 2

BF16 = jnp.bfloat16
F32 = jnp.float32


def _rope_lanes(x, cos_l, sin_l, first_half):
    rot = jnp.where(first_half, pltpu.roll(x, 96, 1), pltpu.roll(x, 32, 1))
    return x * cos_l + rot * sin_l


def _proj_kernel(x_ref, g_ref, wnat_ref, wtr_ref, cosl_ref, sinl_ref, cost_ref, sint_ref,
                 dqt_ref, dk_ref, dvt_ref, sqt_ref, sk_ref, svt_ref, *, tk):
    x = x_ref[0]
    tm = x.shape[0]
    n1 = x * lax.rsqrt(jnp.mean(x * x, axis=-1, keepdims=True) + EPS) * g_ref[...]
    n1b = n1.astype(BF16)
    nat = jnp.dot(n1b, wnat_ref[...], preferred_element_type=F32)
    tr = lax.dot_general(wtr_ref[...], n1b, (((1,), (1,)), ((), ())),
                         preferred_element_type=F32)

    cos_l, sin_l = cosl_ref[...], sinl_ref[...]
    first_half = (lax.broadcasted_iota(jnp.int32, (tm, LANES), 1) & (HEAD_DIM - 1)) < HEAD_DIM // 2
    for h in range(DIFF_HEADS):
        slab = nat[:, h * LANES:(h + 1) * LANES]
        dk_ref[0, h] = _rope_lanes(slab, cos_l, sin_l, first_half).astype(BF16)
    sk = _rope_lanes(nat[:, DIFF_QK_COLS:DIFF_QK_COLS + LANES], cos_l, sin_l, first_half).astype(BF16)
    for c in range(tm // WINDOW):
        sk_ref[0, c] = sk[c * WINDOW:(c + 1) * WINDOW]

    cos_t, sin_t = cost_ref[...], sint_ref[...]
    half = HEAD_DIM // 2

    def rope_rows(r0):
        x1 = tr[r0:r0 + half]
        x2 = tr[r0 + half:r0 + HEAD_DIM]
        return (x1 * cos_t - x2 * sin_t).astype(BF16), (x1 * sin_t + x2 * cos_t).astype(BF16)

    for h in range(DIFF_HEADS):
        for c in range(2):
            lo, hi = rope_rows(h * 2 * HEAD_DIM + c * HEAD_DIM)
            dqt_ref[0, h, c * HEAD_DIM:c * HEAD_DIM + half] = lo
            dqt_ref[0, h, c * HEAD_DIM + half:(c + 1) * HEAD_DIM] = hi
    ones_rows = (lax.broadcasted_iota(jnp.int32, (BF16_SUBLANES, tk), 0) == 0).astype(BF16)
    for h in range(DIFF_HEADS):
        r0 = DIFF_QK_COLS + h * DIFF_V_DIM
        for c in range(tm // tk):
            dvt_ref[0, h, c, :DIFF_V_DIM] = tr[r0:r0 + DIFF_V_DIM, c * tk:(c + 1) * tk].astype(BF16)
            dvt_ref[0, h, c, DIFF_V_DIM:] = ones_rows

    r0 = DIFF_QK_COLS + DIFF_V_COLS
    for h in range(SWA_Q_HEADS):
        lo, hi = rope_rows(r0 + h * HEAD_DIM)
        sqt_ref[0, h * HEAD_DIM:h * HEAD_DIM + half] = lo
        sqt_ref[0, h * HEAD_DIM + half:(h + 1) * HEAD_DIM] = hi
    r0 += SWA_Q_COLS
    for c in range(tm // WINDOW):
        svt_ref[0, c, :SWA_KV_COLS] = tr[r0:r0 + SWA_KV_COLS, c * WINDOW:(c + 1) * WINDOW].astype(BF16)
        svt_ref[0, c, SWA_KV_COLS:] = ones_rows[:, :WINDOW]


def _proj_call(x, g1, w_nat, w_tr, cos_l, sin_l, cos_t, sin_t, *, tm, tk):
    B, S, D = x.shape
    nkv = S // tk
    grid = (B, S // tm)
    const = lambda b, i: (0, 0)
    out_shape = (
        jax.ShapeDtypeStruct((B, DIFF_HEADS, 2 * HEAD_DIM, S), BF16),
        jax.ShapeDtypeStruct((B, DIFF_HEADS, S, 2 * HEAD_DIM), BF16),
        jax.ShapeDtypeStruct((B, DIFF_HEADS, nkv, VT_ROWS, tk), BF16),
        jax.ShapeDtypeStruct((B, SWA_Q_COLS, S), BF16),
        jax.ShapeDtypeStruct((B, S // WINDOW, WINDOW, SWA_KV_COLS), BF16),
        jax.ShapeDtypeStruct((B, S // WINDOW, SWA_VT_ROWS, WINDOW), BF16),
    )
    return pl.pallas_call(
        functools.partial(_proj_kernel, tk=tk),
        grid=grid,
        in_specs=[
            pl.BlockSpec((1, tm, D), lambda b, i: (b, i, 0)),
            pl.BlockSpec((1, D), const),
            pl.BlockSpec(w_nat.shape, const),
            pl.BlockSpec(w_tr.shape, const),
            pl.BlockSpec((tm, LANES), lambda b, i: (i, 0)),
            pl.BlockSpec((tm, LANES), lambda b, i: (i, 0)),
            pl.BlockSpec((HEAD_DIM // 2, tm), lambda b, i: (0, i)),
            pl.BlockSpec((HEAD_DIM // 2, tm), lambda b, i: (0, i)),
        ],
        out_specs=(
            pl.BlockSpec((1, DIFF_HEADS, 2 * HEAD_DIM, tm), lambda b, i: (b, 0, 0, i)),
            pl.BlockSpec((1, DIFF_HEADS, tm, 2 * HEAD_DIM), lambda b, i: (b, 0, i, 0)),
            pl.BlockSpec((1, DIFF_HEADS, tm // tk, VT_ROWS, tk), lambda b, i: (b, 0, i, 0, 0)),
            pl.BlockSpec((1, SWA_Q_COLS, tm), lambda b, i: (b, 0, i)),
            pl.BlockSpec((1, tm // WINDOW, WINDOW, SWA_KV_COLS), lambda b, i: (b, i, 0, 0)),
            pl.BlockSpec((1, tm // WINDOW, SWA_VT_ROWS, WINDOW), lambda b, i: (b, i, 0, 0)),
        ),
        out_shape=out_shape,
        compiler_params=pltpu.CompilerParams(
            dimension_semantics=("parallel", "parallel"), vmem_limit_bytes=VMEM_LIMIT_BYTES),
        name="proj_rope",
    )(x, g1, w_nat, w_tr, cos_l, sin_l, cos_t, sin_t)


def _diff_kernel(lam_ref, qt_ref, k_ref, vt_ref, g_ref, o_ref, *scratch, tq, tk, lambda_init):
    step = pl.program_id(2)
    s_bufs = scratch[:DIFF_S_BUFS]
    top_bufs = scratch[DIFF_S_BUFS:2 * DIFF_S_BUFS]
    state = scratch[2 * DIFF_S_BUFS:2 * DIFF_S_BUFS + 2 * DIFF_Q_TILES]
    bias_ref = scratch[-1]

    @pl.when(step == 0)
    def _():
        r = lax.broadcasted_iota(jnp.int32, (tk, 2 * tq), 0)
        c = lax.broadcasted_iota(jnp.int32, (tk, 2 * tq), 1) & (tq - 1)
        bias_ref[...] = jnp.where(r <= c, 0.0, NEG).astype(F32)

    lam_p = lam_ref[...]
    lam = (jnp.exp(jnp.sum(lam_p[0:1] * lam_p[1:2], axis=-1, keepdims=True))
           - jnp.exp(jnp.sum(lam_p[2:3] * lam_p[3:4], axis=-1, keepdims=True)) + lambda_init)

    for sub in range(DIFF_Q_TILES):
        _diff_query_tile(step * DIFF_Q_TILES + sub, qt_ref[0, 0, :, sub * tq:(sub + 1) * tq], k_ref, vt_ref,
                         g_ref, o_ref.at[0, pl.ds(sub * tq, tq), :], s_bufs, top_bufs,
                         state[2 * sub], state[2 * sub + 1], bias_ref, lam,
                         tq=tq, tk=tk, lambda_init=lambda_init)


def _diff_query_tile(i, qt, k_ref, vt_ref, g_ref, o_ref, s_bufs, top_bufs, m_ref, acc_ref, bias_ref, lam,
                     *, tq, tk, lambda_init):
    z = jnp.zeros((HEAD_DIM, tq), BF16)
    qw = jnp.concatenate([jnp.concatenate([qt[:HEAD_DIM], z], axis=1),
                          jnp.concatenate([z, qt[HEAD_DIM:]], axis=1)], axis=0)

    def scores(j, par):
        kt = k_ref[0, 0, pl.ds(pl.multiple_of(j * tk, tk), tk), :]
        s = jnp.dot(kt, qw, preferred_element_type=F32)
        s_bufs[par][...] = s
        top_bufs[par][...] = jnp.max(s, axis=0, keepdims=True)

    def absorb(j, par, masked):
        s = s_bufs[par][...]
        if masked:
            s = s + bias_ref[...]
            top = jnp.max(s, axis=0, keepdims=True)
        else:
            top = top_bufs[par][...]
        m = m_ref[...]
        m_new = jnp.maximum(m, top)
        alpha = jnp.exp2(m - m_new)
        p = jnp.exp2(s - m_new).astype(BF16)
        m_ref[...] = m_new
        pv = jnp.dot(vt_ref[0, 0, j], p, preferred_element_type=F32)
        acc_ref[...] = alpha * acc_ref[...] + pv

    m_ref[...] = jnp.full(m_ref.shape, NEG, F32)
    acc_ref[...] = jnp.zeros(acc_ref.shape, F32)

    nfull = (i * tq) // tk
    scores(nfull, 0)
    scores(0, 1)
    absorb(nfull, 0, True)

    def group(t, c):
        j = DIFF_UNROLL * t
        for idx in range(DIFF_UNROLL):
            scores(j + idx + 1, (idx + 2) % DIFF_S_BUFS)
            absorb(j + idx, (idx + 1) % DIFF_S_BUFS, False)
        return c

    lax.fori_loop(0, nfull // DIFF_UNROLL, group, 0)

    for rem in range(1, DIFF_UNROLL):
        @pl.when(nfull % DIFF_UNROLL == rem)
        def _():
            first = nfull - rem
            for idx in range(rem):
                if idx + 1 < rem:
                    scores(first + idx + 1, (idx + 2) % DIFF_S_BUFS)
                absorb(first + idx, (idx + 1) % DIFF_S_BUFS, False)

    inv_l = 1.0 / acc_ref[DIFF_V_DIM:DIFF_V_DIM + 1, :]
    o = (acc_ref[:DIFF_V_DIM, :tq] * inv_l[:, :tq]
         - lam * (acc_ref[:DIFF_V_DIM, tq:] * inv_l[:, tq:]))
    o = o * lax.rsqrt(jnp.mean(o * o, axis=0, keepdims=True) + EPS)
    o_ref[...] = (o.T * g_ref[...] * (1.0 - lambda_init)).astype(BF16)


def _diff_call(lam_p, dqt, dk, dvt, subln_g, *, tq, tk, lambda_init):
    B, H, _, S = dqt.shape
    assert tk == tq and S % tk == 0, "the diagonal tile's causal pattern is built for square tiles"
    nkv = S // tk
    tq_step = DIFF_Q_TILES * tq
    assert S % tq_step == 0
    grid = (B, H, S // tq_step)
    return pl.pallas_call(
        functools.partial(_diff_kernel, tq=tq, tk=tk, lambda_init=lambda_init),
        grid=grid,
        in_specs=[
            pl.BlockSpec(lam_p.shape, lambda b, h, i: (0, 0)),
            pl.BlockSpec((1, 1, 2 * HEAD_DIM, tq_step), lambda b, h, i: (b, h, 0, i)),
            pl.BlockSpec((1, 1, S, 2 * HEAD_DIM), lambda b, h, i: (b, h, 0, 0)),
            pl.BlockSpec((1, 1, nkv, VT_ROWS, tk), lambda b, h, i: (b, h, 0, 0, 0)),
            pl.BlockSpec((1, DIFF_V_DIM), lambda b, h, i: (0, 0)),
        ],
        out_specs=pl.BlockSpec((1, tq_step, DIFF_V_DIM), lambda b, h, i: (b, i, h)),
        out_shape=jax.ShapeDtypeStruct((B, S, DIFF_V_COLS), BF16),
        scratch_shapes=[pltpu.VMEM((tk, 2 * tq), F32)] * DIFF_S_BUFS + [
            pltpu.VMEM((1, 2 * tq), F32)] * DIFF_S_BUFS + [
            pltpu.VMEM((1, 2 * tq), F32),
            pltpu.VMEM((VT_ROWS, 2 * tq), F32)] * DIFF_Q_TILES + [
            pltpu.VMEM((tk, 2 * tq), F32),
        ],
        compiler_params=pltpu.CompilerParams(
            dimension_semantics=("parallel", "parallel", "arbitrary"),
            vmem_limit_bytes=VMEM_LIMIT_BYTES),
        name="diff_attn",
    )(lam_p, dqt, dk, dvt, subln_g)


def _swa_kernel(sink_ref, qt_ref, k_ref, vt_ref, o_ref, *, tq):
    i = pl.program_id(1)
    n_cols = SWA_Q_HEADS * WINDOW
    half_cols = n_cols // SWA_KV_HEADS
    sink = sink_ref[...]
    row = lax.broadcasted_iota(jnp.int32, (2 * WINDOW, WINDOW), 0)
    qrel = lax.broadcasted_iota(jnp.int32, (2 * WINDOW, WINDOW), 1)
    band = (row - WINDOW <= qrel) & (row > qrel)
    in_current = row >= WINDOW
    z = jnp.zeros((HEAD_DIM, half_cols), BF16)
    for sub in range(tq // WINDOW):
        n = i * (tq // WINDOW) + sub
        prev = jnp.maximum(n - 1, 0)
        kwin = jnp.concatenate([k_ref[0, prev], k_ref[0, n]], axis=0)
        vtwin = jnp.concatenate([vt_ref[0, prev], vt_ref[0, n]], axis=1)
        qt = qt_ref[0, :, sub * WINDOW:(sub + 1) * WINDOW]
        heads = [qt[h * HEAD_DIM:(h + 1) * HEAD_DIM] for h in range(SWA_Q_HEADS)]
        qw = jnp.concatenate(
            [jnp.concatenate(heads[:SWA_GROUP] + [z], axis=1),
             jnp.concatenate([z] + heads[SWA_GROUP:], axis=1)], axis=0)
        s = jnp.dot(kwin, qw, preferred_element_type=F32)
        valid = band & (in_current | (n >= 1))
        s = jnp.concatenate(
            [jnp.where(valid, s[:, h * WINDOW:(h + 1) * WINDOW], NEG) for h in range(SWA_Q_HEADS)], axis=1)
        m = jnp.maximum(jnp.max(s, axis=0, keepdims=True), sink)
        p = jnp.exp2(s - m).astype(BF16)
        acc = jnp.dot(vtwin, p, preferred_element_type=F32)
        den = acc[SWA_KV_COLS:SWA_KV_COLS + 1] + jnp.exp2(sink - m)
        on = acc[:SWA_KV_COLS] / den
        u = jnp.concatenate([on[:HEAD_DIM, :half_cols], on[HEAD_DIM:, half_cols:]], axis=1)
        for hp in range(SWA_Q_HEADS // 2):
            two = jnp.concatenate([u[:, (2 * hp) * WINDOW:(2 * hp + 1) * WINDOW],
                                   u[:, (2 * hp + 1) * WINDOW:(2 * hp + 2) * WINDOW]], axis=0)
            o_ref[0, sub * WINDOW:(sub + 1) * WINDOW, hp * LANES:(hp + 1) * LANES] = two.T.astype(BF16)


def _swa_call(sink_row, sqt, sk, svt, *, tq):
    B, _, S = sqt.shape
    nb = S // WINDOW
    return pl.pallas_call(
        functools.partial(_swa_kernel, tq=tq),
        grid=(B, S // tq),
        in_specs=[
            pl.BlockSpec(sink_row.shape, lambda b, i: (0, 0)),
            pl.BlockSpec((1, SWA_Q_COLS, tq), lambda b, i: (b, 0, i)),
            pl.BlockSpec((1, nb, WINDOW, SWA_KV_COLS), lambda b, i: (b, 0, 0, 0)),
            pl.BlockSpec((1, nb, SWA_VT_ROWS, WINDOW), lambda b, i: (b, 0, 0, 0)),
        ],
        out_specs=pl.BlockSpec((1, tq, SWA_Q_COLS), lambda b, i: (b, i, 0)),
        out_shape=jax.ShapeDtypeStruct((B, S, SWA_Q_COLS), BF16),
        compiler_params=pltpu.CompilerParams(
            dimension_semantics=("parallel", "arbitrary"), vmem_limit_bytes=VMEM_LIMIT_BYTES),
        name="swa_attn",
    )(sink_row, sqt, sk, svt)


def _pack_bf16_pairs(x):
    n = x.shape[1] // 2
    lo = lax.bitcast_convert_type(x[:, :n].astype(BF16).astype(F32), jnp.uint32)
    hi = lax.bitcast_convert_type(x[:, n:].astype(BF16).astype(F32), jnp.uint32)
    return (lo >> 16) | (hi & jnp.uint32(0xFFFF0000))


def _unpack_bf16_pairs(w):
    lo = lax.bitcast_convert_type(w << 16, F32)
    hi = lax.bitcast_convert_type(w & jnp.uint32(0xFFFF0000), F32)
    return jnp.concatenate([lo, hi], axis=1).astype(BF16)


def _mix_kernel(x_ref, od_ref, os_ref, wo_ref, g2_ref, wr_ref, br_ref, x1_ref, n2_ref, rt_ref, cnt_ref):
    tm = x_ref.shape[1]
    lane = lax.broadcasted_iota(jnp.int32, (tm, ROUTER_COLS), 1)
    lane_f = lane.astype(F32)
    big = float(ROUTER_COLS)
    mixed = jnp.concatenate([od_ref[0], os_ref[0]], axis=1)
    h = x_ref[0] + jnp.dot(mixed, wo_ref[...], preferred_element_type=F32)
    x1_ref[0] = h
    n2 = h * lax.rsqrt(jnp.mean(h * h, axis=-1, keepdims=True) + EPS) * g2_ref[...]
    n2_ref[0] = _pack_bf16_pairs(n2)
    n2_hi = n2.astype(BF16)
    n2_lo = (n2 - n2_hi.astype(F32)).astype(BF16)
    parts = jnp.dot(jnp.concatenate([n2_hi, n2_lo], axis=0), wr_ref[...],
                    preferred_element_type=F32)
    logits = ((parts[:tm, :ROUTER_COLS] + parts[tm:, ROUTER_COLS:])
              + (parts[:tm, ROUTER_COLS:] + parts[tm:, :ROUTER_COLS])) + br_ref[...]
    gl = jnp.where(lane < N_GROUPS, logits, -jnp.inf)
    gm = jnp.max(gl, axis=-1, keepdims=True)
    p_top = 1.0 / jnp.sum(jnp.exp(gl - gm), axis=-1, keepdims=True)
    g_idx = jnp.min(jnp.where(gl == gm, lane_f, big), axis=-1, keepdims=True)
    e_lo = N_GROUPS + EXPERTS_PER_GROUP * g_idx
    el = jnp.where((lane_f >= e_lo) & (lane_f < e_lo + EXPERTS_PER_GROUP), logits, -jnp.inf)
    v1 = jnp.max(el, axis=-1, keepdims=True)
    i1 = jnp.min(jnp.where(el == v1, lane_f, big), axis=-1, keepdims=True)
    el2 = jnp.where(lane_f == i1, -jnp.inf, el)
    v2 = jnp.max(el2, axis=-1, keepdims=True)
    i2 = jnp.min(jnp.where(el2 == v2, lane_f, big), axis=-1, keepdims=True)
    e21 = jnp.exp(v2 - v1)
    gate1 = p_top / (1.0 + e21)
    gate2 = p_top * e21 / (1.0 + e21)
    rt_ref[0] = jnp.where(lane == 0, i1 - N_GROUPS,
                jnp.where(lane == 1, i2 - N_GROUPS,
                jnp.where(lane == 2, gate1, jnp.where(lane == 3, gate2, 0.0))))
    chosen = ((lane_f == i1 - N_GROUPS) | (lane_f == i2 - N_GROUPS)).astype(F32)
    cnt_ref[0, 0] = jnp.broadcast_to(jnp.sum(chosen, axis=0, keepdims=True), cnt_ref.shape[2:])


def _mix_call(x, o_diff, o_swa, w_out, g2, w_router, b_router, *, tm):
    B, S, D = x.shape
    const = lambda b, i: (0, 0)
    row = lambda b, i: (b, i, 0)
    nt = S // tm
    return pl.pallas_call(
        _mix_kernel,
        grid=(B, nt),
        in_specs=[
            pl.BlockSpec((1, tm, D), row),
            pl.BlockSpec((1, tm, DIFF_V_COLS), row),
            pl.BlockSpec((1, tm, SWA_Q_COLS), row),
            pl.BlockSpec(w_out.shape, const),
            pl.BlockSpec((1, D), const),
            pl.BlockSpec(w_router.shape, const),
            pl.BlockSpec((1, ROUTER_COLS), const),
        ],
        out_specs=(pl.BlockSpec((1, tm, D), row), pl.BlockSpec((1, tm, D // 2), row),
                   pl.BlockSpec((1, tm, ROUTER_COLS), row),
                   pl.BlockSpec((1, 1, SUBLANES, ROUTER_COLS), lambda b, i: (b, i, 0, 0))),
        out_shape=(jax.ShapeDtypeStruct((B, S, D), F32), jax.ShapeDtypeStruct((B, S, D // 2), jnp.uint32),
                   jax.ShapeDtypeStruct((B, S, ROUTER_COLS), F32),
                   jax.ShapeDtypeStruct((B, nt, SUBLANES, ROUTER_COLS), F32)),
        compiler_params=pltpu.CompilerParams(
            dimension_semantics=("parallel", "parallel"), vmem_limit_bytes=VMEM_LIMIT_BYTES),
        name="outproj_router",
    )(x, o_diff, o_swa, w_out, g2, w_router, b_router)


def _slot_kernel(rt_ref, base_ref, dest_ref, *, tm):
    rt_t = rt_ref[...].T
    e1 = rt_t[0:1].astype(jnp.int32)
    e2 = rt_t[1:2].astype(jnp.int32)
    eid = lax.broadcasted_iota(jnp.int32, (N_EXPERTS, tm), 0)
    oh1 = eid == e1
    oh2 = eid == e2
    earlier = (lax.broadcasted_iota(jnp.int32, (tm, tm), 0)
               < lax.broadcasted_iota(jnp.int32, (tm, tm), 1)).astype(BF16)
    before = jnp.dot((oh1 | oh2).astype(BF16), earlier, preferred_element_type=F32)
    slot = before + base_ref[0][:, 0:1]
    d1 = jnp.sum(jnp.where(oh1, slot, 0.0), axis=0, keepdims=True).astype(jnp.int32)
    d2 = jnp.sum(jnp.where(oh2, slot, 0.0), axis=0, keepdims=True).astype(jnp.int32)
    dest_ref[0] = jnp.concatenate([d1, d2, jnp.zeros((SUBLANES - TOP_K, tm), jnp.int32)], axis=0)


def _slot_call(rt, tile_base, *, tm):
    nt = rt.shape[0] // tm
    return pl.pallas_call(
        functools.partial(_slot_kernel, tm=tm),
        grid=(nt,),
        in_specs=[
            pl.BlockSpec((tm, ROUTER_COLS), lambda t: (t, 0)),
            pl.BlockSpec((1, N_EXPERTS, LANES), lambda t: (t, 0, 0)),
        ],
        out_specs=pl.BlockSpec((1, SUBLANES, tm), lambda t: (t, 0, 0)),
        out_shape=jax.ShapeDtypeStruct((nt, SUBLANES, tm), jnp.int32),
        compiler_params=pltpu.CompilerParams(dimension_semantics=("parallel",)),
        name="moe_slots",
    )(rt, tile_base)


SC_ROW_CHUNK = 64


def _sc_workers():
    info = plsc.get_sparse_core_info()
    return info.num_cores, info.num_cores * info.num_subcores


def _sc_scatter_rows(rows, idx, n_out):
    n, width = rows.shape
    n_cores, n_workers = _sc_workers()
    n_chunks = n // SC_ROW_CHUNK
    per_worker = n_chunks // n_workers
    assert n_chunks % n_workers == 0
    mesh = plsc.VectorSubcoreMesh(core_axis_name="c", subcore_axis_name="s")

    @functools.partial(
        pl.kernel, mesh=mesh,
        out_type=jax.ShapeDtypeStruct((n_out, width), rows.dtype),
        scratch_types=[
            pltpu.VMEM((SC_ROW_CHUNK,), jnp.int32),
            pltpu.VMEM((SC_ROW_CHUNK, width), rows.dtype),
        ],
    )
    def scatter(rows_hbm, idx_hbm, out_hbm, idx_v, rows_v):
        worker = lax.axis_index("s") * n_cores + lax.axis_index("c")

        @pl.loop(0, per_worker)
        def _(i):
            c = worker * per_worker + i
            pltpu.sync_copy(rows_hbm.at[pl.ds(pl.multiple_of(c * SC_ROW_CHUNK, SC_ROW_CHUNK), SC_ROW_CHUNK)], rows_v)
            for k in range(TOP_K):
                pltpu.sync_copy(idx_hbm.at[k, c], idx_v)
                pltpu.sync_copy(rows_v, out_hbm.at[idx_v])

    return scatter(rows, idx)


def _expert_kernel(be_ref, nvalid_ref, next_ref, xs_ref, wg_hbm, wu_hbm, wd_hbm, y_ref,
                   wg_st, wu_st, wd_st, wg_b, wu_b, wd_b, slot_ref, sems):
    b = pl.program_id(0)
    n_valid = nvalid_ref[b]
    sources, staged, cast = (wg_hbm, wu_hbm, wd_hbm), (wg_st, wu_st, wd_st), (wg_b, wu_b, wd_b)

    def weight_copies(expert, slot):
        return [pltpu.make_async_copy(src.at[expert], dst.at[slot], sems.at[slot, i])
                for i, (src, dst) in enumerate(zip(sources, staged))]

    @pl.when(n_valid > 0)
    def _():
        @pl.when((b == 0) | (be_ref[b] != be_ref[jnp.maximum(b - 1, 0)]))
        def _():
            @pl.when(b == 0)
            def _():
                slot_ref[0] = 0
                for copy in weight_copies(be_ref[0], 0):
                    copy.start()

            slot = slot_ref[0]
            for copy in weight_copies(be_ref[b], slot):
                copy.wait()
            for dst, src in zip(cast, staged):
                dst[...] = src[slot].astype(BF16)

            @pl.when(next_ref[b] >= 0)
            def _():
                for copy in weight_copies(next_ref[b], 1 - slot):
                    copy.start()

            slot_ref[0] = 1 - slot

        for c in range(EXPERT_BLOCK // EXPERT_CHUNK):
            rows = pl.ds(c * EXPERT_CHUNK, EXPERT_CHUNK)
            row_id = c * EXPERT_CHUNK + lax.broadcasted_iota(jnp.int32, (EXPERT_CHUNK, xs_ref.shape[1]), 0)
            packed = jnp.where(row_id < n_valid, xs_ref[rows, :], jnp.uint32(0))
            xb = _unpack_bf16_pairs(packed)
            gate = jnp.dot(xb, wg_b[...], preferred_element_type=F32)
            up = jnp.dot(xb, wu_b[...], preferred_element_type=F32)
            hid = (gate * jax.nn.sigmoid(gate) * up).astype(BF16)
            y_ref[rows, :] = _pack_bf16_pairs(jnp.dot(hid, wd_b[...], preferred_element_type=F32))

    @pl.when(n_valid == 0)
    def _():
        y_ref[...] = jnp.zeros_like(y_ref)


def _expert_call(block_expert, n_valid, next_expert, xs, w_gate, w_up, w_down):
    P = xs.shape[0]
    NB = P // EXPERT_BLOCK
    E, D, F = w_gate.shape
    grid_spec = pltpu.PrefetchScalarGridSpec(
        num_scalar_prefetch=3,
        grid=(NB,),
        in_specs=[
            pl.BlockSpec((EXPERT_BLOCK,) + xs.shape[1:], lambda b, *_: (b, 0)),
            pl.BlockSpec(memory_space=pl.ANY),
            pl.BlockSpec(memory_space=pl.ANY),
            pl.BlockSpec(memory_space=pl.ANY),
        ],
        out_specs=pl.BlockSpec((EXPERT_BLOCK, D // 2), lambda b, *_: (b, 0)),
        scratch_shapes=[
            pltpu.VMEM((2, D, F), F32),
            pltpu.VMEM((2, D, F), F32),
            pltpu.VMEM((2, F, D), F32),
            pltpu.VMEM((D, F), BF16),
            pltpu.VMEM((D, F), BF16),
            pltpu.VMEM((F, D), BF16),
            pltpu.SMEM((1,), jnp.int32),
            pltpu.SemaphoreType.DMA((2, 3)),
        ],
    )
    return pl.pallas_call(
        _expert_kernel,
        grid_spec=grid_spec,
        out_shape=jax.ShapeDtypeStruct((P, D // 2), jnp.uint32),
        compiler_params=pltpu.CompilerParams(
            dimension_semantics=("arbitrary",), vmem_limit_bytes=VMEM_LIMIT_BYTES),
        name="moe_experts",
    )(block_expert, n_valid, next_expert, xs, w_gate, w_up, w_down)


def _sc_gather_rows(table, idx):
    n_rows, width = idx.shape[0], table.shape[1]
    n_cores, n_workers = _sc_workers()
    per_worker = n_rows // n_workers
    assert n_rows % (n_workers * 2 * SC_ROW_CHUNK) == 0
    mesh = plsc.VectorSubcoreMesh(core_axis_name="c", subcore_axis_name="s")

    @functools.partial(
        pl.kernel, mesh=mesh,
        out_type=jax.ShapeDtypeStruct((n_rows, width), table.dtype),
        scratch_types=[pltpu.VMEM((SC_ROW_CHUNK,), jnp.int32)] * 2
        + [pltpu.VMEM((SC_ROW_CHUNK, width), table.dtype)] * 2
        + [pltpu.SemaphoreType.DMA] * 4,
    )
    def gather(table_hbm, idx_hbm, out_hbm, idx_a, idx_b, rows_a, rows_b, sem_ga, sem_gb, sem_wa, sem_wb):
        worker = lax.axis_index("s") * n_cores + lax.axis_index("c")
        base = worker * per_worker

        @pl.loop(0, per_worker // SC_ROW_CHUNK, step=2)
        def _(c):
            off_a = pl.multiple_of(base + c * SC_ROW_CHUNK, SC_ROW_CHUNK)
            off_b = pl.multiple_of(off_a + SC_ROW_CHUNK, SC_ROW_CHUNK)
            pltpu.sync_copy(idx_hbm.at[pl.ds(off_a, SC_ROW_CHUNK)], idx_a)
            pltpu.sync_copy(idx_hbm.at[pl.ds(off_b, SC_ROW_CHUNK)], idx_b)
            gather_a = pltpu.async_copy(table_hbm.at[idx_a], rows_a, sem_ga)
            gather_b = pltpu.async_copy(table_hbm.at[idx_b], rows_b, sem_gb)
            gather_a.wait()
            write_a = pltpu.async_copy(rows_a, out_hbm.at[pl.ds(off_a, SC_ROW_CHUNK)], sem_wa)
            gather_b.wait()
            write_b = pltpu.async_copy(rows_b, out_hbm.at[pl.ds(off_b, SC_ROW_CHUNK)], sem_wb)
            write_a.wait()
            write_b.wait()

    return gather(table, idx)


def _combine_kernel(x1_ref, rt_ref, y_ref, fg_ref, o_ref, *, final_norm):
    rt = rt_ref[...]
    y1 = _unpack_bf16_pairs(y_ref[0, 0]).astype(F32)
    y2 = _unpack_bf16_pairs(y_ref[0, 1]).astype(F32)
    h = x1_ref[...] + rt[:, 2:3] * y1 + rt[:, 3:4] * y2
    if final_norm:
        h = h * lax.rsqrt(jnp.mean(h * h, axis=-1, keepdims=True) + EPS) * fg_ref[...]
    o_ref[...] = h


def _combine_call(x1, rt, ysg, final_g, *, tm, final_norm):
    T, D = x1.shape
    return pl.pallas_call(
        functools.partial(_combine_kernel, final_norm=final_norm),
        grid=(T // tm,),
        in_specs=[
            pl.BlockSpec((tm, D), lambda t: (t, 0)),
            pl.BlockSpec((tm, ROUTER_COLS), lambda t: (t, 0)),
            pl.BlockSpec((1, TOP_K, tm, D // 2), lambda t: (t, 0, 0, 0)),
            pl.BlockSpec((1, D), lambda t: (0, 0)),
        ],
        out_specs=pl.BlockSpec((tm, D), lambda t: (t, 0)),
        out_shape=jax.ShapeDtypeStruct((T, D), F32),
        compiler_params=pltpu.CompilerParams(
            dimension_semantics=("parallel",), vmem_limit_bytes=VMEM_LIMIT_BYTES),
        name="moe_combine",
    )(x1, rt, ysg, final_g)


def _slot_layout(tile_counts, n_assign):
    NB = -(-n_assign // EXPERT_BLOCK) + N_EXPERTS
    n_tiles = tile_counts.shape[0]
    tc = tile_counts.astype(F32)
    hp = lax.Precision.HIGHEST
    counts = jnp.sum(tc, axis=0)
    padded = jnp.ceil(counts / EXPERT_BLOCK) * EXPERT_BLOCK
    upper = (jnp.arange(N_EXPERTS)[:, None] < jnp.arange(N_EXPERTS)[None, :]).astype(F32)
    pad_start = jnp.dot(padded, upper, precision=hp)
    pad_end = pad_start + padded
    lower = (jnp.arange(n_tiles)[:, None] > jnp.arange(n_tiles)[None, :]).astype(F32)
    tile_base = pad_start[None, :] + jnp.dot(lower, tc, precision=hp)
    block_start = jnp.arange(NB, dtype=F32) * EXPERT_BLOCK
    block_expert = jnp.minimum(jnp.sum((pad_end[None, :] <= block_start[:, None]).astype(jnp.int32), axis=1),
                               N_EXPERTS - 1)
    mine = block_expert[:, None] == jnp.arange(N_EXPERTS)[None, :]
    run_end = jnp.sum(jnp.where(mine, (pad_start + counts)[None, :], 0.0), axis=1)
    n_valid = jnp.clip(run_end - block_start, 0, EXPERT_BLOCK).astype(jnp.int32)
    eid = jnp.arange(N_EXPERTS)
    later_nonempty = (eid[None, :] > eid[:, None]) & (counts[None, :] > 0)
    next_nonempty = jnp.min(jnp.where(later_nonempty, eid[None, :], N_EXPERTS), axis=1)
    next_nonempty = jnp.where(next_nonempty < N_EXPERTS, next_nonempty, -1)
    next_expert = jnp.sum(jnp.where(mine, next_nonempty[None, :], 0), axis=1).astype(jnp.int32)
    return NB, block_expert.astype(jnp.int32), n_valid, next_expert, tile_base


def _rope_tables(S):
    half = HEAD_DIM // 2
    inv = 1.0 / (ROPE_THETA ** (jnp.arange(0, HEAD_DIM, 2, dtype=F32) / HEAD_DIM))
    pos = jnp.arange(S, dtype=F32)
    ang_l = pos[:, None] * jnp.tile(inv, LANES // half)[None, :]
    sign = jnp.tile(jnp.concatenate([-jnp.ones((half,), F32), jnp.ones((half,), F32)]), LANES // HEAD_DIM)
    ang_t = inv[:, None] * pos[None, :]
    return jnp.cos(ang_l), jnp.sin(ang_l) * sign[None, :], jnp.cos(ang_t), jnp.sin(ang_t)


def _tiles(S):
    tile = min(512, S)
    return tile, tile, tile, tile, min(1024, S)


def kernel(x, norm1_g, w_in, lambda_q1, lambda_k1, lambda_q2, lambda_k2, subln_g, sinks, w_out,
           norm2_g, w_router_group, b_router_group, w_router_expert, b_router_expert,
           w_gate, w_up, w_down, final_g):
    B, S, D = x.shape
    T = B * S
    depth = w_in.shape[0]
    tq, tk, tm_proj, tm_tok, tq_swa = _tiles(S)
    qscale = HEAD_DIM ** -0.5 * math.log2(math.e)
    cos_l, sin_l, cos_t, sin_t = _rope_tables(S)

    c0 = DIFF_QK_COLS
    c1 = 2 * DIFF_QK_COLS
    c2 = c1 + DIFF_V_COLS
    c3 = c2 + SWA_Q_COLS
    c4 = c3 + SWA_KV_COLS
    for l in range(depth):
        lambda_init = 0.8 - 0.6 * math.exp(-0.3 * l)
        w = w_in[l]
        w_nat = jnp.concatenate([w[:, c0:c1], w[:, c3:c4]], axis=1).astype(BF16)
        w_tr = jnp.concatenate([w[:, :c0] * qscale, w[:, c1:c2], w[:, c2:c3] * qscale, w[:, c4:]],
                               axis=1).T.astype(BF16)
        dqt, dk, dvt, sqt, sk, svt = _proj_call(
            x, norm1_g[l][None, :], w_nat, w_tr, cos_l, sin_l, cos_t, sin_t, tm=tm_proj, tk=tk)

        lam_p = jnp.stack([lambda_q1[l], lambda_k1[l], lambda_q2[l], lambda_k2[l]]).astype(F32)
        o_diff = _diff_call(lam_p, dqt, dk, dvt, subln_g[l][None, :].astype(F32),
                            tq=tq, tk=tk, lambda_init=lambda_init)
        sink_row = jnp.repeat(sinks[l].astype(F32) * math.log2(math.e), WINDOW)[None, :]
        o_swa = _swa_call(sink_row, sqt, sk, svt, tq=tq_swa)

        wo_b = w_out[l].astype(BF16)
        w_router = jnp.zeros((D, ROUTER_COLS), F32)
        w_router = w_router.at[:, :N_GROUPS].set(w_router_group[l])
        w_router = w_router.at[:, N_GROUPS:N_GROUPS + N_EXPERTS].set(w_router_expert[l])
        w_router_hi = w_router.astype(BF16)
        w_router_lo = (w_router - w_router_hi.astype(F32)).astype(BF16)
        w_router = jnp.concatenate([w_router_hi, w_router_lo], axis=1)
        b_router = jnp.zeros((1, ROUTER_COLS), F32)
        b_router = b_router.at[0, :N_GROUPS].set(b_router_group[l])
        b_router = b_router.at[0, N_GROUPS:N_GROUPS + N_EXPERTS].set(b_router_expert[l])
        x1, n2p, rt, cnt = _mix_call(x, o_diff, o_swa, wo_b, norm2_g[l][None, :], w_router, b_router, tm=tm_tok)

        rt2 = rt.reshape(T, ROUTER_COLS)
        tile_counts = cnt[:, :, 0, :N_EXPERTS].reshape(T // tm_tok, N_EXPERTS).astype(jnp.int32)
        NB, block_expert, n_valid, next_expert, tile_base = _slot_layout(tile_counts, T * TOP_K)
        tile_base = jnp.broadcast_to(tile_base.astype(F32)[:, :, None], (T // tm_tok, N_EXPERTS, LANES))
        dest = _slot_call(rt2, tile_base, tm=tm_tok)
        scatter_idx = jnp.swapaxes(dest[:, :TOP_K, :], 0, 1).reshape(TOP_K, T // SC_ROW_CHUNK, SC_ROW_CHUNK)
        xs = _sc_scatter_rows(n2p.reshape(T, D // 2), scatter_idx, NB * EXPERT_BLOCK)
        ys = _expert_call(block_expert, n_valid, next_expert, xs, w_gate[l], w_up[l], w_down[l])
        ysg = _sc_gather_rows(ys, dest[:, :TOP_K, :].reshape(T * TOP_K))
        x = _combine_call(x1.reshape(T, D), rt2, ysg.reshape(T // tm_tok, TOP_K, tm_tok, D // 2),
                          final_g[None, :], tm=tm_tok, final_norm=(l == depth - 1)).reshape(B, S, D)
    return x
```

```python
import functools
import math

import jax
import jax.numpy as jnp
from jax import lax
from jax.experimental import pallas as pl
from jax.experimental.pallas import tpu as pltpu
from jax.experimental.pallas import tpu_sc as plsc

HEAD_DIM = 64
DIFF_HEADS = 4
DIFF_V_DIM = 2 * HEAD_DIM
SWA_Q_HEADS = 8
SWA_KV_HEADS = 2
SWA_GROUP = SWA_Q_HEADS // SWA_KV_HEADS
WINDOW = 128
ROPE_THETA = 10000.0
N_GROUPS = 4
EXPERTS_PER_GROUP = 8
N_EXPERTS = N_GROUPS * EXPERTS_PER_GROUP
TOP_K = 2
EXPERT_BLOCK = 512
EXPERT_CHUNK = 256
EPS = 1e-6
NEG = -1e30

DIFF_QK_COLS = DIFF_HEADS * 2 * HEAD_DIM
DIFF_V_COLS = DIFF_HEADS * DIFF_V_DIM
SWA_Q_COLS = SWA_Q_HEADS * HEAD_DIM
SWA_KV_COLS = SWA_KV_HEADS * HEAD_DIM
LANES = 128
SUBLANES = 8
BF16_SUBLANES = 16
VMEM_LIMIT_BYTES = 48 * 1024 * 1024
VT_ROWS = DIFF_V_DIM + BF16_SUBLANES
SWA_VT_ROWS = SWA_KV_COLS + BF16_SUBLANES
ROUTER_COLS = LANES
DIFF_UNROLL = 4
DIFF_S_BUFS = 4
DIFF_Q_TILES = 2

BF16 = jnp.bfloat16
F32 = jnp.float32


def _rope_lanes(x, cos_l, sin_l, first_half):
    rot = jnp.where(first_half, pltpu.roll(x, 96, 1), pltpu.roll(x, 32, 1))
    return x * cos_l + rot * sin_l


def _proj_kernel(x_ref, g_ref, wnat_ref, wtr_ref, cosl_ref, sinl_ref, cost_ref, sint_ref,
                 dqt_ref, dk_ref, dvt_ref, sqt_ref, sk_ref, svt_ref, *, tk):
    x = x_ref[0]
    tm = x.shape[0]
    n1 = x * lax.rsqrt(jnp.mean(x * x, axis=-1, keepdims=True) + EPS) * g_ref[...]
    n1b = n1.astype(BF16)
    nat = jnp.dot(n1b, wnat_ref[...], preferred_element_type=F32)
    tr = lax.dot_general(wtr_ref[...], n1b, (((1,), (1,)), ((), ())),
                         preferred_element_type=F32)

    cos_l, sin_l = cosl_ref[...], sinl_ref[...]
    first_half = (lax.broadcasted_iota(jnp.int32, (tm, LANES), 1) & (HEAD_DIM - 1)) < HEAD_DIM // 2
    for h in range(DIFF_HEADS):
        slab = nat[:, h * LANES:(h + 1) * LANES]
        dk_ref[0, h] = _rope_lanes(slab, cos_l, sin_l, first_half).astype(BF16)
    sk = _rope_lanes(nat[:, DIFF_QK_COLS:DIFF_QK_COLS + LANES], cos_l, sin_l, first_half).astype(BF16)
    for c in range(tm // WINDOW):
        sk_ref[0, c] = sk[c * WINDOW:(c + 1) * WINDOW]

    cos_t, sin_t = cost_ref[...], sint_ref[...]
    half = HEAD_DIM // 2

    def rope_rows(r0):
        x1 = tr[r0:r0 + half]
        x2 = tr[r0 + half:r0 + HEAD_DIM]
        return (x1 * cos_t - x2 * sin_t).astype(BF16), (x1 * sin_t + x2 * cos_t).astype(BF16)

    for h in range(DIFF_HEADS):
        for c in range(2):
            lo, hi = rope_rows(h * 2 * HEAD_DIM + c * HEAD_DIM)
            dqt_ref[0, h, c * HEAD_DIM:c * HEAD_DIM + half] = lo
            dqt_ref[0, h, c * HEAD_DIM + half:(c + 1) * HEAD_DIM] = hi
    ones_rows = (lax.broadcasted_iota(jnp.int32, (BF16_SUBLANES, tk), 0) == 0).astype(BF16)
    for h in range(DIFF_HEADS):
        r0 = DIFF_QK_COLS + h * DIFF_V_DIM
        for c in range(tm // tk):
            dvt_ref[0, h, c, :DIFF_V_DIM] = tr[r0:r0 + DIFF_V_DIM, c * tk:(c + 1) * tk].astype(BF16)
            dvt_ref[0, h, c, DIFF_V_DIM:] = ones_rows

    r0 = DIFF_QK_COLS + DIFF_V_COLS
    for h in range(SWA_Q_HEADS):
        lo, hi = rope_rows(r0 + h * HEAD_DIM)
        sqt_ref[0, h * HEAD_DIM:h * HEAD_DIM + half] = lo
        sqt_ref[0, h * HEAD_DIM + half:(h + 1) * HEAD_DIM] = hi
    r0 += SWA_Q_COLS
    for c in range(tm // WINDOW):
        svt_ref[0, c, :SWA_KV_COLS] = tr[r0:r0 + SWA_KV_COLS, c * WINDOW:(c + 1) * WINDOW].astype(BF16)
        svt_ref[0, c, SWA_KV_COLS:] = ones_rows[:, :WINDOW]


def _proj_call(x, g1, w_nat, w_tr, cos_l, sin_l, cos_t, sin_t, *, tm, tk):
    B, S, D = x.shape
    nkv = S // tk
    grid = (B, S // tm)
    const = lambda b, i: (0, 0)
    out_shape = (
        jax.ShapeDtypeStruct((B, DIFF_HEADS, 2 * HEAD_DIM, S), BF16),
        jax.ShapeDtypeStruct((B, DIFF_HEADS, S, 2 * HEAD_DIM), BF16),
        jax.ShapeDtypeStruct((B, DIFF_HEADS, nkv, VT_ROWS, tk), BF16),
        jax.ShapeDtypeStruct((B, SWA_Q_COLS, S), BF16),
        jax.ShapeDtypeStruct((B, S // WINDOW, WINDOW, SWA_KV_COLS), BF16),
        jax.ShapeDtypeStruct((B, S // WINDOW, SWA_VT_ROWS, WINDOW), BF16),
    )
    return pl.pallas_call(
        functools.partial(_proj_kernel, tk=tk),
        grid=grid,
        in_specs=[
            pl.BlockSpec((1, tm, D), lambda b, i: (b, i, 0)),
            pl.BlockSpec((1, D), const),
            pl.BlockSpec(w_nat.shape, const),
            pl.BlockSpec(w_tr.shape, const),
            pl.BlockSpec((tm, LANES), lambda b, i: (i, 0)),
            pl.BlockSpec((tm, LANES), lambda b, i: (i, 0)),
            pl.BlockSpec((HEAD_DIM // 2, tm), lambda b, i: (0, i)),
            pl.BlockSpec((HEAD_DIM // 2, tm), lambda b, i: (0, i)),
        ],
        out_specs=(
            pl.BlockSpec((1, DIFF_HEADS, 2 * HEAD_DIM, tm), lambda b, i: (b, 0, 0, i)),
            pl.BlockSpec((1, DIFF_HEADS, tm, 2 * HEAD_DIM), lambda b, i: (b, 0, i, 0)),
            pl.BlockSpec((1, DIFF_HEADS, tm // tk, VT_ROWS, tk), lambda b, i: (b, 0, i, 0, 0)),
            pl.BlockSpec((1, SWA_Q_COLS, tm), lambda b, i: (b, 0, i)),
            pl.BlockSpec((1, tm // WINDOW, WINDOW, SWA_KV_COLS), lambda b, i: (b, i, 0, 0)),
            pl.BlockSpec((1, tm // WINDOW, SWA_VT_ROWS, WINDOW), lambda b, i: (b, i, 0, 0)),
        ),
        out_shape=out_shape,
        compiler_params=pltpu.CompilerParams(
            dimension_semantics=("parallel", "parallel"), vmem_limit_bytes=VMEM_LIMIT_BYTES),
        name="proj_rope",
    )(x, g1, w_nat, w_tr, cos_l, sin_l, cos_t, sin_t)


def _diff_kernel(lam_ref, qt_ref, k_ref, vt_ref, g_ref, o_ref, *scratch, tq, tk, lambda_init):
    step = pl.program_id(2)
    s_bufs = scratch[:DIFF_S_BUFS]
    top_bufs = scratch[DIFF_S_BUFS:2 * DIFF_S_BUFS]
    state = scratch[2 * DIFF_S_BUFS:2 * DIFF_S_BUFS + 2 * DIFF_Q_TILES]
    bias_ref = scratch[-1]

    @pl.when(step == 0)
    def _():
        r = lax.broadcasted_iota(jnp.int32, (tk, 2 * tq), 0)
        c = lax.broadcasted_iota(jnp.int32, (tk, 2 * tq), 1) & (tq - 1)
        bias_ref[...] = jnp.where(r <= c, 0.0, NEG).astype(F32)

    lam_p = lam_ref[...]
    lam = (jnp.exp(jnp.sum(lam_p[0:1] * lam_p[1:2], axis=-1, keepdims=True))
           - jnp.exp(jnp.sum(lam_p[2:3] * lam_p[3:4], axis=-1, keepdims=True)) + lambda_init)

    for sub in range(DIFF_Q_TILES):
        _diff_query_tile(step * DIFF_Q_TILES + sub, qt_ref[0, 0, :, sub * tq:(sub + 1) * tq], k_ref, vt_ref,
                         g_ref, o_ref.at[0, pl.ds(sub * tq, tq), :], s_bufs, top_bufs,
                         state[2 * sub], state[2 * sub + 1], bias_ref, lam,
                         tq=tq, tk=tk, lambda_init=lambda_init)


def _diff_query_tile(i, qt, k_ref, vt_ref, g_ref, o_ref, s_bufs, top_bufs, m_ref, acc_ref, bias_ref, lam,
                     *, tq, tk, lambda_init):
    z = jnp.zeros((HEAD_DIM, tq), BF16)
    qw = jnp.concatenate([jnp.concatenate([qt[:HEAD_DIM], z], axis=1),
                          jnp.concatenate([z, qt[HEAD_DIM:]], axis=1)], axis=0)

    def scores(j, par):
        kt = k_ref[0, 0, pl.ds(pl.multiple_of(j * tk, tk), tk), :]
        s = jnp.dot(kt, qw, preferred_element_type=F32)
        s_bufs[par][...] = s
        top_bufs[par][...] = jnp.max(s, axis=0, keepdims=True)

    def absorb(j, par, masked):
        s = s_bufs[par][...]
        if masked:
            s = s + bias_ref[...]
            top = jnp.max(s, axis=0, keepdims=True)
        else:
            top = top_bufs[par][...]
        m = m_ref[...]
        m_new = jnp.maximum(m, top)
        alpha = jnp.exp2(m - m_new)
        p = jnp.exp2(s - m_new).astype(BF16)
        m_ref[...] = m_new
        pv = jnp.dot(vt_ref[0, 0, j], p, preferred_element_type=F32)
        acc_ref[...] = alpha * acc_ref[...] + pv

    m_ref[...] = jnp.full(m_ref.shape, NEG, F32)
    acc_ref[...] = jnp.zeros(acc_ref.shape, F32)

    nfull = (i * tq) // tk
    scores(nfull, 0)
    scores(0, 1)
    absorb(nfull, 0, True)

    def group(t, c):
        j = DIFF_UNROLL * t
        for idx in range(DIFF_UNROLL):
            scores(j + idx + 1, (idx + 2) % DIFF_S_BUFS)
            absorb(j + idx, (idx + 1) % DIFF_S_BUFS, False)
        return c

    lax.fori_loop(0, nfull // DIFF_UNROLL, group, 0)

    for rem in range(1, DIFF_UNROLL):
        @pl.when(nfull % DIFF_UNROLL == rem)
        def _():
            first = nfull - rem
            for idx in range(rem):
                if idx + 1 < rem:
                    scores(first + idx + 1, (idx + 2) % DIFF_S_BUFS)
                absorb(first + idx, (idx + 1) % DIFF_S_BUFS, False)

    inv_l = 1.0 / acc_ref[DIFF_V_DIM:DIFF_V_DIM + 1, :]
    o = (acc_ref[:DIFF_V_DIM, :tq] * inv_l[:, :tq]
         - lam * (acc_ref[:DIFF_V_DIM, tq:] * inv_l[:, tq:]))
    o = o * lax.rsqrt(jnp.mean(o * o, axis=0, keepdims=True) + EPS)
    o_ref[...] = (o.T * g_ref[...] * (1.0 - lambda_init)).astype(BF16)


def _diff_call(lam_p, dqt, dk, dvt, subln_g, *, tq, tk, lambda_init):
    B, H, _, S = dqt.shape
    assert tk == tq and S % tk == 0, "the diagonal tile's causal pattern is built for square tiles"
    nkv = S // tk
    tq_step = DIFF_Q_TILES * tq
    assert S % tq_step == 0
    grid = (B, H, S // tq_step)
    return pl.pallas_call(
        functools.partial(_diff_kernel, tq=tq, tk=tk, lambda_init=lambda_init),
        grid=grid,
        in_specs=[
            pl.BlockSpec(lam_p.shape, lambda b, h, i: (0, 0)),
            pl.BlockSpec((1, 1, 2 * HEAD_DIM, tq_step), lambda b, h, i: (b, h, 0, i)),
            pl.BlockSpec((1, 1, S, 2 * HEAD_DIM), lambda b, h, i: (b, h, 0, 0)),
            pl.BlockSpec((1, 1, nkv, VT_ROWS, tk), lambda b, h, i: (b, h, 0, 0, 0)),
            pl.BlockSpec((1, DIFF_V_DIM), lambda b, h, i: (0, 0)),
        ],
        out_specs=pl.BlockSpec((1, tq_step, DIFF_V_DIM), lambda b, h, i: (b, i, h)),
        out_shape=jax.ShapeDtypeStruct((B, S, DIFF_V_COLS), BF16),
        scratch_shapes=[pltpu.VMEM((tk, 2 * tq), F32)] * DIFF_S_BUFS + [
            pltpu.VMEM((1, 2 * tq), F32)] * DIFF_S_BUFS + [
            pltpu.VMEM((1, 2 * tq), F32),
            pltpu.VMEM((VT_ROWS, 2 * tq), F32)] * DIFF_Q_TILES + [
            pltpu.VMEM((tk, 2 * tq), F32),
        ],
        compiler_params=pltpu.CompilerParams(
            dimension_semantics=("parallel", "parallel", "arbitrary"),
            vmem_limit_bytes=VMEM_LIMIT_BYTES),
        name="diff_attn",
    )(lam_p, dqt, dk, dvt, subln_g)


def _swa_kernel(sink_ref, qt_ref, k_ref, vt_ref, o_ref, *, tq):
    i = pl.program_id(1)
    n_cols = SWA_Q_HEADS * WINDOW
    half_cols = n_cols // SWA_KV_HEADS
    sink = sink_ref[...]
    row = lax.broadcasted_iota(jnp.int32, (2 * WINDOW, WINDOW), 0)
    qrel = lax.broadcasted_iota(jnp.int32, (2 * WINDOW, WINDOW), 1)
    band = (row - WINDOW <= qrel) & (row > qrel)
    in_current = row >= WINDOW
    z = jnp.zeros((HEAD_DIM, half_cols), BF16)
    for sub in range(tq // WINDOW):
        n = i * (tq // WINDOW) + sub
        prev = jnp.maximum(n - 1, 0)
        kwin = jnp.concatenate([k_ref[0, prev], k_ref[0, n]], axis=0)
        vtwin = jnp.concatenate([vt_ref[0, prev], vt_ref[0, n]], axis=1)
        qt = qt_ref[0, :, sub * WINDOW:(sub + 1) * WINDOW]
        heads = [qt[h * HEAD_DIM:(h + 1) * HEAD_DIM] for h in range(SWA_Q_HEADS)]
        qw = jnp.concatenate(
            [jnp.concatenate(heads[:SWA_GROUP] + [z], axis=1),
             jnp.concatenate([z] + heads[SWA_GROUP:], axis=1)], axis=0)
        s = jnp.dot(kwin, qw, preferred_element_type=F32)
        valid = band & (in_current | (n >= 1))
        s = jnp.concatenate(
            [jnp.where(valid, s[:, h * WINDOW:(h + 1) * WINDOW], NEG) for h in range(SWA_Q_HEADS)], axis=1)
        m = jnp.maximum(jnp.max(s, axis=0, keepdims=True), sink)
        p = jnp.exp2(s - m).astype(BF16)
        acc = jnp.dot(vtwin, p, preferred_element_type=F32)
        den = acc[SWA_KV_COLS:SWA_KV_COLS + 1] + jnp.exp2(sink - m)
        on = acc[:SWA_KV_COLS] / den
        u = jnp.concatenate([on[:HEAD_DIM, :half_cols], on[HEAD_DIM:, half_cols:]], axis=1)
        for hp in range(SWA_Q_HEADS // 2):
            two = jnp.concatenate([u[:, (2 * hp) * WINDOW:(2 * hp + 1) * WINDOW],
                                   u[:, (2 * hp + 1) * WINDOW:(2 * hp + 2) * WINDOW]], axis=0)
            o_ref[0, sub * WINDOW:(sub + 1) * WINDOW, hp * LANES:(hp + 1) * LANES] = two.T.astype(BF16)


def _swa_call(sink_row, sqt, sk, svt, *, tq):
    B, _, S = sqt.shape
    nb = S // WINDOW
    return pl.pallas_call(
        functools.partial(_swa_kernel, tq=tq),
        grid=(B, S // tq),
        in_specs=[
            pl.BlockSpec(sink_row.shape, lambda b, i: (0, 0)),
            pl.BlockSpec((1, SWA_Q_COLS, tq), lambda b, i: (b, 0, i)),
            pl.BlockSpec((1, nb, WINDOW, SWA_KV_COLS), lambda b, i: (b, 0, 0, 0)),
            pl.BlockSpec((1, nb, SWA_VT_ROWS, WINDOW), lambda b, i: (b, 0, 0, 0)),
        ],
        out_specs=pl.BlockSpec((1, tq, SWA_Q_COLS), lambda b, i: (b, i, 0)),
        out_shape=jax.ShapeDtypeStruct((B, S, SWA_Q_COLS), BF16),
        compiler_params=pltpu.CompilerParams(
            dimension_semantics=("parallel", "arbitrary"), vmem_limit_bytes=VMEM_LIMIT_BYTES),
        name="swa_attn",
    )(sink_row, sqt, sk, svt)


def _pack_bf16_pairs(x):
    n = x.shape[1] // 2
    lo = lax.bitcast_convert_type(x[:, :n].astype(BF16).astype(F32), jnp.uint32)
    hi = lax.bitcast_convert_type(x[:, n:].astype(BF16).astype(F32), jnp.uint32)
    return (lo >> 16) | (hi & jnp.uint32(0xFFFF0000))


def _unpack_bf16_pairs(w):
    lo = lax.bitcast_convert_type(w << 16, F32)
    hi = lax.bitcast_convert_type(w & jnp.uint32(0xFFFF0000), F32)
    return jnp.concatenate([lo, hi], axis=1).astype(BF16)


def _mix_kernel(x_ref, od_ref, os_ref, wo_ref, g2_ref, wr_ref, br_ref, x1_ref, n2_ref, rt_ref, cnt_ref):
    tm = x_ref.shape[1]
    lane = lax.broadcasted_iota(jnp.int32, (tm, ROUTER_COLS), 1)
    lane_f = lane.astype(F32)
    big = float(ROUTER_COLS)
    mixed = jnp.concatenate([od_ref[0], os_ref[0]], axis=1)
    h = x_ref[0] + jnp.dot(mixed, wo_ref[...], preferred_element_type=F32)
    x1_ref[0] = h
    n2 = h * lax.rsqrt(jnp.mean(h * h, axis=-1, keepdims=True) + EPS) * g2_ref[...]
    n2_ref[0] = _pack_bf16_pairs(n2)
    n2_hi = n2.astype(BF16)
    n2_lo = (n2 - n2_hi.astype(F32)).astype(BF16)
    parts = jnp.dot(jnp.concatenate([n2_hi, n2_lo], axis=0), wr_ref[...],
                    preferred_element_type=F32)
    logits = ((parts[:tm, :ROUTER_COLS] + parts[tm:, ROUTER_COLS:])
              + (parts[:tm, ROUTER_COLS:] + parts[tm:, :ROUTER_COLS])) + br_ref[...]
    gl = jnp.where(lane < N_GROUPS, logits, -jnp.inf)
    gm = jnp.max(gl, axis=-1, keepdims=True)
    p_top = 1.0 / jnp.sum(jnp.exp(gl - gm), axis=-1, keepdims=True)
    g_idx = jnp.min(jnp.where(gl == gm, lane_f, big), axis=-1, keepdims=True)
    e_lo = N_GROUPS + EXPERTS_PER_GROUP * g_idx
    el = jnp.where((lane_f >= e_lo) & (lane_f < e_lo + EXPERTS_PER_GROUP), logits, -jnp.inf)
    v1 = jnp.max(el, axis=-1, keepdims=True)
    i1 = jnp.min(jnp.where(el == v1, lane_f, big), axis=-1, keepdims=True)
    el2 = jnp.where(lane_f == i1, -jnp.inf, el)
    v2 = jnp.max(el2, axis=-1, keepdims=True)
    i2 = jnp.min(jnp.where(el2 == v2, lane_f, big), axis=-1, keepdims=True)
    e21 = jnp.exp(v2 - v1)
    gate1 = p_top / (1.0 + e21)
    gate2 = p_top * e21 / (1.0 + e21)
    rt_ref[0] = jnp.where(lane == 0, i1 - N_GROUPS,
                jnp.where(lane == 1, i2 - N_GROUPS,
                jnp.where(lane == 2, gate1, jnp.where(lane == 3, gate2, 0.0))))
    chosen = ((lane_f == i1 - N_GROUPS) | (lane_f == i2 - N_GROUPS)).astype(F32)
    cnt_ref[0, 0] = jnp.broadcast_to(jnp.sum(chosen, axis=0, keepdims=True), cnt_ref.shape[2:])


def _mix_call(x, o_diff, o_swa, w_out, g2, w_router, b_router, *, tm):
    B, S, D = x.shape
    const = lambda b, i: (0, 0)
    row = lambda b, i: (b, i, 0)
    nt = S // tm
    return pl.pallas_call(
        _mix_kernel,
        grid=(B, nt),
        in_specs=[
            pl.BlockSpec((1, tm, D), row),
            pl.BlockSpec((1, tm, DIFF_V_COLS), row),
            pl.BlockSpec((1, tm, SWA_Q_COLS), row),
            pl.BlockSpec(w_out.shape, const),
            pl.BlockSpec((1, D), const),
            pl.BlockSpec(w_router.shape, const),
            pl.BlockSpec((1, ROUTER_COLS), const),
        ],
        out_specs=(pl.BlockSpec((1, tm, D), row), pl.BlockSpec((1, tm, D // 2), row),
                   pl.BlockSpec((1, tm, ROUTER_COLS), row),
                   pl.BlockSpec((1, 1, SUBLANES, ROUTER_COLS), lambda b, i: (b, i, 0, 0))),
        out_shape=(jax.ShapeDtypeStruct((B, S, D), F32), jax.ShapeDtypeStruct((B, S, D // 2), jnp.uint32),
                   jax.ShapeDtypeStruct((B, S, ROUTER_COLS), F32),
                   jax.ShapeDtypeStruct((B, nt, SUBLANES, ROUTER_COLS), F32)),
        compiler_params=pltpu.CompilerParams(
            dimension_semantics=("parallel", "parallel"), vmem_limit_bytes=VMEM_LIMIT_BYTES),
        name="outproj_router",
    )(x, o_diff, o_swa, w_out, g2, w_router, b_router)


def _slot_kernel(rt_ref, base_ref, dest_ref, *, tm):
    rt_t = rt_ref[...].T
    e1 = rt_t[0:1].astype(jnp.int32)
    e2 = rt_t[1:2].astype(jnp.int32)
    eid = lax.broadcasted_iota(jnp.int32, (N_EXPERTS, tm), 0)
    oh1 = eid == e1
    oh2 = eid == e2
    earlier = (lax.broadcasted_iota(jnp.int32, (tm, tm), 0)
               < lax.broadcasted_iota(jnp.int32, (tm, tm), 1)).astype(BF16)
    before = jnp.dot((oh1 | oh2).astype(BF16), earlier, preferred_element_type=F32)
    slot = before + base_ref[0][:, 0:1]
    d1 = jnp.sum(jnp.where(oh1, slot, 0.0), axis=0, keepdims=True).astype(jnp.int32)
    d2 = jnp.sum(jnp.where(oh2, slot, 0.0), axis=0, keepdims=True).astype(jnp.int32)
    dest_ref[0] = jnp.concatenate([d1, d2, jnp.zeros((SUBLANES - TOP_K, tm), jnp.int32)], axis=0)


def _slot_call(rt, tile_base, *, tm):
    nt = rt.shape[0] // tm
    return pl.pallas_call(
        functools.partial(_slot_kernel, tm=tm),
        grid=(nt,),
        in_specs=[
            pl.BlockSpec((tm, ROUTER_COLS), lambda t: (t, 0)),
            pl.BlockSpec((1, N_EXPERTS, LANES), lambda t: (t, 0, 0)),
        ],
        out_specs=pl.BlockSpec((1, SUBLANES, tm), lambda t: (t, 0, 0)),
        out_shape=jax.ShapeDtypeStruct((nt, SUBLANES, tm), jnp.int32),
        compiler_params=pltpu.CompilerParams(dimension_semantics=("parallel",)),
        name="moe_slots",
    )(rt, tile_base)


SC_ROW_CHUNK = 64


def _sc_workers():
    info = plsc.get_sparse_core_info()
    return info.num_cores, info.num_cores * info.num_subcores


def _sc_scatter_rows(rows, idx, n_out):
    n, width = rows.shape
    n_cores, n_workers = _sc_workers()
    n_chunks = n // SC_ROW_CHUNK
    per_worker = n_chunks // n_workers
    assert n_chunks % (2 * n_workers) == 0
    mesh = plsc.VectorSubcoreMesh(core_axis_name="c", subcore_axis_name="s")

    @functools.partial(
        pl.kernel, mesh=mesh,
        out_type=jax.ShapeDtypeStruct((n_out, width), rows.dtype),
        scratch_types=[pltpu.VMEM((SC_ROW_CHUNK,), jnp.int32)] * (2 * TOP_K)
        + [pltpu.VMEM((SC_ROW_CHUNK, width), rows.dtype)] * 2
        + [pltpu.SemaphoreType.DMA] * (2 + 2 * TOP_K),
    )
    def scatter(rows_hbm, idx_hbm, out_hbm, *scratch):
        idx_v = scratch[:2 * TOP_K]
        rows_v = scratch[2 * TOP_K:2 * TOP_K + 2]
        load_sems, store_sems = scratch[-(2 + 2 * TOP_K):-2 * TOP_K], scratch[-2 * TOP_K:]
        worker = lax.axis_index("s") * n_cores + lax.axis_index("c")

        @pl.loop(0, per_worker, step=2)
        def _(i):
            loads = []
            for half in range(2):
                c = worker * per_worker + i + half
                src = rows_hbm.at[pl.ds(pl.multiple_of(c * SC_ROW_CHUNK, SC_ROW_CHUNK), SC_ROW_CHUNK)]
                loads.append(pltpu.async_copy(src, rows_v[half], load_sems[half]))
                for k in range(TOP_K):
                    pltpu.sync_copy(idx_hbm.at[k, c], idx_v[half * TOP_K + k])
            stores = []
            for half in range(2):
                loads[half].wait()
                for k in range(TOP_K):
                    j = half * TOP_K + k
                    stores.append(pltpu.async_copy(rows_v[half], out_hbm.at[idx_v[j]], store_sems[j]))
            for store in stores:
                store.wait()

    return scatter(rows, idx)


def _expert_kernel(be_ref, nvalid_ref, next_ref, xs_ref, wg_hbm, wu_hbm, wd_hbm, y_ref,
                   wg_st, wu_st, wd_st, wg_b, wu_b, wd_b, slot_ref, sems):
    b = pl.program_id(0)
    n_valid = nvalid_ref[b]
    sources, staged, cast = (wg_hbm, wu_hbm, wd_hbm), (wg_st, wu_st, wd_st), (wg_b, wu_b, wd_b)

    def weight_copies(expert, slot):
        return [pltpu.make_async_copy(src.at[expert], dst.at[slot], sems.at[slot, i])
                for i, (src, dst) in enumerate(zip(sources, staged))]

    @pl.when(n_valid > 0)
    def _():
        @pl.when((b == 0) | (be_ref[b] != be_ref[jnp.maximum(b - 1, 0)]))
        def _():
            @pl.when(b == 0)
            def _():
                slot_ref[0] = 0
                for copy in weight_copies(be_ref[0], 0):
                    copy.start()

            slot = slot_ref[0]
            for copy in weight_copies(be_ref[b], slot):
                copy.wait()
            for dst, src in zip(cast, staged):
                dst[...] = src[slot].astype(BF16)

            @pl.when(next_ref[b] >= 0)
            def _():
                for copy in weight_copies(next_ref[b], 1 - slot):
                    copy.start()

            slot_ref[0] = 1 - slot

        for c in range(EXPERT_BLOCK // EXPERT_CHUNK):
            rows = pl.ds(c * EXPERT_CHUNK, EXPERT_CHUNK)
            row_id = c * EXPERT_CHUNK + lax.broadcasted_iota(jnp.int32, (EXPERT_CHUNK, xs_ref.shape[1]), 0)
            packed = jnp.where(row_id < n_valid, xs_ref[rows, :], jnp.uint32(0))
            xb = _unpack_bf16_pairs(packed)
            gate = jnp.dot(xb, wg_b[...], preferred_element_type=F32)
            up = jnp.dot(xb, wu_b[...], preferred_element_type=F32)
            hid = (gate * jax.nn.sigmoid(gate) * up).astype(BF16)
            y_ref[rows, :] = _pack_bf16_pairs(jnp.dot(hid, wd_b[...], preferred_element_type=F32))

    @pl.when(n_valid == 0)
    def _():
        y_ref[...] = jnp.zeros_like(y_ref)


def _expert_call(block_expert, n_valid, next_expert, xs, w_gate, w_up, w_down):
    P = xs.shape[0]
    NB = P // EXPERT_BLOCK
    E, D, F = w_gate.shape
    grid_spec = pltpu.PrefetchScalarGridSpec(
        num_scalar_prefetch=3,
        grid=(NB,),
        in_specs=[
            pl.BlockSpec((EXPERT_BLOCK,) + xs.shape[1:], lambda b, *_: (b, 0)),
            pl.BlockSpec(memory_space=pl.ANY),
            pl.BlockSpec(memory_space=pl.ANY),
            pl.BlockSpec(memory_space=pl.ANY),
        ],
        out_specs=pl.BlockSpec((EXPERT_BLOCK, D // 2), lambda b, *_: (b, 0)),
        scratch_shapes=[
            pltpu.VMEM((2, D, F), F32),
            pltpu.VMEM((2, D, F), F32),
            pltpu.VMEM((2, F, D), F32),
            pltpu.VMEM((D, F), BF16),
            pltpu.VMEM((D, F), BF16),
            pltpu.VMEM((F, D), BF16),
            pltpu.SMEM((1,), jnp.int32),
            pltpu.SemaphoreType.DMA((2, 3)),
        ],
    )
    return pl.pallas_call(
        _expert_kernel,
        grid_spec=grid_spec,
        out_shape=jax.ShapeDtypeStruct((P, D // 2), jnp.uint32),
        compiler_params=pltpu.CompilerParams(
            dimension_semantics=("arbitrary",), vmem_limit_bytes=VMEM_LIMIT_BYTES),
        name="moe_experts",
    )(block_expert, n_valid, next_expert, xs, w_gate, w_up, w_down)


def _sc_gather_rows(table, idx):
    n_rows, width = idx.shape[0], table.shape[1]
    n_cores, n_workers = _sc_workers()
    per_worker = n_rows // n_workers
    assert n_rows % (n_workers * 2 * SC_ROW_CHUNK) == 0
    mesh = plsc.VectorSubcoreMesh(core_axis_name="c", subcore_axis_name="s")

    @functools.partial(
        pl.kernel, mesh=mesh,
        out_type=jax.ShapeDtypeStruct((n_rows, width), table.dtype),
        scratch_types=[pltpu.VMEM((SC_ROW_CHUNK,), jnp.int32)] * 2
        + [pltpu.VMEM((SC_ROW_CHUNK, width), table.dtype)] * 2
        + [pltpu.SemaphoreType.DMA] * 4,
    )
    def gather(table_hbm, idx_hbm, out_hbm, idx_a, idx_b, rows_a, rows_b, sem_ga, sem_gb, sem_wa, sem_wb):
        worker = lax.axis_index("s") * n_cores + lax.axis_index("c")
        base = worker * per_worker

        @pl.loop(0, per_worker // SC_ROW_CHUNK, step=2)
        def _(c):
            off_a = pl.multiple_of(base + c * SC_ROW_CHUNK, SC_ROW_CHUNK)
            off_b = pl.multiple_of(off_a + SC_ROW_CHUNK, SC_ROW_CHUNK)
            pltpu.sync_copy(idx_hbm.at[pl.ds(off_a, SC_ROW_CHUNK)], idx_a)
            pltpu.sync_copy(idx_hbm.at[pl.ds(off_b, SC_ROW_CHUNK)], idx_b)
            gather_a = pltpu.async_copy(table_hbm.at[idx_a], rows_a, sem_ga)
            gather_b = pltpu.async_copy(table_hbm.at[idx_b], rows_b, sem_gb)
            gather_a.wait()
            write_a = pltpu.async_copy(rows_a, out_hbm.at[pl.ds(off_a, SC_ROW_CHUNK)], sem_wa)
            gather_b.wait()
            write_b = pltpu.async_copy(rows_b, out_hbm.at[pl.ds(off_b, SC_ROW_CHUNK)], sem_wb)
            write_a.wait()
            write_b.wait()

    return gather(table, idx)


def _combine_kernel(x1_ref, rt_ref, y_ref, fg_ref, o_ref, *, final_norm):
    rt = rt_ref[...]
    y1 = _unpack_bf16_pairs(y_ref[0, 0]).astype(F32)
    y2 = _unpack_bf16_pairs(y_ref[0, 1]).astype(F32)
    h = x1_ref[...] + rt[:, 2:3] * y1 + rt[:, 3:4] * y2
    if final_norm:
        h = h * lax.rsqrt(jnp.mean(h * h, axis=-1, keepdims=True) + EPS) * fg_ref[...]
    o_ref[...] = h


def _combine_call(x1, rt, ysg, final_g, *, tm, final_norm):
    T, D = x1.shape
    return pl.pallas_call(
        functools.partial(_combine_kernel, final_norm=final_norm),
        grid=(T // tm,),
        in_specs=[
            pl.BlockSpec((tm, D), lambda t: (t, 0)),
            pl.BlockSpec((tm, ROUTER_COLS), lambda t: (t, 0)),
            pl.BlockSpec((1, TOP_K, tm, D // 2), lambda t: (t, 0, 0, 0)),
            pl.BlockSpec((1, D), lambda t: (0, 0)),
        ],
        out_specs=pl.BlockSpec((tm, D), lambda t: (t, 0)),
        out_shape=jax.ShapeDtypeStruct((T, D), F32),
        compiler_params=pltpu.CompilerParams(
            dimension_semantics=("parallel",), vmem_limit_bytes=VMEM_LIMIT_BYTES),
        name="moe_combine",
    )(x1, rt, ysg, final_g)


def _slot_layout(tile_counts, n_assign):
    NB = -(-n_assign // EXPERT_BLOCK) + N_EXPERTS
    n_tiles = tile_counts.shape[0]
    tc = tile_counts.astype(F32)
    hp = lax.Precision.HIGHEST
    counts = jnp.sum(tc, axis=0)
    padded = jnp.ceil(counts / EXPERT_BLOCK) * EXPERT_BLOCK
    upper = (jnp.arange(N_EXPERTS)[:, None] < jnp.arange(N_EXPERTS)[None, :]).astype(F32)
    pad_start = jnp.dot(padded, upper, precision=hp)
    pad_end = pad_start + padded
    lower = (jnp.arange(n_tiles)[:, None] > jnp.arange(n_tiles)[None, :]).astype(F32)
    tile_base = pad_start[None, :] + jnp.dot(lower, tc, precision=hp)
    block_start = jnp.arange(NB, dtype=F32) * EXPERT_BLOCK
    block_expert = jnp.minimum(jnp.sum((pad_end[None, :] <= block_start[:, None]).astype(jnp.int32), axis=1),
                               N_EXPERTS - 1)
    mine = block_expert[:, None] == jnp.arange(N_EXPERTS)[None, :]
    run_end = jnp.sum(jnp.where(mine, (pad_start + counts)[None, :], 0.0), axis=1)
    n_valid = jnp.clip(run_end - block_start, 0, EXPERT_BLOCK).astype(jnp.int32)
    eid = jnp.arange(N_EXPERTS)
    later_nonempty = (eid[None, :] > eid[:, None]) & (counts[None, :] > 0)
    next_nonempty = jnp.min(jnp.where(later_nonempty, eid[None, :], N_EXPERTS), axis=1)
    next_nonempty = jnp.where(next_nonempty < N_EXPERTS, next_nonempty, -1)
    next_expert = jnp.sum(jnp.where(mine, next_nonempty[None, :], 0), axis=1).astype(jnp.int32)
    return NB, block_expert.astype(jnp.int32), n_valid, next_expert, tile_base


def _rope_tables(S):
    half = HEAD_DIM // 2
    inv = 1.0 / (ROPE_THETA ** (jnp.arange(0, HEAD_DIM, 2, dtype=F32) / HEAD_DIM))
    pos = jnp.arange(S, dtype=F32)
    ang_l = pos[:, None] * jnp.tile(inv, LANES // half)[None, :]
    sign = jnp.tile(jnp.concatenate([-jnp.ones((half,), F32), jnp.ones((half,), F32)]), LANES // HEAD_DIM)
    ang_t = inv[:, None] * pos[None, :]
    return jnp.cos(ang_l), jnp.sin(ang_l) * sign[None, :], jnp.cos(ang_t), jnp.sin(ang_t)


def _tiles(S):
    tile = min(512, S)
    return tile, tile, tile, tile, min(1024, S)


def kernel(x, norm1_g, w_in, lambda_q1, lambda_k1, lambda_q2, lambda_k2, subln_g, sinks, w_out,
           norm2_g, w_router_group, b_router_group, w_router_expert, b_router_expert,
           w_gate, w_up, w_down, final_g):
    B, S, D = x.shape
    T = B * S
    depth = w_in.shape[0]
    tq, tk, tm_proj, tm_tok, tq_swa = _tiles(S)
    qscale = HEAD_DIM ** -0.5 * math.log2(math.e)
    cos_l, sin_l, cos_t, sin_t = _rope_tables(S)

    c0 = DIFF_QK_COLS
    c1 = 2 * DIFF_QK_COLS
    c2 = c1 + DIFF_V_COLS
    c3 = c2 + SWA_Q_COLS
    c4 = c3 + SWA_KV_COLS
    for l in range(depth):
        lambda_init = 0.8 - 0.6 * math.exp(-0.3 * l)
        w = w_in[l]
        w_nat = jnp.concatenate([w[:, c0:c1], w[:, c3:c4]], axis=1).astype(BF16)
        w_tr = jnp.concatenate([w[:, :c0] * qscale, w[:, c1:c2], w[:, c2:c3] * qscale, w[:, c4:]],
                               axis=1).T.astype(BF16)
        dqt, dk, dvt, sqt, sk, svt = _proj_call(
            x, norm1_g[l][None, :], w_nat, w_tr, cos_l, sin_l, cos_t, sin_t, tm=tm_proj, tk=tk)

        lam_p = jnp.stack([lambda_q1[l], lambda_k1[l], lambda_q2[l], lambda_k2[l]]).astype(F32)
        o_diff = _diff_call(lam_p, dqt, dk, dvt, subln_g[l][None, :].astype(F32),
                            tq=tq, tk=tk, lambda_init=lambda_init)
        sink_row = jnp.repeat(sinks[l].astype(F32) * math.log2(math.e), WINDOW)[None, :]
        o_swa = _swa_call(sink_row, sqt, sk, svt, tq=tq_swa)

        wo_b = w_out[l].astype(BF16)
        w_router = jnp.zeros((D, ROUTER_COLS), F32)
        w_router = w_router.at[:, :N_GROUPS].set(w_router_group[l])
        w_router = w_router.at[:, N_GROUPS:N_GROUPS + N_EXPERTS].set(w_router_expert[l])
        w_router_hi = w_router.astype(BF16)
        w_router_lo = (w_router - w_router_hi.astype(F32)).astype(BF16)
        w_router = jnp.concatenate([w_router_hi, w_router_lo], axis=1)
        b_router = jnp.zeros((1, ROUTER_COLS), F32)
        b_router = b_router.at[0, :N_GROUPS].set(b_router_group[l])
        b_router = b_router.at[0, N_GROUPS:N_GROUPS + N_EXPERTS].set(b_router_expert[l])
        x1, n2p, rt, cnt = _mix_call(x, o_diff, o_swa, wo_b, norm2_g[l][None, :], w_router, b_router, tm=tm_tok)

        rt2 = rt.reshape(T, ROUTER_COLS)
        tile_counts = cnt[:, :, 0, :N_EXPERTS].reshape(T // tm_tok, N_EXPERTS).astype(jnp.int32)
        NB, block_expert, n_valid, next_expert, tile_base = _slot_layout(tile_counts, T * TOP_K)
        tile_base = jnp.broadcast_to(tile_base.astype(F32)[:, :, None], (T // tm_tok, N_EXPERTS, LANES))
        dest = _slot_call(rt2, tile_base, tm=tm_tok)
        scatter_idx = jnp.swapaxes(dest[:, :TOP_K, :], 0, 1).reshape(TOP_K, T // SC_ROW_CHUNK, SC_ROW_CHUNK)
        xs = _sc_scatter_rows(n2p.reshape(T, D // 2), scatter_idx, NB * EXPERT_BLOCK)
        ys = _expert_call(block_expert, n_valid, next_expert, xs, w_gate[l], w_up[l], w_down[l])
        ysg = _sc_gather_rows(ys, dest[:, :TOP_K, :].reshape(T * TOP_K))
        x = _combine_call(x1.reshape(T, D), rt2, ysg.reshape(T // tm_tok, TOP_K, tm_tok, D // 2),
                          final_g[None, :], tm=tm_tok, final_norm=(l == depth - 1)).reshape(B, S, D)
    return x
```

```python
import functools
import math

import jax
import jax.numpy as jnp
from jax import lax
from jax.experimental import pallas as pl
from jax.experimental.pallas import tpu as pltpu
from jax.experimental.pallas import tpu_sc as plsc

HEAD_DIM = 64
DIFF_HEADS = 4
DIFF_V_DIM = 2 * HEAD_DIM
SWA_Q_HEADS = 8
SWA_KV_HEADS = 2
SWA_GROUP = SWA_Q_HEADS // SWA_KV_HEADS
WINDOW = 128
ROPE_THETA = 10000.0
N_GROUPS = 4
EXPERTS_PER_GROUP = 8
N_EXPERTS = N_GROUPS * EXPERTS_PER_GROUP
TOP_K = 2
EXPERT_BLOCK = 512
EXPERT_CHUNK = 256
EPS = 1e-6
NEG = -1e30

DIFF_QK_COLS = DIFF_HEADS * 2 * HEAD_DIM
DIFF_V_COLS = DIFF_HEADS * DIFF_V_DIM
SWA_Q_COLS = SWA_Q_HEADS * HEAD_DIM
SWA_KV_COLS = SWA_KV_HEADS * HEAD_DIM
LANES = 128
SUBLANES = 8
BF16_SUBLANES = 16
VMEM_LIMIT_BYTES = 48 * 1024 * 1024
VT_ROWS = DIFF_V_DIM + BF16_SUBLANES
SWA_VT_ROWS = SWA_KV_COLS + BF16_SUBLANES
ROUTER_COLS = LANES
DIFF_UNROLL = 4
DIFF_S_BUFS = 4
DIFF_Q_TILES = 2

BF16 = jnp.bfloat16
F32 = jnp.float32


def _rope_lanes(x, cos_l, sin_l, first_half):
    rot = jnp.where(first_half, pltpu.roll(x, 96, 1), pltpu.roll(x, 32, 1))
    return x * cos_l + rot * sin_l


def _proj_kernel(x_ref, g_ref, wnat_ref, wtr_ref, cosl_ref, sinl_ref, cost_ref, sint_ref,
                 dqt_ref, dk_ref, dvt_ref, sqt_ref, sk_ref, svt_ref, *, tk):
    x = x_ref[0]
    tm = x.shape[0]
    n1 = x * lax.rsqrt(jnp.mean(x * x, axis=-1, keepdims=True) + EPS) * g_ref[...]
    n1b = n1.astype(BF16)
    nat = jnp.dot(n1b, wnat_ref[...], preferred_element_type=F32)
    tr = lax.dot_general(wtr_ref[...], n1b, (((1,), (1,)), ((), ())),
                         preferred_element_type=F32)

    cos_l, sin_l = cosl_ref[...], sinl_ref[...]
    first_half = (lax.broadcasted_iota(jnp.int32, (tm, LANES), 1) & (HEAD_DIM - 1)) < HEAD_DIM // 2
    for h in range(DIFF_HEADS):
        slab = nat[:, h * LANES:(h + 1) * LANES]
        dk_ref[0, h] = _rope_lanes(slab, cos_l, sin_l, first_half).astype(BF16)
    sk = _rope_lanes(nat[:, DIFF_QK_COLS:DIFF_QK_COLS + LANES], cos_l, sin_l, first_half).astype(BF16)
    for c in range(tm // WINDOW):
        sk_ref[0, c] = sk[c * WINDOW:(c + 1) * WINDOW]

    cos_t, sin_t = cost_ref[...], sint_ref[...]
    half = HEAD_DIM // 2

    def rope_rows(r0):
        x1 = tr[r0:r0 + half]
        x2 = tr[r0 + half:r0 + HEAD_DIM]
        return (x1 * cos_t - x2 * sin_t).astype(BF16), (x1 * sin_t + x2 * cos_t).astype(BF16)

    for h in range(DIFF_HEADS):
        for c in range(2):
            lo, hi = rope_rows(h * 2 * HEAD_DIM + c * HEAD_DIM)
            dqt_ref[0, h, c * HEAD_DIM:c * HEAD_DIM + half] = lo
            dqt_ref[0, h, c * HEAD_DIM + half:(c + 1) * HEAD_DIM] = hi
    ones_rows = (lax.broadcasted_iota(jnp.int32, (BF16_SUBLANES, tk), 0) == 0).astype(BF16)
    for h in range(DIFF_HEADS):
        r0 = DIFF_QK_COLS + h * DIFF_V_DIM
        for c in range(tm // tk):
            dvt_ref[0, h, c, :DIFF_V_DIM] = tr[r0:r0 + DIFF_V_DIM, c * tk:(c + 1) * tk].astype(BF16)
            dvt_ref[0, h, c, DIFF_V_DIM:] = ones_rows

    r0 = DIFF_QK_COLS + DIFF_V_COLS
    for h in range(SWA_Q_HEADS):
        lo, hi = rope_rows(r0 + h * HEAD_DIM)
        sqt_ref[0, h * HEAD_DIM:h * HEAD_DIM + half] = lo
        sqt_ref[0, h * HEAD_DIM + half:(h + 1) * HEAD_DIM] = hi
    r0 += SWA_Q_COLS
    for c in range(tm // WINDOW):
        svt_ref[0, c, :SWA_KV_COLS] = tr[r0:r0 + SWA_KV_COLS, c * WINDOW:(c + 1) * WINDOW].astype(BF16)
        svt_ref[0, c, SWA_KV_COLS:] = ones_rows[:, :WINDOW]


def _proj_call(x, g1, w_nat, w_tr, cos_l, sin_l, cos_t, sin_t, *, tm, tk):
    B, S, D = x.shape
    nkv = S // tk
    grid = (B, S // tm)
    const = lambda b, i: (0, 0)
    out_shape = (
        jax.ShapeDtypeStruct((B, DIFF_HEADS, 2 * HEAD_DIM, S), BF16),
        jax.ShapeDtypeStruct((B, DIFF_HEADS, S, 2 * HEAD_DIM), BF16),
        jax.ShapeDtypeStruct((B, DIFF_HEADS, nkv, VT_ROWS, tk), BF16),
        jax.ShapeDtypeStruct((B, SWA_Q_COLS, S), BF16),
        jax.ShapeDtypeStruct((B, S // WINDOW, WINDOW, SWA_KV_COLS), BF16),
        jax.ShapeDtypeStruct((B, S // WINDOW, SWA_VT_ROWS, WINDOW), BF16),
    )
    return pl.pallas_call(
        functools.partial(_proj_kernel, tk=tk),
        grid=grid,
        in_specs=[
            pl.BlockSpec((1, tm, D), lambda b, i: (b, i, 0)),
            pl.BlockSpec((1, D), const),
            pl.BlockSpec(w_nat.shape, const),
            pl.BlockSpec(w_tr.shape, const),
            pl.BlockSpec((tm, LANES), lambda b, i: (i, 0)),
            pl.BlockSpec((tm, LANES), lambda b, i: (i, 0)),
            pl.BlockSpec((HEAD_DIM // 2, tm), lambda b, i: (0, i)),
            pl.BlockSpec((HEAD_DIM // 2, tm), lambda b, i: (0, i)),
        ],
        out_specs=(
            pl.BlockSpec((1, DIFF_HEADS, 2 * HEAD_DIM, tm), lambda b, i: (b, 0, 0, i)),
            pl.BlockSpec((1, DIFF_HEADS, tm, 2 * HEAD_DIM), lambda b, i: (b, 0, i, 0)),
            pl.BlockSpec((1, DIFF_HEADS, tm // tk, VT_ROWS, tk), lambda b, i: (b, 0, i, 0, 0)),
            pl.BlockSpec((1, SWA_Q_COLS, tm), lambda b, i: (b, 0, i)),
            pl.BlockSpec((1, tm // WINDOW, WINDOW, SWA_KV_COLS), lambda b, i: (b, i, 0, 0)),
            pl.BlockSpec((1, tm // WINDOW, SWA_VT_ROWS, WINDOW), lambda b, i: (b, i, 0, 0)),
        ),
        out_shape=out_shape,
        compiler_params=pltpu.CompilerParams(
            dimension_semantics=("parallel", "parallel"), vmem_limit_bytes=VMEM_LIMIT_BYTES),
        name="proj_rope",
    )(x, g1, w_nat, w_tr, cos_l, sin_l, cos_t, sin_t)


def _diff_kernel(lam_ref, qt_ref, k_ref, vt_ref, g_ref, o_ref, *scratch, tq, tk, lambda_init):
    step = pl.program_id(2)
    s_bufs = scratch[:DIFF_S_BUFS]
    top_bufs = scratch[DIFF_S_BUFS:2 * DIFF_S_BUFS]
    state = scratch[2 * DIFF_S_BUFS:2 * DIFF_S_BUFS + 2 * DIFF_Q_TILES]
    bias_ref = scratch[-1]

    @pl.when(step == 0)
    def _():
        r = lax.broadcasted_iota(jnp.int32, (tk, 2 * tq), 0)
        c = lax.broadcasted_iota(jnp.int32, (tk, 2 * tq), 1) & (tq - 1)
        bias_ref[...] = jnp.where(r <= c, 0.0, NEG).astype(F32)

    lam_p = lam_ref[...]
    lam = (jnp.exp(jnp.sum(lam_p[0:1] * lam_p[1:2], axis=-1, keepdims=True))
           - jnp.exp(jnp.sum(lam_p[2:3] * lam_p[3:4], axis=-1, keepdims=True)) + lambda_init)

    for sub in range(DIFF_Q_TILES):
        _diff_query_tile(step * DIFF_Q_TILES + sub, qt_ref[0, 0, :, sub * tq:(sub + 1) * tq], k_ref, vt_ref,
                         g_ref, o_ref.at[0, pl.ds(sub * tq, tq), :], s_bufs, top_bufs,
                         state[2 * sub], state[2 * sub + 1], bias_ref, lam,
                         tq=tq, tk=tk, lambda_init=lambda_init)


def _diff_query_tile(i, qt, k_ref, vt_ref, g_ref, o_ref, s_bufs, top_bufs, m_ref, acc_ref, bias_ref, lam,
                     *, tq, tk, lambda_init):
    z = jnp.zeros((HEAD_DIM, tq), BF16)
    qw = jnp.concatenate([jnp.concatenate([qt[:HEAD_DIM], z], axis=1),
                          jnp.concatenate([z, qt[HEAD_DIM:]], axis=1)], axis=0)

    def scores(j, par):
        kt = k_ref[0, 0, pl.ds(pl.multiple_of(j * tk, tk), tk), :]
        s = jnp.dot(kt, qw, preferred_element_type=F32)
        s_bufs[par][...] = s
        top_bufs[par][...] = jnp.max(s, axis=0, keepdims=True)

    def absorb(j, par, masked):
        s = s_bufs[par][...]
        if masked:
            s = s + bias_ref[...]
            top = jnp.max(s, axis=0, keepdims=True)
        else:
            top = top_bufs[par][...]
        m = m_ref[...]
        m_new = jnp.maximum(m, top)
        alpha = jnp.exp2(m - m_new)
        p = jnp.exp2(s - m_new).astype(BF16)
        m_ref[...] = m_new
        pv = jnp.dot(vt_ref[0, 0, j], p, preferred_element_type=F32)
        acc_ref[...] = alpha * acc_ref[...] + pv

    m_ref[...] = jnp.full(m_ref.shape, NEG, F32)
    acc_ref[...] = jnp.zeros(acc_ref.shape, F32)

    nfull = (i * tq) // tk
    scores(nfull, 0)
    scores(0, 1)
    absorb(nfull, 0, True)

    def group(t, c):
        j = DIFF_UNROLL * t
        for idx in range(DIFF_UNROLL):
            scores(j + idx + 1, (idx + 2) % DIFF_S_BUFS)
            absorb(j + idx, (idx + 1) % DIFF_S_BUFS, False)
        return c

    lax.fori_loop(0, nfull // DIFF_UNROLL, group, 0)

    for rem in range(1, DIFF_UNROLL):
        @pl.when(nfull % DIFF_UNROLL == rem)
        def _():
            first = nfull - rem
            for idx in range(rem):
                if idx + 1 < rem:
                    scores(first + idx + 1, (idx + 2) % DIFF_S_BUFS)
                absorb(first + idx, (idx + 1) % DIFF_S_BUFS, False)

    inv_l = 1.0 / acc_ref[DIFF_V_DIM:DIFF_V_DIM + 1, :]
    o = (acc_ref[:DIFF_V_DIM, :tq] * inv_l[:, :tq]
         - lam * (acc_ref[:DIFF_V_DIM, tq:] * inv_l[:, tq:]))
    o = o * lax.rsqrt(jnp.mean(o * o, axis=0, keepdims=True) + EPS)
    o_ref[...] = (o.T * g_ref[...] * (1.0 - lambda_init)).astype(BF16)


def _diff_call(lam_p, dqt, dk, dvt, subln_g, *, tq, tk, lambda_init):
    B, H, _, S = dqt.shape
    assert tk == tq and S % tk == 0, "the diagonal tile's causal pattern is built for square tiles"
    nkv = S // tk
    tq_step = DIFF_Q_TILES * tq
    assert S % tq_step == 0
    grid = (B, H, S // tq_step)
    return pl.pallas_call(
        functools.partial(_diff_kernel, tq=tq, tk=tk, lambda_init=lambda_init),
        grid=grid,
        in_specs=[
            pl.BlockSpec(lam_p.shape, lambda b, h, i: (0, 0)),
            pl.BlockSpec((1, 1, 2 * HEAD_DIM, tq_step), lambda b, h, i: (b, h, 0, i)),
            pl.BlockSpec((1, 1, S, 2 * HEAD_DIM), lambda b, h, i: (b, h, 0, 0)),
            pl.BlockSpec((1, 1, nkv, VT_ROWS, tk), lambda b, h, i: (b, h, 0, 0, 0)),
            pl.BlockSpec((1, DIFF_V_DIM), lambda b, h, i: (0, 0)),
        ],
        out_specs=pl.BlockSpec((1, tq_step, DIFF_V_DIM), lambda b, h, i: (b, i, h)),
        out_shape=jax.ShapeDtypeStruct((B, S, DIFF_V_COLS), BF16),
        scratch_shapes=[pltpu.VMEM((tk, 2 * tq), F32)] * DIFF_S_BUFS + [
            pltpu.VMEM((1, 2 * tq), F32)] * DIFF_S_BUFS + [
            pltpu.VMEM((1, 2 * tq), F32),
            pltpu.VMEM((VT_ROWS, 2 * tq), F32)] * DIFF_Q_TILES + [
            pltpu.VMEM((tk, 2 * tq), F32),
        ],
        compiler_params=pltpu.CompilerParams(
            dimension_semantics=("parallel", "parallel", "arbitrary"),
            vmem_limit_bytes=VMEM_LIMIT_BYTES),
        name="diff_attn",
    )(lam_p, dqt, dk, dvt, subln_g)


def _swa_kernel(sink_ref, qt_ref, k_ref, vt_ref, o_ref, *, tq):
    i = pl.program_id(1)
    n_cols = SWA_Q_HEADS * WINDOW
    half_cols = n_cols // SWA_KV_HEADS
    sink = sink_ref[...]
    row = lax.broadcasted_iota(jnp.int32, (2 * WINDOW, WINDOW), 0)
    qrel = lax.broadcasted_iota(jnp.int32, (2 * WINDOW, WINDOW), 1)
    band = (row - WINDOW <= qrel) & (row > qrel)
    in_current = row >= WINDOW
    z = jnp.zeros((HEAD_DIM, half_cols), BF16)
    for sub in range(tq // WINDOW):
        n = i * (tq // WINDOW) + sub
        prev = jnp.maximum(n - 1, 0)
        kwin = jnp.concatenate([k_ref[0, prev], k_ref[0, n]], axis=0)
        vtwin = jnp.concatenate([vt_ref[0, prev], vt_ref[0, n]], axis=1)
        qt = qt_ref[0, :, sub * WINDOW:(sub + 1) * WINDOW]
        heads = [qt[h * HEAD_DIM:(h + 1) * HEAD_DIM] for h in range(SWA_Q_HEADS)]
        qw = jnp.concatenate(
            [jnp.concatenate(heads[:SWA_GROUP] + [z], axis=1),
             jnp.concatenate([z] + heads[SWA_GROUP:], axis=1)], axis=0)
        s = jnp.dot(kwin, qw, preferred_element_type=F32)
        valid = band & (in_current | (n >= 1))
        s = jnp.concatenate(
            [jnp.where(valid, s[:, h * WINDOW:(h + 1) * WINDOW], NEG) for h in range(SWA_Q_HEADS)], axis=1)
        m = jnp.maximum(jnp.max(s, axis=0, keepdims=True), sink)
        p = jnp.exp2(s - m).astype(BF16)
        acc = jnp.dot(vtwin, p, preferred_element_type=F32)
        den = acc[SWA_KV_COLS:SWA_KV_COLS + 1] + jnp.exp2(sink - m)
        on = acc[:SWA_KV_COLS] / den
        u = jnp.concatenate([on[:HEAD_DIM, :half_cols], on[HEAD_DIM:, half_cols:]], axis=1)
        for hp in range(SWA_Q_HEADS // 2):
            two = jnp.concatenate([u[:, (2 * hp) * WINDOW:(2 * hp + 1) * WINDOW],
                                   u[:, (2 * hp + 1) * WINDOW:(2 * hp + 2) * WINDOW]], axis=0)
            o_ref[0, sub * WINDOW:(sub + 1) * WINDOW, hp * LANES:(hp + 1) * LANES] = two.T.astype(BF16)


def _swa_call(sink_row, sqt, sk, svt, *, tq):
    B, _, S = sqt.shape
    nb = S // WINDOW
    return pl.pallas_call(
        functools.partial(_swa_kernel, tq=tq),
        grid=(B, S // tq),
        in_specs=[
            pl.BlockSpec(sink_row.shape, lambda b, i: (0, 0)),
            pl.BlockSpec((1, SWA_Q_COLS, tq), lambda b, i: (b, 0, i)),
            pl.BlockSpec((1, nb, WINDOW, SWA_KV_COLS), lambda b, i: (b, 0, 0, 0)),
            pl.BlockSpec((1, nb, SWA_VT_ROWS, WINDOW), lambda b, i: (b, 0, 0, 0)),
        ],
        out_specs=pl.BlockSpec((1, tq, SWA_Q_COLS), lambda b, i: (b, i, 0)),
        out_shape=jax.ShapeDtypeStruct((B, S, SWA_Q_COLS), BF16),
        compiler_params=pltpu.CompilerParams(
            dimension_semantics=("parallel", "arbitrary"), vmem_limit_bytes=VMEM_LIMIT_BYTES),
        name="swa_attn",
    )(sink_row, sqt, sk, svt)


def _pack_bf16_pairs(x):
    n = x.shape[1] // 2
    lo = lax.bitcast_convert_type(x[:, :n].astype(BF16).astype(F32), jnp.uint32)
    hi = lax.bitcast_convert_type(x[:, n:].astype(BF16).astype(F32), jnp.uint32)
    return (lo >> 16) | (hi & jnp.uint32(0xFFFF0000))


def _unpack_bf16_pairs(w):
    lo = lax.bitcast_convert_type(w << 16, F32)
    hi = lax.bitcast_convert_type(w & jnp.uint32(0xFFFF0000), F32)
    return jnp.concatenate([lo, hi], axis=1).astype(BF16)


def _mix_kernel(x_ref, od_ref, os_ref, wo_ref, g2_ref, wr_ref, br_ref, x1_ref, n2_ref, rt_ref, cnt_ref):
    tm = x_ref.shape[1]
    lane = lax.broadcasted_iota(jnp.int32, (tm, ROUTER_COLS), 1)
    lane_f = lane.astype(F32)
    big = float(ROUTER_COLS)
    mixed = jnp.concatenate([od_ref[0], os_ref[0]], axis=1)
    h = x_ref[0] + jnp.dot(mixed, wo_ref[...], preferred_element_type=F32)
    x1_ref[0] = h
    n2 = h * lax.rsqrt(jnp.mean(h * h, axis=-1, keepdims=True) + EPS) * g2_ref[...]
    n2_ref[0] = _pack_bf16_pairs(n2)
    n2_hi = n2.astype(BF16)
    n2_lo = (n2 - n2_hi.astype(F32)).astype(BF16)
    parts = jnp.dot(jnp.concatenate([n2_hi, n2_lo], axis=0), wr_ref[...],
                    preferred_element_type=F32)
    logits = ((parts[:tm, :ROUTER_COLS] + parts[tm:, ROUTER_COLS:])
              + (parts[:tm, ROUTER_COLS:] + parts[tm:, :ROUTER_COLS])) + br_ref[...]
    gl = jnp.where(lane < N_GROUPS, logits, -jnp.inf)
    gm = jnp.max(gl, axis=-1, keepdims=True)
    p_top = 1.0 / jnp.sum(jnp.exp(gl - gm), axis=-1, keepdims=True)
    g_idx = jnp.min(jnp.where(gl == gm, lane_f, big), axis=-1, keepdims=True)
    e_lo = N_GROUPS + EXPERTS_PER_GROUP * g_idx
    el = jnp.where((lane_f >= e_lo) & (lane_f < e_lo + EXPERTS_PER_GROUP), logits, -jnp.inf)
    v1 = jnp.max(el, axis=-1, keepdims=True)
    i1 = jnp.min(jnp.where(el == v1, lane_f, big), axis=-1, keepdims=True)
    el2 = jnp.where(lane_f == i1, -jnp.inf, el)
    v2 = jnp.max(el2, axis=-1, keepdims=True)
    i2 = jnp.min(jnp.where(el2 == v2, lane_f, big), axis=-1, keepdims=True)
    e21 = jnp.exp(v2 - v1)
    gate1 = p_top / (1.0 + e21)
    gate2 = p_top * e21 / (1.0 + e21)
    rt_ref[0] = jnp.where(lane == 0, i1 - N_GROUPS,
                jnp.where(lane == 1, i2 - N_GROUPS,
                jnp.where(lane == 2, gate1, jnp.where(lane == 3, gate2, 0.0))))
    chosen = ((lane_f == i1 - N_GROUPS) | (lane_f == i2 - N_GROUPS)).astype(F32)
    cnt_ref[0, 0] = jnp.broadcast_to(jnp.sum(chosen, axis=0, keepdims=True), cnt_ref.shape[2:])


def _mix_call(x, o_diff, o_swa, w_out, g2, w_router, b_router, *, tm):
    B, S, D = x.shape
    const = lambda b, i: (0, 0)
    row = lambda b, i: (b, i, 0)
    nt = S // tm
    return pl.pallas_call(
        _mix_kernel,
        grid=(B, nt),
        in_specs=[
            pl.BlockSpec((1, tm, D), row),
            pl.BlockSpec((1, tm, DIFF_V_COLS), row),
            pl.BlockSpec((1, tm, SWA_Q_COLS), row),
            pl.BlockSpec(w_out.shape, const),
            pl.BlockSpec((1, D), const),
            pl.BlockSpec(w_router.shape, const),
            pl.BlockSpec((1, ROUTER_COLS), const),
        ],
        out_specs=(pl.BlockSpec((1, tm, D), row), pl.BlockSpec((1, tm, D // 2), row),
                   pl.BlockSpec((1, tm, ROUTER_COLS), row),
                   pl.BlockSpec((1, 1, SUBLANES, ROUTER_COLS), lambda b, i: (b, i, 0, 0))),
        out_shape=(jax.ShapeDtypeStruct((B, S, D), F32), jax.ShapeDtypeStruct((B, S, D // 2), jnp.uint32),
                   jax.ShapeDtypeStruct((B, S, ROUTER_COLS), F32),
                   jax.ShapeDtypeStruct((B, nt, SUBLANES, ROUTER_COLS), F32)),
        compiler_params=pltpu.CompilerParams(
            dimension_semantics=("parallel", "parallel"), vmem_limit_bytes=VMEM_LIMIT_BYTES),
        name="outproj_router",
    )(x, o_diff, o_swa, w_out, g2, w_router, b_router)


def _slot_kernel(rt_ref, base_ref, dest_ref, *, tm):
    rt_t = rt_ref[...].T
    e1 = rt_t[0:1].astype(jnp.int32)
    e2 = rt_t[1:2].astype(jnp.int32)
    eid = lax.broadcasted_iota(jnp.int32, (N_EXPERTS, tm), 0)
    oh1 = eid == e1
    oh2 = eid == e2
    earlier = (lax.broadcasted_iota(jnp.int32, (tm, tm), 0)
               < lax.broadcasted_iota(jnp.int32, (tm, tm), 1)).astype(BF16)
    before = jnp.dot((oh1 | oh2).astype(BF16), earlier, preferred_element_type=F32)
    slot = before + base_ref[0][:, 0:1]
    d1 = jnp.sum(jnp.where(oh1, slot, 0.0), axis=0, keepdims=True).astype(jnp.int32)
    d2 = jnp.sum(jnp.where(oh2, slot, 0.0), axis=0, keepdims=True).astype(jnp.int32)
    dest_ref[0] = jnp.concatenate([d1, d2, jnp.zeros((SUBLANES - TOP_K, tm), jnp.int32)], axis=0)


def _slot_call(rt, tile_base, *, tm):
    nt = rt.shape[0] // tm
    return pl.pallas_call(
        functools.partial(_slot_kernel, tm=tm),
        grid=(nt,),
        in_specs=[
            pl.BlockSpec((tm, ROUTER_COLS), lambda t: (t, 0)),
            pl.BlockSpec((1, N_EXPERTS, LANES), lambda t: (t, 0, 0)),
        ],
        out_specs=pl.BlockSpec((1, SUBLANES, tm), lambda t: (t, 0, 0)),
        out_shape=jax.ShapeDtypeStruct((nt, SUBLANES, tm), jnp.int32),
        compiler_params=pltpu.CompilerParams(dimension_semantics=("parallel",)),
        name="moe_slots",
    )(rt, tile_base)


SC_ROW_CHUNK = 64


def _sc_workers():
    info = plsc.get_sparse_core_info()
    return info.num_cores, info.num_cores * info.num_subcores


def _sc_scatter_rows(rows, idx, n_out):
    n, width = rows.shape
    n_cores, n_workers = _sc_workers()
    n_chunks = n // SC_ROW_CHUNK
    per_worker = n_chunks // n_workers
    assert n_chunks % (2 * n_workers) == 0
    mesh = plsc.VectorSubcoreMesh(core_axis_name="c", subcore_axis_name="s")

    @functools.partial(
        pl.kernel, mesh=mesh,
        out_type=jax.ShapeDtypeStruct((n_out, width), rows.dtype),
        scratch_types=[pltpu.VMEM((SC_ROW_CHUNK,), jnp.int32)] * (2 * TOP_K)
        + [pltpu.VMEM((SC_ROW_CHUNK, width), rows.dtype)] * 2
        + [pltpu.SemaphoreType.DMA] * (2 + 2 * TOP_K),
    )
    def scatter(rows_hbm, idx_hbm, out_hbm, *scratch):
        idx_v = scratch[:2 * TOP_K]
        rows_v = scratch[2 * TOP_K:2 * TOP_K + 2]
        load_sems, store_sems = scratch[-(2 + 2 * TOP_K):-2 * TOP_K], scratch[-2 * TOP_K:]
        worker = lax.axis_index("s") * n_cores + lax.axis_index("c")

        @pl.loop(0, per_worker, step=2)
        def _(i):
            loads = []
            for half in range(2):
                c = worker * per_worker + i + half
                src = rows_hbm.at[pl.ds(pl.multiple_of(c * SC_ROW_CHUNK, SC_ROW_CHUNK), SC_ROW_CHUNK)]
                loads.append(pltpu.async_copy(src, rows_v[half], load_sems[half]))
                for k in range(TOP_K):
                    pltpu.sync_copy(idx_hbm.at[k, c], idx_v[half * TOP_K + k])
            stores = []
            for half in range(2):
                loads[half].wait()
                for k in range(TOP_K):
                    j = half * TOP_K + k
                    stores.append(pltpu.async_copy(rows_v[half], out_hbm.at[idx_v[j]], store_sems[j]))
            for store in stores:
                store.wait()

    return scatter(rows, idx)


def _expert_kernel(be_ref, nvalid_ref, next_ref, xs_ref, wg_hbm, wu_hbm, wd_hbm, y_ref,
                   wg_st, wu_st, wd_st, wg_b, wu_b, wd_b, slot_ref, sems):
    b = pl.program_id(0)
    n_valid = nvalid_ref[b]
    sources, staged, cast = (wg_hbm, wu_hbm, wd_hbm), (wg_st, wu_st, wd_st), (wg_b, wu_b, wd_b)

    def weight_copies(expert, slot):
        return [pltpu.make_async_copy(src.at[expert], dst.at[slot], sems.at[slot, i])
                for i, (src, dst) in enumerate(zip(sources, staged))]

    @pl.when(n_valid > 0)
    def _():
        @pl.when((b == 0) | (be_ref[b] != be_ref[jnp.maximum(b - 1, 0)]))
        def _():
            @pl.when(b == 0)
            def _():
                slot_ref[0] = 0
                for copy in weight_copies(be_ref[0], 0):
                    copy.start()

            slot = slot_ref[0]
            for copy in weight_copies(be_ref[b], slot):
                copy.wait()
            for dst, src in zip(cast, staged):
                dst[...] = src[slot].astype(BF16)

            @pl.when(next_ref[b] >= 0)
            def _():
                for copy in weight_copies(next_ref[b], 1 - slot):
                    copy.start()

            slot_ref[0] = 1 - slot

        for c in range(EXPERT_BLOCK // EXPERT_CHUNK):
            rows = pl.ds(c * EXPERT_CHUNK, EXPERT_CHUNK)
            row_id = c * EXPERT_CHUNK + lax.broadcasted_iota(jnp.int32, (EXPERT_CHUNK, xs_ref.shape[1]), 0)
            packed = jnp.where(row_id < n_valid, xs_ref[rows, :], jnp.uint32(0))
            xb = _unpack_bf16_pairs(packed)
            gate = jnp.dot(xb, wg_b[...], preferred_element_type=F32)
            up = jnp.dot(xb, wu_b[...], preferred_element_type=F32)
            hid = (gate * jax.nn.sigmoid(gate) * up).astype(BF16)
            y_ref[rows, :] = _pack_bf16_pairs(jnp.dot(hid, wd_b[...], preferred_element_type=F32))

    @pl.when(n_valid == 0)
    def _():
        y_ref[...] = jnp.zeros_like(y_ref)


def _expert_call(block_expert, n_valid, next_expert, xs, w_gate, w_up, w_down):
    P = xs.shape[0]
    NB = P // EXPERT_BLOCK
    E, D, F = w_gate.shape
    grid_spec = pltpu.PrefetchScalarGridSpec(
        num_scalar_prefetch=3,
        grid=(NB,),
        in_specs=[
            pl.BlockSpec((EXPERT_BLOCK,) + xs.shape[1:], lambda b, *_: (b, 0)),
            pl.BlockSpec(memory_space=pl.ANY),
            pl.BlockSpec(memory_space=pl.ANY),
            pl.BlockSpec(memory_space=pl.ANY),
        ],
        out_specs=pl.BlockSpec((EXPERT_BLOCK, D // 2), lambda b, *_: (b, 0)),
        scratch_shapes=[
            pltpu.VMEM((2, D, F), F32),
            pltpu.VMEM((2, D, F), F32),
            pltpu.VMEM((2, F, D), F32),
            pltpu.VMEM((D, F), BF16),
            pltpu.VMEM((D, F), BF16),
            pltpu.VMEM((F, D), BF16),
            pltpu.SMEM((1,), jnp.int32),
            pltpu.SemaphoreType.DMA((2, 3)),
        ],
    )
    return pl.pallas_call(
        _expert_kernel,
        grid_spec=grid_spec,
        out_shape=jax.ShapeDtypeStruct((P, D // 2), jnp.uint32),
        compiler_params=pltpu.CompilerParams(
            dimension_semantics=("arbitrary",), vmem_limit_bytes=VMEM_LIMIT_BYTES),
        name="moe_experts",
    )(block_expert, n_valid, next_expert, xs, w_gate, w_up, w_down)


def _sc_gather_rows(table, idx):
    n_rows, width = idx.shape[0], table.shape[1]
    n_cores, n_workers = _sc_workers()
    per_worker = n_rows // n_workers
    assert n_rows % (n_workers * 2 * SC_ROW_CHUNK) == 0
    mesh = plsc.VectorSubcoreMesh(core_axis_name="c", subcore_axis_name="s")

    @functools.partial(
        pl.kernel, mesh=mesh,
        out_type=jax.ShapeDtypeStruct((n_rows, width), table.dtype),
        scratch_types=[pltpu.VMEM((SC_ROW_CHUNK,), jnp.int32)] * 2
        + [pltpu.VMEM((SC_ROW_CHUNK, width), table.dtype)] * 2
        + [pltpu.SemaphoreType.DMA] * 4,
    )
    def gather(table_hbm, idx_hbm, out_hbm, idx_a, idx_b, rows_a, rows_b, sem_ga, sem_gb, sem_wa, sem_wb):
        worker = lax.axis_index("s") * n_cores + lax.axis_index("c")
        base = worker * per_worker

        @pl.loop(0, per_worker // SC_ROW_CHUNK, step=2)
        def _(c):
            off_a = pl.multiple_of(base + c * SC_ROW_CHUNK, SC_ROW_CHUNK)
            off_b = pl.multiple_of(off_a + SC_ROW_CHUNK, SC_ROW_CHUNK)
            pltpu.sync_copy(idx_hbm.at[pl.ds(off_a, SC_ROW_CHUNK)], idx_a)
            pltpu.sync_copy(idx_hbm.at[pl.ds(off_b, SC_ROW_CHUNK)], idx_b)
            gather_a = pltpu.async_copy(table_hbm.at[idx_a], rows_a, sem_ga)
            gather_b = pltpu.async_copy(table_hbm.at[idx_b], rows_b, sem_gb)
            gather_a.wait()
            write_a = pltpu.async_copy(rows_a, out_hbm.at[pl.ds(off_a, SC_ROW_CHUNK)], sem_wa)
            gather_b.wait()
            write_b = pltpu.async_copy(rows_b, out_hbm.at[pl.ds(off_b, SC_ROW_CHUNK)], sem_wb)
            write_a.wait()
            write_b.wait()

    return gather(table, idx)


def _combine_kernel(x1_ref, rt_ref, y_ref, fg_ref, o_ref, *, final_norm):
    rt = rt_ref[...]
    y1 = _unpack_bf16_pairs(y_ref[0, 0]).astype(F32)
    y2 = _unpack_bf16_pairs(y_ref[0, 1]).astype(F32)
    h = x1_ref[...] + rt[:, 2:3] * y1 + rt[:, 3:4] * y2
    if final_norm:
        h = h * lax.rsqrt(jnp.mean(h * h, axis=-1, keepdims=True) + EPS) * fg_ref[...]
    o_ref[...] = h


def _combine_call(x1, rt, ysg, final_g, *, tm, final_norm):
    T, D = x1.shape
    return pl.pallas_call(
        functools.partial(_combine_kernel, final_norm=final_norm),
        grid=(T // tm,),
        in_specs=[
            pl.BlockSpec((tm, D), lambda t: (t, 0)),
            pl.BlockSpec((tm, ROUTER_COLS), lambda t: (t, 0)),
            pl.BlockSpec((1, TOP_K, tm, D // 2), lambda t: (t, 0, 0, 0)),
            pl.BlockSpec((1, D), lambda t: (0, 0)),
        ],
        out_specs=pl.BlockSpec((tm, D), lambda t: (t, 0)),
        out_shape=jax.ShapeDtypeStruct((T, D), F32),
        compiler_params=pltpu.CompilerParams(
            dimension_semantics=("parallel",), vmem_limit_bytes=VMEM_LIMIT_BYTES),
        name="moe_combine",
    )(x1, rt, ysg, final_g)


def _slot_layout(tile_counts, n_assign):
    NB = -(-n_assign // EXPERT_BLOCK) + N_EXPERTS
    n_tiles = tile_counts.shape[0]
    tc = tile_counts.astype(F32)
    hp = lax.Precision.HIGHEST
    counts = jnp.sum(tc, axis=0)
    padded = jnp.ceil(counts / EXPERT_BLOCK) * EXPERT_BLOCK
    upper = (jnp.arange(N_EXPERTS)[:, None] < jnp.arange(N_EXPERTS)[None, :]).astype(F32)
    pad_start = jnp.dot(padded, upper, precision=hp)
    pad_end = pad_start + padded
    lower = (jnp.arange(n_tiles)[:, None] > jnp.arange(n_tiles)[None, :]).astype(F32)
    tile_base = pad_start[None, :] + jnp.dot(lower, tc, precision=hp)
    block_start = jnp.arange(NB, dtype=F32) * EXPERT_BLOCK
    block_expert = jnp.minimum(jnp.sum((pad_end[None, :] <= block_start[:, None]).astype(jnp.int32), axis=1),
                               N_EXPERTS - 1)
    mine = block_expert[:, None] == jnp.arange(N_EXPERTS)[None, :]
    run_end = jnp.sum(jnp.where(mine, (pad_start + counts)[None, :], 0.0), axis=1)
    n_valid = jnp.clip(run_end - block_start, 0, EXPERT_BLOCK).astype(jnp.int32)
    eid = jnp.arange(N_EXPERTS)
    later_nonempty = (eid[None, :] > eid[:, None]) & (counts[None, :] > 0)
    next_nonempty = jnp.min(jnp.where(later_nonempty, eid[None, :], N_EXPERTS), axis=1)
    next_nonempty = jnp.where(next_nonempty < N_EXPERTS, next_nonempty, -1)
    next_expert = jnp.sum(jnp.where(mine, next_nonempty[None, :], 0), axis=1).astype(jnp.int32)
    return NB, block_expert.astype(jnp.int32), n_valid, next_expert, tile_base


def _rope_tables(S):
    half = HEAD_DIM // 2
    inv = 1.0 / (ROPE_THETA ** (jnp.arange(0, HEAD_DIM, 2, dtype=F32) / HEAD_DIM))
    pos = jnp.arange(S, dtype=F32)
    ang_l = pos[:, None] * jnp.tile(inv, LANES // half)[None, :]
    sign = jnp.tile(jnp.concatenate([-jnp.ones((half,), F32), jnp.ones((half,), F32)]), LANES // HEAD_DIM)
    ang_t = inv[:, None] * pos[None, :]
    return jnp.cos(ang_l), jnp.sin(ang_l) * sign[None, :], jnp.cos(ang_t), jnp.sin(ang_t)


def _tiles(S):
    tile = min(512, S)
    return tile, tile, min(1024, S), min(1024, S), min(1024, S)


def kernel(x, norm1_g, w_in, lambda_q1, lambda_k1, lambda_q2, lambda_k2, subln_g, sinks, w_out,
           norm2_g, w_router_group, b_router_group, w_router_expert, b_router_expert,
           w_gate, w_up, w_down, final_g):
    B, S, D = x.shape
    T = B * S
    depth = w_in.shape[0]
    tq, tk, tm_proj, tm_tok, tq_swa = _tiles(S)
    qscale = HEAD_DIM ** -0.5 * math.log2(math.e)
    cos_l, sin_l, cos_t, sin_t = _rope_tables(S)

    c0 = DIFF_QK_COLS
    c1 = 2 * DIFF_QK_COLS
    c2 = c1 + DIFF_V_COLS
    c3 = c2 + SWA_Q_COLS
    c4 = c3 + SWA_KV_COLS
    for l in range(depth):
        lambda_init = 0.8 - 0.6 * math.exp(-0.3 * l)
        w = w_in[l]
        w_nat = jnp.concatenate([w[:, c0:c1], w[:, c3:c4]], axis=1).astype(BF16)
        w_tr = jnp.concatenate([w[:, :c0] * qscale, w[:, c1:c2], w[:, c2:c3] * qscale, w[:, c4:]],
                               axis=1).T.astype(BF16)
        dqt, dk, dvt, sqt, sk, svt = _proj_call(
            x, norm1_g[l][None, :], w_nat, w_tr, cos_l, sin_l, cos_t, sin_t, tm=tm_proj, tk=tk)

        lam_p = jnp.stack([lambda_q1[l], lambda_k1[l], lambda_q2[l], lambda_k2[l]]).astype(F32)
        o_diff = _diff_call(lam_p, dqt, dk, dvt, subln_g[l][None, :].astype(F32),
                            tq=tq, tk=tk, lambda_init=lambda_init)
        sink_row = jnp.repeat(sinks[l].astype(F32) * math.log2(math.e), WINDOW)[None, :]
        o_swa = _swa_call(sink_row, sqt, sk, svt, tq=tq_swa)

        wo_b = w_out[l].astype(BF16)
        w_router = jnp.zeros((D, ROUTER_COLS), F32)
        w_router = w_router.at[:, :N_GROUPS].set(w_router_group[l])
        w_router = w_router.at[:, N_GROUPS:N_GROUPS + N_EXPERTS].set(w_router_expert[l])
        w_router_hi = w_router.astype(BF16)
        w_router_lo = (w_router - w_router_hi.astype(F32)).astype(BF16)
        w_router = jnp.concatenate([w_router_hi, w_router_lo], axis=1)
        b_router = jnp.zeros((1, ROUTER_COLS), F32)
        b_router = b_router.at[0, :N_GROUPS].set(b_router_group[l])
        b_router = b_router.at[0, N_GROUPS:N_GROUPS + N_EXPERTS].set(b_router_expert[l])
        x1, n2p, rt, cnt = _mix_call(x, o_diff, o_swa, wo_b, norm2_g[l][None, :], w_router, b_router, tm=tm_tok)

        rt2 = rt.reshape(T, ROUTER_COLS)
        tile_counts = cnt[:, :, 0, :N_EXPERTS].reshape(T // tm_tok, N_EXPERTS).astype(jnp.int32)
        NB, block_expert, n_valid, next_expert, tile_base = _slot_layout(tile_counts, T * TOP_K)
        tile_base = jnp.broadcast_to(tile_base.astype(F32)[:, :, None], (T // tm_tok, N_EXPERTS, LANES))
        dest = _slot_call(rt2, tile_base, tm=tm_tok)
        scatter_idx = jnp.swapaxes(dest[:, :TOP_K, :], 0, 1).reshape(TOP_K, T // SC_ROW_CHUNK, SC_ROW_CHUNK)
        xs = _sc_scatter_rows(n2p.reshape(T, D // 2), scatter_idx, NB * EXPERT_BLOCK)
        ys = _expert_call(block_expert, n_valid, next_expert, xs, w_gate[l], w_up[l], w_down[l])
        ysg = _sc_gather_rows(ys, dest[:, :TOP_K, :].reshape(T * TOP_K))
        x = _combine_call(x1.reshape(T, D), rt2, ysg.reshape(T // tm_tok, TOP_K, tm_tok, D // 2),
                          final_g[None, :], tm=tm_tok, final_norm=(l == depth - 1)).reshape(B, S, D)
    return x
```

```python
import functools
import math

import jax
import jax.numpy as jnp
from jax import lax
from jax.experimental import pallas as pl
from jax.experimental.pallas import tpu as pltpu
from jax.experimental.pallas import tpu_sc as plsc

HEAD_DIM = 64
DIFF_HEADS = 4
DIFF_V_DIM = 2 * HEAD_DIM
SWA_Q_HEADS = 8
SWA_KV_HEADS = 2
SWA_GROUP = SWA_Q_HEADS // SWA_KV_HEADS
WINDOW = 128
ROPE_THETA = 10000.0
N_GROUPS = 4
EXPERTS_PER_GROUP = 8
N_EXPERTS = N_GROUPS * EXPERTS_PER_GROUP
TOP_K = 2
EXPERT_BLOCK = 512
EXPERT_CHUNK = 256
EPS = 1e-6
NEG = -1e30

DIFF_QK_COLS = DIFF_HEADS * 2 * HEAD_DIM
DIFF_V_COLS = DIFF_HEADS * DIFF_V_DIM
SWA_Q_COLS = SWA_Q_HEADS * HEAD_DIM
SWA_KV_COLS = SWA_KV_HEADS * HEAD_DIM
LANES = 128
SUBLANES = 8
BF16_SUBLANES = 16
VMEM_LIMIT_BYTES = 48 * 1024 * 1024
VT_ROWS = DIFF_V_DIM + BF16_SUBLANES
SWA_VT_ROWS = SWA_KV_COLS + BF16_SUBLANES
ROUTER_COLS = LANES
DIFF_UNROLL = 4
DIFF_S_BUFS = 4
DIFF_Q_TILES = 2

BF16 = jnp.bfloat16
F32 = jnp.float32


def _rope_lanes(x, cos_l, sin_l, first_half):
    rot = jnp.where(first_half, pltpu.roll(x, 96, 1), pltpu.roll(x, 32, 1))
    return x * cos_l + rot * sin_l


def _proj_kernel(x_ref, g_ref, wnat_ref, wtr_ref, cost_ref, sint_ref,
                 dqt_ref, dk_ref, dvt_ref, sqt_ref, sk_ref, svt_ref, *, tk):
    x = x_ref[0]
    tm = x.shape[0]
    n1 = x * lax.rsqrt(jnp.mean(x * x, axis=-1, keepdims=True) + EPS) * g_ref[...]
    n1b = n1.astype(BF16)
    nat = jnp.dot(n1b, wnat_ref[...], preferred_element_type=F32)
    tr = lax.dot_general(wtr_ref[...], n1b, (((1,), (1,)), ((), ())),
                         preferred_element_type=F32)

    c32, s32 = cost_ref[...].T, sint_ref[...].T
    cos_l = jnp.concatenate([c32, c32, c32, c32], axis=1)
    sin_l = jnp.concatenate([-s32, s32, -s32, s32], axis=1)
    first_half = (lax.broadcasted_iota(jnp.int32, (tm, LANES), 1) & (HEAD_DIM - 1)) < HEAD_DIM // 2
    for h in range(DIFF_HEADS):
        slab = nat[:, h * LANES:(h + 1) * LANES]
        dk_ref[0, h] = _rope_lanes(slab, cos_l, sin_l, first_half).astype(BF16)
    sk = _rope_lanes(nat[:, DIFF_QK_COLS:DIFF_QK_COLS + LANES], cos_l, sin_l, first_half).astype(BF16)
    for c in range(tm // WINDOW):
        sk_ref[0, c] = sk[c * WINDOW:(c + 1) * WINDOW]

    cos_t, sin_t = cost_ref[...], sint_ref[...]
    half = HEAD_DIM // 2

    def rope_rows(r0):
        x1 = tr[r0:r0 + half]
        x2 = tr[r0 + half:r0 + HEAD_DIM]
        return (x1 * cos_t - x2 * sin_t).astype(BF16), (x1 * sin_t + x2 * cos_t).astype(BF16)

    for h in range(DIFF_HEADS):
        for c in range(2):
            lo, hi = rope_rows(h * 2 * HEAD_DIM + c * HEAD_DIM)
            dqt_ref[0, h, c * HEAD_DIM:c * HEAD_DIM + half] = lo
            dqt_ref[0, h, c * HEAD_DIM + half:(c + 1) * HEAD_DIM] = hi
    ones_rows = (lax.broadcasted_iota(jnp.int32, (BF16_SUBLANES, tk), 0) == 0).astype(BF16)
    for h in range(DIFF_HEADS):
        r0 = DIFF_QK_COLS + h * DIFF_V_DIM
        for c in range(tm // tk):
            dvt_ref[0, h, c, :DIFF_V_DIM] = tr[r0:r0 + DIFF_V_DIM, c * tk:(c + 1) * tk].astype(BF16)
            dvt_ref[0, h, c, DIFF_V_DIM:] = ones_rows

    r0 = DIFF_QK_COLS + DIFF_V_COLS
    for h in range(SWA_Q_HEADS):
        lo, hi = rope_rows(r0 + h * HEAD_DIM)
        sqt_ref[0, h * HEAD_DIM:h * HEAD_DIM + half] = lo
        sqt_ref[0, h * HEAD_DIM + half:(h + 1) * HEAD_DIM] = hi
    r0 += SWA_Q_COLS
    for c in range(tm // WINDOW):
        svt_ref[0, c, :SWA_KV_COLS] = tr[r0:r0 + SWA_KV_COLS, c * WINDOW:(c + 1) * WINDOW].astype(BF16)
        svt_ref[0, c, SWA_KV_COLS:] = ones_rows[:, :WINDOW]


def _proj_call(x, g1, w_nat, w_tr, cos_t, sin_t, *, tm, tk):
    B, S, D = x.shape
    nkv = S // tk
    grid = (B, S // tm)
    const = lambda b, i: (0, 0)
    out_shape = (
        jax.ShapeDtypeStruct((B, DIFF_HEADS, 2 * HEAD_DIM, S), BF16),
        jax.ShapeDtypeStruct((B, DIFF_HEADS, S, 2 * HEAD_DIM), BF16),
        jax.ShapeDtypeStruct((B, DIFF_HEADS, nkv, VT_ROWS, tk), BF16),
        jax.ShapeDtypeStruct((B, SWA_Q_COLS, S), BF16),
        jax.ShapeDtypeStruct((B, S // WINDOW, WINDOW, SWA_KV_COLS), BF16),
        jax.ShapeDtypeStruct((B, S // WINDOW, SWA_VT_ROWS, WINDOW), BF16),
    )
    return pl.pallas_call(
        functools.partial(_proj_kernel, tk=tk),
        grid=grid,
        in_specs=[
            pl.BlockSpec((1, tm, D), lambda b, i: (b, i, 0)),
            pl.BlockSpec((1, D), const),
            pl.BlockSpec(w_nat.shape, const),
            pl.BlockSpec(w_tr.shape, const),
            pl.BlockSpec((HEAD_DIM // 2, tm), lambda b, i: (0, i)),
            pl.BlockSpec((HEAD_DIM // 2, tm), lambda b, i: (0, i)),
        ],
        out_specs=(
            pl.BlockSpec((1, DIFF_HEADS, 2 * HEAD_DIM, tm), lambda b, i: (b, 0, 0, i)),
            pl.BlockSpec((1, DIFF_HEADS, tm, 2 * HEAD_DIM), lambda b, i: (b, 0, i, 0)),
            pl.BlockSpec((1, DIFF_HEADS, tm // tk, VT_ROWS, tk), lambda b, i: (b, 0, i, 0, 0)),
            pl.BlockSpec((1, SWA_Q_COLS, tm), lambda b, i: (b, 0, i)),
            pl.BlockSpec((1, tm // WINDOW, WINDOW, SWA_KV_COLS), lambda b, i: (b, i, 0, 0)),
            pl.BlockSpec((1, tm // WINDOW, SWA_VT_ROWS, WINDOW), lambda b, i: (b, i, 0, 0)),
        ),
        out_shape=out_shape,
        compiler_params=pltpu.CompilerParams(
            dimension_semantics=("parallel", "parallel"), vmem_limit_bytes=VMEM_LIMIT_BYTES),
        name="proj_rope",
    )(x, g1, w_nat, w_tr, cos_t, sin_t)


def _diff_kernel(lam_ref, qt_ref, k_ref, vt_ref, g_ref, o_ref, *scratch, tq, tk, lambda_init):
    step = pl.program_id(2)
    s_bufs = scratch[:DIFF_S_BUFS]
    top_bufs = scratch[DIFF_S_BUFS:2 * DIFF_S_BUFS]
    state = scratch[2 * DIFF_S_BUFS:2 * DIFF_S_BUFS + 2 * DIFF_Q_TILES]
    bias_ref = scratch[-1]

    @pl.when(step == 0)
    def _():
        r = lax.broadcasted_iota(jnp.int32, (tk, 2 * tq), 0)
        c = lax.broadcasted_iota(jnp.int32, (tk, 2 * tq), 1) & (tq - 1)
        bias_ref[...] = jnp.where(r <= c, 0.0, NEG).astype(F32)

    lam_p = lam_ref[...]
    lam = (jnp.exp(jnp.sum(lam_p[0:1] * lam_p[1:2], axis=-1, keepdims=True))
           - jnp.exp(jnp.sum(lam_p[2:3] * lam_p[3:4], axis=-1, keepdims=True)) + lambda_init)

    for sub in range(DIFF_Q_TILES):
        _diff_query_tile(step * DIFF_Q_TILES + sub, qt_ref[0, 0, :, sub * tq:(sub + 1) * tq], k_ref, vt_ref,
                         g_ref, o_ref.at[0, pl.ds(sub * tq, tq), :], s_bufs, top_bufs,
                         state[2 * sub], state[2 * sub + 1], bias_ref, lam,
                         tq=tq, tk=tk, lambda_init=lambda_init)


def _diff_query_tile(i, qt, k_ref, vt_ref, g_ref, o_ref, s_bufs, top_bufs, m_ref, acc_ref, bias_ref, lam,
                     *, tq, tk, lambda_init):
    z = jnp.zeros((HEAD_DIM, tq), BF16)
    qw = jnp.concatenate([jnp.concatenate([qt[:HEAD_DIM], z], axis=1),
                          jnp.concatenate([z, qt[HEAD_DIM:]], axis=1)], axis=0)

    def scores(j, par):
        kt = k_ref[0, 0, pl.ds(pl.multiple_of(j * tk, tk), tk), :]
        s = jnp.dot(kt, qw, preferred_element_type=F32)
        s_bufs[par][...] = s
        top_bufs[par][...] = jnp.max(s, axis=0, keepdims=True)

    def absorb(j, par, masked):
        s = s_bufs[par][...]
        if masked:
            s = s + bias_ref[...]
            top = jnp.max(s, axis=0, keepdims=True)
        else:
            top = top_bufs[par][...]
        m = m_ref[...]
        m_new = jnp.maximum(m, top)
        alpha = jnp.exp2(m - m_new)
        p = jnp.exp2(s - m_new).astype(BF16)
        m_ref[...] = m_new
        pv = jnp.dot(vt_ref[0, 0, j], p, preferred_element_type=F32)
        acc_ref[...] = alpha * acc_ref[...] + pv

    m_ref[...] = jnp.full(m_ref.shape, NEG, F32)
    acc_ref[...] = jnp.zeros(acc_ref.shape, F32)

    nfull = (i * tq) // tk
    scores(nfull, 0)
    scores(0, 1)
    absorb(nfull, 0, True)

    def group(t, c):
        j = DIFF_UNROLL * t
        for idx in range(DIFF_UNROLL):
            scores(j + idx + 1, (idx + 2) % DIFF_S_BUFS)
            absorb(j + idx, (idx + 1) % DIFF_S_BUFS, False)
        return c

    lax.fori_loop(0, nfull // DIFF_UNROLL, group, 0)

    for rem in range(1, DIFF_UNROLL):
        @pl.when(nfull % DIFF_UNROLL == rem)
        def _():
            first = nfull - rem
            for idx in range(rem):
                if idx + 1 < rem:
                    scores(first + idx + 1, (idx + 2) % DIFF_S_BUFS)
                absorb(first + idx, (idx + 1) % DIFF_S_BUFS, False)

    inv_l = 1.0 / acc_ref[DIFF_V_DIM:DIFF_V_DIM + 1, :]
    o = (acc_ref[:DIFF_V_DIM, :tq] * inv_l[:, :tq]
         - lam * (acc_ref[:DIFF_V_DIM, tq:] * inv_l[:, tq:]))
    o = o * lax.rsqrt(jnp.mean(o * o, axis=0, keepdims=True) + EPS)
    o_ref[...] = (o.T * g_ref[...] * (1.0 - lambda_init)).astype(BF16)


def _diff_call(lam_p, dqt, dk, dvt, subln_g, *, tq, tk, lambda_init):
    B, H, _, S = dqt.shape
    assert tk == tq and S % tk == 0, "the diagonal tile's causal pattern is built for square tiles"
    nkv = S // tk
    tq_step = DIFF_Q_TILES * tq
    assert S % tq_step == 0
    grid = (B, H, S // tq_step)
    return pl.pallas_call(
        functools.partial(_diff_kernel, tq=tq, tk=tk, lambda_init=lambda_init),
        grid=grid,
        in_specs=[
            pl.BlockSpec(lam_p.shape, lambda b, h, i: (0, 0)),
            pl.BlockSpec((1, 1, 2 * HEAD_DIM, tq_step), lambda b, h, i: (b, h, 0, i)),
            pl.BlockSpec((1, 1, S, 2 * HEAD_DIM), lambda b, h, i: (b, h, 0, 0)),
            pl.BlockSpec((1, 1, nkv, VT_ROWS, tk), lambda b, h, i: (b, h, 0, 0, 0)),
            pl.BlockSpec((1, DIFF_V_DIM), lambda b, h, i: (0, 0)),
        ],
        out_specs=pl.BlockSpec((1, tq_step, DIFF_V_DIM), lambda b, h, i: (b, i, h)),
        out_shape=jax.ShapeDtypeStruct((B, S, DIFF_V_COLS), BF16),
        scratch_shapes=[pltpu.VMEM((tk, 2 * tq), F32)] * DIFF_S_BUFS + [
            pltpu.VMEM((1, 2 * tq), F32)] * DIFF_S_BUFS + [
            pltpu.VMEM((1, 2 * tq), F32),
            pltpu.VMEM((VT_ROWS, 2 * tq), F32)] * DIFF_Q_TILES + [
            pltpu.VMEM((tk, 2 * tq), F32),
        ],
        compiler_params=pltpu.CompilerParams(
            dimension_semantics=("parallel", "parallel", "arbitrary"),
            vmem_limit_bytes=VMEM_LIMIT_BYTES),
        name="diff_attn",
    )(lam_p, dqt, dk, dvt, subln_g)


def _swa_kernel(sink_ref, qt_ref, k_ref, vt_ref, o_ref, *, tq):
    i = pl.program_id(1)
    n_cols = SWA_Q_HEADS * WINDOW
    half_cols = n_cols // SWA_KV_HEADS
    sink = sink_ref[...]
    row = lax.broadcasted_iota(jnp.int32, (2 * WINDOW, WINDOW), 0)
    qrel = lax.broadcasted_iota(jnp.int32, (2 * WINDOW, WINDOW), 1)
    band = (row - WINDOW <= qrel) & (row > qrel)
    in_current = row >= WINDOW
    z = jnp.zeros((HEAD_DIM, half_cols), BF16)
    for sub in range(tq // WINDOW):
        n = i * (tq // WINDOW) + sub
        prev = jnp.maximum(n - 1, 0)
        kwin = jnp.concatenate([k_ref[0, prev], k_ref[0, n]], axis=0)
        vtwin = jnp.concatenate([vt_ref[0, prev], vt_ref[0, n]], axis=1)
        qt = qt_ref[0, :, sub * WINDOW:(sub + 1) * WINDOW]
        heads = [qt[h * HEAD_DIM:(h + 1) * HEAD_DIM] for h in range(SWA_Q_HEADS)]
        qw = jnp.concatenate(
            [jnp.concatenate(heads[:SWA_GROUP] + [z], axis=1),
             jnp.concatenate([z] + heads[SWA_GROUP:], axis=1)], axis=0)
        s = jnp.dot(kwin, qw, preferred_element_type=F32)
        valid = band & (in_current | (n >= 1))
        s = jnp.concatenate(
            [jnp.where(valid, s[:, h * WINDOW:(h + 1) * WINDOW], NEG) for h in range(SWA_Q_HEADS)], axis=1)
        m = jnp.maximum(jnp.max(s, axis=0, keepdims=True), sink)
        p = jnp.exp2(s - m).astype(BF16)
        acc = jnp.dot(vtwin, p, preferred_element_type=F32)
        den = acc[SWA_KV_COLS:SWA_KV_COLS + 1] + jnp.exp2(sink - m)
        on = acc[:SWA_KV_COLS] / den
        u = jnp.concatenate([on[:HEAD_DIM, :half_cols], on[HEAD_DIM:, half_cols:]], axis=1)
        for hp in range(SWA_Q_HEADS // 2):
            two = jnp.concatenate([u[:, (2 * hp) * WINDOW:(2 * hp + 1) * WINDOW],
                                   u[:, (2 * hp + 1) * WINDOW:(2 * hp + 2) * WINDOW]], axis=0)
            o_ref[0, sub * WINDOW:(sub + 1) * WINDOW, hp * LANES:(hp + 1) * LANES] = two.T.astype(BF16)


def _swa_call(sink_row, sqt, sk, svt, *, tq):
    B, _, S = sqt.shape
    nb = S // WINDOW
    return pl.pallas_call(
        functools.partial(_swa_kernel, tq=tq),
        grid=(B, S // tq),
        in_specs=[
            pl.BlockSpec(sink_row.shape, lambda b, i: (0, 0)),
            pl.BlockSpec((1, SWA_Q_COLS, tq), lambda b, i: (b, 0, i)),
            pl.BlockSpec((1, nb, WINDOW, SWA_KV_COLS), lambda b, i: (b, 0, 0, 0)),
            pl.BlockSpec((1, nb, SWA_VT_ROWS, WINDOW), lambda b, i: (b, 0, 0, 0)),
        ],
        out_specs=pl.BlockSpec((1, tq, SWA_Q_COLS), lambda b, i: (b, i, 0)),
        out_shape=jax.ShapeDtypeStruct((B, S, SWA_Q_COLS), BF16),
        compiler_params=pltpu.CompilerParams(
            dimension_semantics=("parallel", "arbitrary"), vmem_limit_bytes=VMEM_LIMIT_BYTES),
        name="swa_attn",
    )(sink_row, sqt, sk, svt)


def _pack_bf16_pairs(x):
    n = x.shape[1] // 2
    lo = lax.bitcast_convert_type(x[:, :n].astype(BF16).astype(F32), jnp.uint32)
    hi = lax.bitcast_convert_type(x[:, n:].astype(BF16).astype(F32), jnp.uint32)
    return (lo >> 16) | (hi & jnp.uint32(0xFFFF0000))


def _unpack_bf16_pairs(w):
    lo = lax.bitcast_convert_type(w << 16, F32)
    hi = lax.bitcast_convert_type(w & jnp.uint32(0xFFFF0000), F32)
    return jnp.concatenate([lo, hi], axis=1).astype(BF16)


def _mix_kernel(x_ref, od_ref, os_ref, wo_ref, g2_ref, wr_ref, br_ref, x1_ref, n2_ref, rt_ref, cnt_ref):
    tm = x_ref.shape[1]
    lane = lax.broadcasted_iota(jnp.int32, (tm, ROUTER_COLS), 1)
    lane_f = lane.astype(F32)
    big = float(ROUTER_COLS)
    mixed = jnp.concatenate([od_ref[0], os_ref[0]], axis=1)
    h = x_ref[0] + jnp.dot(mixed, wo_ref[...], preferred_element_type=F32)
    x1_ref[0] = h
    n2 = h * lax.rsqrt(jnp.mean(h * h, axis=-1, keepdims=True) + EPS) * g2_ref[...]
    n2_ref[0] = _pack_bf16_pairs(n2)
    n2_hi = n2.astype(BF16)
    n2_lo = (n2 - n2_hi.astype(F32)).astype(BF16)
    parts = jnp.dot(jnp.concatenate([n2_hi, n2_lo], axis=0), wr_ref[...],
                    preferred_element_type=F32)
    logits = ((parts[:tm, :ROUTER_COLS] + parts[tm:, ROUTER_COLS:])
              + (parts[:tm, ROUTER_COLS:] + parts[tm:, :ROUTER_COLS])) + br_ref[...]
    gl = jnp.where(lane < N_GROUPS, logits, -jnp.inf)
    gm = jnp.max(gl, axis=-1, keepdims=True)
    p_top = 1.0 / jnp.sum(jnp.exp(gl - gm), axis=-1, keepdims=True)
    g_idx = jnp.min(jnp.where(gl == gm, lane_f, big), axis=-1, keepdims=True)
    e_lo = N_GROUPS + EXPERTS_PER_GROUP * g_idx
    el = jnp.where((lane_f >= e_lo) & (lane_f < e_lo + EXPERTS_PER_GROUP), logits, -jnp.inf)
    v1 = jnp.max(el, axis=-1, keepdims=True)
    i1 = jnp.min(jnp.where(el == v1, lane_f, big), axis=-1, keepdims=True)
    el2 = jnp.where(lane_f == i1, -jnp.inf, el)
    v2 = jnp.max(el2, axis=-1, keepdims=True)
    i2 = jnp.min(jnp.where(el2 == v2, lane_f, big), axis=-1, keepdims=True)
    e21 = jnp.exp(v2 - v1)
    gate1 = p_top / (1.0 + e21)
    gate2 = p_top * e21 / (1.0 + e21)
    rt_ref[0] = jnp.where(lane == 0, i1 - N_GROUPS,
                jnp.where(lane == 1, i2 - N_GROUPS,
                jnp.where(lane == 2, gate1, jnp.where(lane == 3, gate2, 0.0))))
    chosen = ((lane_f == i1 - N_GROUPS) | (lane_f == i2 - N_GROUPS)).astype(F32)
    cnt_ref[0, 0] = jnp.broadcast_to(jnp.sum(chosen, axis=0, keepdims=True), cnt_ref.shape[2:])


def _mix_call(x, o_diff, o_swa, w_out, g2, w_router, b_router, *, tm):
    B, S, D = x.shape
    const = lambda b, i: (0, 0)
    row = lambda b, i: (b, i, 0)
    nt = S // tm
    return pl.pallas_call(
        _mix_kernel,
        grid=(B, nt),
        in_specs=[
            pl.BlockSpec((1, tm, D), row),
            pl.BlockSpec((1, tm, DIFF_V_COLS), row),
            pl.BlockSpec((1, tm, SWA_Q_COLS), row),
            pl.BlockSpec(w_out.shape, const),
            pl.BlockSpec((1, D), const),
            pl.BlockSpec(w_router.shape, const),
            pl.BlockSpec((1, ROUTER_COLS), const),
        ],
        out_specs=(pl.BlockSpec((1, tm, D), row), pl.BlockSpec((1, tm, D // 2), row),
                   pl.BlockSpec((1, tm, ROUTER_COLS), row),
                   pl.BlockSpec((1, 1, SUBLANES, ROUTER_COLS), lambda b, i: (b, i, 0, 0))),
        out_shape=(jax.ShapeDtypeStruct((B, S, D), F32), jax.ShapeDtypeStruct((B, S, D // 2), jnp.uint32),
                   jax.ShapeDtypeStruct((B, S, ROUTER_COLS), F32),
                   jax.ShapeDtypeStruct((B, nt, SUBLANES, ROUTER_COLS), F32)),
        compiler_params=pltpu.CompilerParams(
            dimension_semantics=("parallel", "parallel"), vmem_limit_bytes=VMEM_LIMIT_BYTES),
        name="outproj_router",
    )(x, o_diff, o_swa, w_out, g2, w_router, b_router)


def _slot_kernel(rt_ref, base_ref, dest_ref, *, tm):
    rt_t = rt_ref[...].T
    e1 = rt_t[0:1].astype(jnp.int32)
    e2 = rt_t[1:2].astype(jnp.int32)
    eid = lax.broadcasted_iota(jnp.int32, (N_EXPERTS, tm), 0)
    oh1 = eid == e1
    oh2 = eid == e2
    earlier = (lax.broadcasted_iota(jnp.int32, (tm, tm), 0)
               < lax.broadcasted_iota(jnp.int32, (tm, tm), 1)).astype(BF16)
    before = jnp.dot((oh1 | oh2).astype(BF16), earlier, preferred_element_type=F32)
    slot = before + base_ref[0][:, 0:1]
    d1 = jnp.sum(jnp.where(oh1, slot, 0.0), axis=0, keepdims=True).astype(jnp.int32)
    d2 = jnp.sum(jnp.where(oh2, slot, 0.0), axis=0, keepdims=True).astype(jnp.int32)
    dest_ref[0] = jnp.concatenate([d1, d2, jnp.zeros((SUBLANES - TOP_K, tm), jnp.int32)], axis=0)


def _slot_call(rt, tile_base, *, tm):
    nt = rt.shape[0] // tm
    return pl.pallas_call(
        functools.partial(_slot_kernel, tm=tm),
        grid=(nt,),
        in_specs=[
            pl.BlockSpec((tm, ROUTER_COLS), lambda t: (t, 0)),
            pl.BlockSpec((1, N_EXPERTS, LANES), lambda t: (t, 0, 0)),
        ],
        out_specs=pl.BlockSpec((1, SUBLANES, tm), lambda t: (t, 0, 0)),
        out_shape=jax.ShapeDtypeStruct((nt, SUBLANES, tm), jnp.int32),
        compiler_params=pltpu.CompilerParams(dimension_semantics=("parallel",)),
        name="moe_slots",
    )(rt, tile_base)


SC_ROW_CHUNK = 64


def _sc_workers():
    info = plsc.get_sparse_core_info()
    return info.num_cores, info.num_cores * info.num_subcores


def _sc_scatter_rows(rows, idx, n_out):
    n, width = rows.shape
    n_cores, n_workers = _sc_workers()
    n_chunks = n // SC_ROW_CHUNK
    per_worker = n_chunks // n_workers
    assert n_chunks % (2 * n_workers) == 0
    mesh = plsc.VectorSubcoreMesh(core_axis_name="c", subcore_axis_name="s")

    @functools.partial(
        pl.kernel, mesh=mesh,
        out_type=jax.ShapeDtypeStruct((n_out, width), rows.dtype),
        scratch_types=[pltpu.VMEM((SC_ROW_CHUNK,), jnp.int32)] * (2 * TOP_K)
        + [pltpu.VMEM((SC_ROW_CHUNK, width), rows.dtype)] * 2
        + [pltpu.SemaphoreType.DMA] * (2 + 2 * TOP_K),
    )
    def scatter(rows_hbm, idx_hbm, out_hbm, *scratch):
        idx_v = scratch[:2 * TOP_K]
        rows_v = scratch[2 * TOP_K:2 * TOP_K + 2]
        load_sems, store_sems = scratch[-(2 + 2 * TOP_K):-2 * TOP_K], scratch[-2 * TOP_K:]
        worker = lax.axis_index("s") * n_cores + lax.axis_index("c")

        @pl.loop(0, per_worker, step=2)
        def _(i):
            loads = []
            for half in range(2):
                c = worker * per_worker + i + half
                src = rows_hbm.at[pl.ds(pl.multiple_of(c * SC_ROW_CHUNK, SC_ROW_CHUNK), SC_ROW_CHUNK)]
                loads.append(pltpu.async_copy(src, rows_v[half], load_sems[half]))
                for k in range(TOP_K):
                    pltpu.sync_copy(idx_hbm.at[k, c], idx_v[half * TOP_K + k])
            stores = []
            for half in range(2):
                loads[half].wait()
                for k in range(TOP_K):
                    j = half * TOP_K + k
                    stores.append(pltpu.async_copy(rows_v[half], out_hbm.at[idx_v[j]], store_sems[j]))
            for store in stores:
                store.wait()

    return scatter(rows, idx)


def _expert_kernel(be_ref, nvalid_ref, next_ref, xs_ref, wg_hbm, wu_hbm, wd_hbm, y_ref,
                   wg_st, wu_st, wd_st, wg_b, wu_b, wd_b, slot_ref, sems):
    b = pl.program_id(0)
    n_valid = nvalid_ref[b]
    sources, staged, cast = (wg_hbm, wu_hbm, wd_hbm), (wg_st, wu_st, wd_st), (wg_b, wu_b, wd_b)

    def weight_copies(expert, slot):
        return [pltpu.make_async_copy(src.at[expert], dst.at[slot], sems.at[slot, i])
                for i, (src, dst) in enumerate(zip(sources, staged))]

    @pl.when(n_valid > 0)
    def _():
        @pl.when((b == 0) | (be_ref[b] != be_ref[jnp.maximum(b - 1, 0)]))
        def _():
            @pl.when(b == 0)
            def _():
                slot_ref[0] = 0
                for copy in weight_copies(be_ref[0], 0):
                    copy.start()

            slot = slot_ref[0]
            for copy in weight_copies(be_ref[b], slot):
                copy.wait()
            for dst, src in zip(cast, staged):
                dst[...] = src[slot].astype(BF16)

            @pl.when(next_ref[b] >= 0)
            def _():
                for copy in weight_copies(next_ref[b], 1 - slot):
                    copy.start()

            slot_ref[0] = 1 - slot

        for c in range(EXPERT_BLOCK // EXPERT_CHUNK):
            rows = pl.ds(c * EXPERT_CHUNK, EXPERT_CHUNK)
            row_id = c * EXPERT_CHUNK + lax.broadcasted_iota(jnp.int32, (EXPERT_CHUNK, xs_ref.shape[1]), 0)
            packed = jnp.where(row_id < n_valid, xs_ref[rows, :], jnp.uint32(0))
            xb = _unpack_bf16_pairs(packed)
            gate = jnp.dot(xb, wg_b[...], preferred_element_type=F32)
            up = jnp.dot(xb, wu_b[...], preferred_element_type=F32)
            hid = (gate * jax.nn.sigmoid(gate) * up).astype(BF16)
            y_ref[rows, :] = _pack_bf16_pairs(jnp.dot(hid, wd_b[...], preferred_element_type=F32))

    @pl.when(n_valid == 0)
    def _():
        y_ref[...] = jnp.zeros_like(y_ref)


def _expert_call(block_expert, n_valid, next_expert, xs, w_gate, w_up, w_down):
    P = xs.shape[0]
    NB = P // EXPERT_BLOCK
    E, D, F = w_gate.shape
    grid_spec = pltpu.PrefetchScalarGridSpec(
        num_scalar_prefetch=3,
        grid=(NB,),
        in_specs=[
            pl.BlockSpec((EXPERT_BLOCK,) + xs.shape[1:], lambda b, *_: (b, 0)),
            pl.BlockSpec(memory_space=pl.ANY),
            pl.BlockSpec(memory_space=pl.ANY),
            pl.BlockSpec(memory_space=pl.ANY),
        ],
        out_specs=pl.BlockSpec((EXPERT_BLOCK, D // 2), lambda b, *_: (b, 0)),
        scratch_shapes=[
            pltpu.VMEM((2, D, F), F32),
            pltpu.VMEM((2, D, F), F32),
            pltpu.VMEM((2, F, D), F32),
            pltpu.VMEM((D, F), BF16),
            pltpu.VMEM((D, F), BF16),
            pltpu.VMEM((F, D), BF16),
            pltpu.SMEM((1,), jnp.int32),
            pltpu.SemaphoreType.DMA((2, 3)),
        ],
    )
    return pl.pallas_call(
        _expert_kernel,
        grid_spec=grid_spec,
        out_shape=jax.ShapeDtypeStruct((P, D // 2), jnp.uint32),
        compiler_params=pltpu.CompilerParams(
            dimension_semantics=("arbitrary",), vmem_limit_bytes=VMEM_LIMIT_BYTES),
        name="moe_experts",
    )(block_expert, n_valid, next_expert, xs, w_gate, w_up, w_down)


def _sc_gather_rows(table, idx):
    n_rows, width = idx.shape[0], table.shape[1]
    n_cores, n_workers = _sc_workers()
    per_worker = n_rows // n_workers
    assert n_rows % (n_workers * 2 * SC_ROW_CHUNK) == 0
    mesh = plsc.VectorSubcoreMesh(core_axis_name="c", subcore_axis_name="s")

    @functools.partial(
        pl.kernel, mesh=mesh,
        out_type=jax.ShapeDtypeStruct((n_rows, width), table.dtype),
        scratch_types=[pltpu.VMEM((SC_ROW_CHUNK,), jnp.int32)] * 2
        + [pltpu.VMEM((SC_ROW_CHUNK, width), table.dtype)] * 2
        + [pltpu.SemaphoreType.DMA] * 4,
    )
    def gather(table_hbm, idx_hbm, out_hbm, idx_a, idx_b, rows_a, rows_b, sem_ga, sem_gb, sem_wa, sem_wb):
        worker = lax.axis_index("s") * n_cores + lax.axis_index("c")
        base = worker * per_worker

        @pl.loop(0, per_worker // SC_ROW_CHUNK, step=2)
        def _(c):
            off_a = pl.multiple_of(base + c * SC_ROW_CHUNK, SC_ROW_CHUNK)
            off_b = pl.multiple_of(off_a + SC_ROW_CHUNK, SC_ROW_CHUNK)
            pltpu.sync_copy(idx_hbm.at[pl.ds(off_a, SC_ROW_CHUNK)], idx_a)
            pltpu.sync_copy(idx_hbm.at[pl.ds(off_b, SC_ROW_CHUNK)], idx_b)
            gather_a = pltpu.async_copy(table_hbm.at[idx_a], rows_a, sem_ga)
            gather_b = pltpu.async_copy(table_hbm.at[idx_b], rows_b, sem_gb)
            gather_a.wait()
            write_a = pltpu.async_copy(rows_a, out_hbm.at[pl.ds(off_a, SC_ROW_CHUNK)], sem_wa)
            gather_b.wait()
            write_b = pltpu.async_copy(rows_b, out_hbm.at[pl.ds(off_b, SC_ROW_CHUNK)], sem_wb)
            write_a.wait()
            write_b.wait()

    return gather(table, idx)


def _combine_kernel(x1_ref, rt_ref, y_ref, fg_ref, o_ref, *, final_norm):
    rt = rt_ref[...]
    y1 = _unpack_bf16_pairs(y_ref[0, 0]).astype(F32)
    y2 = _unpack_bf16_pairs(y_ref[0, 1]).astype(F32)
    h = x1_ref[...] + rt[:, 2:3] * y1 + rt[:, 3:4] * y2
    if final_norm:
        h = h * lax.rsqrt(jnp.mean(h * h, axis=-1, keepdims=True) + EPS) * fg_ref[...]
    o_ref[...] = h


def _combine_call(x1, rt, ysg, final_g, *, tm, final_norm):
    T, D = x1.shape
    return pl.pallas_call(
        functools.partial(_combine_kernel, final_norm=final_norm),
        grid=(T // tm,),
        in_specs=[
            pl.BlockSpec((tm, D), lambda t: (t, 0)),
            pl.BlockSpec((tm, ROUTER_COLS), lambda t: (t, 0)),
            pl.BlockSpec((1, TOP_K, tm, D // 2), lambda t: (t, 0, 0, 0)),
            pl.BlockSpec((1, D), lambda t: (0, 0)),
        ],
        out_specs=pl.BlockSpec((tm, D), lambda t: (t, 0)),
        out_shape=jax.ShapeDtypeStruct((T, D), F32),
        compiler_params=pltpu.CompilerParams(
            dimension_semantics=("parallel",), vmem_limit_bytes=VMEM_LIMIT_BYTES),
        name="moe_combine",
    )(x1, rt, ysg, final_g)


def _slot_layout(tile_counts, n_assign):
    NB = -(-n_assign // EXPERT_BLOCK) + N_EXPERTS
    n_tiles = tile_counts.shape[0]
    tc = tile_counts.astype(F32)
    hp = lax.Precision.HIGHEST
    counts = jnp.sum(tc, axis=0)
    padded = jnp.ceil(counts / EXPERT_BLOCK) * EXPERT_BLOCK
    upper = (jnp.arange(N_EXPERTS)[:, None] < jnp.arange(N_EXPERTS)[None, :]).astype(F32)
    pad_start = jnp.dot(padded, upper, precision=hp)
    pad_end = pad_start + padded
    lower = (jnp.arange(n_tiles)[:, None] > jnp.arange(n_tiles)[None, :]).astype(F32)
    tile_base = pad_start[None, :] + jnp.dot(lower, tc, precision=hp)
    block_start = jnp.arange(NB, dtype=F32) * EXPERT_BLOCK
    block_expert = jnp.minimum(jnp.sum((pad_end[None, :] <= block_start[:, None]).astype(jnp.int32), axis=1),
                               N_EXPERTS - 1)
    mine = block_expert[:, None] == jnp.arange(N_EXPERTS)[None, :]
    run_end = jnp.sum(jnp.where(mine, (pad_start + counts)[None, :], 0.0), axis=1)
    n_valid = jnp.clip(run_end - block_start, 0, EXPERT_BLOCK).astype(jnp.int32)
    eid = jnp.arange(N_EXPERTS)
    later_nonempty = (eid[None, :] > eid[:, None]) & (counts[None, :] > 0)
    next_nonempty = jnp.min(jnp.where(later_nonempty, eid[None, :], N_EXPERTS), axis=1)
    next_nonempty = jnp.where(next_nonempty < N_EXPERTS, next_nonempty, -1)
    next_expert = jnp.sum(jnp.where(mine, next_nonempty[None, :], 0), axis=1).astype(jnp.int32)
    return NB, block_expert.astype(jnp.int32), n_valid, next_expert, tile_base


def _rope_tables(S):
    inv = 1.0 / (ROPE_THETA ** (jnp.arange(0, HEAD_DIM, 2, dtype=F32) / HEAD_DIM))
    ang_t = inv[:, None] * jnp.arange(S, dtype=F32)[None, :]
    return jnp.cos(ang_t), jnp.sin(ang_t)


def _tiles(S):
    tile = min(512, S)
    return tile, tile, min(1024, S), min(1024, S), min(1024, S)


def kernel(x, norm1_g, w_in, lambda_q1, lambda_k1, lambda_q2, lambda_k2, subln_g, sinks, w_out,
           norm2_g, w_router_group, b_router_group, w_router_expert, b_router_expert,
           w_gate, w_up, w_down, final_g):
    B, S, D = x.shape
    T = B * S
    depth = w_in.shape[0]
    tq, tk, tm_proj, tm_tok, tq_swa = _tiles(S)
    qscale = HEAD_DIM ** -0.5 * math.log2(math.e)
    cos_t, sin_t = _rope_tables(S)

    c0 = DIFF_QK_COLS
    c1 = 2 * DIFF_QK_COLS
    c2 = c1 + DIFF_V_COLS
    c3 = c2 + SWA_Q_COLS
    c4 = c3 + SWA_KV_COLS
    for l in range(depth):
        lambda_init = 0.8 - 0.6 * math.exp(-0.3 * l)
        w = w_in[l]
        w_nat = jnp.concatenate([w[:, c0:c1], w[:, c3:c4]], axis=1).astype(BF16)
        w_tr = jnp.concatenate([w[:, :c0] * qscale, w[:, c1:c2], w[:, c2:c3] * qscale, w[:, c4:]],
                               axis=1).T.astype(BF16)
        dqt, dk, dvt, sqt, sk, svt = _proj_call(
            x, norm1_g[l][None, :], w_nat, w_tr, cos_t, sin_t, tm=tm_proj, tk=tk)

        lam_p = jnp.stack([lambda_q1[l], lambda_k1[l], lambda_q2[l], lambda_k2[l]]).astype(F32)
        o_diff = _diff_call(lam_p, dqt, dk, dvt, subln_g[l][None, :].astype(F32),
                            tq=tq, tk=tk, lambda_init=lambda_init)
        sink_row = jnp.repeat(sinks[l].astype(F32) * math.log2(math.e), WINDOW)[None, :]
        o_swa = _swa_call(sink_row, sqt, sk, svt, tq=tq_swa)

        wo_b = w_out[l].astype(BF16)
        w_router = jnp.zeros((D, ROUTER_COLS), F32)
        w_router = w_router.at[:, :N_GROUPS].set(w_router_group[l])
        w_router = w_router.at[:, N_GROUPS:N_GROUPS + N_EXPERTS].set(w_router_expert[l])
        w_router_hi = w_router.astype(BF16)
        w_router_lo = (w_router - w_router_hi.astype(F32)).astype(BF16)
        w_router = jnp.concatenate([w_router_hi, w_router_lo], axis=1)
        b_router = jnp.zeros((1, ROUTER_COLS), F32)
        b_router = b_router.at[0, :N_GROUPS].set(b_router_group[l])
        b_router = b_router.at[0, N_GROUPS:N_GROUPS + N_EXPERTS].set(b_router_expert[l])
        x1, n2p, rt, cnt = _mix_call(x, o_diff, o_swa, wo_b, norm2_g[l][None, :], w_router, b_router, tm=tm_tok)

        rt2 = rt.reshape(T, ROUTER_COLS)
        tile_counts = cnt[:, :, 0, :N_EXPERTS].reshape(T // tm_tok, N_EXPERTS).astype(jnp.int32)
        NB, block_expert, n_valid, next_expert, tile_base = _slot_layout(tile_counts, T * TOP_K)
        tile_base = jnp.broadcast_to(tile_base.astype(F32)[:, :, None], (T // tm_tok, N_EXPERTS, LANES))
        dest = _slot_call(rt2, tile_base, tm=tm_tok)
        scatter_idx = jnp.swapaxes(dest[:, :TOP_K, :], 0, 1).reshape(TOP_K, T // SC_ROW_CHUNK, SC_ROW_CHUNK)
        xs = _sc_scatter_rows(n2p.reshape(T, D // 2), scatter_idx, NB * EXPERT_BLOCK)
        ys = _expert_call(block_expert, n_valid, next_expert, xs, w_gate[l], w_up[l], w_down[l])
        ysg = _sc_gather_rows(ys, dest[:, :TOP_K, :].reshape(T * TOP_K))
        x = _combine_call(x1.reshape(T, D), rt2, ysg.reshape(T // tm_tok, TOP_K, tm_tok, D // 2),
                          final_g[None, :], tm=tm_tok, final_norm=(l == depth - 1)).reshape(B, S, D)
    return x
```

```python
import functools
import math

import jax
import jax.numpy as jnp
from jax import lax
from jax.experimental import pallas as pl
from jax.experimental.pallas import tpu as pltpu
from jax.experimental.pallas import tpu_sc as plsc

HEAD_DIM = 64
DIFF_HEADS = 4
DIFF_V_DIM = 2 * HEAD_DIM
SWA_Q_HEADS = 8
SWA_KV_HEADS = 2
SWA_GROUP = SWA_Q_HEADS // SWA_KV_HEADS
WINDOW = 128
ROPE_THETA = 10000.0
N_GROUPS = 4
EXPERTS_PER_GROUP = 8
N_EXPERTS = N_GROUPS * EXPERTS_PER_GROUP
TOP_K = 2
EXPERT_BLOCK = 512
EXPERT_CHUNK = 256
EPS = 1e-6
NEG = -1e30

DIFF_QK_COLS = DIFF_HEADS * 2 * HEAD_DIM
DIFF_V_COLS = DIFF_HEADS * DIFF_V_DIM
SWA_Q_COLS = SWA_Q_HEADS * HEAD_DIM
SWA_KV_COLS = SWA_KV_HEADS * HEAD_DIM
LANES = 128
SUBLANES = 8
BF16_SUBLANES = 16
VMEM_LIMIT_BYTES = 48 * 1024 * 1024
VT_ROWS = DIFF_V_DIM + BF16_SUBLANES
SWA_VT_ROWS = SWA_KV_COLS + BF16_SUBLANES
ROUTER_COLS = LANES
DIFF_UNROLL = 4
DIFF_S_BUFS = 4
DIFF_Q_TILES = 2

BF16 = jnp.bfloat16
F32 = jnp.float32


def _rope_lanes(x, cos_l, sin_l, first_half):
    rot = jnp.where(first_half, pltpu.roll(x, 96, 1), pltpu.roll(x, 32, 1))
    return x * cos_l + rot * sin_l


def _proj_kernel(x_ref, g_ref, wnat_ref, wtr_ref, cost_ref, sint_ref,
                 dqt_ref, dk_ref, dvt_ref, sqt_ref, sk_ref, svt_ref, *, tk):
    x = x_ref[0]
    tm = x.shape[0]
    n1 = x * lax.rsqrt(jnp.mean(x * x, axis=-1, keepdims=True) + EPS) * g_ref[...]
    n1b = n1.astype(BF16)
    nat = jnp.dot(n1b, wnat_ref[...], preferred_element_type=F32)
    tr = lax.dot_general(wtr_ref[...], n1b, (((1,), (1,)), ((), ())),
                         preferred_element_type=F32)

    c32, s32 = cost_ref[...].T, sint_ref[...].T
    cos_l = jnp.concatenate([c32] * (LANES // c32.shape[1]), axis=1)
    sin_l = jnp.concatenate([-s32, s32] * (LANES // HEAD_DIM), axis=1)
    first_half = (lax.broadcasted_iota(jnp.int32, (tm, LANES), 1) & (HEAD_DIM - 1)) < HEAD_DIM // 2
    for h in range(DIFF_HEADS):
        slab = nat[:, h * LANES:(h + 1) * LANES]
        dk_ref[0, h] = _rope_lanes(slab, cos_l, sin_l, first_half).astype(BF16)
    sk = _rope_lanes(nat[:, DIFF_QK_COLS:DIFF_QK_COLS + LANES], cos_l, sin_l, first_half).astype(BF16)
    for c in range(tm // WINDOW):
        sk_ref[0, c] = sk[c * WINDOW:(c + 1) * WINDOW]

    cos_t, sin_t = cost_ref[...], sint_ref[...]
    half = HEAD_DIM // 2

    def rope_rows(r0):
        x1 = tr[r0:r0 + half]
        x2 = tr[r0 + half:r0 + HEAD_DIM]
        return (x1 * cos_t - x2 * sin_t).astype(BF16), (x1 * sin_t + x2 * cos_t).astype(BF16)

    for h in range(DIFF_HEADS):
        for c in range(2):
            lo, hi = rope_rows(h * 2 * HEAD_DIM + c * HEAD_DIM)
            dqt_ref[0, h, c * HEAD_DIM:c * HEAD_DIM + half] = lo
            dqt_ref[0, h, c * HEAD_DIM + half:(c + 1) * HEAD_DIM] = hi
    ones_rows = (lax.broadcasted_iota(jnp.int32, (BF16_SUBLANES, tk), 0) == 0).astype(BF16)
    for h in range(DIFF_HEADS):
        r0 = DIFF_QK_COLS + h * DIFF_V_DIM
        for c in range(tm // tk):
            dvt_ref[0, h, c, :DIFF_V_DIM] = tr[r0:r0 + DIFF_V_DIM, c * tk:(c + 1) * tk].astype(BF16)
            dvt_ref[0, h, c, DIFF_V_DIM:] = ones_rows

    r0 = DIFF_QK_COLS + DIFF_V_COLS
    for h in range(SWA_Q_HEADS):
        lo, hi = rope_rows(r0 + h * HEAD_DIM)
        sqt_ref[0, h * HEAD_DIM:h * HEAD_DIM + half] = lo
        sqt_ref[0, h * HEAD_DIM + half:(h + 1) * HEAD_DIM] = hi
    r0 += SWA_Q_COLS
    for c in range(tm // WINDOW):
        svt_ref[0, c, :SWA_KV_COLS] = tr[r0:r0 + SWA_KV_COLS, c * WINDOW:(c + 1) * WINDOW].astype(BF16)
        svt_ref[0, c, SWA_KV_COLS:] = ones_rows[:, :WINDOW]


def _proj_call(x, g1, w_nat, w_tr, cos_t, sin_t, *, tm, tk):
    B, S, D = x.shape
    nkv = S // tk
    grid = (B, S // tm)
    const = lambda b, i: (0, 0)
    out_shape = (
        jax.ShapeDtypeStruct((B, DIFF_HEADS, 2 * HEAD_DIM, S), BF16),
        jax.ShapeDtypeStruct((B, DIFF_HEADS, S, 2 * HEAD_DIM), BF16),
        jax.ShapeDtypeStruct((B, DIFF_HEADS, nkv, VT_ROWS, tk), BF16),
        jax.ShapeDtypeStruct((B, SWA_Q_COLS, S), BF16),
        jax.ShapeDtypeStruct((B, S // WINDOW, WINDOW, SWA_KV_COLS), BF16),
        jax.ShapeDtypeStruct((B, S // WINDOW, SWA_VT_ROWS, WINDOW), BF16),
    )
    return pl.pallas_call(
        functools.partial(_proj_kernel, tk=tk),
        grid=grid,
        in_specs=[
            pl.BlockSpec((1, tm, D), lambda b, i: (b, i, 0)),
            pl.BlockSpec((1, D), const),
            pl.BlockSpec(w_nat.shape, const),
            pl.BlockSpec(w_tr.shape, const),
            pl.BlockSpec((HEAD_DIM // 2, tm), lambda b, i: (0, i)),
            pl.BlockSpec((HEAD_DIM // 2, tm), lambda b, i: (0, i)),
        ],
        out_specs=(
            pl.BlockSpec((1, DIFF_HEADS, 2 * HEAD_DIM, tm), lambda b, i: (b, 0, 0, i)),
            pl.BlockSpec((1, DIFF_HEADS, tm, 2 * HEAD_DIM), lambda b, i: (b, 0, i, 0)),
            pl.BlockSpec((1, DIFF_HEADS, tm // tk, VT_ROWS, tk), lambda b, i: (b, 0, i, 0, 0)),
            pl.BlockSpec((1, SWA_Q_COLS, tm), lambda b, i: (b, 0, i)),
            pl.BlockSpec((1, tm // WINDOW, WINDOW, SWA_KV_COLS), lambda b, i: (b, i, 0, 0)),
            pl.BlockSpec((1, tm // WINDOW, SWA_VT_ROWS, WINDOW), lambda b, i: (b, i, 0, 0)),
        ),
        out_shape=out_shape,
        compiler_params=pltpu.CompilerParams(
            dimension_semantics=("parallel", "parallel"), vmem_limit_bytes=VMEM_LIMIT_BYTES),
        name="proj_rope",
    )(x, g1, w_nat, w_tr, cos_t, sin_t)


def _diff_kernel(lam_ref, qt_ref, k_ref, vt_ref, g_ref, o_ref, *scratch, tq, tk, lambda_init):
    step = pl.program_id(2)
    s_bufs = scratch[:DIFF_S_BUFS]
    top_bufs = scratch[DIFF_S_BUFS:2 * DIFF_S_BUFS]
    state = scratch[2 * DIFF_S_BUFS:2 * DIFF_S_BUFS + 2 * DIFF_Q_TILES]
    bias_ref = scratch[-1]

    @pl.when(step == 0)
    def _():
        r = lax.broadcasted_iota(jnp.int32, (tk, 2 * tq), 0)
        c = lax.broadcasted_iota(jnp.int32, (tk, 2 * tq), 1) & (tq - 1)
        bias_ref[...] = jnp.where(r <= c, 0.0, NEG).astype(F32)

    lam_p = lam_ref[...]
    lam = (jnp.exp(jnp.sum(lam_p[0:1] * lam_p[1:2], axis=-1, keepdims=True))
           - jnp.exp(jnp.sum(lam_p[2:3] * lam_p[3:4], axis=-1, keepdims=True)) + lambda_init)

    for sub in range(DIFF_Q_TILES):
        _diff_query_tile(step * DIFF_Q_TILES + sub, qt_ref[0, 0, :, sub * tq:(sub + 1) * tq], k_ref, vt_ref,
                         g_ref, o_ref.at[0, pl.ds(sub * tq, tq), :], s_bufs, top_bufs,
                         state[2 * sub], state[2 * sub + 1], bias_ref, lam,
                         tq=tq, tk=tk, lambda_init=lambda_init)


def _diff_query_tile(i, qt, k_ref, vt_ref, g_ref, o_ref, s_bufs, top_bufs, m_ref, acc_ref, bias_ref, lam,
                     *, tq, tk, lambda_init):
    z = jnp.zeros((HEAD_DIM, tq), BF16)
    qw = jnp.concatenate([jnp.concatenate([qt[:HEAD_DIM], z], axis=1),
                          jnp.concatenate([z, qt[HEAD_DIM:]], axis=1)], axis=0)

    def scores(j, par):
        kt = k_ref[0, 0, pl.ds(pl.multiple_of(j * tk, tk), tk), :]
        s = jnp.dot(kt, qw, preferred_element_type=F32)
        s_bufs[par][...] = s
        top_bufs[par][...] = jnp.max(s, axis=0, keepdims=True)

    def absorb(j, par, masked):
        s = s_bufs[par][...]
        if masked:
            s = s + bias_ref[...]
            top = jnp.max(s, axis=0, keepdims=True)
        else:
            top = top_bufs[par][...]
        m = m_ref[...]
        m_new = jnp.maximum(m, top)
        alpha = jnp.exp2(m - m_new)
        p = jnp.exp2(s - m_new).astype(BF16)
        m_ref[...] = m_new
        pv = jnp.dot(vt_ref[0, 0, j], p, preferred_element_type=F32)
        acc_ref[...] = alpha * acc_ref[...] + pv

    m_ref[...] = jnp.full(m_ref.shape, NEG, F32)
    acc_ref[...] = jnp.zeros(acc_ref.shape, F32)

    nfull = (i * tq) // tk
    scores(nfull, 0)
    scores(0, 1)
    absorb(nfull, 0, True)

    def group(t, c):
        j = DIFF_UNROLL * t
        for idx in range(DIFF_UNROLL):
            scores(j + idx + 1, (idx + 2) % DIFF_S_BUFS)
            absorb(j + idx, (idx + 1) % DIFF_S_BUFS, False)
        return c

    lax.fori_loop(0, nfull // DIFF_UNROLL, group, 0)

    for rem in range(1, DIFF_UNROLL):
        @pl.when(nfull % DIFF_UNROLL == rem)
        def _():
            first = nfull - rem
            for idx in range(rem):
                if idx + 1 < rem:
                    scores(first + idx + 1, (idx + 2) % DIFF_S_BUFS)
                absorb(first + idx, (idx + 1) % DIFF_S_BUFS, False)

    inv_l = 1.0 / acc_ref[DIFF_V_DIM:DIFF_V_DIM + 1, :]
    o = (acc_ref[:DIFF_V_DIM, :tq] * inv_l[:, :tq]
         - lam * (acc_ref[:DIFF_V_DIM, tq:] * inv_l[:, tq:]))
    o = o * lax.rsqrt(jnp.mean(o * o, axis=0, keepdims=True) + EPS)
    o_ref[...] = (o.T * g_ref[...] * (1.0 - lambda_init)).astype(BF16)


def _diff_call(lam_p, dqt, dk, dvt, subln_g, *, tq, tk, lambda_init):
    B, H, _, S = dqt.shape
    assert tk == tq and S % tk == 0, "the diagonal tile's causal pattern is built for square tiles"
    nkv = S // tk
    tq_step = DIFF_Q_TILES * tq
    assert S % tq_step == 0
    grid = (B, H, S // tq_step)
    return pl.pallas_call(
        functools.partial(_diff_kernel, tq=tq, tk=tk, lambda_init=lambda_init),
        grid=grid,
        in_specs=[
            pl.BlockSpec(lam_p.shape, lambda b, h, i: (0, 0)),
            pl.BlockSpec((1, 1, 2 * HEAD_DIM, tq_step), lambda b, h, i: (b, h, 0, i)),
            pl.BlockSpec((1, 1, S, 2 * HEAD_DIM), lambda b, h, i: (b, h, 0, 0)),
            pl.BlockSpec((1, 1, nkv, VT_ROWS, tk), lambda b, h, i: (b, h, 0, 0, 0)),
            pl.BlockSpec((1, DIFF_V_DIM), lambda b, h, i: (0, 0)),
        ],
        out_specs=pl.BlockSpec((1, tq_step, DIFF_V_DIM), lambda b, h, i: (b, i, h)),
        out_shape=jax.ShapeDtypeStruct((B, S, DIFF_V_COLS), BF16),
        scratch_shapes=[pltpu.VMEM((tk, 2 * tq), F32)] * DIFF_S_BUFS + [
            pltpu.VMEM((1, 2 * tq), F32)] * DIFF_S_BUFS + [
            pltpu.VMEM((1, 2 * tq), F32),
            pltpu.VMEM((VT_ROWS, 2 * tq), F32)] * DIFF_Q_TILES + [
            pltpu.VMEM((tk, 2 * tq), F32),
        ],
        compiler_params=pltpu.CompilerParams(
            dimension_semantics=("parallel", "parallel", "arbitrary"),
            vmem_limit_bytes=VMEM_LIMIT_BYTES),
        name="diff_attn",
    )(lam_p, dqt, dk, dvt, subln_g)


def _swa_kernel(sink_ref, qt_ref, k_ref, vt_ref, o_ref, *, tq):
    i = pl.program_id(1)
    n_cols = SWA_Q_HEADS * WINDOW
    half_cols = n_cols // SWA_KV_HEADS
    sink = sink_ref[...]
    row = lax.broadcasted_iota(jnp.int32, (2 * WINDOW, WINDOW), 0)
    qrel = lax.broadcasted_iota(jnp.int32, (2 * WINDOW, WINDOW), 1)
    band = (row - WINDOW <= qrel) & (row > qrel)
    in_current = row >= WINDOW
    z = jnp.zeros((HEAD_DIM, half_cols), BF16)
    for sub in range(tq // WINDOW):
        n = i * (tq // WINDOW) + sub
        prev = jnp.maximum(n - 1, 0)
        kwin = jnp.concatenate([k_ref[0, prev], k_ref[0, n]], axis=0)
        vtwin = jnp.concatenate([vt_ref[0, prev], vt_ref[0, n]], axis=1)
        qt = qt_ref[0, :, sub * WINDOW:(sub + 1) * WINDOW]
        heads = [qt[h * HEAD_DIM:(h + 1) * HEAD_DIM] for h in range(SWA_Q_HEADS)]
        qw = jnp.concatenate(
            [jnp.concatenate(heads[:SWA_GROUP] + [z], axis=1),
             jnp.concatenate([z] + heads[SWA_GROUP:], axis=1)], axis=0)
        s = jnp.dot(kwin, qw, preferred_element_type=F32)
        valid = band & (in_current | (n >= 1))
        s = jnp.concatenate(
            [jnp.where(valid, s[:, h * WINDOW:(h + 1) * WINDOW], NEG) for h in range(SWA_Q_HEADS)], axis=1)
        m = jnp.maximum(jnp.max(s, axis=0, keepdims=True), sink)
        p = jnp.exp2(s - m).astype(BF16)
        acc = jnp.dot(vtwin, p, preferred_element_type=F32)
        den = acc[SWA_KV_COLS:SWA_KV_COLS + 1] + jnp.exp2(sink - m)
        on = acc[:SWA_KV_COLS] / den
        u = jnp.concatenate([on[:HEAD_DIM, :half_cols], on[HEAD_DIM:, half_cols:]], axis=1)
        for hp in range(SWA_Q_HEADS // 2):
            two = jnp.concatenate([u[:, (2 * hp) * WINDOW:(2 * hp + 1) * WINDOW],
                                   u[:, (2 * hp + 1) * WINDOW:(2 * hp + 2) * WINDOW]], axis=0)
            o_ref[0, sub * WINDOW:(sub + 1) * WINDOW, hp * LANES:(hp + 1) * LANES] = two.T.astype(BF16)


def _swa_call(sink_row, sqt, sk, svt, *, tq):
    B, _, S = sqt.shape
    nb = S // WINDOW
    return pl.pallas_call(
        functools.partial(_swa_kernel, tq=tq),
        grid=(B, S // tq),
        in_specs=[
            pl.BlockSpec(sink_row.shape, lambda b, i: (0, 0)),
            pl.BlockSpec((1, SWA_Q_COLS, tq), lambda b, i: (b, 0, i)),
            pl.BlockSpec((1, nb, WINDOW, SWA_KV_COLS), lambda b, i: (b, 0, 0, 0)),
            pl.BlockSpec((1, nb, SWA_VT_ROWS, WINDOW), lambda b, i: (b, 0, 0, 0)),
        ],
        out_specs=pl.BlockSpec((1, tq, SWA_Q_COLS), lambda b, i: (b, i, 0)),
        out_shape=jax.ShapeDtypeStruct((B, S, SWA_Q_COLS), BF16),
        compiler_params=pltpu.CompilerParams(
            dimension_semantics=("parallel", "arbitrary"), vmem_limit_bytes=VMEM_LIMIT_BYTES),
        name="swa_attn",
    )(sink_row, sqt, sk, svt)


def _pack_bf16_pairs(x):
    n = x.shape[1] // 2
    lo = lax.bitcast_convert_type(x[:, :n].astype(BF16).astype(F32), jnp.uint32)
    hi = lax.bitcast_convert_type(x[:, n:].astype(BF16).astype(F32), jnp.uint32)
    return (lo >> 16) | (hi & jnp.uint32(0xFFFF0000))


def _unpack_bf16_pairs(w):
    lo = lax.bitcast_convert_type(w << 16, F32)
    hi = lax.bitcast_convert_type(w & jnp.uint32(0xFFFF0000), F32)
    return jnp.concatenate([lo, hi], axis=1).astype(BF16)


def _mix_kernel(x_ref, od_ref, os_ref, wo_ref, g2_ref, wr_ref, br_ref, x1_ref, n2_ref, rt_ref, cnt_ref):
    tm = x_ref.shape[1]
    lane = lax.broadcasted_iota(jnp.int32, (tm, ROUTER_COLS), 1)
    lane_f = lane.astype(F32)
    big = float(ROUTER_COLS)
    mixed = jnp.concatenate([od_ref[0], os_ref[0]], axis=1)
    h = x_ref[0] + jnp.dot(mixed, wo_ref[...], preferred_element_type=F32)
    x1_ref[0] = h
    n2 = h * lax.rsqrt(jnp.mean(h * h, axis=-1, keepdims=True) + EPS) * g2_ref[...]
    n2_ref[0] = _pack_bf16_pairs(n2)
    n2_hi = n2.astype(BF16)
    n2_lo = (n2 - n2_hi.astype(F32)).astype(BF16)
    parts = jnp.dot(jnp.concatenate([n2_hi, n2_lo], axis=0), wr_ref[...],
                    preferred_element_type=F32)
    logits = ((parts[:tm, :ROUTER_COLS] + parts[tm:, ROUTER_COLS:])
              + (parts[:tm, ROUTER_COLS:] + parts[tm:, :ROUTER_COLS])) + br_ref[...]
    gl = jnp.where(lane < N_GROUPS, logits, -jnp.inf)
    gm = jnp.max(gl, axis=-1, keepdims=True)
    p_top = 1.0 / jnp.sum(jnp.exp(gl - gm), axis=-1, keepdims=True)
    g_idx = jnp.min(jnp.where(gl == gm, lane_f, big), axis=-1, keepdims=True)
    e_lo = N_GROUPS + EXPERTS_PER_GROUP * g_idx
    el = jnp.where((lane_f >= e_lo) & (lane_f < e_lo + EXPERTS_PER_GROUP), logits, -jnp.inf)
    v1 = jnp.max(el, axis=-1, keepdims=True)
    i1 = jnp.min(jnp.where(el == v1, lane_f, big), axis=-1, keepdims=True)
    el2 = jnp.where(lane_f == i1, -jnp.inf, el)
    v2 = jnp.max(el2, axis=-1, keepdims=True)
    i2 = jnp.min(jnp.where(el2 == v2, lane_f, big), axis=-1, keepdims=True)
    e21 = jnp.exp(v2 - v1)
    gate1 = p_top / (1.0 + e21)
    gate2 = p_top * e21 / (1.0 + e21)
    rt_ref[0] = jnp.where(lane == 0, i1 - N_GROUPS,
                jnp.where(lane == 1, i2 - N_GROUPS,
                jnp.where(lane == 2, gate1, jnp.where(lane == 3, gate2, 0.0))))
    chosen = ((lane_f == i1 - N_GROUPS) | (lane_f == i2 - N_GROUPS)).astype(F32)
    cnt_ref[0, 0] = jnp.broadcast_to(jnp.sum(chosen, axis=0, keepdims=True), cnt_ref.shape[2:])


def _mix_call(x, o_diff, o_swa, w_out, g2, w_router, b_router, *, tm):
    B, S, D = x.shape
    const = lambda b, i: (0, 0)
    row = lambda b, i: (b, i, 0)
    nt = S // tm
    return pl.pallas_call(
        _mix_kernel,
        grid=(B, nt),
        in_specs=[
            pl.BlockSpec((1, tm, D), row),
            pl.BlockSpec((1, tm, DIFF_V_COLS), row),
            pl.BlockSpec((1, tm, SWA_Q_COLS), row),
            pl.BlockSpec(w_out.shape, const),
            pl.BlockSpec((1, D), const),
            pl.BlockSpec(w_router.shape, const),
            pl.BlockSpec((1, ROUTER_COLS), const),
        ],
        out_specs=(pl.BlockSpec((1, tm, D), row), pl.BlockSpec((1, tm, D // 2), row),
                   pl.BlockSpec((1, tm, ROUTER_COLS), row),
                   pl.BlockSpec((1, 1, SUBLANES, ROUTER_COLS), lambda b, i: (b, i, 0, 0))),
        out_shape=(jax.ShapeDtypeStruct((B, S, D), F32), jax.ShapeDtypeStruct((B, S, D // 2), jnp.uint32),
                   jax.ShapeDtypeStruct((B, S, ROUTER_COLS), F32),
                   jax.ShapeDtypeStruct((B, nt, SUBLANES, ROUTER_COLS), F32)),
        compiler_params=pltpu.CompilerParams(
            dimension_semantics=("parallel", "parallel"), vmem_limit_bytes=VMEM_LIMIT_BYTES),
        name="outproj_router",
    )(x, o_diff, o_swa, w_out, g2, w_router, b_router)


def _slot_kernel(rt_ref, base_ref, dest_ref, *, tm):
    rt_t = rt_ref[...].T
    e1 = rt_t[0:1].astype(jnp.int32)
    e2 = rt_t[1:2].astype(jnp.int32)
    eid = lax.broadcasted_iota(jnp.int32, (N_EXPERTS, tm), 0)
    oh1 = eid == e1
    oh2 = eid == e2
    earlier = (lax.broadcasted_iota(jnp.int32, (tm, tm), 0)
               < lax.broadcasted_iota(jnp.int32, (tm, tm), 1)).astype(BF16)
    before = jnp.dot((oh1 | oh2).astype(BF16), earlier, preferred_element_type=F32)
    slot = before + base_ref[0][:, 0:1]
    d1 = jnp.sum(jnp.where(oh1, slot, 0.0), axis=0, keepdims=True).astype(jnp.int32)
    d2 = jnp.sum(jnp.where(oh2, slot, 0.0), axis=0, keepdims=True).astype(jnp.int32)
    dest_ref[0] = jnp.concatenate([d1, d2, jnp.zeros((SUBLANES - TOP_K, tm), jnp.int32)], axis=0)


def _slot_call(rt, tile_base, *, tm):
    nt = rt.shape[0] // tm
    return pl.pallas_call(
        functools.partial(_slot_kernel, tm=tm),
        grid=(nt,),
        in_specs=[
            pl.BlockSpec((tm, ROUTER_COLS), lambda t: (t, 0)),
            pl.BlockSpec((1, N_EXPERTS, LANES), lambda t: (t, 0, 0)),
        ],
        out_specs=pl.BlockSpec((1, SUBLANES, tm), lambda t: (t, 0, 0)),
        out_shape=jax.ShapeDtypeStruct((nt, SUBLANES, tm), jnp.int32),
        compiler_params=pltpu.CompilerParams(dimension_semantics=("parallel",)),
        name="moe_slots",
    )(rt, tile_base)


SC_ROW_CHUNK = 64


def _sc_workers():
    info = plsc.get_sparse_core_info()
    return info.num_cores, info.num_cores * info.num_subcores


def _sc_scatter_rows(rows, idx, n_out):
    n, width = rows.shape
    n_cores, n_workers = _sc_workers()
    n_chunks = n // SC_ROW_CHUNK
    per_worker = n_chunks // n_workers
    assert n_chunks % (2 * n_workers) == 0
    mesh = plsc.VectorSubcoreMesh(core_axis_name="c", subcore_axis_name="s")

    @functools.partial(
        pl.kernel, mesh=mesh,
        out_type=jax.ShapeDtypeStruct((n_out, width), rows.dtype),
        scratch_types=[pltpu.VMEM((SC_ROW_CHUNK,), jnp.int32)] * (2 * TOP_K)
        + [pltpu.VMEM((SC_ROW_CHUNK, width), rows.dtype)] * 2
        + [pltpu.SemaphoreType.DMA] * (2 + 2 * TOP_K),
    )
    def scatter(rows_hbm, idx_hbm, out_hbm, *scratch):
        idx_v = scratch[:2 * TOP_K]
        rows_v = scratch[2 * TOP_K:2 * TOP_K + 2]
        load_sems, store_sems = scratch[-(2 + 2 * TOP_K):-2 * TOP_K], scratch[-2 * TOP_K:]
        worker = lax.axis_index("s") * n_cores + lax.axis_index("c")

        @pl.loop(0, per_worker, step=2)
        def _(i):
            loads = []
            for half in range(2):
                c = worker * per_worker + i + half
                src = rows_hbm.at[pl.ds(pl.multiple_of(c * SC_ROW_CHUNK, SC_ROW_CHUNK), SC_ROW_CHUNK)]
                loads.append(pltpu.async_copy(src, rows_v[half], load_sems[half]))
                for k in range(TOP_K):
                    pltpu.sync_copy(idx_hbm.at[k, c], idx_v[half * TOP_K + k])
            stores = []
            for half in range(2):
                loads[half].wait()
                for k in range(TOP_K):
                    j = half * TOP_K + k
                    stores.append(pltpu.async_copy(rows_v[half], out_hbm.at[idx_v[j]], store_sems[j]))
            for store in stores:
                store.wait()

    return scatter(rows, idx)


def _expert_kernel(be_ref, nvalid_ref, next_ref, xs_ref, wg_hbm, wu_hbm, wd_hbm, y_ref,
                   wg_st, wu_st, wd_st, wg_b, wu_b, wd_b, slot_ref, sems):
    b = pl.program_id(0)
    n_valid = nvalid_ref[b]
    sources, staged, cast = (wg_hbm, wu_hbm, wd_hbm), (wg_st, wu_st, wd_st), (wg_b, wu_b, wd_b)

    def weight_copies(expert, slot):
        return [pltpu.make_async_copy(src.at[expert], dst.at[slot], sems.at[slot, i])
                for i, (src, dst) in enumerate(zip(sources, staged))]

    @pl.when(n_valid > 0)
    def _():
        @pl.when((b == 0) | (be_ref[b] != be_ref[jnp.maximum(b - 1, 0)]))
        def _():
            @pl.when(b == 0)
            def _():
                slot_ref[0] = 0
                for copy in weight_copies(be_ref[0], 0):
                    copy.start()

            slot = slot_ref[0]
            for copy in weight_copies(be_ref[b], slot):
                copy.wait()
            for dst, src in zip(cast, staged):
                dst[...] = src[slot].astype(BF16)

            @pl.when(next_ref[b] >= 0)
            def _():
                for copy in weight_copies(next_ref[b], 1 - slot):
                    copy.start()

            slot_ref[0] = 1 - slot

        for c in range(EXPERT_BLOCK // EXPERT_CHUNK):
            rows = pl.ds(c * EXPERT_CHUNK, EXPERT_CHUNK)
            row_id = c * EXPERT_CHUNK + lax.broadcasted_iota(jnp.int32, (EXPERT_CHUNK, xs_ref.shape[1]), 0)
            packed = jnp.where(row_id < n_valid, xs_ref[rows, :], jnp.uint32(0))
            xb = _unpack_bf16_pairs(packed)
            gate = jnp.dot(xb, wg_b[...], preferred_element_type=F32)
            up = jnp.dot(xb, wu_b[...], preferred_element_type=F32)
            hid = (gate * jax.nn.sigmoid(gate) * up).astype(BF16)
            y_ref[rows, :] = _pack_bf16_pairs(jnp.dot(hid, wd_b[...], preferred_element_type=F32))

    @pl.when(n_valid == 0)
    def _():
        y_ref[...] = jnp.zeros_like(y_ref)


def _expert_call(block_expert, n_valid, next_expert, xs, w_gate, w_up, w_down):
    P = xs.shape[0]
    NB = P // EXPERT_BLOCK
    E, D, F = w_gate.shape
    grid_spec = pltpu.PrefetchScalarGridSpec(
        num_scalar_prefetch=3,
        grid=(NB,),
        in_specs=[
            pl.BlockSpec((EXPERT_BLOCK,) + xs.shape[1:], lambda b, *_: (b, 0)),
            pl.BlockSpec(memory_space=pl.ANY),
            pl.BlockSpec(memory_space=pl.ANY),
            pl.BlockSpec(memory_space=pl.ANY),
        ],
        out_specs=pl.BlockSpec((EXPERT_BLOCK, D // 2), lambda b, *_: (b, 0)),
        scratch_shapes=[
            pltpu.VMEM((2, D, F), F32),
            pltpu.VMEM((2, D, F), F32),
            pltpu.VMEM((2, F, D), F32),
            pltpu.VMEM((D, F), BF16),
            pltpu.VMEM((D, F), BF16),
            pltpu.VMEM((F, D), BF16),
            pltpu.SMEM((1,), jnp.int32),
            pltpu.SemaphoreType.DMA((2, 3)),
        ],
    )
    return pl.pallas_call(
        _expert_kernel,
        grid_spec=grid_spec,
        out_shape=jax.ShapeDtypeStruct((P, D // 2), jnp.uint32),
        compiler_params=pltpu.CompilerParams(
            dimension_semantics=("arbitrary",), vmem_limit_bytes=VMEM_LIMIT_BYTES),
        name="moe_experts",
    )(block_expert, n_valid, next_expert, xs, w_gate, w_up, w_down)


def _sc_gather_rows(table, idx):
    n_rows, width = idx.shape[0], table.shape[1]
    n_cores, n_workers = _sc_workers()
    per_worker = n_rows // n_workers
    assert n_rows % (n_workers * 2 * SC_ROW_CHUNK) == 0
    mesh = plsc.VectorSubcoreMesh(core_axis_name="c", subcore_axis_name="s")

    @functools.partial(
        pl.kernel, mesh=mesh,
        out_type=jax.ShapeDtypeStruct((n_rows, width), table.dtype),
        scratch_types=[pltpu.VMEM((SC_ROW_CHUNK,), jnp.int32)] * 2
        + [pltpu.VMEM((SC_ROW_CHUNK, width), table.dtype)] * 2
        + [pltpu.SemaphoreType.DMA] * 4,
    )
    def gather(table_hbm, idx_hbm, out_hbm, idx_a, idx_b, rows_a, rows_b, sem_ga, sem_gb, sem_wa, sem_wb):
        worker = lax.axis_index("s") * n_cores + lax.axis_index("c")
        base = worker * per_worker

        @pl.loop(0, per_worker // SC_ROW_CHUNK, step=2)
        def _(c):
            off_a = pl.multiple_of(base + c * SC_ROW_CHUNK, SC_ROW_CHUNK)
            off_b = pl.multiple_of(off_a + SC_ROW_CHUNK, SC_ROW_CHUNK)
            pltpu.sync_copy(idx_hbm.at[pl.ds(off_a, SC_ROW_CHUNK)], idx_a)
            pltpu.sync_copy(idx_hbm.at[pl.ds(off_b, SC_ROW_CHUNK)], idx_b)
            gather_a = pltpu.async_copy(table_hbm.at[idx_a], rows_a, sem_ga)
            gather_b = pltpu.async_copy(table_hbm.at[idx_b], rows_b, sem_gb)
            gather_a.wait()
            write_a = pltpu.async_copy(rows_a, out_hbm.at[pl.ds(off_a, SC_ROW_CHUNK)], sem_wa)
            gather_b.wait()
            write_b = pltpu.async_copy(rows_b, out_hbm.at[pl.ds(off_b, SC_ROW_CHUNK)], sem_wb)
            write_a.wait()
            write_b.wait()

    return gather(table, idx)


def _combine_kernel(x1_ref, rt_ref, y_ref, fg_ref, o_ref, *, final_norm):
    rt = rt_ref[...]
    y1 = _unpack_bf16_pairs(y_ref[0, 0]).astype(F32)
    y2 = _unpack_bf16_pairs(y_ref[0, 1]).astype(F32)
    h = x1_ref[...] + rt[:, 2:3] * y1 + rt[:, 3:4] * y2
    if final_norm:
        h = h * lax.rsqrt(jnp.mean(h * h, axis=-1, keepdims=True) + EPS) * fg_ref[...]
    o_ref[...] = h


def _combine_call(x1, rt, ysg, final_g, *, tm, final_norm):
    T, D = x1.shape
    return pl.pallas_call(
        functools.partial(_combine_kernel, final_norm=final_norm),
        grid=(T // tm,),
        in_specs=[
            pl.BlockSpec((tm, D), lambda t: (t, 0)),
            pl.BlockSpec((tm, ROUTER_COLS), lambda t: (t, 0)),
            pl.BlockSpec((1, TOP_K, tm, D // 2), lambda t: (t, 0, 0, 0)),
            pl.BlockSpec((1, D), lambda t: (0, 0)),
        ],
        out_specs=pl.BlockSpec((tm, D), lambda t: (t, 0)),
        out_shape=jax.ShapeDtypeStruct((T, D), F32),
        compiler_params=pltpu.CompilerParams(
            dimension_semantics=("parallel",), vmem_limit_bytes=VMEM_LIMIT_BYTES),
        name="moe_combine",
    )(x1, rt, ysg, final_g)


def _slot_layout(tile_counts, n_assign):
    NB = -(-n_assign // EXPERT_BLOCK) + N_EXPERTS
    n_tiles = tile_counts.shape[0]
    tc = tile_counts.astype(F32)
    hp = lax.Precision.HIGHEST
    counts = jnp.sum(tc, axis=0)
    padded = jnp.ceil(counts / EXPERT_BLOCK) * EXPERT_BLOCK
    upper = (jnp.arange(N_EXPERTS)[:, None] < jnp.arange(N_EXPERTS)[None, :]).astype(F32)
    pad_start = jnp.dot(padded, upper, precision=hp)
    pad_end = pad_start + padded
    lower = (jnp.arange(n_tiles)[:, None] > jnp.arange(n_tiles)[None, :]).astype(F32)
    tile_base = pad_start[None, :] + jnp.dot(lower, tc, precision=hp)
    block_start = jnp.arange(NB, dtype=F32) * EXPERT_BLOCK
    block_expert = jnp.minimum(jnp.sum((pad_end[None, :] <= block_start[:, None]).astype(jnp.int32), axis=1),
                               N_EXPERTS - 1)
    mine = block_expert[:, None] == jnp.arange(N_EXPERTS)[None, :]
    run_end = jnp.sum(jnp.where(mine, (pad_start + counts)[None, :], 0.0), axis=1)
    n_valid = jnp.clip(run_end - block_start, 0, EXPERT_BLOCK).astype(jnp.int32)
    eid = jnp.arange(N_EXPERTS)
    later_nonempty = (eid[None, :] > eid[:, None]) & (counts[None, :] > 0)
    next_nonempty = jnp.min(jnp.where(later_nonempty, eid[None, :], N_EXPERTS), axis=1)
    next_nonempty = jnp.where(next_nonempty < N_EXPERTS, next_nonempty, -1)
    next_expert = jnp.sum(jnp.where(mine, next_nonempty[None, :], 0), axis=1).astype(jnp.int32)
    return NB, block_expert.astype(jnp.int32), n_valid, next_expert, tile_base


def _rope_tables(S):
    inv = 1.0 / (ROPE_THETA ** (jnp.arange(0, HEAD_DIM, 2, dtype=F32) / HEAD_DIM))
    ang_t = inv[:, None] * jnp.arange(S, dtype=F32)[None, :]
    return jnp.cos(ang_t), jnp.sin(ang_t)


def _tiles(S):
    tile = min(512, S)
    return tile, tile, min(1024, S), min(1024, S), min(1024, S)


def kernel(x, norm1_g, w_in, lambda_q1, lambda_k1, lambda_q2, lambda_k2, subln_g, sinks, w_out,
           norm2_g, w_router_group, b_router_group, w_router_expert, b_router_expert,
           w_gate, w_up, w_down, final_g):
    B, S, D = x.shape
    T = B * S
    depth = w_in.shape[0]
    tq, tk, tm_proj, tm_tok, tq_swa = _tiles(S)
    qscale = HEAD_DIM ** -0.5 * math.log2(math.e)
    cos_t, sin_t = _rope_tables(S)

    c0 = DIFF_QK_COLS
    c1 = 2 * DIFF_QK_COLS
    c2 = c1 + DIFF_V_COLS
    c3 = c2 + SWA_Q_COLS
    c4 = c3 + SWA_KV_COLS
    for l in range(depth):
        lambda_init = 0.8 - 0.6 * math.exp(-0.3 * l)
        w = w_in[l]
        w_nat = jnp.concatenate([w[:, c0:c1], w[:, c3:c4]], axis=1).astype(BF16)
        w_tr = jnp.concatenate([w[:, :c0] * qscale, w[:, c1:c2], w[:, c2:c3] * qscale, w[:, c4:]],
                               axis=1).astype(BF16).T
        dqt, dk, dvt, sqt, sk, svt = _proj_call(
            x, norm1_g[l][None, :], w_nat, w_tr, cos_t, sin_t, tm=tm_proj, tk=tk)

        lam_p = jnp.stack([lambda_q1[l], lambda_k1[l], lambda_q2[l], lambda_k2[l]]).astype(F32)
        o_diff = _diff_call(lam_p, dqt, dk, dvt, subln_g[l][None, :].astype(F32),
                            tq=tq, tk=tk, lambda_init=lambda_init)
        sink_row = jnp.repeat(sinks[l].astype(F32) * math.log2(math.e), WINDOW)[None, :]
        o_swa = _swa_call(sink_row, sqt, sk, svt, tq=tq_swa)

        wo_b = w_out[l].astype(BF16)
        w_router = jnp.zeros((D, ROUTER_COLS), F32)
        w_router = w_router.at[:, :N_GROUPS].set(w_router_group[l])
        w_router = w_router.at[:, N_GROUPS:N_GROUPS + N_EXPERTS].set(w_router_expert[l])
        w_router_hi = w_router.astype(BF16)
        w_router_lo = (w_router - w_router_hi.astype(F32)).astype(BF16)
        w_router = jnp.concatenate([w_router_hi, w_router_lo], axis=1)
        b_router = jnp.zeros((1, ROUTER_COLS), F32)
        b_router = b_router.at[0, :N_GROUPS].set(b_router_group[l])
        b_router = b_router.at[0, N_GROUPS:N_GROUPS + N_EXPERTS].set(b_router_expert[l])
        x1, n2p, rt, cnt = _mix_call(x, o_diff, o_swa, wo_b, norm2_g[l][None, :], w_router, b_router, tm=tm_tok)

        rt2 = rt.reshape(T, ROUTER_COLS)
        tile_counts = cnt[:, :, 0, :N_EXPERTS].reshape(T // tm_tok, N_EXPERTS).astype(jnp.int32)
        NB, block_expert, n_valid, next_expert, tile_base = _slot_layout(tile_counts, T * TOP_K)
        tile_base = jnp.broadcast_to(tile_base.astype(F32)[:, :, None], (T // tm_tok, N_EXPERTS, LANES))
        dest = _slot_call(rt2, tile_base, tm=tm_tok)
        scatter_idx = jnp.swapaxes(dest[:, :TOP_K, :], 0, 1).reshape(TOP_K, T // SC_ROW_CHUNK, SC_ROW_CHUNK)
        xs = _sc_scatter_rows(n2p.reshape(T, D // 2), scatter_idx, NB * EXPERT_BLOCK)
        ys = _expert_call(block_expert, n_valid, next_expert, xs, w_gate[l], w_up[l], w_down[l])
        ysg = _sc_gather_rows(ys, dest[:, :TOP_K, :].reshape(T * TOP_K))
        x = _combine_call(x1.reshape(T, D), rt2, ysg.reshape(T // tm_tok, TOP_K, tm_tok, D // 2),
                          final_g[None, :], tm=tm_tok, final_norm=(l == depth - 1)).reshape(B, S, D)
    return x
```

```python
import functools
import math

import jax
import jax.numpy as jnp
from jax import lax
from jax.experimental import pallas as pl
from jax.experimental.pallas import tpu as pltpu
from jax.experimental.pallas import tpu_sc as plsc

HEAD_DIM = 64
DIFF_HEADS = 4
DIFF_V_DIM = 2 * HEAD_DIM
SWA_Q_HEADS = 8
SWA_KV_HEADS = 2
SWA_GROUP = SWA_Q_HEADS // SWA_KV_HEADS
WINDOW = 128
ROPE_THETA = 10000.0
N_GROUPS = 4
EXPERTS_PER_GROUP = 8
N_EXPERTS = N_GROUPS * EXPERTS_PER_GROUP
TOP_K = 2
EXPERT_BLOCK = 512
EXPERT_CHUNK = 256
EPS = 1e-6
NEG = -1e30

DIFF_QK_COLS = DIFF_HEADS * 2 * HEAD_DIM
DIFF_V_COLS = DIFF_HEADS * DIFF_V_DIM
SWA_Q_COLS = SWA_Q_HEADS * HEAD_DIM
SWA_KV_COLS = SWA_KV_HEADS * HEAD_DIM
LANES = 128
SUBLANES = 8
BF16_SUBLANES = 16
VMEM_LIMIT_BYTES = 48 * 1024 * 1024
VT_ROWS = DIFF_V_DIM + BF16_SUBLANES
SWA_VT_ROWS = SWA_KV_COLS + BF16_SUBLANES
ROUTER_COLS = LANES
DIFF_UNROLL = 4
DIFF_S_BUFS = 4
DIFF_Q_TILES = 4

BF16 = jnp.bfloat16
F32 = jnp.float32


def _rope_lanes(x, cos_l, sin_l, first_half):
    rot = jnp.where(first_half, pltpu.roll(x, 96, 1), pltpu.roll(x, 32, 1))
    return x * cos_l + rot * sin_l


def _proj_kernel(x_ref, g_ref, wnat_ref, wtr_ref, cost_ref, sint_ref,
                 dqt_ref, dk_ref, dvt_ref, sqt_ref, sk_ref, svt_ref, *, tk):
    x = x_ref[0]
    tm = x.shape[0]
    n1 = x * lax.rsqrt(jnp.mean(x * x, axis=-1, keepdims=True) + EPS) * g_ref[...]
    n1b = n1.astype(BF16)
    nat = jnp.dot(n1b, wnat_ref[...], preferred_element_type=F32)
    tr = lax.dot_general(wtr_ref[...], n1b, (((1,), (1,)), ((), ())),
                         preferred_element_type=F32)

    c32, s32 = cost_ref[...].T, sint_ref[...].T
    cos_l = jnp.concatenate([c32] * (LANES // c32.shape[1]), axis=1)
    sin_l = jnp.concatenate([-s32, s32] * (LANES // HEAD_DIM), axis=1)
    first_half = (lax.broadcasted_iota(jnp.int32, (tm, LANES), 1) & (HEAD_DIM - 1)) < HEAD_DIM // 2
    for h in range(DIFF_HEADS):
        slab = nat[:, h * LANES:(h + 1) * LANES]
        dk_ref[0, h] = _rope_lanes(slab, cos_l, sin_l, first_half).astype(BF16)
    sk = _rope_lanes(nat[:, DIFF_QK_COLS:DIFF_QK_COLS + LANES], cos_l, sin_l, first_half).astype(BF16)
    for c in range(tm // WINDOW):
        sk_ref[0, c] = sk[c * WINDOW:(c + 1) * WINDOW]

    cos_t, sin_t = cost_ref[...], sint_ref[...]
    half = HEAD_DIM // 2

    def rope_rows(r0):
        x1 = tr[r0:r0 + half]
        x2 = tr[r0 + half:r0 + HEAD_DIM]
        return (x1 * cos_t - x2 * sin_t).astype(BF16), (x1 * sin_t + x2 * cos_t).astype(BF16)

    for h in range(DIFF_HEADS):
        for c in range(2):
            lo, hi = rope_rows(h * 2 * HEAD_DIM + c * HEAD_DIM)
            dqt_ref[0, h, c * HEAD_DIM:c * HEAD_DIM + half] = lo
            dqt_ref[0, h, c * HEAD_DIM + half:(c + 1) * HEAD_DIM] = hi
    ones_rows = (lax.broadcasted_iota(jnp.int32, (BF16_SUBLANES, tk), 0) == 0).astype(BF16)
    for h in range(DIFF_HEADS):
        r0 = DIFF_QK_COLS + h * DIFF_V_DIM
        for c in range(tm // tk):
            dvt_ref[0, h, c, :DIFF_V_DIM] = tr[r0:r0 + DIFF_V_DIM, c * tk:(c + 1) * tk].astype(BF16)
            dvt_ref[0, h, c, DIFF_V_DIM:] = ones_rows

    r0 = DIFF_QK_COLS + DIFF_V_COLS
    for h in range(SWA_Q_HEADS):
        lo, hi = rope_rows(r0 + h * HEAD_DIM)
        sqt_ref[0, h * HEAD_DIM:h * HEAD_DIM + half] = lo
        sqt_ref[0, h * HEAD_DIM + half:(h + 1) * HEAD_DIM] = hi
    r0 += SWA_Q_COLS
    for c in range(tm // WINDOW):
        svt_ref[0, c, :SWA_KV_COLS] = tr[r0:r0 + SWA_KV_COLS, c * WINDOW:(c + 1) * WINDOW].astype(BF16)
        svt_ref[0, c, SWA_KV_COLS:] = ones_rows[:, :WINDOW]


def _proj_call(x, g1, w_nat, w_tr, cos_t, sin_t, *, tm, tk):
    B, S, D = x.shape
    nkv = S // tk
    grid = (B, S // tm)
    const = lambda b, i: (0, 0)
    out_shape = (
        jax.ShapeDtypeStruct((B, DIFF_HEADS, 2 * HEAD_DIM, S), BF16),
        jax.ShapeDtypeStruct((B, DIFF_HEADS, S, 2 * HEAD_DIM), BF16),
        jax.ShapeDtypeStruct((B, DIFF_HEADS, nkv, VT_ROWS, tk), BF16),
        jax.ShapeDtypeStruct((B, SWA_Q_COLS, S), BF16),
        jax.ShapeDtypeStruct((B, S // WINDOW, WINDOW, SWA_KV_COLS), BF16),
        jax.ShapeDtypeStruct((B, S // WINDOW, SWA_VT_ROWS, WINDOW), BF16),
    )
    return pl.pallas_call(
        functools.partial(_proj_kernel, tk=tk),
        grid=grid,
        in_specs=[
            pl.BlockSpec((1, tm, D), lambda b, i: (b, i, 0)),
            pl.BlockSpec((1, D), const),
            pl.BlockSpec(w_nat.shape, const),
            pl.BlockSpec(w_tr.shape, const),
            pl.BlockSpec((HEAD_DIM // 2, tm), lambda b, i: (0, i)),
            pl.BlockSpec((HEAD_DIM // 2, tm), lambda b, i: (0, i)),
        ],
        out_specs=(
            pl.BlockSpec((1, DIFF_HEADS, 2 * HEAD_DIM, tm), lambda b, i: (b, 0, 0, i)),
            pl.BlockSpec((1, DIFF_HEADS, tm, 2 * HEAD_DIM), lambda b, i: (b, 0, i, 0)),
            pl.BlockSpec((1, DIFF_HEADS, tm // tk, VT_ROWS, tk), lambda b, i: (b, 0, i, 0, 0)),
            pl.BlockSpec((1, SWA_Q_COLS, tm), lambda b, i: (b, 0, i)),
            pl.BlockSpec((1, tm // WINDOW, WINDOW, SWA_KV_COLS), lambda b, i: (b, i, 0, 0)),
            pl.BlockSpec((1, tm // WINDOW, SWA_VT_ROWS, WINDOW), lambda b, i: (b, i, 0, 0)),
        ),
        out_shape=out_shape,
        compiler_params=pltpu.CompilerParams(
            dimension_semantics=("parallel", "parallel"), vmem_limit_bytes=VMEM_LIMIT_BYTES),
        name="proj_rope",
    )(x, g1, w_nat, w_tr, cos_t, sin_t)


def _diff_kernel(lam_ref, qt_ref, k_ref, vt_ref, g_ref, o_ref, *scratch, tq, tk, lambda_init):
    step = pl.program_id(2)
    s_bufs = scratch[:DIFF_S_BUFS]
    top_bufs = scratch[DIFF_S_BUFS:2 * DIFF_S_BUFS]
    state = scratch[2 * DIFF_S_BUFS:2 * DIFF_S_BUFS + 2 * DIFF_Q_TILES]
    bias_ref = scratch[-1]

    @pl.when(step == 0)
    def _():
        r = lax.broadcasted_iota(jnp.int32, (tk, 2 * tq), 0)
        c = lax.broadcasted_iota(jnp.int32, (tk, 2 * tq), 1) & (tq - 1)
        bias_ref[...] = jnp.where(r <= c, 0.0, NEG).astype(F32)

    lam_p = lam_ref[...]
    lam = (jnp.exp(jnp.sum(lam_p[0:1] * lam_p[1:2], axis=-1, keepdims=True))
           - jnp.exp(jnp.sum(lam_p[2:3] * lam_p[3:4], axis=-1, keepdims=True)) + lambda_init)

    for sub in range(DIFF_Q_TILES):
        _diff_query_tile(step * DIFF_Q_TILES + sub, qt_ref[0, 0, :, sub * tq:(sub + 1) * tq], k_ref, vt_ref,
                         g_ref, o_ref.at[0, pl.ds(sub * tq, tq), :], s_bufs, top_bufs,
                         state[2 * sub], state[2 * sub + 1], bias_ref, lam,
                         tq=tq, tk=tk, lambda_init=lambda_init)


def _diff_query_tile(i, qt, k_ref, vt_ref, g_ref, o_ref, s_bufs, top_bufs, m_ref, acc_ref, bias_ref, lam,
                     *, tq, tk, lambda_init):
    z = jnp.zeros((HEAD_DIM, tq), BF16)
    qw = jnp.concatenate([jnp.concatenate([qt[:HEAD_DIM], z], axis=1),
                          jnp.concatenate([z, qt[HEAD_DIM:]], axis=1)], axis=0)

    def scores(j, par):
        kt = k_ref[0, 0, pl.ds(pl.multiple_of(j * tk, tk), tk), :]
        s = jnp.dot(kt, qw, preferred_element_type=F32)
        s_bufs[par][...] = s
        top_bufs[par][...] = jnp.max(s, axis=0, keepdims=True)

    def absorb(j, par, masked):
        s = s_bufs[par][...]
        if masked:
            s = s + bias_ref[...]
            top = jnp.max(s, axis=0, keepdims=True)
        else:
            top = top_bufs[par][...]
        m = m_ref[...]
        m_new = jnp.maximum(m, top)
        alpha = jnp.exp2(m - m_new)
        p = jnp.exp2(s - m_new).astype(BF16)
        m_ref[...] = m_new
        pv = jnp.dot(vt_ref[0, 0, j], p, preferred_element_type=F32)
        acc_ref[...] = alpha * acc_ref[...] + pv

    m_ref[...] = jnp.full(m_ref.shape, NEG, F32)
    acc_ref[...] = jnp.zeros(acc_ref.shape, F32)

    nfull = (i * tq) // tk
    scores(nfull, 0)
    scores(0, 1)
    absorb(nfull, 0, True)

    def group(t, c):
        j = DIFF_UNROLL * t
        for idx in range(DIFF_UNROLL):
            scores(j + idx + 1, (idx + 2) % DIFF_S_BUFS)
            absorb(j + idx, (idx + 1) % DIFF_S_BUFS, False)
        return c

    lax.fori_loop(0, nfull // DIFF_UNROLL, group, 0)

    for rem in range(1, DIFF_UNROLL):
        @pl.when(nfull % DIFF_UNROLL == rem)
        def _():
            first = nfull - rem
            for idx in range(rem):
                if idx + 1 < rem:
                    scores(first + idx + 1, (idx + 2) % DIFF_S_BUFS)
                absorb(first + idx, (idx + 1) % DIFF_S_BUFS, False)

    inv_l = 1.0 / acc_ref[DIFF_V_DIM:DIFF_V_DIM + 1, :]
    o = (acc_ref[:DIFF_V_DIM, :tq] * inv_l[:, :tq]
         - lam * (acc_ref[:DIFF_V_DIM, tq:] * inv_l[:, tq:]))
    o = o * lax.rsqrt(jnp.mean(o * o, axis=0, keepdims=True) + EPS)
    o_ref[...] = (o.T * g_ref[...] * (1.0 - lambda_init)).astype(BF16)


def _diff_call(lam_p, dqt, dk, dvt, subln_g, *, tq, tk, lambda_init):
    B, H, _, S = dqt.shape
    assert tk == tq and S % tk == 0, "the diagonal tile's causal pattern is built for square tiles"
    nkv = S // tk
    tq_step = DIFF_Q_TILES * tq
    assert S % tq_step == 0
    grid = (B, H, S // tq_step)
    return pl.pallas_call(
        functools.partial(_diff_kernel, tq=tq, tk=tk, lambda_init=lambda_init),
        grid=grid,
        in_specs=[
            pl.BlockSpec(lam_p.shape, lambda b, h, i: (0, 0)),
            pl.BlockSpec((1, 1, 2 * HEAD_DIM, tq_step), lambda b, h, i: (b, h, 0, i)),
            pl.BlockSpec((1, 1, S, 2 * HEAD_DIM), lambda b, h, i: (b, h, 0, 0)),
            pl.BlockSpec((1, 1, nkv, VT_ROWS, tk), lambda b, h, i: (b, h, 0, 0, 0)),
            pl.BlockSpec((1, DIFF_V_DIM), lambda b, h, i: (0, 0)),
        ],
        out_specs=pl.BlockSpec((1, tq_step, DIFF_V_DIM), lambda b, h, i: (b, i, h)),
        out_shape=jax.ShapeDtypeStruct((B, S, DIFF_V_COLS), BF16),
        scratch_shapes=[pltpu.VMEM((tk, 2 * tq), F32)] * DIFF_S_BUFS + [
            pltpu.VMEM((1, 2 * tq), F32)] * DIFF_S_BUFS + [
            pltpu.VMEM((1, 2 * tq), F32),
            pltpu.VMEM((VT_ROWS, 2 * tq), F32)] * DIFF_Q_TILES + [
            pltpu.VMEM((tk, 2 * tq), F32),
        ],
        compiler_params=pltpu.CompilerParams(
            dimension_semantics=("parallel", "parallel", "arbitrary"),
            vmem_limit_bytes=VMEM_LIMIT_BYTES),
        name="diff_attn",
    )(lam_p, dqt, dk, dvt, subln_g)


def _swa_kernel(sink_ref, qt_ref, k_ref, vt_ref, o_ref, *, tq):
    i = pl.program_id(1)
    n_cols = SWA_Q_HEADS * WINDOW
    half_cols = n_cols // SWA_KV_HEADS
    sink = sink_ref[...]
    row = lax.broadcasted_iota(jnp.int32, (2 * WINDOW, WINDOW), 0)
    qrel = lax.broadcasted_iota(jnp.int32, (2 * WINDOW, WINDOW), 1)
    band = (row - WINDOW <= qrel) & (row > qrel)
    in_current = row >= WINDOW
    z = jnp.zeros((HEAD_DIM, half_cols), BF16)
    for sub in range(tq // WINDOW):
        n = i * (tq // WINDOW) + sub
        prev = jnp.maximum(n - 1, 0)
        kwin = jnp.concatenate([k_ref[0, prev], k_ref[0, n]], axis=0)
        vtwin = jnp.concatenate([vt_ref[0, prev], vt_ref[0, n]], axis=1)
        qt = qt_ref[0, :, sub * WINDOW:(sub + 1) * WINDOW]
        heads = [qt[h * HEAD_DIM:(h + 1) * HEAD_DIM] for h in range(SWA_Q_HEADS)]
        qw = jnp.concatenate(
            [jnp.concatenate(heads[:SWA_GROUP] + [z], axis=1),
             jnp.concatenate([z] + heads[SWA_GROUP:], axis=1)], axis=0)
        s = jnp.dot(kwin, qw, preferred_element_type=F32)
        valid = band & (in_current | (n >= 1))
        s = jnp.concatenate(
            [jnp.where(valid, s[:, h * WINDOW:(h + 1) * WINDOW], NEG) for h in range(SWA_Q_HEADS)], axis=1)
        m = jnp.maximum(jnp.max(s, axis=0, keepdims=True), sink)
        p = jnp.exp2(s - m).astype(BF16)
        acc = jnp.dot(vtwin, p, preferred_element_type=F32)
        den = acc[SWA_KV_COLS:SWA_KV_COLS + 1] + jnp.exp2(sink - m)
        on = acc[:SWA_KV_COLS] / den
        u = jnp.concatenate([on[:HEAD_DIM, :half_cols], on[HEAD_DIM:, half_cols:]], axis=1)
        for hp in range(SWA_Q_HEADS // 2):
            two = jnp.concatenate([u[:, (2 * hp) * WINDOW:(2 * hp + 1) * WINDOW],
                                   u[:, (2 * hp + 1) * WINDOW:(2 * hp + 2) * WINDOW]], axis=0)
            o_ref[0, sub * WINDOW:(sub + 1) * WINDOW, hp * LANES:(hp + 1) * LANES] = two.T.astype(BF16)


def _swa_call(sink_row, sqt, sk, svt, *, tq):
    B, _, S = sqt.shape
    nb = S // WINDOW
    return pl.pallas_call(
        functools.partial(_swa_kernel, tq=tq),
        grid=(B, S // tq),
        in_specs=[
            pl.BlockSpec(sink_row.shape, lambda b, i: (0, 0)),
            pl.BlockSpec((1, SWA_Q_COLS, tq), lambda b, i: (b, 0, i)),
            pl.BlockSpec((1, nb, WINDOW, SWA_KV_COLS), lambda b, i: (b, 0, 0, 0)),
            pl.BlockSpec((1, nb, SWA_VT_ROWS, WINDOW), lambda b, i: (b, 0, 0, 0)),
        ],
        out_specs=pl.BlockSpec((1, tq, SWA_Q_COLS), lambda b, i: (b, i, 0)),
        out_shape=jax.ShapeDtypeStruct((B, S, SWA_Q_COLS), BF16),
        compiler_params=pltpu.CompilerParams(
            dimension_semantics=("parallel", "arbitrary"), vmem_limit_bytes=VMEM_LIMIT_BYTES),
        name="swa_attn",
    )(sink_row, sqt, sk, svt)


def _pack_bf16_pairs(x):
    n = x.shape[1] // 2
    lo = lax.bitcast_convert_type(x[:, :n].astype(BF16).astype(F32), jnp.uint32)
    hi = lax.bitcast_convert_type(x[:, n:].astype(BF16).astype(F32), jnp.uint32)
    return (lo >> 16) | (hi & jnp.uint32(0xFFFF0000))


def _unpack_bf16_pairs(w):
    lo = lax.bitcast_convert_type(w << 16, F32)
    hi = lax.bitcast_convert_type(w & jnp.uint32(0xFFFF0000), F32)
    return jnp.concatenate([lo, hi], axis=1).astype(BF16)


def _mix_kernel(x_ref, od_ref, os_ref, wo_ref, g2_ref, wr_ref, br_ref, x1_ref, n2_ref, rt_ref, cnt_ref):
    tm = x_ref.shape[1]
    lane = lax.broadcasted_iota(jnp.int32, (tm, ROUTER_COLS), 1)
    lane_f = lane.astype(F32)
    big = float(ROUTER_COLS)
    mixed = jnp.concatenate([od_ref[0], os_ref[0]], axis=1)
    h = x_ref[0] + jnp.dot(mixed, wo_ref[...], preferred_element_type=F32)
    x1_ref[0] = h
    n2 = h * lax.rsqrt(jnp.mean(h * h, axis=-1, keepdims=True) + EPS) * g2_ref[...]
    n2_ref[0] = _pack_bf16_pairs(n2)
    n2_hi = n2.astype(BF16)
    n2_lo = (n2 - n2_hi.astype(F32)).astype(BF16)
    parts = jnp.dot(jnp.concatenate([n2_hi, n2_lo], axis=0), wr_ref[...],
                    preferred_element_type=F32)
    logits = ((parts[:tm, :ROUTER_COLS] + parts[tm:, ROUTER_COLS:])
              + (parts[:tm, ROUTER_COLS:] + parts[tm:, :ROUTER_COLS])) + br_ref[...]
    gl = jnp.where(lane < N_GROUPS, logits, -jnp.inf)
    gm = jnp.max(gl, axis=-1, keepdims=True)
    p_top = 1.0 / jnp.sum(jnp.exp(gl - gm), axis=-1, keepdims=True)
    g_idx = jnp.min(jnp.where(gl == gm, lane_f, big), axis=-1, keepdims=True)
    e_lo = N_GROUPS + EXPERTS_PER_GROUP * g_idx
    el = jnp.where((lane_f >= e_lo) & (lane_f < e_lo + EXPERTS_PER_GROUP), logits, -jnp.inf)
    v1 = jnp.max(el, axis=-1, keepdims=True)
    i1 = jnp.min(jnp.where(el == v1, lane_f, big), axis=-1, keepdims=True)
    el2 = jnp.where(lane_f == i1, -jnp.inf, el)
    v2 = jnp.max(el2, axis=-1, keepdims=True)
    i2 = jnp.min(jnp.where(el2 == v2, lane_f, big), axis=-1, keepdims=True)
    e21 = jnp.exp(v2 - v1)
    gate1 = p_top / (1.0 + e21)
    gate2 = p_top * e21 / (1.0 + e21)
    rt_ref[0] = jnp.where(lane == 0, i1 - N_GROUPS,
                jnp.where(lane == 1, i2 - N_GROUPS,
                jnp.where(lane == 2, gate1, jnp.where(lane == 3, gate2, 0.0))))
    chosen = ((lane_f == i1 - N_GROUPS) | (lane_f == i2 - N_GROUPS)).astype(F32)
    cnt_ref[0, 0] = jnp.broadcast_to(jnp.sum(chosen, axis=0, keepdims=True), cnt_ref.shape[2:])


def _mix_call(x, o_diff, o_swa, w_out, g2, w_router, b_router, *, tm):
    B, S, D = x.shape
    const = lambda b, i: (0, 0)
    row = lambda b, i: (b, i, 0)
    nt = S // tm
    return pl.pallas_call(
        _mix_kernel,
        grid=(B, nt),
        in_specs=[
            pl.BlockSpec((1, tm, D), row),
            pl.BlockSpec((1, tm, DIFF_V_COLS), row),
            pl.BlockSpec((1, tm, SWA_Q_COLS), row),
            pl.BlockSpec(w_out.shape, const),
            pl.BlockSpec((1, D), const),
            pl.BlockSpec(w_router.shape, const),
            pl.BlockSpec((1, ROUTER_COLS), const),
        ],
        out_specs=(pl.BlockSpec((1, tm, D), row), pl.BlockSpec((1, tm, D // 2), row),
                   pl.BlockSpec((1, tm, ROUTER_COLS), row),
                   pl.BlockSpec((1, 1, SUBLANES, ROUTER_COLS), lambda b, i: (b, i, 0, 0))),
        out_shape=(jax.ShapeDtypeStruct((B, S, D), F32), jax.ShapeDtypeStruct((B, S, D // 2), jnp.uint32),
                   jax.ShapeDtypeStruct((B, S, ROUTER_COLS), F32),
                   jax.ShapeDtypeStruct((B, nt, SUBLANES, ROUTER_COLS), F32)),
        compiler_params=pltpu.CompilerParams(
            dimension_semantics=("parallel", "parallel"), vmem_limit_bytes=VMEM_LIMIT_BYTES),
        name="outproj_router",
    )(x, o_diff, o_swa, w_out, g2, w_router, b_router)


def _slot_kernel(rt_ref, base_ref, dest_ref, *, tm):
    rt_t = rt_ref[...].T
    e1 = rt_t[0:1].astype(jnp.int32)
    e2 = rt_t[1:2].astype(jnp.int32)
    eid = lax.broadcasted_iota(jnp.int32, (N_EXPERTS, tm), 0)
    oh1 = eid == e1
    oh2 = eid == e2
    earlier = (lax.broadcasted_iota(jnp.int32, (tm, tm), 0)
               < lax.broadcasted_iota(jnp.int32, (tm, tm), 1)).astype(BF16)
    before = jnp.dot((oh1 | oh2).astype(BF16), earlier, preferred_element_type=F32)
    slot = before + base_ref[0][:, 0:1]
    d1 = jnp.sum(jnp.where(oh1, slot, 0.0), axis=0, keepdims=True).astype(jnp.int32)
    d2 = jnp.sum(jnp.where(oh2, slot, 0.0), axis=0, keepdims=True).astype(jnp.int32)
    dest_ref[0] = jnp.concatenate([d1, d2, jnp.zeros((SUBLANES - TOP_K, tm), jnp.int32)], axis=0)


def _slot_call(rt, tile_base, *, tm):
    nt = rt.shape[0] // tm
    return pl.pallas_call(
        functools.partial(_slot_kernel, tm=tm),
        grid=(nt,),
        in_specs=[
            pl.BlockSpec((tm, ROUTER_COLS), lambda t: (t, 0)),
            pl.BlockSpec((1, N_EXPERTS, LANES), lambda t: (t, 0, 0)),
        ],
        out_specs=pl.BlockSpec((1, SUBLANES, tm), lambda t: (t, 0, 0)),
        out_shape=jax.ShapeDtypeStruct((nt, SUBLANES, tm), jnp.int32),
        compiler_params=pltpu.CompilerParams(dimension_semantics=("parallel",)),
        name="moe_slots",
    )(rt, tile_base)


SC_ROW_CHUNK = 64


def _sc_workers():
    info = plsc.get_sparse_core_info()
    return info.num_cores, info.num_cores * info.num_subcores


def _sc_scatter_rows(rows, idx, n_out):
    n, width = rows.shape
    n_cores, n_workers = _sc_workers()
    n_chunks = n // SC_ROW_CHUNK
    per_worker = n_chunks // n_workers
    assert n_chunks % (2 * n_workers) == 0
    mesh = plsc.VectorSubcoreMesh(core_axis_name="c", subcore_axis_name="s")

    @functools.partial(
        pl.kernel, mesh=mesh,
        out_type=jax.ShapeDtypeStruct((n_out, width), rows.dtype),
        scratch_types=[pltpu.VMEM((SC_ROW_CHUNK,), jnp.int32)] * (2 * TOP_K)
        + [pltpu.VMEM((SC_ROW_CHUNK, width), rows.dtype)] * 2
        + [pltpu.SemaphoreType.DMA] * (2 + 2 * TOP_K),
    )
    def scatter(rows_hbm, idx_hbm, out_hbm, *scratch):
        idx_v = scratch[:2 * TOP_K]
        rows_v = scratch[2 * TOP_K:2 * TOP_K + 2]
        load_sems, store_sems = scratch[-(2 + 2 * TOP_K):-2 * TOP_K], scratch[-2 * TOP_K:]
        worker = lax.axis_index("s") * n_cores + lax.axis_index("c")

        @pl.loop(0, per_worker, step=2)
        def _(i):
            loads = []
            for half in range(2):
                c = worker * per_worker + i + half
                src = rows_hbm.at[pl.ds(pl.multiple_of(c * SC_ROW_CHUNK, SC_ROW_CHUNK), SC_ROW_CHUNK)]
                loads.append(pltpu.async_copy(src, rows_v[half], load_sems[half]))
                for k in range(TOP_K):
                    pltpu.sync_copy(idx_hbm.at[k, c], idx_v[half * TOP_K + k])
            stores = []
            for half in range(2):
                loads[half].wait()
                for k in range(TOP_K):
                    j = half * TOP_K + k
                    stores.append(pltpu.async_copy(rows_v[half], out_hbm.at[idx_v[j]], store_sems[j]))
            for store in stores:
                store.wait()

    return scatter(rows, idx)


def _expert_kernel(be_ref, nvalid_ref, next_ref, xs_ref, wg_hbm, wu_hbm, wd_hbm, y_ref,
                   wg_st, wu_st, wd_st, wg_b, wu_b, wd_b, slot_ref, sems):
    b = pl.program_id(0)
    n_valid = nvalid_ref[b]
    sources, staged, cast = (wg_hbm, wu_hbm, wd_hbm), (wg_st, wu_st, wd_st), (wg_b, wu_b, wd_b)

    def weight_copies(expert, slot):
        return [pltpu.make_async_copy(src.at[expert], dst.at[slot], sems.at[slot, i])
                for i, (src, dst) in enumerate(zip(sources, staged))]

    @pl.when(n_valid > 0)
    def _():
        @pl.when((b == 0) | (be_ref[b] != be_ref[jnp.maximum(b - 1, 0)]))
        def _():
            @pl.when(b == 0)
            def _():
                slot_ref[0] = 0
                for copy in weight_copies(be_ref[0], 0):
                    copy.start()

            slot = slot_ref[0]
            for copy in weight_copies(be_ref[b], slot):
                copy.wait()
            for dst, src in zip(cast, staged):
                dst[...] = src[slot].astype(BF16)

            @pl.when(next_ref[b] >= 0)
            def _():
                for copy in weight_copies(next_ref[b], 1 - slot):
                    copy.start()

            slot_ref[0] = 1 - slot

        for c in range(EXPERT_BLOCK // EXPERT_CHUNK):
            rows = pl.ds(c * EXPERT_CHUNK, EXPERT_CHUNK)
            row_id = c * EXPERT_CHUNK + lax.broadcasted_iota(jnp.int32, (EXPERT_CHUNK, xs_ref.shape[1]), 0)
            packed = jnp.where(row_id < n_valid, xs_ref[rows, :], jnp.uint32(0))
            xb = _unpack_bf16_pairs(packed)
            gate = jnp.dot(xb, wg_b[...], preferred_element_type=F32)
            up = jnp.dot(xb, wu_b[...], preferred_element_type=F32)
            hid = (gate * jax.nn.sigmoid(gate) * up).astype(BF16)
            y_ref[rows, :] = _pack_bf16_pairs(jnp.dot(hid, wd_b[...], preferred_element_type=F32))

    @pl.when(n_valid == 0)
    def _():
        y_ref[...] = jnp.zeros_like(y_ref)


def _expert_call(block_expert, n_valid, next_expert, xs, w_gate, w_up, w_down):
    P = xs.shape[0]
    NB = P // EXPERT_BLOCK
    E, D, F = w_gate.shape
    grid_spec = pltpu.PrefetchScalarGridSpec(
        num_scalar_prefetch=3,
        grid=(NB,),
        in_specs=[
            pl.BlockSpec((EXPERT_BLOCK,) + xs.shape[1:], lambda b, *_: (b, 0)),
            pl.BlockSpec(memory_space=pl.ANY),
            pl.BlockSpec(memory_space=pl.ANY),
            pl.BlockSpec(memory_space=pl.ANY),
        ],
        out_specs=pl.BlockSpec((EXPERT_BLOCK, D // 2), lambda b, *_: (b, 0)),
        scratch_shapes=[
            pltpu.VMEM((2, D, F), F32),
            pltpu.VMEM((2, D, F), F32),
            pltpu.VMEM((2, F, D), F32),
            pltpu.VMEM((D, F), BF16),
            pltpu.VMEM((D, F), BF16),
            pltpu.VMEM((F, D), BF16),
            pltpu.SMEM((1,), jnp.int32),
            pltpu.SemaphoreType.DMA((2, 3)),
        ],
    )
    return pl.pallas_call(
        _expert_kernel,
        grid_spec=grid_spec,
        out_shape=jax.ShapeDtypeStruct((P, D // 2), jnp.uint32),
        compiler_params=pltpu.CompilerParams(
            dimension_semantics=("arbitrary",), vmem_limit_bytes=VMEM_LIMIT_BYTES),
        name="moe_experts",
    )(block_expert, n_valid, next_expert, xs, w_gate, w_up, w_down)


def _sc_gather_rows(table, idx):
    n_rows, width = idx.shape[0], table.shape[1]
    n_cores, n_workers = _sc_workers()
    per_worker = n_rows // n_workers
    assert n_rows % (n_workers * 2 * SC_ROW_CHUNK) == 0
    mesh = plsc.VectorSubcoreMesh(core_axis_name="c", subcore_axis_name="s")

    @functools.partial(
        pl.kernel, mesh=mesh,
        out_type=jax.ShapeDtypeStruct((n_rows, width), table.dtype),
        scratch_types=[pltpu.VMEM((SC_ROW_CHUNK,), jnp.int32)] * 2
        + [pltpu.VMEM((SC_ROW_CHUNK, width), table.dtype)] * 2
        + [pltpu.SemaphoreType.DMA] * 4,
    )
    def gather(table_hbm, idx_hbm, out_hbm, idx_a, idx_b, rows_a, rows_b, sem_ga, sem_gb, sem_wa, sem_wb):
        worker = lax.axis_index("s") * n_cores + lax.axis_index("c")
        base = worker * per_worker

        @pl.loop(0, per_worker // SC_ROW_CHUNK, step=2)
        def _(c):
            off_a = pl.multiple_of(base + c * SC_ROW_CHUNK, SC_ROW_CHUNK)
            off_b = pl.multiple_of(off_a + SC_ROW_CHUNK, SC_ROW_CHUNK)
            pltpu.sync_copy(idx_hbm.at[pl.ds(off_a, SC_ROW_CHUNK)], idx_a)
            pltpu.sync_copy(idx_hbm.at[pl.ds(off_b, SC_ROW_CHUNK)], idx_b)
            gather_a = pltpu.async_copy(table_hbm.at[idx_a], rows_a, sem_ga)
            gather_b = pltpu.async_copy(table_hbm.at[idx_b], rows_b, sem_gb)
            gather_a.wait()
            write_a = pltpu.async_copy(rows_a, out_hbm.at[pl.ds(off_a, SC_ROW_CHUNK)], sem_wa)
            gather_b.wait()
            write_b = pltpu.async_copy(rows_b, out_hbm.at[pl.ds(off_b, SC_ROW_CHUNK)], sem_wb)
            write_a.wait()
            write_b.wait()

    return gather(table, idx)


def _combine_kernel(x1_ref, rt_ref, y_ref, fg_ref, o_ref, *, final_norm):
    rt = rt_ref[...]
    y1 = _unpack_bf16_pairs(y_ref[0, 0]).astype(F32)
    y2 = _unpack_bf16_pairs(y_ref[0, 1]).astype(F32)
    h = x1_ref[...] + rt[:, 2:3] * y1 + rt[:, 3:4] * y2
    if final_norm:
        h = h * lax.rsqrt(jnp.mean(h * h, axis=-1, keepdims=True) + EPS) * fg_ref[...]
    o_ref[...] = h


def _combine_call(x1, rt, ysg, final_g, *, tm, final_norm):
    T, D = x1.shape
    return pl.pallas_call(
        functools.partial(_combine_kernel, final_norm=final_norm),
        grid=(T // tm,),
        in_specs=[
            pl.BlockSpec((tm, D), lambda t: (t, 0)),
            pl.BlockSpec((tm, ROUTER_COLS), lambda t: (t, 0)),
            pl.BlockSpec((1, TOP_K, tm, D // 2), lambda t: (t, 0, 0, 0)),
            pl.BlockSpec((1, D), lambda t: (0, 0)),
        ],
        out_specs=pl.BlockSpec((tm, D), lambda t: (t, 0)),
        out_shape=jax.ShapeDtypeStruct((T, D), F32),
        compiler_params=pltpu.CompilerParams(
            dimension_semantics=("parallel",), vmem_limit_bytes=VMEM_LIMIT_BYTES),
        name="moe_combine",
    )(x1, rt, ysg, final_g)


def _slot_layout(tile_counts, n_assign):
    NB = -(-n_assign // EXPERT_BLOCK) + N_EXPERTS
    n_tiles = tile_counts.shape[0]
    tc = tile_counts.astype(F32)
    hp = lax.Precision.HIGHEST
    counts = jnp.sum(tc, axis=0)
    padded = jnp.ceil(counts / EXPERT_BLOCK) * EXPERT_BLOCK
    upper = (jnp.arange(N_EXPERTS)[:, None] < jnp.arange(N_EXPERTS)[None, :]).astype(F32)
    pad_start = jnp.dot(padded, upper, precision=hp)
    pad_end = pad_start + padded
    lower = (jnp.arange(n_tiles)[:, None] > jnp.arange(n_tiles)[None, :]).astype(F32)
    tile_base = pad_start[None, :] + jnp.dot(lower, tc, precision=hp)
    block_start = jnp.arange(NB, dtype=F32) * EXPERT_BLOCK
    block_expert = jnp.minimum(jnp.sum((pad_end[None, :] <= block_start[:, None]).astype(jnp.int32), axis=1),
                               N_EXPERTS - 1)
    mine = block_expert[:, None] == jnp.arange(N_EXPERTS)[None, :]
    run_end = jnp.sum(jnp.where(mine, (pad_start + counts)[None, :], 0.0), axis=1)
    n_valid = jnp.clip(run_end - block_start, 0, EXPERT_BLOCK).astype(jnp.int32)
    eid = jnp.arange(N_EXPERTS)
    later_nonempty = (eid[None, :] > eid[:, None]) & (counts[None, :] > 0)
    next_nonempty = jnp.min(jnp.where(later_nonempty, eid[None, :], N_EXPERTS), axis=1)
    next_nonempty = jnp.where(next_nonempty < N_EXPERTS, next_nonempty, -1)
    next_expert = jnp.sum(jnp.where(mine, next_nonempty[None, :], 0), axis=1).astype(jnp.int32)
    return NB, block_expert.astype(jnp.int32), n_valid, next_expert, tile_base


def _rope_tables(S):
    inv = 1.0 / (ROPE_THETA ** (jnp.arange(0, HEAD_DIM, 2, dtype=F32) / HEAD_DIM))
    ang_t = inv[:, None] * jnp.arange(S, dtype=F32)[None, :]
    return jnp.cos(ang_t), jnp.sin(ang_t)


def _tiles(S):
    tile = min(512, S)
    return tile, tile, min(1024, S), min(1024, S), min(1024, S)


def kernel(x, norm1_g, w_in, lambda_q1, lambda_k1, lambda_q2, lambda_k2, subln_g, sinks, w_out,
           norm2_g, w_router_group, b_router_group, w_router_expert, b_router_expert,
           w_gate, w_up, w_down, final_g):
    B, S, D = x.shape
    T = B * S
    depth = w_in.shape[0]
    tq, tk, tm_proj, tm_tok, tq_swa = _tiles(S)
    qscale = HEAD_DIM ** -0.5 * math.log2(math.e)
    cos_t, sin_t = _rope_tables(S)

    c0 = DIFF_QK_COLS
    c1 = 2 * DIFF_QK_COLS
    c2 = c1 + DIFF_V_COLS
    c3 = c2 + SWA_Q_COLS
    c4 = c3 + SWA_KV_COLS
    for l in range(depth):
        lambda_init = 0.8 - 0.6 * math.exp(-0.3 * l)
        w = w_in[l]
        w_nat = jnp.concatenate([w[:, c0:c1], w[:, c3:c4]], axis=1).astype(BF16)
        w_tr = jnp.concatenate([w[:, :c0] * qscale, w[:, c1:c2], w[:, c2:c3] * qscale, w[:, c4:]],
                               axis=1).astype(BF16).T
        dqt, dk, dvt, sqt, sk, svt = _proj_call(
            x, norm1_g[l][None, :], w_nat, w_tr, cos_t, sin_t, tm=tm_proj, tk=tk)

        lam_p = jnp.stack([lambda_q1[l], lambda_k1[l], lambda_q2[l], lambda_k2[l]]).astype(F32)
        o_diff = _diff_call(lam_p, dqt, dk, dvt, subln_g[l][None, :].astype(F32),
                            tq=tq, tk=tk, lambda_init=lambda_init)
        sink_row = jnp.repeat(sinks[l].astype(F32) * math.log2(math.e), WINDOW)[None, :]
        o_swa = _swa_call(sink_row, sqt, sk, svt, tq=tq_swa)

        wo_b = w_out[l].astype(BF16)
        w_router = jnp.zeros((D, ROUTER_COLS), F32)
        w_router = w_router.at[:, :N_GROUPS].set(w_router_group[l])
        w_router = w_router.at[:, N_GROUPS:N_GROUPS + N_EXPERTS].set(w_router_expert[l])
        w_router_hi = w_router.astype(BF16)
        w_router_lo = (w_router - w_router_hi.astype(F32)).astype(BF16)
        w_router = jnp.concatenate([w_router_hi, w_router_lo], axis=1)
        b_router = jnp.zeros((1, ROUTER_COLS), F32)
        b_router = b_router.at[0, :N_GROUPS].set(b_router_group[l])
        b_router = b_router.at[0, N_GROUPS:N_GROUPS + N_EXPERTS].set(b_router_expert[l])
        x1, n2p, rt, cnt = _mix_call(x, o_diff, o_swa, wo_b, norm2_g[l][None, :], w_router, b_router, tm=tm_tok)

        rt2 = rt.reshape(T, ROUTER_COLS)
        tile_counts = cnt[:, :, 0, :N_EXPERTS].reshape(T // tm_tok, N_EXPERTS).astype(jnp.int32)
        NB, block_expert, n_valid, next_expert, tile_base = _slot_layout(tile_counts, T * TOP_K)
        tile_base = jnp.broadcast_to(tile_base.astype(F32)[:, :, None], (T // tm_tok, N_EXPERTS, LANES))
        dest = _slot_call(rt2, tile_base, tm=tm_tok)
        scatter_idx = jnp.swapaxes(dest[:, :TOP_K, :], 0, 1).reshape(TOP_K, T // SC_ROW_CHUNK, SC_ROW_CHUNK)
        xs = _sc_scatter_rows(n2p.reshape(T, D // 2), scatter_idx, NB * EXPERT_BLOCK)
        ys = _expert_call(block_expert, n_valid, next_expert, xs, w_gate[l], w_up[l], w_down[l])
        ysg = _sc_gather_rows(ys, dest[:, :TOP_K, :].reshape(T * TOP_K))
        x = _combine_call(x1.reshape(T, D), rt2, ysg.reshape(T // tm_tok, TOP_K, tm_tok, D // 2),
                          final_g[None, :], tm=tm_tok, final_norm=(l == depth - 1)).reshape(B, S, D)
    return x
```

```python
import functools
import math

import jax
import jax.numpy as jnp
from jax import lax
from jax.experimental import pallas as pl
from jax.experimental.pallas import tpu as pltpu
from jax.experimental.pallas import tpu_sc as plsc

HEAD_DIM = 64
DIFF_HEADS = 4
DIFF_V_DIM = 2 * HEAD_DIM
SWA_Q_HEADS = 8
SWA_KV_HEADS = 2
SWA_GROUP = SWA_Q_HEADS // SWA_KV_HEADS
WINDOW = 128
ROPE_THETA = 10000.0
N_GROUPS = 4
EXPERTS_PER_GROUP = 8
N_EXPERTS = N_GROUPS * EXPERTS_PER_GROUP
TOP_K = 2
EXPERT_BLOCK = 512
EXPERT_CHUNK = 256
EPS = 1e-6
NEG = -1e30

DIFF_QK_COLS = DIFF_HEADS * 2 * HEAD_DIM
DIFF_V_COLS = DIFF_HEADS * DIFF_V_DIM
SWA_Q_COLS = SWA_Q_HEADS * HEAD_DIM
SWA_KV_COLS = SWA_KV_HEADS * HEAD_DIM
LANES = 128
SUBLANES = 8
BF16_SUBLANES = 16
VMEM_LIMIT_BYTES = 48 * 1024 * 1024
VT_ROWS = DIFF_V_DIM + BF16_SUBLANES
SWA_VT_ROWS = SWA_KV_COLS + BF16_SUBLANES
ROUTER_COLS = LANES
DIFF_UNROLL = 4
DIFF_S_BUFS = 4
DIFF_Q_TILES = 8

BF16 = jnp.bfloat16
F32 = jnp.float32


def _rope_lanes(x, cos_l, sin_l, first_half):
    rot = jnp.where(first_half, pltpu.roll(x, 96, 1), pltpu.roll(x, 32, 1))
    return x * cos_l + rot * sin_l


def _proj_kernel(x_ref, g_ref, wnat_ref, wtr_ref, cost_ref, sint_ref,
                 dqt_ref, dk_ref, dvt_ref, sqt_ref, sk_ref, svt_ref, *, tk):
    x = x_ref[0]
    tm = x.shape[0]
    n1 = x * lax.rsqrt(jnp.mean(x * x, axis=-1, keepdims=True) + EPS) * g_ref[...]
    n1b = n1.astype(BF16)
    nat = jnp.dot(n1b, wnat_ref[...], preferred_element_type=F32)
    tr = lax.dot_general(wtr_ref[...], n1b, (((1,), (1,)), ((), ())),
                         preferred_element_type=F32)

    c32, s32 = cost_ref[...].T, sint_ref[...].T
    cos_l = jnp.concatenate([c32] * (LANES // c32.shape[1]), axis=1)
    sin_l = jnp.concatenate([-s32, s32] * (LANES // HEAD_DIM), axis=1)
    first_half = (lax.broadcasted_iota(jnp.int32, (tm, LANES), 1) & (HEAD_DIM - 1)) < HEAD_DIM // 2
    for h in range(DIFF_HEADS):
        slab = nat[:, h * LANES:(h + 1) * LANES]
        dk_ref[0, h] = _rope_lanes(slab, cos_l, sin_l, first_half).astype(BF16)
    sk = _rope_lanes(nat[:, DIFF_QK_COLS:DIFF_QK_COLS + LANES], cos_l, sin_l, first_half).astype(BF16)
    for c in range(tm // WINDOW):
        sk_ref[0, c] = sk[c * WINDOW:(c + 1) * WINDOW]

    cos_t, sin_t = cost_ref[...], sint_ref[...]
    half = HEAD_DIM // 2

    def rope_rows(r0):
        x1 = tr[r0:r0 + half]
        x2 = tr[r0 + half:r0 + HEAD_DIM]
        return (x1 * cos_t - x2 * sin_t).astype(BF16), (x1 * sin_t + x2 * cos_t).astype(BF16)

    for h in range(DIFF_HEADS):
        for c in range(2):
            lo, hi = rope_rows(h * 2 * HEAD_DIM + c * HEAD_DIM)
            dqt_ref[0, h, c * HEAD_DIM:c * HEAD_DIM + half] = lo
            dqt_ref[0, h, c * HEAD_DIM + half:(c + 1) * HEAD_DIM] = hi
    ones_rows = (lax.broadcasted_iota(jnp.int32, (BF16_SUBLANES, tk), 0) == 0).astype(BF16)
    for h in range(DIFF_HEADS):
        r0 = DIFF_QK_COLS + h * DIFF_V_DIM
        for c in range(tm // tk):
            dvt_ref[0, h, c, :DIFF_V_DIM] = tr[r0:r0 + DIFF_V_DIM, c * tk:(c + 1) * tk].astype(BF16)
            dvt_ref[0, h, c, DIFF_V_DIM:] = ones_rows

    r0 = DIFF_QK_COLS + DIFF_V_COLS
    for h in range(SWA_Q_HEADS):
        lo, hi = rope_rows(r0 + h * HEAD_DIM)
        sqt_ref[0, h * HEAD_DIM:h * HEAD_DIM + half] = lo
        sqt_ref[0, h * HEAD_DIM + half:(h + 1) * HEAD_DIM] = hi
    r0 += SWA_Q_COLS
    for c in range(tm // WINDOW):
        svt_ref[0, c, :SWA_KV_COLS] = tr[r0:r0 + SWA_KV_COLS, c * WINDOW:(c + 1) * WINDOW].astype(BF16)
        svt_ref[0, c, SWA_KV_COLS:] = ones_rows[:, :WINDOW]


def _proj_call(x, g1, w_nat, w_tr, cos_t, sin_t, *, tm, tk):
    B, S, D = x.shape
    nkv = S // tk
    grid = (B, S // tm)
    const = lambda b, i: (0, 0)
    out_shape = (
        jax.ShapeDtypeStruct((B, DIFF_HEADS, 2 * HEAD_DIM, S), BF16),
        jax.ShapeDtypeStruct((B, DIFF_HEADS, S, 2 * HEAD_DIM), BF16),
        jax.ShapeDtypeStruct((B, DIFF_HEADS, nkv, VT_ROWS, tk), BF16),
        jax.ShapeDtypeStruct((B, SWA_Q_COLS, S), BF16),
        jax.ShapeDtypeStruct((B, S // WINDOW, WINDOW, SWA_KV_COLS), BF16),
        jax.ShapeDtypeStruct((B, S // WINDOW, SWA_VT_ROWS, WINDOW), BF16),
    )
    return pl.pallas_call(
        functools.partial(_proj_kernel, tk=tk),
        grid=grid,
        in_specs=[
            pl.BlockSpec((1, tm, D), lambda b, i: (b, i, 0)),
            pl.BlockSpec((1, D), const),
            pl.BlockSpec(w_nat.shape, const),
            pl.BlockSpec(w_tr.shape, const),
            pl.BlockSpec((HEAD_DIM // 2, tm), lambda b, i: (0, i)),
            pl.BlockSpec((HEAD_DIM // 2, tm), lambda b, i: (0, i)),
        ],
        out_specs=(
            pl.BlockSpec((1, DIFF_HEADS, 2 * HEAD_DIM, tm), lambda b, i: (b, 0, 0, i)),
            pl.BlockSpec((1, DIFF_HEADS, tm, 2 * HEAD_DIM), lambda b, i: (b, 0, i, 0)),
            pl.BlockSpec((1, DIFF_HEADS, tm // tk, VT_ROWS, tk), lambda b, i: (b, 0, i, 0, 0)),
            pl.BlockSpec((1, SWA_Q_COLS, tm), lambda b, i: (b, 0, i)),
            pl.BlockSpec((1, tm // WINDOW, WINDOW, SWA_KV_COLS), lambda b, i: (b, i, 0, 0)),
            pl.BlockSpec((1, tm // WINDOW, SWA_VT_ROWS, WINDOW), lambda b, i: (b, i, 0, 0)),
        ),
        out_shape=out_shape,
        compiler_params=pltpu.CompilerParams(
            dimension_semantics=("parallel", "parallel"), vmem_limit_bytes=VMEM_LIMIT_BYTES),
        name="proj_rope",
    )(x, g1, w_nat, w_tr, cos_t, sin_t)


def _diff_kernel(lam_ref, qt_ref, k_ref, vt_ref, g_ref, o_ref, *scratch, tq, tk, lambda_init):
    step = pl.program_id(2)
    s_bufs = scratch[:DIFF_S_BUFS]
    top_bufs = scratch[DIFF_S_BUFS:2 * DIFF_S_BUFS]
    state = scratch[2 * DIFF_S_BUFS:2 * DIFF_S_BUFS + 2 * DIFF_Q_TILES]
    bias_ref = scratch[-1]

    @pl.when(step == 0)
    def _():
        r = lax.broadcasted_iota(jnp.int32, (tk, 2 * tq), 0)
        c = lax.broadcasted_iota(jnp.int32, (tk, 2 * tq), 1) & (tq - 1)
        bias_ref[...] = jnp.where(r <= c, 0.0, NEG).astype(F32)

    lam_p = lam_ref[...]
    lam = (jnp.exp(jnp.sum(lam_p[0:1] * lam_p[1:2], axis=-1, keepdims=True))
           - jnp.exp(jnp.sum(lam_p[2:3] * lam_p[3:4], axis=-1, keepdims=True)) + lambda_init)

    for sub in range(DIFF_Q_TILES):
        _diff_query_tile(step * DIFF_Q_TILES + sub, qt_ref[0, 0, :, sub * tq:(sub + 1) * tq], k_ref, vt_ref,
                         g_ref, o_ref.at[0, pl.ds(sub * tq, tq), :], s_bufs, top_bufs,
                         state[2 * sub], state[2 * sub + 1], bias_ref, lam,
                         tq=tq, tk=tk, lambda_init=lambda_init)


def _diff_query_tile(i, qt, k_ref, vt_ref, g_ref, o_ref, s_bufs, top_bufs, m_ref, acc_ref, bias_ref, lam,
                     *, tq, tk, lambda_init):
    z = jnp.zeros((HEAD_DIM, tq), BF16)
    qw = jnp.concatenate([jnp.concatenate([qt[:HEAD_DIM], z], axis=1),
                          jnp.concatenate([z, qt[HEAD_DIM:]], axis=1)], axis=0)

    def scores(j, par):
        kt = k_ref[0, 0, pl.ds(pl.multiple_of(j * tk, tk), tk), :]
        s = jnp.dot(kt, qw, preferred_element_type=F32)
        s_bufs[par][...] = s
        top_bufs[par][...] = jnp.max(s, axis=0, keepdims=True)

    def absorb(j, par, masked):
        s = s_bufs[par][...]
        if masked:
            s = s + bias_ref[...]
            top = jnp.max(s, axis=0, keepdims=True)
        else:
            top = top_bufs[par][...]
        m = m_ref[...]
        m_new = jnp.maximum(m, top)
        alpha = jnp.exp2(m - m_new)
        p = jnp.exp2(s - m_new).astype(BF16)
        m_ref[...] = m_new
        pv = jnp.dot(vt_ref[0, 0, j], p, preferred_element_type=F32)
        acc_ref[...] = alpha * acc_ref[...] + pv

    m_ref[...] = jnp.full(m_ref.shape, NEG, F32)
    acc_ref[...] = jnp.zeros(acc_ref.shape, F32)

    nfull = (i * tq) // tk
    scores(nfull, 0)
    scores(0, 1)
    absorb(nfull, 0, True)

    def group(t, c):
        j = DIFF_UNROLL * t
        for idx in range(DIFF_UNROLL):
            scores(j + idx + 1, (idx + 2) % DIFF_S_BUFS)
            absorb(j + idx, (idx + 1) % DIFF_S_BUFS, False)
        return c

    lax.fori_loop(0, nfull // DIFF_UNROLL, group, 0)

    for rem in range(1, DIFF_UNROLL):
        @pl.when(nfull % DIFF_UNROLL == rem)
        def _():
            first = nfull - rem
            for idx in range(rem):
                if idx + 1 < rem:
                    scores(first + idx + 1, (idx + 2) % DIFF_S_BUFS)
                absorb(first + idx, (idx + 1) % DIFF_S_BUFS, False)

    inv_l = 1.0 / acc_ref[DIFF_V_DIM:DIFF_V_DIM + 1, :]
    o = (acc_ref[:DIFF_V_DIM, :tq] * inv_l[:, :tq]
         - lam * (acc_ref[:DIFF_V_DIM, tq:] * inv_l[:, tq:]))
    o = o * lax.rsqrt(jnp.mean(o * o, axis=0, keepdims=True) + EPS)
    o_ref[...] = (o.T * g_ref[...] * (1.0 - lambda_init)).astype(BF16)


def _diff_call(lam_p, dqt, dk, dvt, subln_g, *, tq, tk, lambda_init):
    B, H, _, S = dqt.shape
    assert tk == tq and S % tk == 0, "the diagonal tile's causal pattern is built for square tiles"
    nkv = S // tk
    tq_step = DIFF_Q_TILES * tq
    assert S % tq_step == 0
    grid = (B, H, S // tq_step)
    return pl.pallas_call(
        functools.partial(_diff_kernel, tq=tq, tk=tk, lambda_init=lambda_init),
        grid=grid,
        in_specs=[
            pl.BlockSpec(lam_p.shape, lambda b, h, i: (0, 0)),
            pl.BlockSpec((1, 1, 2 * HEAD_DIM, tq_step), lambda b, h, i: (b, h, 0, i)),
            pl.BlockSpec((1, 1, S, 2 * HEAD_DIM), lambda b, h, i: (b, h, 0, 0)),
            pl.BlockSpec((1, 1, nkv, VT_ROWS, tk), lambda b, h, i: (b, h, 0, 0, 0)),
            pl.BlockSpec((1, DIFF_V_DIM), lambda b, h, i: (0, 0)),
        ],
        out_specs=pl.BlockSpec((1, tq_step, DIFF_V_DIM), lambda b, h, i: (b, i, h)),
        out_shape=jax.ShapeDtypeStruct((B, S, DIFF_V_COLS), BF16),
        scratch_shapes=[pltpu.VMEM((tk, 2 * tq), F32)] * DIFF_S_BUFS + [
            pltpu.VMEM((1, 2 * tq), F32)] * DIFF_S_BUFS + [
            pltpu.VMEM((1, 2 * tq), F32),
            pltpu.VMEM((VT_ROWS, 2 * tq), F32)] * DIFF_Q_TILES + [
            pltpu.VMEM((tk, 2 * tq), F32),
        ],
        compiler_params=pltpu.CompilerParams(
            dimension_semantics=("parallel", "parallel", "arbitrary"),
            vmem_limit_bytes=VMEM_LIMIT_BYTES),
        name="diff_attn",
    )(lam_p, dqt, dk, dvt, subln_g)


def _swa_kernel(sink_ref, qt_ref, k_ref, vt_ref, o_ref, *, tq):
    i = pl.program_id(1)
    n_cols = SWA_Q_HEADS * WINDOW
    half_cols = n_cols // SWA_KV_HEADS
    sink = sink_ref[...]
    row = lax.broadcasted_iota(jnp.int32, (2 * WINDOW, WINDOW), 0)
    qrel = lax.broadcasted_iota(jnp.int32, (2 * WINDOW, WINDOW), 1)
    band = (row - WINDOW <= qrel) & (row > qrel)
    in_current = row >= WINDOW
    z = jnp.zeros((HEAD_DIM, half_cols), BF16)
    for sub in range(tq // WINDOW):
        n = i * (tq // WINDOW) + sub
        prev = jnp.maximum(n - 1, 0)
        kwin = jnp.concatenate([k_ref[0, prev], k_ref[0, n]], axis=0)
        vtwin = jnp.concatenate([vt_ref[0, prev], vt_ref[0, n]], axis=1)
        qt = qt_ref[0, :, sub * WINDOW:(sub + 1) * WINDOW]
        heads = [qt[h * HEAD_DIM:(h + 1) * HEAD_DIM] for h in range(SWA_Q_HEADS)]
        qw = jnp.concatenate(
            [jnp.concatenate(heads[:SWA_GROUP] + [z], axis=1),
             jnp.concatenate([z] + heads[SWA_GROUP:], axis=1)], axis=0)
        s = jnp.dot(kwin, qw, preferred_element_type=F32)
        valid = band & (in_current | (n >= 1))
        s = jnp.concatenate(
            [jnp.where(valid, s[:, h * WINDOW:(h + 1) * WINDOW], NEG) for h in range(SWA_Q_HEADS)], axis=1)
        m = jnp.maximum(jnp.max(s, axis=0, keepdims=True), sink)
        p = jnp.exp2(s - m).astype(BF16)
        acc = jnp.dot(vtwin, p, preferred_element_type=F32)
        den = acc[SWA_KV_COLS:SWA_KV_COLS + 1] + jnp.exp2(sink - m)
        on = acc[:SWA_KV_COLS] / den
        u = jnp.concatenate([on[:HEAD_DIM, :half_cols], on[HEAD_DIM:, half_cols:]], axis=1)
        for hp in range(SWA_Q_HEADS // 2):
            two = jnp.concatenate([u[:, (2 * hp) * WINDOW:(2 * hp + 1) * WINDOW],
                                   u[:, (2 * hp + 1) * WINDOW:(2 * hp + 2) * WINDOW]], axis=0)
            o_ref[0, sub * WINDOW:(sub + 1) * WINDOW, hp * LANES:(hp + 1) * LANES] = two.T.astype(BF16)


def _swa_call(sink_row, sqt, sk, svt, *, tq):
    B, _, S = sqt.shape
    nb = S // WINDOW
    return pl.pallas_call(
        functools.partial(_swa_kernel, tq=tq),
        grid=(B, S // tq),
        in_specs=[
            pl.BlockSpec(sink_row.shape, lambda b, i: (0, 0)),
            pl.BlockSpec((1, SWA_Q_COLS, tq), lambda b, i: (b, 0, i)),
            pl.BlockSpec((1, nb, WINDOW, SWA_KV_COLS), lambda b, i: (b, 0, 0, 0)),
            pl.BlockSpec((1, nb, SWA_VT_ROWS, WINDOW), lambda b, i: (b, 0, 0, 0)),
        ],
        out_specs=pl.BlockSpec((1, tq, SWA_Q_COLS), lambda b, i: (b, i, 0)),
        out_shape=jax.ShapeDtypeStruct((B, S, SWA_Q_COLS), BF16),
        compiler_params=pltpu.CompilerParams(
            dimension_semantics=("parallel", "arbitrary"), vmem_limit_bytes=VMEM_LIMIT_BYTES),
        name="swa_attn",
    )(sink_row, sqt, sk, svt)


def _pack_bf16_pairs(x):
    n = x.shape[1] // 2
    lo = lax.bitcast_convert_type(x[:, :n].astype(BF16).astype(F32), jnp.uint32)
    hi = lax.bitcast_convert_type(x[:, n:].astype(BF16).astype(F32), jnp.uint32)
    return (lo >> 16) | (hi & jnp.uint32(0xFFFF0000))


def _unpack_bf16_pairs(w):
    lo = lax.bitcast_convert_type(w << 16, F32)
    hi = lax.bitcast_convert_type(w & jnp.uint32(0xFFFF0000), F32)
    return jnp.concatenate([lo, hi], axis=1).astype(BF16)


def _mix_kernel(x_ref, od_ref, os_ref, wo_ref, g2_ref, wr_ref, br_ref, x1_ref, n2_ref, rt_ref, cnt_ref):
    tm = x_ref.shape[1]
    lane = lax.broadcasted_iota(jnp.int32, (tm, ROUTER_COLS), 1)
    lane_f = lane.astype(F32)
    big = float(ROUTER_COLS)
    mixed = jnp.concatenate([od_ref[0], os_ref[0]], axis=1)
    h = x_ref[0] + jnp.dot(mixed, wo_ref[...], preferred_element_type=F32)
    x1_ref[0] = h
    n2 = h * lax.rsqrt(jnp.mean(h * h, axis=-1, keepdims=True) + EPS) * g2_ref[...]
    n2_ref[0] = _pack_bf16_pairs(n2)
    n2_hi = n2.astype(BF16)
    n2_lo = (n2 - n2_hi.astype(F32)).astype(BF16)
    parts = jnp.dot(jnp.concatenate([n2_hi, n2_lo], axis=0), wr_ref[...],
                    preferred_element_type=F32)
    logits = ((parts[:tm, :ROUTER_COLS] + parts[tm:, ROUTER_COLS:])
              + (parts[:tm, ROUTER_COLS:] + parts[tm:, :ROUTER_COLS])) + br_ref[...]
    gl = jnp.where(lane < N_GROUPS, logits, -jnp.inf)
    gm = jnp.max(gl, axis=-1, keepdims=True)
    p_top = 1.0 / jnp.sum(jnp.exp(gl - gm), axis=-1, keepdims=True)
    g_idx = jnp.min(jnp.where(gl == gm, lane_f, big), axis=-1, keepdims=True)
    e_lo = N_GROUPS + EXPERTS_PER_GROUP * g_idx
    el = jnp.where((lane_f >= e_lo) & (lane_f < e_lo + EXPERTS_PER_GROUP), logits, -jnp.inf)
    v1 = jnp.max(el, axis=-1, keepdims=True)
    i1 = jnp.min(jnp.where(el == v1, lane_f, big), axis=-1, keepdims=True)
    el2 = jnp.where(lane_f == i1, -jnp.inf, el)
    v2 = jnp.max(el2, axis=-1, keepdims=True)
    i2 = jnp.min(jnp.where(el2 == v2, lane_f, big), axis=-1, keepdims=True)
    e21 = jnp.exp(v2 - v1)
    gate1 = p_top / (1.0 + e21)
    gate2 = p_top * e21 / (1.0 + e21)
    rt_ref[0] = jnp.where(lane == 0, i1 - N_GROUPS,
                jnp.where(lane == 1, i2 - N_GROUPS,
                jnp.where(lane == 2, gate1, jnp.where(lane == 3, gate2, 0.0))))
    chosen = ((lane_f == i1 - N_GROUPS) | (lane_f == i2 - N_GROUPS)).astype(F32)
    cnt_ref[0, 0] = jnp.broadcast_to(jnp.sum(chosen, axis=0, keepdims=True), cnt_ref.shape[2:])


def _mix_call(x, o_diff, o_swa, w_out, g2, w_router, b_router, *, tm):
    B, S, D = x.shape
    const = lambda b, i: (0, 0)
    row = lambda b, i: (b, i, 0)
    nt = S // tm
    return pl.pallas_call(
        _mix_kernel,
        grid=(B, nt),
        in_specs=[
            pl.BlockSpec((1, tm, D), row),
            pl.BlockSpec((1, tm, DIFF_V_COLS), row),
            pl.BlockSpec((1, tm, SWA_Q_COLS), row),
            pl.BlockSpec(w_out.shape, const),
            pl.BlockSpec((1, D), const),
            pl.BlockSpec(w_router.shape, const),
            pl.BlockSpec((1, ROUTER_COLS), const),
        ],
        out_specs=(pl.BlockSpec((1, tm, D), row), pl.BlockSpec((1, tm, D // 2), row),
                   pl.BlockSpec((1, tm, ROUTER_COLS), row),
                   pl.BlockSpec((1, 1, SUBLANES, ROUTER_COLS), lambda b, i: (b, i, 0, 0))),
        out_shape=(jax.ShapeDtypeStruct((B, S, D), F32), jax.ShapeDtypeStruct((B, S, D // 2), jnp.uint32),
                   jax.ShapeDtypeStruct((B, S, ROUTER_COLS), F32),
                   jax.ShapeDtypeStruct((B, nt, SUBLANES, ROUTER_COLS), F32)),
        compiler_params=pltpu.CompilerParams(
            dimension_semantics=("parallel", "parallel"), vmem_limit_bytes=VMEM_LIMIT_BYTES),
        name="outproj_router",
    )(x, o_diff, o_swa, w_out, g2, w_router, b_router)


def _slot_kernel(rt_ref, base_ref, dest_ref, *, tm):
    rt_t = rt_ref[...].T
    e1 = rt_t[0:1].astype(jnp.int32)
    e2 = rt_t[1:2].astype(jnp.int32)
    eid = lax.broadcasted_iota(jnp.int32, (N_EXPERTS, tm), 0)
    oh1 = eid == e1
    oh2 = eid == e2
    earlier = (lax.broadcasted_iota(jnp.int32, (tm, tm), 0)
               < lax.broadcasted_iota(jnp.int32, (tm, tm), 1)).astype(BF16)
    before = jnp.dot((oh1 | oh2).astype(BF16), earlier, preferred_element_type=F32)
    slot = before + base_ref[0][:, 0:1]
    d1 = jnp.sum(jnp.where(oh1, slot, 0.0), axis=0, keepdims=True).astype(jnp.int32)
    d2 = jnp.sum(jnp.where(oh2, slot, 0.0), axis=0, keepdims=True).astype(jnp.int32)
    dest_ref[0] = jnp.concatenate([d1, d2, jnp.zeros((SUBLANES - TOP_K, tm), jnp.int32)], axis=0)


def _slot_call(rt, tile_base, *, tm):
    nt = rt.shape[0] // tm
    return pl.pallas_call(
        functools.partial(_slot_kernel, tm=tm),
        grid=(nt,),
        in_specs=[
            pl.BlockSpec((tm, ROUTER_COLS), lambda t: (t, 0)),
            pl.BlockSpec((1, N_EXPERTS, LANES), lambda t: (t, 0, 0)),
        ],
        out_specs=pl.BlockSpec((1, SUBLANES, tm), lambda t: (t, 0, 0)),
        out_shape=jax.ShapeDtypeStruct((nt, SUBLANES, tm), jnp.int32),
        compiler_params=pltpu.CompilerParams(dimension_semantics=("parallel",)),
        name="moe_slots",
    )(rt, tile_base)


SC_ROW_CHUNK = 64


def _sc_workers():
    info = plsc.get_sparse_core_info()
    return info.num_cores, info.num_cores * info.num_subcores


def _sc_scatter_rows(rows, idx, n_out):
    n, width = rows.shape
    n_cores, n_workers = _sc_workers()
    n_chunks = n // SC_ROW_CHUNK
    per_worker = n_chunks // n_workers
    assert n_chunks % (2 * n_workers) == 0
    mesh = plsc.VectorSubcoreMesh(core_axis_name="c", subcore_axis_name="s")

    @functools.partial(
        pl.kernel, mesh=mesh,
        out_type=jax.ShapeDtypeStruct((n_out, width), rows.dtype),
        scratch_types=[pltpu.VMEM((SC_ROW_CHUNK,), jnp.int32)] * (2 * TOP_K)
        + [pltpu.VMEM((SC_ROW_CHUNK, width), rows.dtype)] * 2
        + [pltpu.SemaphoreType.DMA] * (2 + 2 * TOP_K),
    )
    def scatter(rows_hbm, idx_hbm, out_hbm, *scratch):
        idx_v = scratch[:2 * TOP_K]
        rows_v = scratch[2 * TOP_K:2 * TOP_K + 2]
        load_sems, store_sems = scratch[-(2 + 2 * TOP_K):-2 * TOP_K], scratch[-2 * TOP_K:]
        worker = lax.axis_index("s") * n_cores + lax.axis_index("c")

        @pl.loop(0, per_worker, step=2)
        def _(i):
            loads = []
            for half in range(2):
                c = worker * per_worker + i + half
                src = rows_hbm.at[pl.ds(pl.multiple_of(c * SC_ROW_CHUNK, SC_ROW_CHUNK), SC_ROW_CHUNK)]
                loads.append(pltpu.async_copy(src, rows_v[half], load_sems[half]))
                for k in range(TOP_K):
                    pltpu.sync_copy(idx_hbm.at[k, c], idx_v[half * TOP_K + k])
            stores = []
            for half in range(2):
                loads[half].wait()
                for k in range(TOP_K):
                    j = half * TOP_K + k
                    stores.append(pltpu.async_copy(rows_v[half], out_hbm.at[idx_v[j]], store_sems[j]))
            for store in stores:
                store.wait()

    return scatter(rows, idx)


def _expert_kernel(be_ref, nvalid_ref, next_ref, xs_ref, wg_hbm, wu_hbm, wd_hbm, y_ref,
                   wg_st, wu_st, wd_st, wg_b, wu_b, wd_b, slot_ref, sems):
    b = pl.program_id(0)
    n_valid = nvalid_ref[b]
    sources, staged, cast = (wg_hbm, wu_hbm, wd_hbm), (wg_st, wu_st, wd_st), (wg_b, wu_b, wd_b)

    def weight_copies(expert, slot):
        return [pltpu.make_async_copy(src.at[expert], dst.at[slot], sems.at[slot, i])
                for i, (src, dst) in enumerate(zip(sources, staged))]

    @pl.when(n_valid > 0)
    def _():
        @pl.when((b == 0) | (be_ref[b] != be_ref[jnp.maximum(b - 1, 0)]))
        def _():
            @pl.when(b == 0)
            def _():
                slot_ref[0] = 0
                for copy in weight_copies(be_ref[0], 0):
                    copy.start()

            slot = slot_ref[0]
            for copy in weight_copies(be_ref[b], slot):
                copy.wait()
            for dst, src in zip(cast, staged):
                dst[...] = src[slot].astype(BF16)

            @pl.when(next_ref[b] >= 0)
            def _():
                for copy in weight_copies(next_ref[b], 1 - slot):
                    copy.start()

            slot_ref[0] = 1 - slot

        for c in range(EXPERT_BLOCK // EXPERT_CHUNK):
            rows = pl.ds(c * EXPERT_CHUNK, EXPERT_CHUNK)
            row_id = c * EXPERT_CHUNK + lax.broadcasted_iota(jnp.int32, (EXPERT_CHUNK, xs_ref.shape[1]), 0)
            packed = jnp.where(row_id < n_valid, xs_ref[rows, :], jnp.uint32(0))
            xb = _unpack_bf16_pairs(packed)
            gate = jnp.dot(xb, wg_b[...], preferred_element_type=F32)
            up = jnp.dot(xb, wu_b[...], preferred_element_type=F32)
            hid = (gate * jax.nn.sigmoid(gate) * up).astype(BF16)
            y_ref[rows, :] = _pack_bf16_pairs(jnp.dot(hid, wd_b[...], preferred_element_type=F32))

    @pl.when(n_valid == 0)
    def _():
        y_ref[...] = jnp.zeros_like(y_ref)


def _expert_call(block_expert, n_valid, next_expert, xs, w_gate, w_up, w_down):
    P = xs.shape[0]
    NB = P // EXPERT_BLOCK
    E, D, F = w_gate.shape
    grid_spec = pltpu.PrefetchScalarGridSpec(
        num_scalar_prefetch=3,
        grid=(NB,),
        in_specs=[
            pl.BlockSpec((EXPERT_BLOCK,) + xs.shape[1:], lambda b, *_: (b, 0)),
            pl.BlockSpec(memory_space=pl.ANY),
            pl.BlockSpec(memory_space=pl.ANY),
            pl.BlockSpec(memory_space=pl.ANY),
        ],
        out_specs=pl.BlockSpec((EXPERT_BLOCK, D // 2), lambda b, *_: (b, 0)),
        scratch_shapes=[
            pltpu.VMEM((2, D, F), F32),
            pltpu.VMEM((2, D, F), F32),
            pltpu.VMEM((2, F, D), F32),
            pltpu.VMEM((D, F), BF16),
            pltpu.VMEM((D, F), BF16),
            pltpu.VMEM((F, D), BF16),
            pltpu.SMEM((1,), jnp.int32),
            pltpu.SemaphoreType.DMA((2, 3)),
        ],
    )
    return pl.pallas_call(
        _expert_kernel,
        grid_spec=grid_spec,
        out_shape=jax.ShapeDtypeStruct((P, D // 2), jnp.uint32),
        compiler_params=pltpu.CompilerParams(
            dimension_semantics=("arbitrary",), vmem_limit_bytes=VMEM_LIMIT_BYTES),
        name="moe_experts",
    )(block_expert, n_valid, next_expert, xs, w_gate, w_up, w_down)


def _sc_gather_rows(table, idx):
    n_rows, width = idx.shape[0], table.shape[1]
    n_cores, n_workers = _sc_workers()
    per_worker = n_rows // n_workers
    assert n_rows % (n_workers * 2 * SC_ROW_CHUNK) == 0
    mesh = plsc.VectorSubcoreMesh(core_axis_name="c", subcore_axis_name="s")

    @functools.partial(
        pl.kernel, mesh=mesh,
        out_type=jax.ShapeDtypeStruct((n_rows, width), table.dtype),
        scratch_types=[pltpu.VMEM((SC_ROW_CHUNK,), jnp.int32)] * 2
        + [pltpu.VMEM((SC_ROW_CHUNK, width), table.dtype)] * 2
        + [pltpu.SemaphoreType.DMA] * 4,
    )
    def gather(table_hbm, idx_hbm, out_hbm, idx_a, idx_b, rows_a, rows_b, sem_ga, sem_gb, sem_wa, sem_wb):
        worker = lax.axis_index("s") * n_cores + lax.axis_index("c")
        base = worker * per_worker

        @pl.loop(0, per_worker // SC_ROW_CHUNK, step=2)
        def _(c):
            off_a = pl.multiple_of(base + c * SC_ROW_CHUNK, SC_ROW_CHUNK)
            off_b = pl.multiple_of(off_a + SC_ROW_CHUNK, SC_ROW_CHUNK)
            pltpu.sync_copy(idx_hbm.at[pl.ds(off_a, SC_ROW_CHUNK)], idx_a)
            pltpu.sync_copy(idx_hbm.at[pl.ds(off_b, SC_ROW_CHUNK)], idx_b)
            gather_a = pltpu.async_copy(table_hbm.at[idx_a], rows_a, sem_ga)
            gather_b = pltpu.async_copy(table_hbm.at[idx_b], rows_b, sem_gb)
            gather_a.wait()
            write_a = pltpu.async_copy(rows_a, out_hbm.at[pl.ds(off_a, SC_ROW_CHUNK)], sem_wa)
            gather_b.wait()
            write_b = pltpu.async_copy(rows_b, out_hbm.at[pl.ds(off_b, SC_ROW_CHUNK)], sem_wb)
            write_a.wait()
            write_b.wait()

    return gather(table, idx)


def _combine_kernel(x1_ref, rt_ref, y_ref, fg_ref, o_ref, *, final_norm):
    rt = rt_ref[...]
    y1 = _unpack_bf16_pairs(y_ref[0, 0]).astype(F32)
    y2 = _unpack_bf16_pairs(y_ref[0, 1]).astype(F32)
    h = x1_ref[...] + rt[:, 2:3] * y1 + rt[:, 3:4] * y2
    if final_norm:
        h = h * lax.rsqrt(jnp.mean(h * h, axis=-1, keepdims=True) + EPS) * fg_ref[...]
    o_ref[...] = h


def _combine_call(x1, rt, ysg, final_g, *, tm, final_norm):
    T, D = x1.shape
    return pl.pallas_call(
        functools.partial(_combine_kernel, final_norm=final_norm),
        grid=(T // tm,),
        in_specs=[
            pl.BlockSpec((tm, D), lambda t: (t, 0)),
            pl.BlockSpec((tm, ROUTER_COLS), lambda t: (t, 0)),
            pl.BlockSpec((1, TOP_K, tm, D // 2), lambda t: (t, 0, 0, 0)),
            pl.BlockSpec((1, D), lambda t: (0, 0)),
        ],
        out_specs=pl.BlockSpec((tm, D), lambda t: (t, 0)),
        out_shape=jax.ShapeDtypeStruct((T, D), F32),
        compiler_params=pltpu.CompilerParams(
            dimension_semantics=("parallel",), vmem_limit_bytes=VMEM_LIMIT_BYTES),
        name="moe_combine",
    )(x1, rt, ysg, final_g)


def _slot_layout(tile_counts, n_assign):
    NB = -(-n_assign // EXPERT_BLOCK) + N_EXPERTS
    n_tiles = tile_counts.shape[0]
    tc = tile_counts.astype(F32)
    hp = lax.Precision.HIGHEST
    counts = jnp.sum(tc, axis=0)
    padded = jnp.ceil(counts / EXPERT_BLOCK) * EXPERT_BLOCK
    upper = (jnp.arange(N_EXPERTS)[:, None] < jnp.arange(N_EXPERTS)[None, :]).astype(F32)
    pad_start = jnp.dot(padded, upper, precision=hp)
    pad_end = pad_start + padded
    lower = (jnp.arange(n_tiles)[:, None] > jnp.arange(n_tiles)[None, :]).astype(F32)
    tile_base = pad_start[None, :] + jnp.dot(lower, tc, precision=hp)
    block_start = jnp.arange(NB, dtype=F32) * EXPERT_BLOCK
    block_expert = jnp.minimum(jnp.sum((pad_end[None, :] <= block_start[:, None]).astype(jnp.int32), axis=1),
                               N_EXPERTS - 1)
    mine = block_expert[:, None] == jnp.arange(N_EXPERTS)[None, :]
    run_end = jnp.sum(jnp.where(mine, (pad_start + counts)[None, :], 0.0), axis=1)
    n_valid = jnp.clip(run_end - block_start, 0, EXPERT_BLOCK).astype(jnp.int32)
    eid = jnp.arange(N_EXPERTS)
    later_nonempty = (eid[None, :] > eid[:, None]) & (counts[None, :] > 0)
    next_nonempty = jnp.min(jnp.where(later_nonempty, eid[None, :], N_EXPERTS), axis=1)
    next_nonempty = jnp.where(next_nonempty < N_EXPERTS, next_nonempty, -1)
    next_expert = jnp.sum(jnp.where(mine, next_nonempty[None, :], 0), axis=1).astype(jnp.int32)
    return NB, block_expert.astype(jnp.int32), n_valid, next_expert, tile_base


def _rope_tables(S):
    inv = 1.0 / (ROPE_THETA ** (jnp.arange(0, HEAD_DIM, 2, dtype=F32) / HEAD_DIM))
    ang_t = inv[:, None] * jnp.arange(S, dtype=F32)[None, :]
    return jnp.cos(ang_t), jnp.sin(ang_t)


def _tiles(S):
    tile = min(512, S)
    return tile, tile, min(1024, S), min(1024, S), min(1024, S)


def kernel(x, norm1_g, w_in, lambda_q1, lambda_k1, lambda_q2, lambda_k2, subln_g, sinks, w_out,
           norm2_g, w_router_group, b_router_group, w_router_expert, b_router_expert,
           w_gate, w_up, w_down, final_g):
    B, S, D = x.shape
    T = B * S
    depth = w_in.shape[0]
    tq, tk, tm_proj, tm_tok, tq_swa = _tiles(S)
    qscale = HEAD_DIM ** -0.5 * math.log2(math.e)
    cos_t, sin_t = _rope_tables(S)

    c0 = DIFF_QK_COLS
    c1 = 2 * DIFF_QK_COLS
    c2 = c1 + DIFF_V_COLS
    c3 = c2 + SWA_Q_COLS
    c4 = c3 + SWA_KV_COLS
    for l in range(depth):
        lambda_init = 0.8 - 0.6 * math.exp(-0.3 * l)
        w = w_in[l]
        w_nat = jnp.concatenate([w[:, c0:c1], w[:, c3:c4]], axis=1).astype(BF16)
        w_tr = jnp.concatenate([w[:, :c0] * qscale, w[:, c1:c2], w[:, c2:c3] * qscale, w[:, c4:]],
                               axis=1).astype(BF16).T
        dqt, dk, dvt, sqt, sk, svt = _proj_call(
            x, norm1_g[l][None, :], w_nat, w_tr, cos_t, sin_t, tm=tm_proj, tk=tk)

        lam_p = jnp.stack([lambda_q1[l], lambda_k1[l], lambda_q2[l], lambda_k2[l]]).astype(F32)
        o_diff = _diff_call(lam_p, dqt, dk, dvt, subln_g[l][None, :].astype(F32),
                            tq=tq, tk=tk, lambda_init=lambda_init)
        sink_row = jnp.repeat(sinks[l].astype(F32) * math.log2(math.e), WINDOW)[None, :]
        o_swa = _swa_call(sink_row, sqt, sk, svt, tq=tq_swa)

        wo_b = w_out[l].astype(BF16)
        w_router = jnp.zeros((D, ROUTER_COLS), F32)
        w_router = w_router.at[:, :N_GROUPS].set(w_router_group[l])
        w_router = w_router.at[:, N_GROUPS:N_GROUPS + N_EXPERTS].set(w_router_expert[l])
        w_router_hi = w_router.astype(BF16)
        w_router_lo = (w_router - w_router_hi.astype(F32)).astype(BF16)
        w_router = jnp.concatenate([w_router_hi, w_router_lo], axis=1)
        b_router = jnp.zeros((1, ROUTER_COLS), F32)
        b_router = b_router.at[0, :N_GROUPS].set(b_router_group[l])
        b_router = b_router.at[0, N_GROUPS:N_GROUPS + N_EXPERTS].set(b_router_expert[l])
        x1, n2p, rt, cnt = _mix_call(x, o_diff, o_swa, wo_b, norm2_g[l][None, :], w_router, b_router, tm=tm_tok)

        rt2 = rt.reshape(T, ROUTER_COLS)
        tile_counts = cnt[:, :, 0, :N_EXPERTS].reshape(T // tm_tok, N_EXPERTS).astype(jnp.int32)
        NB, block_expert, n_valid, next_expert, tile_base = _slot_layout(tile_counts, T * TOP_K)
        tile_base = jnp.broadcast_to(tile_base.astype(F32)[:, :, None], (T // tm_tok, N_EXPERTS, LANES))
        dest = _slot_call(rt2, tile_base, tm=tm_tok)
        scatter_idx = jnp.swapaxes(dest[:, :TOP_K, :], 0, 1).reshape(TOP_K, T // SC_ROW_CHUNK, SC_ROW_CHUNK)
        xs = _sc_scatter_rows(n2p.reshape(T, D // 2), scatter_idx, NB * EXPERT_BLOCK)
        ys = _expert_call(block_expert, n_valid, next_expert, xs, w_gate[l], w_up[l], w_down[l])
        ysg = _sc_gather_rows(ys, dest[:, :TOP_K, :].reshape(T * TOP_K))
        x = _combine_call(x1.reshape(T, D), rt2, ysg.reshape(T // tm_tok, TOP_K, tm_tok, D // 2),
                          final_g[None, :], tm=tm_tok, final_norm=(l == depth - 1)).reshape(B, S, D)
    return x
```

```python
import functools
import math

import jax
import jax.numpy as jnp
from jax import lax
from jax.experimental import pallas as pl
from jax.experimental.pallas import tpu as pltpu
from jax.experimental.pallas import tpu_sc as plsc

HEAD_DIM = 64
DIFF_HEADS = 4
DIFF_V_DIM = 2 * HEAD_DIM
SWA_Q_HEADS = 8
SWA_KV_HEADS = 2
SWA_GROUP = SWA_Q_HEADS // SWA_KV_HEADS
WINDOW = 128
ROPE_THETA = 10000.0
N_GROUPS = 4
EXPERTS_PER_GROUP = 8
N_EXPERTS = N_GROUPS * EXPERTS_PER_GROUP
TOP_K = 2
EXPERT_BLOCK = 512
EXPERT_CHUNK = 256
EPS = 1e-6
NEG = -1e30

DIFF_QK_COLS = DIFF_HEADS * 2 * HEAD_DIM
DIFF_V_COLS = DIFF_HEADS * DIFF_V_DIM
SWA_Q_COLS = SWA_Q_HEADS * HEAD_DIM
SWA_KV_COLS = SWA_KV_HEADS * HEAD_DIM
LANES = 128
SUBLANES = 8
BF16_SUBLANES = 16
VMEM_LIMIT_BYTES = 48 * 1024 * 1024
VT_ROWS = DIFF_V_DIM + BF16_SUBLANES
SWA_VT_ROWS = SWA_KV_COLS + BF16_SUBLANES
ROUTER_COLS = LANES
DIFF_UNROLL = 4
DIFF_S_BUFS = 4
DIFF_Q_TILES = 4

BF16 = jnp.bfloat16
F32 = jnp.float32


def _rope_lanes(x, cos_l, sin_l, first_half):
    rot = jnp.where(first_half, pltpu.roll(x, 96, 1), pltpu.roll(x, 32, 1))
    return x * cos_l + rot * sin_l


def _proj_kernel(x_ref, g_ref, wnat_ref, wtr_ref, cost_ref, sint_ref,
                 dqt_ref, dk_ref, dvt_ref, sqt_ref, sk_ref, svt_ref, *, tk):
    x = x_ref[0]
    tm = x.shape[0]
    n1 = x * lax.rsqrt(jnp.mean(x * x, axis=-1, keepdims=True) + EPS) * g_ref[...]
    n1b = n1.astype(BF16)
    nat = jnp.dot(n1b, wnat_ref[...], preferred_element_type=F32)
    tr = lax.dot_general(wtr_ref[...], n1b, (((1,), (1,)), ((), ())),
                         preferred_element_type=F32)

    c32, s32 = cost_ref[...].T, sint_ref[...].T
    cos_l = jnp.concatenate([c32] * (LANES // c32.shape[1]), axis=1)
    sin_l = jnp.concatenate([-s32, s32] * (LANES // HEAD_DIM), axis=1)
    first_half = (lax.broadcasted_iota(jnp.int32, (tm, LANES), 1) & (HEAD_DIM - 1)) < HEAD_DIM // 2
    for h in range(DIFF_HEADS):
        slab = nat[:, h * LANES:(h + 1) * LANES]
        dk_ref[0, h] = _rope_lanes(slab, cos_l, sin_l, first_half).astype(BF16)
    sk = _rope_lanes(nat[:, DIFF_QK_COLS:DIFF_QK_COLS + LANES], cos_l, sin_l, first_half).astype(BF16)
    for c in range(tm // WINDOW):
        sk_ref[0, c] = sk[c * WINDOW:(c + 1) * WINDOW]

    cos_t, sin_t = cost_ref[...], sint_ref[...]
    half = HEAD_DIM // 2

    def rope_rows(r0):
        x1 = tr[r0:r0 + half]
        x2 = tr[r0 + half:r0 + HEAD_DIM]
        return (x1 * cos_t - x2 * sin_t).astype(BF16), (x1 * sin_t + x2 * cos_t).astype(BF16)

    for h in range(DIFF_HEADS):
        for c in range(2):
            lo, hi = rope_rows(h * 2 * HEAD_DIM + c * HEAD_DIM)
            dqt_ref[0, h, c * HEAD_DIM:c * HEAD_DIM + half] = lo
            dqt_ref[0, h, c * HEAD_DIM + half:(c + 1) * HEAD_DIM] = hi
    ones_rows = (lax.broadcasted_iota(jnp.int32, (BF16_SUBLANES, tk), 0) == 0).astype(BF16)
    for h in range(DIFF_HEADS):
        r0 = DIFF_QK_COLS + h * DIFF_V_DIM
        for c in range(tm // tk):
            dvt_ref[0, h, c, :DIFF_V_DIM] = tr[r0:r0 + DIFF_V_DIM, c * tk:(c + 1) * tk].astype(BF16)
            dvt_ref[0, h, c, DIFF_V_DIM:] = ones_rows

    r0 = DIFF_QK_COLS + DIFF_V_COLS
    for h in range(SWA_Q_HEADS):
        lo, hi = rope_rows(r0 + h * HEAD_DIM)
        sqt_ref[0, h * HEAD_DIM:h * HEAD_DIM + half] = lo
        sqt_ref[0, h * HEAD_DIM + half:(h + 1) * HEAD_DIM] = hi
    r0 += SWA_Q_COLS
    for c in range(tm // WINDOW):
        svt_ref[0, c, :SWA_KV_COLS] = tr[r0:r0 + SWA_KV_COLS, c * WINDOW:(c + 1) * WINDOW].astype(BF16)
        svt_ref[0, c, SWA_KV_COLS:] = ones_rows[:, :WINDOW]


def _proj_call(x, g1, w_nat, w_tr, cos_t, sin_t, *, tm, tk):
    B, S, D = x.shape
    nkv = S // tk
    grid = (B, S // tm)
    const = lambda b, i: (0, 0)
    out_shape = (
        jax.ShapeDtypeStruct((B, DIFF_HEADS, 2 * HEAD_DIM, S), BF16),
        jax.ShapeDtypeStruct((B, DIFF_HEADS, S, 2 * HEAD_DIM), BF16),
        jax.ShapeDtypeStruct((B, DIFF_HEADS, nkv, VT_ROWS, tk), BF16),
        jax.ShapeDtypeStruct((B, SWA_Q_COLS, S), BF16),
        jax.ShapeDtypeStruct((B, S // WINDOW, WINDOW, SWA_KV_COLS), BF16),
        jax.ShapeDtypeStruct((B, S // WINDOW, SWA_VT_ROWS, WINDOW), BF16),
    )
    return pl.pallas_call(
        functools.partial(_proj_kernel, tk=tk),
        grid=grid,
        in_specs=[
            pl.BlockSpec((1, tm, D), lambda b, i: (b, i, 0)),
            pl.BlockSpec((1, D), const),
            pl.BlockSpec(w_nat.shape, const),
            pl.BlockSpec(w_tr.shape, const),
            pl.BlockSpec((HEAD_DIM // 2, tm), lambda b, i: (0, i)),
            pl.BlockSpec((HEAD_DIM // 2, tm), lambda b, i: (0, i)),
        ],
        out_specs=(
            pl.BlockSpec((1, DIFF_HEADS, 2 * HEAD_DIM, tm), lambda b, i: (b, 0, 0, i)),
            pl.BlockSpec((1, DIFF_HEADS, tm, 2 * HEAD_DIM), lambda b, i: (b, 0, i, 0)),
            pl.BlockSpec((1, DIFF_HEADS, tm // tk, VT_ROWS, tk), lambda b, i: (b, 0, i, 0, 0)),
            pl.BlockSpec((1, SWA_Q_COLS, tm), lambda b, i: (b, 0, i)),
            pl.BlockSpec((1, tm // WINDOW, WINDOW, SWA_KV_COLS), lambda b, i: (b, i, 0, 0)),
            pl.BlockSpec((1, tm // WINDOW, SWA_VT_ROWS, WINDOW), lambda b, i: (b, i, 0, 0)),
        ),
        out_shape=out_shape,
        compiler_params=pltpu.CompilerParams(
            dimension_semantics=("parallel", "parallel"), vmem_limit_bytes=VMEM_LIMIT_BYTES),
        name="proj_rope",
    )(x, g1, w_nat, w_tr, cos_t, sin_t)


def _diff_kernel(lam_ref, qt_ref, k_ref, vt_ref, g_ref, o_ref, *scratch, tq, tk, lambda_init):
    step = pl.program_id(2)
    s_bufs = scratch[:DIFF_S_BUFS]
    top_bufs = scratch[DIFF_S_BUFS:2 * DIFF_S_BUFS]
    state = scratch[2 * DIFF_S_BUFS:2 * DIFF_S_BUFS + 2 * DIFF_Q_TILES]
    bias_ref = scratch[-1]

    @pl.when(step == 0)
    def _():
        r = lax.broadcasted_iota(jnp.int32, (tk, 2 * tq), 0)
        c = lax.broadcasted_iota(jnp.int32, (tk, 2 * tq), 1) & (tq - 1)
        bias_ref[...] = jnp.where(r <= c, 0.0, NEG).astype(F32)

    lam_p = lam_ref[...]
    lam = (jnp.exp(jnp.sum(lam_p[0:1] * lam_p[1:2], axis=-1, keepdims=True))
           - jnp.exp(jnp.sum(lam_p[2:3] * lam_p[3:4], axis=-1, keepdims=True)) + lambda_init)

    for sub in range(DIFF_Q_TILES):
        _diff_query_tile(step * DIFF_Q_TILES + sub, qt_ref[0, 0, :, sub * tq:(sub + 1) * tq], k_ref, vt_ref,
                         g_ref, o_ref.at[0, pl.ds(sub * tq, tq), :], s_bufs, top_bufs,
                         state[2 * sub], state[2 * sub + 1], bias_ref, lam,
                         tq=tq, tk=tk, lambda_init=lambda_init)


def _diff_query_tile(i, qt, k_ref, vt_ref, g_ref, o_ref, s_bufs, top_bufs, m_ref, acc_ref, bias_ref, lam,
                     *, tq, tk, lambda_init):
    z = jnp.zeros((HEAD_DIM, tq), BF16)
    qw = jnp.concatenate([jnp.concatenate([qt[:HEAD_DIM], z], axis=1),
                          jnp.concatenate([z, qt[HEAD_DIM:]], axis=1)], axis=0)

    def scores(j, par):
        kt = k_ref[0, 0, pl.ds(pl.multiple_of(j * tk, tk), tk), :]
        s = jnp.dot(kt, qw, preferred_element_type=F32)
        s_bufs[par][...] = s
        top_bufs[par][...] = jnp.max(s, axis=0, keepdims=True)

    def absorb(j, par, masked):
        s = s_bufs[par][...]
        if masked:
            s = s + bias_ref[...]
            top = jnp.max(s, axis=0, keepdims=True)
        else:
            top = top_bufs[par][...]
        m = m_ref[...]
        m_new = jnp.maximum(m, top)
        alpha = jnp.exp2(m - m_new)
        p = jnp.exp2(s - m_new).astype(BF16)
        m_ref[...] = m_new
        pv = jnp.dot(vt_ref[0, 0, j], p, preferred_element_type=F32)
        acc_ref[...] = alpha * acc_ref[...] + pv

    m_ref[...] = jnp.full(m_ref.shape, NEG, F32)
    acc_ref[...] = jnp.zeros(acc_ref.shape, F32)

    nfull = (i * tq) // tk
    scores(nfull, 0)
    scores(0, 1)
    absorb(nfull, 0, True)

    def group(t, c):
        j = DIFF_UNROLL * t
        for idx in range(DIFF_UNROLL):
            scores(j + idx + 1, (idx + 2) % DIFF_S_BUFS)
            absorb(j + idx, (idx + 1) % DIFF_S_BUFS, False)
        return c

    lax.fori_loop(0, nfull // DIFF_UNROLL, group, 0)

    for rem in range(1, DIFF_UNROLL):
        @pl.when(nfull % DIFF_UNROLL == rem)
        def _():
            first = nfull - rem
            for idx in range(rem):
                if idx + 1 < rem:
                    scores(first + idx + 1, (idx + 2) % DIFF_S_BUFS)
                absorb(first + idx, (idx + 1) % DIFF_S_BUFS, False)

    inv_l = 1.0 / acc_ref[DIFF_V_DIM:DIFF_V_DIM + 1, :]
    o = (acc_ref[:DIFF_V_DIM, :tq] * inv_l[:, :tq]
         - lam * (acc_ref[:DIFF_V_DIM, tq:] * inv_l[:, tq:]))
    o = o * lax.rsqrt(jnp.mean(o * o, axis=0, keepdims=True) + EPS)
    o_ref[...] = (o.T * g_ref[...] * (1.0 - lambda_init)).astype(BF16)


def _diff_call(lam_p, dqt, dk, dvt, subln_g, *, tq, tk, lambda_init):
    B, H, _, S = dqt.shape
    assert tk == tq and S % tk == 0, "the diagonal tile's causal pattern is built for square tiles"
    nkv = S // tk
    tq_step = DIFF_Q_TILES * tq
    assert S % tq_step == 0
    grid = (B, H, S // tq_step)
    return pl.pallas_call(
        functools.partial(_diff_kernel, tq=tq, tk=tk, lambda_init=lambda_init),
        grid=grid,
        in_specs=[
            pl.BlockSpec(lam_p.shape, lambda b, h, i: (0, 0)),
            pl.BlockSpec((1, 1, 2 * HEAD_DIM, tq_step), lambda b, h, i: (b, h, 0, i)),
            pl.BlockSpec((1, 1, S, 2 * HEAD_DIM), lambda b, h, i: (b, h, 0, 0)),
            pl.BlockSpec((1, 1, nkv, VT_ROWS, tk), lambda b, h, i: (b, h, 0, 0, 0)),
            pl.BlockSpec((1, DIFF_V_DIM), lambda b, h, i: (0, 0)),
        ],
        out_specs=pl.BlockSpec((1, tq_step, DIFF_V_DIM), lambda b, h, i: (b, i, h)),
        out_shape=jax.ShapeDtypeStruct((B, S, DIFF_V_COLS), BF16),
        scratch_shapes=[pltpu.VMEM((tk, 2 * tq), F32)] * DIFF_S_BUFS + [
            pltpu.VMEM((1, 2 * tq), F32)] * DIFF_S_BUFS + [
            pltpu.VMEM((1, 2 * tq), F32),
            pltpu.VMEM((VT_ROWS, 2 * tq), F32)] * DIFF_Q_TILES + [
            pltpu.VMEM((tk, 2 * tq), F32),
        ],
        compiler_params=pltpu.CompilerParams(
            dimension_semantics=("parallel", "parallel", "arbitrary"),
            vmem_limit_bytes=VMEM_LIMIT_BYTES),
        name="diff_attn",
    )(lam_p, dqt, dk, dvt, subln_g)


def _swa_kernel(sink_ref, qt_ref, k_ref, vt_ref, o_ref, *, tq):
    i = pl.program_id(1)
    n_cols = SWA_Q_HEADS * WINDOW
    half_cols = n_cols // SWA_KV_HEADS
    sink = sink_ref[...]
    row = lax.broadcasted_iota(jnp.int32, (2 * WINDOW, WINDOW), 0)
    qrel = lax.broadcasted_iota(jnp.int32, (2 * WINDOW, WINDOW), 1)
    band = (row - WINDOW <= qrel) & (row > qrel)
    in_current = row >= WINDOW
    z = jnp.zeros((HEAD_DIM, half_cols), BF16)
    for sub in range(tq // WINDOW):
        n = i * (tq // WINDOW) + sub
        prev = jnp.maximum(n - 1, 0)
        kwin = jnp.concatenate([k_ref[0, prev], k_ref[0, n]], axis=0)
        vtwin = jnp.concatenate([vt_ref[0, prev], vt_ref[0, n]], axis=1)
        qt = qt_ref[0, :, sub * WINDOW:(sub + 1) * WINDOW]
        heads = [qt[h * HEAD_DIM:(h + 1) * HEAD_DIM] for h in range(SWA_Q_HEADS)]
        qw = jnp.concatenate(
            [jnp.concatenate(heads[:SWA_GROUP] + [z], axis=1),
             jnp.concatenate([z] + heads[SWA_GROUP:], axis=1)], axis=0)
        s = jnp.dot(kwin, qw, preferred_element_type=F32)
        valid = band & (in_current | (n >= 1))
        s = jnp.concatenate(
            [jnp.where(valid, s[:, h * WINDOW:(h + 1) * WINDOW], NEG) for h in range(SWA_Q_HEADS)], axis=1)
        m = jnp.maximum(jnp.max(s, axis=0, keepdims=True), sink)
        p = jnp.exp2(s - m).astype(BF16)
        acc = jnp.dot(vtwin, p, preferred_element_type=F32)
        den = acc[SWA_KV_COLS:SWA_KV_COLS + 1] + jnp.exp2(sink - m)
        on = acc[:SWA_KV_COLS] / den
        u = jnp.concatenate([on[:HEAD_DIM, :half_cols], on[HEAD_DIM:, half_cols:]], axis=1)
        for hp in range(SWA_Q_HEADS // 2):
            two = jnp.concatenate([u[:, (2 * hp) * WINDOW:(2 * hp + 1) * WINDOW],
                                   u[:, (2 * hp + 1) * WINDOW:(2 * hp + 2) * WINDOW]], axis=0)
            o_ref[0, sub * WINDOW:(sub + 1) * WINDOW, hp * LANES:(hp + 1) * LANES] = two.T.astype(BF16)


def _swa_call(sink_row, sqt, sk, svt, *, tq):
    B, _, S = sqt.shape
    nb = S // WINDOW
    return pl.pallas_call(
        functools.partial(_swa_kernel, tq=tq),
        grid=(B, S // tq),
        in_specs=[
            pl.BlockSpec(sink_row.shape, lambda b, i: (0, 0)),
            pl.BlockSpec((1, SWA_Q_COLS, tq), lambda b, i: (b, 0, i)),
            pl.BlockSpec((1, nb, WINDOW, SWA_KV_COLS), lambda b, i: (b, 0, 0, 0)),
            pl.BlockSpec((1, nb, SWA_VT_ROWS, WINDOW), lambda b, i: (b, 0, 0, 0)),
        ],
        out_specs=pl.BlockSpec((1, tq, SWA_Q_COLS), lambda b, i: (b, i, 0)),
        out_shape=jax.ShapeDtypeStruct((B, S, SWA_Q_COLS), BF16),
        compiler_params=pltpu.CompilerParams(
            dimension_semantics=("parallel", "arbitrary"), vmem_limit_bytes=VMEM_LIMIT_BYTES),
        name="swa_attn",
    )(sink_row, sqt, sk, svt)


def _pack_bf16_pairs(x):
    n = x.shape[1] // 2
    lo = lax.bitcast_convert_type(x[:, :n].astype(BF16).astype(F32), jnp.uint32)
    hi = lax.bitcast_convert_type(x[:, n:].astype(BF16).astype(F32), jnp.uint32)
    return (lo >> 16) | (hi & jnp.uint32(0xFFFF0000))


def _unpack_bf16_pairs(w):
    lo = lax.bitcast_convert_type(w << 16, F32)
    hi = lax.bitcast_convert_type(w & jnp.uint32(0xFFFF0000), F32)
    return jnp.concatenate([lo, hi], axis=1).astype(BF16)


def _mix_kernel(x_ref, od_ref, os_ref, wo_ref, g2_ref, wr_ref, br_ref, x1_ref, n2_ref, rt_ref, cnt_ref):
    tm = x_ref.shape[1]
    lane = lax.broadcasted_iota(jnp.int32, (tm, ROUTER_COLS), 1)
    lane_f = lane.astype(F32)
    big = float(ROUTER_COLS)
    mixed = jnp.concatenate([od_ref[0], os_ref[0]], axis=1)
    h = x_ref[0] + jnp.dot(mixed, wo_ref[...], preferred_element_type=F32)
    x1_ref[0] = h
    n2 = h * lax.rsqrt(jnp.mean(h * h, axis=-1, keepdims=True) + EPS) * g2_ref[...]
    n2_ref[0] = _pack_bf16_pairs(n2)
    n2_hi = n2.astype(BF16)
    n2_lo = (n2 - n2_hi.astype(F32)).astype(BF16)
    parts = jnp.dot(jnp.concatenate([n2_hi, n2_lo], axis=0), wr_ref[...],
                    preferred_element_type=F32)
    logits = ((parts[:tm, :ROUTER_COLS] + parts[tm:, ROUTER_COLS:])
              + (parts[:tm, ROUTER_COLS:] + parts[tm:, :ROUTER_COLS])) + br_ref[...]
    gl = jnp.where(lane < N_GROUPS, logits, -jnp.inf)
    gm = jnp.max(gl, axis=-1, keepdims=True)
    p_top = 1.0 / jnp.sum(jnp.exp(gl - gm), axis=-1, keepdims=True)
    g_idx = jnp.min(jnp.where(gl == gm, lane_f, big), axis=-1, keepdims=True)
    e_lo = N_GROUPS + EXPERTS_PER_GROUP * g_idx
    el = jnp.where((lane_f >= e_lo) & (lane_f < e_lo + EXPERTS_PER_GROUP), logits, -jnp.inf)
    v1 = jnp.max(el, axis=-1, keepdims=True)
    i1 = jnp.min(jnp.where(el == v1, lane_f, big), axis=-1, keepdims=True)
    el2 = jnp.where(lane_f == i1, -jnp.inf, el)
    v2 = jnp.max(el2, axis=-1, keepdims=True)
    i2 = jnp.min(jnp.where(el2 == v2, lane_f, big), axis=-1, keepdims=True)
    e21 = jnp.exp(v2 - v1)
    gate1 = p_top / (1.0 + e21)
    gate2 = p_top * e21 / (1.0 + e21)
    rt_ref[0] = jnp.where(lane == 0, i1 - N_GROUPS,
                jnp.where(lane == 1, i2 - N_GROUPS,
                jnp.where(lane == 2, gate1, jnp.where(lane == 3, gate2, 0.0))))
    chosen = ((lane_f == i1 - N_GROUPS) | (lane_f == i2 - N_GROUPS)).astype(F32)
    cnt_ref[0, 0] = jnp.broadcast_to(jnp.sum(chosen, axis=0, keepdims=True), cnt_ref.shape[2:])


def _mix_call(x, o_diff, o_swa, w_out, g2, w_router, b_router, *, tm):
    B, S, D = x.shape
    const = lambda b, i: (0, 0)
    row = lambda b, i: (b, i, 0)
    nt = S // tm
    return pl.pallas_call(
        _mix_kernel,
        grid=(B, nt),
        in_specs=[
            pl.BlockSpec((1, tm, D), row),
            pl.BlockSpec((1, tm, DIFF_V_COLS), row),
            pl.BlockSpec((1, tm, SWA_Q_COLS), row),
            pl.BlockSpec(w_out.shape, const),
            pl.BlockSpec((1, D), const),
            pl.BlockSpec(w_router.shape, const),
            pl.BlockSpec((1, ROUTER_COLS), const),
        ],
        out_specs=(pl.BlockSpec((1, tm, D), row), pl.BlockSpec((1, tm, D // 2), row),
                   pl.BlockSpec((1, tm, ROUTER_COLS), row),
                   pl.BlockSpec((1, 1, SUBLANES, ROUTER_COLS), lambda b, i: (b, i, 0, 0))),
        out_shape=(jax.ShapeDtypeStruct((B, S, D), F32), jax.ShapeDtypeStruct((B, S, D // 2), jnp.uint32),
                   jax.ShapeDtypeStruct((B, S, ROUTER_COLS), F32),
                   jax.ShapeDtypeStruct((B, nt, SUBLANES, ROUTER_COLS), F32)),
        compiler_params=pltpu.CompilerParams(
            dimension_semantics=("parallel", "parallel"), vmem_limit_bytes=VMEM_LIMIT_BYTES),
        name="outproj_router",
    )(x, o_diff, o_swa, w_out, g2, w_router, b_router)


def _slot_kernel(rt_ref, base_ref, dest_ref, *, tm):
    rt_t = rt_ref[...].T
    e1 = rt_t[0:1].astype(jnp.int32)
    e2 = rt_t[1:2].astype(jnp.int32)
    eid = lax.broadcasted_iota(jnp.int32, (N_EXPERTS, tm), 0)
    oh1 = eid == e1
    oh2 = eid == e2
    earlier = (lax.broadcasted_iota(jnp.int32, (tm, tm), 0)
               < lax.broadcasted_iota(jnp.int32, (tm, tm), 1)).astype(BF16)
    before = jnp.dot((oh1 | oh2).astype(BF16), earlier, preferred_element_type=F32)
    slot = before + base_ref[0][:, 0:1]
    d1 = jnp.sum(jnp.where(oh1, slot, 0.0), axis=0, keepdims=True).astype(jnp.int32)
    d2 = jnp.sum(jnp.where(oh2, slot, 0.0), axis=0, keepdims=True).astype(jnp.int32)
    dest_ref[0] = jnp.concatenate([d1, d2, jnp.zeros((SUBLANES - TOP_K, tm), jnp.int32)], axis=0)


def _slot_call(rt, tile_base, *, tm):
    nt = rt.shape[0] // tm
    return pl.pallas_call(
        functools.partial(_slot_kernel, tm=tm),
        grid=(nt,),
        in_specs=[
            pl.BlockSpec((tm, ROUTER_COLS), lambda t: (t, 0)),
            pl.BlockSpec((1, N_EXPERTS, LANES), lambda t: (t, 0, 0)),
        ],
        out_specs=pl.BlockSpec((1, SUBLANES, tm), lambda t: (t, 0, 0)),
        out_shape=jax.ShapeDtypeStruct((nt, SUBLANES, tm), jnp.int32),
        compiler_params=pltpu.CompilerParams(dimension_semantics=("parallel",)),
        name="moe_slots",
    )(rt, tile_base)


SC_ROW_CHUNK = 64


def _sc_workers():
    info = plsc.get_sparse_core_info()
    return info.num_cores, info.num_cores * info.num_subcores


def _sc_scatter_rows(rows, idx, n_out):
    n, width = rows.shape
    n_cores, n_workers = _sc_workers()
    n_chunks = n // SC_ROW_CHUNK
    per_worker = n_chunks // n_workers
    assert n_chunks % (2 * n_workers) == 0
    mesh = plsc.VectorSubcoreMesh(core_axis_name="c", subcore_axis_name="s")

    @functools.partial(
        pl.kernel, mesh=mesh,
        out_type=jax.ShapeDtypeStruct((n_out, width), rows.dtype),
        scratch_types=[pltpu.VMEM((SC_ROW_CHUNK,), jnp.int32)] * (2 * TOP_K)
        + [pltpu.VMEM((SC_ROW_CHUNK, width), rows.dtype)] * 2
        + [pltpu.SemaphoreType.DMA] * (2 + 2 * TOP_K),
    )
    def scatter(rows_hbm, idx_hbm, out_hbm, *scratch):
        idx_v = scratch[:2 * TOP_K]
        rows_v = scratch[2 * TOP_K:2 * TOP_K + 2]
        load_sems, store_sems = scratch[-(2 + 2 * TOP_K):-2 * TOP_K], scratch[-2 * TOP_K:]
        worker = lax.axis_index("s") * n_cores + lax.axis_index("c")

        @pl.loop(0, per_worker, step=2)
        def _(i):
            loads = []
            for half in range(2):
                c = worker * per_worker + i + half
                src = rows_hbm.at[pl.ds(pl.multiple_of(c * SC_ROW_CHUNK, SC_ROW_CHUNK), SC_ROW_CHUNK)]
                loads.append(pltpu.async_copy(src, rows_v[half], load_sems[half]))
                for k in range(TOP_K):
                    pltpu.sync_copy(idx_hbm.at[k, c], idx_v[half * TOP_K + k])
            stores = []
            for half in range(2):
                loads[half].wait()
                for k in range(TOP_K):
                    j = half * TOP_K + k
                    stores.append(pltpu.async_copy(rows_v[half], out_hbm.at[idx_v[j]], store_sems[j]))
            for store in stores:
                store.wait()

    return scatter(rows, idx)


def _expert_kernel(be_ref, nvalid_ref, next_ref, xs_ref, wg_hbm, wu_hbm, wd_hbm, y_ref,
                   wg_st, wu_st, wd_st, wg_b, wu_b, wd_b, slot_ref, sems):
    b = pl.program_id(0)
    n_valid = nvalid_ref[b]
    sources, staged, cast = (wg_hbm, wu_hbm, wd_hbm), (wg_st, wu_st, wd_st), (wg_b, wu_b, wd_b)

    def weight_copies(expert, slot):
        return [pltpu.make_async_copy(src.at[expert], dst.at[slot], sems.at[slot, i])
                for i, (src, dst) in enumerate(zip(sources, staged))]

    @pl.when(n_valid > 0)
    def _():
        @pl.when((b == 0) | (be_ref[b] != be_ref[jnp.maximum(b - 1, 0)]))
        def _():
            @pl.when(b == 0)
            def _():
                slot_ref[0] = 0
                for copy in weight_copies(be_ref[0], 0):
                    copy.start()

            slot = slot_ref[0]
            for copy in weight_copies(be_ref[b], slot):
                copy.wait()
            for dst, src in zip(cast, staged):
                dst[...] = src[slot].astype(BF16)

            @pl.when(next_ref[b] >= 0)
            def _():
                for copy in weight_copies(next_ref[b], 1 - slot):
                    copy.start(priority=1)

            slot_ref[0] = 1 - slot

        for c in range(EXPERT_BLOCK // EXPERT_CHUNK):
            rows = pl.ds(c * EXPERT_CHUNK, EXPERT_CHUNK)
            row_id = c * EXPERT_CHUNK + lax.broadcasted_iota(jnp.int32, (EXPERT_CHUNK, xs_ref.shape[1]), 0)
            packed = jnp.where(row_id < n_valid, xs_ref[rows, :], jnp.uint32(0))
            xb = _unpack_bf16_pairs(packed)
            gate = jnp.dot(xb, wg_b[...], preferred_element_type=F32)
            up = jnp.dot(xb, wu_b[...], preferred_element_type=F32)
            hid = (gate * jax.nn.sigmoid(gate) * up).astype(BF16)
            y_ref[rows, :] = _pack_bf16_pairs(jnp.dot(hid, wd_b[...], preferred_element_type=F32))

    @pl.when(n_valid == 0)
    def _():
        y_ref[...] = jnp.zeros_like(y_ref)


def _expert_call(block_expert, n_valid, next_expert, xs, w_gate, w_up, w_down):
    P = xs.shape[0]
    NB = P // EXPERT_BLOCK
    E, D, F = w_gate.shape
    grid_spec = pltpu.PrefetchScalarGridSpec(
        num_scalar_prefetch=3,
        grid=(NB,),
        in_specs=[
            pl.BlockSpec((EXPERT_BLOCK,) + xs.shape[1:], lambda b, *_: (b, 0)),
            pl.BlockSpec(memory_space=pl.ANY),
            pl.BlockSpec(memory_space=pl.ANY),
            pl.BlockSpec(memory_space=pl.ANY),
        ],
        out_specs=pl.BlockSpec((EXPERT_BLOCK, D // 2), lambda b, *_: (b, 0)),
        scratch_shapes=[
            pltpu.VMEM((2, D, F), F32),
            pltpu.VMEM((2, D, F), F32),
            pltpu.VMEM((2, F, D), F32),
            pltpu.VMEM((D, F), BF16),
            pltpu.VMEM((D, F), BF16),
            pltpu.VMEM((F, D), BF16),
            pltpu.SMEM((1,), jnp.int32),
            pltpu.SemaphoreType.DMA((2, 3)),
        ],
    )
    return pl.pallas_call(
        _expert_kernel,
        grid_spec=grid_spec,
        out_shape=jax.ShapeDtypeStruct((P, D // 2), jnp.uint32),
        compiler_params=pltpu.CompilerParams(
            dimension_semantics=("arbitrary",), vmem_limit_bytes=VMEM_LIMIT_BYTES),
        name="moe_experts",
    )(block_expert, n_valid, next_expert, xs, w_gate, w_up, w_down)


def _sc_gather_rows(table, idx):
    n_rows, width = idx.shape[0], table.shape[1]
    n_cores, n_workers = _sc_workers()
    per_worker = n_rows // n_workers
    assert n_rows % (n_workers * 2 * SC_ROW_CHUNK) == 0
    mesh = plsc.VectorSubcoreMesh(core_axis_name="c", subcore_axis_name="s")

    @functools.partial(
        pl.kernel, mesh=mesh,
        out_type=jax.ShapeDtypeStruct((n_rows, width), table.dtype),
        scratch_types=[pltpu.VMEM((SC_ROW_CHUNK,), jnp.int32)] * 2
        + [pltpu.VMEM((SC_ROW_CHUNK, width), table.dtype)] * 2
        + [pltpu.SemaphoreType.DMA] * 4,
    )
    def gather(table_hbm, idx_hbm, out_hbm, idx_a, idx_b, rows_a, rows_b, sem_ga, sem_gb, sem_wa, sem_wb):
        worker = lax.axis_index("s") * n_cores + lax.axis_index("c")
        base = worker * per_worker

        @pl.loop(0, per_worker // SC_ROW_CHUNK, step=2)
        def _(c):
            off_a = pl.multiple_of(base + c * SC_ROW_CHUNK, SC_ROW_CHUNK)
            off_b = pl.multiple_of(off_a + SC_ROW_CHUNK, SC_ROW_CHUNK)
            pltpu.sync_copy(idx_hbm.at[pl.ds(off_a, SC_ROW_CHUNK)], idx_a)
            pltpu.sync_copy(idx_hbm.at[pl.ds(off_b, SC_ROW_CHUNK)], idx_b)
            gather_a = pltpu.async_copy(table_hbm.at[idx_a], rows_a, sem_ga)
            gather_b = pltpu.async_copy(table_hbm.at[idx_b], rows_b, sem_gb)
            gather_a.wait()
            write_a = pltpu.async_copy(rows_a, out_hbm.at[pl.ds(off_a, SC_ROW_CHUNK)], sem_wa)
            gather_b.wait()
            write_b = pltpu.async_copy(rows_b, out_hbm.at[pl.ds(off_b, SC_ROW_CHUNK)], sem_wb)
            write_a.wait()
            write_b.wait()

    return gather(table, idx)


def _combine_kernel(x1_ref, rt_ref, y_ref, fg_ref, o_ref, *, final_norm):
    rt = rt_ref[...]
    y1 = _unpack_bf16_pairs(y_ref[0, 0]).astype(F32)
    y2 = _unpack_bf16_pairs(y_ref[0, 1]).astype(F32)
    h = x1_ref[...] + rt[:, 2:3] * y1 + rt[:, 3:4] * y2
    if final_norm:
        h = h * lax.rsqrt(jnp.mean(h * h, axis=-1, keepdims=True) + EPS) * fg_ref[...]
    o_ref[...] = h


def _combine_call(x1, rt, ysg, final_g, *, tm, final_norm):
    T, D = x1.shape
    return pl.pallas_call(
        functools.partial(_combine_kernel, final_norm=final_norm),
        grid=(T // tm,),
        in_specs=[
            pl.BlockSpec((tm, D), lambda t: (t, 0)),
            pl.BlockSpec((tm, ROUTER_COLS), lambda t: (t, 0)),
            pl.BlockSpec((1, TOP_K, tm, D // 2), lambda t: (t, 0, 0, 0)),
            pl.BlockSpec((1, D), lambda t: (0, 0)),
        ],
        out_specs=pl.BlockSpec((tm, D), lambda t: (t, 0)),
        out_shape=jax.ShapeDtypeStruct((T, D), F32),
        compiler_params=pltpu.CompilerParams(
            dimension_semantics=("parallel",), vmem_limit_bytes=VMEM_LIMIT_BYTES),
        name="moe_combine",
    )(x1, rt, ysg, final_g)


def _slot_layout(tile_counts, n_assign):
    NB = -(-n_assign // EXPERT_BLOCK) + N_EXPERTS
    n_tiles = tile_counts.shape[0]
    tc = tile_counts.astype(F32)
    hp = lax.Precision.HIGHEST
    counts = jnp.sum(tc, axis=0)
    padded = jnp.ceil(counts / EXPERT_BLOCK) * EXPERT_BLOCK
    upper = (jnp.arange(N_EXPERTS)[:, None] < jnp.arange(N_EXPERTS)[None, :]).astype(F32)
    pad_start = jnp.dot(padded, upper, precision=hp)
    pad_end = pad_start + padded
    lower = (jnp.arange(n_tiles)[:, None] > jnp.arange(n_tiles)[None, :]).astype(F32)
    tile_base = pad_start[None, :] + jnp.dot(lower, tc, precision=hp)
    block_start = jnp.arange(NB, dtype=F32) * EXPERT_BLOCK
    block_expert = jnp.minimum(jnp.sum((pad_end[None, :] <= block_start[:, None]).astype(jnp.int32), axis=1),
                               N_EXPERTS - 1)
    mine = block_expert[:, None] == jnp.arange(N_EXPERTS)[None, :]
    run_end = jnp.sum(jnp.where(mine, (pad_start + counts)[None, :], 0.0), axis=1)
    n_valid = jnp.clip(run_end - block_start, 0, EXPERT_BLOCK).astype(jnp.int32)
    eid = jnp.arange(N_EXPERTS)
    later_nonempty = (eid[None, :] > eid[:, None]) & (counts[None, :] > 0)
    next_nonempty = jnp.min(jnp.where(later_nonempty, eid[None, :], N_EXPERTS), axis=1)
    next_nonempty = jnp.where(next_nonempty < N_EXPERTS, next_nonempty, -1)
    next_expert = jnp.sum(jnp.where(mine, next_nonempty[None, :], 0), axis=1).astype(jnp.int32)
    return NB, block_expert.astype(jnp.int32), n_valid, next_expert, tile_base


def _rope_tables(S):
    inv = 1.0 / (ROPE_THETA ** (jnp.arange(0, HEAD_DIM, 2, dtype=F32) / HEAD_DIM))
    ang_t = inv[:, None] * jnp.arange(S, dtype=F32)[None, :]
    return jnp.cos(ang_t), jnp.sin(ang_t)


def _tiles(S):
    tile = min(512, S)
    return tile, tile, min(1024, S), min(1024, S), min(1024, S)


def kernel(x, norm1_g, w_in, lambda_q1, lambda_k1, lambda_q2, lambda_k2, subln_g, sinks, w_out,
           norm2_g, w_router_group, b_router_group, w_router_expert, b_router_expert,
           w_gate, w_up, w_down, final_g):
    B, S, D = x.shape
    T = B * S
    depth = w_in.shape[0]
    tq, tk, tm_proj, tm_tok, tq_swa = _tiles(S)
    qscale = HEAD_DIM ** -0.5 * math.log2(math.e)
    cos_t, sin_t = _rope_tables(S)

    c0 = DIFF_QK_COLS
    c1 = 2 * DIFF_QK_COLS
    c2 = c1 + DIFF_V_COLS
    c3 = c2 + SWA_Q_COLS
    c4 = c3 + SWA_KV_COLS
    for l in range(depth):
        lambda_init = 0.8 - 0.6 * math.exp(-0.3 * l)
        w = w_in[l]
        w_nat = jnp.concatenate([w[:, c0:c1], w[:, c3:c4]], axis=1).astype(BF16)
        w_tr = jnp.concatenate([w[:, :c0] * qscale, w[:, c1:c2], w[:, c2:c3] * qscale, w[:, c4:]],
                               axis=1).astype(BF16).T
        dqt, dk, dvt, sqt, sk, svt = _proj_call(
            x, norm1_g[l][None, :], w_nat, w_tr, cos_t, sin_t, tm=tm_proj, tk=tk)

        lam_p = jnp.stack([lambda_q1[l], lambda_k1[l], lambda_q2[l], lambda_k2[l]]).astype(F32)
        o_diff = _diff_call(lam_p, dqt, dk, dvt, subln_g[l][None, :].astype(F32),
                            tq=tq, tk=tk, lambda_init=lambda_init)
        sink_row = jnp.repeat(sinks[l].astype(F32) * math.log2(math.e), WINDOW)[None, :]
        o_swa = _swa_call(sink_row, sqt, sk, svt, tq=tq_swa)

        wo_b = w_out[l].astype(BF16)
        w_router = jnp.zeros((D, ROUTER_COLS), F32)
        w_router = w_router.at[:, :N_GROUPS].set(w_router_group[l])
        w_router = w_router.at[:, N_GROUPS:N_GROUPS + N_EXPERTS].set(w_router_expert[l])
        w_router_hi = w_router.astype(BF16)
        w_router_lo = (w_router - w_router_hi.astype(F32)).astype(BF16)
        w_router = jnp.concatenate([w_router_hi, w_router_lo], axis=1)
        b_router = jnp.zeros((1, ROUTER_COLS), F32)
        b_router = b_router.at[0, :N_GROUPS].set(b_router_group[l])
        b_router = b_router.at[0, N_GROUPS:N_GROUPS + N_EXPERTS].set(b_router_expert[l])
        x1, n2p, rt, cnt = _mix_call(x, o_diff, o_swa, wo_b, norm2_g[l][None, :], w_router, b_router, tm=tm_tok)

        rt2 = rt.reshape(T, ROUTER_COLS)
        tile_counts = cnt[:, :, 0, :N_EXPERTS].reshape(T // tm_tok, N_EXPERTS).astype(jnp.int32)
        NB, block_expert, n_valid, next_expert, tile_base = _slot_layout(tile_counts, T * TOP_K)
        tile_base = jnp.broadcast_to(tile_base.astype(F32)[:, :, None], (T // tm_tok, N_EXPERTS, LANES))
        dest = _slot_call(rt2, tile_base, tm=tm_tok)
        scatter_idx = jnp.swapaxes(dest[:, :TOP_K, :], 0, 1).reshape(TOP_K, T // SC_ROW_CHUNK, SC_ROW_CHUNK)
        xs = _sc_scatter_rows(n2p.reshape(T, D // 2), scatter_idx, NB * EXPERT_BLOCK)
        ys = _expert_call(block_expert, n_valid, next_expert, xs, w_gate[l], w_up[l], w_down[l])
        ysg = _sc_gather_rows(ys, dest[:, :TOP_K, :].reshape(T * TOP_K))
        x = _combine_call(x1.reshape(T, D), rt2, ysg.reshape(T // tm_tok, TOP_K, tm_tok, D // 2),
                          final_g[None, :], tm=tm_tok, final_norm=(l == depth - 1)).reshape(B, S, D)
    return x
```

```python
import functools
import math

import jax
import jax.numpy as jnp
from jax import lax
from jax.experimental import pallas as pl
from jax.experimental.pallas import tpu as pltpu
from jax.experimental.pallas import tpu_sc as plsc

HEAD_DIM = 64
DIFF_HEADS = 4
DIFF_V_DIM = 2 * HEAD_DIM
SWA_Q_HEADS = 8
SWA_KV_HEADS = 2
SWA_GROUP = SWA_Q_HEADS // SWA_KV_HEADS
WINDOW = 128
ROPE_THETA = 10000.0
N_GROUPS = 4
EXPERTS_PER_GROUP = 8
N_EXPERTS = N_GROUPS * EXPERTS_PER_GROUP
TOP_K = 2
EXPERT_BLOCK = 512
EXPERT_CHUNK = 256
EPS = 1e-6
NEG = -1e30

DIFF_QK_COLS = DIFF_HEADS * 2 * HEAD_DIM
DIFF_V_COLS = DIFF_HEADS * DIFF_V_DIM
SWA_Q_COLS = SWA_Q_HEADS * HEAD_DIM
SWA_KV_COLS = SWA_KV_HEADS * HEAD_DIM
LANES = 128
SUBLANES = 8
BF16_SUBLANES = 16
VMEM_LIMIT_BYTES = 48 * 1024 * 1024
VT_ROWS = DIFF_V_DIM + BF16_SUBLANES
SWA_VT_ROWS = SWA_KV_COLS + BF16_SUBLANES
ROUTER_COLS = LANES
DIFF_UNROLL = 4
DIFF_S_BUFS = 4
DIFF_Q_TILES = 4

BF16 = jnp.bfloat16
F32 = jnp.float32


def _rope_lanes(x, cos_l, sin_l, first_half):
    rot = jnp.where(first_half, pltpu.roll(x, 96, 1), pltpu.roll(x, 32, 1))
    return x * cos_l + rot * sin_l


def _proj_kernel(x_ref, g_ref, wnat_ref, wtr_ref, cost_ref, sint_ref,
                 dqt_ref, dk_ref, dvt_ref, sqt_ref, sk_ref, svt_ref, *, tk):
    x = x_ref[0]
    tm = x.shape[0]
    n1 = x * lax.rsqrt(jnp.mean(x * x, axis=-1, keepdims=True) + EPS) * g_ref[...]
    n1b = n1.astype(BF16)
    nat = jnp.dot(n1b, wnat_ref[...], preferred_element_type=F32)
    tr = lax.dot_general(wtr_ref[...], n1b, (((1,), (1,)), ((), ())),
                         preferred_element_type=F32)

    c32, s32 = cost_ref[...].T, sint_ref[...].T
    cos_l = jnp.concatenate([c32] * (LANES // c32.shape[1]), axis=1)
    sin_l = jnp.concatenate([-s32, s32] * (LANES // HEAD_DIM), axis=1)
    first_half = (lax.broadcasted_iota(jnp.int32, (tm, LANES), 1) & (HEAD_DIM - 1)) < HEAD_DIM // 2
    for h in range(DIFF_HEADS):
        slab = nat[:, h * LANES:(h + 1) * LANES]
        dk_ref[0, h] = _rope_lanes(slab, cos_l, sin_l, first_half).astype(BF16)
    sk = _rope_lanes(nat[:, DIFF_QK_COLS:DIFF_QK_COLS + LANES], cos_l, sin_l, first_half).astype(BF16)
    for c in range(tm // WINDOW):
        sk_ref[0, c] = sk[c * WINDOW:(c + 1) * WINDOW]

    cos_t, sin_t = cost_ref[...], sint_ref[...]
    half = HEAD_DIM // 2

    def rope_rows(r0):
        x1 = tr[r0:r0 + half]
        x2 = tr[r0 + half:r0 + HEAD_DIM]
        return (x1 * cos_t - x2 * sin_t).astype(BF16), (x1 * sin_t + x2 * cos_t).astype(BF16)

    for h in range(DIFF_HEADS):
        for c in range(2):
            lo, hi = rope_rows(h * 2 * HEAD_DIM + c * HEAD_DIM)
            dqt_ref[0, h, c * HEAD_DIM:c * HEAD_DIM + half] = lo
            dqt_ref[0, h, c * HEAD_DIM + half:(c + 1) * HEAD_DIM] = hi
    ones_rows = (lax.broadcasted_iota(jnp.int32, (BF16_SUBLANES, tk), 0) == 0).astype(BF16)
    for h in range(DIFF_HEADS):
        r0 = DIFF_QK_COLS + h * DIFF_V_DIM
        for c in range(tm // tk):
            dvt_ref[0, h, c, :DIFF_V_DIM] = tr[r0:r0 + DIFF_V_DIM, c * tk:(c + 1) * tk].astype(BF16)
            dvt_ref[0, h, c, DIFF_V_DIM:] = ones_rows

    r0 = DIFF_QK_COLS + DIFF_V_COLS
    for h in range(SWA_Q_HEADS):
        lo, hi = rope_rows(r0 + h * HEAD_DIM)
        sqt_ref[0, h * HEAD_DIM:h * HEAD_DIM + half] = lo
        sqt_ref[0, h * HEAD_DIM + half:(h + 1) * HEAD_DIM] = hi
    r0 += SWA_Q_COLS
    for c in range(tm // WINDOW):
        svt_ref[0, c, :SWA_KV_COLS] = tr[r0:r0 + SWA_KV_COLS, c * WINDOW:(c + 1) * WINDOW].astype(BF16)
        svt_ref[0, c, SWA_KV_COLS:] = ones_rows[:, :WINDOW]


def _proj_call(x, g1, w_nat, w_tr, cos_t, sin_t, *, tm, tk):
    B, S, D = x.shape
    nkv = S // tk
    grid = (B, S // tm)
    const = lambda b, i: (0, 0)
    out_shape = (
        jax.ShapeDtypeStruct((B, DIFF_HEADS, 2 * HEAD_DIM, S), BF16),
        jax.ShapeDtypeStruct((B, DIFF_HEADS, S, 2 * HEAD_DIM), BF16),
        jax.ShapeDtypeStruct((B, DIFF_HEADS, nkv, VT_ROWS, tk), BF16),
        jax.ShapeDtypeStruct((B, SWA_Q_COLS, S), BF16),
        jax.ShapeDtypeStruct((B, S // WINDOW, WINDOW, SWA_KV_COLS), BF16),
        jax.ShapeDtypeStruct((B, S // WINDOW, SWA_VT_ROWS, WINDOW), BF16),
    )
    return pl.pallas_call(
        functools.partial(_proj_kernel, tk=tk),
        grid=grid,
        in_specs=[
            pl.BlockSpec((1, tm, D), lambda b, i: (b, i, 0)),
            pl.BlockSpec((1, D), const),
            pl.BlockSpec(w_nat.shape, const),
            pl.BlockSpec(w_tr.shape, const),
            pl.BlockSpec((HEAD_DIM // 2, tm), lambda b, i: (0, i)),
            pl.BlockSpec((HEAD_DIM // 2, tm), lambda b, i: (0, i)),
        ],
        out_specs=(
            pl.BlockSpec((1, DIFF_HEADS, 2 * HEAD_DIM, tm), lambda b, i: (b, 0, 0, i)),
            pl.BlockSpec((1, DIFF_HEADS, tm, 2 * HEAD_DIM), lambda b, i: (b, 0, i, 0)),
            pl.BlockSpec((1, DIFF_HEADS, tm // tk, VT_ROWS, tk), lambda b, i: (b, 0, i, 0, 0)),
            pl.BlockSpec((1, SWA_Q_COLS, tm), lambda b, i: (b, 0, i)),
            pl.BlockSpec((1, tm // WINDOW, WINDOW, SWA_KV_COLS), lambda b, i: (b, i, 0, 0)),
            pl.BlockSpec((1, tm // WINDOW, SWA_VT_ROWS, WINDOW), lambda b, i: (b, i, 0, 0)),
        ),
        out_shape=out_shape,
        compiler_params=pltpu.CompilerParams(
            dimension_semantics=("parallel", "parallel"), vmem_limit_bytes=VMEM_LIMIT_BYTES),
        name="proj_rope",
    )(x, g1, w_nat, w_tr, cos_t, sin_t)


def _diff_kernel(lam_ref, qt_ref, k_ref, vt_ref, g_ref, o_ref, *scratch, tq, tk, lambda_init):
    step = pl.program_id(2)
    s_bufs = scratch[:DIFF_S_BUFS]
    top_bufs = scratch[DIFF_S_BUFS:2 * DIFF_S_BUFS]
    state = scratch[2 * DIFF_S_BUFS:2 * DIFF_S_BUFS + 2 * DIFF_Q_TILES]
    bias_ref = scratch[-1]

    @pl.when(step == 0)
    def _():
        r = lax.broadcasted_iota(jnp.int32, (tk, 2 * tq), 0)
        c = lax.broadcasted_iota(jnp.int32, (tk, 2 * tq), 1) & (tq - 1)
        bias_ref[...] = jnp.where(r <= c, 0.0, NEG).astype(F32)

    lam_p = lam_ref[...]
    lam = (jnp.exp(jnp.sum(lam_p[0:1] * lam_p[1:2], axis=-1, keepdims=True))
           - jnp.exp(jnp.sum(lam_p[2:3] * lam_p[3:4], axis=-1, keepdims=True)) + lambda_init)

    for sub in range(DIFF_Q_TILES):
        _diff_query_tile(step * DIFF_Q_TILES + sub, qt_ref[0, 0, :, sub * tq:(sub + 1) * tq], k_ref, vt_ref,
                         g_ref, o_ref.at[0, pl.ds(sub * tq, tq), :], s_bufs, top_bufs,
                         state[2 * sub], state[2 * sub + 1], bias_ref, lam,
                         tq=tq, tk=tk, lambda_init=lambda_init)


def _diff_query_tile(i, qt, k_ref, vt_ref, g_ref, o_ref, s_bufs, top_bufs, m_ref, acc_ref, bias_ref, lam,
                     *, tq, tk, lambda_init):
    z = jnp.zeros((HEAD_DIM, tq), BF16)
    qw = jnp.concatenate([jnp.concatenate([qt[:HEAD_DIM], z], axis=1),
                          jnp.concatenate([z, qt[HEAD_DIM:]], axis=1)], axis=0)

    def scores(j, par):
        kt = k_ref[0, 0, pl.ds(pl.multiple_of(j * tk, tk), tk), :]
        s = jnp.dot(kt, qw, preferred_element_type=F32)
        s_bufs[par][...] = s
        top_bufs[par][...] = jnp.max(s, axis=0, keepdims=True)

    def absorb(j, par, masked):
        s = s_bufs[par][...]
        if masked:
            s = s + bias_ref[...]
            top = jnp.max(s, axis=0, keepdims=True)
        else:
            top = top_bufs[par][...]
        m = m_ref[...]
        m_new = jnp.maximum(m, top)
        alpha = jnp.exp2(m - m_new)
        p = jnp.exp2(s - m_new).astype(BF16)
        m_ref[...] = m_new
        pv = jnp.dot(vt_ref[0, 0, j], p, preferred_element_type=F32)
        acc_ref[...] = alpha * acc_ref[...] + pv

    m_ref[...] = jnp.full(m_ref.shape, NEG, F32)
    acc_ref[...] = jnp.zeros(acc_ref.shape, F32)

    nfull = (i * tq) // tk
    scores(nfull, 0)
    scores(0, 1)
    absorb(nfull, 0, True)

    def group(t, c):
        j = DIFF_UNROLL * t
        for idx in range(DIFF_UNROLL):
            scores(j + idx + 1, (idx + 2) % DIFF_S_BUFS)
            absorb(j + idx, (idx + 1) % DIFF_S_BUFS, False)
        return c

    lax.fori_loop(0, nfull // DIFF_UNROLL, group, 0)

    for rem in range(1, DIFF_UNROLL):
        @pl.when(nfull % DIFF_UNROLL == rem)
        def _():
            first = nfull - rem
            for idx in range(rem):
                if idx + 1 < rem:
                    scores(first + idx + 1, (idx + 2) % DIFF_S_BUFS)
                absorb(first + idx, (idx + 1) % DIFF_S_BUFS, False)

    inv_l = 1.0 / acc_ref[DIFF_V_DIM:DIFF_V_DIM + 1, :]
    o = (acc_ref[:DIFF_V_DIM, :tq] * inv_l[:, :tq]
         - lam * (acc_ref[:DIFF_V_DIM, tq:] * inv_l[:, tq:]))
    o = o * lax.rsqrt(jnp.mean(o * o, axis=0, keepdims=True) + EPS)
    o_ref[...] = (o.T * g_ref[...] * (1.0 - lambda_init)).astype(BF16)


def _diff_call(lam_p, dqt, dk, dvt, subln_g, *, tq, tk, lambda_init):
    B, H, _, S = dqt.shape
    assert tk == tq and S % tk == 0, "the diagonal tile's causal pattern is built for square tiles"
    nkv = S // tk
    tq_step = DIFF_Q_TILES * tq
    assert S % tq_step == 0
    grid = (B, H, S // tq_step)
    return pl.pallas_call(
        functools.partial(_diff_kernel, tq=tq, tk=tk, lambda_init=lambda_init),
        grid=grid,
        in_specs=[
            pl.BlockSpec(lam_p.shape, lambda b, h, i: (0, 0)),
            pl.BlockSpec((1, 1, 2 * HEAD_DIM, tq_step), lambda b, h, i: (b, h, 0, i)),
            pl.BlockSpec((1, 1, S, 2 * HEAD_DIM), lambda b, h, i: (b, h, 0, 0)),
            pl.BlockSpec((1, 1, nkv, VT_ROWS, tk), lambda b, h, i: (b, h, 0, 0, 0)),
            pl.BlockSpec((1, DIFF_V_DIM), lambda b, h, i: (0, 0)),
        ],
        out_specs=pl.BlockSpec((1, tq_step, DIFF_V_DIM), lambda b, h, i: (b, i, h)),
        out_shape=jax.ShapeDtypeStruct((B, S, DIFF_V_COLS), BF16),
        scratch_shapes=[pltpu.VMEM((tk, 2 * tq), F32)] * DIFF_S_BUFS + [
            pltpu.VMEM((1, 2 * tq), F32)] * DIFF_S_BUFS + [
            pltpu.VMEM((1, 2 * tq), F32),
            pltpu.VMEM((VT_ROWS, 2 * tq), F32)] * DIFF_Q_TILES + [
            pltpu.VMEM((tk, 2 * tq), F32),
        ],
        compiler_params=pltpu.CompilerParams(
            dimension_semantics=("parallel", "parallel", "arbitrary"),
            vmem_limit_bytes=VMEM_LIMIT_BYTES),
        name="diff_attn",
    )(lam_p, dqt, dk, dvt, subln_g)


def _swa_kernel(sink_ref, qt_ref, k_ref, vt_ref, o_ref, *, tq):
    i = pl.program_id(1)
    n_cols = SWA_Q_HEADS * WINDOW
    half_cols = n_cols // SWA_KV_HEADS
    sink = sink_ref[...]
    row = lax.broadcasted_iota(jnp.int32, (2 * WINDOW, WINDOW), 0)
    qrel = lax.broadcasted_iota(jnp.int32, (2 * WINDOW, WINDOW), 1)
    band = (row - WINDOW <= qrel) & (row > qrel)
    in_current = row >= WINDOW
    z = jnp.zeros((HEAD_DIM, half_cols), BF16)
    for sub in range(tq // WINDOW):
        n = i * (tq // WINDOW) + sub
        prev = jnp.maximum(n - 1, 0)
        kwin = jnp.concatenate([k_ref[0, prev], k_ref[0, n]], axis=0)
        vtwin = jnp.concatenate([vt_ref[0, prev], vt_ref[0, n]], axis=1)
        qt = qt_ref[0, :, sub * WINDOW:(sub + 1) * WINDOW]
        heads = [qt[h * HEAD_DIM:(h + 1) * HEAD_DIM] for h in range(SWA_Q_HEADS)]
        qw = jnp.concatenate(
            [jnp.concatenate(heads[:SWA_GROUP] + [z], axis=1),
             jnp.concatenate([z] + heads[SWA_GROUP:], axis=1)], axis=0)
        s = jnp.dot(kwin, qw, preferred_element_type=F32)
        valid = band & (in_current | (n >= 1))
        s = jnp.concatenate(
            [jnp.where(valid, s[:, h * WINDOW:(h + 1) * WINDOW], NEG) for h in range(SWA_Q_HEADS)], axis=1)
        m = jnp.maximum(jnp.max(s, axis=0, keepdims=True), sink)
        p = jnp.exp2(s - m).astype(BF16)
        acc = jnp.dot(vtwin, p, preferred_element_type=F32)
        den = acc[SWA_KV_COLS:SWA_KV_COLS + 1] + jnp.exp2(sink - m)
        on = acc[:SWA_KV_COLS] / den
        u = jnp.concatenate([on[:HEAD_DIM, :half_cols], on[HEAD_DIM:, half_cols:]], axis=1)
        for hp in range(SWA_Q_HEADS // 2):
            two = jnp.concatenate([u[:, (2 * hp) * WINDOW:(2 * hp + 1) * WINDOW],
                                   u[:, (2 * hp + 1) * WINDOW:(2 * hp + 2) * WINDOW]], axis=0)
            o_ref[0, sub * WINDOW:(sub + 1) * WINDOW, hp * LANES:(hp + 1) * LANES] = two.T.astype(BF16)


def _swa_call(sink_row, sqt, sk, svt, *, tq):
    B, _, S = sqt.shape
    nb = S // WINDOW
    return pl.pallas_call(
        functools.partial(_swa_kernel, tq=tq),
        grid=(B, S // tq),
        in_specs=[
            pl.BlockSpec(sink_row.shape, lambda b, i: (0, 0)),
            pl.BlockSpec((1, SWA_Q_COLS, tq), lambda b, i: (b, 0, i)),
            pl.BlockSpec((1, nb, WINDOW, SWA_KV_COLS), lambda b, i: (b, 0, 0, 0)),
            pl.BlockSpec((1, nb, SWA_VT_ROWS, WINDOW), lambda b, i: (b, 0, 0, 0)),
        ],
        out_specs=pl.BlockSpec((1, tq, SWA_Q_COLS), lambda b, i: (b, i, 0)),
        out_shape=jax.ShapeDtypeStruct((B, S, SWA_Q_COLS), BF16),
        compiler_params=pltpu.CompilerParams(
            dimension_semantics=("parallel", "arbitrary"), vmem_limit_bytes=VMEM_LIMIT_BYTES),
        name="swa_attn",
    )(sink_row, sqt, sk, svt)


def _pack_bf16_pairs(x):
    n = x.shape[1] // 2
    lo = lax.bitcast_convert_type(x[:, :n].astype(BF16).astype(F32), jnp.uint32)
    hi = lax.bitcast_convert_type(x[:, n:].astype(BF16).astype(F32), jnp.uint32)
    return (lo >> 16) | (hi & jnp.uint32(0xFFFF0000))


def _unpack_bf16_pairs(w):
    lo = lax.bitcast_convert_type(w << 16, F32)
    hi = lax.bitcast_convert_type(w & jnp.uint32(0xFFFF0000), F32)
    return jnp.concatenate([lo, hi], axis=1).astype(BF16)


def _mix_kernel(x_ref, od_ref, os_ref, wo_ref, g2_ref, wr_ref, br_ref, x1_ref, n2_ref, rt_ref, cnt_ref):
    tm = x_ref.shape[1]
    lane = lax.broadcasted_iota(jnp.int32, (tm, ROUTER_COLS), 1)
    lane_f = lane.astype(F32)
    big = float(ROUTER_COLS)
    mixed = jnp.concatenate([od_ref[0], os_ref[0]], axis=1)
    h = x_ref[0] + jnp.dot(mixed, wo_ref[...], preferred_element_type=F32)
    x1_ref[0] = h
    n2 = h * lax.rsqrt(jnp.mean(h * h, axis=-1, keepdims=True) + EPS) * g2_ref[...]
    n2_ref[0] = _pack_bf16_pairs(n2)
    n2_hi = n2.astype(BF16)
    n2_lo = (n2 - n2_hi.astype(F32)).astype(BF16)
    parts = jnp.dot(jnp.concatenate([n2_hi, n2_lo], axis=0), wr_ref[...],
                    preferred_element_type=F32)
    logits = ((parts[:tm, :ROUTER_COLS] + parts[tm:, ROUTER_COLS:])
              + (parts[:tm, ROUTER_COLS:] + parts[tm:, :ROUTER_COLS])) + br_ref[...]
    gl = jnp.where(lane < N_GROUPS, logits, -jnp.inf)
    gm = jnp.max(gl, axis=-1, keepdims=True)
    p_top = 1.0 / jnp.sum(jnp.exp(gl - gm), axis=-1, keepdims=True)
    g_idx = jnp.min(jnp.where(gl == gm, lane_f, big), axis=-1, keepdims=True)
    e_lo = N_GROUPS + EXPERTS_PER_GROUP * g_idx
    el = jnp.where((lane_f >= e_lo) & (lane_f < e_lo + EXPERTS_PER_GROUP), logits, -jnp.inf)
    v1 = jnp.max(el, axis=-1, keepdims=True)
    i1 = jnp.min(jnp.where(el == v1, lane_f, big), axis=-1, keepdims=True)
    el2 = jnp.where(lane_f == i1, -jnp.inf, el)
    v2 = jnp.max(el2, axis=-1, keepdims=True)
    i2 = jnp.min(jnp.where(el2 == v2, lane_f, big), axis=-1, keepdims=True)
    e21 = jnp.exp(v2 - v1)
    gate1 = p_top / (1.0 + e21)
    gate2 = p_top * e21 / (1.0 + e21)
    rt_ref[0] = jnp.where(lane == 0, i1 - N_GROUPS,
                jnp.where(lane == 1, i2 - N_GROUPS,
                jnp.where(lane == 2, gate1, jnp.where(lane == 3, gate2, 0.0))))
    chosen = ((lane_f == i1 - N_GROUPS) | (lane_f == i2 - N_GROUPS)).astype(F32)
    cnt_ref[0, 0] = jnp.broadcast_to(jnp.sum(chosen, axis=0, keepdims=True), cnt_ref.shape[2:])


def _mix_call(x, o_diff, o_swa, w_out, g2, w_router, b_router, *, tm):
    B, S, D = x.shape
    const = lambda b, i: (0, 0)
    row = lambda b, i: (b, i, 0)
    nt = S // tm
    return pl.pallas_call(
        _mix_kernel,
        grid=(B, nt),
        in_specs=[
            pl.BlockSpec((1, tm, D), row),
            pl.BlockSpec((1, tm, DIFF_V_COLS), row),
            pl.BlockSpec((1, tm, SWA_Q_COLS), row),
            pl.BlockSpec(w_out.shape, const),
            pl.BlockSpec((1, D), const),
            pl.BlockSpec(w_router.shape, const),
            pl.BlockSpec((1, ROUTER_COLS), const),
        ],
        out_specs=(pl.BlockSpec((1, tm, D), row), pl.BlockSpec((1, tm, D // 2), row),
                   pl.BlockSpec((1, tm, ROUTER_COLS), row),
                   pl.BlockSpec((1, 1, SUBLANES, ROUTER_COLS), lambda b, i: (b, i, 0, 0))),
        out_shape=(jax.ShapeDtypeStruct((B, S, D), F32), jax.ShapeDtypeStruct((B, S, D // 2), jnp.uint32),
                   jax.ShapeDtypeStruct((B, S, ROUTER_COLS), F32),
                   jax.ShapeDtypeStruct((B, nt, SUBLANES, ROUTER_COLS), F32)),
        compiler_params=pltpu.CompilerParams(
            dimension_semantics=("parallel", "parallel"), vmem_limit_bytes=VMEM_LIMIT_BYTES),
        name="outproj_router",
    )(x, o_diff, o_swa, w_out, g2, w_router, b_router)


def _slot_kernel(rt_ref, base_ref, dest_ref, *, tm):
    rt_t = rt_ref[...].T
    e1 = rt_t[0:1].astype(jnp.int32)
    e2 = rt_t[1:2].astype(jnp.int32)
    eid = lax.broadcasted_iota(jnp.int32, (N_EXPERTS, tm), 0)
    oh1 = eid == e1
    oh2 = eid == e2
    earlier = (lax.broadcasted_iota(jnp.int32, (tm, tm), 0)
               < lax.broadcasted_iota(jnp.int32, (tm, tm), 1)).astype(BF16)
    before = jnp.dot((oh1 | oh2).astype(BF16), earlier, preferred_element_type=F32)
    slot = before + base_ref[0][:, 0:1]
    d1 = jnp.sum(jnp.where(oh1, slot, 0.0), axis=0, keepdims=True).astype(jnp.int32)
    d2 = jnp.sum(jnp.where(oh2, slot, 0.0), axis=0, keepdims=True).astype(jnp.int32)
    dest_ref[0] = jnp.concatenate([d1, d2, jnp.zeros((SUBLANES - TOP_K, tm), jnp.int32)], axis=0)


def _slot_call(rt, tile_base, *, tm):
    nt = rt.shape[0] // tm
    return pl.pallas_call(
        functools.partial(_slot_kernel, tm=tm),
        grid=(nt,),
        in_specs=[
            pl.BlockSpec((tm, ROUTER_COLS), lambda t: (t, 0)),
            pl.BlockSpec((1, N_EXPERTS, LANES), lambda t: (t, 0, 0)),
        ],
        out_specs=pl.BlockSpec((1, SUBLANES, tm), lambda t: (t, 0, 0)),
        out_shape=jax.ShapeDtypeStruct((nt, SUBLANES, tm), jnp.int32),
        compiler_params=pltpu.CompilerParams(dimension_semantics=("parallel",)),
        name="moe_slots",
    )(rt, tile_base)


SC_ROW_CHUNK = 64


def _sc_workers():
    info = plsc.get_sparse_core_info()
    return info.num_cores, info.num_cores * info.num_subcores


def _sc_scatter_rows(rows, idx, n_out):
    n, width = rows.shape
    n_cores, n_workers = _sc_workers()
    n_chunks = n // SC_ROW_CHUNK
    per_worker = n_chunks // n_workers
    assert n_chunks % (2 * n_workers) == 0
    mesh = plsc.VectorSubcoreMesh(core_axis_name="c", subcore_axis_name="s")

    @functools.partial(
        pl.kernel, mesh=mesh,
        out_type=jax.ShapeDtypeStruct((n_out, width), rows.dtype),
        scratch_types=[pltpu.VMEM((SC_ROW_CHUNK,), jnp.int32)] * (2 * TOP_K)
        + [pltpu.VMEM((SC_ROW_CHUNK, width), rows.dtype)] * 2
        + [pltpu.SemaphoreType.DMA] * (2 + 2 * TOP_K),
    )
    def scatter(rows_hbm, idx_hbm, out_hbm, *scratch):
        idx_v = scratch[:2 * TOP_K]
        rows_v = scratch[2 * TOP_K:2 * TOP_K + 2]
        load_sems, store_sems = scratch[-(2 + 2 * TOP_K):-2 * TOP_K], scratch[-2 * TOP_K:]
        worker = lax.axis_index("s") * n_cores + lax.axis_index("c")

        @pl.loop(0, per_worker, step=2)
        def _(i):
            loads = []
            for half in range(2):
                c = worker * per_worker + i + half
                src = rows_hbm.at[pl.ds(pl.multiple_of(c * SC_ROW_CHUNK, SC_ROW_CHUNK), SC_ROW_CHUNK)]
                loads.append(pltpu.async_copy(src, rows_v[half], load_sems[half]))
                for k in range(TOP_K):
                    pltpu.sync_copy(idx_hbm.at[k, c], idx_v[half * TOP_K + k])
            stores = []
            for half in range(2):
                loads[half].wait()
                for k in range(TOP_K):
                    j = half * TOP_K + k
                    stores.append(pltpu.async_copy(rows_v[half], out_hbm.at[idx_v[j]], store_sems[j]))
            for store in stores:
                store.wait()

    return scatter(rows, idx)


def _expert_kernel(be_ref, nvalid_ref, next_ref, xs_ref, wg_hbm, wu_hbm, wd_hbm, y_ref,
                   wg_st, wu_st, wd_st, wg_b, wu_b, wd_b, slot_ref, sems):
    b = pl.program_id(0)
    n_valid = nvalid_ref[b]
    sources, staged, cast = (wg_hbm, wu_hbm, wd_hbm), (wg_st, wu_st, wd_st), (wg_b, wu_b, wd_b)

    def weight_copies(expert, slot):
        return [pltpu.make_async_copy(src.at[expert], dst.at[slot], sems.at[slot, i])
                for i, (src, dst) in enumerate(zip(sources, staged))]

    @pl.when(n_valid > 0)
    def _():
        @pl.when((b == 0) | (be_ref[b] != be_ref[jnp.maximum(b - 1, 0)]))
        def _():
            @pl.when(b == 0)
            def _():
                slot_ref[0] = 0
                for copy in weight_copies(be_ref[0], 0):
                    copy.start()

            slot = slot_ref[0]
            for copy in weight_copies(be_ref[b], slot):
                copy.wait()
            @pl.when(next_ref[b] >= 0)
            def _():
                for copy in weight_copies(next_ref[b], 1 - slot):
                    copy.start()

            for dst, src in zip(cast, staged):
                dst[...] = src[slot].astype(BF16)
            slot_ref[0] = 1 - slot

        for c in range(EXPERT_BLOCK // EXPERT_CHUNK):
            rows = pl.ds(c * EXPERT_CHUNK, EXPERT_CHUNK)
            row_id = c * EXPERT_CHUNK + lax.broadcasted_iota(jnp.int32, (EXPERT_CHUNK, xs_ref.shape[1]), 0)
            packed = jnp.where(row_id < n_valid, xs_ref[rows, :], jnp.uint32(0))
            xb = _unpack_bf16_pairs(packed)
            gate = jnp.dot(xb, wg_b[...], preferred_element_type=F32)
            up = jnp.dot(xb, wu_b[...], preferred_element_type=F32)
            hid = (gate * jax.nn.sigmoid(gate) * up).astype(BF16)
            y_ref[rows, :] = _pack_bf16_pairs(jnp.dot(hid, wd_b[...], preferred_element_type=F32))

    @pl.when(n_valid == 0)
    def _():
        y_ref[...] = jnp.zeros_like(y_ref)


def _expert_call(block_expert, n_valid, next_expert, xs, w_gate, w_up, w_down):
    P = xs.shape[0]
    NB = P // EXPERT_BLOCK
    E, D, F = w_gate.shape
    grid_spec = pltpu.PrefetchScalarGridSpec(
        num_scalar_prefetch=3,
        grid=(NB,),
        in_specs=[
            pl.BlockSpec((EXPERT_BLOCK,) + xs.shape[1:], lambda b, *_: (b, 0)),
            pl.BlockSpec(memory_space=pl.ANY),
            pl.BlockSpec(memory_space=pl.ANY),
            pl.BlockSpec(memory_space=pl.ANY),
        ],
        out_specs=pl.BlockSpec((EXPERT_BLOCK, D // 2), lambda b, *_: (b, 0)),
        scratch_shapes=[
            pltpu.VMEM((2, D, F), F32),
            pltpu.VMEM((2, D, F), F32),
            pltpu.VMEM((2, F, D), F32),
            pltpu.VMEM((D, F), BF16),
            pltpu.VMEM((D, F), BF16),
            pltpu.VMEM((F, D), BF16),
            pltpu.SMEM((1,), jnp.int32),
            pltpu.SemaphoreType.DMA((2, 3)),
        ],
    )
    return pl.pallas_call(
        _expert_kernel,
        grid_spec=grid_spec,
        out_shape=jax.ShapeDtypeStruct((P, D // 2), jnp.uint32),
        compiler_params=pltpu.CompilerParams(
            dimension_semantics=("arbitrary",), vmem_limit_bytes=VMEM_LIMIT_BYTES),
        name="moe_experts",
    )(block_expert, n_valid, next_expert, xs, w_gate, w_up, w_down)


def _sc_gather_rows(table, idx):
    n_rows, width = idx.shape[0], table.shape[1]
    n_cores, n_workers = _sc_workers()
    per_worker = n_rows // n_workers
    assert n_rows % (n_workers * 2 * SC_ROW_CHUNK) == 0
    mesh = plsc.VectorSubcoreMesh(core_axis_name="c", subcore_axis_name="s")

    @functools.partial(
        pl.kernel, mesh=mesh,
        out_type=jax.ShapeDtypeStruct((n_rows, width), table.dtype),
        scratch_types=[pltpu.VMEM((SC_ROW_CHUNK,), jnp.int32)] * 2
        + [pltpu.VMEM((SC_ROW_CHUNK, width), table.dtype)] * 2
        + [pltpu.SemaphoreType.DMA] * 4,
    )
    def gather(table_hbm, idx_hbm, out_hbm, idx_a, idx_b, rows_a, rows_b, sem_ga, sem_gb, sem_wa, sem_wb):
        worker = lax.axis_index("s") * n_cores + lax.axis_index("c")
        base = worker * per_worker

        @pl.loop(0, per_worker // SC_ROW_CHUNK, step=2)
        def _(c):
            off_a = pl.multiple_of(base + c * SC_ROW_CHUNK, SC_ROW_CHUNK)
            off_b = pl.multiple_of(off_a + SC_ROW_CHUNK, SC_ROW_CHUNK)
            pltpu.sync_copy(idx_hbm.at[pl.ds(off_a, SC_ROW_CHUNK)], idx_a)
            pltpu.sync_copy(idx_hbm.at[pl.ds(off_b, SC_ROW_CHUNK)], idx_b)
            gather_a = pltpu.async_copy(table_hbm.at[idx_a], rows_a, sem_ga)
            gather_b = pltpu.async_copy(table_hbm.at[idx_b], rows_b, sem_gb)
            gather_a.wait()
            write_a = pltpu.async_copy(rows_a, out_hbm.at[pl.ds(off_a, SC_ROW_CHUNK)], sem_wa)
            gather_b.wait()
            write_b = pltpu.async_copy(rows_b, out_hbm.at[pl.ds(off_b, SC_ROW_CHUNK)], sem_wb)
            write_a.wait()
            write_b.wait()

    return gather(table, idx)


def _combine_kernel(x1_ref, rt_ref, y_ref, fg_ref, o_ref, *, final_norm):
    rt = rt_ref[...]
    y1 = _unpack_bf16_pairs(y_ref[0, 0]).astype(F32)
    y2 = _unpack_bf16_pairs(y_ref[0, 1]).astype(F32)
    h = x1_ref[...] + rt[:, 2:3] * y1 + rt[:, 3:4] * y2
    if final_norm:
        h = h * lax.rsqrt(jnp.mean(h * h, axis=-1, keepdims=True) + EPS) * fg_ref[...]
    o_ref[...] = h


def _combine_call(x1, rt, ysg, final_g, *, tm, final_norm):
    T, D = x1.shape
    return pl.pallas_call(
        functools.partial(_combine_kernel, final_norm=final_norm),
        grid=(T // tm,),
        in_specs=[
            pl.BlockSpec((tm, D), lambda t: (t, 0)),
            pl.BlockSpec((tm, ROUTER_COLS), lambda t: (t, 0)),
            pl.BlockSpec((1, TOP_K, tm, D // 2), lambda t: (t, 0, 0, 0)),
            pl.BlockSpec((1, D), lambda t: (0, 0)),
        ],
        out_specs=pl.BlockSpec((tm, D), lambda t: (t, 0)),
        out_shape=jax.ShapeDtypeStruct((T, D), F32),
        compiler_params=pltpu.CompilerParams(
            dimension_semantics=("parallel",), vmem_limit_bytes=VMEM_LIMIT_BYTES),
        name="moe_combine",
    )(x1, rt, ysg, final_g)


def _slot_layout(tile_counts, n_assign):
    NB = -(-n_assign // EXPERT_BLOCK) + N_EXPERTS
    n_tiles = tile_counts.shape[0]
    tc = tile_counts.astype(F32)
    hp = lax.Precision.HIGHEST
    counts = jnp.sum(tc, axis=0)
    padded = jnp.ceil(counts / EXPERT_BLOCK) * EXPERT_BLOCK
    upper = (jnp.arange(N_EXPERTS)[:, None] < jnp.arange(N_EXPERTS)[None, :]).astype(F32)
    pad_start = jnp.dot(padded, upper, precision=hp)
    pad_end = pad_start + padded
    lower = (jnp.arange(n_tiles)[:, None] > jnp.arange(n_tiles)[None, :]).astype(F32)
    tile_base = pad_start[None, :] + jnp.dot(lower, tc, precision=hp)
    block_start = jnp.arange(NB, dtype=F32) * EXPERT_BLOCK
    block_expert = jnp.minimum(jnp.sum((pad_end[None, :] <= block_start[:, None]).astype(jnp.int32), axis=1),
                               N_EXPERTS - 1)
    mine = block_expert[:, None] == jnp.arange(N_EXPERTS)[None, :]
    run_end = jnp.sum(jnp.where(mine, (pad_start + counts)[None, :], 0.0), axis=1)
    n_valid = jnp.clip(run_end - block_start, 0, EXPERT_BLOCK).astype(jnp.int32)
    eid = jnp.arange(N_EXPERTS)
    later_nonempty = (eid[None, :] > eid[:, None]) & (counts[None, :] > 0)
    next_nonempty = jnp.min(jnp.where(later_nonempty, eid[None, :], N_EXPERTS), axis=1)
    next_nonempty = jnp.where(next_nonempty < N_EXPERTS, next_nonempty, -1)
    next_expert = jnp.sum(jnp.where(mine, next_nonempty[None, :], 0), axis=1).astype(jnp.int32)
    return NB, block_expert.astype(jnp.int32), n_valid, next_expert, tile_base


def _rope_tables(S):
    inv = 1.0 / (ROPE_THETA ** (jnp.arange(0, HEAD_DIM, 2, dtype=F32) / HEAD_DIM))
    ang_t = inv[:, None] * jnp.arange(S, dtype=F32)[None, :]
    return jnp.cos(ang_t), jnp.sin(ang_t)


def _tiles(S):
    tile = min(512, S)
    return tile, tile, min(1024, S), min(1024, S), min(1024, S)


def kernel(x, norm1_g, w_in, lambda_q1, lambda_k1, lambda_q2, lambda_k2, subln_g, sinks, w_out,
           norm2_g, w_router_group, b_router_group, w_router_expert, b_router_expert,
           w_gate, w_up, w_down, final_g):
    B, S, D = x.shape
    T = B * S
    depth = w_in.shape[0]
    tq, tk, tm_proj, tm_tok, tq_swa = _tiles(S)
    qscale = HEAD_DIM ** -0.5 * math.log2(math.e)
    cos_t, sin_t = _rope_tables(S)

    c0 = DIFF_QK_COLS
    c1 = 2 * DIFF_QK_COLS
    c2 = c1 + DIFF_V_COLS
    c3 = c2 + SWA_Q_COLS
    c4 = c3 + SWA_KV_COLS
    for l in range(depth):
        lambda_init = 0.8 - 0.6 * math.exp(-0.3 * l)
        w = w_in[l]
        w_nat = jnp.concatenate([w[:, c0:c1], w[:, c3:c4]], axis=1).astype(BF16)
        w_tr = jnp.concatenate([w[:, :c0] * qscale, w[:, c1:c2], w[:, c2:c3] * qscale, w[:, c4:]],
                               axis=1).astype(BF16).T
        dqt, dk, dvt, sqt, sk, svt = _proj_call(
            x, norm1_g[l][None, :], w_nat, w_tr, cos_t, sin_t, tm=tm_proj, tk=tk)

        lam_p = jnp.stack([lambda_q1[l], lambda_k1[l], lambda_q2[l], lambda_k2[l]]).astype(F32)
        o_diff = _diff_call(lam_p, dqt, dk, dvt, subln_g[l][None, :].astype(F32),
                            tq=tq, tk=tk, lambda_init=lambda_init)
        sink_row = jnp.repeat(sinks[l].astype(F32) * math.log2(math.e), WINDOW)[None, :]
        o_swa = _swa_call(sink_row, sqt, sk, svt, tq=tq_swa)

        wo_b = w_out[l].astype(BF16)
        w_router = jnp.zeros((D, ROUTER_COLS), F32)
        w_router = w_router.at[:, :N_GROUPS].set(w_router_group[l])
        w_router = w_router.at[:, N_GROUPS:N_GROUPS + N_EXPERTS].set(w_router_expert[l])
        w_router_hi = w_router.astype(BF16)
        w_router_lo = (w_router - w_router_hi.astype(F32)).astype(BF16)
        w_router = jnp.concatenate([w_router_hi, w_router_lo], axis=1)
        b_router = jnp.zeros((1, ROUTER_COLS), F32)
        b_router = b_router.at[0, :N_GROUPS].set(b_router_group[l])
        b_router = b_router.at[0, N_GROUPS:N_GROUPS + N_EXPERTS].set(b_router_expert[l])
        x1, n2p, rt, cnt = _mix_call(x, o_diff, o_swa, wo_b, norm2_g[l][None, :], w_router, b_router, tm=tm_tok)

        rt2 = rt.reshape(T, ROUTER_COLS)
        tile_counts = cnt[:, :, 0, :N_EXPERTS].reshape(T // tm_tok, N_EXPERTS).astype(jnp.int32)
        NB, block_expert, n_valid, next_expert, tile_base = _slot_layout(tile_counts, T * TOP_K)
        tile_base = jnp.broadcast_to(tile_base.astype(F32)[:, :, None], (T // tm_tok, N_EXPERTS, LANES))
        dest = _slot_call(rt2, tile_base, tm=tm_tok)
        scatter_idx = jnp.swapaxes(dest[:, :TOP_K, :], 0, 1).reshape(TOP_K, T // SC_ROW_CHUNK, SC_ROW_CHUNK)
        xs = _sc_scatter_rows(n2p.reshape(T, D // 2), scatter_idx, NB * EXPERT_BLOCK)
        ys = _expert_call(block_expert, n_valid, next_expert, xs, w_gate[l], w_up[l], w_down[l])
        ysg = _sc_gather_rows(ys, dest[:, :TOP_K, :].reshape(T * TOP_K))
        x = _combine_call(x1.reshape(T, D), rt2, ysg.reshape(T // tm_tok, TOP_K, tm_tok, D // 2),
                          final_g[None, :], tm=tm_tok, final_norm=(l == depth - 1)).reshape(B, S, D)
    return x
```
